```python
import jax, jax.numpy as jnp
from jax import lax
import numpy as np

D_MODEL = 1024
BATCH = 8
SEQ = 8192
DEPTH = 1

SB_HEADS = 8
SB_HEAD_DIM = D_MODEL // SB_HEADS
SB_WIDTH = SB_HEADS * SB_HEAD_DIM
SB_BLOCK = 128
HG_HEADS = 8
HG_KEY_DIM = 128
HG_VAL_DIM = D_MODEL // HG_HEADS
HG_KEY_WIDTH = HG_HEADS * HG_KEY_DIM
HG_VAL_WIDTH = HG_HEADS * HG_VAL_DIM
HG_CHUNK = 64
N_BRANCHES = 2
RMS_EPS = 1e-6
IN_WIDTHS = (SB_WIDTH, SB_WIDTH, SB_WIDTH, SB_WIDTH,
             HG_KEY_WIDTH, HG_KEY_WIDTH, HG_VAL_WIDTH, HG_VAL_WIDTH,
             N_BRANCHES * D_MODEL)
IN_WIDTH = sum(IN_WIDTHS)

kernel_name = "hybrid_stickbreak_hgrn2_gated"


def rmsnorm(x, g):
    xf = x.astype(jnp.float32)
    y = xf * lax.rsqrt(jnp.mean(xf * xf, axis=-1, keepdims=True) + RMS_EPS)
    return (y * g.astype(jnp.float32)).astype(x.dtype)


def sb_attention(q, k, v):
    b, h, s, d = q.shape
    n_blocks = s // SB_BLOCK
    scale = d ** -0.5
    kf = k.astype(jnp.float32)
    vf = v.astype(jnp.float32)
    key_pos = jnp.arange(s)

    def block(i):
        start = i * SB_BLOCK
        qb = lax.dynamic_slice_in_dim(q, start, SB_BLOCK, axis=2).astype(jnp.float32)
        z = jnp.einsum('bhqd,bhkd->bhqk', qb, kf) * scale
        q_pos = start + jnp.arange(SB_BLOCK)
        causal = key_pos[None, :] < q_pos[:, None]
        log_not = jnp.where(causal, jax.nn.log_sigmoid(-z), 0.0)
        survive = lax.cumsum(log_not, axis=3, reverse=True) - log_not
        log_w = jnp.where(causal, jax.nn.log_sigmoid(z) + survive, -jnp.inf)
        w = jnp.exp(log_w)
        return jnp.einsum('bhqk,bhkd->bhqd', w, vf)

    out = lax.map(block, jnp.arange(n_blocks))
    out = jnp.transpose(out, (1, 2, 0, 3, 4)).reshape(b, h, s, d)
    return out.astype(q.dtype)


def hgrn2_chunked(q, k, g, v):
    b, s, h, dk = q.shape
    dv = v.shape[-1]
    n = s // HG_CHUNK

    def to_chunks(t):
        return t.reshape(b, n, HG_CHUNK, h, t.shape[-1]).transpose(1, 0, 3, 2, 4)

    causal = jnp.tril(jnp.ones((HG_CHUNK, HG_CHUNK), dtype=bool))[:, :, None]

    def step(state, inp):
        qc, kc, gc, vc = inp
        cum = jnp.cumsum(gc, axis=2)
        rel = cum[:, :, :, None, :] - cum[:, :, None, :, :]
        decay = jnp.exp(jnp.where(causal, rel, -jnp.inf))
        scores = jnp.einsum('bhtk,bhtsk,bhsk->bhts', qc, decay, kc)
        o = (jnp.einsum('bhts,bhsv->bhtv', scores, vc)
             + jnp.einsum('bhtk,bhkv->bhtv', qc * jnp.exp(cum), state))
        last = cum[:, :, -1:, :]
        state = (jnp.exp(last[:, :, 0, :])[..., None] * state
                 + jnp.einsum('bhsk,bhsv->bhkv', kc * jnp.exp(last - cum), vc))
        return state, o

    state0 = jnp.zeros((b, h, dk, dv), jnp.float32)
    _, o = lax.scan(step, state0, (to_chunks(q), to_chunks(k), to_chunks(g), to_chunks(v)))
    return o.transpose(1, 0, 3, 2, 4).reshape(b, s, h, dv)


def hybrid_layer(x, norm_g, w_in, b_gate, lb, hg_norm_g, w_sb_proj, w_hg_proj, w_out):
    b, s, _ = x.shape
    h = rmsnorm(x, norm_g)
    proj = jnp.einsum('bsd,de->bse', h, w_in)
    split_points = tuple(int(p) for p in np.cumsum(IN_WIDTHS)[:-1])
    sb_q, sb_k, sb_v, sb_z, hg_q, hg_f, hg_i, hg_z, gate_logits = jnp.split(proj, split_points, axis=-1)

    def heads(t):
        return t.reshape(b, s, SB_HEADS, SB_HEAD_DIM).transpose(0, 2, 1, 3)
    sb_o = sb_attention(heads(sb_q), heads(sb_k), heads(sb_v))
    sb_o = sb_o.transpose(0, 2, 1, 3).reshape(b, s, SB_WIDTH)
    u_sb = jnp.einsum('bse,ed->bsd', sb_o * jax.nn.silu(sb_z), w_sb_proj)

    f_logit = hg_f.reshape(b, s, HG_HEADS, HG_KEY_DIM).astype(jnp.float32)
    f = lb + (1.0 - lb) * jax.nn.sigmoid(f_logit)
    g = jnp.log(f)
    kk = 1.0 - f
    qq = jax.nn.silu(hg_q.reshape(b, s, HG_HEADS, HG_KEY_DIM).astype(jnp.float32))
    vv = hg_i.reshape(b, s, HG_HEADS, HG_VAL_DIM).astype(jnp.float32)
    hg_o = hgrn2_chunked(qq, kk, g, vv)
    hg_o = rmsnorm(hg_o, hg_norm_g).reshape(b, s, HG_VAL_WIDTH).astype(x.dtype)
    u_hg = jnp.einsum('bse,ed->bsd', hg_o * jax.nn.silu(hg_z), w_hg_proj)

    gates = jax.nn.sigmoid((gate_logits + b_gate).astype(jnp.float32)).astype(x.dtype)
    gate_sb, gate_hg = jnp.split(gates, N_BRANCHES, axis=-1)
    y = gate_sb * u_sb + gate_hg * u_hg
    return x + jnp.einsum('bsd,de->bse', y, w_out)


def _fwd_setup_inputs(seed: int = 0) -> dict:
    key = jax.random.key(seed)
    ks = jax.random.split(key, 11)
    f32 = jnp.float32
    x = jax.random.normal(ks[0], (BATCH, SEQ, D_MODEL), f32)
    norm_g = 1.0 + 0.02 * jax.random.normal(ks[1], (DEPTH, D_MODEL), f32)
    w_in = jax.random.normal(ks[2], (DEPTH, D_MODEL, IN_WIDTH), f32) * D_MODEL ** -0.5
    b_gate = 0.1 * jax.random.normal(ks[3], (DEPTH, N_BRANCHES * D_MODEL), f32)
    lb_logits = 0.5 * jax.random.normal(ks[4], (DEPTH + 1, HG_HEADS, HG_KEY_DIM), f32)
    hg_norm_g = 1.0 + 0.02 * jax.random.normal(ks[5], (DEPTH, HG_HEADS, HG_VAL_DIM), f32)
    w_sb_proj = jax.random.normal(ks[6], (DEPTH, SB_WIDTH, D_MODEL), f32) * SB_WIDTH ** -0.5
    w_hg_proj = jax.random.normal(ks[7], (DEPTH, HG_VAL_WIDTH, D_MODEL), f32) * HG_VAL_WIDTH ** -0.5
    w_out = jax.random.normal(ks[8], (DEPTH, D_MODEL, D_MODEL), f32) * D_MODEL ** -0.5
    final_norm_g = 1.0 + 0.02 * jax.random.normal(ks[9], (D_MODEL,), f32)
    return {"x": x, "norm_g": norm_g, "w_in": w_in, "b_gate": b_gate, "lb_logits": lb_logits,
            "hg_norm_g": hg_norm_g, "w_sb_proj": w_sb_proj, "w_hg_proj": w_hg_proj,
            "w_out": w_out, "final_norm_g": final_norm_g}


def _fwd_reference(x, norm_g, w_in, b_gate, lb_logits, hg_norm_g, w_sb_proj, w_hg_proj, w_out, final_norm_g):
    lb_all = jnp.cumsum(jax.nn.softmax(lb_logits.astype(jnp.float32), axis=0), axis=0)
    for l in range(DEPTH):
        x = hybrid_layer(x, norm_g[l], w_in[l], b_gate[l], lb_all[l], hg_norm_g[l],
                         w_sb_proj[l], w_hg_proj[l], w_out[l])
    return rmsnorm(x, final_norm_g)


import jax as _jax
import jax.numpy as _jnp

TWIN_FORMAT = 'train_step'
FWD_PARAMS = ['x', 'norm_g', 'w_in', 'b_gate', 'lb_logits', 'hg_norm_g', 'w_sb_proj', 'w_hg_proj', 'w_out', 'final_norm_g']
TWIN_WEIGHTS = ['norm_g', 'w_in', 'b_gate', 'lb_logits', 'hg_norm_g', 'w_sb_proj', 'w_hg_proj', 'w_out', 'final_norm_g']
TWIN_DIFF_INPUT = 'x'
TWIN_INPUTS = ['x', 'norm_g', 'w_in', 'b_gate', 'lb_logits', 'hg_norm_g', 'w_sb_proj', 'w_hg_proj', 'w_out', 'final_norm_g', 'loss_target', 'm_norm_g', 'm_w_in', 'm_b_gate', 'm_lb_logits', 'm_hg_norm_g', 'm_w_sb_proj', 'm_w_hg_proj', 'm_w_out', 'm_final_norm_g', 'v_norm_g', 'v_w_in', 'v_b_gate', 'v_lb_logits', 'v_hg_norm_g', 'v_w_sb_proj', 'v_w_hg_proj', 'v_w_out', 'v_final_norm_g']
TWIN_OUTPUTS = ['loss', 'grad_x', 'grad_norm_g', 'grad_w_in', 'grad_b_gate', 'grad_lb_logits', 'grad_hg_norm_g', 'grad_w_sb_proj', 'grad_w_hg_proj', 'grad_w_out', 'grad_final_norm_g', 'delta_norm_g', 'delta_w_in', 'delta_b_gate', 'delta_lb_logits', 'delta_hg_norm_g', 'delta_w_sb_proj', 'delta_w_hg_proj', 'delta_w_out', 'delta_final_norm_g', 'new_m_norm_g', 'new_m_w_in', 'new_m_b_gate', 'new_m_lb_logits', 'new_m_hg_norm_g', 'new_m_w_sb_proj', 'new_m_w_hg_proj', 'new_m_w_out', 'new_m_final_norm_g', 'new_v_norm_g', 'new_v_w_in', 'new_v_b_gate', 'new_v_lb_logits', 'new_v_hg_norm_g', 'new_v_w_sb_proj', 'new_v_w_hg_proj', 'new_v_w_out', 'new_v_final_norm_g']
TWIN_LEAF_KINDS = {'loss': 'loss', 'grad_x': 'grad_x', 'grad_norm_g': 'grad_w', 'grad_w_in': 'grad_w', 'grad_b_gate': 'grad_w', 'grad_lb_logits': 'grad_w', 'grad_hg_norm_g': 'grad_w', 'grad_w_sb_proj': 'grad_w', 'grad_w_hg_proj': 'grad_w', 'grad_w_out': 'grad_w', 'grad_final_norm_g': 'grad_w', 'delta_norm_g': 'delta_w', 'delta_w_in': 'delta_w', 'delta_b_gate': 'delta_w', 'delta_lb_logits': 'delta_w', 'delta_hg_norm_g': 'delta_w', 'delta_w_sb_proj': 'delta_w', 'delta_w_hg_proj': 'delta_w', 'delta_w_out': 'delta_w', 'delta_final_norm_g': 'delta_w', 'new_m_norm_g': 'new_m', 'new_m_w_in': 'new_m', 'new_m_b_gate': 'new_m', 'new_m_lb_logits': 'new_m', 'new_m_hg_norm_g': 'new_m', 'new_m_w_sb_proj': 'new_m', 'new_m_w_hg_proj': 'new_m', 'new_m_w_out': 'new_m', 'new_m_final_norm_g': 'new_m', 'new_v_norm_g': 'new_v', 'new_v_w_in': 'new_v', 'new_v_b_gate': 'new_v', 'new_v_lb_logits': 'new_v', 'new_v_hg_norm_g': 'new_v', 'new_v_w_sb_proj': 'new_v', 'new_v_w_hg_proj': 'new_v', 'new_v_w_out': 'new_v', 'new_v_final_norm_g': 'new_v'}


def _forward(args):
    return _fwd_reference(*[args[k] for k in FWD_PARAMS])


def _output_shape():
    def fwd():
        inp = _fwd_setup_inputs(0)
        return _fwd_reference(*[inp[k] for k in FWD_PARAMS])
    out = _jax.eval_shape(fwd)
    return out.shape, out.dtype

N_MICROBATCH = 1
ADAM_LR = 0.001
ADAM_B1 = 0.9
ADAM_B2 = 0.999
ADAM_EPS = 1e-08
ADAM_WD = 0.01
ADAM_STEP = 10
PER_EXAMPLE_BATCH_AXIS = {'x': 0, 'loss_target': 0}
SHARED_INPUTS = []
_WEIGHT_DTYPES = {'norm_g': _jnp.float32, 'w_in': _jnp.float32, 'b_gate': _jnp.float32, 'lb_logits': _jnp.float32, 'hg_norm_g': _jnp.float32, 'w_sb_proj': _jnp.float32, 'w_hg_proj': _jnp.float32, 'w_out': _jnp.float32, 'final_norm_g': _jnp.float32}
MOMENT_SCALE = {'norm_g': 1.388792e-01, 'w_in': 4.388620e-02, 'b_gate': 2.489943e-02, 'lb_logits': 6.535486e-03, 'hg_norm_g': 7.807222e-02, 'w_sb_proj': 4.869118e-02, 'w_hg_proj': 7.552127e-02, 'w_out': 9.000749e-02, 'final_norm_g': 6.398307e+01}


def _to_microbatches(a, axis):
    t = _jnp.moveaxis(a, axis, 0)
    t = t.reshape((N_MICROBATCH, t.shape[0] // N_MICROBATCH) + t.shape[1:])
    return _jnp.moveaxis(t, 1, axis + 1)


def setup_inputs(seed: int = 0) -> dict:
    inp = _fwd_setup_inputs(seed)
    key = _jax.random.fold_in(_jax.random.key(seed), 7919)
    shape, _ = _output_shape()
    out = dict(inp)
    out["loss_target"] = _jax.random.normal(_jax.random.fold_in(key, 0), shape, _jnp.float32)
    for i, name in enumerate(TWIN_WEIGHTS):
        w = inp[name].astype(_jnp.float32)
        if MOMENT_SCALE is None:
            s = _jnp.sqrt(_jnp.mean(_jnp.square(w)) + 1e-30)
        else:
            s = MOMENT_SCALE[name]
        km, kv = _jax.random.split(_jax.random.fold_in(key, i + 1))
        out[name] = w
        out["m_" + name] = s * _jax.random.normal(km, w.shape, _jnp.float32)
        out["v_" + name] = (s * s) * _jax.random.uniform(kv, w.shape, _jnp.float32, 0.5, 1.5)
    if N_MICROBATCH > 1:
        for name, axis in PER_EXAMPLE_BATCH_AXIS.items():
            out[name] = _to_microbatches(out[name], axis)
    return {'x': out['x'], 'norm_g': out['norm_g'], 'w_in': out['w_in'], 'b_gate': out['b_gate'], 'lb_logits': out['lb_logits'], 'hg_norm_g': out['hg_norm_g'], 'w_sb_proj': out['w_sb_proj'], 'w_hg_proj': out['w_hg_proj'], 'w_out': out['w_out'], 'final_norm_g': out['final_norm_g'], 'loss_target': out['loss_target'], 'm_norm_g': out['m_norm_g'], 'm_w_in': out['m_w_in'], 'm_b_gate': out['m_b_gate'], 'm_lb_logits': out['m_lb_logits'], 'm_hg_norm_g': out['m_hg_norm_g'], 'm_w_sb_proj': out['m_w_sb_proj'], 'm_w_hg_proj': out['m_w_hg_proj'], 'm_w_out': out['m_w_out'], 'm_final_norm_g': out['m_final_norm_g'], 'v_norm_g': out['v_norm_g'], 'v_w_in': out['v_w_in'], 'v_b_gate': out['v_b_gate'], 'v_lb_logits': out['v_lb_logits'], 'v_hg_norm_g': out['v_hg_norm_g'], 'v_w_sb_proj': out['v_w_sb_proj'], 'v_w_hg_proj': out['v_w_hg_proj'], 'v_w_out': out['v_w_out'], 'v_final_norm_g': out['v_final_norm_g']}


def _loss(weights, diff, rest, loss_target):
    with _jax.named_scope("forward"):
        args = {**rest, TWIN_DIFF_INPUT: diff, **{k: w.astype(_WEIGHT_DTYPES[k]) for k, w in weights.items()}}
        y = _forward(args)
    with _jax.named_scope("loss_head"):
        err = _jnp.square(y.astype(_jnp.float32) - loss_target)
        return 0.5 * _jnp.sum(_jnp.mean(err, axis=-1)) if err.ndim else 0.5 * err


def _adamw(w, g, m, v):
    m = ADAM_B1 * m + (1.0 - ADAM_B1) * g
    v = ADAM_B2 * v + (1.0 - ADAM_B2) * _jnp.square(g)
    m_hat = m / (1.0 - ADAM_B1 ** ADAM_STEP)
    v_hat = v / (1.0 - ADAM_B2 ** ADAM_STEP)
    delta = -ADAM_LR * (m_hat / (_jnp.sqrt(v_hat) + ADAM_EPS) + ADAM_WD * w)
    return delta, m, v


def reference(x, norm_g, w_in, b_gate, lb_logits, hg_norm_g, w_sb_proj, w_hg_proj, w_out, final_norm_g, loss_target, m_norm_g, m_w_in, m_b_gate, m_lb_logits, m_hg_norm_g, m_w_sb_proj, m_w_hg_proj, m_w_out, m_final_norm_g, v_norm_g, v_w_in, v_b_gate, v_lb_logits, v_hg_norm_g, v_w_sb_proj, v_w_hg_proj, v_w_out, v_final_norm_g):
    given = dict(x=x, norm_g=norm_g, w_in=w_in, b_gate=b_gate, lb_logits=lb_logits, hg_norm_g=hg_norm_g, w_sb_proj=w_sb_proj, w_hg_proj=w_hg_proj, w_out=w_out, final_norm_g=final_norm_g, loss_target=loss_target, m_norm_g=m_norm_g, m_w_in=m_w_in, m_b_gate=m_b_gate, m_lb_logits=m_lb_logits, m_hg_norm_g=m_hg_norm_g, m_w_sb_proj=m_w_sb_proj, m_w_hg_proj=m_w_hg_proj, m_w_out=m_w_out, m_final_norm_g=m_final_norm_g, v_norm_g=v_norm_g, v_w_in=v_w_in, v_b_gate=v_b_gate, v_lb_logits=v_lb_logits, v_hg_norm_g=v_hg_norm_g, v_w_sb_proj=v_w_sb_proj, v_w_hg_proj=v_w_hg_proj, v_w_out=v_w_out, v_final_norm_g=v_final_norm_g)
    weights = {n: given[n] for n in TWIN_WEIGHTS}
    shared = {n: given[n] for n in SHARED_INPUTS}
    per_example = {n: given[n] for n in ['x']}
    grad_fn = _jax.value_and_grad(_loss, argnums=(0, 1))

    def one_microbatch(ex, loss_target):
        ex = dict(ex)
        diff = ex.pop(TWIN_DIFF_INPUT)
        return grad_fn(weights, diff, {**shared, **ex}, loss_target)

    if N_MICROBATCH == 1:
        loss, (grad_w, grad_x) = one_microbatch(per_example, given["loss_target"])
    else:
        def body(carry, xs):
            loss_sum, grad_sum = carry
            l_k, (gw_k, gx_k) = one_microbatch(xs[0], xs[1])
            with _jax.named_scope("update"):
                return (loss_sum + l_k, _jax.tree.map(_jnp.add, grad_sum, gw_k)), gx_k

        init = (_jnp.zeros((), _jnp.float32), _jax.tree.map(_jnp.zeros_like, weights))
        (loss, grad_w), grad_x = _jax.lax.scan(body, init, (per_example, given["loss_target"]))
    with _jax.named_scope("update"):
        delta_w, new_m, new_v = {}, {}, {}
        for n in TWIN_WEIGHTS:
            delta_w[n], new_m[n], new_v[n] = _adamw(weights[n], grad_w[n], given["m_" + n], given["v_" + n])
    return (loss, grad_x, *[grad_w[n] for n in TWIN_WEIGHTS], *[delta_w[n] for n in TWIN_WEIGHTS],
            *[new_m[n] for n in TWIN_WEIGHTS], *[new_v[n] for n in TWIN_WEIGHTS])
```

```python
import functools

import jax
import jax.numpy as jnp
from jax import lax
from jax.experimental import pallas as pl
from jax.experimental.pallas import tpu as pltpu

F32 = jnp.float32
BF16 = jnp.bfloat16

D_MODEL = 1024
HEADS = 8
HEAD_DIM = 128
IN_WIDTH = 10240
N_CHIPS = 4
W_IN_SHARD = IN_WIDTH // N_CHIPS
ROW_SHARD = D_MODEL // N_CHIPS
RMS_EPS = 1e-6

OFF_SB_Q, OFF_SB_K, OFF_SB_V, OFF_SB_Z = 0, 1024, 2048, 3072
OFF_HG_Q, OFF_HG_F, OFF_HG_I, OFF_HG_Z, OFF_GATE = 4096, 5120, 6144, 7168, 8192

SB_BLOCK = 256
HG_CHUNK = 32
HG_STEP = 256
HG_MID = HG_CHUNK // 2 - 1

ADAM_LR, ADAM_B1, ADAM_B2, ADAM_EPS, ADAM_WD, ADAM_STEP = 0.001, 0.9, 0.999, 1e-08, 0.01, 10

VMEM_LIMIT = 56 * 1024 * 1024

MESH = pl.DeviceIdType.MESH


def _cparams(sem, vmem=VMEM_LIMIT):
    return pltpu.CompilerParams(dimension_semantics=sem, vmem_limit_bytes=vmem)


def _dot(a, b):
    return jnp.dot(a, b, preferred_element_type=F32)


def _dot_nt(a, b):
    return lax.dot_general(a, b, (((1,), (1,)), ((), ())), preferred_element_type=F32)


def _dot_tn(a, b):
    return lax.dot_general(a, b, (((0,), (0,)), ((), ())), preferred_element_type=F32)


def _split_dot(x, tri):
    hi = x.astype(BF16)
    lo = (x - hi.astype(F32)).astype(BF16)
    return _dot(hi, tri) + _dot(lo, tri)


def _split_dot_left(tri, x):
    hi = x.astype(BF16)
    lo = (x - hi.astype(F32)).astype(BF16)
    return _dot(tri, hi) + _dot(tri, lo)


def _sigmoid(x):
    return 1.0 / (1.0 + jnp.exp(-x))


def _inproj(x, norm_g, w4):
    s_len = x.shape[0]
    ts = min(1024, s_len)
    tn = 1280
    per = W_IN_SHARD // tn

    def body(x_ref, g_ref, w_ref, proj_ref, h_ref):
        @pl.when(pl.program_id(1) == 0)
        def _():
            xv = x_ref[...]
            r = lax.rsqrt(jnp.mean(xv * xv, axis=-1, keepdims=True) + RMS_EPS)
            h_ref[...] = ((xv * r) * g_ref[...]).astype(BF16)

        proj_ref[...] = _dot(h_ref[...], w_ref[0])

    return pl.pallas_call(
        body,
        name="inproj",
        grid=(s_len // ts, IN_WIDTH // tn),
        in_specs=[
            pl.BlockSpec((ts, D_MODEL), lambda s, n: (s, 0)),
            pl.BlockSpec((1, D_MODEL), lambda s, n: (0, 0)),
            pl.BlockSpec((1, D_MODEL, tn), lambda s, n: (n // per, 0, n % per)),
        ],
        out_specs=[
            pl.BlockSpec((ts, tn), lambda s, n: (s, n)),
            pl.BlockSpec((ts, D_MODEL), lambda s, n: (s, 0)),
        ],
        out_shape=[
            jax.ShapeDtypeStruct((s_len, IN_WIDTH), F32),
            jax.ShapeDtypeStruct((s_len, D_MODEL), BF16),
        ],
        compiler_params=_cparams(("arbitrary", "arbitrary")),
    )(x, norm_g, w4)


def _sb_tile_fwd(qb, kb, row_gt_col, tri_excl, carry, diag):
    scale = HEAD_DIM ** -0.5
    z = _dot_nt(qb, kb) * scale
    ls_pos = jnp.minimum(z, 0.0) - jnp.log1p(jnp.exp(-jnp.abs(z)))
    log_not = ls_pos - z
    log_not_m = jnp.where(row_gt_col, log_not, 0.0) if diag else log_not
    surv = _split_dot(log_not_m, tri_excl) + carry
    w = jnp.exp(ls_pos + surv)
    if diag:
        w = jnp.where(row_gt_col, w, 0.0)
    return ls_pos, log_not, log_not_m, surv, w


def _sb_fwd(proj):
    s_len = proj.shape[0]
    blk = min(SB_BLOCK, s_len)
    nq = s_len // blk

    def body(q_ref, k_ref, v_ref, o_ref, of_ref):
        i = pl.program_id(1)
        qb = q_ref[...].astype(BF16)
        row = lax.broadcasted_iota(jnp.int32, (blk, blk), 0)
        col = lax.broadcasted_iota(jnp.int32, (blk, blk), 1)
        row_gt_col = row > col
        tri_excl = row_gt_col.astype(BF16)

        def tile(j, carry, acc, acc_lo, diag):
            start = pl.multiple_of(j * blk, blk)
            kb = k_ref[pl.ds(start, blk), :].astype(BF16)
            vb = v_ref[pl.ds(start, blk), :].astype(BF16)
            _, _, log_not_m, surv, w = _sb_tile_fwd(qb, kb, row_gt_col, tri_excl, carry, diag)
            wb = w.astype(BF16)
            w_lo = (w - wb.astype(F32)).astype(BF16)
            acc = acc + _dot(wb, vb)
            acc_lo = acc_lo + _dot(w_lo, vb)
            carry = surv[:, 0:1] + log_not_m[:, 0:1]
            return carry, acc, acc_lo

        zero = jnp.zeros((blk, HEAD_DIM), F32)
        carry, acc, acc_lo = tile(i, jnp.zeros((blk, 1), F32), zero, zero, True)

        def step(n, st):
            return tile(i - 1 - n, *st, False)

        carry, acc, acc_lo = lax.fori_loop(0, i, step, (carry, acc, acc_lo))
        o_ref[...] = acc
        of_ref[...] = acc + acc_lo

    def col_spec(off, rows):
        if rows == blk:
            return pl.BlockSpec((blk, HEAD_DIM), lambda h, i: (i, off // HEAD_DIM + h))
        return pl.BlockSpec((s_len, HEAD_DIM), lambda h, i: (0, off // HEAD_DIM + h))

    out_spec = pl.BlockSpec((blk, HEAD_DIM), lambda h, i: (i, h))
    return pl.pallas_call(
        body,
        name="sb_fwd",
        grid=(HEADS, nq),
        in_specs=[col_spec(OFF_SB_Q, blk), col_spec(OFF_SB_K, s_len), col_spec(OFF_SB_V, s_len)],
        out_specs=[out_spec, out_spec],
        out_shape=[jax.ShapeDtypeStruct((s_len, D_MODEL), F32)] * 2,
        compiler_params=_cparams(("arbitrary", "arbitrary")),
    )(proj, proj, proj)


def _sb_bwd(proj, o_fine, d_o):
    s_len = proj.shape[0]
    blk = min(SB_BLOCK, s_len)
    nq = s_len // blk
    scale = HEAD_DIM ** -0.5

    def body(q_ref, k_ref, v_ref, of_ref, do_ref, dq_ref, dk_ref, dv_ref, dk_acc, dv_acc):
        i = pl.program_id(1)

        @pl.when(i == 0)
        def _():
            dk_acc[...] = jnp.zeros_like(dk_acc)
            dv_acc[...] = jnp.zeros_like(dv_acc)

        qb = q_ref[...].astype(BF16)
        dob = do_ref[...].astype(BF16)
        total = jnp.sum(dob.astype(F32) * of_ref[...], axis=-1, keepdims=True)
        row = lax.broadcasted_iota(jnp.int32, (blk, blk), 0)
        col = lax.broadcasted_iota(jnp.int32, (blk, blk), 1)
        row_gt_col = row > col
        tri_excl = row_gt_col.astype(BF16)
        tri_incl = (row >= col).astype(BF16)

        def tile(j, c_not, c_dlw, dq, diag):
            start = pl.multiple_of(j * blk, blk)
            kb = k_ref[pl.ds(start, blk), :].astype(BF16)
            vb = v_ref[pl.ds(start, blk), :].astype(BF16)
            ls_pos, log_not, log_not_m, surv, w = _sb_tile_fwd(qb, kb, row_gt_col, tri_excl, c_not, diag)
            dlw = _dot_nt(dob, vb) * w
            suffix = _split_dot(dlw, tri_incl)
            d_not = total - c_dlw - suffix
            dz = (dlw * jnp.exp(log_not) - d_not * jnp.exp(ls_pos)) * scale
            if diag:
                dz = jnp.where(row_gt_col, dz, 0.0)
            dzb = dz.astype(BF16)
            dq = dq + _dot(dzb, kb)
            dk_acc[pl.ds(start, blk), :] += _dot_tn(dzb, qb)
            dv_acc[pl.ds(start, blk), :] += _dot_tn(w.astype(BF16), dob)
            c_not = surv[:, 0:1] + log_not_m[:, 0:1]
            c_dlw = c_dlw + suffix[:, 0:1]
            return c_not, c_dlw, dq

        zcol = jnp.zeros((blk, 1), F32)
        st = tile(i, zcol, zcol, jnp.zeros((blk, HEAD_DIM), F32), True)

        def step(n, st):
            return tile(i - 1 - n, *st, False)

        _, _, dq = lax.fori_loop(0, i, step, st)
        dq_ref[...] = dq.astype(BF16)

        @pl.when(i == nq - 1)
        def _():
            dk_ref[...] = dk_acc[...].astype(BF16)
            dv_ref[...] = dv_acc[...].astype(BF16)

    def blk_spec(off):
        return pl.BlockSpec((blk, HEAD_DIM), lambda h, i: (i, off // HEAD_DIM + h))

    def head_spec(off):
        return pl.BlockSpec((s_len, HEAD_DIM), lambda h, i: (0, off // HEAD_DIM + h))

    return pl.pallas_call(
        body,
        name="sb_bwd",
        grid=(HEADS, nq),
        in_specs=[blk_spec(OFF_SB_Q), head_spec(OFF_SB_K), head_spec(OFF_SB_V), blk_spec(0), blk_spec(0)],
        out_specs=[blk_spec(0), head_spec(0), head_spec(0)],
        out_shape=[jax.ShapeDtypeStruct((s_len, D_MODEL), BF16)] * 3,
        scratch_shapes=[pltpu.VMEM((s_len, HEAD_DIM), F32), pltpu.VMEM((s_len, HEAD_DIM), F32)],
        compiler_params=_cparams(("arbitrary", "arbitrary")),
    )(proj, proj, proj, o_fine, d_o)


def _hg_lower_bound(lbl_ref):
    l0 = lbl_ref[0:1, :]
    l1 = lbl_ref[1:2, :]
    mx = jnp.maximum(l0, l1)
    e0 = jnp.exp(l0 - mx)
    e1 = jnp.exp(l1 - mx)
    return e0 / (e0 + e1)


def _hg_gates(hq, hf, lb):
    sig_f = _sigmoid(hf)
    f = lb + (1.0 - lb) * sig_f
    g = jnp.log(f)
    kk = 1.0 - f
    sig_q = _sigmoid(hq)
    qq = hq * sig_q
    return qq, kk, g, f, sig_f, sig_q


def _chunk_bcast(x, r, rows):
    w = x.shape[-1]
    x3 = x.reshape(rows // HG_CHUNK, HG_CHUNK, w)
    return jnp.broadcast_to(x3[:, r : r + 1, :], x3.shape).reshape(rows, w)


def _hg_decays(qq, kk, g, tri_blk, rows):
    cum = _split_dot_left(tri_blk, g)
    mid = _chunk_bcast(cum, HG_MID, rows)
    last = _chunk_bcast(cum, HG_CHUNK - 1, rows)
    e_qm = jnp.exp(cum - mid)
    e_km = jnp.exp(mid - cum)
    e_q = jnp.exp(cum)
    e_kl = jnp.exp(last - cum)
    return cum, last, e_qm, e_km, e_q, e_kl


def _blockdiag(rows, kind):
    row = lax.broadcasted_iota(jnp.int32, (rows, rows), 0)
    col = lax.broadcasted_iota(jnp.int32, (rows, rows), 1)
    keep = (row // HG_CHUNK) == (col // HG_CHUNK)
    if kind == "lower":
        keep = keep & (row >= col)
    elif kind == "upper":
        keep = keep & (row <= col)
    return jnp.where(keep, 1.0, 0.0).astype(BF16)


def _hg_fwd(proj, lbl):
    s_len = proj.shape[0]
    rows = min(HG_STEP, s_len)
    n_chunks = rows // HG_CHUNK

    def body(hq_ref, hf_ref, hi_ref, lbl_ref, o_ref, st_ref, state, q_mid, k_mid, q_dec, k_last, v_b):
        @pl.when(pl.program_id(0) == 0)
        def _():
            state[...] = jnp.zeros_like(state)

        lb = _hg_lower_bound(lbl_ref)
        qq, kk, g, _, _, _ = _hg_gates(hq_ref[...], hf_ref[...], lb)
        tri_blk = _blockdiag(rows, "lower")
        _, last, e_qm, e_km, e_q, e_kl = _hg_decays(qq, kk, g, tri_blk, rows)
        q_mid[...] = (qq * e_qm).astype(BF16)
        k_mid[...] = (kk * e_km).astype(BF16)
        q_dec[...] = (qq * e_q).astype(BF16)
        k_last[...] = (kk * e_kl).astype(BF16)
        v_b[...] = hi_ref[...].astype(BF16)
        e_last = jnp.exp(last)
        row = lax.broadcasted_iota(jnp.int32, (HG_CHUNK, HG_CHUNK), 0)
        col = lax.broadcasted_iota(jnp.int32, (HG_CHUNK, HG_CHUNK), 1)
        causal = row >= col

        for c in range(n_chunks):
            r0 = c * HG_CHUNK
            for h in range(HEADS):
                c0 = h * HEAD_DIM
                sl = (slice(r0, r0 + HG_CHUNK), slice(c0, c0 + HEAD_DIM))
                st = state[h]
                st_ref[c, h] = st
                a = jnp.where(causal, _dot_nt(q_mid[sl], k_mid[sl]), 0.0)
                vb = v_b[sl]
                o_ref[sl] = _dot(a.astype(BF16), vb) + _dot_nt(q_dec[sl], st.astype(BF16))
                decay = e_last[r0 : r0 + 1, c0 : c0 + HEAD_DIM]
                state[h] = st * decay + _dot_tn(vb, k_last[sl])

    def col_spec(off):
        return pl.BlockSpec((rows, D_MODEL), lambda s: (s, off // D_MODEL))

    scratch = [pltpu.VMEM((HEADS, HEAD_DIM, HEAD_DIM), F32)] + [pltpu.VMEM((rows, D_MODEL), BF16)] * 5
    return pl.pallas_call(
        body,
        name="hg_fwd",
        grid=(s_len // rows,),
        in_specs=[col_spec(OFF_HG_Q), col_spec(OFF_HG_F), col_spec(OFF_HG_I), pl.BlockSpec((2, D_MODEL), lambda s: (0, 0))],
        out_specs=[
            pl.BlockSpec((rows, D_MODEL), lambda s: (s, 0)),
            pl.BlockSpec((n_chunks, HEADS, HEAD_DIM, HEAD_DIM), lambda s: (s, 0, 0, 0)),
        ],
        out_shape=[
            jax.ShapeDtypeStruct((s_len, D_MODEL), F32),
            jax.ShapeDtypeStruct((s_len // HG_CHUNK, HEADS, HEAD_DIM, HEAD_DIM), F32),
        ],
        scratch_shapes=scratch,
        compiler_params=_cparams(("arbitrary",)),
    )(proj, proj, proj, lbl)


def _hg_bwd(proj, lbl, states, d_o):
    s_len = proj.shape[0]
    rows = min(HG_STEP, s_len)
    n_chunks = rows // HG_CHUNK
    n_steps = s_len // rows

    def body(hq_ref, hf_ref, hi_ref, lbl_ref, st_ref, do_ref, dp_ref, dlb_ref,
             dstate, q_mid, k_mid, q_dec, k_last, v_b, do_b, d_qm, d_km, d_qd, d_kl, d_v, d_last):
        @pl.when(pl.program_id(0) == 0)
        def _():
            dstate[...] = jnp.zeros_like(dstate)
            dlb_ref[...] = jnp.zeros_like(dlb_ref)

        lb = _hg_lower_bound(lbl_ref)
        hq = hq_ref[...]
        qq, kk, g, f, sig_f, sig_q = _hg_gates(hq, hf_ref[...], lb)
        tri_blk = _blockdiag(rows, "lower")
        _, last, e_qm, e_km, e_q, e_kl = _hg_decays(qq, kk, g, tri_blk, rows)
        qm, km, qd, kl = qq * e_qm, kk * e_km, qq * e_q, kk * e_kl
        q_mid[...] = qm.astype(BF16)
        k_mid[...] = km.astype(BF16)
        q_dec[...] = qd.astype(BF16)
        k_last[...] = kl.astype(BF16)
        v_b[...] = hi_ref[...].astype(BF16)
        do_b[...] = do_ref[...].astype(BF16)
        e_last = jnp.exp(last)
        row = lax.broadcasted_iota(jnp.int32, (HG_CHUNK, HG_CHUNK), 0)
        col = lax.broadcasted_iota(jnp.int32, (HG_CHUNK, HG_CHUNK), 1)
        causal = row >= col

        for c in reversed(range(n_chunks)):
            r0 = c * HG_CHUNK
            for h in range(HEADS):
                c0 = h * HEAD_DIM
                sl = (slice(r0, r0 + HG_CHUNK), slice(c0, c0 + HEAD_DIM))
                st0 = st_ref[c, h]
                ds1 = dstate[h]
                ds1b = ds1.astype(BF16)
                dob, vb, qmb, kmb = do_b[sl], v_b[sl], q_mid[sl], k_mid[sl]
                a = jnp.where(causal, _dot_nt(qmb, kmb), 0.0).astype(BF16)
                da = jnp.where(causal, _dot_nt(dob, vb), 0.0).astype(BF16)
                d_v[sl] = _dot_tn(a, dob) + _dot_nt(k_last[sl], ds1b)
                d_qm[sl] = _dot(da, kmb)
                d_km[sl] = _dot_tn(da, qmb)
                d_qd[sl] = _dot(dob, st0.astype(BF16))
                d_kl[sl] = _dot(vb, ds1b)
                decay = e_last[r0 : r0 + 1, c0 : c0 + HEAD_DIM]
                d_last[c : c + 1, c0 : c0 + HEAD_DIM] = decay * jnp.sum(ds1 * st0, axis=0, keepdims=True)
                dstate[h] = ds1 * decay + _dot_tn(dob, q_dec[sl])

        dqm, dkm, dqd, dkl = d_qm[...], d_km[...], d_qd[...], d_kl[...]
        dq = dqm * e_qm + dqd * e_q
        dk = dkm * e_km + dkl * e_kl
        t_kl = dkl * kl
        dcum = dqm * qm - dkm * km + dqd * qd - t_kl
        dl = d_last[...]
        dl_b = jnp.broadcast_to(dl[:, None, :], (n_chunks, HG_CHUNK, D_MODEL)).reshape(rows, D_MODEL)
        dg = _split_dot_left(_blockdiag(rows, "upper"), dcum) + _split_dot_left(_blockdiag(rows, "all"), t_kl) + dl_b
        df = dg / f - dk
        one_m = 1.0 - sig_f
        dp_ref[:, 0:D_MODEL] = (dq * (sig_q * (1.0 + hq * (1.0 - sig_q)))).astype(BF16)
        dp_ref[:, D_MODEL : 2 * D_MODEL] = (df * (1.0 - lb) * sig_f * one_m).astype(BF16)
        dp_ref[:, 2 * D_MODEL : 3 * D_MODEL] = d_v[...].astype(BF16)
        dlb_ref[...] += jnp.sum(df * one_m, axis=0, keepdims=True)

    def col_spec(off):
        return pl.BlockSpec((rows, D_MODEL), lambda s: (n_steps - 1 - s, off // D_MODEL))

    f32_tile = pltpu.VMEM((rows, D_MODEL), F32)
    bf_tile = pltpu.VMEM((rows, D_MODEL), BF16)
    scratch = [pltpu.VMEM((HEADS, HEAD_DIM, HEAD_DIM), F32)] + [bf_tile] * 6 + [f32_tile] * 5
    scratch += [pltpu.VMEM((n_chunks, D_MODEL), F32)]
    return pl.pallas_call(
        body,
        name="hg_bwd",
        grid=(n_steps,),
        in_specs=[
            col_spec(OFF_HG_Q), col_spec(OFF_HG_F), col_spec(OFF_HG_I),
            pl.BlockSpec((2, D_MODEL), lambda s: (0, 0)),
            pl.BlockSpec((n_chunks, HEADS, HEAD_DIM, HEAD_DIM), lambda s: (n_steps - 1 - s, 0, 0, 0)),
            pl.BlockSpec((rows, D_MODEL), lambda s: (n_steps - 1 - s, 0)),
        ],
        out_specs=[
            pl.BlockSpec((rows, 3 * D_MODEL), lambda s: (n_steps - 1 - s, 0)),
            pl.BlockSpec((1, D_MODEL), lambda s: (0, 0)),
        ],
        out_shape=[
            jax.ShapeDtypeStruct((s_len, 3 * D_MODEL), BF16),
            jax.ShapeDtypeStruct((1, D_MODEL), F32),
        ],
        scratch_shapes=scratch,
        compiler_params=_cparams(("arbitrary",)),
    )(proj, proj, proj, lbl, states, d_o)


def _mid(proj, sb_o, hg_o, x, target, b_gate, hg_gain, final_g, w_sb, w_hg, w_out):
    s_len = proj.shape[0]
    ts = min(128, s_len)
    inv_d = 1.0 / D_MODEL

    def body(zsb_ref, hz_ref, gl_ref, sbo_ref, hgo_ref, x_ref, tgt_ref, bg_ref, hgn_ref, fg_ref,
             wsb_ref, whg_ref, wout_ref,
             dout_ref, dsbo_ref, dhgo_ref, dzsb_ref, dhz_ref, dgl_ref,
             asb_ref, dusb_ref, ahg_ref, duhg_ref, y_ref, doutb_ref,
             loss_ref, dfg_ref, dbg_ref, dhgn_ref):
        @pl.when(pl.program_id(0) == 0)
        def _():
            loss_ref[...] = jnp.zeros_like(loss_ref)
            dfg_ref[...] = jnp.zeros_like(dfg_ref)
            dbg_ref[...] = jnp.zeros_like(dbg_ref)
            dhgn_ref[...] = jnp.zeros_like(dhgn_ref)

        z_sb = zsb_ref[...]
        sb_o = sbo_ref[...]
        sig_zsb = _sigmoid(z_sb)
        silu_zsb = z_sb * sig_zsb
        a_sb = (sb_o * silu_zsb).astype(BF16)
        u_sb = _dot(a_sb, wsb_ref[...])

        hg_o = hgo_ref[...]
        gain = hgn_ref[...]
        r_parts, yn_parts = [], []
        for h in range(HEADS):
            oh = hg_o[:, h * HEAD_DIM : (h + 1) * HEAD_DIM]
            r = lax.rsqrt(jnp.mean(oh * oh, axis=-1, keepdims=True) + RMS_EPS)
            r_parts.append(jnp.broadcast_to(r, oh.shape))
            yn_parts.append(oh * r)
        r_hg = jnp.concatenate(r_parts, axis=-1)
        yn_hg = jnp.concatenate(yn_parts, axis=-1)
        hn = yn_hg * gain
        hz = hz_ref[...]
        sig_hz = _sigmoid(hz)
        silu_hz = hz * sig_hz
        a_hg = (hn * silu_hz).astype(BF16)
        u_hg = _dot(a_hg, whg_ref[...])

        gates = _sigmoid(gl_ref[...] + bg_ref[...])
        g_sb = gates[:, 0:D_MODEL]
        g_hg = gates[:, D_MODEL:]
        y = (g_sb * u_sb + g_hg * u_hg).astype(BF16)
        out = x_ref[...] + _dot(y, wout_ref[...])
        r2 = lax.rsqrt(jnp.mean(out * out, axis=-1, keepdims=True) + RMS_EPS)
        yn = out * r2
        fg = fg_ref[...]
        diff = yn * fg - tgt_ref[...]
        loss_ref[...] += 0.5 * inv_d * jnp.sum(diff * diff)

        dyf = diff * inv_d
        dfg_ref[...] += jnp.sum(dyf * yn, axis=0, keepdims=True)
        dyn = dyf * fg
        dout = r2 * (dyn - yn * jnp.mean(dyn * yn, axis=-1, keepdims=True))
        dout_ref[...] = dout
        doutb = dout.astype(BF16)
        doutb_ref[...] = doutb
        dy = _dot_nt(doutb, wout_ref[...])
        du_sb = (dy * g_sb).astype(BF16)
        du_hg = (dy * g_hg).astype(BF16)
        dgl_sb = dy * u_sb * g_sb * (1.0 - g_sb)
        dgl_hg = dy * u_hg * g_hg * (1.0 - g_hg)
        dgl_ref[:, 0:D_MODEL] = dgl_sb.astype(BF16)
        dgl_ref[:, D_MODEL:] = dgl_hg.astype(BF16)
        dbg_ref[:, 0:D_MODEL] += jnp.sum(dgl_sb, axis=0, keepdims=True)
        dbg_ref[:, D_MODEL:] += jnp.sum(dgl_hg, axis=0, keepdims=True)

        da_sb = _dot_nt(du_sb, wsb_ref[...])
        dsbo_ref[...] = da_sb * silu_zsb
        dzsb_ref[...] = (da_sb * sb_o * (sig_zsb * (1.0 + z_sb * (1.0 - sig_zsb)))).astype(BF16)

        da_hg = _dot_nt(du_hg, whg_ref[...])
        dhn = da_hg * silu_hz
        dhz_ref[...] = (da_hg * hn * (sig_hz * (1.0 + hz * (1.0 - sig_hz)))).astype(BF16)
        dhgn_ref[...] += jnp.sum(dhn * yn_hg, axis=0, keepdims=True)
        dyn_hg = dhn * gain
        prod = dyn_hg * yn_hg
        m_parts = []
        for h in range(HEADS):
            ph = prod[:, h * HEAD_DIM : (h + 1) * HEAD_DIM]
            m_parts.append(jnp.broadcast_to(jnp.mean(ph, axis=-1, keepdims=True), ph.shape))
        dhgo_ref[...] = r_hg * (dyn_hg - yn_hg * jnp.concatenate(m_parts, axis=-1))

        asb_ref[...] = a_sb
        dusb_ref[...] = du_sb
        ahg_ref[...] = a_hg
        duhg_ref[...] = du_hg
        y_ref[...] = y

    def tile(width, off=0):
        return pl.BlockSpec((ts, width), lambda s: (s, off // width))

    def whole(shape):
        return pl.BlockSpec(shape, lambda s: (0,) * len(shape))

    sq = (D_MODEL, D_MODEL)
    f32_act = jax.ShapeDtypeStruct((s_len, D_MODEL), F32)
    bf_act = jax.ShapeDtypeStruct((s_len, D_MODEL), BF16)
    return pl.pallas_call(
        body,
        name="mid",
        grid=(s_len // ts,),
        in_specs=[
            tile(D_MODEL, OFF_SB_Z), tile(D_MODEL, OFF_HG_Z), tile(2 * D_MODEL, OFF_GATE),
            tile(D_MODEL), tile(D_MODEL), tile(D_MODEL), tile(D_MODEL),
            whole((1, 2 * D_MODEL)), whole((1, D_MODEL)), whole((1, D_MODEL)),
            whole(sq), whole(sq), whole(sq),
        ],
        out_specs=[
            tile(D_MODEL), tile(D_MODEL), tile(D_MODEL), tile(D_MODEL), tile(D_MODEL), tile(2 * D_MODEL),
            tile(D_MODEL), tile(D_MODEL), tile(D_MODEL), tile(D_MODEL), tile(D_MODEL), tile(D_MODEL),
            whole((1, 1)), whole((1, D_MODEL)), whole((1, 2 * D_MODEL)), whole((1, D_MODEL)),
        ],
        out_shape=[
            f32_act, f32_act, f32_act, bf_act, bf_act, jax.ShapeDtypeStruct((s_len, 2 * D_MODEL), BF16),
            bf_act, bf_act, bf_act, bf_act, bf_act, bf_act,
            jax.ShapeDtypeStruct((1, 1), F32), jax.ShapeDtypeStruct((1, D_MODEL), F32),
            jax.ShapeDtypeStruct((1, 2 * D_MODEL), F32), jax.ShapeDtypeStruct((1, D_MODEL), F32),
        ],
        compiler_params=_cparams(("arbitrary",)),
    )(proj, proj, proj, sb_o, hg_o, x, target, b_gate, hg_gain, final_g, w_sb, w_hg, w_out)


def _grad_matmul(a, b, name, tn):
    s_len, m = a.shape
    n = b.shape[1]
    tk = min(512, s_len)

    def body(a_ref, b_ref, o_ref):
        @pl.when(pl.program_id(1) == 0)
        def _():
            o_ref[...] = jnp.zeros_like(o_ref)

        o_ref[...] += _dot_tn(a_ref[...], b_ref[...])

    return pl.pallas_call(
        body,
        name=name,
        grid=(n // tn, s_len // tk),
        in_specs=[pl.BlockSpec((tk, m), lambda j, k: (k, 0)), pl.BlockSpec((tk, tn), lambda j, k: (k, j))],
        out_specs=pl.BlockSpec((m, tn), lambda j, k: (0, j)),
        out_shape=jax.ShapeDtypeStruct((m, n), F32),
        compiler_params=_cparams(("arbitrary", "arbitrary")),
    )(a, b)


def _dx(dproj, w4, x, norm_g, dout):
    s_len = x.shape[0]
    ts = min(1024, s_len)
    tk = 1280
    per = W_IN_SHARD // tk
    nk = IN_WIDTH // tk

    def body(dp_ref, w_ref, x_ref, g_ref, dout_ref, gx_ref, dg_ref, acc):
        k = pl.program_id(1)

        @pl.when((pl.program_id(0) == 0) & (k == 0))
        def _():
            dg_ref[...] = jnp.zeros_like(dg_ref)

        part = _dot_nt(dp_ref[...], w_ref[0])

        @pl.when(k == 0)
        def _():
            acc[...] = part

        @pl.when(k > 0)
        def _():
            acc[...] += part

        @pl.when(k == nk - 1)
        def _():
            dh = acc[...]
            xv = x_ref[...]
            r = lax.rsqrt(jnp.mean(xv * xv, axis=-1, keepdims=True) + RMS_EPS)
            xn = xv * r
            dg_ref[...] += jnp.sum(dh * xn, axis=0, keepdims=True)
            dxn = dh * g_ref[...]
            gx_ref[...] = r * (dxn - xn * jnp.mean(dxn * xn, axis=-1, keepdims=True)) + dout_ref[...]

    row_tile = pl.BlockSpec((ts, D_MODEL), lambda s, k: (s, 0))
    vec = pl.BlockSpec((1, D_MODEL), lambda s, k: (0, 0))
    return pl.pallas_call(
        body,
        name="dx",
        grid=(s_len // ts, nk),
        in_specs=[
            pl.BlockSpec((ts, tk), lambda s, k: (s, k)),
            pl.BlockSpec((1, D_MODEL, tk), lambda s, k: (k // per, 0, k % per)),
            row_tile, vec, row_tile,
        ],
        out_specs=[row_tile, vec],
        out_shape=[jax.ShapeDtypeStruct((s_len, D_MODEL), F32), jax.ShapeDtypeStruct((1, D_MODEL), F32)],
        scratch_shapes=[pltpu.VMEM((ts, D_MODEL), F32)],
        compiler_params=_cparams(("arbitrary", "arbitrary")),
    )(dproj, w4, x, norm_g, dout)


def _local_step(x, target, norm_g, b_gate, lbl, hg_gain, final_g, w4, w_sb, w_hg, w_out):
    proj, h = _inproj(x, norm_g, w4)
    sb_o, sb_o_fine = _sb_fwd(proj)
    hg_o, states = _hg_fwd(proj, lbl)
    (dout, d_sbo, d_hgo, d_zsb, d_hz, d_gl, a_sb, du_sb, a_hg, du_hg, y, doutb,
     loss, d_fg, d_bg, d_hgn) = _mid(proj, sb_o, hg_o, x, target, b_gate, hg_gain, final_g, w_sb, w_hg, w_out)
    g_w_sb = _grad_matmul(a_sb, du_sb, "grad_w_sb", 512)
    g_w_hg = _grad_matmul(a_hg, du_hg, "grad_w_hg", 512)
    g_w_out = _grad_matmul(y, doutb, "grad_w_out", 512)
    d_q, d_k, d_v = _sb_bwd(proj, sb_o_fine, d_sbo)
    d_hg, d_lb = _hg_bwd(proj, lbl, states, d_hgo)
    dproj = jnp.concatenate([d_q, d_k, d_v, d_zsb, d_hg, d_hz, d_gl], axis=1)
    g_w_in = _grad_matmul(h, dproj, "grad_w_in", 512)
    grad_x, d_ng = _dx(dproj, w4, x, norm_g, dout)
    return grad_x, g_w_in, g_w_sb, g_w_hg, g_w_out, loss, d_ng, d_bg, d_lb, d_hgn, d_fg


ANY = pl.BlockSpec(memory_space=pl.ANY)
HALF_IN = D_MODEL // 2
HALF_SQ = ROW_SHARD // 2


def _position():
    x, y, c = lax.axis_index("x"), lax.axis_index("y"), lax.axis_index("c")
    chips = [(1 - x, y), (x, 1 - y), (1 - x, 1 - y)]
    return x, y, c, chips


def _remote(src, dst, send_sem, recv_sem, to):
    return pltpu.make_async_remote_copy(src_ref=src, dst_ref=dst, send_sem=send_sem, recv_sem=recv_sem,
                                        device_id=to, device_id_type=MESH)


def _gather_weights(w_in_b, w_sq_b):
    def body(win_ref, wsq_ref, in_ref, sq_ref, send_sems, recv_sems, local_sems):
        x, y, c, chips = _position()
        me = 2 * x + y
        sibling = (x, y, 1 - c)

        def in_half(chip, core):
            return in_ref.at[chip, pl.ds(core * HALF_IN, HALF_IN), :]

        def sq_half(chip, core):
            return sq_ref.at[:, chip, pl.ds(core * HALF_SQ, HALF_SQ), :]

        own = [pltpu.make_async_copy(win_ref, in_ref.at[me], local_sems.at[0]),
               pltpu.make_async_copy(wsq_ref, sq_ref.at[:, me], local_sems.at[1])]
        for cp in own:
            cp.start()
        sends = []
        for k, (px, py) in enumerate(chips):
            sends.append(_remote(win_ref.at[pl.ds(c * HALF_IN, HALF_IN), :], in_half(me, c),
                                 send_sems.at[0, k], recv_sems.at[0, k], (px, py, c)))
            sends.append(_remote(wsq_ref.at[:, pl.ds(c * HALF_SQ, HALF_SQ), :], sq_half(me, c),
                                 send_sems.at[1, k], recv_sems.at[1, k], (px, py, c)))
        for cp in sends:
            cp.start()
        for k, (px, py) in enumerate(chips):
            chip = 2 * px + py
            for a, half in enumerate((in_half, sq_half)):
                _remote(half(chip, c), half(chip, c), send_sems.at[a, k], recv_sems.at[a, k], (px, py, c)).wait_recv()
                fwd = _remote(half(chip, c), half(chip, c), send_sems.at[a, 3 + k], recv_sems.at[a, 3 + k], sibling)
                fwd.start()
                sends.append(fwd)
        for k, (px, py) in enumerate(chips):
            chip = 2 * px + py
            for a, half in enumerate((in_half, sq_half)):
                _remote(half(chip, 1 - c), half(chip, 1 - c), send_sems.at[a, 3 + k], recv_sems.at[a, 3 + k],
                        sibling).wait_recv()
        for cp in sends:
            cp.wait_send()
        for cp in own:
            cp.wait()

    return pl.pallas_call(
        body,
        name="gather_weights",
        in_specs=[ANY, ANY],
        out_specs=[ANY, ANY],
        out_shape=[jax.ShapeDtypeStruct((N_CHIPS, D_MODEL, W_IN_SHARD), BF16),
                   jax.ShapeDtypeStruct((3, N_CHIPS, ROW_SHARD, D_MODEL), BF16)],
        scratch_shapes=[pltpu.SemaphoreType.DMA((2, 6)), pltpu.SemaphoreType.DMA((2, 6)),
                        pltpu.SemaphoreType.DMA((2,))],
    )(w_in_b, w_sq_b)


def _swap_halves(g_in, g_sq):
    def body(gin_ref, gsq_ref, mine_in, mine_sq, got_in, got_sq, send_sems, recv_sems, local_sems):
        x, y, c, _ = _position()
        sibling = (x, y, 1 - c)

        def rows_in(core):
            return gin_ref.at[pl.ds(core * HALF_IN, HALF_IN), :]

        def rows_sq(core):
            return gsq_ref.at[:, :, pl.ds(core * HALF_SQ, HALF_SQ), :]

        own = [pltpu.make_async_copy(rows_in(c), mine_in, local_sems.at[0]),
               pltpu.make_async_copy(rows_sq(c), mine_sq, local_sems.at[1])]
        out = [_remote(rows_in(1 - c), got_in, send_sems.at[0], recv_sems.at[0], sibling),
               _remote(rows_sq(1 - c), got_sq, send_sems.at[1], recv_sems.at[1], sibling)]
        for cp in own + out:
            cp.start()
        for cp in out:
            cp.wait()
        for cp in own:
            cp.wait()

    half_in = jax.ShapeDtypeStruct((HALF_IN, IN_WIDTH), F32)
    half_sq = jax.ShapeDtypeStruct((3, N_CHIPS, HALF_SQ, D_MODEL), F32)
    return pl.pallas_call(
        body,
        name="swap_halves",
        in_specs=[ANY, ANY],
        out_specs=[ANY] * 4,
        out_shape=[half_in, half_sq, half_in, half_sq],
        scratch_shapes=[pltpu.SemaphoreType.DMA((2,)), pltpu.SemaphoreType.DMA((2,)), pltpu.SemaphoreType.DMA((2,))],
    )(g_in, g_sq)


def _exchange_chunks(s_in, s_sq):
    def body(sin_ref, ssq_ref, mine_in, mine_sq, got_in, got_sq, send_sems, recv_sems, local_sems):
        x, y, c, chips = _position()
        me = 2 * x + y
        own = [pltpu.make_async_copy(sin_ref.at[me], mine_in, local_sems.at[0]),
               pltpu.make_async_copy(ssq_ref.at[:, me], mine_sq, local_sems.at[1])]
        out = []
        for k, (px, py) in enumerate(chips):
            chip = 2 * px + py
            out.append(_remote(sin_ref.at[chip], got_in.at[k], send_sems.at[0, k], recv_sems.at[0, k], (px, py, c)))
            out.append(_remote(ssq_ref.at[:, chip], got_sq.at[k], send_sems.at[1, k], recv_sems.at[1, k], (px, py, c)))
        for cp in own + out:
            cp.start()
        for cp in out:
            cp.wait()
        for cp in own:
            cp.wait()

    return pl.pallas_call(
        body,
        name="exchange_chunks",
        in_specs=[ANY, ANY],
        out_specs=[ANY] * 4,
        out_shape=[jax.ShapeDtypeStruct((HALF_IN, W_IN_SHARD), F32),
                   jax.ShapeDtypeStruct((3, HALF_SQ, D_MODEL), F32),
                   jax.ShapeDtypeStruct((3, HALF_IN, W_IN_SHARD), F32),
                   jax.ShapeDtypeStruct((3, 3, HALF_SQ, D_MODEL), F32)],
        scratch_shapes=[pltpu.SemaphoreType.DMA((2, 3)), pltpu.SemaphoreType.DMA((2, 3)),
                        pltpu.SemaphoreType.DMA((2,))],
    )(s_in, s_sq)


def _join_halves(r_in, r_sq):
    def body(rin_ref, rsq_ref, full_in, full_sq, send_sems, recv_sems, local_sems):
        x, y, c, _ = _position()
        sibling = (x, y, 1 - c)

        def rows_in(core):
            return full_in.at[pl.ds(core * HALF_IN, HALF_IN), :]

        def rows_sq(core):
            return full_sq.at[:, pl.ds(core * HALF_SQ, HALF_SQ), :]

        own = [pltpu.make_async_copy(rin_ref, rows_in(c), local_sems.at[0]),
               pltpu.make_async_copy(rsq_ref, rows_sq(c), local_sems.at[1])]
        out = [_remote(rin_ref, rows_in(c), send_sems.at[0], recv_sems.at[0], sibling),
               _remote(rsq_ref, rows_sq(c), send_sems.at[1], recv_sems.at[1], sibling)]
        for cp in own + out:
            cp.start()
        _remote(rin_ref, rows_in(1 - c), send_sems.at[0], recv_sems.at[0], sibling).wait_recv()
        _remote(rsq_ref, rows_sq(1 - c), send_sems.at[1], recv_sems.at[1], sibling).wait_recv()
        for cp in out:
            cp.wait_send()
        for cp in own:
            cp.wait()

    return pl.pallas_call(
        body,
        name="join_halves",
        in_specs=[ANY, ANY],
        out_specs=[ANY, ANY],
        out_shape=[jax.ShapeDtypeStruct((D_MODEL, W_IN_SHARD), F32),
                   jax.ShapeDtypeStruct((3, ROW_SHARD, D_MODEL), F32)],
        scratch_shapes=[pltpu.SemaphoreType.DMA((2,)), pltpu.SemaphoreType.DMA((2,)), pltpu.SemaphoreType.DMA((2,))],
    )(r_in, r_sq)


SMALL_ROWS = 56
N_DEV = 8


def _sum_small(part):
    def body(part_ref, out_ref, slots, send_sems, recv_sems):
        x, y, c, _ = _position()
        me = 4 * x + 2 * y + c
        slots[me] = part_ref[...]
        out = []
        for r in range(1, N_DEV):
            rx, ry, rc = (r >> 2) & 1, (r >> 1) & 1, r & 1
            to = (1 - x if rx else x, 1 - y if ry else y, 1 - c if rc else c)
            out.append(_remote(part_ref, slots.at[me], send_sems.at[r - 1], recv_sems.at[r - 1], to))
        for cp in out:
            cp.start()
        for r in range(1, N_DEV):
            _remote(part_ref, slots.at[me ^ r], send_sems.at[r - 1], recv_sems.at[r - 1], (x, y, c)).wait_recv()
        for cp in out:
            cp.wait_send()
        total = slots[0]
        for d in range(1, N_DEV):
            total = total + slots[d]
        out_ref[...] = total

    vmem = pl.BlockSpec(memory_space=pltpu.VMEM)
    return pl.pallas_call(
        body,
        name="sum_small",
        in_specs=[vmem],
        out_specs=vmem,
        out_shape=jax.ShapeDtypeStruct((SMALL_ROWS, HEAD_DIM), F32),
        scratch_shapes=[pltpu.VMEM((N_DEV, SMALL_ROWS, HEAD_DIM), F32),
                        pltpu.SemaphoreType.DMA((N_DEV - 1,)), pltpu.SemaphoreType.DMA((N_DEV - 1,))],
    )(part)


def _add_by_chunk(a, b):
    tr = 128

    def body(a_ref, b_ref, o_ref):
        o_ref[0] = a_ref[...] + b_ref[...]

    spec = pl.BlockSpec((tr, W_IN_SHARD), lambda j, r: (r, j))
    return pl.pallas_call(
        body,
        name="add_by_chunk",
        grid=(N_CHIPS, HALF_IN // tr),
        in_specs=[spec, spec],
        out_specs=pl.BlockSpec((1, tr, W_IN_SHARD), lambda j, r: (j, r, 0)),
        out_shape=jax.ShapeDtypeStruct((N_CHIPS, HALF_IN, W_IN_SHARD), F32),
        compiler_params=_cparams(("arbitrary", "arbitrary")),
    )(a, b)


def _add_rows(first, rest, name):
    rows, cols = first.shape
    n = rest.shape[0]
    tr = min(128, rows)

    def body(a_ref, b_ref, o_ref):
        acc = a_ref[...]
        for i in range(n):
            acc = acc + b_ref[i]
        o_ref[...] = acc

    return pl.pallas_call(
        body,
        name=name,
        grid=(rows // tr,),
        in_specs=[pl.BlockSpec((tr, cols), lambda r: (r, 0)), pl.BlockSpec((n, tr, cols), lambda r: (0, r, 0))],
        out_specs=pl.BlockSpec((tr, cols), lambda r: (r, 0)),
        out_shape=jax.ShapeDtypeStruct((rows, cols), F32),
        compiler_params=_cparams(("arbitrary",)),
    )(first, rest)


def _adamw_math(w, g, m, v):
    m = ADAM_B1 * m + (1.0 - ADAM_B1) * g
    v = ADAM_B2 * v + (1.0 - ADAM_B2) * (g * g)
    m_hat = m / (1.0 - ADAM_B1 ** ADAM_STEP)
    v_hat = v / (1.0 - ADAM_B2 ** ADAM_STEP)
    delta = -ADAM_LR * (m_hat / (jnp.sqrt(v_hat) + ADAM_EPS) + ADAM_WD * w)
    return delta, m, v


def _adamw(w, g, m, v, name):
    rows, cols = w.shape
    tr = min(128, rows)

    def body(w_ref, g_ref, m_ref, v_ref, d_ref, nm_ref, nv_ref):
        d_ref[...], nm_ref[...], nv_ref[...] = _adamw_math(w_ref[...], g_ref[...], m_ref[...], v_ref[...])

    spec = pl.BlockSpec((tr, cols), lambda r: (r, 0))
    return pl.pallas_call(
        body,
        name=name,
        grid=(rows // tr,),
        in_specs=[spec] * 4,
        out_specs=[spec] * 3,
        out_shape=[jax.ShapeDtypeStruct((rows, cols), F32)] * 3,
        compiler_params=_cparams(("arbitrary",)),
    )(w, g, m, v)


def _adamw_small(sums, w, m, v):
    def body(s_ref, w_ref, m_ref, v_ref, loss_ref, g_ref, d_ref, nm_ref, nv_ref):
        s = s_ref[...]
        w = w_ref[...]
        loss_ref[...] = s[0:1, 0:1]
        l0, l1 = w[24:32], w[32:40]
        mx = jnp.maximum(l0, l1)
        e0, e1 = jnp.exp(l0 - mx), jnp.exp(l1 - mx)
        p0, p1 = e0 / (e0 + e1), e1 / (e0 + e1)
        d_lb = s[32:40]
        g = jnp.concatenate([s[8:16], s[16:32], d_lb * p0 * (1.0 - p0), -d_lb * p0 * p1, s[40:48], s[48:56]], axis=0)
        g_ref[...] = g
        d_ref[...], nm_ref[...], nv_ref[...] = _adamw_math(w, g, m_ref[...], v_ref[...])

    packed = jax.ShapeDtypeStruct((SMALL_ROWS, HEAD_DIM), F32)
    return pl.pallas_call(
        body,
        name="adamw_small",
        out_shape=[jax.ShapeDtypeStruct((1, 1), F32), packed, packed, packed, packed],
    )(sums, w, m, v)


def _pack_small(ng, bg, lbl, hgn, fg):
    return jnp.concatenate([a.reshape(-1, HEAD_DIM) for a in (ng, bg, lbl, hgn, fg)], axis=0)


def _unpack_small(p):
    return (p[0:8].reshape(1, D_MODEL), p[8:24].reshape(1, 2 * D_MODEL), p[24:40].reshape(2, HEADS, HEAD_DIM),
            p[40:48].reshape(1, HEADS, HEAD_DIM), p[48:56].reshape(D_MODEL))


def kernel(x, norm_g, w_in, b_gate, lb_logits, hg_norm_g, w_sb_proj, w_hg_proj, w_out, final_norm_g, loss_target, m_norm_g, m_w_in, m_b_gate, m_lb_logits, m_hg_norm_g, m_w_sb_proj, m_w_hg_proj, m_w_out, m_final_norm_g, v_norm_g, v_w_in, v_b_gate, v_lb_logits, v_hg_norm_g, v_w_sb_proj, v_w_hg_proj, v_w_out, v_final_norm_g):
    s_len = x.shape[1]
    w_sq = jnp.stack([w_sb_proj[0], w_hg_proj[0], w_out[0]])
    w4, wsq = _gather_weights(w_in[0].astype(BF16), w_sq.astype(BF16))
    wsq = wsq.reshape(3, D_MODEL, D_MODEL)

    (grad_x, g_in, g_sb, g_hg, g_out, loss, d_ng, d_bg, d_lb, d_hgn, d_fg) = _local_step(
        x[0], loss_target[0], norm_g, b_gate, lb_logits.reshape(2, D_MODEL), hg_norm_g.reshape(1, D_MODEL),
        final_norm_g.reshape(1, D_MODEL), w4, wsq[0], wsq[1], wsq[2])

    g_sq = jnp.stack([g_sb, g_hg, g_out]).reshape(3, N_CHIPS, ROW_SHARD, D_MODEL)
    mine_in, mine_sq, got_in, got_sq = _swap_halves(g_in, g_sq)
    s_in = _add_by_chunk(mine_in, got_in)
    sq_cols = N_CHIPS * HALF_SQ
    s_sq = _add_rows(mine_sq.reshape(3 * sq_cols, D_MODEL), got_sq.reshape(1, 3 * sq_cols, D_MODEL), "add_sq_a")
    mine_in, mine_sq, got_in, got_sq = _exchange_chunks(s_in, s_sq.reshape(3, N_CHIPS, HALF_SQ, D_MODEL))
    r_in = _add_rows(mine_in, got_in, "add_in_b")
    r_sq = _add_rows(mine_sq.reshape(3 * HALF_SQ, D_MODEL), got_sq.reshape(3, 3 * HALF_SQ, D_MODEL), "add_sq_b")
    grad_in, grad_sq = _join_halves(r_in, r_sq.reshape(3, HALF_SQ, D_MODEL))

    d_in, nm_in, nv_in = _adamw(w_in[0], grad_in, m_w_in[0], v_w_in[0], "adamw_in")
    flat = lambda a, b, c: jnp.concatenate([a[0], b[0], c[0]], axis=0)
    d_sq, nm_sq, nv_sq = _adamw(flat(w_sb_proj, w_hg_proj, w_out), grad_sq.reshape(3 * ROW_SHARD, D_MODEL),
                                flat(m_w_sb_proj, m_w_hg_proj, m_w_out), flat(v_w_sb_proj, v_w_hg_proj, v_w_out),
                                "adamw_sq")

    pad = jnp.zeros((8, HEAD_DIM), F32).at[0, 0].set(loss[0, 0])
    part = jnp.concatenate([pad] + [a.reshape(-1, HEAD_DIM) for a in (d_ng, d_bg, d_lb, d_hgn, d_fg)], axis=0)
    sums = _sum_small(part)
    loss_out, g_sm, d_sm, nm_sm, nv_sm = _adamw_small(
        sums, _pack_small(norm_g, b_gate, lb_logits, hg_norm_g, final_norm_g),
        _pack_small(m_norm_g, m_b_gate, m_lb_logits, m_hg_norm_g, m_final_norm_g),
        _pack_small(v_norm_g, v_b_gate, v_lb_logits, v_hg_norm_g, v_final_norm_g))

    def big(t_in, t_sq):
        sq = t_sq.reshape(3, 1, ROW_SHARD, D_MODEL)
        return t_in[None], sq[0], sq[1], sq[2]

    def order(small, in_, sb, hg, out):
        ng, bg, lbl, hgn, fg = small
        return [ng, in_, bg, lbl, hgn, sb, hg, out, fg]

    outs = [loss_out[0, 0], grad_x[None]]
    for small, (t_in, t_sq) in ((g_sm, (grad_in, grad_sq)), (d_sm, (d_in, d_sq)), (nm_sm, (nm_in, nm_sq)), (nv_sm, (nv_in, nv_sq))):
        outs += order(_unpack_small(small), *big(t_in, t_sq))
    return tuple(outs)
```

```python
import functools

import jax
import jax.numpy as jnp
from jax import lax
from jax.experimental import pallas as pl
from jax.experimental.pallas import tpu as pltpu

F32 = jnp.float32
BF16 = jnp.bfloat16

D_MODEL = 1024
HEADS = 8
HEAD_DIM = 128
IN_WIDTH = 10240
N_CHIPS = 4
W_IN_SHARD = IN_WIDTH // N_CHIPS
ROW_SHARD = D_MODEL // N_CHIPS
RMS_EPS = 1e-6

OFF_SB_Q, OFF_SB_K, OFF_SB_V, OFF_SB_Z = 0, 1024, 2048, 3072
OFF_HG_Q, OFF_HG_F, OFF_HG_I, OFF_HG_Z, OFF_GATE = 4096, 5120, 6144, 7168, 8192

SB_BLOCK = 256
SB_DEAD = -110.0
HG_CHUNK = 32
HG_STEP = 256
HG_MID = HG_CHUNK // 2 - 1

ADAM_LR, ADAM_B1, ADAM_B2, ADAM_EPS, ADAM_WD, ADAM_STEP = 0.001, 0.9, 0.999, 1e-08, 0.01, 10

VMEM_LIMIT = 56 * 1024 * 1024

MESH = pl.DeviceIdType.MESH


def _cparams(sem, vmem=VMEM_LIMIT):
    return pltpu.CompilerParams(dimension_semantics=sem, vmem_limit_bytes=vmem)


def _dot(a, b):
    return jnp.dot(a, b, preferred_element_type=F32)


def _dot_nt(a, b):
    return lax.dot_general(a, b, (((1,), (1,)), ((), ())), preferred_element_type=F32)


def _dot_tn(a, b):
    return lax.dot_general(a, b, (((0,), (0,)), ((), ())), preferred_element_type=F32)


def _split_dot(x, tri):
    hi = x.astype(BF16)
    lo = (x - hi.astype(F32)).astype(BF16)
    return _dot(hi, tri) + _dot(lo, tri)


def _split_dot_left(tri, x):
    hi = x.astype(BF16)
    lo = (x - hi.astype(F32)).astype(BF16)
    return _dot(tri, hi) + _dot(tri, lo)


def _sigmoid(x):
    return 1.0 / (1.0 + jnp.exp(-x))


def _inproj(x, norm_g, w4):
    s_len = x.shape[0]
    ts = min(1024, s_len)
    tn = 1280
    per = W_IN_SHARD // tn

    def body(x_ref, g_ref, w_ref, proj_ref, h_ref):
        @pl.when(pl.program_id(1) == 0)
        def _():
            xv = x_ref[...]
            r = lax.rsqrt(jnp.mean(xv * xv, axis=-1, keepdims=True) + RMS_EPS)
            h_ref[...] = ((xv * r) * g_ref[...]).astype(BF16)

        proj_ref[...] = _dot(h_ref[...], w_ref[0])

    return pl.pallas_call(
        body,
        name="inproj",
        grid=(s_len // ts, IN_WIDTH // tn),
        in_specs=[
            pl.BlockSpec((ts, D_MODEL), lambda s, n: (s, 0)),
            pl.BlockSpec((1, D_MODEL), lambda s, n: (0, 0)),
            pl.BlockSpec((1, D_MODEL, tn), lambda s, n: (n // per, 0, n % per)),
        ],
        out_specs=[
            pl.BlockSpec((ts, tn), lambda s, n: (s, n)),
            pl.BlockSpec((ts, D_MODEL), lambda s, n: (s, 0)),
        ],
        out_shape=[
            jax.ShapeDtypeStruct((s_len, IN_WIDTH), F32),
            jax.ShapeDtypeStruct((s_len, D_MODEL), BF16),
        ],
        compiler_params=_cparams(("arbitrary", "arbitrary")),
    )(x, norm_g, w4)


def _sb_tile_fwd(qb, kb, row_gt_col, tri_excl, carry, diag):
    scale = HEAD_DIM ** -0.5
    z = _dot_nt(qb, kb) * scale
    ls_pos = jnp.minimum(z, 0.0) - jnp.log1p(jnp.exp(-jnp.abs(z)))
    log_not = ls_pos - z
    log_not_m = jnp.where(row_gt_col, log_not, 0.0) if diag else log_not
    surv = _split_dot(log_not_m, tri_excl) + carry
    w = jnp.exp(ls_pos + surv)
    if diag:
        w = jnp.where(row_gt_col, w, 0.0)
    return ls_pos, log_not, log_not_m, surv, w


def _sb_fwd(proj):
    s_len = proj.shape[0]
    blk = min(SB_BLOCK, s_len)
    nq = s_len // blk

    def body(q_ref, k_ref, v_ref, o_ref, of_ref):
        i = pl.program_id(1)
        qb = q_ref[...].astype(BF16)
        row = lax.broadcasted_iota(jnp.int32, (blk, blk), 0)
        col = lax.broadcasted_iota(jnp.int32, (blk, blk), 1)
        row_gt_col = row > col
        tri_excl = row_gt_col.astype(BF16)

        def tile(j, carry, acc, acc_lo, diag):
            start = pl.multiple_of(j * blk, blk)
            kb = k_ref[pl.ds(start, blk), :].astype(BF16)
            vb = v_ref[pl.ds(start, blk), :].astype(BF16)
            _, _, log_not_m, surv, w = _sb_tile_fwd(qb, kb, row_gt_col, tri_excl, carry, diag)
            wb = w.astype(BF16)
            w_lo = (w - wb.astype(F32)).astype(BF16)
            acc = acc + _dot(wb, vb)
            acc_lo = acc_lo + _dot(w_lo, vb)
            carry = surv[:, 0:1] + log_not_m[:, 0:1]
            return carry, acc, acc_lo

        zero = jnp.zeros((blk, HEAD_DIM), F32)
        carry, acc, acc_lo = tile(i, jnp.zeros((blk, 1), F32), zero, zero, True)

        def more(st):
            return (st[0] < i) & (jnp.max(st[1]) > SB_DEAD)

        def step(st):
            return (st[0] + 1,) + tile(i - 1 - st[0], *st[1:], False)

        _, carry, acc, acc_lo = lax.while_loop(more, step, (0, carry, acc, acc_lo))
        o_ref[...] = acc
        of_ref[...] = acc + acc_lo

    def col_spec(off, rows):
        if rows == blk:
            return pl.BlockSpec((blk, HEAD_DIM), lambda h, i: (i, off // HEAD_DIM + h))
        return pl.BlockSpec((s_len, HEAD_DIM), lambda h, i: (0, off // HEAD_DIM + h))

    out_spec = pl.BlockSpec((blk, HEAD_DIM), lambda h, i: (i, h))
    return pl.pallas_call(
        body,
        name="sb_fwd",
        grid=(HEADS, nq),
        in_specs=[col_spec(OFF_SB_Q, blk), col_spec(OFF_SB_K, s_len), col_spec(OFF_SB_V, s_len)],
        out_specs=[out_spec, out_spec],
        out_shape=[jax.ShapeDtypeStruct((s_len, D_MODEL), F32)] * 2,
        compiler_params=_cparams(("arbitrary", "arbitrary")),
    )(proj, proj, proj)


def _sb_bwd(proj, o_fine, d_o):
    s_len = proj.shape[0]
    blk = min(SB_BLOCK, s_len)
    nq = s_len // blk
    scale = HEAD_DIM ** -0.5

    def body(q_ref, k_ref, v_ref, of_ref, do_ref, dq_ref, dk_ref, dv_ref, dk_acc, dv_acc):
        i = pl.program_id(1)

        @pl.when(i == 0)
        def _():
            dk_acc[...] = jnp.zeros_like(dk_acc)
            dv_acc[...] = jnp.zeros_like(dv_acc)

        qb = q_ref[...].astype(BF16)
        dob = do_ref[...].astype(BF16)
        total = jnp.sum(dob.astype(F32) * of_ref[...], axis=-1, keepdims=True)
        row = lax.broadcasted_iota(jnp.int32, (blk, blk), 0)
        col = lax.broadcasted_iota(jnp.int32, (blk, blk), 1)
        row_gt_col = row > col
        tri_excl = row_gt_col.astype(BF16)
        tri_incl = (row >= col).astype(BF16)

        def tile(j, c_not, c_dlw, dq, diag):
            start = pl.multiple_of(j * blk, blk)
            kb = k_ref[pl.ds(start, blk), :].astype(BF16)
            vb = v_ref[pl.ds(start, blk), :].astype(BF16)
            ls_pos, log_not, log_not_m, surv, w = _sb_tile_fwd(qb, kb, row_gt_col, tri_excl, c_not, diag)
            dlw = _dot_nt(dob, vb) * w
            suffix = _split_dot(dlw, tri_incl)
            d_not = total - c_dlw - suffix
            dz = (dlw * jnp.exp(log_not) - d_not * jnp.exp(ls_pos)) * scale
            if diag:
                dz = jnp.where(row_gt_col, dz, 0.0)
            dzb = dz.astype(BF16)
            dq = dq + _dot(dzb, kb)
            dk_acc[pl.ds(start, blk), :] += _dot_tn(dzb, qb)
            dv_acc[pl.ds(start, blk), :] += _dot_tn(w.astype(BF16), dob)
            c_not = surv[:, 0:1] + log_not_m[:, 0:1]
            c_dlw = c_dlw + suffix[:, 0:1]
            return c_not, c_dlw, dq

        zcol = jnp.zeros((blk, 1), F32)
        st = tile(i, zcol, zcol, jnp.zeros((blk, HEAD_DIM), F32), True)

        def more(st):
            return (st[0] < i) & (jnp.max(st[1]) > SB_DEAD)

        def step(st):
            return (st[0] + 1,) + tile(i - 1 - st[0], *st[1:], False)

        _, _, _, dq = lax.while_loop(more, step, (0,) + st)
        dq_ref[...] = dq.astype(BF16)

        @pl.when(i == nq - 1)
        def _():
            dk_ref[...] = dk_acc[...].astype(BF16)
            dv_ref[...] = dv_acc[...].astype(BF16)

    def blk_spec(off):
        return pl.BlockSpec((blk, HEAD_DIM), lambda h, i: (i, off // HEAD_DIM + h))

    def head_spec(off):
        return pl.BlockSpec((s_len, HEAD_DIM), lambda h, i: (0, off // HEAD_DIM + h))

    return pl.pallas_call(
        body,
        name="sb_bwd",
        grid=(HEADS, nq),
        in_specs=[blk_spec(OFF_SB_Q), head_spec(OFF_SB_K), head_spec(OFF_SB_V), blk_spec(0), blk_spec(0)],
        out_specs=[blk_spec(0), head_spec(0), head_spec(0)],
        out_shape=[jax.ShapeDtypeStruct((s_len, D_MODEL), BF16)] * 3,
        scratch_shapes=[pltpu.VMEM((s_len, HEAD_DIM), F32), pltpu.VMEM((s_len, HEAD_DIM), F32)],
        compiler_params=_cparams(("arbitrary", "arbitrary")),
    )(proj, proj, proj, o_fine, d_o)


def _hg_lower_bound(lbl_ref):
    l0 = lbl_ref[0:1, :]
    l1 = lbl_ref[1:2, :]
    mx = jnp.maximum(l0, l1)
    e0 = jnp.exp(l0 - mx)
    e1 = jnp.exp(l1 - mx)
    return e0 / (e0 + e1)


def _hg_gates(hq, hf, lb):
    sig_f = _sigmoid(hf)
    f = lb + (1.0 - lb) * sig_f
    g = jnp.log(f)
    kk = 1.0 - f
    sig_q = _sigmoid(hq)
    qq = hq * sig_q
    return qq, kk, g, f, sig_f, sig_q


def _chunk_bcast(x, r, rows):
    w = x.shape[-1]
    x3 = x.reshape(rows // HG_CHUNK, HG_CHUNK, w)
    return jnp.broadcast_to(x3[:, r : r + 1, :], x3.shape).reshape(rows, w)


def _hg_decays(qq, kk, g, tri_blk, rows):
    cum = _split_dot_left(tri_blk, g)
    mid = _chunk_bcast(cum, HG_MID, rows)
    last = _chunk_bcast(cum, HG_CHUNK - 1, rows)
    e_qm = jnp.exp(cum - mid)
    e_km = jnp.exp(mid - cum)
    e_q = jnp.exp(cum)
    e_kl = jnp.exp(last - cum)
    return cum, last, e_qm, e_km, e_q, e_kl


def _blockdiag(rows, kind):
    row = lax.broadcasted_iota(jnp.int32, (rows, rows), 0)
    col = lax.broadcasted_iota(jnp.int32, (rows, rows), 1)
    keep = (row // HG_CHUNK) == (col // HG_CHUNK)
    if kind == "lower":
        keep = keep & (row >= col)
    elif kind == "upper":
        keep = keep & (row <= col)
    return jnp.where(keep, 1.0, 0.0).astype(BF16)


def _hg_fwd(proj, lbl):
    s_len = proj.shape[0]
    rows = min(HG_STEP, s_len)
    n_chunks = rows // HG_CHUNK

    def body(hq_ref, hf_ref, hi_ref, lbl_ref, o_ref, st_ref, state, q_mid, k_mid, q_dec, k_last, v_b):
        @pl.when(pl.program_id(0) == 0)
        def _():
            state[...] = jnp.zeros_like(state)

        lb = _hg_lower_bound(lbl_ref)
        qq, kk, g, _, _, _ = _hg_gates(hq_ref[...], hf_ref[...], lb)
        tri_blk = _blockdiag(rows, "lower")
        _, last, e_qm, e_km, e_q, e_kl = _hg_decays(qq, kk, g, tri_blk, rows)
        q_mid[...] = (qq * e_qm).astype(BF16)
        k_mid[...] = (kk * e_km).astype(BF16)
        q_dec[...] = (qq * e_q).astype(BF16)
        k_last[...] = (kk * e_kl).astype(BF16)
        v_b[...] = hi_ref[...].astype(BF16)
        e_last = jnp.exp(last)
        row = lax.broadcasted_iota(jnp.int32, (HG_CHUNK, HG_CHUNK), 0)
        col = lax.broadcasted_iota(jnp.int32, (HG_CHUNK, HG_CHUNK), 1)
        causal = row >= col

        for c in range(n_chunks):
            r0 = c * HG_CHUNK
            for h in range(HEADS):
                c0 = h * HEAD_DIM
                sl = (slice(r0, r0 + HG_CHUNK), slice(c0, c0 + HEAD_DIM))
                st = state[h]
                st_ref[c, h] = st
                a = jnp.where(causal, _dot_nt(q_mid[sl], k_mid[sl]), 0.0)
                vb = v_b[sl]
                o_ref[sl] = _dot(a.astype(BF16), vb) + _dot_nt(q_dec[sl], st.astype(BF16))
                decay = e_last[r0 : r0 + 1, c0 : c0 + HEAD_DIM]
                state[h] = st * decay + _dot_tn(vb, k_last[sl])

    def col_spec(off):
        return pl.BlockSpec((rows, D_MODEL), lambda s: (s, off // D_MODEL))

    scratch = [pltpu.VMEM((HEADS, HEAD_DIM, HEAD_DIM), F32)] + [pltpu.VMEM((rows, D_MODEL), BF16)] * 5
    return pl.pallas_call(
        body,
        name="hg_fwd",
        grid=(s_len // rows,),
        in_specs=[col_spec(OFF_HG_Q), col_spec(OFF_HG_F), col_spec(OFF_HG_I), pl.BlockSpec((2, D_MODEL), lambda s: (0, 0))],
        out_specs=[
            pl.BlockSpec((rows, D_MODEL), lambda s: (s, 0)),
            pl.BlockSpec((n_chunks, HEADS, HEAD_DIM, HEAD_DIM), lambda s: (s, 0, 0, 0)),
        ],
        out_shape=[
            jax.ShapeDtypeStruct((s_len, D_MODEL), F32),
            jax.ShapeDtypeStruct((s_len // HG_CHUNK, HEADS, HEAD_DIM, HEAD_DIM), F32),
        ],
        scratch_shapes=scratch,
        compiler_params=_cparams(("arbitrary",)),
    )(proj, proj, proj, lbl)


def _hg_bwd(proj, lbl, states, d_o):
    s_len = proj.shape[0]
    rows = min(HG_STEP, s_len)
    n_chunks = rows // HG_CHUNK
    n_steps = s_len // rows

    def body(hq_ref, hf_ref, hi_ref, lbl_ref, st_ref, do_ref, dp_ref, dlb_ref,
             dstate, q_mid, k_mid, q_dec, k_last, v_b, do_b, d_qm, d_km, d_qd, d_kl, d_v, d_last):
        @pl.when(pl.program_id(0) == 0)
        def _():
            dstate[...] = jnp.zeros_like(dstate)
            dlb_ref[...] = jnp.zeros_like(dlb_ref)

        lb = _hg_lower_bound(lbl_ref)
        hq = hq_ref[...]
        qq, kk, g, f, sig_f, sig_q = _hg_gates(hq, hf_ref[...], lb)
        tri_blk = _blockdiag(rows, "lower")
        _, last, e_qm, e_km, e_q, e_kl = _hg_decays(qq, kk, g, tri_blk, rows)
        qm, km, qd, kl = qq * e_qm, kk * e_km, qq * e_q, kk * e_kl
        q_mid[...] = qm.astype(BF16)
        k_mid[...] = km.astype(BF16)
        q_dec[...] = qd.astype(BF16)
        k_last[...] = kl.astype(BF16)
        v_b[...] = hi_ref[...].astype(BF16)
        do_b[...] = do_ref[...].astype(BF16)
        e_last = jnp.exp(last)
        row = lax.broadcasted_iota(jnp.int32, (HG_CHUNK, HG_CHUNK), 0)
        col = lax.broadcasted_iota(jnp.int32, (HG_CHUNK, HG_CHUNK), 1)
        causal = row >= col

        for c in reversed(range(n_chunks)):
            r0 = c * HG_CHUNK
            for h in range(HEADS):
                c0 = h * HEAD_DIM
                sl = (slice(r0, r0 + HG_CHUNK), slice(c0, c0 + HEAD_DIM))
                st0 = st_ref[c, h]
                ds1 = dstate[h]
                ds1b = ds1.astype(BF16)
                dob, vb, qmb, kmb = do_b[sl], v_b[sl], q_mid[sl], k_mid[sl]
                a = jnp.where(causal, _dot_nt(qmb, kmb), 0.0).astype(BF16)
                da = jnp.where(causal, _dot_nt(dob, vb), 0.0).astype(BF16)
                d_v[sl] = _dot_tn(a, dob) + _dot_nt(k_last[sl], ds1b)
                d_qm[sl] = _dot(da, kmb)
                d_km[sl] = _dot_tn(da, qmb)
                d_qd[sl] = _dot(dob, st0.astype(BF16))
                d_kl[sl] = _dot(vb, ds1b)
                decay = e_last[r0 : r0 + 1, c0 : c0 + HEAD_DIM]
                d_last[c : c + 1, c0 : c0 + HEAD_DIM] = decay * jnp.sum(ds1 * st0, axis=0, keepdims=True)
                dstate[h] = ds1 * decay + _dot_tn(dob, q_dec[sl])

        dqm, dkm, dqd, dkl = d_qm[...], d_km[...], d_qd[...], d_kl[...]
        dq = dqm * e_qm + dqd * e_q
        dk = dkm * e_km + dkl * e_kl
        t_kl = dkl * kl
        dcum = dqm * qm - dkm * km + dqd * qd - t_kl
        dl = d_last[...]
        dl_b = jnp.broadcast_to(dl[:, None, :], (n_chunks, HG_CHUNK, D_MODEL)).reshape(rows, D_MODEL)
        dg = _split_dot_left(_blockdiag(rows, "upper"), dcum) + _split_dot_left(_blockdiag(rows, "all"), t_kl) + dl_b
        df = dg / f - dk
        one_m = 1.0 - sig_f
        dp_ref[:, 0:D_MODEL] = (dq * (sig_q * (1.0 + hq * (1.0 - sig_q)))).astype(BF16)
        dp_ref[:, D_MODEL : 2 * D_MODEL] = (df * (1.0 - lb) * sig_f * one_m).astype(BF16)
        dp_ref[:, 2 * D_MODEL : 3 * D_MODEL] = d_v[...].astype(BF16)
        dlb_ref[...] += jnp.sum(df * one_m, axis=0, keepdims=True)

    def col_spec(off):
        return pl.BlockSpec((rows, D_MODEL), lambda s: (n_steps - 1 - s, off // D_MODEL))

    f32_tile = pltpu.VMEM((rows, D_MODEL), F32)
    bf_tile = pltpu.VMEM((rows, D_MODEL), BF16)
    scratch = [pltpu.VMEM((HEADS, HEAD_DIM, HEAD_DIM), F32)] + [bf_tile] * 6 + [f32_tile] * 5
    scratch += [pltpu.VMEM((n_chunks, D_MODEL), F32)]
    return pl.pallas_call(
        body,
        name="hg_bwd",
        grid=(n_steps,),
        in_specs=[
            col_spec(OFF_HG_Q), col_spec(OFF_HG_F), col_spec(OFF_HG_I),
            pl.BlockSpec((2, D_MODEL), lambda s: (0, 0)),
            pl.BlockSpec((n_chunks, HEADS, HEAD_DIM, HEAD_DIM), lambda s: (n_steps - 1 - s, 0, 0, 0)),
            pl.BlockSpec((rows, D_MODEL), lambda s: (n_steps - 1 - s, 0)),
        ],
        out_specs=[
            pl.BlockSpec((rows, 3 * D_MODEL), lambda s: (n_steps - 1 - s, 0)),
            pl.BlockSpec((1, D_MODEL), lambda s: (0, 0)),
        ],
        out_shape=[
            jax.ShapeDtypeStruct((s_len, 3 * D_MODEL), BF16),
            jax.ShapeDtypeStruct((1, D_MODEL), F32),
        ],
        scratch_shapes=scratch,
        compiler_params=_cparams(("arbitrary",)),
    )(proj, proj, proj, lbl, states, d_o)


def _mid(proj, sb_o, hg_o, x, target, b_gate, hg_gain, final_g, w_sb, w_hg, w_out):
    s_len = proj.shape[0]
    ts = min(128, s_len)
    inv_d = 1.0 / D_MODEL

    def body(zsb_ref, hz_ref, gl_ref, sbo_ref, hgo_ref, x_ref, tgt_ref, bg_ref, hgn_ref, fg_ref,
             wsb_ref, whg_ref, wout_ref,
             dout_ref, dsbo_ref, dhgo_ref, dzsb_ref, dhz_ref, dgl_ref,
             asb_ref, dusb_ref, ahg_ref, duhg_ref, y_ref, doutb_ref,
             loss_ref, dfg_ref, dbg_ref, dhgn_ref):
        @pl.when(pl.program_id(0) == 0)
        def _():
            loss_ref[...] = jnp.zeros_like(loss_ref)
            dfg_ref[...] = jnp.zeros_like(dfg_ref)
            dbg_ref[...] = jnp.zeros_like(dbg_ref)
            dhgn_ref[...] = jnp.zeros_like(dhgn_ref)

        z_sb = zsb_ref[...]
        sb_o = sbo_ref[...]
        sig_zsb = _sigmoid(z_sb)
        silu_zsb = z_sb * sig_zsb
        a_sb = (sb_o * silu_zsb).astype(BF16)
        u_sb = _dot(a_sb, wsb_ref[...])

        hg_o = hgo_ref[...]
        gain = hgn_ref[...]
        r_parts, yn_parts = [], []
        for h in range(HEADS):
            oh = hg_o[:, h * HEAD_DIM : (h + 1) * HEAD_DIM]
            r = lax.rsqrt(jnp.mean(oh * oh, axis=-1, keepdims=True) + RMS_EPS)
            r_parts.append(jnp.broadcast_to(r, oh.shape))
            yn_parts.append(oh * r)
        r_hg = jnp.concatenate(r_parts, axis=-1)
        yn_hg = jnp.concatenate(yn_parts, axis=-1)
        hn = yn_hg * gain
        hz = hz_ref[...]
        sig_hz = _sigmoid(hz)
        silu_hz = hz * sig_hz
        a_hg = (hn * silu_hz).astype(BF16)
        u_hg = _dot(a_hg, whg_ref[...])

        gates = _sigmoid(gl_ref[...] + bg_ref[...])
        g_sb = gates[:, 0:D_MODEL]
        g_hg = gates[:, D_MODEL:]
        y = (g_sb * u_sb + g_hg * u_hg).astype(BF16)
        out = x_ref[...] + _dot(y, wout_ref[...])
        r2 = lax.rsqrt(jnp.mean(out * out, axis=-1, keepdims=True) + RMS_EPS)
        yn = out * r2
        fg = fg_ref[...]
        diff = yn * fg - tgt_ref[...]
        loss_ref[...] += 0.5 * inv_d * jnp.sum(diff * diff)

        dyf = diff * inv_d
        dfg_ref[...] += jnp.sum(dyf * yn, axis=0, keepdims=True)
        dyn = dyf * fg
        dout = r2 * (dyn - yn * jnp.mean(dyn * yn, axis=-1, keepdims=True))
        dout_ref[...] = dout
        doutb = dout.astype(BF16)
        doutb_ref[...] = doutb
        dy = _dot_nt(doutb, wout_ref[...])
        du_sb = (dy * g_sb).astype(BF16)
        du_hg = (dy * g_hg).astype(BF16)
        dgl_sb = dy * u_sb * g_sb * (1.0 - g_sb)
        dgl_hg = dy * u_hg * g_hg * (1.0 - g_hg)
        dgl_ref[:, 0:D_MODEL] = dgl_sb.astype(BF16)
        dgl_ref[:, D_MODEL:] = dgl_hg.astype(BF16)
        dbg_ref[:, 0:D_MODEL] += jnp.sum(dgl_sb, axis=0, keepdims=True)
        dbg_ref[:, D_MODEL:] += jnp.sum(dgl_hg, axis=0, keepdims=True)

        da_sb = _dot_nt(du_sb, wsb_ref[...])
        dsbo_ref[...] = da_sb * silu_zsb
        dzsb_ref[...] = (da_sb * sb_o * (sig_zsb * (1.0 + z_sb * (1.0 - sig_zsb)))).astype(BF16)

        da_hg = _dot_nt(du_hg, whg_ref[...])
        dhn = da_hg * silu_hz
        dhz_ref[...] = (da_hg * hn * (sig_hz * (1.0 + hz * (1.0 - sig_hz)))).astype(BF16)
        dhgn_ref[...] += jnp.sum(dhn * yn_hg, axis=0, keepdims=True)
        dyn_hg = dhn * gain
        prod = dyn_hg * yn_hg
        m_parts = []
        for h in range(HEADS):
            ph = prod[:, h * HEAD_DIM : (h + 1) * HEAD_DIM]
            m_parts.append(jnp.broadcast_to(jnp.mean(ph, axis=-1, keepdims=True), ph.shape))
        dhgo_ref[...] = r_hg * (dyn_hg - yn_hg * jnp.concatenate(m_parts, axis=-1))

        asb_ref[...] = a_sb
        dusb_ref[...] = du_sb
        ahg_ref[...] = a_hg
        duhg_ref[...] = du_hg
        y_ref[...] = y

    def tile(width, off=0):
        return pl.BlockSpec((ts, width), lambda s: (s, off // width))

    def whole(shape):
        return pl.BlockSpec(shape, lambda s: (0,) * len(shape))

    sq = (D_MODEL, D_MODEL)
    f32_act = jax.ShapeDtypeStruct((s_len, D_MODEL), F32)
    bf_act = jax.ShapeDtypeStruct((s_len, D_MODEL), BF16)
    return pl.pallas_call(
        body,
        name="mid",
        grid=(s_len // ts,),
        in_specs=[
            tile(D_MODEL, OFF_SB_Z), tile(D_MODEL, OFF_HG_Z), tile(2 * D_MODEL, OFF_GATE),
            tile(D_MODEL), tile(D_MODEL), tile(D_MODEL), tile(D_MODEL),
            whole((1, 2 * D_MODEL)), whole((1, D_MODEL)), whole((1, D_MODEL)),
            whole(sq), whole(sq), whole(sq),
        ],
        out_specs=[
            tile(D_MODEL), tile(D_MODEL), tile(D_MODEL), tile(D_MODEL), tile(D_MODEL), tile(2 * D_MODEL),
            tile(D_MODEL), tile(D_MODEL), tile(D_MODEL), tile(D_MODEL), tile(D_MODEL), tile(D_MODEL),
            whole((1, 1)), whole((1, D_MODEL)), whole((1, 2 * D_MODEL)), whole((1, D_MODEL)),
        ],
        out_shape=[
            f32_act, f32_act, f32_act, bf_act, bf_act, jax.ShapeDtypeStruct((s_len, 2 * D_MODEL), BF16),
            bf_act, bf_act, bf_act, bf_act, bf_act, bf_act,
            jax.ShapeDtypeStruct((1, 1), F32), jax.ShapeDtypeStruct((1, D_MODEL), F32),
            jax.ShapeDtypeStruct((1, 2 * D_MODEL), F32), jax.ShapeDtypeStruct((1, D_MODEL), F32),
        ],
        compiler_params=_cparams(("arbitrary",)),
    )(proj, proj, proj, sb_o, hg_o, x, target, b_gate, hg_gain, final_g, w_sb, w_hg, w_out)


def _grad_matmul(a, b, name, tn):
    s_len, m = a.shape
    n = b.shape[1]
    tk = min(512, s_len)

    def body(a_ref, b_ref, o_ref):
        @pl.when(pl.program_id(1) == 0)
        def _():
            o_ref[...] = jnp.zeros_like(o_ref)

        o_ref[...] += _dot_tn(a_ref[...], b_ref[...])

    return pl.pallas_call(
        body,
        name=name,
        grid=(n // tn, s_len // tk),
        in_specs=[pl.BlockSpec((tk, m), lambda j, k: (k, 0)), pl.BlockSpec((tk, tn), lambda j, k: (k, j))],
        out_specs=pl.BlockSpec((m, tn), lambda j, k: (0, j)),
        out_shape=jax.ShapeDtypeStruct((m, n), F32),
        compiler_params=_cparams(("arbitrary", "arbitrary")),
    )(a, b)


def _dx(dproj, w4, x, norm_g, dout):
    s_len = x.shape[0]
    ts = min(1024, s_len)
    tk = 1280
    per = W_IN_SHARD // tk
    nk = IN_WIDTH // tk

    def body(dp_ref, w_ref, x_ref, g_ref, dout_ref, gx_ref, dg_ref, acc):
        k = pl.program_id(1)

        @pl.when((pl.program_id(0) == 0) & (k == 0))
        def _():
            dg_ref[...] = jnp.zeros_like(dg_ref)

        part = _dot_nt(dp_ref[...], w_ref[0])

        @pl.when(k == 0)
        def _():
            acc[...] = part

        @pl.when(k > 0)
        def _():
            acc[...] += part

        @pl.when(k == nk - 1)
        def _():
            dh = acc[...]
            xv = x_ref[...]
            r = lax.rsqrt(jnp.mean(xv * xv, axis=-1, keepdims=True) + RMS_EPS)
            xn = xv * r
            dg_ref[...] += jnp.sum(dh * xn, axis=0, keepdims=True)
            dxn = dh * g_ref[...]
            gx_ref[...] = r * (dxn - xn * jnp.mean(dxn * xn, axis=-1, keepdims=True)) + dout_ref[...]

    row_tile = pl.BlockSpec((ts, D_MODEL), lambda s, k: (s, 0))
    vec = pl.BlockSpec((1, D_MODEL), lambda s, k: (0, 0))
    return pl.pallas_call(
        body,
        name="dx",
        grid=(s_len // ts, nk),
        in_specs=[
            pl.BlockSpec((ts, tk), lambda s, k: (s, k)),
            pl.BlockSpec((1, D_MODEL, tk), lambda s, k: (k // per, 0, k % per)),
            row_tile, vec, row_tile,
        ],
        out_specs=[row_tile, vec],
        out_shape=[jax.ShapeDtypeStruct((s_len, D_MODEL), F32), jax.ShapeDtypeStruct((1, D_MODEL), F32)],
        scratch_shapes=[pltpu.VMEM((ts, D_MODEL), F32)],
        compiler_params=_cparams(("arbitrary", "arbitrary")),
    )(dproj, w4, x, norm_g, dout)


def _local_step(x, target, norm_g, b_gate, lbl, hg_gain, final_g, w4, w_sb, w_hg, w_out):
    proj, h = _inproj(x, norm_g, w4)
    sb_o, sb_o_fine = _sb_fwd(proj)
    hg_o, states = _hg_fwd(proj, lbl)
    (dout, d_sbo, d_hgo, d_zsb, d_hz, d_gl, a_sb, du_sb, a_hg, du_hg, y, doutb,
     loss, d_fg, d_bg, d_hgn) = _mid(proj, sb_o, hg_o, x, target, b_gate, hg_gain, final_g, w_sb, w_hg, w_out)
    g_w_sb = _grad_matmul(a_sb, du_sb, "grad_w_sb", 512)
    g_w_hg = _grad_matmul(a_hg, du_hg, "grad_w_hg", 512)
    g_w_out = _grad_matmul(y, doutb, "grad_w_out", 512)
    d_q, d_k, d_v = _sb_bwd(proj, sb_o_fine, d_sbo)
    d_hg, d_lb = _hg_bwd(proj, lbl, states, d_hgo)
    dproj = jnp.concatenate([d_q, d_k, d_v, d_zsb, d_hg, d_hz, d_gl], axis=1)
    g_w_in = _grad_matmul(h, dproj, "grad_w_in", 512)
    grad_x, d_ng = _dx(dproj, w4, x, norm_g, dout)
    return grad_x, g_w_in, g_w_sb, g_w_hg, g_w_out, loss, d_ng, d_bg, d_lb, d_hgn, d_fg


ANY = pl.BlockSpec(memory_space=pl.ANY)
HALF_IN = D_MODEL // 2
HALF_SQ = ROW_SHARD // 2


def _position():
    x, y, c = lax.axis_index("x"), lax.axis_index("y"), lax.axis_index("c")
    chips = [(1 - x, y), (x, 1 - y), (1 - x, 1 - y)]
    return x, y, c, chips


def _remote(src, dst, send_sem, recv_sem, to):
    return pltpu.make_async_remote_copy(src_ref=src, dst_ref=dst, send_sem=send_sem, recv_sem=recv_sem,
                                        device_id=to, device_id_type=MESH)


def _gather_weights(w_in_b, w_sq_b):
    n_in = 4
    n_piece = n_in + 3
    rows = HALF_IN // n_in

    def body(win_ref, wsq_ref, in_ref, sq_ref, send_sems, recv_sems, local_sems):
        x, y, c, chips = _position()
        me = 2 * x + y
        sibling = (x, y, 1 - c)

        def src_piece(p):
            if p < n_in:
                return win_ref.at[pl.ds(c * HALF_IN + p * rows, rows), :]
            return wsq_ref.at[p - n_in, pl.ds(c * HALF_SQ, HALF_SQ), :]

        def piece(p, chip, core):
            if p < n_in:
                return in_ref.at[chip, pl.ds(core * HALF_IN + p * rows, rows), :]
            return sq_ref.at[p - n_in, chip, pl.ds(core * HALF_SQ, HALF_SQ), :]

        own = [pltpu.make_async_copy(win_ref, in_ref.at[me], local_sems.at[0]),
               pltpu.make_async_copy(wsq_ref, sq_ref.at[:, me], local_sems.at[1])]
        for cp in own:
            cp.start()
        sends = []
        for k, (px, py) in enumerate(chips):
            for p in range(n_piece):
                sends.append(_remote(src_piece(p), piece(p, me, c), send_sems.at[k, p], recv_sems.at[k, p], (px, py, c)))
        for cp in sends:
            cp.start()
        for k, (px, py) in enumerate(chips):
            chip = 2 * px + py
            for p in range(n_piece):
                got = piece(p, chip, c)
                _remote(got, got, send_sems.at[k, p], recv_sems.at[k, p], (px, py, c)).wait_recv()
                fwd = _remote(got, got, send_sems.at[3 + k, p], recv_sems.at[3 + k, p], sibling)
                fwd.start()
                sends.append(fwd)
        for k, (px, py) in enumerate(chips):
            chip = 2 * px + py
            for p in range(n_piece):
                got = piece(p, chip, 1 - c)
                _remote(got, got, send_sems.at[3 + k, p], recv_sems.at[3 + k, p], sibling).wait_recv()
        for cp in sends:
            cp.wait_send()
        for cp in own:
            cp.wait()

    return pl.pallas_call(
        body,
        name="gather_weights",
        in_specs=[ANY, ANY],
        out_specs=[ANY, ANY],
        out_shape=[jax.ShapeDtypeStruct((N_CHIPS, D_MODEL, W_IN_SHARD), BF16),
                   jax.ShapeDtypeStruct((3, N_CHIPS, ROW_SHARD, D_MODEL), BF16)],
        scratch_shapes=[pltpu.SemaphoreType.DMA((6, n_piece)), pltpu.SemaphoreType.DMA((6, n_piece)),
                        pltpu.SemaphoreType.DMA((2,))],
    )(w_in_b, w_sq_b)


def _swap_halves(g_in, g_sq):
    n_in = 16
    n_piece = n_in + 3 * N_CHIPS
    rows = HALF_IN // n_in

    def body(gin_ref, gsq_ref, mine_in, mine_sq, got_in, got_sq, send_sems, recv_sems, local_sems):
        x, y, c, _ = _position()
        sibling = (x, y, 1 - c)

        def src_piece(p, core):
            if p < n_in:
                return gin_ref.at[pl.ds(core * HALF_IN + p * rows, rows), :]
            a, chip = divmod(p - n_in, N_CHIPS)
            return gsq_ref.at[a, chip, pl.ds(core * HALF_SQ, HALF_SQ), :]

        def dst_piece(p, ref_in, ref_sq):
            if p < n_in:
                return ref_in.at[pl.ds(p * rows, rows), :]
            a, chip = divmod(p - n_in, N_CHIPS)
            return ref_sq.at[a, chip]

        own = [pltpu.make_async_copy(src_piece(p, c), dst_piece(p, mine_in, mine_sq), local_sems.at[p])
               for p in range(n_piece)]
        out = [_remote(src_piece(p, 1 - c), dst_piece(p, got_in, got_sq), send_sems.at[p], recv_sems.at[p], sibling)
               for p in range(n_piece)]
        for cp in out + own:
            cp.start()
        for cp in out + own:
            cp.wait()

    half_in = jax.ShapeDtypeStruct((HALF_IN, IN_WIDTH), F32)
    half_sq = jax.ShapeDtypeStruct((3, N_CHIPS, HALF_SQ, D_MODEL), F32)
    return pl.pallas_call(
        body,
        name="swap_halves",
        in_specs=[ANY, ANY],
        out_specs=[ANY] * 4,
        out_shape=[half_in, half_sq, half_in, half_sq],
        scratch_shapes=[pltpu.SemaphoreType.DMA((n_piece,))] * 3,
    )(g_in, g_sq)


def _exchange_chunks(s_in, s_sq):
    n_in = 8
    n_piece = n_in + 3
    rows = HALF_IN // n_in

    def body(sin_ref, ssq_ref, mine_in, mine_sq, got_in, got_sq, send_sems, recv_sems, local_sems):
        x, y, c, chips = _position()
        me = 2 * x + y

        def src_piece(p, chip):
            if p < n_in:
                return sin_ref.at[chip, pl.ds(p * rows, rows), :]
            return ssq_ref.at[p - n_in, chip]

        def dst_piece(p, ref_in, ref_sq):
            if p < n_in:
                return ref_in.at[pl.ds(p * rows, rows), :]
            return ref_sq.at[p - n_in]

        own = [pltpu.make_async_copy(sin_ref.at[me], mine_in, local_sems.at[0]),
               pltpu.make_async_copy(ssq_ref.at[:, me], mine_sq, local_sems.at[1])]
        out = []
        for k, (px, py) in enumerate(chips):
            chip = 2 * px + py
            for p in range(n_piece):
                out.append(_remote(src_piece(p, chip), dst_piece(p, got_in.at[k], got_sq.at[k]),
                                   send_sems.at[k, p], recv_sems.at[k, p], (px, py, c)))
        for cp in out + own:
            cp.start()
        for cp in out + own:
            cp.wait()

    return pl.pallas_call(
        body,
        name="exchange_chunks",
        in_specs=[ANY, ANY],
        out_specs=[ANY] * 4,
        out_shape=[jax.ShapeDtypeStruct((HALF_IN, W_IN_SHARD), F32),
                   jax.ShapeDtypeStruct((3, HALF_SQ, D_MODEL), F32),
                   jax.ShapeDtypeStruct((3, HALF_IN, W_IN_SHARD), F32),
                   jax.ShapeDtypeStruct((3, 3, HALF_SQ, D_MODEL), F32)],
        scratch_shapes=[pltpu.SemaphoreType.DMA((3, n_piece)), pltpu.SemaphoreType.DMA((3, n_piece)),
                        pltpu.SemaphoreType.DMA((2,))],
    )(s_in, s_sq)


def _join_halves(r_in, r_sq):
    n_in = 16
    n_piece = n_in + 3
    rows = HALF_IN // n_in

    def body(rin_ref, rsq_ref, full_in, full_sq, send_sems, recv_sems, local_sems):
        x, y, c, _ = _position()
        sibling = (x, y, 1 - c)

        def src_piece(p):
            if p < n_in:
                return rin_ref.at[pl.ds(p * rows, rows), :]
            return rsq_ref.at[p - n_in]

        def dst_piece(p, core):
            if p < n_in:
                return full_in.at[pl.ds(core * HALF_IN + p * rows, rows), :]
            return full_sq.at[p - n_in, pl.ds(core * HALF_SQ, HALF_SQ), :]

        own = [pltpu.make_async_copy(rin_ref, full_in.at[pl.ds(c * HALF_IN, HALF_IN), :], local_sems.at[0]),
               pltpu.make_async_copy(rsq_ref, full_sq.at[:, pl.ds(c * HALF_SQ, HALF_SQ), :], local_sems.at[1])]
        out = [_remote(src_piece(p), dst_piece(p, c), send_sems.at[p], recv_sems.at[p], sibling)
               for p in range(n_piece)]
        for cp in out + own:
            cp.start()
        for p in range(n_piece):
            _remote(src_piece(p), dst_piece(p, 1 - c), send_sems.at[p], recv_sems.at[p], sibling).wait_recv()
        for cp in out:
            cp.wait_send()
        for cp in own:
            cp.wait()

    return pl.pallas_call(
        body,
        name="join_halves",
        in_specs=[ANY, ANY],
        out_specs=[ANY, ANY],
        out_shape=[jax.ShapeDtypeStruct((D_MODEL, W_IN_SHARD), F32),
                   jax.ShapeDtypeStruct((3, ROW_SHARD, D_MODEL), F32)],
        scratch_shapes=[pltpu.SemaphoreType.DMA((n_piece,)), pltpu.SemaphoreType.DMA((n_piece,)),
                        pltpu.SemaphoreType.DMA((2,))],
    )(r_in, r_sq)


SMALL_ROWS = 56
N_DEV = 8


def _sum_small(part):
    def body(part_ref, out_ref, slots, send_sems, recv_sems):
        x, y, c, _ = _position()
        me = 4 * x + 2 * y + c
        slots[me] = part_ref[...]
        out = []
        for r in range(1, N_DEV):
            rx, ry, rc = (r >> 2) & 1, (r >> 1) & 1, r & 1
            to = (1 - x if rx else x, 1 - y if ry else y, 1 - c if rc else c)
            out.append(_remote(part_ref, slots.at[me], send_sems.at[r - 1], recv_sems.at[r - 1], to))
        for cp in out:
            cp.start()
        for r in range(1, N_DEV):
            _remote(part_ref, slots.at[me ^ r], send_sems.at[r - 1], recv_sems.at[r - 1], (x, y, c)).wait_recv()
        for cp in out:
            cp.wait_send()
        total = slots[0]
        for d in range(1, N_DEV):
            total = total + slots[d]
        out_ref[...] = total

    vmem = pl.BlockSpec(memory_space=pltpu.VMEM)
    return pl.pallas_call(
        body,
        name="sum_small",
        in_specs=[vmem],
        out_specs=vmem,
        out_shape=jax.ShapeDtypeStruct((SMALL_ROWS, HEAD_DIM), F32),
        scratch_shapes=[pltpu.VMEM((N_DEV, SMALL_ROWS, HEAD_DIM), F32),
                        pltpu.SemaphoreType.DMA((N_DEV - 1,)), pltpu.SemaphoreType.DMA((N_DEV - 1,))],
    )(part)


def _add_by_chunk(a, b):
    tr = 128

    def body(a_ref, b_ref, o_ref):
        o_ref[0] = a_ref[...] + b_ref[...]

    spec = pl.BlockSpec((tr, W_IN_SHARD), lambda j, r: (r, j))
    return pl.pallas_call(
        body,
        name="add_by_chunk",
        grid=(N_CHIPS, HALF_IN // tr),
        in_specs=[spec, spec],
        out_specs=pl.BlockSpec((1, tr, W_IN_SHARD), lambda j, r: (j, r, 0)),
        out_shape=jax.ShapeDtypeStruct((N_CHIPS, HALF_IN, W_IN_SHARD), F32),
        compiler_params=_cparams(("arbitrary", "arbitrary")),
    )(a, b)


def _add_rows(first, rest, name):
    rows, cols = first.shape
    n = rest.shape[0]
    tr = min(128, rows)

    def body(a_ref, b_ref, o_ref):
        acc = a_ref[...]
        for i in range(n):
            acc = acc + b_ref[i]
        o_ref[...] = acc

    return pl.pallas_call(
        body,
        name=name,
        grid=(rows // tr,),
        in_specs=[pl.BlockSpec((tr, cols), lambda r: (r, 0)), pl.BlockSpec((n, tr, cols), lambda r: (0, r, 0))],
        out_specs=pl.BlockSpec((tr, cols), lambda r: (r, 0)),
        out_shape=jax.ShapeDtypeStruct((rows, cols), F32),
        compiler_params=_cparams(("arbitrary",)),
    )(first, rest)


def _adamw_math(w, g, m, v):
    m = ADAM_B1 * m + (1.0 - ADAM_B1) * g
    v = ADAM_B2 * v + (1.0 - ADAM_B2) * (g * g)
    m_hat = m / (1.0 - ADAM_B1 ** ADAM_STEP)
    v_hat = v / (1.0 - ADAM_B2 ** ADAM_STEP)
    delta = -ADAM_LR * (m_hat / (jnp.sqrt(v_hat) + ADAM_EPS) + ADAM_WD * w)
    return delta, m, v


def _adamw(w, g, m, v, name):
    rows, cols = w.shape
    tr = min(128, rows)

    def body(w_ref, g_ref, m_ref, v_ref, d_ref, nm_ref, nv_ref):
        d_ref[...], nm_ref[...], nv_ref[...] = _adamw_math(w_ref[...], g_ref[...], m_ref[...], v_ref[...])

    spec = pl.BlockSpec((tr, cols), lambda r: (r, 0))
    return pl.pallas_call(
        body,
        name=name,
        grid=(rows // tr,),
        in_specs=[spec] * 4,
        out_specs=[spec] * 3,
        out_shape=[jax.ShapeDtypeStruct((rows, cols), F32)] * 3,
        compiler_params=_cparams(("arbitrary",)),
    )(w, g, m, v)


def _adamw_small(sums, w, m, v):
    def body(s_ref, w_ref, m_ref, v_ref, loss_ref, g_ref, d_ref, nm_ref, nv_ref):
        s = s_ref[...]
        w = w_ref[...]
        loss_ref[...] = s[0:1, 0:1]
        l0, l1 = w[24:32], w[32:40]
        mx = jnp.maximum(l0, l1)
        e0, e1 = jnp.exp(l0 - mx), jnp.exp(l1 - mx)
        p0, p1 = e0 / (e0 + e1), e1 / (e0 + e1)
        d_lb = s[32:40]
        g = jnp.concatenate([s[8:16], s[16:32], d_lb * p0 * (1.0 - p0), -d_lb * p0 * p1, s[40:48], s[48:56]], axis=0)
        g_ref[...] = g
        d_ref[...], nm_ref[...], nv_ref[...] = _adamw_math(w, g, m_ref[...], v_ref[...])

    packed = jax.ShapeDtypeStruct((SMALL_ROWS, HEAD_DIM), F32)
    return pl.pallas_call(
        body,
        name="adamw_small",
        out_shape=[jax.ShapeDtypeStruct((1, 1), F32), packed, packed, packed, packed],
    )(sums, w, m, v)


def _pack_small(ng, bg, lbl, hgn, fg):
    return jnp.concatenate([a.reshape(-1, HEAD_DIM) for a in (ng, bg, lbl, hgn, fg)], axis=0)


def _unpack_small(p):
    return (p[0:8].reshape(1, D_MODEL), p[8:24].reshape(1, 2 * D_MODEL), p[24:40].reshape(2, HEADS, HEAD_DIM),
            p[40:48].reshape(1, HEADS, HEAD_DIM), p[48:56].reshape(D_MODEL))


def kernel(x, norm_g, w_in, b_gate, lb_logits, hg_norm_g, w_sb_proj, w_hg_proj, w_out, final_norm_g, loss_target, m_norm_g, m_w_in, m_b_gate, m_lb_logits, m_hg_norm_g, m_w_sb_proj, m_w_hg_proj, m_w_out, m_final_norm_g, v_norm_g, v_w_in, v_b_gate, v_lb_logits, v_hg_norm_g, v_w_sb_proj, v_w_hg_proj, v_w_out, v_final_norm_g):
    s_len = x.shape[1]
    w_sq = jnp.stack([w_sb_proj[0], w_hg_proj[0], w_out[0]])
    w4, wsq = _gather_weights(w_in[0].astype(BF16), w_sq.astype(BF16))
    wsq = wsq.reshape(3, D_MODEL, D_MODEL)

    (grad_x, g_in, g_sb, g_hg, g_out, loss, d_ng, d_bg, d_lb, d_hgn, d_fg) = _local_step(
        x[0], loss_target[0], norm_g, b_gate, lb_logits.reshape(2, D_MODEL), hg_norm_g.reshape(1, D_MODEL),
        final_norm_g.reshape(1, D_MODEL), w4, wsq[0], wsq[1], wsq[2])

    g_sq = jnp.stack([g_sb, g_hg, g_out]).reshape(3, N_CHIPS, ROW_SHARD, D_MODEL)
    mine_in, mine_sq, got_in, got_sq = _swap_halves(g_in, g_sq)
    s_in = _add_by_chunk(mine_in, got_in)
    sq_cols = N_CHIPS * HALF_SQ
    s_sq = _add_rows(mine_sq.reshape(3 * sq_cols, D_MODEL), got_sq.reshape(1, 3 * sq_cols, D_MODEL), "add_sq_a")
    mine_in, mine_sq, got_in, got_sq = _exchange_chunks(s_in, s_sq.reshape(3, N_CHIPS, HALF_SQ, D_MODEL))
    r_in = _add_rows(mine_in, got_in, "add_in_b")
    r_sq = _add_rows(mine_sq.reshape(3 * HALF_SQ, D_MODEL), got_sq.reshape(3, 3 * HALF_SQ, D_MODEL), "add_sq_b")
    grad_in, grad_sq = _join_halves(r_in, r_sq.reshape(3, HALF_SQ, D_MODEL))

    d_in, nm_in, nv_in = _adamw(w_in[0], grad_in, m_w_in[0], v_w_in[0], "adamw_in")
    flat = lambda a, b, c: jnp.concatenate([a[0], b[0], c[0]], axis=0)
    d_sq, nm_sq, nv_sq = _adamw(flat(w_sb_proj, w_hg_proj, w_out), grad_sq.reshape(3 * ROW_SHARD, D_MODEL),
                                flat(m_w_sb_proj, m_w_hg_proj, m_w_out), flat(v_w_sb_proj, v_w_hg_proj, v_w_out),
                                "adamw_sq")

    pad = jnp.zeros((8, HEAD_DIM), F32).at[0, 0].set(loss[0, 0])
    part = jnp.concatenate([pad] + [a.reshape(-1, HEAD_DIM) for a in (d_ng, d_bg, d_lb, d_hgn, d_fg)], axis=0)
    sums = _sum_small(part)
    loss_out, g_sm, d_sm, nm_sm, nv_sm = _adamw_small(
        sums, _pack_small(norm_g, b_gate, lb_logits, hg_norm_g, final_norm_g),
        _pack_small(m_norm_g, m_b_gate, m_lb_logits, m_hg_norm_g, m_final_norm_g),
        _pack_small(v_norm_g, v_b_gate, v_lb_logits, v_hg_norm_g, v_final_norm_g))

    def big(t_in, t_sq):
        sq = t_sq.reshape(3, 1, ROW_SHARD, D_MODEL)
        return t_in[None], sq[0], sq[1], sq[2]

    def order(small, in_, sb, hg, out):
        ng, bg, lbl, hgn, fg = small
        return [ng, in_, bg, lbl, hgn, sb, hg, out, fg]

    outs = [loss_out[0, 0], grad_x[None]]
    for small, (t_in, t_sq) in ((g_sm, (grad_in, grad_sq)), (d_sm, (d_in, d_sq)), (nm_sm, (nm_in, nm_sq)), (nv_sm, (nv_in, nv_sq))):
        outs += order(_unpack_small(small), *big(t_in, t_sq))
    return tuple(outs)
```

```python
import functools

import jax
import jax.numpy as jnp
from jax import lax
from jax.experimental import pallas as pl
from jax.experimental.pallas import tpu as pltpu

F32 = jnp.float32
BF16 = jnp.bfloat16

D_MODEL = 1024
HEADS = 8
HEAD_DIM = 128
IN_WIDTH = 10240
N_CHIPS = 4
W_IN_SHARD = IN_WIDTH // N_CHIPS
ROW_SHARD = D_MODEL // N_CHIPS
RMS_EPS = 1e-6

OFF_SB_Q, OFF_SB_K, OFF_SB_V, OFF_SB_Z = 0, 1024, 2048, 3072
OFF_HG_Q, OFF_HG_F, OFF_HG_I, OFF_HG_Z, OFF_GATE = 4096, 5120, 6144, 7168, 8192

SB_BLOCK = 256
SB_DEAD = -110.0
HG_CHUNK = 32
HG_STEP = 256
HG_MID = HG_CHUNK // 2 - 1

ADAM_LR, ADAM_B1, ADAM_B2, ADAM_EPS, ADAM_WD, ADAM_STEP = 0.001, 0.9, 0.999, 1e-08, 0.01, 10

VMEM_LIMIT = 56 * 1024 * 1024

MESH = pl.DeviceIdType.MESH


def _cparams(sem, vmem=VMEM_LIMIT):
    return pltpu.CompilerParams(dimension_semantics=sem, vmem_limit_bytes=vmem)


def _dot(a, b):
    return jnp.dot(a, b, preferred_element_type=F32)


def _dot_nt(a, b):
    return lax.dot_general(a, b, (((1,), (1,)), ((), ())), preferred_element_type=F32)


def _dot_tn(a, b):
    return lax.dot_general(a, b, (((0,), (0,)), ((), ())), preferred_element_type=F32)


def _split_dot(x, tri):
    hi = x.astype(BF16)
    lo = (x - hi.astype(F32)).astype(BF16)
    return _dot(hi, tri) + _dot(lo, tri)


def _split_dot_left(tri, x):
    hi = x.astype(BF16)
    lo = (x - hi.astype(F32)).astype(BF16)
    return _dot(tri, hi) + _dot(tri, lo)


def _sigmoid(x):
    return 1.0 / (1.0 + jnp.exp(-x))


def _inproj(x, norm_g, w4):
    s_len = x.shape[0]
    ts = min(1024, s_len)
    tn = 1280
    per = W_IN_SHARD // tn

    def body(x_ref, g_ref, w_ref, proj_ref, h_ref):
        @pl.when(pl.program_id(1) == 0)
        def _():
            xv = x_ref[...]
            r = lax.rsqrt(jnp.mean(xv * xv, axis=-1, keepdims=True) + RMS_EPS)
            h_ref[...] = ((xv * r) * g_ref[...]).astype(BF16)

        proj_ref[...] = _dot(h_ref[...], w_ref[0])

    return pl.pallas_call(
        body,
        name="inproj",
        grid=(s_len // ts, IN_WIDTH // tn),
        in_specs=[
            pl.BlockSpec((ts, D_MODEL), lambda s, n: (s, 0)),
            pl.BlockSpec((1, D_MODEL), lambda s, n: (0, 0)),
            pl.BlockSpec((1, D_MODEL, tn), lambda s, n: (n // per, 0, n % per)),
        ],
        out_specs=[
            pl.BlockSpec((ts, tn), lambda s, n: (s, n)),
            pl.BlockSpec((ts, D_MODEL), lambda s, n: (s, 0)),
        ],
        out_shape=[
            jax.ShapeDtypeStruct((s_len, IN_WIDTH), F32),
            jax.ShapeDtypeStruct((s_len, D_MODEL), BF16),
        ],
        compiler_params=_cparams(("arbitrary", "arbitrary")),
    )(x, norm_g, w4)


def _sb_tile_fwd(qb, kb, row_gt_col, tri_excl, carry, diag):
    scale = HEAD_DIM ** -0.5
    z = _dot_nt(qb, kb) * scale
    ls_pos = jnp.minimum(z, 0.0) - jnp.log1p(jnp.exp(-jnp.abs(z)))
    log_not = ls_pos - z
    log_not_m = jnp.where(row_gt_col, log_not, 0.0) if diag else log_not
    surv = _split_dot(log_not_m, tri_excl) + carry
    w = jnp.exp(ls_pos + surv)
    if diag:
        w = jnp.where(row_gt_col, w, 0.0)
    return ls_pos, log_not, log_not_m, surv, w


def _sb_fwd(proj):
    s_len = proj.shape[0]
    blk = min(SB_BLOCK, s_len)
    nq = s_len // blk

    def body(q_ref, k_ref, v_ref, o_ref, of_ref):
        i = pl.program_id(1)
        qb = q_ref[...].astype(BF16)
        row = lax.broadcasted_iota(jnp.int32, (blk, blk), 0)
        col = lax.broadcasted_iota(jnp.int32, (blk, blk), 1)
        row_gt_col = row > col
        tri_excl = row_gt_col.astype(BF16)

        def tile(j, carry, acc, acc_lo, diag):
            start = pl.multiple_of(j * blk, blk)
            kb = k_ref[pl.ds(start, blk), :].astype(BF16)
            vb = v_ref[pl.ds(start, blk), :].astype(BF16)
            _, _, log_not_m, surv, w = _sb_tile_fwd(qb, kb, row_gt_col, tri_excl, carry, diag)
            wb = w.astype(BF16)
            w_lo = (w - wb.astype(F32)).astype(BF16)
            acc = acc + _dot(wb, vb)
            acc_lo = acc_lo + _dot(w_lo, vb)
            carry = surv[:, 0:1] + log_not_m[:, 0:1]
            return carry, acc, acc_lo

        zero = jnp.zeros((blk, HEAD_DIM), F32)
        carry, acc, acc_lo = tile(i, jnp.zeros((blk, 1), F32), zero, zero, True)

        def more(st):
            return (st[0] < i) & (jnp.max(st[1]) > SB_DEAD)

        def step(st):
            return (st[0] + 1,) + tile(i - 1 - st[0], *st[1:], False)

        _, carry, acc, acc_lo = lax.while_loop(more, step, (0, carry, acc, acc_lo))
        o_ref[...] = acc
        of_ref[...] = acc + acc_lo

    def col_spec(off, rows):
        if rows == blk:
            return pl.BlockSpec((blk, HEAD_DIM), lambda h, i: (i, off // HEAD_DIM + h))
        return pl.BlockSpec((s_len, HEAD_DIM), lambda h, i: (0, off // HEAD_DIM + h))

    out_spec = pl.BlockSpec((blk, HEAD_DIM), lambda h, i: (i, h))
    return pl.pallas_call(
        body,
        name="sb_fwd",
        grid=(HEADS, nq),
        in_specs=[col_spec(OFF_SB_Q, blk), col_spec(OFF_SB_K, s_len), col_spec(OFF_SB_V, s_len)],
        out_specs=[out_spec, out_spec],
        out_shape=[jax.ShapeDtypeStruct((s_len, D_MODEL), F32)] * 2,
        compiler_params=_cparams(("arbitrary", "arbitrary")),
    )(proj, proj, proj)


def _sb_bwd(proj, o_fine, d_o):
    s_len = proj.shape[0]
    blk = min(SB_BLOCK, s_len)
    nq = s_len // blk
    scale = HEAD_DIM ** -0.5

    def body(q_ref, k_ref, v_ref, of_ref, do_ref, dq_ref, dk_ref, dv_ref, dk_acc, dv_acc):
        i = pl.program_id(1)

        @pl.when(i == 0)
        def _():
            dk_acc[...] = jnp.zeros_like(dk_acc)
            dv_acc[...] = jnp.zeros_like(dv_acc)

        qb = q_ref[...].astype(BF16)
        dob = do_ref[...].astype(BF16)
        total = jnp.sum(dob.astype(F32) * of_ref[...], axis=-1, keepdims=True)
        row = lax.broadcasted_iota(jnp.int32, (blk, blk), 0)
        col = lax.broadcasted_iota(jnp.int32, (blk, blk), 1)
        row_gt_col = row > col
        tri_excl = row_gt_col.astype(BF16)
        tri_incl = (row >= col).astype(BF16)

        def tile(j, c_not, c_dlw, dq, diag):
            start = pl.multiple_of(j * blk, blk)
            kb = k_ref[pl.ds(start, blk), :].astype(BF16)
            vb = v_ref[pl.ds(start, blk), :].astype(BF16)
            ls_pos, log_not, log_not_m, surv, w = _sb_tile_fwd(qb, kb, row_gt_col, tri_excl, c_not, diag)
            dlw = _dot_nt(dob, vb) * w
            suffix = _split_dot(dlw, tri_incl)
            d_not = total - c_dlw - suffix
            dz = (dlw * jnp.exp(log_not) - d_not * jnp.exp(ls_pos)) * scale
            if diag:
                dz = jnp.where(row_gt_col, dz, 0.0)
            dzb = dz.astype(BF16)
            dq = dq + _dot(dzb, kb)
            dk_acc[pl.ds(start, blk), :] += _dot_tn(dzb, qb)
            dv_acc[pl.ds(start, blk), :] += _dot_tn(w.astype(BF16), dob)
            c_not = surv[:, 0:1] + log_not_m[:, 0:1]
            c_dlw = c_dlw + suffix[:, 0:1]
            return c_not, c_dlw, dq

        zcol = jnp.zeros((blk, 1), F32)
        st = tile(i, zcol, zcol, jnp.zeros((blk, HEAD_DIM), F32), True)

        def more(st):
            return (st[0] < i) & (jnp.max(st[1]) > SB_DEAD)

        def step(st):
            return (st[0] + 1,) + tile(i - 1 - st[0], *st[1:], False)

        _, _, _, dq = lax.while_loop(more, step, (0,) + st)
        dq_ref[...] = dq.astype(BF16)

        @pl.when(i == nq - 1)
        def _():
            dk_ref[...] = dk_acc[...].astype(BF16)
            dv_ref[...] = dv_acc[...].astype(BF16)

    def blk_spec(off):
        return pl.BlockSpec((blk, HEAD_DIM), lambda h, i: (i, off // HEAD_DIM + h))

    def head_spec(off):
        return pl.BlockSpec((s_len, HEAD_DIM), lambda h, i: (0, off // HEAD_DIM + h))

    return pl.pallas_call(
        body,
        name="sb_bwd",
        grid=(HEADS, nq),
        in_specs=[blk_spec(OFF_SB_Q), head_spec(OFF_SB_K), head_spec(OFF_SB_V), blk_spec(0), blk_spec(0)],
        out_specs=[blk_spec(0), head_spec(0), head_spec(0)],
        out_shape=[jax.ShapeDtypeStruct((s_len, D_MODEL), BF16)] * 3,
        scratch_shapes=[pltpu.VMEM((s_len, HEAD_DIM), F32), pltpu.VMEM((s_len, HEAD_DIM), F32)],
        compiler_params=_cparams(("arbitrary", "arbitrary")),
    )(proj, proj, proj, o_fine, d_o)


def _hg_lower_bound(lbl_ref):
    l0 = lbl_ref[0:1, :]
    l1 = lbl_ref[1:2, :]
    mx = jnp.maximum(l0, l1)
    e0 = jnp.exp(l0 - mx)
    e1 = jnp.exp(l1 - mx)
    return e0 / (e0 + e1)


def _hg_gates(hq, hf, lb):
    sig_f = _sigmoid(hf)
    f = lb + (1.0 - lb) * sig_f
    g = jnp.log(f)
    kk = 1.0 - f
    sig_q = _sigmoid(hq)
    qq = hq * sig_q
    return qq, kk, g, f, sig_f, sig_q


def _chunk_bcast(x, r, rows):
    w = x.shape[-1]
    x3 = x.reshape(rows // HG_CHUNK, HG_CHUNK, w)
    return jnp.broadcast_to(x3[:, r : r + 1, :], x3.shape).reshape(rows, w)


def _hg_decays(qq, kk, g, tri_blk, rows):
    cum = _split_dot_left(tri_blk, g)
    mid = _chunk_bcast(cum, HG_MID, rows)
    last = _chunk_bcast(cum, HG_CHUNK - 1, rows)
    e_qm = jnp.exp(cum - mid)
    e_km = jnp.exp(mid - cum)
    e_q = jnp.exp(cum)
    e_kl = jnp.exp(last - cum)
    return cum, last, e_qm, e_km, e_q, e_kl


def _blockdiag(rows, kind):
    row = lax.broadcasted_iota(jnp.int32, (rows, rows), 0)
    col = lax.broadcasted_iota(jnp.int32, (rows, rows), 1)
    keep = (row // HG_CHUNK) == (col // HG_CHUNK)
    if kind == "lower":
        keep = keep & (row >= col)
    elif kind == "upper":
        keep = keep & (row <= col)
    return jnp.where(keep, 1.0, 0.0).astype(BF16)


def _hg_fwd(proj, lbl):
    s_len = proj.shape[0]
    rows = min(HG_STEP, s_len)
    n_chunks = rows // HG_CHUNK

    def body(hq_ref, hf_ref, hi_ref, lbl_ref, o_ref, st_ref, state, q_mid, k_mid, q_dec, k_last, v_b):
        @pl.when(pl.program_id(0) == 0)
        def _():
            state[...] = jnp.zeros_like(state)

        lb = _hg_lower_bound(lbl_ref)
        qq, kk, g, _, _, _ = _hg_gates(hq_ref[...], hf_ref[...], lb)
        tri_blk = _blockdiag(rows, "lower")
        _, last, e_qm, e_km, e_q, e_kl = _hg_decays(qq, kk, g, tri_blk, rows)
        q_mid[...] = (qq * e_qm).astype(BF16)
        k_mid[...] = (kk * e_km).astype(BF16)
        q_dec[...] = (qq * e_q).astype(BF16)
        k_last[...] = (kk * e_kl).astype(BF16)
        v_b[...] = hi_ref[...].astype(BF16)
        e_last = jnp.exp(last)
        row = lax.broadcasted_iota(jnp.int32, (HG_CHUNK, HG_CHUNK), 0)
        col = lax.broadcasted_iota(jnp.int32, (HG_CHUNK, HG_CHUNK), 1)
        causal = row >= col

        for c in range(n_chunks):
            r0 = c * HG_CHUNK
            for h in range(HEADS):
                c0 = h * HEAD_DIM
                sl = (slice(r0, r0 + HG_CHUNK), slice(c0, c0 + HEAD_DIM))
                st = state[h]
                st_ref[c, h] = st
                a = jnp.where(causal, _dot_nt(q_mid[sl], k_mid[sl]), 0.0)
                vb = v_b[sl]
                o_ref[sl] = _dot(a.astype(BF16), vb) + _dot_nt(q_dec[sl], st.astype(BF16))
                decay = e_last[r0 : r0 + 1, c0 : c0 + HEAD_DIM]
                state[h] = st * decay + _dot_tn(vb, k_last[sl])

    def col_spec(off):
        return pl.BlockSpec((rows, D_MODEL), lambda s: (s, off // D_MODEL))

    scratch = [pltpu.VMEM((HEADS, HEAD_DIM, HEAD_DIM), F32)] + [pltpu.VMEM((rows, D_MODEL), BF16)] * 5
    return pl.pallas_call(
        body,
        name="hg_fwd",
        grid=(s_len // rows,),
        in_specs=[col_spec(OFF_HG_Q), col_spec(OFF_HG_F), col_spec(OFF_HG_I), pl.BlockSpec((2, D_MODEL), lambda s: (0, 0))],
        out_specs=[
            pl.BlockSpec((rows, D_MODEL), lambda s: (s, 0)),
            pl.BlockSpec((n_chunks, HEADS, HEAD_DIM, HEAD_DIM), lambda s: (s, 0, 0, 0)),
        ],
        out_shape=[
            jax.ShapeDtypeStruct((s_len, D_MODEL), F32),
            jax.ShapeDtypeStruct((s_len // HG_CHUNK, HEADS, HEAD_DIM, HEAD_DIM), F32),
        ],
        scratch_shapes=scratch,
        compiler_params=_cparams(("arbitrary",)),
    )(proj, proj, proj, lbl)


def _hg_bwd(proj, lbl, states, d_o):
    s_len = proj.shape[0]
    rows = min(HG_STEP, s_len)
    n_chunks = rows // HG_CHUNK
    n_steps = s_len // rows

    def body(hq_ref, hf_ref, hi_ref, lbl_ref, st_ref, do_ref, dp_ref, dlb_ref,
             dstate, q_mid, k_mid, q_dec, k_last, v_b, do_b, d_qm, d_km, d_qd, d_kl, d_v, d_last):
        @pl.when(pl.program_id(0) == 0)
        def _():
            dstate[...] = jnp.zeros_like(dstate)
            dlb_ref[...] = jnp.zeros_like(dlb_ref)

        lb = _hg_lower_bound(lbl_ref)
        hq = hq_ref[...]
        qq, kk, g, f, sig_f, sig_q = _hg_gates(hq, hf_ref[...], lb)
        tri_blk = _blockdiag(rows, "lower")
        _, last, e_qm, e_km, e_q, e_kl = _hg_decays(qq, kk, g, tri_blk, rows)
        qm, km, qd, kl = qq * e_qm, kk * e_km, qq * e_q, kk * e_kl
        q_mid[...] = qm.astype(BF16)
        k_mid[...] = km.astype(BF16)
        q_dec[...] = qd.astype(BF16)
        k_last[...] = kl.astype(BF16)
        v_b[...] = hi_ref[...].astype(BF16)
        do_b[...] = do_ref[...].astype(BF16)
        e_last = jnp.exp(last)
        row = lax.broadcasted_iota(jnp.int32, (HG_CHUNK, HG_CHUNK), 0)
        col = lax.broadcasted_iota(jnp.int32, (HG_CHUNK, HG_CHUNK), 1)
        causal = row >= col

        for c in reversed(range(n_chunks)):
            r0 = c * HG_CHUNK
            for h in range(HEADS):
                c0 = h * HEAD_DIM
                sl = (slice(r0, r0 + HG_CHUNK), slice(c0, c0 + HEAD_DIM))
                st0 = st_ref[c, h]
                ds1 = dstate[h]
                ds1b = ds1.astype(BF16)
                dob, vb, qmb, kmb = do_b[sl], v_b[sl], q_mid[sl], k_mid[sl]
                a = jnp.where(causal, _dot_nt(qmb, kmb), 0.0).astype(BF16)
                da = jnp.where(causal, _dot_nt(dob, vb), 0.0).astype(BF16)
                d_v[sl] = _dot_tn(a, dob) + _dot_nt(k_last[sl], ds1b)
                d_qm[sl] = _dot(da, kmb)
                d_km[sl] = _dot_tn(da, qmb)
                d_qd[sl] = _dot(dob, st0.astype(BF16))
                d_kl[sl] = _dot(vb, ds1b)
                decay = e_last[r0 : r0 + 1, c0 : c0 + HEAD_DIM]
                d_last[c : c + 1, c0 : c0 + HEAD_DIM] = decay * jnp.sum(ds1 * st0, axis=0, keepdims=True)
                dstate[h] = ds1 * decay + _dot_tn(dob, q_dec[sl])

        dqm, dkm, dqd, dkl = d_qm[...], d_km[...], d_qd[...], d_kl[...]
        dq = dqm * e_qm + dqd * e_q
        dk = dkm * e_km + dkl * e_kl
        t_kl = dkl * kl
        dcum = dqm * qm - dkm * km + dqd * qd - t_kl
        dl = d_last[...]
        dl_b = jnp.broadcast_to(dl[:, None, :], (n_chunks, HG_CHUNK, D_MODEL)).reshape(rows, D_MODEL)
        dg = _split_dot_left(_blockdiag(rows, "upper"), dcum) + _split_dot_left(_blockdiag(rows, "all"), t_kl) + dl_b
        df = dg / f - dk
        one_m = 1.0 - sig_f
        dp_ref[:, 0:D_MODEL] = (dq * (sig_q * (1.0 + hq * (1.0 - sig_q)))).astype(BF16)
        dp_ref[:, D_MODEL : 2 * D_MODEL] = (df * (1.0 - lb) * sig_f * one_m).astype(BF16)
        dp_ref[:, 2 * D_MODEL : 3 * D_MODEL] = d_v[...].astype(BF16)
        dlb_ref[...] += jnp.sum(df * one_m, axis=0, keepdims=True)

    def col_spec(off):
        return pl.BlockSpec((rows, D_MODEL), lambda s: (n_steps - 1 - s, off // D_MODEL))

    f32_tile = pltpu.VMEM((rows, D_MODEL), F32)
    bf_tile = pltpu.VMEM((rows, D_MODEL), BF16)
    scratch = [pltpu.VMEM((HEADS, HEAD_DIM, HEAD_DIM), F32)] + [bf_tile] * 6 + [f32_tile] * 5
    scratch += [pltpu.VMEM((n_chunks, D_MODEL), F32)]
    return pl.pallas_call(
        body,
        name="hg_bwd",
        grid=(n_steps,),
        in_specs=[
            col_spec(OFF_HG_Q), col_spec(OFF_HG_F), col_spec(OFF_HG_I),
            pl.BlockSpec((2, D_MODEL), lambda s: (0, 0)),
            pl.BlockSpec((n_chunks, HEADS, HEAD_DIM, HEAD_DIM), lambda s: (n_steps - 1 - s, 0, 0, 0)),
            pl.BlockSpec((rows, D_MODEL), lambda s: (n_steps - 1 - s, 0)),
        ],
        out_specs=[
            pl.BlockSpec((rows, 3 * D_MODEL), lambda s: (n_steps - 1 - s, 0)),
            pl.BlockSpec((1, D_MODEL), lambda s: (0, 0)),
        ],
        out_shape=[
            jax.ShapeDtypeStruct((s_len, 3 * D_MODEL), BF16),
            jax.ShapeDtypeStruct((1, D_MODEL), F32),
        ],
        scratch_shapes=scratch,
        compiler_params=_cparams(("arbitrary",)),
    )(proj, proj, proj, lbl, states, d_o)


def _mid(proj, sb_o, hg_o, x, target, b_gate, hg_gain, final_g, w_sb, w_hg, w_out):
    s_len = proj.shape[0]
    ts = min(128, s_len)
    inv_d = 1.0 / D_MODEL

    def body(zsb_ref, hz_ref, gl_ref, sbo_ref, hgo_ref, x_ref, tgt_ref, bg_ref, hgn_ref, fg_ref,
             wsb_ref, whg_ref, wout_ref,
             dout_ref, dsbo_ref, dhgo_ref, dzsb_ref, dhz_ref, dgl_ref,
             asb_ref, dusb_ref, ahg_ref, duhg_ref, y_ref, doutb_ref,
             loss_ref, dfg_ref, dbg_ref, dhgn_ref):
        @pl.when(pl.program_id(0) == 0)
        def _():
            loss_ref[...] = jnp.zeros_like(loss_ref)
            dfg_ref[...] = jnp.zeros_like(dfg_ref)
            dbg_ref[...] = jnp.zeros_like(dbg_ref)
            dhgn_ref[...] = jnp.zeros_like(dhgn_ref)

        z_sb = zsb_ref[...]
        sb_o = sbo_ref[...]
        sig_zsb = _sigmoid(z_sb)
        silu_zsb = z_sb * sig_zsb
        a_sb = (sb_o * silu_zsb).astype(BF16)
        u_sb = _dot(a_sb, wsb_ref[...])

        hg_o = hgo_ref[...]
        gain = hgn_ref[...]
        r_parts, yn_parts = [], []
        for h in range(HEADS):
            oh = hg_o[:, h * HEAD_DIM : (h + 1) * HEAD_DIM]
            r = lax.rsqrt(jnp.mean(oh * oh, axis=-1, keepdims=True) + RMS_EPS)
            r_parts.append(jnp.broadcast_to(r, oh.shape))
            yn_parts.append(oh * r)
        r_hg = jnp.concatenate(r_parts, axis=-1)
        yn_hg = jnp.concatenate(yn_parts, axis=-1)
        hn = yn_hg * gain
        hz = hz_ref[...]
        sig_hz = _sigmoid(hz)
        silu_hz = hz * sig_hz
        a_hg = (hn * silu_hz).astype(BF16)
        u_hg = _dot(a_hg, whg_ref[...])

        gates = _sigmoid(gl_ref[...] + bg_ref[...])
        g_sb = gates[:, 0:D_MODEL]
        g_hg = gates[:, D_MODEL:]
        y = (g_sb * u_sb + g_hg * u_hg).astype(BF16)
        out = x_ref[...] + _dot(y, wout_ref[...])
        r2 = lax.rsqrt(jnp.mean(out * out, axis=-1, keepdims=True) + RMS_EPS)
        yn = out * r2
        fg = fg_ref[...]
        diff = yn * fg - tgt_ref[...]
        loss_ref[...] += 0.5 * inv_d * jnp.sum(diff * diff)

        dyf = diff * inv_d
        dfg_ref[...] += jnp.sum(dyf * yn, axis=0, keepdims=True)
        dyn = dyf * fg
        dout = r2 * (dyn - yn * jnp.mean(dyn * yn, axis=-1, keepdims=True))
        dout_ref[...] = dout
        doutb = dout.astype(BF16)
        doutb_ref[...] = doutb
        dy = _dot_nt(doutb, wout_ref[...])
        du_sb = (dy * g_sb).astype(BF16)
        du_hg = (dy * g_hg).astype(BF16)
        dgl_sb = dy * u_sb * g_sb * (1.0 - g_sb)
        dgl_hg = dy * u_hg * g_hg * (1.0 - g_hg)
        dgl_ref[:, 0:D_MODEL] = dgl_sb.astype(BF16)
        dgl_ref[:, D_MODEL:] = dgl_hg.astype(BF16)
        dbg_ref[:, 0:D_MODEL] += jnp.sum(dgl_sb, axis=0, keepdims=True)
        dbg_ref[:, D_MODEL:] += jnp.sum(dgl_hg, axis=0, keepdims=True)

        da_sb = _dot_nt(du_sb, wsb_ref[...])
        dsbo_ref[...] = da_sb * silu_zsb
        dzsb_ref[...] = (da_sb * sb_o * (sig_zsb * (1.0 + z_sb * (1.0 - sig_zsb)))).astype(BF16)

        da_hg = _dot_nt(du_hg, whg_ref[...])
        dhn = da_hg * silu_hz
        dhz_ref[...] = (da_hg * hn * (sig_hz * (1.0 + hz * (1.0 - sig_hz)))).astype(BF16)
        dhgn_ref[...] += jnp.sum(dhn * yn_hg, axis=0, keepdims=True)
        dyn_hg = dhn * gain
        prod = dyn_hg * yn_hg
        m_parts = []
        for h in range(HEADS):
            ph = prod[:, h * HEAD_DIM : (h + 1) * HEAD_DIM]
            m_parts.append(jnp.broadcast_to(jnp.mean(ph, axis=-1, keepdims=True), ph.shape))
        dhgo_ref[...] = r_hg * (dyn_hg - yn_hg * jnp.concatenate(m_parts, axis=-1))

        asb_ref[...] = a_sb
        dusb_ref[...] = du_sb
        ahg_ref[...] = a_hg
        duhg_ref[...] = du_hg
        y_ref[...] = y

    def tile(width, off=0):
        return pl.BlockSpec((ts, width), lambda s: (s, off // width))

    def whole(shape):
        return pl.BlockSpec(shape, lambda s: (0,) * len(shape))

    sq = (D_MODEL, D_MODEL)
    f32_act = jax.ShapeDtypeStruct((s_len, D_MODEL), F32)
    bf_act = jax.ShapeDtypeStruct((s_len, D_MODEL), BF16)
    return pl.pallas_call(
        body,
        name="mid",
        grid=(s_len // ts,),
        in_specs=[
            tile(D_MODEL, OFF_SB_Z), tile(D_MODEL, OFF_HG_Z), tile(2 * D_MODEL, OFF_GATE),
            tile(D_MODEL), tile(D_MODEL), tile(D_MODEL), tile(D_MODEL),
            whole((1, 2 * D_MODEL)), whole((1, D_MODEL)), whole((1, D_MODEL)),
            whole(sq), whole(sq), whole(sq),
        ],
        out_specs=[
            tile(D_MODEL), tile(D_MODEL), tile(D_MODEL), tile(D_MODEL), tile(D_MODEL), tile(2 * D_MODEL),
            tile(D_MODEL), tile(D_MODEL), tile(D_MODEL), tile(D_MODEL), tile(D_MODEL), tile(D_MODEL),
            whole((1, 1)), whole((1, D_MODEL)), whole((1, 2 * D_MODEL)), whole((1, D_MODEL)),
        ],
        out_shape=[
            f32_act, f32_act, f32_act, bf_act, bf_act, jax.ShapeDtypeStruct((s_len, 2 * D_MODEL), BF16),
            bf_act, bf_act, bf_act, bf_act, bf_act, bf_act,
            jax.ShapeDtypeStruct((1, 1), F32), jax.ShapeDtypeStruct((1, D_MODEL), F32),
            jax.ShapeDtypeStruct((1, 2 * D_MODEL), F32), jax.ShapeDtypeStruct((1, D_MODEL), F32),
        ],
        compiler_params=_cparams(("arbitrary",)),
    )(proj, proj, proj, sb_o, hg_o, x, target, b_gate, hg_gain, final_g, w_sb, w_hg, w_out)


def _grad_matmul(a, b, name, tn):
    s_len, m = a.shape
    n = b.shape[1]
    tk = min(512, s_len)

    def body(a_ref, b_ref, o_ref):
        @pl.when(pl.program_id(1) == 0)
        def _():
            o_ref[...] = jnp.zeros_like(o_ref)

        o_ref[...] += _dot_tn(a_ref[...], b_ref[...])

    return pl.pallas_call(
        body,
        name=name,
        grid=(n // tn, s_len // tk),
        in_specs=[pl.BlockSpec((tk, m), lambda j, k: (k, 0)), pl.BlockSpec((tk, tn), lambda j, k: (k, j))],
        out_specs=pl.BlockSpec((m, tn), lambda j, k: (0, j)),
        out_shape=jax.ShapeDtypeStruct((m, n), F32),
        compiler_params=_cparams(("arbitrary", "arbitrary")),
    )(a, b)


def _dx(dproj, w4, x, norm_g, dout):
    s_len = x.shape[0]
    ts = min(1024, s_len)
    tk = 1280
    per = W_IN_SHARD // tk
    nk = IN_WIDTH // tk

    def body(dp_ref, w_ref, x_ref, g_ref, dout_ref, gx_ref, dg_ref, acc):
        k = pl.program_id(1)

        @pl.when((pl.program_id(0) == 0) & (k == 0))
        def _():
            dg_ref[...] = jnp.zeros_like(dg_ref)

        part = _dot_nt(dp_ref[...], w_ref[0])

        @pl.when(k == 0)
        def _():
            acc[...] = part

        @pl.when(k > 0)
        def _():
            acc[...] += part

        @pl.when(k == nk - 1)
        def _():
            dh = acc[...]
            xv = x_ref[...]
            r = lax.rsqrt(jnp.mean(xv * xv, axis=-1, keepdims=True) + RMS_EPS)
            xn = xv * r
            dg_ref[...] += jnp.sum(dh * xn, axis=0, keepdims=True)
            dxn = dh * g_ref[...]
            gx_ref[...] = r * (dxn - xn * jnp.mean(dxn * xn, axis=-1, keepdims=True)) + dout_ref[...]

    row_tile = pl.BlockSpec((ts, D_MODEL), lambda s, k: (s, 0))
    vec = pl.BlockSpec((1, D_MODEL), lambda s, k: (0, 0))
    return pl.pallas_call(
        body,
        name="dx",
        grid=(s_len // ts, nk),
        in_specs=[
            pl.BlockSpec((ts, tk), lambda s, k: (s, k)),
            pl.BlockSpec((1, D_MODEL, tk), lambda s, k: (k // per, 0, k % per)),
            row_tile, vec, row_tile,
        ],
        out_specs=[row_tile, vec],
        out_shape=[jax.ShapeDtypeStruct((s_len, D_MODEL), F32), jax.ShapeDtypeStruct((1, D_MODEL), F32)],
        scratch_shapes=[pltpu.VMEM((ts, D_MODEL), F32)],
        compiler_params=_cparams(("arbitrary", "arbitrary")),
    )(dproj, w4, x, norm_g, dout)


def _local_step(x, target, norm_g, b_gate, lbl, hg_gain, final_g, w4, w_sb, w_hg, w_out):
    proj, h = _inproj(x, norm_g, w4)
    sb_o, sb_o_fine = _sb_fwd(proj)
    hg_o, states = _hg_fwd(proj, lbl)
    (dout, d_sbo, d_hgo, d_zsb, d_hz, d_gl, a_sb, du_sb, a_hg, du_hg, y, doutb,
     loss, d_fg, d_bg, d_hgn) = _mid(proj, sb_o, hg_o, x, target, b_gate, hg_gain, final_g, w_sb, w_hg, w_out)
    g_w_sb = _grad_matmul(a_sb, du_sb, "grad_w_sb", 512)
    g_w_hg = _grad_matmul(a_hg, du_hg, "grad_w_hg", 512)
    g_w_out = _grad_matmul(y, doutb, "grad_w_out", 512)
    d_q, d_k, d_v = _sb_bwd(proj, sb_o_fine, d_sbo)
    d_hg, d_lb = _hg_bwd(proj, lbl, states, d_hgo)
    dproj = jnp.concatenate([d_q, d_k, d_v, d_zsb, d_hg, d_hz, d_gl], axis=1)
    g_w_in = _grad_matmul(h, dproj, "grad_w_in", 512)
    grad_x, d_ng = _dx(dproj, w4, x, norm_g, dout)
    return grad_x, g_w_in, g_w_sb, g_w_hg, g_w_out, loss, d_ng, d_bg, d_lb, d_hgn, d_fg


ANY = pl.BlockSpec(memory_space=pl.ANY)
HALF_IN = D_MODEL // 2
HALF_SQ = ROW_SHARD // 2


def _position():
    x, y, c = lax.axis_index("x"), lax.axis_index("y"), lax.axis_index("c")
    chips = [(1 - x, y), (x, 1 - y), (1 - x, 1 - y)]
    return x, y, c, chips


def _remote(src, dst, send_sem, recv_sem, to):
    return pltpu.make_async_remote_copy(src_ref=src, dst_ref=dst, send_sem=send_sem, recv_sem=recv_sem,
                                        device_id=to, device_id_type=MESH)


def _gather_weights(w_in_b, w_sq_b):
    n_in = 4
    n_piece = n_in + 3
    rows = HALF_IN // n_in

    def body(win_ref, wsq_ref, in_ref, sq_ref, send_sems, recv_sems):
        x, y, c, chips = _position()
        me = 2 * x + y
        sibling = (x, y, 1 - c)

        def src_piece(p):
            if p < n_in:
                return win_ref.at[pl.ds(c * HALF_IN + p * rows, rows), :]
            return wsq_ref.at[p - n_in, pl.ds(c * HALF_SQ, HALF_SQ), :]

        def piece(p, chip, core):
            if p < n_in:
                return in_ref.at[chip, pl.ds(core * HALF_IN + p * rows, rows), :]
            return sq_ref.at[p - n_in, chip, pl.ds(core * HALF_SQ, HALF_SQ), :]

        sends = []
        for k, (px, py) in enumerate(chips):
            for p in range(n_piece):
                sends.append(_remote(src_piece(p), piece(p, me, c), send_sems.at[k, p], recv_sems.at[k, p], (px, py, c)))
        for cp in sends:
            cp.start()
        for k, (px, py) in enumerate(chips):
            chip = 2 * px + py
            for p in range(n_piece):
                got = piece(p, chip, c)
                _remote(got, got, send_sems.at[k, p], recv_sems.at[k, p], (px, py, c)).wait_recv()
                fwd = _remote(got, got, send_sems.at[3 + k, p], recv_sems.at[3 + k, p], sibling)
                fwd.start()
                sends.append(fwd)
        for k, (px, py) in enumerate(chips):
            chip = 2 * px + py
            for p in range(n_piece):
                got = piece(p, chip, 1 - c)
                _remote(got, got, send_sems.at[3 + k, p], recv_sems.at[3 + k, p], sibling).wait_recv()
        for cp in sends:
            cp.wait_send()

    return pl.pallas_call(
        body,
        name="gather_weights",
        in_specs=[ANY, ANY],
        out_specs=[ANY, ANY],
        out_shape=[jax.ShapeDtypeStruct((N_CHIPS, D_MODEL, W_IN_SHARD), BF16),
                   jax.ShapeDtypeStruct((3, N_CHIPS, ROW_SHARD, D_MODEL), BF16)],
        scratch_shapes=[pltpu.SemaphoreType.DMA((6, n_piece)), pltpu.SemaphoreType.DMA((6, n_piece))],
    )(w_in_b, w_sq_b)


def _place_own(idx, w_in_b, w_sq_b, w4, wsq):
    n = 4
    r_in, r_sq = D_MODEL // n, ROW_SHARD // n

    def body(idx_ref, win_ref, wsq_ref, w4_in, wsq_in, w4_out, wsq_out):
        w4_out[0] = win_ref[...]
        wsq_out[:, 0] = wsq_ref[...]

    grid_spec = pltpu.PrefetchScalarGridSpec(
        num_scalar_prefetch=1,
        grid=(n,),
        in_specs=[pl.BlockSpec((r_in, W_IN_SHARD), lambda r, idx: (r, 0)),
                  pl.BlockSpec((3, r_sq, D_MODEL), lambda r, idx: (0, r, 0)), ANY, ANY],
        out_specs=[pl.BlockSpec((1, r_in, W_IN_SHARD), lambda r, idx: (idx[0], r, 0)),
                   pl.BlockSpec((3, 1, r_sq, D_MODEL), lambda r, idx: (0, idx[0], r, 0))],
    )
    return pl.pallas_call(
        body,
        name="place_own",
        grid_spec=grid_spec,
        out_shape=[jax.ShapeDtypeStruct(w4.shape, BF16), jax.ShapeDtypeStruct(wsq.shape, BF16)],
        input_output_aliases={3: 0, 4: 1},
        compiler_params=_cparams(("arbitrary",)),
    )(idx, w_in_b, w_sq_b, w4, wsq)


def _swap_halves(g_in, g_sq):
    n_in = 16
    n_piece = n_in + 3 * N_CHIPS
    rows = HALF_IN // n_in

    def body(gin_ref, gsq_ref, got_in, got_sq, send_sems, recv_sems):
        x, y, c, _ = _position()
        sibling = (x, y, 1 - c)

        def src_piece(p):
            if p < n_in:
                return gin_ref.at[pl.ds((1 - c) * HALF_IN + p * rows, rows), :]
            a, chip = divmod(p - n_in, N_CHIPS)
            return gsq_ref.at[a, chip, pl.ds((1 - c) * HALF_SQ, HALF_SQ), :]

        def dst_piece(p):
            if p < n_in:
                return got_in.at[pl.ds(p * rows, rows), :]
            a, chip = divmod(p - n_in, N_CHIPS)
            return got_sq.at[a, chip]

        out = [_remote(src_piece(p), dst_piece(p), send_sems.at[p], recv_sems.at[p], sibling) for p in range(n_piece)]
        for cp in out:
            cp.start()
        for cp in out:
            cp.wait()

    return pl.pallas_call(
        body,
        name="swap_halves",
        in_specs=[ANY, ANY],
        out_specs=[ANY, ANY],
        out_shape=[jax.ShapeDtypeStruct((HALF_IN, IN_WIDTH), F32),
                   jax.ShapeDtypeStruct((3, N_CHIPS, HALF_SQ, D_MODEL), F32)],
        scratch_shapes=[pltpu.SemaphoreType.DMA((n_piece,))] * 2,
    )(g_in, g_sq)


def _exchange_chunks(s_in, s_sq):
    n_in = 8
    n_piece = n_in + 3
    rows = HALF_IN // n_in

    def body(sin_ref, ssq_ref, got_in, got_sq, send_sems, recv_sems):
        x, y, c, chips = _position()

        def src_piece(p, chip):
            if p < n_in:
                return sin_ref.at[chip, pl.ds(p * rows, rows), :]
            return ssq_ref.at[p - n_in, chip]

        def dst_piece(p, k):
            if p < n_in:
                return got_in.at[k, pl.ds(p * rows, rows), :]
            return got_sq.at[k, p - n_in]

        out = []
        for k, (px, py) in enumerate(chips):
            chip = 2 * px + py
            for p in range(n_piece):
                out.append(_remote(src_piece(p, chip), dst_piece(p, k), send_sems.at[k, p], recv_sems.at[k, p],
                                   (px, py, c)))
        for cp in out:
            cp.start()
        for cp in out:
            cp.wait()

    return pl.pallas_call(
        body,
        name="exchange_chunks",
        in_specs=[ANY, ANY],
        out_specs=[ANY, ANY],
        out_shape=[jax.ShapeDtypeStruct((3, HALF_IN, W_IN_SHARD), F32),
                   jax.ShapeDtypeStruct((3, 3, HALF_SQ, D_MODEL), F32)],
        scratch_shapes=[pltpu.SemaphoreType.DMA((3, n_piece)), pltpu.SemaphoreType.DMA((3, n_piece))],
    )(s_in, s_sq)


def _join_halves(r_in, r_sq):
    n_in = 16
    n_piece = n_in + 3
    rows = HALF_IN // n_in

    def body(in_alias, sq_alias, full_in, full_sq, send_sems, recv_sems):
        del in_alias, sq_alias
        x, y, c, _ = _position()
        sibling = (x, y, 1 - c)

        def piece(p, core):
            if p < n_in:
                return full_in.at[pl.ds(core * HALF_IN + p * rows, rows), :]
            return full_sq.at[p - n_in, pl.ds(core * HALF_SQ, HALF_SQ), :]

        out = [_remote(piece(p, c), piece(p, c), send_sems.at[p], recv_sems.at[p], sibling) for p in range(n_piece)]
        for cp in out:
            cp.start()
        for p in range(n_piece):
            _remote(piece(p, 1 - c), piece(p, 1 - c), send_sems.at[p], recv_sems.at[p], sibling).wait_recv()
        for cp in out:
            cp.wait_send()

    return pl.pallas_call(
        body,
        name="join_halves",
        in_specs=[ANY, ANY],
        out_specs=[ANY, ANY],
        out_shape=[jax.ShapeDtypeStruct((D_MODEL, W_IN_SHARD), F32),
                   jax.ShapeDtypeStruct((3, ROW_SHARD, D_MODEL), F32)],
        input_output_aliases={0: 0, 1: 1},
        scratch_shapes=[pltpu.SemaphoreType.DMA((n_piece,)), pltpu.SemaphoreType.DMA((n_piece,))],
    )(r_in, r_sq)


SMALL_ROWS = 56
N_DEV = 8


def _sum_small(part):
    def body(part_ref, out_ref, slots, send_sems, recv_sems):
        x, y, c, _ = _position()
        me = 4 * x + 2 * y + c
        slots[me] = part_ref[...]
        out = []
        for r in range(1, N_DEV):
            rx, ry, rc = (r >> 2) & 1, (r >> 1) & 1, r & 1
            to = (1 - x if rx else x, 1 - y if ry else y, 1 - c if rc else c)
            out.append(_remote(part_ref, slots.at[me], send_sems.at[r - 1], recv_sems.at[r - 1], to))
        for cp in out:
            cp.start()
        for r in range(1, N_DEV):
            _remote(part_ref, slots.at[me ^ r], send_sems.at[r - 1], recv_sems.at[r - 1], (x, y, c)).wait_recv()
        for cp in out:
            cp.wait_send()
        total = slots[0]
        for d in range(1, N_DEV):
            total = total + slots[d]
        out_ref[...] = total

    vmem = pl.BlockSpec(memory_space=pltpu.VMEM)
    return pl.pallas_call(
        body,
        name="sum_small",
        in_specs=[vmem],
        out_specs=vmem,
        out_shape=jax.ShapeDtypeStruct((SMALL_ROWS, HEAD_DIM), F32),
        scratch_shapes=[pltpu.VMEM((N_DEV, SMALL_ROWS, HEAD_DIM), F32),
                        pltpu.SemaphoreType.DMA((N_DEV - 1,)), pltpu.SemaphoreType.DMA((N_DEV - 1,))],
    )(part)


def _prefetch_call(body, name, idx, grid, in_specs, out_specs, out_shape, args):
    grid_spec = pltpu.PrefetchScalarGridSpec(num_scalar_prefetch=1, grid=grid, in_specs=in_specs, out_specs=out_specs)
    return pl.pallas_call(body, name=name, grid_spec=grid_spec, out_shape=out_shape,
                          compiler_params=_cparams(("arbitrary",) * len(grid)))(idx, *args)


def _sum_a_in(idx, g_in, got_in):
    tr = 128
    nr = HALF_IN // tr

    def body(idx_ref, a_ref, b_ref, o_ref):
        o_ref[0] = a_ref[...] + b_ref[...]

    return _prefetch_call(
        body, "sum_a_in", idx, (N_CHIPS, nr),
        [pl.BlockSpec((tr, W_IN_SHARD), lambda j, r, idx: (idx[1] * nr + r, j)),
         pl.BlockSpec((tr, W_IN_SHARD), lambda j, r, idx: (r, j))],
        pl.BlockSpec((1, tr, W_IN_SHARD), lambda j, r, idx: (j, r, 0)),
        jax.ShapeDtypeStruct((N_CHIPS, HALF_IN, W_IN_SHARD), F32), (g_in, got_in))


def _sum_a_sq(idx, g_sq, got_sq):
    blk = (1, 1, HALF_SQ, D_MODEL)

    def body(idx_ref, a_ref, b_ref, o_ref):
        o_ref[...] = a_ref[...] + b_ref[...]

    return _prefetch_call(
        body, "sum_a_sq", idx, (3, N_CHIPS),
        [pl.BlockSpec(blk, lambda a, j, idx: (a, j, idx[1], 0)), pl.BlockSpec(blk, lambda a, j, idx: (a, j, 0, 0))],
        pl.BlockSpec(blk, lambda a, j, idx: (a, j, 0, 0)),
        jax.ShapeDtypeStruct((3, N_CHIPS, HALF_SQ, D_MODEL), F32), (g_sq, got_sq))


def _sum_b_in(idx, s_in, got_in):
    tr = 128
    nr = HALF_IN // tr

    def body(idx_ref, a_ref, b_ref, o_ref):
        o_ref[...] = ((a_ref[0] + b_ref[0]) + b_ref[1]) + b_ref[2]

    return _prefetch_call(
        body, "sum_b_in", idx, (nr,),
        [pl.BlockSpec((1, tr, W_IN_SHARD), lambda r, idx: (idx[0], r, 0)),
         pl.BlockSpec((3, tr, W_IN_SHARD), lambda r, idx: (0, r, 0))],
        pl.BlockSpec((tr, W_IN_SHARD), lambda r, idx: (idx[1] * nr + r, 0)),
        jax.ShapeDtypeStruct((D_MODEL, W_IN_SHARD), F32), (s_in, got_in))


def _sum_b_sq(idx, s_sq, got_sq):
    def body(idx_ref, a_ref, b_ref, o_ref):
        o_ref[0] = ((a_ref[0, 0] + b_ref[0, 0]) + b_ref[1, 0]) + b_ref[2, 0]

    return _prefetch_call(
        body, "sum_b_sq", idx, (3,),
        [pl.BlockSpec((1, 1, HALF_SQ, D_MODEL), lambda a, idx: (a, idx[0], 0, 0)),
         pl.BlockSpec((3, 1, HALF_SQ, D_MODEL), lambda a, idx: (0, a, 0, 0))],
        pl.BlockSpec((1, HALF_SQ, D_MODEL), lambda a, idx: (a, idx[1], 0)),
        jax.ShapeDtypeStruct((3, ROW_SHARD, D_MODEL), F32), (s_sq, got_sq))


def _adamw_math(w, g, m, v):
    m = ADAM_B1 * m + (1.0 - ADAM_B1) * g
    v = ADAM_B2 * v + (1.0 - ADAM_B2) * (g * g)
    m_hat = m / (1.0 - ADAM_B1 ** ADAM_STEP)
    v_hat = v / (1.0 - ADAM_B2 ** ADAM_STEP)
    delta = -ADAM_LR * (m_hat / (jnp.sqrt(v_hat) + ADAM_EPS) + ADAM_WD * w)
    return delta, m, v


def _adamw(w, g, m, v, name):
    rows, cols = w.shape
    tr = min(128, rows)

    def body(w_ref, g_ref, m_ref, v_ref, d_ref, nm_ref, nv_ref):
        d_ref[...], nm_ref[...], nv_ref[...] = _adamw_math(w_ref[...], g_ref[...], m_ref[...], v_ref[...])

    spec = pl.BlockSpec((tr, cols), lambda r: (r, 0))
    return pl.pallas_call(
        body,
        name=name,
        grid=(rows // tr,),
        in_specs=[spec] * 4,
        out_specs=[spec] * 3,
        out_shape=[jax.ShapeDtypeStruct((rows, cols), F32)] * 3,
        compiler_params=_cparams(("arbitrary",)),
    )(w, g, m, v)


def _adamw_small(sums, w, m, v):
    def body(s_ref, w_ref, m_ref, v_ref, loss_ref, g_ref, d_ref, nm_ref, nv_ref):
        s = s_ref[...]
        w = w_ref[...]
        loss_ref[...] = s[0:1, 0:1]
        l0, l1 = w[24:32], w[32:40]
        mx = jnp.maximum(l0, l1)
        e0, e1 = jnp.exp(l0 - mx), jnp.exp(l1 - mx)
        p0, p1 = e0 / (e0 + e1), e1 / (e0 + e1)
        d_lb = s[32:40]
        g = jnp.concatenate([s[8:16], s[16:32], d_lb * p0 * (1.0 - p0), -d_lb * p0 * p1, s[40:48], s[48:56]], axis=0)
        g_ref[...] = g
        d_ref[...], nm_ref[...], nv_ref[...] = _adamw_math(w, g, m_ref[...], v_ref[...])

    packed = jax.ShapeDtypeStruct((SMALL_ROWS, HEAD_DIM), F32)
    return pl.pallas_call(
        body,
        name="adamw_small",
        out_shape=[jax.ShapeDtypeStruct((1, 1), F32), packed, packed, packed, packed],
    )(sums, w, m, v)


def _pack_small(ng, bg, lbl, hgn, fg):
    return jnp.concatenate([a.reshape(-1, HEAD_DIM) for a in (ng, bg, lbl, hgn, fg)], axis=0)


def _unpack_small(p):
    return (p[0:8].reshape(1, D_MODEL), p[8:24].reshape(1, 2 * D_MODEL), p[24:40].reshape(2, HEADS, HEAD_DIM),
            p[40:48].reshape(1, HEADS, HEAD_DIM), p[48:56].reshape(D_MODEL))


def kernel(x, norm_g, w_in, b_gate, lb_logits, hg_norm_g, w_sb_proj, w_hg_proj, w_out, final_norm_g, loss_target, m_norm_g, m_w_in, m_b_gate, m_lb_logits, m_hg_norm_g, m_w_sb_proj, m_w_hg_proj, m_w_out, m_final_norm_g, v_norm_g, v_w_in, v_b_gate, v_lb_logits, v_hg_norm_g, v_w_sb_proj, v_w_hg_proj, v_w_out, v_final_norm_g):
    s_len = x.shape[1]
    w_sq = jnp.stack([w_sb_proj[0], w_hg_proj[0], w_out[0]])
    idx = jnp.stack([2 * lax.axis_index("x") + lax.axis_index("y"), lax.axis_index("c")]).astype(jnp.int32)
    w_in_b, w_sq_b = w_in[0].astype(BF16), w_sq.astype(BF16)
    w4, wsq = _place_own(idx, w_in_b, w_sq_b, *_gather_weights(w_in_b, w_sq_b))
    wsq = wsq.reshape(3, D_MODEL, D_MODEL)

    (grad_x, g_in, g_sb, g_hg, g_out, loss, d_ng, d_bg, d_lb, d_hgn, d_fg) = _local_step(
        x[0], loss_target[0], norm_g, b_gate, lb_logits.reshape(2, D_MODEL), hg_norm_g.reshape(1, D_MODEL),
        final_norm_g.reshape(1, D_MODEL), w4, wsq[0], wsq[1], wsq[2])

    g_sq = jnp.stack([g_sb, g_hg, g_out]).reshape(3, N_CHIPS, ROW_SHARD, D_MODEL)
    got_in, got_sq = _swap_halves(g_in, g_sq)
    s_in, s_sq = _sum_a_in(idx, g_in, got_in), _sum_a_sq(idx, g_sq, got_sq)
    got_in, got_sq = _exchange_chunks(s_in, s_sq)
    grad_in, grad_sq = _join_halves(_sum_b_in(idx, s_in, got_in), _sum_b_sq(idx, s_sq, got_sq))

    d_in, nm_in, nv_in = _adamw(w_in[0], grad_in, m_w_in[0], v_w_in[0], "adamw_in")
    flat = lambda a, b, c: jnp.concatenate([a[0], b[0], c[0]], axis=0)
    d_sq, nm_sq, nv_sq = _adamw(flat(w_sb_proj, w_hg_proj, w_out), grad_sq.reshape(3 * ROW_SHARD, D_MODEL),
                                flat(m_w_sb_proj, m_w_hg_proj, m_w_out), flat(v_w_sb_proj, v_w_hg_proj, v_w_out),
                                "adamw_sq")

    pad = jnp.zeros((8, HEAD_DIM), F32).at[0, 0].set(loss[0, 0])
    part = jnp.concatenate([pad] + [a.reshape(-1, HEAD_DIM) for a in (d_ng, d_bg, d_lb, d_hgn, d_fg)], axis=0)
    sums = _sum_small(part)
    loss_out, g_sm, d_sm, nm_sm, nv_sm = _adamw_small(
        sums, _pack_small(norm_g, b_gate, lb_logits, hg_norm_g, final_norm_g),
        _pack_small(m_norm_g, m_b_gate, m_lb_logits, m_hg_norm_g, m_final_norm_g),
        _pack_small(v_norm_g, v_b_gate, v_lb_logits, v_hg_norm_g, v_final_norm_g))

    def big(t_in, t_sq):
        sq = t_sq.reshape(3, 1, ROW_SHARD, D_MODEL)
        return t_in[None], sq[0], sq[1], sq[2]

    def order(small, in_, sb, hg, out):
        ng, bg, lbl, hgn, fg = small
        return [ng, in_, bg, lbl, hgn, sb, hg, out, fg]

    outs = [loss_out[0, 0], grad_x[None]]
    for small, (t_in, t_sq) in ((g_sm, (grad_in, grad_sq)), (d_sm, (d_in, d_sq)), (nm_sm, (nm_in, nm_sq)), (nv_sm, (nv_in, nv_sq))):
        outs += order(_unpack_small(small), *big(t_in, t_sq))
    return tuple(outs)
```

```python
import functools

import jax
import jax.numpy as jnp
from jax import lax
from jax.experimental import pallas as pl
from jax.experimental.pallas import tpu as pltpu

F32 = jnp.float32
BF16 = jnp.bfloat16

D_MODEL = 1024
HEADS = 8
HEAD_DIM = 128
IN_WIDTH = 10240
N_CHIPS = 4
W_IN_SHARD = IN_WIDTH // N_CHIPS
ROW_SHARD = D_MODEL // N_CHIPS
RMS_EPS = 1e-6

OFF_SB_Q, OFF_SB_K, OFF_SB_V, OFF_SB_Z = 0, 1024, 2048, 3072
OFF_HG_Q, OFF_HG_F, OFF_HG_I, OFF_HG_Z, OFF_GATE = 4096, 5120, 6144, 7168, 8192

QKV_COLS = 3840
SB_BLOCK = 256
SB_PAIR = 2
SB_DEAD = -110.0
HG_CHUNK = 32
HG_STEP = 256
HG_MID = HG_CHUNK // 2 - 1

ADAM_LR, ADAM_B1, ADAM_B2, ADAM_EPS, ADAM_WD, ADAM_STEP = 0.001, 0.9, 0.999, 1e-08, 0.01, 10

VMEM_LIMIT = 56 * 1024 * 1024

MESH = pl.DeviceIdType.MESH


def _cparams(sem, vmem=VMEM_LIMIT):
    return pltpu.CompilerParams(dimension_semantics=sem, vmem_limit_bytes=vmem)


def _dot(a, b):
    return jnp.dot(a, b, preferred_element_type=F32)


def _dot_nt(a, b):
    return lax.dot_general(a, b, (((1,), (1,)), ((), ())), preferred_element_type=F32)


def _dot_tn(a, b):
    return lax.dot_general(a, b, (((0,), (0,)), ((), ())), preferred_element_type=F32)


def _split_dot(x, tri):
    hi = x.astype(BF16)
    lo = (x - hi.astype(F32)).astype(BF16)
    return _dot(hi, tri) + _dot(lo, tri)


def _split_dot_left(tri, x):
    hi = x.astype(BF16)
    lo = (x - hi.astype(F32)).astype(BF16)
    return _dot(tri, hi) + _dot(tri, lo)


def _sigmoid(x):
    return 1.0 / (1.0 + jnp.exp(-x))


def _inproj(x, norm_g, w4):
    s_len = x.shape[0]
    ts = min(1024, s_len)
    tn = QKV_COLS // 3
    per = W_IN_SHARD // tn

    def body(x_ref, g_ref, w_ref, proj_ref, ht_ref, qkv_ref, h_scr):
        n = pl.program_id(1)

        @pl.when(n == 0)
        def _():
            xv = x_ref[...]
            r = lax.rsqrt(jnp.mean(xv * xv, axis=-1, keepdims=True) + RMS_EPS)
            hv = (xv * r) * g_ref[...]
            h_scr[...] = hv.astype(BF16)
            ht_ref[...] = hv.T.astype(BF16)

        p = _dot(h_scr[...], w_ref[0])
        proj_ref[...] = p

        @pl.when(n < 3)
        def _():
            qkv_ref[...] = p.astype(BF16)

    return pl.pallas_call(
        body,
        name="inproj",
        grid=(s_len // ts, IN_WIDTH // tn),
        in_specs=[
            pl.BlockSpec((ts, D_MODEL), lambda s, n: (s, 0)),
            pl.BlockSpec((1, D_MODEL), lambda s, n: (0, 0)),
            pl.BlockSpec((1, D_MODEL, tn), lambda s, n: (n // per, 0, n % per)),
        ],
        out_specs=[
            pl.BlockSpec((ts, tn), lambda s, n: (s, n)),
            pl.BlockSpec((D_MODEL, ts), lambda s, n: (0, s)),
            pl.BlockSpec((ts, tn), lambda s, n: (s, jnp.minimum(n, 2))),
        ],
        out_shape=[
            jax.ShapeDtypeStruct((s_len, IN_WIDTH), F32),
            jax.ShapeDtypeStruct((D_MODEL, s_len), BF16),
            jax.ShapeDtypeStruct((s_len, QKV_COLS), BF16),
        ],
        scratch_shapes=[pltpu.VMEM((ts, D_MODEL), BF16)],
        compiler_params=_cparams(("arbitrary", "arbitrary")),
    )(x, norm_g, w4)


def _sb_tile_fwd(qb, kb, row_gt_col, tri_excl, carry, diag):
    scale = HEAD_DIM ** -0.5
    z = _dot_nt(qb, kb) * scale
    ls_pos = jnp.minimum(z, 0.0) - jnp.log1p(jnp.exp(-jnp.abs(z)))
    log_not = ls_pos - z
    log_not_m = jnp.where(row_gt_col, log_not, 0.0) if diag else log_not
    surv = _split_dot(log_not_m, tri_excl) + carry
    w = jnp.exp(ls_pos + surv)
    if diag:
        w = jnp.where(row_gt_col, w, 0.0)
    return ls_pos, log_not, log_not_m, surv, w


def _sb_specs(s_len, blk):
    width = SB_PAIR * HEAD_DIM

    def blk_spec(off):
        return pl.BlockSpec((blk, width), lambda h, i: (i, off // width + h))

    def head_spec(off):
        return pl.BlockSpec((s_len, width), lambda h, i: (0, off // width + h))

    return blk_spec, head_spec


def _head_cols(p):
    return slice(p * HEAD_DIM, (p + 1) * HEAD_DIM)


def _sb_fwd(qkv):
    s_len = qkv.shape[0]
    blk = min(SB_BLOCK, s_len)
    nq = s_len // blk

    def body(q_ref, k_ref, v_ref, o_ref, of_ref):
        i = pl.program_id(1)
        row = lax.broadcasted_iota(jnp.int32, (blk, blk), 0)
        col = lax.broadcasted_iota(jnp.int32, (blk, blk), 1)
        row_gt_col = row > col
        tri_excl = row_gt_col.astype(BF16)

        def tile(j, st, diag):
            start = pl.multiple_of(j * blk, blk)
            new = []
            for p in range(SB_PAIR):
                carry, acc, acc_lo = st[3 * p : 3 * p + 3]
                kb = k_ref[pl.ds(start, blk), _head_cols(p)]
                vb = v_ref[pl.ds(start, blk), _head_cols(p)]
                _, _, log_not_m, surv, w = _sb_tile_fwd(q_ref[:, _head_cols(p)], kb, row_gt_col, tri_excl, carry, diag)
                wb = w.astype(BF16)
                w_lo = (w - wb.astype(F32)).astype(BF16)
                new += [surv[:, 0:1] + log_not_m[:, 0:1], acc + _dot(wb, vb), acc_lo + _dot(w_lo, vb)]
            return tuple(new)

        zero = jnp.zeros((blk, HEAD_DIM), F32)
        st = tile(i, (jnp.zeros((blk, 1), F32), zero, zero) * SB_PAIR, True)

        def more(st):
            alive = functools.reduce(jnp.maximum, [st[1 + 3 * p] for p in range(SB_PAIR)])
            return (st[0] < i) & (jnp.max(alive) > SB_DEAD)

        def step(st):
            return (st[0] + 1,) + tile(i - 1 - st[0], st[1:], False)

        st = lax.while_loop(more, step, (0,) + st)[1:]
        for p in range(SB_PAIR):
            o_ref[:, _head_cols(p)] = st[3 * p + 1]
            of_ref[:, _head_cols(p)] = st[3 * p + 1] + st[3 * p + 2]

    blk_spec, head_spec = _sb_specs(s_len, blk)
    return pl.pallas_call(
        body,
        name="sb_fwd",
        grid=(HEADS // SB_PAIR, nq),
        in_specs=[blk_spec(OFF_SB_Q), head_spec(OFF_SB_K), head_spec(OFF_SB_V)],
        out_specs=[blk_spec(0), blk_spec(0)],
        out_shape=[jax.ShapeDtypeStruct((s_len, D_MODEL), F32)] * 2,
        compiler_params=_cparams(("arbitrary", "arbitrary")),
    )(qkv, qkv, qkv)


def _sb_bwd(qkv, o_fine, d_o):
    s_len = qkv.shape[0]
    blk = min(SB_BLOCK, s_len)
    nq = s_len // blk
    scale = HEAD_DIM ** -0.5

    def body(q_ref, k_ref, v_ref, of_ref, do_ref, dq_ref, dk_ref, dv_ref, dk_acc, dv_acc):
        i = pl.program_id(1)

        @pl.when(i == 0)
        def _():
            dk_acc[...] = jnp.zeros_like(dk_acc)
            dv_acc[...] = jnp.zeros_like(dv_acc)

        dob = do_ref[...].astype(BF16)
        prod = dob.astype(F32) * of_ref[...]
        totals = [jnp.sum(prod[:, _head_cols(p)], axis=-1, keepdims=True) for p in range(SB_PAIR)]
        row = lax.broadcasted_iota(jnp.int32, (blk, blk), 0)
        col = lax.broadcasted_iota(jnp.int32, (blk, blk), 1)
        row_gt_col = row > col
        tri_excl = row_gt_col.astype(BF16)
        tri_incl = (row >= col).astype(BF16)

        def tile(j, st, diag):
            start = pl.multiple_of(j * blk, blk)
            new = []
            for p in range(SB_PAIR):
                c_not, c_dlw, dq = st[3 * p : 3 * p + 3]
                cols = _head_cols(p)
                qb, dob_p = q_ref[:, cols], dob[:, cols]
                kb = k_ref[pl.ds(start, blk), cols]
                vb = v_ref[pl.ds(start, blk), cols]
                ls_pos, log_not, log_not_m, surv, w = _sb_tile_fwd(qb, kb, row_gt_col, tri_excl, c_not, diag)
                dlw = _dot_nt(dob_p, vb) * w
                suffix = _split_dot(dlw, tri_incl)
                d_not = totals[p] - c_dlw - suffix
                dz = (dlw * jnp.exp(log_not) - d_not * jnp.exp(ls_pos)) * scale
                if diag:
                    dz = jnp.where(row_gt_col, dz, 0.0)
                dzb = dz.astype(BF16)
                dk_acc[pl.ds(start, blk), cols] += _dot_tn(dzb, qb)
                dv_acc[pl.ds(start, blk), cols] += _dot_tn(w.astype(BF16), dob_p)
                new += [surv[:, 0:1] + log_not_m[:, 0:1], c_dlw + suffix[:, 0:1], dq + _dot(dzb, kb)]
            return tuple(new)

        zcol = jnp.zeros((blk, 1), F32)
        st = tile(i, (zcol, zcol, jnp.zeros((blk, HEAD_DIM), F32)) * SB_PAIR, True)

        def more(st):
            alive = functools.reduce(jnp.maximum, [st[1 + 3 * p] for p in range(SB_PAIR)])
            return (st[0] < i) & (jnp.max(alive) > SB_DEAD)

        def step(st):
            return (st[0] + 1,) + tile(i - 1 - st[0], st[1:], False)

        st = lax.while_loop(more, step, (0,) + st)[1:]
        for p in range(SB_PAIR):
            dq_ref[:, _head_cols(p)] = st[3 * p + 2].astype(BF16)

        @pl.when(i == nq - 1)
        def _():
            dk_ref[...] = dk_acc[...].astype(BF16)
            dv_ref[...] = dv_acc[...].astype(BF16)

    blk_spec, head_spec = _sb_specs(s_len, blk)
    width = SB_PAIR * HEAD_DIM
    return pl.pallas_call(
        body,
        name="sb_bwd",
        grid=(HEADS // SB_PAIR, nq),
        in_specs=[blk_spec(OFF_SB_Q), head_spec(OFF_SB_K), head_spec(OFF_SB_V), blk_spec(0), blk_spec(0)],
        out_specs=[blk_spec(0), head_spec(0), head_spec(0)],
        out_shape=[jax.ShapeDtypeStruct((s_len, D_MODEL), BF16)] * 3,
        scratch_shapes=[pltpu.VMEM((s_len, width), F32), pltpu.VMEM((s_len, width), F32)],
        compiler_params=_cparams(("arbitrary", "arbitrary")),
    )(qkv, qkv, qkv, o_fine, d_o)


def _hg_lower_bound(lbl_ref):
    l0 = lbl_ref[0:1, :]
    l1 = lbl_ref[1:2, :]
    mx = jnp.maximum(l0, l1)
    e0 = jnp.exp(l0 - mx)
    e1 = jnp.exp(l1 - mx)
    return e0 / (e0 + e1)


def _hg_gates(hq, hf, lb):
    sig_f = _sigmoid(hf)
    f = lb + (1.0 - lb) * sig_f
    g = jnp.log(f)
    kk = 1.0 - f
    sig_q = _sigmoid(hq)
    qq = hq * sig_q
    return qq, kk, g, f, sig_f, sig_q


def _chunk_bcast(x, r, rows):
    w = x.shape[-1]
    x3 = x.reshape(rows // HG_CHUNK, HG_CHUNK, w)
    return jnp.broadcast_to(x3[:, r : r + 1, :], x3.shape).reshape(rows, w)


def _hg_decays(qq, kk, g, tri_blk, rows):
    cum = _split_dot_left(tri_blk, g)
    mid = _chunk_bcast(cum, HG_MID, rows)
    last = _chunk_bcast(cum, HG_CHUNK - 1, rows)
    e_qm = jnp.exp(cum - mid)
    e_km = jnp.exp(mid - cum)
    e_q = jnp.exp(cum)
    e_kl = jnp.exp(last - cum)
    return cum, last, e_qm, e_km, e_q, e_kl


def _blockdiag(rows, kind):
    row = lax.broadcasted_iota(jnp.int32, (rows, rows), 0)
    col = lax.broadcasted_iota(jnp.int32, (rows, rows), 1)
    keep = (row // HG_CHUNK) == (col // HG_CHUNK)
    if kind == "lower":
        keep = keep & (row >= col)
    elif kind == "upper":
        keep = keep & (row <= col)
    return jnp.where(keep, 1.0, 0.0).astype(BF16)


def _hg_fwd(proj, lbl):
    s_len = proj.shape[0]
    rows = min(HG_STEP, s_len)
    n_chunks = rows // HG_CHUNK

    def body(hq_ref, hf_ref, hi_ref, lbl_ref, o_ref, st_ref, state, q_mid, k_mid, q_dec, k_last, v_b):
        @pl.when(pl.program_id(0) == 0)
        def _():
            state[...] = jnp.zeros_like(state)

        lb = _hg_lower_bound(lbl_ref)
        qq, kk, g, _, _, _ = _hg_gates(hq_ref[...], hf_ref[...], lb)
        tri_blk = _blockdiag(rows, "lower")
        _, last, e_qm, e_km, e_q, e_kl = _hg_decays(qq, kk, g, tri_blk, rows)
        q_mid[...] = (qq * e_qm).astype(BF16)
        k_mid[...] = (kk * e_km).astype(BF16)
        q_dec[...] = (qq * e_q).astype(BF16)
        k_last[...] = (kk * e_kl).astype(BF16)
        v_b[...] = hi_ref[...].astype(BF16)
        e_last = jnp.exp(last)
        row = lax.broadcasted_iota(jnp.int32, (HG_CHUNK, HG_CHUNK), 0)
        col = lax.broadcasted_iota(jnp.int32, (HG_CHUNK, HG_CHUNK), 1)
        causal = row >= col

        for c in range(n_chunks):
            r0 = c * HG_CHUNK
            for h in range(HEADS):
                c0 = h * HEAD_DIM
                sl = (slice(r0, r0 + HG_CHUNK), slice(c0, c0 + HEAD_DIM))
                st = state[h]
                st_ref[c, h] = st
                a = jnp.where(causal, _dot_nt(q_mid[sl], k_mid[sl]), 0.0)
                vb = v_b[sl]
                o_ref[sl] = _dot(a.astype(BF16), vb) + _dot_nt(q_dec[sl], st.astype(BF16))
                decay = e_last[r0 : r0 + 1, c0 : c0 + HEAD_DIM]
                state[h] = st * decay + _dot_tn(vb, k_last[sl])

    def col_spec(off):
        return pl.BlockSpec((rows, D_MODEL), lambda s: (s, off // D_MODEL))

    scratch = [pltpu.VMEM((HEADS, HEAD_DIM, HEAD_DIM), F32)] + [pltpu.VMEM((rows, D_MODEL), BF16)] * 5
    return pl.pallas_call(
        body,
        name="hg_fwd",
        grid=(s_len // rows,),
        in_specs=[col_spec(OFF_HG_Q), col_spec(OFF_HG_F), col_spec(OFF_HG_I), pl.BlockSpec((2, D_MODEL), lambda s: (0, 0))],
        out_specs=[
            pl.BlockSpec((rows, D_MODEL), lambda s: (s, 0)),
            pl.BlockSpec((n_chunks, HEADS, HEAD_DIM, HEAD_DIM), lambda s: (s, 0, 0, 0)),
        ],
        out_shape=[
            jax.ShapeDtypeStruct((s_len, D_MODEL), F32),
            jax.ShapeDtypeStruct((s_len // HG_CHUNK, HEADS, HEAD_DIM, HEAD_DIM), F32),
        ],
        scratch_shapes=scratch,
        compiler_params=_cparams(("arbitrary",)),
    )(proj, proj, proj, lbl)


def _hg_bwd(proj, lbl, states, d_o):
    s_len = proj.shape[0]
    rows = min(HG_STEP, s_len)
    n_chunks = rows // HG_CHUNK
    n_steps = s_len // rows

    def body(hq_ref, hf_ref, hi_ref, lbl_ref, st_ref, do_ref, dp_ref, dlb_ref,
             dstate, q_mid, k_mid, q_dec, k_last, v_b, do_b, d_qm, d_km, d_qd, d_kl, d_v, d_last):
        @pl.when(pl.program_id(0) == 0)
        def _():
            dstate[...] = jnp.zeros_like(dstate)
            dlb_ref[...] = jnp.zeros_like(dlb_ref)

        lb = _hg_lower_bound(lbl_ref)
        hq = hq_ref[...]
        qq, kk, g, f, sig_f, sig_q = _hg_gates(hq, hf_ref[...], lb)
        tri_blk = _blockdiag(rows, "lower")
        _, last, e_qm, e_km, e_q, e_kl = _hg_decays(qq, kk, g, tri_blk, rows)
        qm, km, qd, kl = qq * e_qm, kk * e_km, qq * e_q, kk * e_kl
        q_mid[...] = qm.astype(BF16)
        k_mid[...] = km.astype(BF16)
        q_dec[...] = qd.astype(BF16)
        k_last[...] = kl.astype(BF16)
        v_b[...] = hi_ref[...].astype(BF16)
        do_b[...] = do_ref[...].astype(BF16)
        e_last = jnp.exp(last)
        row = lax.broadcasted_iota(jnp.int32, (HG_CHUNK, HG_CHUNK), 0)
        col = lax.broadcasted_iota(jnp.int32, (HG_CHUNK, HG_CHUNK), 1)
        causal = row >= col

        for c in reversed(range(n_chunks)):
            r0 = c * HG_CHUNK
            for h in range(HEADS):
                c0 = h * HEAD_DIM
                sl = (slice(r0, r0 + HG_CHUNK), slice(c0, c0 + HEAD_DIM))
                st0 = st_ref[c, h]
                ds1 = dstate[h]
                ds1b = ds1.astype(BF16)
                dob, vb, qmb, kmb = do_b[sl], v_b[sl], q_mid[sl], k_mid[sl]
                a = jnp.where(causal, _dot_nt(qmb, kmb), 0.0).astype(BF16)
                da = jnp.where(causal, _dot_nt(dob, vb), 0.0).astype(BF16)
                d_v[sl] = _dot_tn(a, dob) + _dot_nt(k_last[sl], ds1b)
                d_qm[sl] = _dot(da, kmb)
                d_km[sl] = _dot_tn(da, qmb)
                d_qd[sl] = _dot(dob, st0.astype(BF16))
                d_kl[sl] = _dot(vb, ds1b)
                decay = e_last[r0 : r0 + 1, c0 : c0 + HEAD_DIM]
                d_last[c : c + 1, c0 : c0 + HEAD_DIM] = decay * jnp.sum(ds1 * st0, axis=0, keepdims=True)
                dstate[h] = ds1 * decay + _dot_tn(dob, q_dec[sl])

        dqm, dkm, dqd, dkl = d_qm[...], d_km[...], d_qd[...], d_kl[...]
        dq = dqm * e_qm + dqd * e_q
        dk = dkm * e_km + dkl * e_kl
        t_kl = dkl * kl
        dcum = dqm * qm - dkm * km + dqd * qd - t_kl
        dl = d_last[...]
        dl_b = jnp.broadcast_to(dl[:, None, :], (n_chunks, HG_CHUNK, D_MODEL)).reshape(rows, D_MODEL)
        dg = _split_dot_left(_blockdiag(rows, "upper"), dcum) + _split_dot_left(_blockdiag(rows, "all"), t_kl) + dl_b
        df = dg / f - dk
        one_m = 1.0 - sig_f
        dp_ref[:, 0:D_MODEL] = (dq * (sig_q * (1.0 + hq * (1.0 - sig_q)))).astype(BF16)
        dp_ref[:, D_MODEL : 2 * D_MODEL] = (df * (1.0 - lb) * sig_f * one_m).astype(BF16)
        dp_ref[:, 2 * D_MODEL : 3 * D_MODEL] = d_v[...].astype(BF16)
        dlb_ref[...] += jnp.sum(df * one_m, axis=0, keepdims=True)

    def col_spec(off):
        return pl.BlockSpec((rows, D_MODEL), lambda s: (n_steps - 1 - s, off // D_MODEL))

    f32_tile = pltpu.VMEM((rows, D_MODEL), F32)
    bf_tile = pltpu.VMEM((rows, D_MODEL), BF16)
    scratch = [pltpu.VMEM((HEADS, HEAD_DIM, HEAD_DIM), F32)] + [bf_tile] * 6 + [f32_tile] * 5
    scratch += [pltpu.VMEM((n_chunks, D_MODEL), F32)]
    return pl.pallas_call(
        body,
        name="hg_bwd",
        grid=(n_steps,),
        in_specs=[
            col_spec(OFF_HG_Q), col_spec(OFF_HG_F), col_spec(OFF_HG_I),
            pl.BlockSpec((2, D_MODEL), lambda s: (0, 0)),
            pl.BlockSpec((n_chunks, HEADS, HEAD_DIM, HEAD_DIM), lambda s: (n_steps - 1 - s, 0, 0, 0)),
            pl.BlockSpec((rows, D_MODEL), lambda s: (n_steps - 1 - s, 0)),
        ],
        out_specs=[
            pl.BlockSpec((rows, 3 * D_MODEL), lambda s: (n_steps - 1 - s, 0)),
            pl.BlockSpec((1, D_MODEL), lambda s: (0, 0)),
        ],
        out_shape=[
            jax.ShapeDtypeStruct((s_len, 3 * D_MODEL), BF16),
            jax.ShapeDtypeStruct((1, D_MODEL), F32),
        ],
        scratch_shapes=scratch,
        compiler_params=_cparams(("arbitrary",)),
    )(proj, proj, proj, lbl, states, d_o)


def _mid(proj, sb_o, hg_o, x, target, b_gate, hg_gain, final_g, w_sb, w_hg, w_out):
    s_len = proj.shape[0]
    ts = min(128, s_len)
    inv_d = 1.0 / D_MODEL

    def body(zsb_ref, hz_ref, gl_ref, sbo_ref, hgo_ref, x_ref, tgt_ref, bg_ref, hgn_ref, fg_ref,
             wsb_ref, whg_ref, wout_ref,
             dout_ref, dsbo_ref, dhgo_ref, dzsb_ref, dhz_ref, dgl_ref,
             asb_ref, dusb_ref, ahg_ref, duhg_ref, y_ref, doutb_ref,
             loss_ref, dfg_ref, dbg_ref, dhgn_ref):
        @pl.when(pl.program_id(0) == 0)
        def _():
            loss_ref[...] = jnp.zeros_like(loss_ref)
            dfg_ref[...] = jnp.zeros_like(dfg_ref)
            dbg_ref[...] = jnp.zeros_like(dbg_ref)
            dhgn_ref[...] = jnp.zeros_like(dhgn_ref)

        z_sb = zsb_ref[...]
        sb_o = sbo_ref[...]
        sig_zsb = _sigmoid(z_sb)
        silu_zsb = z_sb * sig_zsb
        a_sb = (sb_o * silu_zsb).astype(BF16)
        u_sb = _dot(a_sb, wsb_ref[...])

        hg_o = hgo_ref[...]
        gain = hgn_ref[...]
        r_parts, yn_parts = [], []
        for h in range(HEADS):
            oh = hg_o[:, h * HEAD_DIM : (h + 1) * HEAD_DIM]
            r = lax.rsqrt(jnp.mean(oh * oh, axis=-1, keepdims=True) + RMS_EPS)
            r_parts.append(jnp.broadcast_to(r, oh.shape))
            yn_parts.append(oh * r)
        r_hg = jnp.concatenate(r_parts, axis=-1)
        yn_hg = jnp.concatenate(yn_parts, axis=-1)
        hn = yn_hg * gain
        hz = hz_ref[...]
        sig_hz = _sigmoid(hz)
        silu_hz = hz * sig_hz
        a_hg = (hn * silu_hz).astype(BF16)
        u_hg = _dot(a_hg, whg_ref[...])

        gates = _sigmoid(gl_ref[...] + bg_ref[...])
        g_sb = gates[:, 0:D_MODEL]
        g_hg = gates[:, D_MODEL:]
        y = (g_sb * u_sb + g_hg * u_hg).astype(BF16)
        out = x_ref[...] + _dot(y, wout_ref[...])
        r2 = lax.rsqrt(jnp.mean(out * out, axis=-1, keepdims=True) + RMS_EPS)
        yn = out * r2
        fg = fg_ref[...]
        diff = yn * fg - tgt_ref[...]
        loss_ref[...] += 0.5 * inv_d * jnp.sum(diff * diff)

        dyf = diff * inv_d
        dfg_ref[...] += jnp.sum(dyf * yn, axis=0, keepdims=True)
        dyn = dyf * fg
        dout = r2 * (dyn - yn * jnp.mean(dyn * yn, axis=-1, keepdims=True))
        dout_ref[...] = dout
        doutb = dout.astype(BF16)
        doutb_ref[...] = doutb
        dy = _dot_nt(doutb, wout_ref[...])
        du_sb = (dy * g_sb).astype(BF16)
        du_hg = (dy * g_hg).astype(BF16)
        dgl_sb = dy * u_sb * g_sb * (1.0 - g_sb)
        dgl_hg = dy * u_hg * g_hg * (1.0 - g_hg)
        dgl_ref[:, 0:D_MODEL] = dgl_sb.astype(BF16)
        dgl_ref[:, D_MODEL:] = dgl_hg.astype(BF16)
        dbg_ref[:, 0:D_MODEL] += jnp.sum(dgl_sb, axis=0, keepdims=True)
        dbg_ref[:, D_MODEL:] += jnp.sum(dgl_hg, axis=0, keepdims=True)

        da_sb = _dot_nt(du_sb, wsb_ref[...])
        dsbo_ref[...] = da_sb * silu_zsb
        dzsb_ref[...] = (da_sb * sb_o * (sig_zsb * (1.0 + z_sb * (1.0 - sig_zsb)))).astype(BF16)

        da_hg = _dot_nt(du_hg, whg_ref[...])
        dhn = da_hg * silu_hz
        dhz_ref[...] = (da_hg * hn * (sig_hz * (1.0 + hz * (1.0 - sig_hz)))).astype(BF16)
        dhgn_ref[...] += jnp.sum(dhn * yn_hg, axis=0, keepdims=True)
        dyn_hg = dhn * gain
        prod = dyn_hg * yn_hg
        m_parts = []
        for h in range(HEADS):
            ph = prod[:, h * HEAD_DIM : (h + 1) * HEAD_DIM]
            m_parts.append(jnp.broadcast_to(jnp.mean(ph, axis=-1, keepdims=True), ph.shape))
        dhgo_ref[...] = r_hg * (dyn_hg - yn_hg * jnp.concatenate(m_parts, axis=-1))

        asb_ref[...] = a_sb
        dusb_ref[...] = du_sb
        ahg_ref[...] = a_hg
        duhg_ref[...] = du_hg
        y_ref[...] = y

    def tile(width, off=0):
        return pl.BlockSpec((ts, width), lambda s: (s, off // width))

    def whole(shape):
        return pl.BlockSpec(shape, lambda s: (0,) * len(shape))

    sq = (D_MODEL, D_MODEL)
    f32_act = jax.ShapeDtypeStruct((s_len, D_MODEL), F32)
    bf_act = jax.ShapeDtypeStruct((s_len, D_MODEL), BF16)
    return pl.pallas_call(
        body,
        name="mid",
        grid=(s_len // ts,),
        in_specs=[
            tile(D_MODEL, OFF_SB_Z), tile(D_MODEL, OFF_HG_Z), tile(2 * D_MODEL, OFF_GATE),
            tile(D_MODEL), tile(D_MODEL), tile(D_MODEL), tile(D_MODEL),
            whole((1, 2 * D_MODEL)), whole((1, D_MODEL)), whole((1, D_MODEL)),
            whole(sq), whole(sq), whole(sq),
        ],
        out_specs=[
            tile(D_MODEL), tile(D_MODEL), tile(D_MODEL), tile(D_MODEL), tile(D_MODEL), tile(2 * D_MODEL),
            tile(D_MODEL), tile(D_MODEL), tile(D_MODEL), tile(D_MODEL), tile(D_MODEL), tile(D_MODEL),
            whole((1, 1)), whole((1, D_MODEL)), whole((1, 2 * D_MODEL)), whole((1, D_MODEL)),
        ],
        out_shape=[
            f32_act, f32_act, f32_act, bf_act, bf_act, jax.ShapeDtypeStruct((s_len, 2 * D_MODEL), BF16),
            bf_act, bf_act, bf_act, bf_act, bf_act, bf_act,
            jax.ShapeDtypeStruct((1, 1), F32), jax.ShapeDtypeStruct((1, D_MODEL), F32),
            jax.ShapeDtypeStruct((1, 2 * D_MODEL), F32), jax.ShapeDtypeStruct((1, D_MODEL), F32),
        ],
        compiler_params=_cparams(("arbitrary",)),
    )(proj, proj, proj, sb_o, hg_o, x, target, b_gate, hg_gain, final_g, w_sb, w_hg, w_out)


def _grad_matmul(a, b, name, tn):
    s_len, m = a.shape
    n = b.shape[1]
    tk = min(512, s_len)

    def body(a_ref, b_ref, o_ref):
        @pl.when(pl.program_id(1) == 0)
        def _():
            o_ref[...] = jnp.zeros_like(o_ref)

        o_ref[...] += _dot_tn(a_ref[...], b_ref[...])

    return pl.pallas_call(
        body,
        name=name,
        grid=(n // tn, s_len // tk),
        in_specs=[pl.BlockSpec((tk, m), lambda j, k: (k, 0)), pl.BlockSpec((tk, tn), lambda j, k: (k, j))],
        out_specs=pl.BlockSpec((m, tn), lambda j, k: (0, j)),
        out_shape=jax.ShapeDtypeStruct((m, n), F32),
        compiler_params=_cparams(("arbitrary", "arbitrary")),
    )(a, b)


def _grad_matmul_nn(a_t, b, name):
    m, s_len = a_t.shape
    n = b.shape[1]
    tk = min(1024, s_len)
    tn = 1024

    def body(a_ref, b_ref, o_ref):
        @pl.when(pl.program_id(1) == 0)
        def _():
            o_ref[...] = jnp.zeros_like(o_ref)

        o_ref[...] += _dot(a_ref[...], b_ref[...])

    return pl.pallas_call(
        body,
        name=name,
        grid=(n // tn, s_len // tk),
        in_specs=[pl.BlockSpec((m, tk), lambda j, k: (0, k)), pl.BlockSpec((tk, tn), lambda j, k: (k, j))],
        out_specs=pl.BlockSpec((m, tn), lambda j, k: (0, j)),
        out_shape=jax.ShapeDtypeStruct((m, n), F32),
        compiler_params=_cparams(("arbitrary", "arbitrary")),
    )(a_t, b)


def _dx(dproj, w4, x, norm_g, dout):
    s_len = x.shape[0]
    ts = min(1024, s_len)
    tk = 1280
    per = W_IN_SHARD // tk
    nk = IN_WIDTH // tk

    def body(dp_ref, w_ref, x_ref, g_ref, dout_ref, gx_ref, dg_ref, acc):
        k = pl.program_id(1)

        @pl.when((pl.program_id(0) == 0) & (k == 0))
        def _():
            dg_ref[...] = jnp.zeros_like(dg_ref)

        part = _dot_nt(dp_ref[...], w_ref[0])

        @pl.when(k == 0)
        def _():
            acc[...] = part

        @pl.when(k > 0)
        def _():
            acc[...] += part

        @pl.when(k == nk - 1)
        def _():
            dh = acc[...]
            xv = x_ref[...]
            r = lax.rsqrt(jnp.mean(xv * xv, axis=-1, keepdims=True) + RMS_EPS)
            xn = xv * r
            dg_ref[...] += jnp.sum(dh * xn, axis=0, keepdims=True)
            dxn = dh * g_ref[...]
            gx_ref[...] = r * (dxn - xn * jnp.mean(dxn * xn, axis=-1, keepdims=True)) + dout_ref[...]

    row_tile = pl.BlockSpec((ts, D_MODEL), lambda s, k: (s, 0))
    vec = pl.BlockSpec((1, D_MODEL), lambda s, k: (0, 0))
    return pl.pallas_call(
        body,
        name="dx",
        grid=(s_len // ts, nk),
        in_specs=[
            pl.BlockSpec((ts, tk), lambda s, k: (s, k)),
            pl.BlockSpec((1, D_MODEL, tk), lambda s, k: (k // per, 0, k % per)),
            row_tile, vec, row_tile,
        ],
        out_specs=[row_tile, vec],
        out_shape=[jax.ShapeDtypeStruct((s_len, D_MODEL), F32), jax.ShapeDtypeStruct((1, D_MODEL), F32)],
        scratch_shapes=[pltpu.VMEM((ts, D_MODEL), F32)],
        compiler_params=_cparams(("arbitrary", "arbitrary")),
    )(dproj, w4, x, norm_g, dout)


def _local_step(x, target, norm_g, b_gate, lbl, hg_gain, final_g, w4, w_sb, w_hg, w_out):
    proj, h_t, qkv = _inproj(x, norm_g, w4)
    sb_o, sb_o_fine = _sb_fwd(qkv)
    hg_o, states = _hg_fwd(proj, lbl)
    (dout, d_sbo, d_hgo, d_zsb, d_hz, d_gl, a_sb, du_sb, a_hg, du_hg, y, doutb,
     loss, d_fg, d_bg, d_hgn) = _mid(proj, sb_o, hg_o, x, target, b_gate, hg_gain, final_g, w_sb, w_hg, w_out)
    g_w_sb = _grad_matmul(a_sb, du_sb, "grad_w_sb", 512)
    g_w_hg = _grad_matmul(a_hg, du_hg, "grad_w_hg", 512)
    g_w_out = _grad_matmul(y, doutb, "grad_w_out", 512)
    d_q, d_k, d_v = _sb_bwd(qkv, sb_o_fine, d_sbo)
    d_hg, d_lb = _hg_bwd(proj, lbl, states, d_hgo)
    dproj = jnp.concatenate([d_q, d_k, d_v, d_zsb, d_hg, d_hz, d_gl], axis=1)
    g_w_in = _grad_matmul_nn(h_t, dproj, "grad_w_in")
    grad_x, d_ng = _dx(dproj, w4, x, norm_g, dout)
    return grad_x, g_w_in, g_w_sb, g_w_hg, g_w_out, loss, d_ng, d_bg, d_lb, d_hgn, d_fg


ANY = pl.BlockSpec(memory_space=pl.ANY)
WIRE = BF16
HALF_IN = D_MODEL // 2
HALF_SQ = ROW_SHARD // 2


def _position():
    x, y, c = lax.axis_index("x"), lax.axis_index("y"), lax.axis_index("c")
    chips = [(1 - x, y), (x, 1 - y), (1 - x, 1 - y)]
    return x, y, c, chips


def _remote(src, dst, send_sem, recv_sem, to):
    return pltpu.make_async_remote_copy(src_ref=src, dst_ref=dst, send_sem=send_sem, recv_sem=recv_sem,
                                        device_id=to, device_id_type=MESH)


def _gather_weights(w_in_b, w_sq_b):
    n_in = 4
    n_piece = n_in + 3
    rows = HALF_IN // n_in

    def body(win_ref, wsq_ref, in_ref, sq_ref, send_sems, recv_sems):
        x, y, c, chips = _position()
        me = 2 * x + y
        sibling = (x, y, 1 - c)

        def src_piece(p):
            if p < n_in:
                return win_ref.at[pl.ds(c * HALF_IN + p * rows, rows), :]
            return wsq_ref.at[p - n_in, pl.ds(c * HALF_SQ, HALF_SQ), :]

        def piece(p, chip, core):
            if p < n_in:
                return in_ref.at[chip, pl.ds(core * HALF_IN + p * rows, rows), :]
            return sq_ref.at[p - n_in, chip, pl.ds(core * HALF_SQ, HALF_SQ), :]

        sends = []
        for k, (px, py) in enumerate(chips):
            for p in range(n_piece):
                sends.append(_remote(src_piece(p), piece(p, me, c), send_sems.at[k, p], recv_sems.at[k, p], (px, py, c)))
        for cp in sends:
            cp.start()
        for k, (px, py) in enumerate(chips):
            chip = 2 * px + py
            for p in range(n_piece):
                got = piece(p, chip, c)
                _remote(got, got, send_sems.at[k, p], recv_sems.at[k, p], (px, py, c)).wait_recv()
                fwd = _remote(got, got, send_sems.at[3 + k, p], recv_sems.at[3 + k, p], sibling)
                fwd.start()
                sends.append(fwd)
        for k, (px, py) in enumerate(chips):
            chip = 2 * px + py
            for p in range(n_piece):
                got = piece(p, chip, 1 - c)
                _remote(got, got, send_sems.at[3 + k, p], recv_sems.at[3 + k, p], sibling).wait_recv()
        for cp in sends:
            cp.wait_send()

    return pl.pallas_call(
        body,
        name="gather_weights",
        in_specs=[ANY, ANY],
        out_specs=[ANY, ANY],
        out_shape=[jax.ShapeDtypeStruct((N_CHIPS, D_MODEL, W_IN_SHARD), BF16),
                   jax.ShapeDtypeStruct((3, N_CHIPS, ROW_SHARD, D_MODEL), BF16)],
        scratch_shapes=[pltpu.SemaphoreType.DMA((6, n_piece)), pltpu.SemaphoreType.DMA((6, n_piece))],
    )(w_in_b, w_sq_b)


def _place_own(idx, w_in_b, w_sq_b, w4, wsq):
    n = 4
    r_in, r_sq = D_MODEL // n, ROW_SHARD // n

    def body(idx_ref, win_ref, wsq_ref, w4_in, wsq_in, w4_out, wsq_out):
        w4_out[0] = win_ref[...]
        wsq_out[:, 0] = wsq_ref[...]

    grid_spec = pltpu.PrefetchScalarGridSpec(
        num_scalar_prefetch=1,
        grid=(n,),
        in_specs=[pl.BlockSpec((r_in, W_IN_SHARD), lambda r, idx: (r, 0)),
                  pl.BlockSpec((3, r_sq, D_MODEL), lambda r, idx: (0, r, 0)), ANY, ANY],
        out_specs=[pl.BlockSpec((1, r_in, W_IN_SHARD), lambda r, idx: (idx[0], r, 0)),
                   pl.BlockSpec((3, 1, r_sq, D_MODEL), lambda r, idx: (0, idx[0], r, 0))],
    )
    return pl.pallas_call(
        body,
        name="place_own",
        grid_spec=grid_spec,
        out_shape=[jax.ShapeDtypeStruct(w4.shape, BF16), jax.ShapeDtypeStruct(wsq.shape, BF16)],
        input_output_aliases={3: 0, 4: 1},
        compiler_params=_cparams(("arbitrary",)),
    )(idx, w_in_b, w_sq_b, w4, wsq)


def _swap_halves(g_in, g_sq):
    n_in = 16
    n_piece = n_in + 3 * N_CHIPS
    rows = HALF_IN // n_in

    def body(gin_ref, gsq_ref, got_in, got_sq, send_sems, recv_sems):
        x, y, c, _ = _position()
        sibling = (x, y, 1 - c)

        def src_piece(p):
            if p < n_in:
                return gin_ref.at[pl.ds((1 - c) * HALF_IN + p * rows, rows), :]
            a, chip = divmod(p - n_in, N_CHIPS)
            return gsq_ref.at[a, chip, pl.ds((1 - c) * HALF_SQ, HALF_SQ), :]

        def dst_piece(p):
            if p < n_in:
                return got_in.at[pl.ds(p * rows, rows), :]
            a, chip = divmod(p - n_in, N_CHIPS)
            return got_sq.at[a, chip]

        out = [_remote(src_piece(p), dst_piece(p), send_sems.at[p], recv_sems.at[p], sibling) for p in range(n_piece)]
        for cp in out:
            cp.start()
        for cp in out:
            cp.wait()

    return pl.pallas_call(
        body,
        name="swap_halves",
        in_specs=[ANY, ANY],
        out_specs=[ANY, ANY],
        out_shape=[jax.ShapeDtypeStruct((HALF_IN, IN_WIDTH), F32),
                   jax.ShapeDtypeStruct((3, N_CHIPS, HALF_SQ, D_MODEL), F32)],
        scratch_shapes=[pltpu.SemaphoreType.DMA((n_piece,))] * 2,
    )(g_in, g_sq)


def _exchange_chunks(s_in, s_sq):
    n_in = 8
    n_piece = n_in + 3
    rows = HALF_IN // n_in

    def body(sin_ref, ssq_ref, got_in, got_sq, send_sems, recv_sems):
        x, y, c, chips = _position()

        def src_piece(p, chip):
            if p < n_in:
                return sin_ref.at[chip, pl.ds(p * rows, rows), :]
            return ssq_ref.at[p - n_in, chip]

        def dst_piece(p, k):
            if p < n_in:
                return got_in.at[k, pl.ds(p * rows, rows), :]
            return got_sq.at[k, p - n_in]

        out = []
        for k, (px, py) in enumerate(chips):
            chip = 2 * px + py
            for p in range(n_piece):
                out.append(_remote(src_piece(p, chip), dst_piece(p, k), send_sems.at[k, p], recv_sems.at[k, p],
                                   (px, py, c)))
        for cp in out:
            cp.start()
        for cp in out:
            cp.wait()

    return pl.pallas_call(
        body,
        name="exchange_chunks",
        in_specs=[ANY, ANY],
        out_specs=[ANY, ANY],
        out_shape=[jax.ShapeDtypeStruct((3, HALF_IN, W_IN_SHARD), WIRE),
                   jax.ShapeDtypeStruct((3, 3, HALF_SQ, D_MODEL), WIRE)],
        scratch_shapes=[pltpu.SemaphoreType.DMA((3, n_piece)), pltpu.SemaphoreType.DMA((3, n_piece))],
    )(s_in, s_sq)


def _join_halves(r_in, r_sq):
    n_in = 16
    n_piece = n_in + 3
    rows = HALF_IN // n_in

    def body(in_alias, sq_alias, full_in, full_sq, send_sems, recv_sems):
        del in_alias, sq_alias
        x, y, c, _ = _position()
        sibling = (x, y, 1 - c)

        def piece(p, core):
            if p < n_in:
                return full_in.at[pl.ds(core * HALF_IN + p * rows, rows), :]
            return full_sq.at[p - n_in, pl.ds(core * HALF_SQ, HALF_SQ), :]

        out = [_remote(piece(p, c), piece(p, c), send_sems.at[p], recv_sems.at[p], sibling) for p in range(n_piece)]
        for cp in out:
            cp.start()
        for p in range(n_piece):
            _remote(piece(p, 1 - c), piece(p, 1 - c), send_sems.at[p], recv_sems.at[p], sibling).wait_recv()
        for cp in out:
            cp.wait_send()

    return pl.pallas_call(
        body,
        name="join_halves",
        in_specs=[ANY, ANY],
        out_specs=[ANY, ANY],
        out_shape=[jax.ShapeDtypeStruct((D_MODEL, W_IN_SHARD), F32),
                   jax.ShapeDtypeStruct((3, ROW_SHARD, D_MODEL), F32)],
        input_output_aliases={0: 0, 1: 1},
        scratch_shapes=[pltpu.SemaphoreType.DMA((n_piece,)), pltpu.SemaphoreType.DMA((n_piece,))],
    )(r_in, r_sq)


SMALL_ROWS = 56
N_DEV = 8


def _sum_small(part):
    def body(part_ref, out_ref, slots, send_sems, recv_sems):
        x, y, c, _ = _position()
        me = 4 * x + 2 * y + c
        slots[me] = part_ref[...]
        out = []
        for r in range(1, N_DEV):
            rx, ry, rc = (r >> 2) & 1, (r >> 1) & 1, r & 1
            to = (1 - x if rx else x, 1 - y if ry else y, 1 - c if rc else c)
            out.append(_remote(part_ref, slots.at[me], send_sems.at[r - 1], recv_sems.at[r - 1], to))
        for cp in out:
            cp.start()
        for r in range(1, N_DEV):
            _remote(part_ref, slots.at[me ^ r], send_sems.at[r - 1], recv_sems.at[r - 1], (x, y, c)).wait_recv()
        for cp in out:
            cp.wait_send()
        total = slots[0]
        for d in range(1, N_DEV):
            total = total + slots[d]
        out_ref[...] = total

    vmem = pl.BlockSpec(memory_space=pltpu.VMEM)
    return pl.pallas_call(
        body,
        name="sum_small",
        in_specs=[vmem],
        out_specs=vmem,
        out_shape=jax.ShapeDtypeStruct((SMALL_ROWS, HEAD_DIM), F32),
        scratch_shapes=[pltpu.VMEM((N_DEV, SMALL_ROWS, HEAD_DIM), F32),
                        pltpu.SemaphoreType.DMA((N_DEV - 1,)), pltpu.SemaphoreType.DMA((N_DEV - 1,))],
    )(part)


def _prefetch_call(body, name, idx, grid, in_specs, out_specs, out_shape, args):
    grid_spec = pltpu.PrefetchScalarGridSpec(num_scalar_prefetch=1, grid=grid, in_specs=in_specs, out_specs=out_specs)
    return pl.pallas_call(body, name=name, grid_spec=grid_spec, out_shape=out_shape,
                          compiler_params=_cparams(("arbitrary",) * len(grid)))(idx, *args)


def _sum_a_in(idx, g_in, got_in):
    tr = 128
    nr = HALF_IN // tr

    def body(idx_ref, a_ref, b_ref, o_ref):
        o_ref[0] = (a_ref[...] + b_ref[...]).astype(WIRE)

    return _prefetch_call(
        body, "sum_a_in", idx, (N_CHIPS, nr),
        [pl.BlockSpec((tr, W_IN_SHARD), lambda j, r, idx: (idx[1] * nr + r, j)),
         pl.BlockSpec((tr, W_IN_SHARD), lambda j, r, idx: (r, j))],
        pl.BlockSpec((1, tr, W_IN_SHARD), lambda j, r, idx: (j, r, 0)),
        jax.ShapeDtypeStruct((N_CHIPS, HALF_IN, W_IN_SHARD), WIRE), (g_in, got_in))


def _sum_a_sq(idx, g_sq, got_sq):
    blk = (1, 1, HALF_SQ, D_MODEL)

    def body(idx_ref, a_ref, b_ref, o_ref):
        o_ref[...] = (a_ref[...] + b_ref[...]).astype(WIRE)

    return _prefetch_call(
        body, "sum_a_sq", idx, (3, N_CHIPS),
        [pl.BlockSpec(blk, lambda a, j, idx: (a, j, idx[1], 0)), pl.BlockSpec(blk, lambda a, j, idx: (a, j, 0, 0))],
        pl.BlockSpec(blk, lambda a, j, idx: (a, j, 0, 0)),
        jax.ShapeDtypeStruct((3, N_CHIPS, HALF_SQ, D_MODEL), WIRE), (g_sq, got_sq))


def _sum_b_in(idx, s_in, got_in):
    tr = 128
    nr = HALF_IN // tr

    def body(idx_ref, a_ref, b_ref, o_ref):
        o_ref[...] = ((a_ref[0].astype(F32) + b_ref[0].astype(F32)) + b_ref[1].astype(F32)) + b_ref[2].astype(F32)

    return _prefetch_call(
        body, "sum_b_in", idx, (nr,),
        [pl.BlockSpec((1, tr, W_IN_SHARD), lambda r, idx: (idx[0], r, 0)),
         pl.BlockSpec((3, tr, W_IN_SHARD), lambda r, idx: (0, r, 0))],
        pl.BlockSpec((tr, W_IN_SHARD), lambda r, idx: (idx[1] * nr + r, 0)),
        jax.ShapeDtypeStruct((D_MODEL, W_IN_SHARD), F32), (s_in, got_in))


def _sum_b_sq(idx, s_sq, got_sq):
    def body(idx_ref, a_ref, b_ref, o_ref):
        o_ref[0] = ((a_ref[0, 0].astype(F32) + b_ref[0, 0].astype(F32)) + b_ref[1, 0].astype(F32)) + b_ref[2, 0].astype(F32)

    return _prefetch_call(
        body, "sum_b_sq", idx, (3,),
        [pl.BlockSpec((1, 1, HALF_SQ, D_MODEL), lambda a, idx: (a, idx[0], 0, 0)),
         pl.BlockSpec((3, 1, HALF_SQ, D_MODEL), lambda a, idx: (0, a, 0, 0))],
        pl.BlockSpec((1, HALF_SQ, D_MODEL), lambda a, idx: (a, idx[1], 0)),
        jax.ShapeDtypeStruct((3, ROW_SHARD, D_MODEL), F32), (s_sq, got_sq))


def _adamw_math(w, g, m, v):
    m = ADAM_B1 * m + (1.0 - ADAM_B1) * g
    v = ADAM_B2 * v + (1.0 - ADAM_B2) * (g * g)
    m_hat = m / (1.0 - ADAM_B1 ** ADAM_STEP)
    v_hat = v / (1.0 - ADAM_B2 ** ADAM_STEP)
    delta = -ADAM_LR * (m_hat / (jnp.sqrt(v_hat) + ADAM_EPS) + ADAM_WD * w)
    return delta, m, v


def _adamw(w, g, m, v, name):
    rows, cols = w.shape
    tr = min(128, rows)

    def body(w_ref, g_ref, m_ref, v_ref, d_ref, nm_ref, nv_ref):
        d_ref[...], nm_ref[...], nv_ref[...] = _adamw_math(w_ref[...], g_ref[...], m_ref[...], v_ref[...])

    spec = pl.BlockSpec((tr, cols), lambda r: (r, 0))
    return pl.pallas_call(
        body,
        name=name,
        grid=(rows // tr,),
        in_specs=[spec] * 4,
        out_specs=[spec] * 3,
        out_shape=[jax.ShapeDtypeStruct((rows, cols), F32)] * 3,
        compiler_params=_cparams(("arbitrary",)),
    )(w, g, m, v)


def _adamw_small(sums, w, m, v):
    def body(s_ref, w_ref, m_ref, v_ref, loss_ref, g_ref, d_ref, nm_ref, nv_ref):
        s = s_ref[...]
        w = w_ref[...]
        loss_ref[...] = s[0:1, 0:1]
        l0, l1 = w[24:32], w[32:40]
        mx = jnp.maximum(l0, l1)
        e0, e1 = jnp.exp(l0 - mx), jnp.exp(l1 - mx)
        p0, p1 = e0 / (e0 + e1), e1 / (e0 + e1)
        d_lb = s[32:40]
        g = jnp.concatenate([s[8:16], s[16:32], d_lb * p0 * (1.0 - p0), -d_lb * p0 * p1, s[40:48], s[48:56]], axis=0)
        g_ref[...] = g
        d_ref[...], nm_ref[...], nv_ref[...] = _adamw_math(w, g, m_ref[...], v_ref[...])

    packed = jax.ShapeDtypeStruct((SMALL_ROWS, HEAD_DIM), F32)
    return pl.pallas_call(
        body,
        name="adamw_small",
        out_shape=[jax.ShapeDtypeStruct((1, 1), F32), packed, packed, packed, packed],
    )(sums, w, m, v)


def _pack_small(ng, bg, lbl, hgn, fg):
    return jnp.concatenate([a.reshape(-1, HEAD_DIM) for a in (ng, bg, lbl, hgn, fg)], axis=0)


def _unpack_small(p):
    return (p[0:8].reshape(1, D_MODEL), p[8:24].reshape(1, 2 * D_MODEL), p[24:40].reshape(2, HEADS, HEAD_DIM),
            p[40:48].reshape(1, HEADS, HEAD_DIM), p[48:56].reshape(D_MODEL))


def kernel(x, norm_g, w_in, b_gate, lb_logits, hg_norm_g, w_sb_proj, w_hg_proj, w_out, final_norm_g, loss_target, m_norm_g, m_w_in, m_b_gate, m_lb_logits, m_hg_norm_g, m_w_sb_proj, m_w_hg_proj, m_w_out, m_final_norm_g, v_norm_g, v_w_in, v_b_gate, v_lb_logits, v_hg_norm_g, v_w_sb_proj, v_w_hg_proj, v_w_out, v_final_norm_g):
    s_len = x.shape[1]
    w_sq = jnp.stack([w_sb_proj[0], w_hg_proj[0], w_out[0]])
    idx = jnp.stack([2 * lax.axis_index("x") + lax.axis_index("y"), lax.axis_index("c")]).astype(jnp.int32)
    w_in_b, w_sq_b = w_in[0].astype(BF16), w_sq.astype(BF16)
    w4, wsq = _place_own(idx, w_in_b, w_sq_b, *_gather_weights(w_in_b, w_sq_b))
    wsq = wsq.reshape(3, D_MODEL, D_MODEL)

    (grad_x, g_in, g_sb, g_hg, g_out, loss, d_ng, d_bg, d_lb, d_hgn, d_fg) = _local_step(
        x[0], loss_target[0], norm_g, b_gate, lb_logits.reshape(2, D_MODEL), hg_norm_g.reshape(1, D_MODEL),
        final_norm_g.reshape(1, D_MODEL), w4, wsq[0], wsq[1], wsq[2])

    g_sq = jnp.stack([g_sb, g_hg, g_out]).reshape(3, N_CHIPS, ROW_SHARD, D_MODEL)
    got_in, got_sq = _swap_halves(g_in, g_sq)
    s_in, s_sq = _sum_a_in(idx, g_in, got_in), _sum_a_sq(idx, g_sq, got_sq)
    got_in, got_sq = _exchange_chunks(s_in, s_sq)
    grad_in, grad_sq = _join_halves(_sum_b_in(idx, s_in, got_in), _sum_b_sq(idx, s_sq, got_sq))

    d_in, nm_in, nv_in = _adamw(w_in[0], grad_in, m_w_in[0], v_w_in[0], "adamw_in")
    flat = lambda a, b, c: jnp.concatenate([a[0], b[0], c[0]], axis=0)
    d_sq, nm_sq, nv_sq = _adamw(flat(w_sb_proj, w_hg_proj, w_out), grad_sq.reshape(3 * ROW_SHARD, D_MODEL),
                                flat(m_w_sb_proj, m_w_hg_proj, m_w_out), flat(v_w_sb_proj, v_w_hg_proj, v_w_out),
                                "adamw_sq")

    pad = jnp.zeros((8, HEAD_DIM), F32).at[0, 0].set(loss[0, 0])
    part = jnp.concatenate([pad] + [a.reshape(-1, HEAD_DIM) for a in (d_ng, d_bg, d_lb, d_hgn, d_fg)], axis=0)
    sums = _sum_small(part)
    loss_out, g_sm, d_sm, nm_sm, nv_sm = _adamw_small(
        sums, _pack_small(norm_g, b_gate, lb_logits, hg_norm_g, final_norm_g),
        _pack_small(m_norm_g, m_b_gate, m_lb_logits, m_hg_norm_g, m_final_norm_g),
        _pack_small(v_norm_g, v_b_gate, v_lb_logits, v_hg_norm_g, v_final_norm_g))

    def big(t_in, t_sq):
        sq = t_sq.reshape(3, 1, ROW_SHARD, D_MODEL)
        return t_in[None], sq[0], sq[1], sq[2]

    def order(small, in_, sb, hg, out):
        ng, bg, lbl, hgn, fg = small
        return [ng, in_, bg, lbl, hgn, sb, hg, out, fg]

    outs = [loss_out[0, 0], grad_x[None]]
    for small, (t_in, t_sq) in ((g_sm, (grad_in, grad_sq)), (d_sm, (d_in, d_sq)), (nm_sm, (nm_in, nm_sq)), (nv_sm, (nv_in, nv_sq))):
        outs += order(_unpack_small(small), *big(t_in, t_sq))
    return tuple(outs)
```

```python
import functools

import jax
import jax.numpy as jnp
from jax import lax
from jax.experimental import pallas as pl
from jax.experimental.pallas import tpu as pltpu

F32 = jnp.float32
BF16 = jnp.bfloat16

D_MODEL = 1024
HEADS = 8
HEAD_DIM = 128
IN_WIDTH = 10240
N_CHIPS = 4
W_IN_SHARD = IN_WIDTH // N_CHIPS
ROW_SHARD = D_MODEL // N_CHIPS
RMS_EPS = 1e-6

OFF_SB_Q, OFF_SB_K, OFF_SB_V, OFF_SB_Z = 0, 1024, 2048, 3072
OFF_HG_Q, OFF_HG_F, OFF_HG_I, OFF_HG_Z, OFF_GATE = 4096, 5120, 6144, 7168, 8192

QKV_COLS = 3840
SB_BLOCK = 256
SB_PAIR = 2
SB_ROWS = 256
SB_DEAD = -110.0
HG_CHUNK = 32
HG_STEP = 256
HG_MID = HG_CHUNK // 2 - 1

ADAM_LR, ADAM_B1, ADAM_B2, ADAM_EPS, ADAM_WD, ADAM_STEP = 0.001, 0.9, 0.999, 1e-08, 0.01, 10

VMEM_LIMIT = 56 * 1024 * 1024

MESH = pl.DeviceIdType.MESH


def _cparams(sem, vmem=VMEM_LIMIT):
    return pltpu.CompilerParams(dimension_semantics=sem, vmem_limit_bytes=vmem)


def _dot(a, b):
    return jnp.dot(a, b, preferred_element_type=F32)


def _dot_nt(a, b):
    return lax.dot_general(a, b, (((1,), (1,)), ((), ())), preferred_element_type=F32)


def _dot_tn(a, b):
    return lax.dot_general(a, b, (((0,), (0,)), ((), ())), preferred_element_type=F32)


def _split_dot(x, tri):
    hi = x.astype(BF16)
    lo = (x - hi.astype(F32)).astype(BF16)
    return _dot(hi, tri) + _dot(lo, tri)


def _split_dot_left(tri, x):
    hi = x.astype(BF16)
    lo = (x - hi.astype(F32)).astype(BF16)
    return _dot(tri, hi) + _dot(tri, lo)


def _sigmoid(x):
    return 1.0 / (1.0 + jnp.exp(-x))


def _inproj(x, norm_g, w4):
    s_len = x.shape[0]
    ts = min(1024, s_len)
    tn = QKV_COLS // 3
    per = W_IN_SHARD // tn

    def body(x_ref, g_ref, w_ref, proj_ref, ht_ref, qkv_ref, h_scr):
        n = pl.program_id(1)

        @pl.when(n == 0)
        def _():
            xv = x_ref[...]
            r = lax.rsqrt(jnp.mean(xv * xv, axis=-1, keepdims=True) + RMS_EPS)
            hv = (xv * r) * g_ref[...]
            h_scr[...] = hv.astype(BF16)
            ht_ref[...] = hv.T.astype(BF16)

        p = _dot(h_scr[...], w_ref[0])
        proj_ref[...] = p

        @pl.when(n < 3)
        def _():
            qkv_ref[...] = p.astype(BF16)

    return pl.pallas_call(
        body,
        name="inproj",
        grid=(s_len // ts, IN_WIDTH // tn),
        in_specs=[
            pl.BlockSpec((ts, D_MODEL), lambda s, n: (s, 0)),
            pl.BlockSpec((1, D_MODEL), lambda s, n: (0, 0)),
            pl.BlockSpec((1, D_MODEL, tn), lambda s, n: (n // per, 0, n % per)),
        ],
        out_specs=[
            pl.BlockSpec((ts, tn), lambda s, n: (s, n)),
            pl.BlockSpec((D_MODEL, ts), lambda s, n: (0, s)),
            pl.BlockSpec((ts, tn), lambda s, n: (s, jnp.minimum(n, 2))),
        ],
        out_shape=[
            jax.ShapeDtypeStruct((s_len, IN_WIDTH), F32),
            jax.ShapeDtypeStruct((D_MODEL, s_len), BF16),
            jax.ShapeDtypeStruct((s_len, QKV_COLS), BF16),
        ],
        scratch_shapes=[pltpu.VMEM((ts, D_MODEL), BF16)],
        compiler_params=_cparams(("arbitrary", "arbitrary")),
    )(x, norm_g, w4)


def _sb_tile_fwd(qb, kb, row_gt_col, tri_excl, carry, diag):
    scale = HEAD_DIM ** -0.5
    z = _dot_nt(qb, kb) * scale
    ls_pos = jnp.minimum(z, 0.0) - jnp.log1p(jnp.exp(-jnp.abs(z)))
    log_not = ls_pos - z
    log_not_m = jnp.where(row_gt_col, log_not, 0.0) if diag else log_not
    surv = _split_dot(log_not_m, tri_excl) + carry
    w = jnp.exp(ls_pos + surv)
    if diag:
        w = jnp.where(row_gt_col, w, 0.0)
    return ls_pos, log_not, log_not_m, surv, w


def _sb_specs(s_len, blk):
    width = SB_PAIR * HEAD_DIM

    def blk_spec(off):
        return pl.BlockSpec((blk, width), lambda h, i: (i, off // width + h))

    def head_spec(off):
        return pl.BlockSpec((s_len, width), lambda h, i: (0, off // width + h))

    return blk_spec, head_spec


def _head_cols(p):
    return slice(p * HEAD_DIM, (p + 1) * HEAD_DIM)


def _sb_chains(blk):
    rows = min(SB_ROWS, blk)
    return [(p, a) for p in range(SB_PAIR) for a in range(blk // rows)], rows


def _sb_masks(blk, rows):
    row = lax.broadcasted_iota(jnp.int32, (rows, blk), 0)
    col = lax.broadcasted_iota(jnp.int32, (rows, blk), 1)
    causal = [row + a * rows > col for a in range(blk // rows)]
    row = lax.broadcasted_iota(jnp.int32, (blk, blk), 0)
    col = lax.broadcasted_iota(jnp.int32, (blk, blk), 1)
    tri_excl = (row > col).astype(BF16)
    tri_incl = (row >= col).astype(BF16)
    return causal, tri_excl, tri_incl


def _sb_alive(st, n_chain):
    alive = functools.reduce(jnp.maximum, [st[1 + 3 * c] for c in range(n_chain)])
    return jnp.max(alive) > SB_DEAD


def _sb_fwd(qkv):
    s_len = qkv.shape[0]
    blk = min(SB_BLOCK, s_len)
    nq = s_len // blk
    chains, rows = _sb_chains(blk)

    def body(q_ref, k_ref, v_ref, o_ref, of_ref):
        i = pl.program_id(1)
        causal, tri_excl, _ = _sb_masks(blk, rows)

        def tile(j, st, diag):
            start = pl.multiple_of(j * blk, blk)
            new = []
            for c, (p, a) in enumerate(chains):
                carry, acc, acc_lo = st[3 * c : 3 * c + 3]
                kb = k_ref[pl.ds(start, blk), _head_cols(p)]
                vb = v_ref[pl.ds(start, blk), _head_cols(p)]
                qb = q_ref[a * rows : (a + 1) * rows, _head_cols(p)]
                _, _, log_not_m, surv, w = _sb_tile_fwd(qb, kb, causal[a], tri_excl, carry, diag)
                wb = w.astype(BF16)
                w_lo = (w - wb.astype(F32)).astype(BF16)
                new += [surv[:, 0:1] + log_not_m[:, 0:1], acc + _dot(wb, vb), acc_lo + _dot(w_lo, vb)]
            return tuple(new)

        zero = jnp.zeros((rows, HEAD_DIM), F32)
        st = tile(i, (jnp.zeros((rows, 1), F32), zero, zero) * len(chains), True)

        def more(st):
            return (st[0] < i) & _sb_alive(st, len(chains))

        def step(st):
            return (st[0] + 1,) + tile(i - 1 - st[0], st[1:], False)

        st = lax.while_loop(more, step, (0,) + st)[1:]
        for c, (p, a) in enumerate(chains):
            o_ref[a * rows : (a + 1) * rows, _head_cols(p)] = st[3 * c + 1]
            of_ref[a * rows : (a + 1) * rows, _head_cols(p)] = st[3 * c + 1] + st[3 * c + 2]

    blk_spec, head_spec = _sb_specs(s_len, blk)
    return pl.pallas_call(
        body,
        name="sb_fwd",
        grid=(HEADS // SB_PAIR, nq),
        in_specs=[blk_spec(OFF_SB_Q), head_spec(OFF_SB_K), head_spec(OFF_SB_V)],
        out_specs=[blk_spec(0), blk_spec(0)],
        out_shape=[jax.ShapeDtypeStruct((s_len, D_MODEL), F32)] * 2,
        compiler_params=_cparams(("arbitrary", "arbitrary")),
    )(qkv, qkv, qkv)


def _sb_bwd(qkv, o_fine, d_o):
    s_len = qkv.shape[0]
    blk = min(SB_BLOCK, s_len)
    nq = s_len // blk
    scale = HEAD_DIM ** -0.5
    chains, rows = _sb_chains(blk)

    def body(q_ref, k_ref, v_ref, of_ref, do_ref, dq_ref, dk_ref, dv_ref, dk_acc, dv_acc):
        i = pl.program_id(1)

        @pl.when(i == 0)
        def _():
            dk_acc[...] = jnp.zeros_like(dk_acc)
            dv_acc[...] = jnp.zeros_like(dv_acc)

        dob = do_ref[...].astype(BF16)
        prod = dob.astype(F32) * of_ref[...]
        causal, tri_excl, tri_incl = _sb_masks(blk, rows)

        def group(x, p, a):
            return x[a * rows : (a + 1) * rows, _head_cols(p)]

        totals = [jnp.sum(group(prod, p, a), axis=-1, keepdims=True) for p, a in chains]

        def tile(j, st, diag):
            start = pl.multiple_of(j * blk, blk)
            new = []
            dk_new = [None] * SB_PAIR
            dv_new = [None] * SB_PAIR
            for c, (p, a) in enumerate(chains):
                c_not, c_dlw, dq = st[3 * c : 3 * c + 3]
                qb, dob_c = group(q_ref, p, a), group(dob, p, a)
                kb = k_ref[pl.ds(start, blk), _head_cols(p)]
                vb = v_ref[pl.ds(start, blk), _head_cols(p)]
                ls_pos, log_not, log_not_m, surv, w = _sb_tile_fwd(qb, kb, causal[a], tri_excl, c_not, diag)
                dlw = _dot_nt(dob_c, vb) * w
                suffix = _split_dot(dlw, tri_incl)
                d_not = totals[c] - c_dlw - suffix
                dz = (dlw * jnp.exp(log_not) - d_not * jnp.exp(ls_pos)) * scale
                if diag:
                    dz = jnp.where(causal[a], dz, 0.0)
                dzb = dz.astype(BF16)
                dk_c, dv_c = _dot_tn(dzb, qb), _dot_tn(w.astype(BF16), dob_c)
                dk_new[p] = dk_c if dk_new[p] is None else dk_new[p] + dk_c
                dv_new[p] = dv_c if dv_new[p] is None else dv_new[p] + dv_c
                new += [surv[:, 0:1] + log_not_m[:, 0:1], c_dlw + suffix[:, 0:1], dq + _dot(dzb, kb)]
            for p in range(SB_PAIR):
                dk_acc[pl.ds(start, blk), _head_cols(p)] += dk_new[p]
                dv_acc[pl.ds(start, blk), _head_cols(p)] += dv_new[p]
            return tuple(new)

        zcol = jnp.zeros((rows, 1), F32)
        st = tile(i, (zcol, zcol, jnp.zeros((rows, HEAD_DIM), F32)) * len(chains), True)

        def more(st):
            return (st[0] < i) & _sb_alive(st, len(chains))

        def step(st):
            return (st[0] + 1,) + tile(i - 1 - st[0], st[1:], False)

        st = lax.while_loop(more, step, (0,) + st)[1:]
        for c, (p, a) in enumerate(chains):
            dq_ref[a * rows : (a + 1) * rows, _head_cols(p)] = st[3 * c + 2].astype(BF16)

        @pl.when(i == nq - 1)
        def _():
            dk_ref[...] = dk_acc[...].astype(BF16)
            dv_ref[...] = dv_acc[...].astype(BF16)

    blk_spec, head_spec = _sb_specs(s_len, blk)
    width = SB_PAIR * HEAD_DIM
    return pl.pallas_call(
        body,
        name="sb_bwd",
        grid=(HEADS // SB_PAIR, nq),
        in_specs=[blk_spec(OFF_SB_Q), head_spec(OFF_SB_K), head_spec(OFF_SB_V), blk_spec(0), blk_spec(0)],
        out_specs=[blk_spec(0), head_spec(0), head_spec(0)],
        out_shape=[jax.ShapeDtypeStruct((s_len, D_MODEL), BF16)] * 3,
        scratch_shapes=[pltpu.VMEM((s_len, width), F32), pltpu.VMEM((s_len, width), F32)],
        compiler_params=_cparams(("arbitrary", "arbitrary")),
    )(qkv, qkv, qkv, o_fine, d_o)


def _hg_lower_bound(lbl_ref):
    l0 = lbl_ref[0:1, :]
    l1 = lbl_ref[1:2, :]
    mx = jnp.maximum(l0, l1)
    e0 = jnp.exp(l0 - mx)
    e1 = jnp.exp(l1 - mx)
    return e0 / (e0 + e1)


def _hg_gates(hq, hf, lb):
    sig_f = _sigmoid(hf)
    f = lb + (1.0 - lb) * sig_f
    g = jnp.log(f)
    kk = 1.0 - f
    sig_q = _sigmoid(hq)
    qq = hq * sig_q
    return qq, kk, g, f, sig_f, sig_q


def _chunk_bcast(x, r, rows):
    w = x.shape[-1]
    x3 = x.reshape(rows // HG_CHUNK, HG_CHUNK, w)
    return jnp.broadcast_to(x3[:, r : r + 1, :], x3.shape).reshape(rows, w)


def _hg_decays(qq, kk, g, tri_blk, rows):
    cum = _split_dot_left(tri_blk, g)
    mid = _chunk_bcast(cum, HG_MID, rows)
    last = _chunk_bcast(cum, HG_CHUNK - 1, rows)
    e_qm = jnp.exp(cum - mid)
    e_km = jnp.exp(mid - cum)
    e_q = jnp.exp(cum)
    e_kl = jnp.exp(last - cum)
    return cum, last, e_qm, e_km, e_q, e_kl


def _blockdiag(rows, kind):
    row = lax.broadcasted_iota(jnp.int32, (rows, rows), 0)
    col = lax.broadcasted_iota(jnp.int32, (rows, rows), 1)
    keep = (row // HG_CHUNK) == (col // HG_CHUNK)
    if kind == "lower":
        keep = keep & (row >= col)
    elif kind == "upper":
        keep = keep & (row <= col)
    return jnp.where(keep, 1.0, 0.0).astype(BF16)


def _hg_fwd(proj, lbl):
    s_len = proj.shape[0]
    rows = min(HG_STEP, s_len)
    n_chunks = rows // HG_CHUNK

    def body(hq_ref, hf_ref, hi_ref, lbl_ref, o_ref, st_ref, state, q_mid, k_mid, q_dec, k_last, v_b):
        @pl.when(pl.program_id(0) == 0)
        def _():
            state[...] = jnp.zeros_like(state)

        lb = _hg_lower_bound(lbl_ref)
        qq, kk, g, _, _, _ = _hg_gates(hq_ref[...], hf_ref[...], lb)
        tri_blk = _blockdiag(rows, "lower")
        _, last, e_qm, e_km, e_q, e_kl = _hg_decays(qq, kk, g, tri_blk, rows)
        q_mid[...] = (qq * e_qm).astype(BF16)
        k_mid[...] = (kk * e_km).astype(BF16)
        q_dec[...] = (qq * e_q).astype(BF16)
        k_last[...] = (kk * e_kl).astype(BF16)
        v_b[...] = hi_ref[...].astype(BF16)
        e_last = jnp.exp(last)
        row = lax.broadcasted_iota(jnp.int32, (HG_CHUNK, HG_CHUNK), 0)
        col = lax.broadcasted_iota(jnp.int32, (HG_CHUNK, HG_CHUNK), 1)
        causal = row >= col

        for c in range(n_chunks):
            r0 = c * HG_CHUNK
            for h in range(HEADS):
                c0 = h * HEAD_DIM
                sl = (slice(r0, r0 + HG_CHUNK), slice(c0, c0 + HEAD_DIM))
                st = state[h]
                st_ref[c, h] = st
                a = jnp.where(causal, _dot_nt(q_mid[sl], k_mid[sl]), 0.0)
                vb = v_b[sl]
                o_ref[sl] = _dot(a.astype(BF16), vb) + _dot_nt(q_dec[sl], st.astype(BF16))
                decay = e_last[r0 : r0 + 1, c0 : c0 + HEAD_DIM]
                state[h] = st * decay + _dot_tn(vb, k_last[sl])

    def col_spec(off):
        return pl.BlockSpec((rows, D_MODEL), lambda s: (s, off // D_MODEL))

    scratch = [pltpu.VMEM((HEADS, HEAD_DIM, HEAD_DIM), F32)] + [pltpu.VMEM((rows, D_MODEL), BF16)] * 5
    return pl.pallas_call(
        body,
        name="hg_fwd",
        grid=(s_len // rows,),
        in_specs=[col_spec(OFF_HG_Q), col_spec(OFF_HG_F), col_spec(OFF_HG_I), pl.BlockSpec((2, D_MODEL), lambda s: (0, 0))],
        out_specs=[
            pl.BlockSpec((rows, D_MODEL), lambda s: (s, 0)),
            pl.BlockSpec((n_chunks, HEADS, HEAD_DIM, HEAD_DIM), lambda s: (s, 0, 0, 0)),
        ],
        out_shape=[
            jax.ShapeDtypeStruct((s_len, D_MODEL), F32),
            jax.ShapeDtypeStruct((s_len // HG_CHUNK, HEADS, HEAD_DIM, HEAD_DIM), F32),
        ],
        scratch_shapes=scratch,
        compiler_params=_cparams(("arbitrary",)),
    )(proj, proj, proj, lbl)


def _hg_bwd(proj, lbl, states, d_o):
    s_len = proj.shape[0]
    rows = min(HG_STEP, s_len)
    n_chunks = rows // HG_CHUNK
    n_steps = s_len // rows

    def body(hq_ref, hf_ref, hi_ref, lbl_ref, st_ref, do_ref, dp_ref, dlb_ref,
             dstate, q_mid, k_mid, q_dec, k_last, v_b, do_b, d_qm, d_km, d_qd, d_kl, d_v, d_last):
        @pl.when(pl.program_id(0) == 0)
        def _():
            dstate[...] = jnp.zeros_like(dstate)
            dlb_ref[...] = jnp.zeros_like(dlb_ref)

        lb = _hg_lower_bound(lbl_ref)
        hq = hq_ref[...]
        qq, kk, g, f, sig_f, sig_q = _hg_gates(hq, hf_ref[...], lb)
        tri_blk = _blockdiag(rows, "lower")
        _, last, e_qm, e_km, e_q, e_kl = _hg_decays(qq, kk, g, tri_blk, rows)
        qm, km, qd, kl = qq * e_qm, kk * e_km, qq * e_q, kk * e_kl
        q_mid[...] = qm.astype(BF16)
        k_mid[...] = km.astype(BF16)
        q_dec[...] = qd.astype(BF16)
        k_last[...] = kl.astype(BF16)
        v_b[...] = hi_ref[...].astype(BF16)
        do_b[...] = do_ref[...].astype(BF16)
        e_last = jnp.exp(last)
        row = lax.broadcasted_iota(jnp.int32, (HG_CHUNK, HG_CHUNK), 0)
        col = lax.broadcasted_iota(jnp.int32, (HG_CHUNK, HG_CHUNK), 1)
        causal = row >= col

        for c in reversed(range(n_chunks)):
            r0 = c * HG_CHUNK
            for h in range(HEADS):
                c0 = h * HEAD_DIM
                sl = (slice(r0, r0 + HG_CHUNK), slice(c0, c0 + HEAD_DIM))
                st0 = st_ref[c, h]
                ds1 = dstate[h]
                ds1b = ds1.astype(BF16)
                dob, vb, qmb, kmb = do_b[sl], v_b[sl], q_mid[sl], k_mid[sl]
                a = jnp.where(causal, _dot_nt(qmb, kmb), 0.0).astype(BF16)
                da = jnp.where(causal, _dot_nt(dob, vb), 0.0).astype(BF16)
                d_v[sl] = _dot_tn(a, dob) + _dot_nt(k_last[sl], ds1b)
                d_qm[sl] = _dot(da, kmb)
                d_km[sl] = _dot_tn(da, qmb)
                d_qd[sl] = _dot(dob, st0.astype(BF16))
                d_kl[sl] = _dot(vb, ds1b)
                decay = e_last[r0 : r0 + 1, c0 : c0 + HEAD_DIM]
                d_last[c : c + 1, c0 : c0 + HEAD_DIM] = decay * jnp.sum(ds1 * st0, axis=0, keepdims=True)
                dstate[h] = ds1 * decay + _dot_tn(dob, q_dec[sl])

        dqm, dkm, dqd, dkl = d_qm[...], d_km[...], d_qd[...], d_kl[...]
        dq = dqm * e_qm + dqd * e_q
        dk = dkm * e_km + dkl * e_kl
        t_kl = dkl * kl
        dcum = dqm * qm - dkm * km + dqd * qd - t_kl
        dl = d_last[...]
        dl_b = jnp.broadcast_to(dl[:, None, :], (n_chunks, HG_CHUNK, D_MODEL)).reshape(rows, D_MODEL)
        dg = _split_dot_left(_blockdiag(rows, "upper"), dcum) + _split_dot_left(_blockdiag(rows, "all"), t_kl) + dl_b
        df = dg / f - dk
        one_m = 1.0 - sig_f
        dp_ref[:, 0:D_MODEL] = (dq * (sig_q * (1.0 + hq * (1.0 - sig_q)))).astype(BF16)
        dp_ref[:, D_MODEL : 2 * D_MODEL] = (df * (1.0 - lb) * sig_f * one_m).astype(BF16)
        dp_ref[:, 2 * D_MODEL : 3 * D_MODEL] = d_v[...].astype(BF16)
        dlb_ref[...] += jnp.sum(df * one_m, axis=0, keepdims=True)

    def col_spec(off):
        return pl.BlockSpec((rows, D_MODEL), lambda s: (n_steps - 1 - s, off // D_MODEL))

    f32_tile = pltpu.VMEM((rows, D_MODEL), F32)
    bf_tile = pltpu.VMEM((rows, D_MODEL), BF16)
    scratch = [pltpu.VMEM((HEADS, HEAD_DIM, HEAD_DIM), F32)] + [bf_tile] * 6 + [f32_tile] * 5
    scratch += [pltpu.VMEM((n_chunks, D_MODEL), F32)]
    return pl.pallas_call(
        body,
        name="hg_bwd",
        grid=(n_steps,),
        in_specs=[
            col_spec(OFF_HG_Q), col_spec(OFF_HG_F), col_spec(OFF_HG_I),
            pl.BlockSpec((2, D_MODEL), lambda s: (0, 0)),
            pl.BlockSpec((n_chunks, HEADS, HEAD_DIM, HEAD_DIM), lambda s: (n_steps - 1 - s, 0, 0, 0)),
            pl.BlockSpec((rows, D_MODEL), lambda s: (n_steps - 1 - s, 0)),
        ],
        out_specs=[
            pl.BlockSpec((rows, 3 * D_MODEL), lambda s: (n_steps - 1 - s, 0)),
            pl.BlockSpec((1, D_MODEL), lambda s: (0, 0)),
        ],
        out_shape=[
            jax.ShapeDtypeStruct((s_len, 3 * D_MODEL), BF16),
            jax.ShapeDtypeStruct((1, D_MODEL), F32),
        ],
        scratch_shapes=scratch,
        compiler_params=_cparams(("arbitrary",)),
    )(proj, proj, proj, lbl, states, d_o)


def _mid(proj, sb_o, hg_o, x, target, b_gate, hg_gain, final_g, w_sb, w_hg, w_out):
    s_len = proj.shape[0]
    ts = min(128, s_len)
    inv_d = 1.0 / D_MODEL

    def body(zsb_ref, hz_ref, gl_ref, sbo_ref, hgo_ref, x_ref, tgt_ref, bg_ref, hgn_ref, fg_ref,
             wsb_ref, whg_ref, wout_ref,
             dout_ref, dsbo_ref, dhgo_ref, dzsb_ref, dhz_ref, dgl_ref,
             asb_ref, dusb_ref, ahg_ref, duhg_ref, y_ref, doutb_ref,
             loss_ref, dfg_ref, dbg_ref, dhgn_ref):
        @pl.when(pl.program_id(0) == 0)
        def _():
            loss_ref[...] = jnp.zeros_like(loss_ref)
            dfg_ref[...] = jnp.zeros_like(dfg_ref)
            dbg_ref[...] = jnp.zeros_like(dbg_ref)
            dhgn_ref[...] = jnp.zeros_like(dhgn_ref)

        z_sb = zsb_ref[...]
        sb_o = sbo_ref[...]
        sig_zsb = _sigmoid(z_sb)
        silu_zsb = z_sb * sig_zsb
        a_sb = (sb_o * silu_zsb).astype(BF16)
        u_sb = _dot(a_sb, wsb_ref[...])

        hg_o = hgo_ref[...]
        gain = hgn_ref[...]
        r_parts, yn_parts = [], []
        for h in range(HEADS):
            oh = hg_o[:, h * HEAD_DIM : (h + 1) * HEAD_DIM]
            r = lax.rsqrt(jnp.mean(oh * oh, axis=-1, keepdims=True) + RMS_EPS)
            r_parts.append(jnp.broadcast_to(r, oh.shape))
            yn_parts.append(oh * r)
        r_hg = jnp.concatenate(r_parts, axis=-1)
        yn_hg = jnp.concatenate(yn_parts, axis=-1)
        hn = yn_hg * gain
        hz = hz_ref[...]
        sig_hz = _sigmoid(hz)
        silu_hz = hz * sig_hz
        a_hg = (hn * silu_hz).astype(BF16)
        u_hg = _dot(a_hg, whg_ref[...])

        gates = _sigmoid(gl_ref[...] + bg_ref[...])
        g_sb = gates[:, 0:D_MODEL]
        g_hg = gates[:, D_MODEL:]
        y = (g_sb * u_sb + g_hg * u_hg).astype(BF16)
        out = x_ref[...] + _dot(y, wout_ref[...])
        r2 = lax.rsqrt(jnp.mean(out * out, axis=-1, keepdims=True) + RMS_EPS)
        yn = out * r2
        fg = fg_ref[...]
        diff = yn * fg - tgt_ref[...]
        loss_ref[...] += 0.5 * inv_d * jnp.sum(diff * diff)

        dyf = diff * inv_d
        dfg_ref[...] += jnp.sum(dyf * yn, axis=0, keepdims=True)
        dyn = dyf * fg
        dout = r2 * (dyn - yn * jnp.mean(dyn * yn, axis=-1, keepdims=True))
        dout_ref[...] = dout
        doutb = dout.astype(BF16)
        doutb_ref[...] = doutb
        dy = _dot_nt(doutb, wout_ref[...])
        du_sb = (dy * g_sb).astype(BF16)
        du_hg = (dy * g_hg).astype(BF16)
        dgl_sb = dy * u_sb * g_sb * (1.0 - g_sb)
        dgl_hg = dy * u_hg * g_hg * (1.0 - g_hg)
        dgl_ref[:, 0:D_MODEL] = dgl_sb.astype(BF16)
        dgl_ref[:, D_MODEL:] = dgl_hg.astype(BF16)
        dbg_ref[:, 0:D_MODEL] += jnp.sum(dgl_sb, axis=0, keepdims=True)
        dbg_ref[:, D_MODEL:] += jnp.sum(dgl_hg, axis=0, keepdims=True)

        da_sb = _dot_nt(du_sb, wsb_ref[...])
        dsbo_ref[...] = da_sb * silu_zsb
        dzsb_ref[...] = (da_sb * sb_o * (sig_zsb * (1.0 + z_sb * (1.0 - sig_zsb)))).astype(BF16)

        da_hg = _dot_nt(du_hg, whg_ref[...])
        dhn = da_hg * silu_hz
        dhz_ref[...] = (da_hg * hn * (sig_hz * (1.0 + hz * (1.0 - sig_hz)))).astype(BF16)
        dhgn_ref[...] += jnp.sum(dhn * yn_hg, axis=0, keepdims=True)
        dyn_hg = dhn * gain
        prod = dyn_hg * yn_hg
        m_parts = []
        for h in range(HEADS):
            ph = prod[:, h * HEAD_DIM : (h + 1) * HEAD_DIM]
            m_parts.append(jnp.broadcast_to(jnp.mean(ph, axis=-1, keepdims=True), ph.shape))
        dhgo_ref[...] = r_hg * (dyn_hg - yn_hg * jnp.concatenate(m_parts, axis=-1))

        asb_ref[...] = a_sb
        dusb_ref[...] = du_sb
        ahg_ref[...] = a_hg
        duhg_ref[...] = du_hg
        y_ref[...] = y

    def tile(width, off=0):
        return pl.BlockSpec((ts, width), lambda s: (s, off // width))

    def whole(shape):
        return pl.BlockSpec(shape, lambda s: (0,) * len(shape))

    sq = (D_MODEL, D_MODEL)
    f32_act = jax.ShapeDtypeStruct((s_len, D_MODEL), F32)
    bf_act = jax.ShapeDtypeStruct((s_len, D_MODEL), BF16)
    return pl.pallas_call(
        body,
        name="mid",
        grid=(s_len // ts,),
        in_specs=[
            tile(D_MODEL, OFF_SB_Z), tile(D_MODEL, OFF_HG_Z), tile(2 * D_MODEL, OFF_GATE),
            tile(D_MODEL), tile(D_MODEL), tile(D_MODEL), tile(D_MODEL),
            whole((1, 2 * D_MODEL)), whole((1, D_MODEL)), whole((1, D_MODEL)),
            whole(sq), whole(sq), whole(sq),
        ],
        out_specs=[
            tile(D_MODEL), tile(D_MODEL), tile(D_MODEL), tile(D_MODEL), tile(D_MODEL), tile(2 * D_MODEL),
            tile(D_MODEL), tile(D_MODEL), tile(D_MODEL), tile(D_MODEL), tile(D_MODEL), tile(D_MODEL),
            whole((1, 1)), whole((1, D_MODEL)), whole((1, 2 * D_MODEL)), whole((1, D_MODEL)),
        ],
        out_shape=[
            f32_act, f32_act, f32_act, bf_act, bf_act, jax.ShapeDtypeStruct((s_len, 2 * D_MODEL), BF16),
            bf_act, bf_act, bf_act, bf_act, bf_act, bf_act,
            jax.ShapeDtypeStruct((1, 1), F32), jax.ShapeDtypeStruct((1, D_MODEL), F32),
            jax.ShapeDtypeStruct((1, 2 * D_MODEL), F32), jax.ShapeDtypeStruct((1, D_MODEL), F32),
        ],
        compiler_params=_cparams(("arbitrary",)),
    )(proj, proj, proj, sb_o, hg_o, x, target, b_gate, hg_gain, final_g, w_sb, w_hg, w_out)


def _grad_matmul(a, b, name, tn):
    s_len, m = a.shape
    n = b.shape[1]
    tk = min(512, s_len)

    def body(a_ref, b_ref, o_ref):
        @pl.when(pl.program_id(1) == 0)
        def _():
            o_ref[...] = jnp.zeros_like(o_ref)

        o_ref[...] += _dot_tn(a_ref[...], b_ref[...])

    return pl.pallas_call(
        body,
        name=name,
        grid=(n // tn, s_len // tk),
        in_specs=[pl.BlockSpec((tk, m), lambda j, k: (k, 0)), pl.BlockSpec((tk, tn), lambda j, k: (k, j))],
        out_specs=pl.BlockSpec((m, tn), lambda j, k: (0, j)),
        out_shape=jax.ShapeDtypeStruct((m, n), F32),
        compiler_params=_cparams(("arbitrary", "arbitrary")),
    )(a, b)


SEG_WIDTHS = (1024, 1024, 1024, 1024, 3072, 1024, 2048)


def _seg_bounds(tile):
    bounds = [0]
    for w in SEG_WIDTHS:
        bounds.append(bounds[-1] + w // tile)
    return bounds


def _grad_w_in(h_t, segs):
    m, s_len = h_t.shape
    tk = min(1024, s_len)
    tn = 1024
    nk = s_len // tk
    bounds = _seg_bounds(tn)

    def body(a_ref, *refs):
        seg_refs, o_ref = refs[:-1], refs[-1]
        j = pl.program_id(0)

        @pl.when(pl.program_id(1) == 0)
        def _():
            o_ref[...] = jnp.zeros_like(o_ref)

        for i, ref in enumerate(seg_refs):
            @pl.when((j >= bounds[i]) & (j < bounds[i + 1]))
            def _(ref=ref):
                o_ref[...] += _dot(a_ref[...], ref[...])

    def seg_spec(lo, hi):
        def index(j, k):
            return (jnp.where(j < lo, 0, jnp.where(j >= hi, nk - 1, k)), jnp.clip(j - lo, 0, hi - lo - 1))
        return pl.BlockSpec((tk, tn), index)

    return pl.pallas_call(
        body,
        name="grad_w_in",
        grid=(IN_WIDTH // tn, nk),
        in_specs=[pl.BlockSpec((m, tk), lambda j, k: (0, k))] + [seg_spec(bounds[i], bounds[i + 1]) for i in range(7)],
        out_specs=pl.BlockSpec((m, tn), lambda j, k: (0, j)),
        out_shape=jax.ShapeDtypeStruct((m, IN_WIDTH), F32),
        compiler_params=_cparams(("arbitrary", "arbitrary")),
    )(h_t, *segs)


EXCHANGE_IN_PIECES = 8
EXCHANGE_PIECES = EXCHANGE_IN_PIECES + 3


def _exchange_copies(sin_ref, ssq_ref, got_in, got_sq, send_sems, recv_sems):
    _, _, c, chips = _position()
    rows = HALF_IN // EXCHANGE_IN_PIECES
    copies = []
    for k, (px, py) in enumerate(chips):
        chip = 2 * px + py
        for p in range(EXCHANGE_PIECES):
            if p < EXCHANGE_IN_PIECES:
                src, dst = sin_ref.at[chip, pl.ds(p * rows, rows), :], got_in.at[k, pl.ds(p * rows, rows), :]
            else:
                src, dst = ssq_ref.at[p - EXCHANGE_IN_PIECES, chip], got_sq.at[k, p - EXCHANGE_IN_PIECES]
            copies.append(_remote(src, dst, send_sems.at[k, p], recv_sems.at[k, p], (px, py, c)))
    return copies


def _dx(segs, w4, x, norm_g, dout, s_in, s_sq):
    s_len = x.shape[0]
    ts = min(1024, s_len)
    tk = 512
    per = W_IN_SHARD // tk
    nk = IN_WIDTH // tk
    ns = s_len // ts
    bounds = _seg_bounds(tk)

    def body(*refs):
        seg_refs = refs[:7]
        w_ref, x_ref, g_ref, dout_ref, sin_ref, ssq_ref, gx_ref, dg_ref, got_in, got_sq, acc, send_sems, recv_sems = refs[7:]
        s, k = pl.program_id(0), pl.program_id(1)

        @pl.when((s == 0) & (k == 0))
        def _():
            dg_ref[...] = jnp.zeros_like(dg_ref)
            for cp in _exchange_copies(sin_ref, ssq_ref, got_in, got_sq, send_sems, recv_sems):
                cp.start()

        @pl.when(k == 0)
        def _():
            acc[...] = jnp.zeros_like(acc)

        for i, ref in enumerate(seg_refs):
            @pl.when((k >= bounds[i]) & (k < bounds[i + 1]))
            def _(ref=ref):
                acc[...] += _dot_nt(ref[...], w_ref[0])

        @pl.when(k == nk - 1)
        def _():
            dh = acc[...]
            xv = x_ref[...]
            r = lax.rsqrt(jnp.mean(xv * xv, axis=-1, keepdims=True) + RMS_EPS)
            xn = xv * r
            dg_ref[...] += jnp.sum(dh * xn, axis=0, keepdims=True)
            dxn = dh * g_ref[...]
            gx_ref[...] = r * (dxn - xn * jnp.mean(dxn * xn, axis=-1, keepdims=True)) + dout_ref[...]

        @pl.when((s == ns - 1) & (k == nk - 1))
        def _():
            for cp in _exchange_copies(sin_ref, ssq_ref, got_in, got_sq, send_sems, recv_sems):
                cp.wait()

    def seg_spec(lo, hi):
        return pl.BlockSpec((ts, tk), lambda s, k: (s, jnp.clip(k - lo, 0, hi - lo - 1)))

    row_tile = pl.BlockSpec((ts, D_MODEL), lambda s, k: (s, 0))
    vec = pl.BlockSpec((1, D_MODEL), lambda s, k: (0, 0))
    return pl.pallas_call(
        body,
        name="dx",
        grid=(ns, nk),
        in_specs=[seg_spec(bounds[i], bounds[i + 1]) for i in range(7)] + [
            pl.BlockSpec((1, D_MODEL, tk), lambda s, k: (k // per, 0, k % per)),
            row_tile, vec, row_tile, ANY, ANY,
        ],
        out_specs=[row_tile, vec, ANY, ANY],
        out_shape=[jax.ShapeDtypeStruct((s_len, D_MODEL), F32), jax.ShapeDtypeStruct((1, D_MODEL), F32),
                   jax.ShapeDtypeStruct((3, HALF_IN, W_IN_SHARD), WIRE),
                   jax.ShapeDtypeStruct((3, 3, HALF_SQ, D_MODEL), WIRE)],
        scratch_shapes=[pltpu.VMEM((ts, D_MODEL), F32),
                        pltpu.SemaphoreType.DMA((3, EXCHANGE_PIECES)), pltpu.SemaphoreType.DMA((3, EXCHANGE_PIECES))],
        compiler_params=_cparams(("arbitrary", "arbitrary")),
    )(*segs, w4, x, norm_g, dout, s_in, s_sq)


def _local_grads(x, target, norm_g, b_gate, lbl, hg_gain, final_g, w4, w_sb, w_hg, w_out):
    proj, h_t, qkv = _inproj(x, norm_g, w4)
    sb_o, sb_o_fine = _sb_fwd(qkv)
    hg_o, states = _hg_fwd(proj, lbl)
    (dout, d_sbo, d_hgo, d_zsb, d_hz, d_gl, a_sb, du_sb, a_hg, du_hg, y, doutb,
     loss, d_fg, d_bg, d_hgn) = _mid(proj, sb_o, hg_o, x, target, b_gate, hg_gain, final_g, w_sb, w_hg, w_out)
    g_w_sb = _grad_matmul(a_sb, du_sb, "grad_w_sb", 512)
    g_w_hg = _grad_matmul(a_hg, du_hg, "grad_w_hg", 512)
    g_w_out = _grad_matmul(y, doutb, "grad_w_out", 512)
    d_q, d_k, d_v = _sb_bwd(qkv, sb_o_fine, d_sbo)
    d_hg, d_lb = _hg_bwd(proj, lbl, states, d_hgo)
    segs = (d_q, d_k, d_v, d_zsb, d_hg, d_hz, d_gl)
    g_w_in = _grad_w_in(h_t, segs)
    return g_w_in, g_w_sb, g_w_hg, g_w_out, segs, dout, loss, d_bg, d_lb, d_hgn, d_fg


ANY = pl.BlockSpec(memory_space=pl.ANY)
WIRE = BF16
HALF_IN = D_MODEL // 2
HALF_SQ = ROW_SHARD // 2


def _position():
    x, y, c = lax.axis_index("x"), lax.axis_index("y"), lax.axis_index("c")
    chips = [(1 - x, y), (x, 1 - y), (1 - x, 1 - y)]
    return x, y, c, chips


def _remote(src, dst, send_sem, recv_sem, to):
    return pltpu.make_async_remote_copy(src_ref=src, dst_ref=dst, send_sem=send_sem, recv_sem=recv_sem,
                                        device_id=to, device_id_type=MESH)


def _gather_weights(w_in_b, w_sq_b):
    n_in = 4
    n_piece = n_in + 3
    rows = HALF_IN // n_in

    def body(win_ref, wsq_ref, in_ref, sq_ref, send_sems, recv_sems):
        x, y, c, chips = _position()
        me = 2 * x + y
        sibling = (x, y, 1 - c)

        def src_piece(p):
            if p < n_in:
                return win_ref.at[pl.ds(c * HALF_IN + p * rows, rows), :]
            return wsq_ref.at[p - n_in, pl.ds(c * HALF_SQ, HALF_SQ), :]

        def piece(p, chip, core):
            if p < n_in:
                return in_ref.at[chip, pl.ds(core * HALF_IN + p * rows, rows), :]
            return sq_ref.at[p - n_in, chip, pl.ds(core * HALF_SQ, HALF_SQ), :]

        sends = []
        for k, (px, py) in enumerate(chips):
            for p in range(n_piece):
                sends.append(_remote(src_piece(p), piece(p, me, c), send_sems.at[k, p], recv_sems.at[k, p], (px, py, c)))
        for cp in sends:
            cp.start()
        for k, (px, py) in enumerate(chips):
            chip = 2 * px + py
            for p in range(n_piece):
                got = piece(p, chip, c)
                _remote(got, got, send_sems.at[k, p], recv_sems.at[k, p], (px, py, c)).wait_recv()
                fwd = _remote(got, got, send_sems.at[3 + k, p], recv_sems.at[3 + k, p], sibling)
                fwd.start()
                sends.append(fwd)
        for k, (px, py) in enumerate(chips):
            chip = 2 * px + py
            for p in range(n_piece):
                got = piece(p, chip, 1 - c)
                _remote(got, got, send_sems.at[3 + k, p], recv_sems.at[3 + k, p], sibling).wait_recv()
        for cp in sends:
            cp.wait_send()

    return pl.pallas_call(
        body,
        name="gather_weights",
        in_specs=[ANY, ANY],
        out_specs=[ANY, ANY],
        out_shape=[jax.ShapeDtypeStruct((N_CHIPS, D_MODEL, W_IN_SHARD), BF16),
                   jax.ShapeDtypeStruct((3, N_CHIPS, ROW_SHARD, D_MODEL), BF16)],
        scratch_shapes=[pltpu.SemaphoreType.DMA((6, n_piece)), pltpu.SemaphoreType.DMA((6, n_piece))],
    )(w_in_b, w_sq_b)


def _place_own(idx, w_in_b, w_sq_b, w4, wsq):
    n = 4
    r_in, r_sq = D_MODEL // n, ROW_SHARD // n

    def body(idx_ref, win_ref, wsq_ref, w4_in, wsq_in, w4_out, wsq_out):
        w4_out[0] = win_ref[...]
        wsq_out[:, 0] = wsq_ref[...]

    grid_spec = pltpu.PrefetchScalarGridSpec(
        num_scalar_prefetch=1,
        grid=(n,),
        in_specs=[pl.BlockSpec((r_in, W_IN_SHARD), lambda r, idx: (r, 0)),
                  pl.BlockSpec((3, r_sq, D_MODEL), lambda r, idx: (0, r, 0)), ANY, ANY],
        out_specs=[pl.BlockSpec((1, r_in, W_IN_SHARD), lambda r, idx: (idx[0], r, 0)),
                   pl.BlockSpec((3, 1, r_sq, D_MODEL), lambda r, idx: (0, idx[0], r, 0))],
    )
    return pl.pallas_call(
        body,
        name="place_own",
        grid_spec=grid_spec,
        out_shape=[jax.ShapeDtypeStruct(w4.shape, BF16), jax.ShapeDtypeStruct(wsq.shape, BF16)],
        input_output_aliases={3: 0, 4: 1},
        compiler_params=_cparams(("arbitrary",)),
    )(idx, w_in_b, w_sq_b, w4, wsq)


def _swap_halves(g_in, g_sq):
    n_in = 16
    n_piece = n_in + 3 * N_CHIPS
    rows = HALF_IN // n_in

    def body(gin_ref, gsq_ref, got_in, got_sq, send_sems, recv_sems):
        x, y, c, _ = _position()
        sibling = (x, y, 1 - c)

        def src_piece(p):
            if p < n_in:
                return gin_ref.at[pl.ds((1 - c) * HALF_IN + p * rows, rows), :]
            a, chip = divmod(p - n_in, N_CHIPS)
            return gsq_ref.at[a, chip, pl.ds((1 - c) * HALF_SQ, HALF_SQ), :]

        def dst_piece(p):
            if p < n_in:
                return got_in.at[pl.ds(p * rows, rows), :]
            a, chip = divmod(p - n_in, N_CHIPS)
            return got_sq.at[a, chip]

        out = [_remote(src_piece(p), dst_piece(p), send_sems.at[p], recv_sems.at[p], sibling) for p in range(n_piece)]
        for cp in out:
            cp.start()
        for cp in out:
            cp.wait()

    return pl.pallas_call(
        body,
        name="swap_halves",
        in_specs=[ANY, ANY],
        out_specs=[ANY, ANY],
        out_shape=[jax.ShapeDtypeStruct((HALF_IN, IN_WIDTH), F32),
                   jax.ShapeDtypeStruct((3, N_CHIPS, HALF_SQ, D_MODEL), F32)],
        scratch_shapes=[pltpu.SemaphoreType.DMA((n_piece,))] * 2,
    )(g_in, g_sq)


def _join_halves(r_in, r_sq):
    n_in = 16
    n_piece = n_in + 3
    rows = HALF_IN // n_in

    def body(in_alias, sq_alias, full_in, full_sq, send_sems, recv_sems):
        del in_alias, sq_alias
        x, y, c, _ = _position()
        sibling = (x, y, 1 - c)

        def piece(p, core):
            if p < n_in:
                return full_in.at[pl.ds(core * HALF_IN + p * rows, rows), :]
            return full_sq.at[p - n_in, pl.ds(core * HALF_SQ, HALF_SQ), :]

        out = [_remote(piece(p, c), piece(p, c), send_sems.at[p], recv_sems.at[p], sibling) for p in range(n_piece)]
        for cp in out:
            cp.start()
        for p in range(n_piece):
            _remote(piece(p, 1 - c), piece(p, 1 - c), send_sems.at[p], recv_sems.at[p], sibling).wait_recv()
        for cp in out:
            cp.wait_send()

    return pl.pallas_call(
        body,
        name="join_halves",
        in_specs=[ANY, ANY],
        out_specs=[ANY, ANY],
        out_shape=[jax.ShapeDtypeStruct((D_MODEL, W_IN_SHARD), F32),
                   jax.ShapeDtypeStruct((3, ROW_SHARD, D_MODEL), F32)],
        input_output_aliases={0: 0, 1: 1},
        scratch_shapes=[pltpu.SemaphoreType.DMA((n_piece,)), pltpu.SemaphoreType.DMA((n_piece,))],
    )(r_in, r_sq)


SMALL_ROWS = 56
N_DEV = 8


def _sum_small(part):
    def body(part_ref, out_ref, slots, send_sems, recv_sems):
        x, y, c, _ = _position()
        me = 4 * x + 2 * y + c
        slots[me] = part_ref[...]
        out = []
        for r in range(1, N_DEV):
            rx, ry, rc = (r >> 2) & 1, (r >> 1) & 1, r & 1
            to = (1 - x if rx else x, 1 - y if ry else y, 1 - c if rc else c)
            out.append(_remote(part_ref, slots.at[me], send_sems.at[r - 1], recv_sems.at[r - 1], to))
        for cp in out:
            cp.start()
        for r in range(1, N_DEV):
            _remote(part_ref, slots.at[me ^ r], send_sems.at[r - 1], recv_sems.at[r - 1], (x, y, c)).wait_recv()
        for cp in out:
            cp.wait_send()
        total = slots[0]
        for d in range(1, N_DEV):
            total = total + slots[d]
        out_ref[...] = total

    vmem = pl.BlockSpec(memory_space=pltpu.VMEM)
    return pl.pallas_call(
        body,
        name="sum_small",
        in_specs=[vmem],
        out_specs=vmem,
        out_shape=jax.ShapeDtypeStruct((SMALL_ROWS, HEAD_DIM), F32),
        scratch_shapes=[pltpu.VMEM((N_DEV, SMALL_ROWS, HEAD_DIM), F32),
                        pltpu.SemaphoreType.DMA((N_DEV - 1,)), pltpu.SemaphoreType.DMA((N_DEV - 1,))],
    )(part)


def _prefetch_call(body, name, idx, grid, in_specs, out_specs, out_shape, args):
    grid_spec = pltpu.PrefetchScalarGridSpec(num_scalar_prefetch=1, grid=grid, in_specs=in_specs, out_specs=out_specs)
    return pl.pallas_call(body, name=name, grid_spec=grid_spec, out_shape=out_shape,
                          compiler_params=_cparams(("arbitrary",) * len(grid)))(idx, *args)


def _sum_a_in(idx, g_in, got_in):
    tr = 128
    nr = HALF_IN // tr

    def body(idx_ref, a_ref, b_ref, o_ref):
        o_ref[0] = (a_ref[...] + b_ref[...]).astype(WIRE)

    return _prefetch_call(
        body, "sum_a_in", idx, (N_CHIPS, nr),
        [pl.BlockSpec((tr, W_IN_SHARD), lambda j, r, idx: (idx[1] * nr + r, j)),
         pl.BlockSpec((tr, W_IN_SHARD), lambda j, r, idx: (r, j))],
        pl.BlockSpec((1, tr, W_IN_SHARD), lambda j, r, idx: (j, r, 0)),
        jax.ShapeDtypeStruct((N_CHIPS, HALF_IN, W_IN_SHARD), WIRE), (g_in, got_in))


def _sum_a_sq(idx, g_sq, got_sq):
    blk = (1, 1, HALF_SQ, D_MODEL)

    def body(idx_ref, a_ref, b_ref, o_ref):
        o_ref[...] = (a_ref[...] + b_ref[...]).astype(WIRE)

    return _prefetch_call(
        body, "sum_a_sq", idx, (3, N_CHIPS),
        [pl.BlockSpec(blk, lambda a, j, idx: (a, j, idx[1], 0)), pl.BlockSpec(blk, lambda a, j, idx: (a, j, 0, 0))],
        pl.BlockSpec(blk, lambda a, j, idx: (a, j, 0, 0)),
        jax.ShapeDtypeStruct((3, N_CHIPS, HALF_SQ, D_MODEL), WIRE), (g_sq, got_sq))


def _sum_b_in(idx, s_in, got_in):
    tr = 128
    nr = HALF_IN // tr

    def body(idx_ref, a_ref, b_ref, o_ref):
        o_ref[...] = ((a_ref[0].astype(F32) + b_ref[0].astype(F32)) + b_ref[1].astype(F32)) + b_ref[2].astype(F32)

    return _prefetch_call(
        body, "sum_b_in", idx, (nr,),
        [pl.BlockSpec((1, tr, W_IN_SHARD), lambda r, idx: (idx[0], r, 0)),
         pl.BlockSpec((3, tr, W_IN_SHARD), lambda r, idx: (0, r, 0))],
        pl.BlockSpec((tr, W_IN_SHARD), lambda r, idx: (idx[1] * nr + r, 0)),
        jax.ShapeDtypeStruct((D_MODEL, W_IN_SHARD), F32), (s_in, got_in))


def _sum_b_sq(idx, s_sq, got_sq):
    def body(idx_ref, a_ref, b_ref, o_ref):
        o_ref[0] = ((a_ref[0, 0].astype(F32) + b_ref[0, 0].astype(F32)) + b_ref[1, 0].astype(F32)) + b_ref[2, 0].astype(F32)

    return _prefetch_call(
        body, "sum_b_sq", idx, (3,),
        [pl.BlockSpec((1, 1, HALF_SQ, D_MODEL), lambda a, idx: (a, idx[0], 0, 0)),
         pl.BlockSpec((3, 1, HALF_SQ, D_MODEL), lambda a, idx: (0, a, 0, 0))],
        pl.BlockSpec((1, HALF_SQ, D_MODEL), lambda a, idx: (a, idx[1], 0)),
        jax.ShapeDtypeStruct((3, ROW_SHARD, D_MODEL), F32), (s_sq, got_sq))


def _adamw_math(w, g, m, v):
    m = ADAM_B1 * m + (1.0 - ADAM_B1) * g
    v = ADAM_B2 * v + (1.0 - ADAM_B2) * (g * g)
    m_hat = m / (1.0 - ADAM_B1 ** ADAM_STEP)
    v_hat = v / (1.0 - ADAM_B2 ** ADAM_STEP)
    delta = -ADAM_LR * (m_hat / (jnp.sqrt(v_hat) + ADAM_EPS) + ADAM_WD * w)
    return delta, m, v


def _adamw(w, g, m, v, name):
    rows, cols = w.shape
    tr = min(128, rows)

    def body(w_ref, g_ref, m_ref, v_ref, d_ref, nm_ref, nv_ref):
        d_ref[...], nm_ref[...], nv_ref[...] = _adamw_math(w_ref[...], g_ref[...], m_ref[...], v_ref[...])

    spec = pl.BlockSpec((tr, cols), lambda r: (r, 0))
    return pl.pallas_call(
        body,
        name=name,
        grid=(rows // tr,),
        in_specs=[spec] * 4,
        out_specs=[spec] * 3,
        out_shape=[jax.ShapeDtypeStruct((rows, cols), F32)] * 3,
        compiler_params=_cparams(("arbitrary",)),
    )(w, g, m, v)


def _adamw_small(sums, w, m, v):
    def body(s_ref, w_ref, m_ref, v_ref, loss_ref, g_ref, d_ref, nm_ref, nv_ref):
        s = s_ref[...]
        w = w_ref[...]
        loss_ref[...] = s[0:1, 0:1]
        l0, l1 = w[24:32], w[32:40]
        mx = jnp.maximum(l0, l1)
        e0, e1 = jnp.exp(l0 - mx), jnp.exp(l1 - mx)
        p0, p1 = e0 / (e0 + e1), e1 / (e0 + e1)
        d_lb = s[32:40]
        g = jnp.concatenate([s[8:16], s[16:32], d_lb * p0 * (1.0 - p0), -d_lb * p0 * p1, s[40:48], s[48:56]], axis=0)
        g_ref[...] = g
        d_ref[...], nm_ref[...], nv_ref[...] = _adamw_math(w, g, m_ref[...], v_ref[...])

    packed = jax.ShapeDtypeStruct((SMALL_ROWS, HEAD_DIM), F32)
    return pl.pallas_call(
        body,
        name="adamw_small",
        out_shape=[jax.ShapeDtypeStruct((1, 1), F32), packed, packed, packed, packed],
    )(sums, w, m, v)


def _pack_small(ng, bg, lbl, hgn, fg):
    return jnp.concatenate([a.reshape(-1, HEAD_DIM) for a in (ng, bg, lbl, hgn, fg)], axis=0)


def _unpack_small(p):
    return (p[0:8].reshape(1, D_MODEL), p[8:24].reshape(1, 2 * D_MODEL), p[24:40].reshape(2, HEADS, HEAD_DIM),
            p[40:48].reshape(1, HEADS, HEAD_DIM), p[48:56].reshape(D_MODEL))


def kernel(x, norm_g, w_in, b_gate, lb_logits, hg_norm_g, w_sb_proj, w_hg_proj, w_out, final_norm_g, loss_target, m_norm_g, m_w_in, m_b_gate, m_lb_logits, m_hg_norm_g, m_w_sb_proj, m_w_hg_proj, m_w_out, m_final_norm_g, v_norm_g, v_w_in, v_b_gate, v_lb_logits, v_hg_norm_g, v_w_sb_proj, v_w_hg_proj, v_w_out, v_final_norm_g):
    s_len = x.shape[1]
    w_sq = jnp.stack([w_sb_proj[0], w_hg_proj[0], w_out[0]])
    idx = jnp.stack([2 * lax.axis_index("x") + lax.axis_index("y"), lax.axis_index("c")]).astype(jnp.int32)
    w_in_b, w_sq_b = w_in[0].astype(BF16), w_sq.astype(BF16)
    w4, wsq = _place_own(idx, w_in_b, w_sq_b, *_gather_weights(w_in_b, w_sq_b))
    wsq = wsq.reshape(3, D_MODEL, D_MODEL)

    (g_in, g_sb, g_hg, g_out, segs, dout, loss, d_bg, d_lb, d_hgn, d_fg) = _local_grads(
        x[0], loss_target[0], norm_g, b_gate, lb_logits.reshape(2, D_MODEL), hg_norm_g.reshape(1, D_MODEL),
        final_norm_g.reshape(1, D_MODEL), w4, wsq[0], wsq[1], wsq[2])

    g_sq = jnp.stack([g_sb, g_hg, g_out]).reshape(3, N_CHIPS, ROW_SHARD, D_MODEL)
    got_in, got_sq = _swap_halves(g_in, g_sq)
    s_in, s_sq = _sum_a_in(idx, g_in, got_in), _sum_a_sq(idx, g_sq, got_sq)
    grad_x, d_ng, got_in, got_sq = _dx(segs, w4, x[0], norm_g, dout, s_in, s_sq)
    grad_in, grad_sq = _join_halves(_sum_b_in(idx, s_in, got_in), _sum_b_sq(idx, s_sq, got_sq))

    d_in, nm_in, nv_in = _adamw(w_in[0], grad_in, m_w_in[0], v_w_in[0], "adamw_in")
    flat = lambda a, b, c: jnp.concatenate([a[0], b[0], c[0]], axis=0)
    d_sq, nm_sq, nv_sq = _adamw(flat(w_sb_proj, w_hg_proj, w_out), grad_sq.reshape(3 * ROW_SHARD, D_MODEL),
                                flat(m_w_sb_proj, m_w_hg_proj, m_w_out), flat(v_w_sb_proj, v_w_hg_proj, v_w_out),
                                "adamw_sq")

    pad = jnp.zeros((8, HEAD_DIM), F32).at[0, 0].set(loss[0, 0])
    part = jnp.concatenate([pad] + [a.reshape(-1, HEAD_DIM) for a in (d_ng, d_bg, d_lb, d_hgn, d_fg)], axis=0)
    sums = _sum_small(part)
    loss_out, g_sm, d_sm, nm_sm, nv_sm = _adamw_small(
        sums, _pack_small(norm_g, b_gate, lb_logits, hg_norm_g, final_norm_g),
        _pack_small(m_norm_g, m_b_gate, m_lb_logits, m_hg_norm_g, m_final_norm_g),
        _pack_small(v_norm_g, v_b_gate, v_lb_logits, v_hg_norm_g, v_final_norm_g))

    def big(t_in, t_sq):
        sq = t_sq.reshape(3, 1, ROW_SHARD, D_MODEL)
        return t_in[None], sq[0], sq[1], sq[2]

    def order(small, in_, sb, hg, out):
        ng, bg, lbl, hgn, fg = small
        return [ng, in_, bg, lbl, hgn, sb, hg, out, fg]

    outs = [loss_out[0, 0], grad_x[None]]
    for small, (t_in, t_sq) in ((g_sm, (grad_in, grad_sq)), (d_sm, (d_in, d_sq)), (nm_sm, (nm_in, nm_sq)), (nv_sm, (nv_in, nv_sq))):
        outs += order(_unpack_small(small), *big(t_in, t_sq))
    return tuple(outs)
```

```python
import functools

import jax
import jax.numpy as jnp
from jax import lax
from jax.experimental import pallas as pl
from jax.experimental.pallas import tpu as pltpu

F32 = jnp.float32
BF16 = jnp.bfloat16

D_MODEL = 1024
HEADS = 8
HEAD_DIM = 128
IN_WIDTH = 10240
N_CHIPS = 4
W_IN_SHARD = IN_WIDTH // N_CHIPS
ROW_SHARD = D_MODEL // N_CHIPS
RMS_EPS = 1e-6

OFF_SB_Q, OFF_SB_K, OFF_SB_V, OFF_SB_Z = 0, 1024, 2048, 3072
OFF_HG_Q, OFF_HG_F, OFF_HG_I, OFF_HG_Z, OFF_GATE = 4096, 5120, 6144, 7168, 8192

QKV_COLS = 3840
SB_BLOCK = 256
SB_PAIR = 2
SB_ROWS = 256
SB_DEAD = -110.0
HG_CHUNK = 32
HG_PAIR = 2 * HG_CHUNK
HG_STEP = 256
HG_MID = HG_CHUNK // 2 - 1

ADAM_LR, ADAM_B1, ADAM_B2, ADAM_EPS, ADAM_WD, ADAM_STEP = 0.001, 0.9, 0.999, 1e-08, 0.01, 10

VMEM_LIMIT = 56 * 1024 * 1024

MESH = pl.DeviceIdType.MESH


def _cparams(sem, vmem=VMEM_LIMIT):
    return pltpu.CompilerParams(dimension_semantics=sem, vmem_limit_bytes=vmem)


def _dot(a, b):
    return jnp.dot(a, b, preferred_element_type=F32)


def _dot_nt(a, b):
    return lax.dot_general(a, b, (((1,), (1,)), ((), ())), preferred_element_type=F32)


def _dot_tn(a, b):
    return lax.dot_general(a, b, (((0,), (0,)), ((), ())), preferred_element_type=F32)


def _split_dot(x, tri):
    hi = x.astype(BF16)
    lo = (x - hi.astype(F32)).astype(BF16)
    both = _dot(jnp.concatenate([hi, lo], axis=0), tri)
    return both[: x.shape[0]] + both[x.shape[0] :]


def _split_dot_left(tri, x):
    hi = x.astype(BF16)
    lo = (x - hi.astype(F32)).astype(BF16)
    return _dot(tri, hi) + _dot(tri, lo)


def _sigmoid(x):
    return 1.0 / (1.0 + jnp.exp(-x))


def _inproj(x, norm_g, w4):
    s_len = x.shape[0]
    ts = min(1024, s_len)
    tn = QKV_COLS // 3
    per = W_IN_SHARD // tn

    def body(x_ref, g_ref, w_ref, proj_ref, ht_ref, qkv_ref, h_scr):
        n = pl.program_id(1)

        @pl.when(n == 0)
        def _():
            xv = x_ref[...]
            r = lax.rsqrt(jnp.mean(xv * xv, axis=-1, keepdims=True) + RMS_EPS)
            hv = (xv * r) * g_ref[...]
            h_scr[...] = hv.astype(BF16)
            ht_ref[...] = hv.T.astype(BF16)

        p = _dot(h_scr[...], w_ref[0])
        proj_ref[...] = p

        @pl.when(n < 3)
        def _():
            qkv_ref[...] = p.astype(BF16)

    return pl.pallas_call(
        body,
        name="inproj",
        grid=(s_len // ts, IN_WIDTH // tn),
        in_specs=[
            pl.BlockSpec((ts, D_MODEL), lambda s, n: (s, 0)),
            pl.BlockSpec((1, D_MODEL), lambda s, n: (0, 0)),
            pl.BlockSpec((1, D_MODEL, tn), lambda s, n: (n // per, 0, n % per)),
        ],
        out_specs=[
            pl.BlockSpec((ts, tn), lambda s, n: (s, n)),
            pl.BlockSpec((D_MODEL, ts), lambda s, n: (0, s)),
            pl.BlockSpec((ts, tn), lambda s, n: (s, jnp.minimum(n, 2))),
        ],
        out_shape=[
            jax.ShapeDtypeStruct((s_len, IN_WIDTH), F32),
            jax.ShapeDtypeStruct((D_MODEL, s_len), BF16),
            jax.ShapeDtypeStruct((s_len, QKV_COLS), BF16),
        ],
        scratch_shapes=[pltpu.VMEM((ts, D_MODEL), BF16)],
        compiler_params=_cparams(("arbitrary", "arbitrary")),
    )(x, norm_g, w4)


def _sb_tile_fwd(qb, kb, row_gt_col, tri_excl, carry, diag):
    scale = HEAD_DIM ** -0.5
    z = _dot_nt(qb, kb) * scale
    ls_pos = jnp.minimum(z, 0.0) - jnp.log1p(jnp.exp(-jnp.abs(z)))
    log_not = ls_pos - z
    log_not_m = jnp.where(row_gt_col, log_not, 0.0) if diag else log_not
    surv = _split_dot(log_not_m, tri_excl) + carry
    w = jnp.exp(ls_pos + surv)
    if diag:
        w = jnp.where(row_gt_col, w, 0.0)
    return ls_pos, log_not, log_not_m, surv, w


def _sb_specs(s_len, blk):
    width = SB_PAIR * HEAD_DIM

    def blk_spec(off):
        return pl.BlockSpec((blk, width), lambda h, i: (i, off // width + h))

    def head_spec(off):
        return pl.BlockSpec((s_len, width), lambda h, i: (0, off // width + h))

    return blk_spec, head_spec


def _head_cols(p):
    return slice(p * HEAD_DIM, (p + 1) * HEAD_DIM)


def _sb_chains(blk):
    rows = min(SB_ROWS, blk)
    return [(p, a) for p in range(SB_PAIR) for a in range(blk // rows)], rows


def _sb_masks(blk, rows):
    row = lax.broadcasted_iota(jnp.int32, (rows, blk), 0)
    col = lax.broadcasted_iota(jnp.int32, (rows, blk), 1)
    causal = [row + a * rows > col for a in range(blk // rows)]
    row = lax.broadcasted_iota(jnp.int32, (blk, blk), 0)
    col = lax.broadcasted_iota(jnp.int32, (blk, blk), 1)
    tri_excl = (row > col).astype(BF16)
    tri_incl = (row >= col).astype(BF16)
    return causal, tri_excl, tri_incl


def _sb_alive(st, n_chain):
    alive = functools.reduce(jnp.maximum, [st[1 + 3 * c] for c in range(n_chain)])
    return jnp.max(alive) > SB_DEAD


def _sb_fwd(qkv):
    s_len = qkv.shape[0]
    blk = min(SB_BLOCK, s_len)
    nq = s_len // blk
    chains, rows = _sb_chains(blk)

    def body(q_ref, k_ref, v_ref, o_ref, of_ref):
        i = pl.program_id(1)
        causal, tri_excl, _ = _sb_masks(blk, rows)

        def tile(j, st, diag):
            start = pl.multiple_of(j * blk, blk)
            new = []
            for c, (p, a) in enumerate(chains):
                carry, acc, acc_lo = st[3 * c : 3 * c + 3]
                kb = k_ref[pl.ds(start, blk), _head_cols(p)]
                vb = v_ref[pl.ds(start, blk), _head_cols(p)]
                qb = q_ref[a * rows : (a + 1) * rows, _head_cols(p)]
                _, _, log_not_m, surv, w = _sb_tile_fwd(qb, kb, causal[a], tri_excl, carry, diag)
                wb = w.astype(BF16)
                w_lo = (w - wb.astype(F32)).astype(BF16)
                both = _dot(jnp.concatenate([wb, w_lo], axis=0), vb)
                new += [surv[:, 0:1] + log_not_m[:, 0:1], acc + both[:rows], acc_lo + both[rows:]]
            return tuple(new)

        zero = jnp.zeros((rows, HEAD_DIM), F32)
        st = tile(i, (jnp.zeros((rows, 1), F32), zero, zero) * len(chains), True)

        def more(st):
            return (st[0] < i) & _sb_alive(st, len(chains))

        def step(st):
            return (st[0] + 1,) + tile(i - 1 - st[0], st[1:], False)

        st = lax.while_loop(more, step, (0,) + st)[1:]
        for c, (p, a) in enumerate(chains):
            o_ref[a * rows : (a + 1) * rows, _head_cols(p)] = st[3 * c + 1]
            of_ref[a * rows : (a + 1) * rows, _head_cols(p)] = st[3 * c + 1] + st[3 * c + 2]

    blk_spec, head_spec = _sb_specs(s_len, blk)
    return pl.pallas_call(
        body,
        name="sb_fwd",
        grid=(HEADS // SB_PAIR, nq),
        in_specs=[blk_spec(OFF_SB_Q), head_spec(OFF_SB_K), head_spec(OFF_SB_V)],
        out_specs=[blk_spec(0), blk_spec(0)],
        out_shape=[jax.ShapeDtypeStruct((s_len, D_MODEL), F32)] * 2,
        compiler_params=_cparams(("arbitrary", "arbitrary")),
    )(qkv, qkv, qkv)


def _sb_bwd(qkv, o_fine, d_o):
    s_len = qkv.shape[0]
    blk = min(SB_BLOCK, s_len)
    nq = s_len // blk
    scale = HEAD_DIM ** -0.5
    chains, rows = _sb_chains(blk)

    def body(q_ref, k_ref, v_ref, of_ref, do_ref, dq_ref, dk_ref, dv_ref, dk_acc, dv_acc):
        i = pl.program_id(1)

        @pl.when(i == 0)
        def _():
            dk_acc[...] = jnp.zeros_like(dk_acc)
            dv_acc[...] = jnp.zeros_like(dv_acc)

        dob = do_ref[...].astype(BF16)
        prod = dob.astype(F32) * of_ref[...]
        causal, tri_excl, tri_incl = _sb_masks(blk, rows)

        def group(x, p, a):
            return x[a * rows : (a + 1) * rows, _head_cols(p)]

        totals = [jnp.sum(group(prod, p, a), axis=-1, keepdims=True) for p, a in chains]

        def tile(j, st, diag):
            start = pl.multiple_of(j * blk, blk)
            new = []
            dk_new = [None] * SB_PAIR
            dv_new = [None] * SB_PAIR
            for c, (p, a) in enumerate(chains):
                c_not, c_dlw, dq = st[3 * c : 3 * c + 3]
                qb, dob_c = group(q_ref, p, a), group(dob, p, a)
                kb = k_ref[pl.ds(start, blk), _head_cols(p)]
                vb = v_ref[pl.ds(start, blk), _head_cols(p)]
                ls_pos, log_not, log_not_m, surv, w = _sb_tile_fwd(qb, kb, causal[a], tri_excl, c_not, diag)
                dlw = _dot_nt(dob_c, vb) * w
                suffix = _split_dot(dlw, tri_incl)
                d_not = totals[c] - c_dlw - suffix
                dz = (dlw * jnp.exp(log_not) - d_not * jnp.exp(ls_pos)) * scale
                if diag:
                    dz = jnp.where(causal[a], dz, 0.0)
                dzb = dz.astype(BF16)
                dk_c, dv_c = _dot_tn(dzb, qb), _dot_tn(w.astype(BF16), dob_c)
                dk_new[p] = dk_c if dk_new[p] is None else dk_new[p] + dk_c
                dv_new[p] = dv_c if dv_new[p] is None else dv_new[p] + dv_c
                new += [surv[:, 0:1] + log_not_m[:, 0:1], c_dlw + suffix[:, 0:1], dq + _dot(dzb, kb)]
            for p in range(SB_PAIR):
                dk_acc[pl.ds(start, blk), _head_cols(p)] += dk_new[p]
                dv_acc[pl.ds(start, blk), _head_cols(p)] += dv_new[p]
            return tuple(new)

        zcol = jnp.zeros((rows, 1), F32)
        st = tile(i, (zcol, zcol, jnp.zeros((rows, HEAD_DIM), F32)) * len(chains), True)

        def more(st):
            return (st[0] < i) & _sb_alive(st, len(chains))

        def step(st):
            return (st[0] + 1,) + tile(i - 1 - st[0], st[1:], False)

        st = lax.while_loop(more, step, (0,) + st)[1:]
        for c, (p, a) in enumerate(chains):
            dq_ref[a * rows : (a + 1) * rows, _head_cols(p)] = st[3 * c + 2].astype(BF16)

        @pl.when(i == nq - 1)
        def _():
            dk_ref[...] = dk_acc[...].astype(BF16)
            dv_ref[...] = dv_acc[...].astype(BF16)

    blk_spec, head_spec = _sb_specs(s_len, blk)
    width = SB_PAIR * HEAD_DIM
    return pl.pallas_call(
        body,
        name="sb_bwd",
        grid=(HEADS // SB_PAIR, nq),
        in_specs=[blk_spec(OFF_SB_Q), head_spec(OFF_SB_K), head_spec(OFF_SB_V), blk_spec(0), blk_spec(0)],
        out_specs=[blk_spec(0), head_spec(0), head_spec(0)],
        out_shape=[jax.ShapeDtypeStruct((s_len, D_MODEL), BF16)] * 3,
        scratch_shapes=[pltpu.VMEM((s_len, width), F32), pltpu.VMEM((s_len, width), F32)],
        compiler_params=_cparams(("arbitrary", "arbitrary")),
    )(qkv, qkv, qkv, o_fine, d_o)


def _hg_lower_bound(lbl_ref):
    l0 = lbl_ref[0:1, :]
    l1 = lbl_ref[1:2, :]
    mx = jnp.maximum(l0, l1)
    e0 = jnp.exp(l0 - mx)
    e1 = jnp.exp(l1 - mx)
    return e0 / (e0 + e1)


def _hg_gates(hq, hf, lb):
    sig_f = _sigmoid(hf)
    f = lb + (1.0 - lb) * sig_f
    g = jnp.log(f)
    kk = 1.0 - f
    sig_q = _sigmoid(hq)
    qq = hq * sig_q
    return qq, kk, g, f, sig_f, sig_q


def _period_bcast(x, r, rows, period):
    w = x.shape[-1]
    x3 = x.reshape(rows // period, period, w)
    return jnp.broadcast_to(x3[:, r : r + 1, :], x3.shape).reshape(rows, w)


def _blockdiag(rows, kind):
    row = lax.broadcasted_iota(jnp.int32, (rows, rows), 0)
    col = lax.broadcasted_iota(jnp.int32, (rows, rows), 1)
    if kind in ("next", "prev"):
        first, second = (row, col) if kind == "next" else (col, row)
        keep = ((row // HG_PAIR) == (col // HG_PAIR)) & (first % HG_PAIR < HG_CHUNK) & (second % HG_PAIR >= HG_CHUNK)
    else:
        keep = (row // HG_CHUNK) == (col // HG_CHUNK)
        if kind == "lower":
            keep = keep & (row >= col)
        elif kind == "upper":
            keep = keep & (row <= col)
    return jnp.where(keep, 1.0, 0.0).astype(BF16)


def _hg_operands(hq, hf, lb, rows):
    qq, kk, g, f, sig_f, sig_q = _hg_gates(hq, hf, lb)
    cum = _split_dot_left(_blockdiag(rows, "lower"), g)
    mid = _period_bcast(cum, HG_MID, rows, HG_CHUNK)
    last = _period_bcast(cum, HG_CHUNK - 1, rows, HG_CHUNK)
    last0 = _period_bcast(cum, HG_CHUNK - 1, rows, HG_PAIR)
    last1 = _period_bcast(cum, HG_PAIR - 1, rows, HG_PAIR)
    second = (lax.broadcasted_iota(jnp.int32, cum.shape, 0) % HG_PAIR) >= HG_CHUNK
    e = dict(qm=jnp.exp(cum - mid), km=jnp.exp(mid - cum), qd=jnp.exp(cum), kl=jnp.exp(last - cum),
             q_in=jnp.where(second, jnp.exp(last0), 1.0), k_out=jnp.where(second, 1.0, jnp.exp(last1)),
             pair=jnp.exp(last0 + last1))
    v = dict(qm=qq * e["qm"], km=kk * e["km"], qd=qq * e["qd"], kl=kk * e["kl"])
    v["qp"] = v["qd"] * e["q_in"]
    v["kp"] = v["kl"] * e["k_out"]
    return v, e, second, (f, sig_f, sig_q)


def _hg_store_operands(v, second, hi, refs):
    zero = jnp.zeros_like(v["qm"])
    q_cat, k_cat, qp_b, kp_b, v_b = refs
    q_cat[:, 0:D_MODEL] = jnp.where(second, zero, v["qm"]).astype(BF16)
    q_cat[:, D_MODEL : 2 * D_MODEL] = jnp.where(second, v["qm"], zero).astype(BF16)
    q_cat[:, 2 * D_MODEL :] = jnp.where(second, v["qd"], zero).astype(BF16)
    k_cat[:, 0:D_MODEL] = jnp.where(second, zero, v["km"]).astype(BF16)
    k_cat[:, D_MODEL : 2 * D_MODEL] = jnp.where(second, v["km"], zero).astype(BF16)
    k_cat[:, 2 * D_MODEL :] = jnp.where(second, zero, v["kl"]).astype(BF16)
    qp_b[...] = v["qp"].astype(BF16)
    kp_b[...] = v["kp"].astype(BF16)
    v_b[...] = hi.astype(BF16)


def _hg_pair_operands(cat, r0, c0):
    return jnp.concatenate([cat[r0 : r0 + HG_PAIR, g * D_MODEL + c0 : g * D_MODEL + c0 + HEAD_DIM] for g in range(3)], axis=1)


def _hg_fwd(proj, lbl):
    s_len = proj.shape[0]
    rows = min(HG_STEP, s_len)
    n_pairs = rows // HG_PAIR

    def body(hq_ref, hf_ref, hi_ref, lbl_ref, o_ref, st_ref, state, q_cat, k_cat, qp_b, kp_b, v_b):
        @pl.when(pl.program_id(0) == 0)
        def _():
            state[...] = jnp.zeros_like(state)

        v, e, second, _ = _hg_operands(hq_ref[...], hf_ref[...], _hg_lower_bound(lbl_ref), rows)
        _hg_store_operands(v, second, hi_ref[...], (q_cat, k_cat, qp_b, kp_b, v_b))
        e_pair = e["pair"]
        row = lax.broadcasted_iota(jnp.int32, (HG_PAIR, HG_PAIR), 0)
        col = lax.broadcasted_iota(jnp.int32, (HG_PAIR, HG_PAIR), 1)
        causal = row >= col

        for u in range(n_pairs):
            r0 = u * HG_PAIR
            for h in range(HEADS):
                c0 = h * HEAD_DIM
                sl = (slice(r0, r0 + HG_PAIR), slice(c0, c0 + HEAD_DIM))
                st = state[h]
                st_ref[u, h] = st
                a = jnp.where(causal, _dot_nt(_hg_pair_operands(q_cat, r0, c0), _hg_pair_operands(k_cat, r0, c0)), 0.0)
                vb = v_b[sl]
                o_ref[sl] = _dot(a.astype(BF16), vb) + _dot_nt(qp_b[sl], st.astype(BF16))
                state[h] = st * e_pair[r0 : r0 + 1, c0 : c0 + HEAD_DIM] + _dot_tn(vb, kp_b[sl])

    def col_spec(off):
        return pl.BlockSpec((rows, D_MODEL), lambda s: (s, off // D_MODEL))

    bf_tile = pltpu.VMEM((rows, D_MODEL), BF16)
    bf_cat = pltpu.VMEM((rows, 3 * D_MODEL), BF16)
    scratch = [pltpu.VMEM((HEADS, HEAD_DIM, HEAD_DIM), F32), bf_cat, bf_cat, bf_tile, bf_tile, bf_tile]
    return pl.pallas_call(
        body,
        name="hg_fwd",
        grid=(s_len // rows,),
        in_specs=[col_spec(OFF_HG_Q), col_spec(OFF_HG_F), col_spec(OFF_HG_I), pl.BlockSpec((2, D_MODEL), lambda s: (0, 0))],
        out_specs=[
            pl.BlockSpec((rows, D_MODEL), lambda s: (s, 0)),
            pl.BlockSpec((n_pairs, HEADS, HEAD_DIM, HEAD_DIM), lambda s: (s, 0, 0, 0)),
        ],
        out_shape=[
            jax.ShapeDtypeStruct((s_len, D_MODEL), F32),
            jax.ShapeDtypeStruct((s_len // HG_PAIR, HEADS, HEAD_DIM, HEAD_DIM), F32),
        ],
        scratch_shapes=scratch,
        compiler_params=_cparams(("arbitrary",)),
    )(proj, proj, proj, lbl)


def _hg_bwd(proj, lbl, states, d_o):
    s_len = proj.shape[0]
    rows = min(HG_STEP, s_len)
    n_pairs = rows // HG_PAIR
    n_steps = s_len // rows

    def body(hq_ref, hf_ref, hi_ref, lbl_ref, st_ref, do_ref, dp_ref, dlb_ref,
             dstate, q_cat, k_cat, qp_b, kp_b, v_b, do_b, d_qcat, d_kcat, d_qp, d_kp, d_v, d_pair):
        @pl.when(pl.program_id(0) == 0)
        def _():
            dstate[...] = jnp.zeros_like(dstate)
            dlb_ref[...] = jnp.zeros_like(dlb_ref)

        lb = _hg_lower_bound(lbl_ref)
        hq = hq_ref[...]
        v, e, second, (f, sig_f, sig_q) = _hg_operands(hq, hf_ref[...], lb, rows)
        _hg_store_operands(v, second, hi_ref[...], (q_cat, k_cat, qp_b, kp_b, v_b))
        do_b[...] = do_ref[...].astype(BF16)
        e_pair = e["pair"]
        row = lax.broadcasted_iota(jnp.int32, (HG_PAIR, HG_PAIR), 0)
        col = lax.broadcasted_iota(jnp.int32, (HG_PAIR, HG_PAIR), 1)
        causal = row >= col

        for u in reversed(range(n_pairs)):
            r0 = u * HG_PAIR
            for h in range(HEADS):
                c0 = h * HEAD_DIM
                sl = (slice(r0, r0 + HG_PAIR), slice(c0, c0 + HEAD_DIM))
                st0 = st_ref[u, h]
                ds1 = dstate[h]
                ds1b = ds1.astype(BF16)
                dob, vb = do_b[sl], v_b[sl]
                lhs, rhs = _hg_pair_operands(q_cat, r0, c0), _hg_pair_operands(k_cat, r0, c0)
                a = jnp.where(causal, _dot_nt(lhs, rhs), 0.0).astype(BF16)
                da = jnp.where(causal, _dot_nt(dob, vb), 0.0).astype(BF16)
                d_v[sl] = _dot_tn(a, dob) + _dot_nt(kp_b[sl], ds1b)
                d_lhs = _dot(da, rhs)
                d_rhs = _dot_tn(da, lhs)
                for g in range(3):
                    gsl = (sl[0], slice(g * D_MODEL + c0, g * D_MODEL + c0 + HEAD_DIM))
                    d_qcat[gsl] = d_lhs[:, g * HEAD_DIM : (g + 1) * HEAD_DIM]
                    d_kcat[gsl] = d_rhs[:, g * HEAD_DIM : (g + 1) * HEAD_DIM]
                d_qp[sl] = _dot(dob, st0.astype(BF16))
                d_kp[sl] = _dot(vb, ds1b)
                decay = e_pair[r0 : r0 + 1, c0 : c0 + HEAD_DIM]
                d_pair[u : u + 1, c0 : c0 + HEAD_DIM] = decay * jnp.sum(ds1 * st0, axis=0, keepdims=True)
                dstate[h] = ds1 * decay + _dot_tn(dob, qp_b[sl])

        zero = jnp.zeros_like(hq)
        dqm = jnp.where(second, d_qcat[:, D_MODEL : 2 * D_MODEL], d_qcat[:, 0:D_MODEL])
        dkm = jnp.where(second, d_kcat[:, D_MODEL : 2 * D_MODEL], d_kcat[:, 0:D_MODEL])
        dqp, dkp = d_qp[...], d_kp[...]
        dqd = dqp * e["q_in"] + jnp.where(second, d_qcat[:, 2 * D_MODEL :], zero)
        dkl = dkp * e["k_out"] + jnp.where(second, zero, d_kcat[:, 2 * D_MODEL :])
        dq = dqm * e["qm"] + dqd * e["qd"]
        dk = dkm * e["km"] + dkl * e["kl"]
        t_kl = dkl * v["kl"]
        dcum = dqm * v["qm"] - dkm * v["km"] + dqd * v["qd"] - t_kl
        dp = d_pair[...]
        dp_b = jnp.broadcast_to(dp[:, None, :], (n_pairs, HG_PAIR, D_MODEL)).reshape(rows, D_MODEL)
        dg = (_split_dot_left(_blockdiag(rows, "upper"), dcum) + _split_dot_left(_blockdiag(rows, "all"), t_kl)
              + _split_dot_left(_blockdiag(rows, "next"), dqp * v["qp"])
              + _split_dot_left(_blockdiag(rows, "prev"), dkp * v["kp"]) + dp_b)
        df = dg / f - dk
        one_m = 1.0 - sig_f
        dp_ref[:, 0:D_MODEL] = (dq * (sig_q * (1.0 + hq * (1.0 - sig_q)))).astype(BF16)
        dp_ref[:, D_MODEL : 2 * D_MODEL] = (df * (1.0 - lb) * sig_f * one_m).astype(BF16)
        dp_ref[:, 2 * D_MODEL : 3 * D_MODEL] = d_v[...].astype(BF16)
        dlb_ref[...] += jnp.sum(df * one_m, axis=0, keepdims=True)

    def col_spec(off):
        return pl.BlockSpec((rows, D_MODEL), lambda s: (n_steps - 1 - s, off // D_MODEL))

    f32_tile = pltpu.VMEM((rows, D_MODEL), F32)
    f32_cat = pltpu.VMEM((rows, 3 * D_MODEL), F32)
    bf_tile = pltpu.VMEM((rows, D_MODEL), BF16)
    bf_cat = pltpu.VMEM((rows, 3 * D_MODEL), BF16)
    scratch = [pltpu.VMEM((HEADS, HEAD_DIM, HEAD_DIM), F32), bf_cat, bf_cat, bf_tile, bf_tile, bf_tile, bf_tile,
               f32_cat, f32_cat, f32_tile, f32_tile, f32_tile, pltpu.VMEM((n_pairs, D_MODEL), F32)]
    return pl.pallas_call(
        body,
        name="hg_bwd",
        grid=(n_steps,),
        in_specs=[
            col_spec(OFF_HG_Q), col_spec(OFF_HG_F), col_spec(OFF_HG_I),
            pl.BlockSpec((2, D_MODEL), lambda s: (0, 0)),
            pl.BlockSpec((n_pairs, HEADS, HEAD_DIM, HEAD_DIM), lambda s: (n_steps - 1 - s, 0, 0, 0)),
            pl.BlockSpec((rows, D_MODEL), lambda s: (n_steps - 1 - s, 0)),
        ],
        out_specs=[
            pl.BlockSpec((rows, 3 * D_MODEL), lambda s: (n_steps - 1 - s, 0)),
            pl.BlockSpec((1, D_MODEL), lambda s: (0, 0)),
        ],
        out_shape=[
            jax.ShapeDtypeStruct((s_len, 3 * D_MODEL), BF16),
            jax.ShapeDtypeStruct((1, D_MODEL), F32),
        ],
        scratch_shapes=scratch,
        compiler_params=_cparams(("arbitrary",)),
    )(proj, proj, proj, lbl, states, d_o)


def _mid(proj, sb_o, hg_o, x, target, b_gate, hg_gain, final_g, w_sb, w_hg, w_out):
    s_len = proj.shape[0]
    ts = min(128, s_len)
    inv_d = 1.0 / D_MODEL

    def body(zsb_ref, hz_ref, gl_ref, sbo_ref, hgo_ref, x_ref, tgt_ref, bg_ref, hgn_ref, fg_ref,
             wsb_ref, whg_ref, wout_ref,
             dout_ref, dsbo_ref, dhgo_ref, dzsb_ref, dhz_ref, dgl_ref,
             asb_ref, dusb_ref, ahg_ref, duhg_ref, y_ref, doutb_ref,
             loss_ref, dfg_ref, dbg_ref, dhgn_ref):
        @pl.when(pl.program_id(0) == 0)
        def _():
            loss_ref[...] = jnp.zeros_like(loss_ref)
            dfg_ref[...] = jnp.zeros_like(dfg_ref)
            dbg_ref[...] = jnp.zeros_like(dbg_ref)
            dhgn_ref[...] = jnp.zeros_like(dhgn_ref)

        z_sb = zsb_ref[...]
        sb_o = sbo_ref[...]
        sig_zsb = _sigmoid(z_sb)
        silu_zsb = z_sb * sig_zsb
        a_sb = (sb_o * silu_zsb).astype(BF16)
        u_sb = _dot(a_sb, wsb_ref[...])

        hg_o = hgo_ref[...]
        gain = hgn_ref[...]
        r_parts, yn_parts = [], []
        for h in range(HEADS):
            oh = hg_o[:, h * HEAD_DIM : (h + 1) * HEAD_DIM]
            r = lax.rsqrt(jnp.mean(oh * oh, axis=-1, keepdims=True) + RMS_EPS)
            r_parts.append(jnp.broadcast_to(r, oh.shape))
            yn_parts.append(oh * r)
        r_hg = jnp.concatenate(r_parts, axis=-1)
        yn_hg = jnp.concatenate(yn_parts, axis=-1)
        hn = yn_hg * gain
        hz = hz_ref[...]
        sig_hz = _sigmoid(hz)
        silu_hz = hz * sig_hz
        a_hg = (hn * silu_hz).astype(BF16)
        u_hg = _dot(a_hg, whg_ref[...])

        gates = _sigmoid(gl_ref[...] + bg_ref[...])
        g_sb = gates[:, 0:D_MODEL]
        g_hg = gates[:, D_MODEL:]
        y = (g_sb * u_sb + g_hg * u_hg).astype(BF16)
        out = x_ref[...] + _dot(y, wout_ref[...])
        r2 = lax.rsqrt(jnp.mean(out * out, axis=-1, keepdims=True) + RMS_EPS)
        yn = out * r2
        fg = fg_ref[...]
        diff = yn * fg - tgt_ref[...]
        loss_ref[...] += 0.5 * inv_d * jnp.sum(diff * diff)

        dyf = diff * inv_d
        dfg_ref[...] += jnp.sum(dyf * yn, axis=0, keepdims=True)
        dyn = dyf * fg
        dout = r2 * (dyn - yn * jnp.mean(dyn * yn, axis=-1, keepdims=True))
        dout_ref[...] = dout
        doutb = dout.astype(BF16)
        doutb_ref[...] = doutb
        dy = _dot_nt(doutb, wout_ref[...])
        du_sb = (dy * g_sb).astype(BF16)
        du_hg = (dy * g_hg).astype(BF16)
        dgl_sb = dy * u_sb * g_sb * (1.0 - g_sb)
        dgl_hg = dy * u_hg * g_hg * (1.0 - g_hg)
        dgl_ref[:, 0:D_MODEL] = dgl_sb.astype(BF16)
        dgl_ref[:, D_MODEL:] = dgl_hg.astype(BF16)
        dbg_ref[:, 0:D_MODEL] += jnp.sum(dgl_sb, axis=0, keepdims=True)
        dbg_ref[:, D_MODEL:] += jnp.sum(dgl_hg, axis=0, keepdims=True)

        da_sb = _dot_nt(du_sb, wsb_ref[...])
        dsbo_ref[...] = da_sb * silu_zsb
        dzsb_ref[...] = (da_sb * sb_o * (sig_zsb * (1.0 + z_sb * (1.0 - sig_zsb)))).astype(BF16)

        da_hg = _dot_nt(du_hg, whg_ref[...])
        dhn = da_hg * silu_hz
        dhz_ref[...] = (da_hg * hn * (sig_hz * (1.0 + hz * (1.0 - sig_hz)))).astype(BF16)
        dhgn_ref[...] += jnp.sum(dhn * yn_hg, axis=0, keepdims=True)
        dyn_hg = dhn * gain
        prod = dyn_hg * yn_hg
        m_parts = []
        for h in range(HEADS):
            ph = prod[:, h * HEAD_DIM : (h + 1) * HEAD_DIM]
            m_parts.append(jnp.broadcast_to(jnp.mean(ph, axis=-1, keepdims=True), ph.shape))
        dhgo_ref[...] = r_hg * (dyn_hg - yn_hg * jnp.concatenate(m_parts, axis=-1))

        asb_ref[...] = a_sb
        dusb_ref[...] = du_sb
        ahg_ref[...] = a_hg
        duhg_ref[...] = du_hg
        y_ref[...] = y

    def tile(width, off=0):
        return pl.BlockSpec((ts, width), lambda s: (s, off // width))

    def whole(shape):
        return pl.BlockSpec(shape, lambda s: (0,) * len(shape))

    sq = (D_MODEL, D_MODEL)
    f32_act = jax.ShapeDtypeStruct((s_len, D_MODEL), F32)
    bf_act = jax.ShapeDtypeStruct((s_len, D_MODEL), BF16)
    return pl.pallas_call(
        body,
        name="mid",
        grid=(s_len // ts,),
        in_specs=[
            tile(D_MODEL, OFF_SB_Z), tile(D_MODEL, OFF_HG_Z), tile(2 * D_MODEL, OFF_GATE),
            tile(D_MODEL), tile(D_MODEL), tile(D_MODEL), tile(D_MODEL),
            whole((1, 2 * D_MODEL)), whole((1, D_MODEL)), whole((1, D_MODEL)),
            whole(sq), whole(sq), whole(sq),
        ],
        out_specs=[
            tile(D_MODEL), tile(D_MODEL), tile(D_MODEL), tile(D_MODEL), tile(D_MODEL), tile(2 * D_MODEL),
            tile(D_MODEL), tile(D_MODEL), tile(D_MODEL), tile(D_MODEL), tile(D_MODEL), tile(D_MODEL),
            whole((1, 1)), whole((1, D_MODEL)), whole((1, 2 * D_MODEL)), whole((1, D_MODEL)),
        ],
        out_shape=[
            f32_act, f32_act, f32_act, bf_act, bf_act, jax.ShapeDtypeStruct((s_len, 2 * D_MODEL), BF16),
            bf_act, bf_act, bf_act, bf_act, bf_act, bf_act,
            jax.ShapeDtypeStruct((1, 1), F32), jax.ShapeDtypeStruct((1, D_MODEL), F32),
            jax.ShapeDtypeStruct((1, 2 * D_MODEL), F32), jax.ShapeDtypeStruct((1, D_MODEL), F32),
        ],
        compiler_params=_cparams(("arbitrary",)),
    )(proj, proj, proj, sb_o, hg_o, x, target, b_gate, hg_gain, final_g, w_sb, w_hg, w_out)


def _grad_matmul(a, b, name, tn):
    s_len, m = a.shape
    n = b.shape[1]
    tk = min(512, s_len)

    def body(a_ref, b_ref, o_ref):
        @pl.when(pl.program_id(1) == 0)
        def _():
            o_ref[...] = jnp.zeros_like(o_ref)

        o_ref[...] += _dot_tn(a_ref[...], b_ref[...])

    return pl.pallas_call(
        body,
        name=name,
        grid=(n // tn, s_len // tk),
        in_specs=[pl.BlockSpec((tk, m), lambda j, k: (k, 0)), pl.BlockSpec((tk, tn), lambda j, k: (k, j))],
        out_specs=pl.BlockSpec((m, tn), lambda j, k: (0, j)),
        out_shape=jax.ShapeDtypeStruct((m, n), F32),
        compiler_params=_cparams(("arbitrary", "arbitrary")),
    )(a, b)


SEG_WIDTHS = (1024, 1024, 1024, 1024, 3072, 1024, 2048)


def _seg_bounds(tile):
    bounds = [0]
    for w in SEG_WIDTHS:
        bounds.append(bounds[-1] + w // tile)
    return bounds


def _grad_w_in(h_t, segs):
    m, s_len = h_t.shape
    tk = min(1024, s_len)
    tn = 1024
    nk = s_len // tk
    bounds = _seg_bounds(tn)

    def body(a_ref, *refs):
        seg_refs, o_ref = refs[:-1], refs[-1]
        j = pl.program_id(0)

        @pl.when(pl.program_id(1) == 0)
        def _():
            o_ref[...] = jnp.zeros_like(o_ref)

        for i, ref in enumerate(seg_refs):
            @pl.when((j >= bounds[i]) & (j < bounds[i + 1]))
            def _(ref=ref):
                o_ref[...] += _dot(a_ref[...], ref[...])

    def seg_spec(lo, hi):
        def index(j, k):
            return (jnp.where(j < lo, 0, jnp.where(j >= hi, nk - 1, k)), jnp.clip(j - lo, 0, hi - lo - 1))
        return pl.BlockSpec((tk, tn), index)

    return pl.pallas_call(
        body,
        name="grad_w_in",
        grid=(IN_WIDTH // tn, nk),
        in_specs=[pl.BlockSpec((m, tk), lambda j, k: (0, k))] + [seg_spec(bounds[i], bounds[i + 1]) for i in range(7)],
        out_specs=pl.BlockSpec((m, tn), lambda j, k: (0, j)),
        out_shape=jax.ShapeDtypeStruct((m, IN_WIDTH), F32),
        compiler_params=_cparams(("arbitrary", "arbitrary")),
    )(h_t, *segs)


EXCHANGE_IN_PIECES = 8
EXCHANGE_PIECES = EXCHANGE_IN_PIECES + 3


def _exchange_copies(sin_ref, ssq_ref, got_in, got_sq, send_sems, recv_sems):
    _, _, c, chips = _position()
    rows = HALF_IN // EXCHANGE_IN_PIECES
    copies = []
    for k, (px, py) in enumerate(chips):
        chip = 2 * px + py
        for p in range(EXCHANGE_PIECES):
            if p < EXCHANGE_IN_PIECES:
                src, dst = sin_ref.at[chip, pl.ds(p * rows, rows), :], got_in.at[k, pl.ds(p * rows, rows), :]
            else:
                src, dst = ssq_ref.at[p - EXCHANGE_IN_PIECES, chip], got_sq.at[k, p - EXCHANGE_IN_PIECES]
            copies.append(_remote(src, dst, send_sems.at[k, p], recv_sems.at[k, p], (px, py, c)))
    return copies


def _dx(segs, w4, x, norm_g, dout, s_in, s_sq):
    s_len = x.shape[0]
    ts = min(1024, s_len)
    tk = 512
    per = W_IN_SHARD // tk
    nk = IN_WIDTH // tk
    ns = s_len // ts
    bounds = _seg_bounds(tk)

    def body(*refs):
        seg_refs = refs[:7]
        w_ref, x_ref, g_ref, dout_ref, sin_ref, ssq_ref, gx_ref, dg_ref, got_in, got_sq, acc, send_sems, recv_sems = refs[7:]
        s, k = pl.program_id(0), pl.program_id(1)

        @pl.when((s == 0) & (k == 0))
        def _():
            dg_ref[...] = jnp.zeros_like(dg_ref)
            for cp in _exchange_copies(sin_ref, ssq_ref, got_in, got_sq, send_sems, recv_sems):
                cp.start()

        @pl.when(k == 0)
        def _():
            acc[...] = jnp.zeros_like(acc)

        for i, ref in enumerate(seg_refs):
            @pl.when((k >= bounds[i]) & (k < bounds[i + 1]))
            def _(ref=ref):
                acc[...] += _dot_nt(ref[...], w_ref[0])

        @pl.when(k == nk - 1)
        def _():
            dh = acc[...]
            xv = x_ref[...]
            r = lax.rsqrt(jnp.mean(xv * xv, axis=-1, keepdims=True) + RMS_EPS)
            xn = xv * r
            dg_ref[...] += jnp.sum(dh * xn, axis=0, keepdims=True)
            dxn = dh * g_ref[...]
            gx_ref[...] = r * (dxn - xn * jnp.mean(dxn * xn, axis=-1, keepdims=True)) + dout_ref[...]

        @pl.when((s == ns - 1) & (k == nk - 1))
        def _():
            for cp in _exchange_copies(sin_ref, ssq_ref, got_in, got_sq, send_sems, recv_sems):
                cp.wait()

    def seg_spec(lo, hi):
        return pl.BlockSpec((ts, tk), lambda s, k: (s, jnp.clip(k - lo, 0, hi - lo - 1)))

    row_tile = pl.BlockSpec((ts, D_MODEL), lambda s, k: (s, 0))
    vec = pl.BlockSpec((1, D_MODEL), lambda s, k: (0, 0))
    return pl.pallas_call(
        body,
        name="dx",
        grid=(ns, nk),
        in_specs=[seg_spec(bounds[i], bounds[i + 1]) for i in range(7)] + [
            pl.BlockSpec((1, D_MODEL, tk), lambda s, k: (k // per, 0, k % per)),
            row_tile, vec, row_tile, ANY, ANY,
        ],
        out_specs=[row_tile, vec, ANY, ANY],
        out_shape=[jax.ShapeDtypeStruct((s_len, D_MODEL), F32), jax.ShapeDtypeStruct((1, D_MODEL), F32),
                   jax.ShapeDtypeStruct((3, HALF_IN, W_IN_SHARD), WIRE),
                   jax.ShapeDtypeStruct((3, 3, HALF_SQ, D_MODEL), WIRE)],
        scratch_shapes=[pltpu.VMEM((ts, D_MODEL), F32),
                        pltpu.SemaphoreType.DMA((3, EXCHANGE_PIECES)), pltpu.SemaphoreType.DMA((3, EXCHANGE_PIECES))],
        compiler_params=_cparams(("arbitrary", "arbitrary")),
    )(*segs, w4, x, norm_g, dout, s_in, s_sq)


def _local_grads(x, target, norm_g, b_gate, lbl, hg_gain, final_g, w4, w_sb, w_hg, w_out):
    proj, h_t, qkv = _inproj(x, norm_g, w4)
    sb_o, sb_o_fine = _sb_fwd(qkv)
    hg_o, states = _hg_fwd(proj, lbl)
    (dout, d_sbo, d_hgo, d_zsb, d_hz, d_gl, a_sb, du_sb, a_hg, du_hg, y, doutb,
     loss, d_fg, d_bg, d_hgn) = _mid(proj, sb_o, hg_o, x, target, b_gate, hg_gain, final_g, w_sb, w_hg, w_out)
    g_w_sb = _grad_matmul(a_sb, du_sb, "grad_w_sb", 512)
    g_w_hg = _grad_matmul(a_hg, du_hg, "grad_w_hg", 512)
    g_w_out = _grad_matmul(y, doutb, "grad_w_out", 512)
    d_q, d_k, d_v = _sb_bwd(qkv, sb_o_fine, d_sbo)
    d_hg, d_lb = _hg_bwd(proj, lbl, states, d_hgo)
    segs = (d_q, d_k, d_v, d_zsb, d_hg, d_hz, d_gl)
    g_w_in = _grad_w_in(h_t, segs)
    return g_w_in, g_w_sb, g_w_hg, g_w_out, segs, dout, loss, d_bg, d_lb, d_hgn, d_fg


ANY = pl.BlockSpec(memory_space=pl.ANY)
WIRE = BF16
HALF_IN = D_MODEL // 2
HALF_SQ = ROW_SHARD // 2


def _position():
    x, y, c = lax.axis_index("x"), lax.axis_index("y"), lax.axis_index("c")
    chips = [(1 - x, y), (x, 1 - y), (1 - x, 1 - y)]
    return x, y, c, chips


def _remote(src, dst, send_sem, recv_sem, to):
    return pltpu.make_async_remote_copy(src_ref=src, dst_ref=dst, send_sem=send_sem, recv_sem=recv_sem,
                                        device_id=to, device_id_type=MESH)


def _gather_weights(w_in_b, w_sq_b):
    n_in = 4
    n_piece = n_in + 3
    rows = HALF_IN // n_in

    def body(win_ref, wsq_ref, in_ref, sq_ref, send_sems, recv_sems):
        x, y, c, chips = _position()
        me = 2 * x + y
        sibling = (x, y, 1 - c)

        def src_piece(p):
            if p < n_in:
                return win_ref.at[pl.ds(c * HALF_IN + p * rows, rows), :]
            return wsq_ref.at[p - n_in, pl.ds(c * HALF_SQ, HALF_SQ), :]

        def piece(p, chip, core):
            if p < n_in:
                return in_ref.at[chip, pl.ds(core * HALF_IN + p * rows, rows), :]
            return sq_ref.at[p - n_in, chip, pl.ds(core * HALF_SQ, HALF_SQ), :]

        sends = []
        for k, (px, py) in enumerate(chips):
            for p in range(n_piece):
                sends.append(_remote(src_piece(p), piece(p, me, c), send_sems.at[k, p], recv_sems.at[k, p], (px, py, c)))
        for cp in sends:
            cp.start()
        for k, (px, py) in enumerate(chips):
            chip = 2 * px + py
            for p in range(n_piece):
                got = piece(p, chip, c)
                _remote(got, got, send_sems.at[k, p], recv_sems.at[k, p], (px, py, c)).wait_recv()
                fwd = _remote(got, got, send_sems.at[3 + k, p], recv_sems.at[3 + k, p], sibling)
                fwd.start()
                sends.append(fwd)
        for k, (px, py) in enumerate(chips):
            chip = 2 * px + py
            for p in range(n_piece):
                got = piece(p, chip, 1 - c)
                _remote(got, got, send_sems.at[3 + k, p], recv_sems.at[3 + k, p], sibling).wait_recv()
        for cp in sends:
            cp.wait_send()

    return pl.pallas_call(
        body,
        name="gather_weights",
        in_specs=[ANY, ANY],
        out_specs=[ANY, ANY],
        out_shape=[jax.ShapeDtypeStruct((N_CHIPS, D_MODEL, W_IN_SHARD), BF16),
                   jax.ShapeDtypeStruct((3, N_CHIPS, ROW_SHARD, D_MODEL), BF16)],
        scratch_shapes=[pltpu.SemaphoreType.DMA((6, n_piece)), pltpu.SemaphoreType.DMA((6, n_piece))],
    )(w_in_b, w_sq_b)


def _place_own(idx, w_in_b, w_sq_b, w4, wsq):
    n = 4
    r_in, r_sq = D_MODEL // n, ROW_SHARD // n

    def body(idx_ref, win_ref, wsq_ref, w4_in, wsq_in, w4_out, wsq_out):
        w4_out[0] = win_ref[...]
        wsq_out[:, 0] = wsq_ref[...]

    grid_spec = pltpu.PrefetchScalarGridSpec(
        num_scalar_prefetch=1,
        grid=(n,),
        in_specs=[pl.BlockSpec((r_in, W_IN_SHARD), lambda r, idx: (r, 0)),
                  pl.BlockSpec((3, r_sq, D_MODEL), lambda r, idx: (0, r, 0)), ANY, ANY],
        out_specs=[pl.BlockSpec((1, r_in, W_IN_SHARD), lambda r, idx: (idx[0], r, 0)),
                   pl.BlockSpec((3, 1, r_sq, D_MODEL), lambda r, idx: (0, idx[0], r, 0))],
    )
    return pl.pallas_call(
        body,
        name="place_own",
        grid_spec=grid_spec,
        out_shape=[jax.ShapeDtypeStruct(w4.shape, BF16), jax.ShapeDtypeStruct(wsq.shape, BF16)],
        input_output_aliases={3: 0, 4: 1},
        compiler_params=_cparams(("arbitrary",)),
    )(idx, w_in_b, w_sq_b, w4, wsq)


def _swap_halves(g_in, g_sq):
    n_in = 16
    n_piece = n_in + 3 * N_CHIPS
    rows = HALF_IN // n_in

    def body(gin_ref, gsq_ref, got_in, got_sq, send_sems, recv_sems):
        x, y, c, _ = _position()
        sibling = (x, y, 1 - c)

        def src_piece(p):
            if p < n_in:
                return gin_ref.at[pl.ds((1 - c) * HALF_IN + p * rows, rows), :]
            a, chip = divmod(p - n_in, N_CHIPS)
            return gsq_ref.at[a, chip, pl.ds((1 - c) * HALF_SQ, HALF_SQ), :]

        def dst_piece(p):
            if p < n_in:
                return got_in.at[pl.ds(p * rows, rows), :]
            a, chip = divmod(p - n_in, N_CHIPS)
            return got_sq.at[a, chip]

        out = [_remote(src_piece(p), dst_piece(p), send_sems.at[p], recv_sems.at[p], sibling) for p in range(n_piece)]
        for cp in out:
            cp.start()
        for cp in out:
            cp.wait()

    return pl.pallas_call(
        body,
        name="swap_halves",
        in_specs=[ANY, ANY],
        out_specs=[ANY, ANY],
        out_shape=[jax.ShapeDtypeStruct((HALF_IN, IN_WIDTH), F32),
                   jax.ShapeDtypeStruct((3, N_CHIPS, HALF_SQ, D_MODEL), F32)],
        scratch_shapes=[pltpu.SemaphoreType.DMA((n_piece,))] * 2,
    )(g_in, g_sq)


def _join_halves(r_in, r_sq):
    n_in = 16
    n_piece = n_in + 3
    rows = HALF_IN // n_in

    def body(in_alias, sq_alias, full_in, full_sq, send_sems, recv_sems):
        del in_alias, sq_alias
        x, y, c, _ = _position()
        sibling = (x, y, 1 - c)

        def piece(p, core):
            if p < n_in:
                return full_in.at[pl.ds(core * HALF_IN + p * rows, rows), :]
            return full_sq.at[p - n_in, pl.ds(core * HALF_SQ, HALF_SQ), :]

        out = [_remote(piece(p, c), piece(p, c), send_sems.at[p], recv_sems.at[p], sibling) for p in range(n_piece)]
        for cp in out:
            cp.start()
        for p in range(n_piece):
            _remote(piece(p, 1 - c), piece(p, 1 - c), send_sems.at[p], recv_sems.at[p], sibling).wait_recv()
        for cp in out:
            cp.wait_send()

    return pl.pallas_call(
        body,
        name="join_halves",
        in_specs=[ANY, ANY],
        out_specs=[ANY, ANY],
        out_shape=[jax.ShapeDtypeStruct((D_MODEL, W_IN_SHARD), F32),
                   jax.ShapeDtypeStruct((3, ROW_SHARD, D_MODEL), F32)],
        input_output_aliases={0: 0, 1: 1},
        scratch_shapes=[pltpu.SemaphoreType.DMA((n_piece,)), pltpu.SemaphoreType.DMA((n_piece,))],
    )(r_in, r_sq)


SMALL_ROWS = 56
N_DEV = 8


def _sum_small(part):
    def body(part_ref, out_ref, slots, send_sems, recv_sems):
        x, y, c, _ = _position()
        me = 4 * x + 2 * y + c
        slots[me] = part_ref[...]
        out = []
        for r in range(1, N_DEV):
            rx, ry, rc = (r >> 2) & 1, (r >> 1) & 1, r & 1
            to = (1 - x if rx else x, 1 - y if ry else y, 1 - c if rc else c)
            out.append(_remote(part_ref, slots.at[me], send_sems.at[r - 1], recv_sems.at[r - 1], to))
        for cp in out:
            cp.start()
        for r in range(1, N_DEV):
            _remote(part_ref, slots.at[me ^ r], send_sems.at[r - 1], recv_sems.at[r - 1], (x, y, c)).wait_recv()
        for cp in out:
            cp.wait_send()
        total = slots[0]
        for d in range(1, N_DEV):
            total = total + slots[d]
        out_ref[...] = total

    vmem = pl.BlockSpec(memory_space=pltpu.VMEM)
    return pl.pallas_call(
        body,
        name="sum_small",
        in_specs=[vmem],
        out_specs=vmem,
        out_shape=jax.ShapeDtypeStruct((SMALL_ROWS, HEAD_DIM), F32),
        scratch_shapes=[pltpu.VMEM((N_DEV, SMALL_ROWS, HEAD_DIM), F32),
                        pltpu.SemaphoreType.DMA((N_DEV - 1,)), pltpu.SemaphoreType.DMA((N_DEV - 1,))],
    )(part)


def _prefetch_call(body, name, idx, grid, in_specs, out_specs, out_shape, args):
    grid_spec = pltpu.PrefetchScalarGridSpec(num_scalar_prefetch=1, grid=grid, in_specs=in_specs, out_specs=out_specs)
    return pl.pallas_call(body, name=name, grid_spec=grid_spec, out_shape=out_shape,
                          compiler_params=_cparams(("arbitrary",) * len(grid)))(idx, *args)


def _sum_a_in(idx, g_in, got_in):
    tr = 128
    nr = HALF_IN // tr

    def body(idx_ref, a_ref, b_ref, o_ref):
        o_ref[0] = (a_ref[...] + b_ref[...]).astype(WIRE)

    return _prefetch_call(
        body, "sum_a_in", idx, (N_CHIPS, nr),
        [pl.BlockSpec((tr, W_IN_SHARD), lambda j, r, idx: (idx[1] * nr + r, j)),
         pl.BlockSpec((tr, W_IN_SHARD), lambda j, r, idx: (r, j))],
        pl.BlockSpec((1, tr, W_IN_SHARD), lambda j, r, idx: (j, r, 0)),
        jax.ShapeDtypeStruct((N_CHIPS, HALF_IN, W_IN_SHARD), WIRE), (g_in, got_in))


def _sum_a_sq(idx, g_sq, got_sq):
    blk = (1, 1, HALF_SQ, D_MODEL)

    def body(idx_ref, a_ref, b_ref, o_ref):
        o_ref[...] = (a_ref[...] + b_ref[...]).astype(WIRE)

    return _prefetch_call(
        body, "sum_a_sq", idx, (3, N_CHIPS),
        [pl.BlockSpec(blk, lambda a, j, idx: (a, j, idx[1], 0)), pl.BlockSpec(blk, lambda a, j, idx: (a, j, 0, 0))],
        pl.BlockSpec(blk, lambda a, j, idx: (a, j, 0, 0)),
        jax.ShapeDtypeStruct((3, N_CHIPS, HALF_SQ, D_MODEL), WIRE), (g_sq, got_sq))


def _sum_b_in(idx, s_in, got_in):
    tr = 128
    nr = HALF_IN // tr

    def body(idx_ref, a_ref, b_ref, o_ref):
        o_ref[...] = ((a_ref[0].astype(F32) + b_ref[0].astype(F32)) + b_ref[1].astype(F32)) + b_ref[2].astype(F32)

    return _prefetch_call(
        body, "sum_b_in", idx, (nr,),
        [pl.BlockSpec((1, tr, W_IN_SHARD), lambda r, idx: (idx[0], r, 0)),
         pl.BlockSpec((3, tr, W_IN_SHARD), lambda r, idx: (0, r, 0))],
        pl.BlockSpec((tr, W_IN_SHARD), lambda r, idx: (idx[1] * nr + r, 0)),
        jax.ShapeDtypeStruct((D_MODEL, W_IN_SHARD), F32), (s_in, got_in))


def _sum_b_sq(idx, s_sq, got_sq):
    def body(idx_ref, a_ref, b_ref, o_ref):
        o_ref[0] = ((a_ref[0, 0].astype(F32) + b_ref[0, 0].astype(F32)) + b_ref[1, 0].astype(F32)) + b_ref[2, 0].astype(F32)

    return _prefetch_call(
        body, "sum_b_sq", idx, (3,),
        [pl.BlockSpec((1, 1, HALF_SQ, D_MODEL), lambda a, idx: (a, idx[0], 0, 0)),
         pl.BlockSpec((3, 1, HALF_SQ, D_MODEL), lambda a, idx: (0, a, 0, 0))],
        pl.BlockSpec((1, HALF_SQ, D_MODEL), lambda a, idx: (a, idx[1], 0)),
        jax.ShapeDtypeStruct((3, ROW_SHARD, D_MODEL), F32), (s_sq, got_sq))


def _adamw_math(w, g, m, v):
    m = ADAM_B1 * m + (1.0 - ADAM_B1) * g
    v = ADAM_B2 * v + (1.0 - ADAM_B2) * (g * g)
    m_hat = m / (1.0 - ADAM_B1 ** ADAM_STEP)
    v_hat = v / (1.0 - ADAM_B2 ** ADAM_STEP)
    delta = -ADAM_LR * (m_hat / (jnp.sqrt(v_hat) + ADAM_EPS) + ADAM_WD * w)
    return delta, m, v


def _adamw(w, g, m, v, name):
    rows, cols = w.shape
    tr = min(128, rows)

    def body(w_ref, g_ref, m_ref, v_ref, d_ref, nm_ref, nv_ref):
        d_ref[...], nm_ref[...], nv_ref[...] = _adamw_math(w_ref[...], g_ref[...], m_ref[...], v_ref[...])

    spec = pl.BlockSpec((tr, cols), lambda r: (r, 0))
    return pl.pallas_call(
        body,
        name=name,
        grid=(rows // tr,),
        in_specs=[spec] * 4,
        out_specs=[spec] * 3,
        out_shape=[jax.ShapeDtypeStruct((rows, cols), F32)] * 3,
        compiler_params=_cparams(("arbitrary",)),
    )(w, g, m, v)


def _adamw_small(sums, w, m, v):
    def body(s_ref, w_ref, m_ref, v_ref, loss_ref, g_ref, d_ref, nm_ref, nv_ref):
        s = s_ref[...]
        w = w_ref[...]
        loss_ref[...] = s[0:1, 0:1]
        l0, l1 = w[24:32], w[32:40]
        mx = jnp.maximum(l0, l1)
        e0, e1 = jnp.exp(l0 - mx), jnp.exp(l1 - mx)
        p0, p1 = e0 / (e0 + e1), e1 / (e0 + e1)
        d_lb = s[32:40]
        g = jnp.concatenate([s[8:16], s[16:32], d_lb * p0 * (1.0 - p0), -d_lb * p0 * p1, s[40:48], s[48:56]], axis=0)
        g_ref[...] = g
        d_ref[...], nm_ref[...], nv_ref[...] = _adamw_math(w, g, m_ref[...], v_ref[...])

    packed = jax.ShapeDtypeStruct((SMALL_ROWS, HEAD_DIM), F32)
    return pl.pallas_call(
        body,
        name="adamw_small",
        out_shape=[jax.ShapeDtypeStruct((1, 1), F32), packed, packed, packed, packed],
    )(sums, w, m, v)


def _pack_small(ng, bg, lbl, hgn, fg):
    return jnp.concatenate([a.reshape(-1, HEAD_DIM) for a in (ng, bg, lbl, hgn, fg)], axis=0)


def _unpack_small(p):
    return (p[0:8].reshape(1, D_MODEL), p[8:24].reshape(1, 2 * D_MODEL), p[24:40].reshape(2, HEADS, HEAD_DIM),
            p[40:48].reshape(1, HEADS, HEAD_DIM), p[48:56].reshape(D_MODEL))


def kernel(x, norm_g, w_in, b_gate, lb_logits, hg_norm_g, w_sb_proj, w_hg_proj, w_out, final_norm_g, loss_target, m_norm_g, m_w_in, m_b_gate, m_lb_logits, m_hg_norm_g, m_w_sb_proj, m_w_hg_proj, m_w_out, m_final_norm_g, v_norm_g, v_w_in, v_b_gate, v_lb_logits, v_hg_norm_g, v_w_sb_proj, v_w_hg_proj, v_w_out, v_final_norm_g):
    s_len = x.shape[1]
    w_sq = jnp.stack([w_sb_proj[0], w_hg_proj[0], w_out[0]])
    idx = jnp.stack([2 * lax.axis_index("x") + lax.axis_index("y"), lax.axis_index("c")]).astype(jnp.int32)
    w_in_b, w_sq_b = w_in[0].astype(BF16), w_sq.astype(BF16)
    w4, wsq = _place_own(idx, w_in_b, w_sq_b, *_gather_weights(w_in_b, w_sq_b))
    wsq = wsq.reshape(3, D_MODEL, D_MODEL)

    (g_in, g_sb, g_hg, g_out, segs, dout, loss, d_bg, d_lb, d_hgn, d_fg) = _local_grads(
        x[0], loss_target[0], norm_g, b_gate, lb_logits.reshape(2, D_MODEL), hg_norm_g.reshape(1, D_MODEL),
        final_norm_g.reshape(1, D_MODEL), w4, wsq[0], wsq[1], wsq[2])

    g_sq = jnp.stack([g_sb, g_hg, g_out]).reshape(3, N_CHIPS, ROW_SHARD, D_MODEL)
    got_in, got_sq = _swap_halves(g_in, g_sq)
    s_in, s_sq = _sum_a_in(idx, g_in, got_in), _sum_a_sq(idx, g_sq, got_sq)
    grad_x, d_ng, got_in, got_sq = _dx(segs, w4, x[0], norm_g, dout, s_in, s_sq)
    grad_in, grad_sq = _join_halves(_sum_b_in(idx, s_in, got_in), _sum_b_sq(idx, s_sq, got_sq))

    d_in, nm_in, nv_in = _adamw(w_in[0], grad_in, m_w_in[0], v_w_in[0], "adamw_in")
    flat = lambda a, b, c: jnp.concatenate([a[0], b[0], c[0]], axis=0)
    d_sq, nm_sq, nv_sq = _adamw(flat(w_sb_proj, w_hg_proj, w_out), grad_sq.reshape(3 * ROW_SHARD, D_MODEL),
                                flat(m_w_sb_proj, m_w_hg_proj, m_w_out), flat(v_w_sb_proj, v_w_hg_proj, v_w_out),
                                "adamw_sq")

    pad = jnp.zeros((8, HEAD_DIM), F32).at[0, 0].set(loss[0, 0])
    part = jnp.concatenate([pad] + [a.reshape(-1, HEAD_DIM) for a in (d_ng, d_bg, d_lb, d_hgn, d_fg)], axis=0)
    sums = _sum_small(part)
    loss_out, g_sm, d_sm, nm_sm, nv_sm = _adamw_small(
        sums, _pack_small(norm_g, b_gate, lb_logits, hg_norm_g, final_norm_g),
        _pack_small(m_norm_g, m_b_gate, m_lb_logits, m_hg_norm_g, m_final_norm_g),
        _pack_small(v_norm_g, v_b_gate, v_lb_logits, v_hg_norm_g, v_final_norm_g))

    def big(t_in, t_sq):
        sq = t_sq.reshape(3, 1, ROW_SHARD, D_MODEL)
        return t_in[None], sq[0], sq[1], sq[2]

    def order(small, in_, sb, hg, out):
        ng, bg, lbl, hgn, fg = small
        return [ng, in_, bg, lbl, hgn, sb, hg, out, fg]

    outs = [loss_out[0, 0], grad_x[None]]
    for small, (t_in, t_sq) in ((g_sm, (grad_in, grad_sq)), (d_sm, (d_in, d_sq)), (nm_sm, (nm_in, nm_sq)), (nv_sm, (nv_in, nv_sq))):
        outs += order(_unpack_small(small), *big(t_in, t_sq))
    return tuple(outs)
```

```python
import functools

import jax
import jax.numpy as jnp
from jax import lax
from jax.experimental import pallas as pl
from jax.experimental.pallas import tpu as pltpu

F32 = jnp.float32
BF16 = jnp.bfloat16

D_MODEL = 1024
HEADS = 8
HEAD_DIM = 128
IN_WIDTH = 10240
N_CHIPS = 4
W_IN_SHARD = IN_WIDTH // N_CHIPS
ROW_SHARD = D_MODEL // N_CHIPS
RMS_EPS = 1e-6

OFF_SB_Q, OFF_SB_K, OFF_SB_V, OFF_SB_Z = 0, 1024, 2048, 3072
OFF_HG_Q, OFF_HG_F, OFF_HG_I, OFF_HG_Z, OFF_GATE = 4096, 5120, 6144, 7168, 8192

QKV_COLS = 3840
SB_BLOCK = 256
SB_PAIR = 2
SB_ROWS = 256
SB_DEAD = -110.0
SB_GONE = -1e30
HG_CHUNK = 32
HG_PAIR = 2 * HG_CHUNK
HG_STEP = 256
HG_MID = HG_CHUNK // 2 - 1

ADAM_LR, ADAM_B1, ADAM_B2, ADAM_EPS, ADAM_WD, ADAM_STEP = 0.001, 0.9, 0.999, 1e-08, 0.01, 10

VMEM_LIMIT = 56 * 1024 * 1024

MESH = pl.DeviceIdType.MESH


def _cparams(sem, vmem=VMEM_LIMIT):
    return pltpu.CompilerParams(dimension_semantics=sem, vmem_limit_bytes=vmem)


def _dot(a, b):
    return jnp.dot(a, b, preferred_element_type=F32)


def _dot_nt(a, b):
    return lax.dot_general(a, b, (((1,), (1,)), ((), ())), preferred_element_type=F32)


def _dot_tn(a, b):
    return lax.dot_general(a, b, (((0,), (0,)), ((), ())), preferred_element_type=F32)


def _split_dot(x, tri):
    hi = x.astype(BF16)
    lo = (x - hi.astype(F32)).astype(BF16)
    both = _dot(jnp.concatenate([hi, lo], axis=0), tri)
    return both[: x.shape[0]] + both[x.shape[0] :]


def _split_dot_left(tri, x):
    hi = x.astype(BF16)
    lo = (x - hi.astype(F32)).astype(BF16)
    return _dot(tri, hi) + _dot(tri, lo)


def _sigmoid(x):
    return 1.0 / (1.0 + jnp.exp(-x))


def _inproj(x, norm_g, w4):
    s_len = x.shape[0]
    ts = min(1024, s_len)
    tn = QKV_COLS // 3
    per = W_IN_SHARD // tn

    def body(x_ref, g_ref, w_ref, proj_ref, ht_ref, qkv_ref, h_scr):
        n = pl.program_id(1)

        @pl.when(n == 0)
        def _():
            xv = x_ref[...]
            r = lax.rsqrt(jnp.mean(xv * xv, axis=-1, keepdims=True) + RMS_EPS)
            hv = (xv * r) * g_ref[...]
            h_scr[...] = hv.astype(BF16)
            ht_ref[...] = hv.T.astype(BF16)

        p = _dot(h_scr[...], w_ref[0])
        proj_ref[...] = p

        @pl.when(n < 3)
        def _():
            qkv_ref[...] = p.astype(BF16)

    return pl.pallas_call(
        body,
        name="inproj",
        grid=(s_len // ts, IN_WIDTH // tn),
        in_specs=[
            pl.BlockSpec((ts, D_MODEL), lambda s, n: (s, 0)),
            pl.BlockSpec((1, D_MODEL), lambda s, n: (0, 0)),
            pl.BlockSpec((1, D_MODEL, tn), lambda s, n: (n // per, 0, n % per)),
        ],
        out_specs=[
            pl.BlockSpec((ts, tn), lambda s, n: (s, n)),
            pl.BlockSpec((D_MODEL, ts), lambda s, n: (0, s)),
            pl.BlockSpec((ts, tn), lambda s, n: (s, jnp.minimum(n, 2))),
        ],
        out_shape=[
            jax.ShapeDtypeStruct((s_len, IN_WIDTH), F32),
            jax.ShapeDtypeStruct((D_MODEL, s_len), BF16),
            jax.ShapeDtypeStruct((s_len, QKV_COLS), BF16),
        ],
        scratch_shapes=[pltpu.VMEM((ts, D_MODEL), BF16)],
        compiler_params=_cparams(("arbitrary", "arbitrary")),
    )(x, norm_g, w4)


def _sb_tile_fwd(qb, kb, row_gt_col, tri_excl, carry, diag):
    scale = HEAD_DIM ** -0.5
    z = _dot_nt(qb, kb) * scale
    ls_pos = jnp.minimum(z, 0.0) - jnp.log1p(jnp.exp(-jnp.abs(z)))
    log_not = ls_pos - z
    log_not_m = jnp.where(row_gt_col, log_not, 0.0) if diag else log_not
    surv = _split_dot(log_not_m, tri_excl) + carry
    w = jnp.exp(ls_pos + surv)
    if diag:
        w = jnp.where(row_gt_col, w, 0.0)
    return ls_pos, log_not, log_not_m, surv, w


def _sb_specs(s_len, blk):
    width = SB_PAIR * HEAD_DIM

    def blk_spec(off):
        return pl.BlockSpec((blk, width), lambda h, i: (i, off // width + h))

    def head_spec(off):
        return pl.BlockSpec((s_len, width), lambda h, i: (0, off // width + h))

    return blk_spec, head_spec


def _head_cols(p):
    return slice(p * HEAD_DIM, (p + 1) * HEAD_DIM)


def _sb_chains(blk):
    rows = min(SB_ROWS, blk)
    return [(p, a) for p in range(SB_PAIR) for a in range(blk // rows)], rows


def _sb_masks(blk, rows):
    row = lax.broadcasted_iota(jnp.int32, (rows, blk), 0)
    col = lax.broadcasted_iota(jnp.int32, (rows, blk), 1)
    causal = [row + a * rows > col for a in range(blk // rows)]
    row = lax.broadcasted_iota(jnp.int32, (blk, blk), 0)
    col = lax.broadcasted_iota(jnp.int32, (blk, blk), 1)
    tri_excl = (row > col).astype(BF16)
    tri_incl = (row >= col).astype(BF16)
    return causal, tri_excl, tri_incl


def _sb_alive(st, n_chain):
    alive = functools.reduce(jnp.maximum, [st[1 + 3 * c] for c in range(n_chain)])
    return jnp.max(alive) > SB_DEAD


def _sb_fwd(qkv):
    s_len = qkv.shape[0]
    blk = min(SB_BLOCK, s_len)
    nq = s_len // blk
    chains, rows = _sb_chains(blk)

    def body(q_ref, k_ref, v_ref, o_ref, of_ref):
        i = pl.program_id(1)
        causal, tri_excl, _ = _sb_masks(blk, rows)

        def tile(j, st, diag, valid=None):
            start = pl.multiple_of(j * blk, blk)
            new = []
            for c, (p, a) in enumerate(chains):
                carry, acc, acc_lo = st[3 * c : 3 * c + 3]
                if valid is not None:
                    carry = jnp.where(valid, carry, SB_GONE)
                kb = k_ref[pl.ds(start, blk), _head_cols(p)]
                vb = v_ref[pl.ds(start, blk), _head_cols(p)]
                qb = q_ref[a * rows : (a + 1) * rows, _head_cols(p)]
                _, _, log_not_m, surv, w = _sb_tile_fwd(qb, kb, causal[a], tri_excl, carry, diag)
                wb = w.astype(BF16)
                w_lo = (w - wb.astype(F32)).astype(BF16)
                both = _dot(jnp.concatenate([wb, w_lo], axis=0), vb)
                new += [surv[:, 0:1] + log_not_m[:, 0:1], acc + both[:rows], acc_lo + both[rows:]]
            return tuple(new)

        zero = jnp.zeros((rows, HEAD_DIM), F32)
        st = tile(i, (jnp.zeros((rows, 1), F32), zero, zero) * len(chains), True)
        st = tile(jnp.maximum(i - 1, 0), st, False, valid=i >= 1)

        def more(st):
            return (st[0] < i) & _sb_alive(st, len(chains))

        def step(st):
            return (st[0] + 1,) + tile(i - 1 - st[0], st[1:], False)

        st = lax.while_loop(more, step, (1,) + st)[1:]
        for c, (p, a) in enumerate(chains):
            o_ref[a * rows : (a + 1) * rows, _head_cols(p)] = st[3 * c + 1]
            of_ref[a * rows : (a + 1) * rows, _head_cols(p)] = st[3 * c + 1] + st[3 * c + 2]

    blk_spec, head_spec = _sb_specs(s_len, blk)
    return pl.pallas_call(
        body,
        name="sb_fwd",
        grid=(HEADS // SB_PAIR, nq),
        in_specs=[blk_spec(OFF_SB_Q), head_spec(OFF_SB_K), head_spec(OFF_SB_V)],
        out_specs=[blk_spec(0), blk_spec(0)],
        out_shape=[jax.ShapeDtypeStruct((s_len, D_MODEL), F32)] * 2,
        compiler_params=_cparams(("arbitrary", "arbitrary")),
    )(qkv, qkv, qkv)


def _sb_bwd(qkv, o_fine, d_o):
    s_len = qkv.shape[0]
    blk = min(SB_BLOCK, s_len)
    nq = s_len // blk
    scale = HEAD_DIM ** -0.5
    chains, rows = _sb_chains(blk)

    def body(q_ref, k_ref, v_ref, of_ref, do_ref, dq_ref, dk_ref, dv_ref, dk_acc, dv_acc):
        i = pl.program_id(1)

        @pl.when(i == 0)
        def _():
            dk_acc[...] = jnp.zeros_like(dk_acc)
            dv_acc[...] = jnp.zeros_like(dv_acc)

        dob = do_ref[...].astype(BF16)
        prod = dob.astype(F32) * of_ref[...]
        causal, tri_excl, tri_incl = _sb_masks(blk, rows)

        def group(x, p, a):
            return x[a * rows : (a + 1) * rows, _head_cols(p)]

        totals = [jnp.sum(group(prod, p, a), axis=-1, keepdims=True) for p, a in chains]

        def tile(j, st, diag, valid=None):
            start = pl.multiple_of(j * blk, blk)
            new = []
            dk_new = [None] * SB_PAIR
            dv_new = [None] * SB_PAIR
            for c, (p, a) in enumerate(chains):
                c_not, c_dlw, dq = st[3 * c : 3 * c + 3]
                if valid is not None:
                    c_not = jnp.where(valid, c_not, SB_GONE)
                qb, dob_c = group(q_ref, p, a), group(dob, p, a)
                kb = k_ref[pl.ds(start, blk), _head_cols(p)]
                vb = v_ref[pl.ds(start, blk), _head_cols(p)]
                ls_pos, log_not, log_not_m, surv, w = _sb_tile_fwd(qb, kb, causal[a], tri_excl, c_not, diag)
                dlw = _dot_nt(dob_c, vb) * w
                suffix = _split_dot(dlw, tri_incl)
                d_not = totals[c] - c_dlw - suffix
                dz = (dlw * jnp.exp(log_not) - d_not * jnp.exp(ls_pos)) * scale
                if diag:
                    dz = jnp.where(causal[a], dz, 0.0)
                if valid is not None:
                    dz = jnp.where(valid, dz, 0.0)
                dzb = dz.astype(BF16)
                dk_c, dv_c = _dot_tn(dzb, qb), _dot_tn(w.astype(BF16), dob_c)
                dk_new[p] = dk_c if dk_new[p] is None else dk_new[p] + dk_c
                dv_new[p] = dv_c if dv_new[p] is None else dv_new[p] + dv_c
                new += [surv[:, 0:1] + log_not_m[:, 0:1], c_dlw + suffix[:, 0:1], dq + _dot(dzb, kb)]
            for p in range(SB_PAIR):
                dk_acc[pl.ds(start, blk), _head_cols(p)] += dk_new[p]
                dv_acc[pl.ds(start, blk), _head_cols(p)] += dv_new[p]
            return tuple(new)

        zcol = jnp.zeros((rows, 1), F32)
        st = tile(i, (zcol, zcol, jnp.zeros((rows, HEAD_DIM), F32)) * len(chains), True)
        st = tile(jnp.maximum(i - 1, 0), st, False, valid=i >= 1)

        def more(st):
            return (st[0] < i) & _sb_alive(st, len(chains))

        def step(st):
            return (st[0] + 1,) + tile(i - 1 - st[0], st[1:], False)

        st = lax.while_loop(more, step, (1,) + st)[1:]
        for c, (p, a) in enumerate(chains):
            dq_ref[a * rows : (a + 1) * rows, _head_cols(p)] = st[3 * c + 2].astype(BF16)

        @pl.when(i == nq - 1)
        def _():
            dk_ref[...] = dk_acc[...].astype(BF16)
            dv_ref[...] = dv_acc[...].astype(BF16)

    blk_spec, head_spec = _sb_specs(s_len, blk)
    width = SB_PAIR * HEAD_DIM
    return pl.pallas_call(
        body,
        name="sb_bwd",
        grid=(HEADS // SB_PAIR, nq),
        in_specs=[blk_spec(OFF_SB_Q), head_spec(OFF_SB_K), head_spec(OFF_SB_V), blk_spec(0), blk_spec(0)],
        out_specs=[blk_spec(0), head_spec(0), head_spec(0)],
        out_shape=[jax.ShapeDtypeStruct((s_len, D_MODEL), BF16)] * 3,
        scratch_shapes=[pltpu.VMEM((s_len, width), F32), pltpu.VMEM((s_len, width), F32)],
        compiler_params=_cparams(("arbitrary", "arbitrary")),
    )(qkv, qkv, qkv, o_fine, d_o)


def _hg_lower_bound(lbl_ref):
    l0 = lbl_ref[0:1, :]
    l1 = lbl_ref[1:2, :]
    mx = jnp.maximum(l0, l1)
    e0 = jnp.exp(l0 - mx)
    e1 = jnp.exp(l1 - mx)
    return e0 / (e0 + e1)


def _hg_gates(hq, hf, lb):
    sig_f = _sigmoid(hf)
    f = lb + (1.0 - lb) * sig_f
    g = jnp.log(f)
    kk = 1.0 - f
    sig_q = _sigmoid(hq)
    qq = hq * sig_q
    return qq, kk, g, f, sig_f, sig_q


def _period_bcast(x, r, rows, period):
    w = x.shape[-1]
    x3 = x.reshape(rows // period, period, w)
    return jnp.broadcast_to(x3[:, r : r + 1, :], x3.shape).reshape(rows, w)


def _blockdiag(rows, kind):
    row = lax.broadcasted_iota(jnp.int32, (rows, rows), 0)
    col = lax.broadcasted_iota(jnp.int32, (rows, rows), 1)
    if kind in ("next", "prev"):
        first, second = (row, col) if kind == "next" else (col, row)
        keep = ((row // HG_PAIR) == (col // HG_PAIR)) & (first % HG_PAIR < HG_CHUNK) & (second % HG_PAIR >= HG_CHUNK)
    else:
        keep = (row // HG_CHUNK) == (col // HG_CHUNK)
        if kind == "lower":
            keep = keep & (row >= col)
        elif kind == "upper":
            keep = keep & (row <= col)
    return jnp.where(keep, 1.0, 0.0).astype(BF16)


def _hg_operands(hq, hf, lb, rows):
    qq, kk, g, f, sig_f, sig_q = _hg_gates(hq, hf, lb)
    cum = _split_dot_left(_blockdiag(rows, "lower"), g)
    mid = _period_bcast(cum, HG_MID, rows, HG_CHUNK)
    last = _period_bcast(cum, HG_CHUNK - 1, rows, HG_CHUNK)
    last0 = _period_bcast(cum, HG_CHUNK - 1, rows, HG_PAIR)
    last1 = _period_bcast(cum, HG_PAIR - 1, rows, HG_PAIR)
    second = (lax.broadcasted_iota(jnp.int32, cum.shape, 0) % HG_PAIR) >= HG_CHUNK
    e = dict(qm=jnp.exp(cum - mid), km=jnp.exp(mid - cum), qd=jnp.exp(cum), kl=jnp.exp(last - cum),
             q_in=jnp.where(second, jnp.exp(last0), 1.0), k_out=jnp.where(second, 1.0, jnp.exp(last1)),
             pair=jnp.exp(last0 + last1))
    v = dict(qm=qq * e["qm"], km=kk * e["km"], qd=qq * e["qd"], kl=kk * e["kl"])
    v["qp"] = v["qd"] * e["q_in"]
    v["kp"] = v["kl"] * e["k_out"]
    return v, e, second, (f, sig_f, sig_q)


def _hg_store_operands(v, second, hi, refs):
    zero = jnp.zeros_like(v["qm"])
    q_cat, k_cat, qp_b, kp_b, v_b = refs
    q_cat[:, 0:D_MODEL] = jnp.where(second, zero, v["qm"]).astype(BF16)
    q_cat[:, D_MODEL : 2 * D_MODEL] = jnp.where(second, v["qm"], zero).astype(BF16)
    q_cat[:, 2 * D_MODEL :] = jnp.where(second, v["qd"], zero).astype(BF16)
    k_cat[:, 0:D_MODEL] = jnp.where(second, zero, v["km"]).astype(BF16)
    k_cat[:, D_MODEL : 2 * D_MODEL] = jnp.where(second, v["km"], zero).astype(BF16)
    k_cat[:, 2 * D_MODEL :] = jnp.where(second, zero, v["kl"]).astype(BF16)
    qp_b[...] = v["qp"].astype(BF16)
    kp_b[...] = v["kp"].astype(BF16)
    v_b[...] = hi.astype(BF16)


def _hg_pair_operands(cat, r0, c0):
    return jnp.concatenate([cat[r0 : r0 + HG_PAIR, g * D_MODEL + c0 : g * D_MODEL + c0 + HEAD_DIM] for g in range(3)], axis=1)


def _hg_fwd(proj, lbl):
    s_len = proj.shape[0]
    rows = min(HG_STEP, s_len)
    n_pairs = rows // HG_PAIR

    def body(hq_ref, hf_ref, hi_ref, lbl_ref, o_ref, st_ref, state, q_cat, k_cat, qp_b, kp_b, v_b):
        @pl.when(pl.program_id(0) == 0)
        def _():
            state[...] = jnp.zeros_like(state)

        v, e, second, _ = _hg_operands(hq_ref[...], hf_ref[...], _hg_lower_bound(lbl_ref), rows)
        _hg_store_operands(v, second, hi_ref[...], (q_cat, k_cat, qp_b, kp_b, v_b))
        e_pair = e["pair"]
        row = lax.broadcasted_iota(jnp.int32, (HG_PAIR, HG_PAIR), 0)
        col = lax.broadcasted_iota(jnp.int32, (HG_PAIR, HG_PAIR), 1)
        causal = row >= col

        for u in range(n_pairs):
            r0 = u * HG_PAIR
            for h in range(HEADS):
                c0 = h * HEAD_DIM
                sl = (slice(r0, r0 + HG_PAIR), slice(c0, c0 + HEAD_DIM))
                st = state[h]
                st_ref[u, h] = st
                a = jnp.where(causal, _dot_nt(_hg_pair_operands(q_cat, r0, c0), _hg_pair_operands(k_cat, r0, c0)), 0.0)
                vb = v_b[sl]
                o_ref[sl] = _dot(a.astype(BF16), vb) + _dot_nt(qp_b[sl], st.astype(BF16))
                state[h] = st * e_pair[r0 : r0 + 1, c0 : c0 + HEAD_DIM] + _dot_tn(vb, kp_b[sl])

    def col_spec(off):
        return pl.BlockSpec((rows, D_MODEL), lambda s: (s, off // D_MODEL))

    bf_tile = pltpu.VMEM((rows, D_MODEL), BF16)
    bf_cat = pltpu.VMEM((rows, 3 * D_MODEL), BF16)
    scratch = [pltpu.VMEM((HEADS, HEAD_DIM, HEAD_DIM), F32), bf_cat, bf_cat, bf_tile, bf_tile, bf_tile]
    return pl.pallas_call(
        body,
        name="hg_fwd",
        grid=(s_len // rows,),
        in_specs=[col_spec(OFF_HG_Q), col_spec(OFF_HG_F), col_spec(OFF_HG_I), pl.BlockSpec((2, D_MODEL), lambda s: (0, 0))],
        out_specs=[
            pl.BlockSpec((rows, D_MODEL), lambda s: (s, 0)),
            pl.BlockSpec((n_pairs, HEADS, HEAD_DIM, HEAD_DIM), lambda s: (s, 0, 0, 0)),
        ],
        out_shape=[
            jax.ShapeDtypeStruct((s_len, D_MODEL), F32),
            jax.ShapeDtypeStruct((s_len // HG_PAIR, HEADS, HEAD_DIM, HEAD_DIM), F32),
        ],
        scratch_shapes=scratch,
        compiler_params=_cparams(("arbitrary",)),
    )(proj, proj, proj, lbl)


def _hg_bwd(proj, lbl, states, d_o):
    s_len = proj.shape[0]
    rows = min(HG_STEP, s_len)
    n_pairs = rows // HG_PAIR
    n_steps = s_len // rows

    def body(hq_ref, hf_ref, hi_ref, lbl_ref, st_ref, do_ref, dp_ref, dlb_ref,
             dstate, q_cat, k_cat, qp_b, kp_b, v_b, do_b, d_qcat, d_kcat, d_qp, d_kp, d_v, d_pair):
        @pl.when(pl.program_id(0) == 0)
        def _():
            dstate[...] = jnp.zeros_like(dstate)
            dlb_ref[...] = jnp.zeros_like(dlb_ref)

        lb = _hg_lower_bound(lbl_ref)
        hq = hq_ref[...]
        v, e, second, (f, sig_f, sig_q) = _hg_operands(hq, hf_ref[...], lb, rows)
        _hg_store_operands(v, second, hi_ref[...], (q_cat, k_cat, qp_b, kp_b, v_b))
        do_b[...] = do_ref[...].astype(BF16)
        e_pair = e["pair"]
        row = lax.broadcasted_iota(jnp.int32, (HG_PAIR, HG_PAIR), 0)
        col = lax.broadcasted_iota(jnp.int32, (HG_PAIR, HG_PAIR), 1)
        causal = row >= col

        for u in reversed(range(n_pairs)):
            r0 = u * HG_PAIR
            for h in range(HEADS):
                c0 = h * HEAD_DIM
                sl = (slice(r0, r0 + HG_PAIR), slice(c0, c0 + HEAD_DIM))
                st0 = st_ref[u, h]
                ds1 = dstate[h]
                ds1b = ds1.astype(BF16)
                dob, vb = do_b[sl], v_b[sl]
                lhs, rhs = _hg_pair_operands(q_cat, r0, c0), _hg_pair_operands(k_cat, r0, c0)
                a = jnp.where(causal, _dot_nt(lhs, rhs), 0.0).astype(BF16)
                da = jnp.where(causal, _dot_nt(dob, vb), 0.0).astype(BF16)
                d_v[sl] = _dot_tn(a, dob) + _dot_nt(kp_b[sl], ds1b)
                d_lhs = _dot(da, rhs)
                d_rhs = _dot_tn(da, lhs)
                for g in range(3):
                    gsl = (sl[0], slice(g * D_MODEL + c0, g * D_MODEL + c0 + HEAD_DIM))
                    d_qcat[gsl] = d_lhs[:, g * HEAD_DIM : (g + 1) * HEAD_DIM]
                    d_kcat[gsl] = d_rhs[:, g * HEAD_DIM : (g + 1) * HEAD_DIM]
                d_qp[sl] = _dot(dob, st0.astype(BF16))
                d_kp[sl] = _dot(vb, ds1b)
                decay = e_pair[r0 : r0 + 1, c0 : c0 + HEAD_DIM]
                d_pair[u : u + 1, c0 : c0 + HEAD_DIM] = decay * jnp.sum(ds1 * st0, axis=0, keepdims=True)
                dstate[h] = ds1 * decay + _dot_tn(dob, qp_b[sl])

        zero = jnp.zeros_like(hq)
        dqm = jnp.where(second, d_qcat[:, D_MODEL : 2 * D_MODEL], d_qcat[:, 0:D_MODEL])
        dkm = jnp.where(second, d_kcat[:, D_MODEL : 2 * D_MODEL], d_kcat[:, 0:D_MODEL])
        dqp, dkp = d_qp[...], d_kp[...]
        dqd = dqp * e["q_in"] + jnp.where(second, d_qcat[:, 2 * D_MODEL :], zero)
        dkl = dkp * e["k_out"] + jnp.where(second, zero, d_kcat[:, 2 * D_MODEL :])
        dq = dqm * e["qm"] + dqd * e["qd"]
        dk = dkm * e["km"] + dkl * e["kl"]
        t_kl = dkl * v["kl"]
        dcum = dqm * v["qm"] - dkm * v["km"] + dqd * v["qd"] - t_kl
        dp = d_pair[...]
        dp_b = jnp.broadcast_to(dp[:, None, :], (n_pairs, HG_PAIR, D_MODEL)).reshape(rows, D_MODEL)
        dg = (_split_dot_left(_blockdiag(rows, "upper"), dcum) + _split_dot_left(_blockdiag(rows, "all"), t_kl)
              + _split_dot_left(_blockdiag(rows, "next"), dqp * v["qp"])
              + _split_dot_left(_blockdiag(rows, "prev"), dkp * v["kp"]) + dp_b)
        df = dg / f - dk
        one_m = 1.0 - sig_f
        dp_ref[:, 0:D_MODEL] = (dq * (sig_q * (1.0 + hq * (1.0 - sig_q)))).astype(BF16)
        dp_ref[:, D_MODEL : 2 * D_MODEL] = (df * (1.0 - lb) * sig_f * one_m).astype(BF16)
        dp_ref[:, 2 * D_MODEL : 3 * D_MODEL] = d_v[...].astype(BF16)
        dlb_ref[...] += jnp.sum(df * one_m, axis=0, keepdims=True)

    def col_spec(off):
        return pl.BlockSpec((rows, D_MODEL), lambda s: (n_steps - 1 - s, off // D_MODEL))

    f32_tile = pltpu.VMEM((rows, D_MODEL), F32)
    f32_cat = pltpu.VMEM((rows, 3 * D_MODEL), F32)
    bf_tile = pltpu.VMEM((rows, D_MODEL), BF16)
    bf_cat = pltpu.VMEM((rows, 3 * D_MODEL), BF16)
    scratch = [pltpu.VMEM((HEADS, HEAD_DIM, HEAD_DIM), F32), bf_cat, bf_cat, bf_tile, bf_tile, bf_tile, bf_tile,
               f32_cat, f32_cat, f32_tile, f32_tile, f32_tile, pltpu.VMEM((n_pairs, D_MODEL), F32)]
    return pl.pallas_call(
        body,
        name="hg_bwd",
        grid=(n_steps,),
        in_specs=[
            col_spec(OFF_HG_Q), col_spec(OFF_HG_F), col_spec(OFF_HG_I),
            pl.BlockSpec((2, D_MODEL), lambda s: (0, 0)),
            pl.BlockSpec((n_pairs, HEADS, HEAD_DIM, HEAD_DIM), lambda s: (n_steps - 1 - s, 0, 0, 0)),
            pl.BlockSpec((rows, D_MODEL), lambda s: (n_steps - 1 - s, 0)),
        ],
        out_specs=[
            pl.BlockSpec((rows, 3 * D_MODEL), lambda s: (n_steps - 1 - s, 0)),
            pl.BlockSpec((1, D_MODEL), lambda s: (0, 0)),
        ],
        out_shape=[
            jax.ShapeDtypeStruct((s_len, 3 * D_MODEL), BF16),
            jax.ShapeDtypeStruct((1, D_MODEL), F32),
        ],
        scratch_shapes=scratch,
        compiler_params=_cparams(("arbitrary",)),
    )(proj, proj, proj, lbl, states, d_o)


def _mid(proj, sb_o, hg_o, x, target, b_gate, hg_gain, final_g, w_sb, w_hg, w_out):
    s_len = proj.shape[0]
    ts = min(128, s_len)
    inv_d = 1.0 / D_MODEL

    def body(zsb_ref, hz_ref, gl_ref, sbo_ref, hgo_ref, x_ref, tgt_ref, bg_ref, hgn_ref, fg_ref,
             wsb_ref, whg_ref, wout_ref,
             dout_ref, dsbo_ref, dhgo_ref, dzsb_ref, dhz_ref, dgl_ref,
             asb_ref, dusb_ref, ahg_ref, duhg_ref, y_ref, doutb_ref,
             loss_ref, dfg_ref, dbg_ref, dhgn_ref):
        @pl.when(pl.program_id(0) == 0)
        def _():
            loss_ref[...] = jnp.zeros_like(loss_ref)
            dfg_ref[...] = jnp.zeros_like(dfg_ref)
            dbg_ref[...] = jnp.zeros_like(dbg_ref)
            dhgn_ref[...] = jnp.zeros_like(dhgn_ref)

        z_sb = zsb_ref[...]
        sb_o = sbo_ref[...]
        sig_zsb = _sigmoid(z_sb)
        silu_zsb = z_sb * sig_zsb
        a_sb = (sb_o * silu_zsb).astype(BF16)
        u_sb = _dot(a_sb, wsb_ref[...])

        hg_o = hgo_ref[...]
        gain = hgn_ref[...]
        r_parts, yn_parts = [], []
        for h in range(HEADS):
            oh = hg_o[:, h * HEAD_DIM : (h + 1) * HEAD_DIM]
            r = lax.rsqrt(jnp.mean(oh * oh, axis=-1, keepdims=True) + RMS_EPS)
            r_parts.append(jnp.broadcast_to(r, oh.shape))
            yn_parts.append(oh * r)
        r_hg = jnp.concatenate(r_parts, axis=-1)
        yn_hg = jnp.concatenate(yn_parts, axis=-1)
        hn = yn_hg * gain
        hz = hz_ref[...]
        sig_hz = _sigmoid(hz)
        silu_hz = hz * sig_hz
        a_hg = (hn * silu_hz).astype(BF16)
        u_hg = _dot(a_hg, whg_ref[...])

        gates = _sigmoid(gl_ref[...] + bg_ref[...])
        g_sb = gates[:, 0:D_MODEL]
        g_hg = gates[:, D_MODEL:]
        y = (g_sb * u_sb + g_hg * u_hg).astype(BF16)
        out = x_ref[...] + _dot(y, wout_ref[...])
        r2 = lax.rsqrt(jnp.mean(out * out, axis=-1, keepdims=True) + RMS_EPS)
        yn = out * r2
        fg = fg_ref[...]
        diff = yn * fg - tgt_ref[...]
        loss_ref[...] += 0.5 * inv_d * jnp.sum(diff * diff)

        dyf = diff * inv_d
        dfg_ref[...] += jnp.sum(dyf * yn, axis=0, keepdims=True)
        dyn = dyf * fg
        dout = r2 * (dyn - yn * jnp.mean(dyn * yn, axis=-1, keepdims=True))
        dout_ref[...] = dout
        doutb = dout.astype(BF16)
        doutb_ref[...] = doutb
        dy = _dot_nt(doutb, wout_ref[...])
        du_sb = (dy * g_sb).astype(BF16)
        du_hg = (dy * g_hg).astype(BF16)
        dgl_sb = dy * u_sb * g_sb * (1.0 - g_sb)
        dgl_hg = dy * u_hg * g_hg * (1.0 - g_hg)
        dgl_ref[:, 0:D_MODEL] = dgl_sb.astype(BF16)
        dgl_ref[:, D_MODEL:] = dgl_hg.astype(BF16)
        dbg_ref[:, 0:D_MODEL] += jnp.sum(dgl_sb, axis=0, keepdims=True)
        dbg_ref[:, D_MODEL:] += jnp.sum(dgl_hg, axis=0, keepdims=True)

        da_sb = _dot_nt(du_sb, wsb_ref[...])
        dsbo_ref[...] = da_sb * silu_zsb
        dzsb_ref[...] = (da_sb * sb_o * (sig_zsb * (1.0 + z_sb * (1.0 - sig_zsb)))).astype(BF16)

        da_hg = _dot_nt(du_hg, whg_ref[...])
        dhn = da_hg * silu_hz
        dhz_ref[...] = (da_hg * hn * (sig_hz * (1.0 + hz * (1.0 - sig_hz)))).astype(BF16)
        dhgn_ref[...] += jnp.sum(dhn * yn_hg, axis=0, keepdims=True)
        dyn_hg = dhn * gain
        prod = dyn_hg * yn_hg
        m_parts = []
        for h in range(HEADS):
            ph = prod[:, h * HEAD_DIM : (h + 1) * HEAD_DIM]
            m_parts.append(jnp.broadcast_to(jnp.mean(ph, axis=-1, keepdims=True), ph.shape))
        dhgo_ref[...] = r_hg * (dyn_hg - yn_hg * jnp.concatenate(m_parts, axis=-1))

        asb_ref[...] = a_sb
        dusb_ref[...] = du_sb
        ahg_ref[...] = a_hg
        duhg_ref[...] = du_hg
        y_ref[...] = y

    def tile(width, off=0):
        return pl.BlockSpec((ts, width), lambda s: (s, off // width))

    def whole(shape):
        return pl.BlockSpec(shape, lambda s: (0,) * len(shape))

    sq = (D_MODEL, D_MODEL)
    f32_act = jax.ShapeDtypeStruct((s_len, D_MODEL), F32)
    bf_act = jax.ShapeDtypeStruct((s_len, D_MODEL), BF16)
    return pl.pallas_call(
        body,
        name="mid",
        grid=(s_len // ts,),
        in_specs=[
            tile(D_MODEL, OFF_SB_Z), tile(D_MODEL, OFF_HG_Z), tile(2 * D_MODEL, OFF_GATE),
            tile(D_MODEL), tile(D_MODEL), tile(D_MODEL), tile(D_MODEL),
            whole((1, 2 * D_MODEL)), whole((1, D_MODEL)), whole((1, D_MODEL)),
            whole(sq), whole(sq), whole(sq),
        ],
        out_specs=[
            tile(D_MODEL), tile(D_MODEL), tile(D_MODEL), tile(D_MODEL), tile(D_MODEL), tile(2 * D_MODEL),
            tile(D_MODEL), tile(D_MODEL), tile(D_MODEL), tile(D_MODEL), tile(D_MODEL), tile(D_MODEL),
            whole((1, 1)), whole((1, D_MODEL)), whole((1, 2 * D_MODEL)), whole((1, D_MODEL)),
        ],
        out_shape=[
            f32_act, f32_act, f32_act, bf_act, bf_act, jax.ShapeDtypeStruct((s_len, 2 * D_MODEL), BF16),
            bf_act, bf_act, bf_act, bf_act, bf_act, bf_act,
            jax.ShapeDtypeStruct((1, 1), F32), jax.ShapeDtypeStruct((1, D_MODEL), F32),
            jax.ShapeDtypeStruct((1, 2 * D_MODEL), F32), jax.ShapeDtypeStruct((1, D_MODEL), F32),
        ],
        compiler_params=_cparams(("arbitrary",)),
    )(proj, proj, proj, sb_o, hg_o, x, target, b_gate, hg_gain, final_g, w_sb, w_hg, w_out)


def _grad_matmul(a, b, name, tn):
    s_len, m = a.shape
    n = b.shape[1]
    tk = min(512, s_len)

    def body(a_ref, b_ref, o_ref):
        @pl.when(pl.program_id(1) == 0)
        def _():
            o_ref[...] = jnp.zeros_like(o_ref)

        o_ref[...] += _dot_tn(a_ref[...], b_ref[...])

    return pl.pallas_call(
        body,
        name=name,
        grid=(n // tn, s_len // tk),
        in_specs=[pl.BlockSpec((tk, m), lambda j, k: (k, 0)), pl.BlockSpec((tk, tn), lambda j, k: (k, j))],
        out_specs=pl.BlockSpec((m, tn), lambda j, k: (0, j)),
        out_shape=jax.ShapeDtypeStruct((m, n), F32),
        compiler_params=_cparams(("arbitrary", "arbitrary")),
    )(a, b)


SEG_WIDTHS = (1024, 1024, 1024, 1024, 3072, 1024, 2048)


def _seg_bounds(tile):
    bounds = [0]
    for w in SEG_WIDTHS:
        bounds.append(bounds[-1] + w // tile)
    return bounds


def _grad_w_in(h_t, segs):
    m, s_len = h_t.shape
    tk = min(1024, s_len)
    tn = 1024
    nk = s_len // tk
    bounds = _seg_bounds(tn)

    def body(a_ref, *refs):
        seg_refs, o_ref = refs[:-1], refs[-1]
        j = pl.program_id(0)

        @pl.when(pl.program_id(1) == 0)
        def _():
            o_ref[...] = jnp.zeros_like(o_ref)

        for i, ref in enumerate(seg_refs):
            @pl.when((j >= bounds[i]) & (j < bounds[i + 1]))
            def _(ref=ref):
                o_ref[...] += _dot(a_ref[...], ref[...])

    def seg_spec(lo, hi):
        def index(j, k):
            return (jnp.where(j < lo, 0, jnp.where(j >= hi, nk - 1, k)), jnp.clip(j - lo, 0, hi - lo - 1))
        return pl.BlockSpec((tk, tn), index)

    return pl.pallas_call(
        body,
        name="grad_w_in",
        grid=(IN_WIDTH // tn, nk),
        in_specs=[pl.BlockSpec((m, tk), lambda j, k: (0, k))] + [seg_spec(bounds[i], bounds[i + 1]) for i in range(7)],
        out_specs=pl.BlockSpec((m, tn), lambda j, k: (0, j)),
        out_shape=jax.ShapeDtypeStruct((m, IN_WIDTH), F32),
        compiler_params=_cparams(("arbitrary", "arbitrary")),
    )(h_t, *segs)


EXCHANGE_IN_PIECES = 8
EXCHANGE_PIECES = EXCHANGE_IN_PIECES + 3


def _exchange_copies(sin_ref, ssq_ref, got_in, got_sq, send_sems, recv_sems):
    _, _, c, chips = _position()
    rows = HALF_IN // EXCHANGE_IN_PIECES
    copies = []
    for k, (px, py) in enumerate(chips):
        chip = 2 * px + py
        for p in range(EXCHANGE_PIECES):
            if p < EXCHANGE_IN_PIECES:
                src, dst = sin_ref.at[chip, pl.ds(p * rows, rows), :], got_in.at[k, pl.ds(p * rows, rows), :]
            else:
                src, dst = ssq_ref.at[p - EXCHANGE_IN_PIECES, chip], got_sq.at[k, p - EXCHANGE_IN_PIECES]
            copies.append(_remote(src, dst, send_sems.at[k, p], recv_sems.at[k, p], (px, py, c)))
    return copies


def _dx(segs, w4, x, norm_g, dout, s_in, s_sq):
    s_len = x.shape[0]
    ts = min(1024, s_len)
    tk = 512
    per = W_IN_SHARD // tk
    nk = IN_WIDTH // tk
    ns = s_len // ts
    bounds = _seg_bounds(tk)

    def body(*refs):
        seg_refs = refs[:7]
        w_ref, x_ref, g_ref, dout_ref, sin_ref, ssq_ref, gx_ref, dg_ref, got_in, got_sq, acc, send_sems, recv_sems = refs[7:]
        s, k = pl.program_id(0), pl.program_id(1)

        @pl.when((s == 0) & (k == 0))
        def _():
            dg_ref[...] = jnp.zeros_like(dg_ref)
            for cp in _exchange_copies(sin_ref, ssq_ref, got_in, got_sq, send_sems, recv_sems):
                cp.start()

        @pl.when(k == 0)
        def _():
            acc[...] = jnp.zeros_like(acc)

        for i, ref in enumerate(seg_refs):
            @pl.when((k >= bounds[i]) & (k < bounds[i + 1]))
            def _(ref=ref):
                acc[...] += _dot_nt(ref[...], w_ref[0])

        @pl.when(k == nk - 1)
        def _():
            dh = acc[...]
            xv = x_ref[...]
            r = lax.rsqrt(jnp.mean(xv * xv, axis=-1, keepdims=True) + RMS_EPS)
            xn = xv * r
            dg_ref[...] += jnp.sum(dh * xn, axis=0, keepdims=True)
            dxn = dh * g_ref[...]
            gx_ref[...] = r * (dxn - xn * jnp.mean(dxn * xn, axis=-1, keepdims=True)) + dout_ref[...]

        @pl.when((s == ns - 1) & (k == nk - 1))
        def _():
            for cp in _exchange_copies(sin_ref, ssq_ref, got_in, got_sq, send_sems, recv_sems):
                cp.wait()

    def seg_spec(lo, hi):
        return pl.BlockSpec((ts, tk), lambda s, k: (s, jnp.clip(k - lo, 0, hi - lo - 1)))

    row_tile = pl.BlockSpec((ts, D_MODEL), lambda s, k: (s, 0))
    vec = pl.BlockSpec((1, D_MODEL), lambda s, k: (0, 0))
    return pl.pallas_call(
        body,
        name="dx",
        grid=(ns, nk),
        in_specs=[seg_spec(bounds[i], bounds[i + 1]) for i in range(7)] + [
            pl.BlockSpec((1, D_MODEL, tk), lambda s, k: (k // per, 0, k % per)),
            row_tile, vec, row_tile, ANY, ANY,
        ],
        out_specs=[row_tile, vec, ANY, ANY],
        out_shape=[jax.ShapeDtypeStruct((s_len, D_MODEL), F32), jax.ShapeDtypeStruct((1, D_MODEL), F32),
                   jax.ShapeDtypeStruct((3, HALF_IN, W_IN_SHARD), WIRE),
                   jax.ShapeDtypeStruct((3, 3, HALF_SQ, D_MODEL), WIRE)],
        scratch_shapes=[pltpu.VMEM((ts, D_MODEL), F32),
                        pltpu.SemaphoreType.DMA((3, EXCHANGE_PIECES)), pltpu.SemaphoreType.DMA((3, EXCHANGE_PIECES))],
        compiler_params=_cparams(("arbitrary", "arbitrary")),
    )(*segs, w4, x, norm_g, dout, s_in, s_sq)


def _local_grads(x, target, norm_g, b_gate, lbl, hg_gain, final_g, w4, w_sb, w_hg, w_out):
    proj, h_t, qkv = _inproj(x, norm_g, w4)
    sb_o, sb_o_fine = _sb_fwd(qkv)
    hg_o, states = _hg_fwd(proj, lbl)
    (dout, d_sbo, d_hgo, d_zsb, d_hz, d_gl, a_sb, du_sb, a_hg, du_hg, y, doutb,
     loss, d_fg, d_bg, d_hgn) = _mid(proj, sb_o, hg_o, x, target, b_gate, hg_gain, final_g, w_sb, w_hg, w_out)
    g_w_sb = _grad_matmul(a_sb, du_sb, "grad_w_sb", 512)
    g_w_hg = _grad_matmul(a_hg, du_hg, "grad_w_hg", 512)
    g_w_out = _grad_matmul(y, doutb, "grad_w_out", 512)
    d_q, d_k, d_v = _sb_bwd(qkv, sb_o_fine, d_sbo)
    d_hg, d_lb = _hg_bwd(proj, lbl, states, d_hgo)
    segs = (d_q, d_k, d_v, d_zsb, d_hg, d_hz, d_gl)
    g_w_in = _grad_w_in(h_t, segs)
    return g_w_in, g_w_sb, g_w_hg, g_w_out, segs, dout, loss, d_bg, d_lb, d_hgn, d_fg


ANY = pl.BlockSpec(memory_space=pl.ANY)
WIRE = BF16
HALF_IN = D_MODEL // 2
HALF_SQ = ROW_SHARD // 2


def _position():
    x, y, c = lax.axis_index("x"), lax.axis_index("y"), lax.axis_index("c")
    chips = [(1 - x, y), (x, 1 - y), (1 - x, 1 - y)]
    return x, y, c, chips


def _remote(src, dst, send_sem, recv_sem, to):
    return pltpu.make_async_remote_copy(src_ref=src, dst_ref=dst, send_sem=send_sem, recv_sem=recv_sem,
                                        device_id=to, device_id_type=MESH)


def _gather_weights(w_in_b, w_sq_b):
    n_in = 4
    n_piece = n_in + 3
    rows = HALF_IN // n_in

    def body(win_ref, wsq_ref, in_ref, sq_ref, send_sems, recv_sems):
        x, y, c, chips = _position()
        me = 2 * x + y
        sibling = (x, y, 1 - c)

        def src_piece(p):
            if p < n_in:
                return win_ref.at[pl.ds(c * HALF_IN + p * rows, rows), :]
            return wsq_ref.at[p - n_in, pl.ds(c * HALF_SQ, HALF_SQ), :]

        def piece(p, chip, core):
            if p < n_in:
                return in_ref.at[chip, pl.ds(core * HALF_IN + p * rows, rows), :]
            return sq_ref.at[p - n_in, chip, pl.ds(core * HALF_SQ, HALF_SQ), :]

        sends = []
        for k, (px, py) in enumerate(chips):
            for p in range(n_piece):
                sends.append(_remote(src_piece(p), piece(p, me, c), send_sems.at[k, p], recv_sems.at[k, p], (px, py, c)))
        for cp in sends:
            cp.start()
        for k, (px, py) in enumerate(chips):
            chip = 2 * px + py
            for p in range(n_piece):
                got = piece(p, chip, c)
                _remote(got, got, send_sems.at[k, p], recv_sems.at[k, p], (px, py, c)).wait_recv()
                fwd = _remote(got, got, send_sems.at[3 + k, p], recv_sems.at[3 + k, p], sibling)
                fwd.start()
                sends.append(fwd)
        for k, (px, py) in enumerate(chips):
            chip = 2 * px + py
            for p in range(n_piece):
                got = piece(p, chip, 1 - c)
                _remote(got, got, send_sems.at[3 + k, p], recv_sems.at[3 + k, p], sibling).wait_recv()
        for cp in sends:
            cp.wait_send()

    return pl.pallas_call(
        body,
        name="gather_weights",
        in_specs=[ANY, ANY],
        out_specs=[ANY, ANY],
        out_shape=[jax.ShapeDtypeStruct((N_CHIPS, D_MODEL, W_IN_SHARD), BF16),
                   jax.ShapeDtypeStruct((3, N_CHIPS, ROW_SHARD, D_MODEL), BF16)],
        scratch_shapes=[pltpu.SemaphoreType.DMA((6, n_piece)), pltpu.SemaphoreType.DMA((6, n_piece))],
    )(w_in_b, w_sq_b)


def _place_own(idx, w_in_b, w_sq_b, w4, wsq):
    n = 4
    r_in, r_sq = D_MODEL // n, ROW_SHARD // n

    def body(idx_ref, win_ref, wsq_ref, w4_in, wsq_in, w4_out, wsq_out):
        w4_out[0] = win_ref[...]
        wsq_out[:, 0] = wsq_ref[...]

    grid_spec = pltpu.PrefetchScalarGridSpec(
        num_scalar_prefetch=1,
        grid=(n,),
        in_specs=[pl.BlockSpec((r_in, W_IN_SHARD), lambda r, idx: (r, 0)),
                  pl.BlockSpec((3, r_sq, D_MODEL), lambda r, idx: (0, r, 0)), ANY, ANY],
        out_specs=[pl.BlockSpec((1, r_in, W_IN_SHARD), lambda r, idx: (idx[0], r, 0)),
                   pl.BlockSpec((3, 1, r_sq, D_MODEL), lambda r, idx: (0, idx[0], r, 0))],
    )
    return pl.pallas_call(
        body,
        name="place_own",
        grid_spec=grid_spec,
        out_shape=[jax.ShapeDtypeStruct(w4.shape, BF16), jax.ShapeDtypeStruct(wsq.shape, BF16)],
        input_output_aliases={3: 0, 4: 1},
        compiler_params=_cparams(("arbitrary",)),
    )(idx, w_in_b, w_sq_b, w4, wsq)


def _swap_halves(g_in, g_sq):
    n_in = 16
    n_piece = n_in + 3 * N_CHIPS
    rows = HALF_IN // n_in

    def body(gin_ref, gsq_ref, got_in, got_sq, send_sems, recv_sems):
        x, y, c, _ = _position()
        sibling = (x, y, 1 - c)

        def src_piece(p):
            if p < n_in:
                return gin_ref.at[pl.ds((1 - c) * HALF_IN + p * rows, rows), :]
            a, chip = divmod(p - n_in, N_CHIPS)
            return gsq_ref.at[a, chip, pl.ds((1 - c) * HALF_SQ, HALF_SQ), :]

        def dst_piece(p):
            if p < n_in:
                return got_in.at[pl.ds(p * rows, rows), :]
            a, chip = divmod(p - n_in, N_CHIPS)
            return got_sq.at[a, chip]

        out = [_remote(src_piece(p), dst_piece(p), send_sems.at[p], recv_sems.at[p], sibling) for p in range(n_piece)]
        for cp in out:
            cp.start()
        for cp in out:
            cp.wait()

    return pl.pallas_call(
        body,
        name="swap_halves",
        in_specs=[ANY, ANY],
        out_specs=[ANY, ANY],
        out_shape=[jax.ShapeDtypeStruct((HALF_IN, IN_WIDTH), F32),
                   jax.ShapeDtypeStruct((3, N_CHIPS, HALF_SQ, D_MODEL), F32)],
        scratch_shapes=[pltpu.SemaphoreType.DMA((n_piece,))] * 2,
    )(g_in, g_sq)


def _join_halves(r_in, r_sq):
    n_in = 16
    n_piece = n_in + 3
    rows = HALF_IN // n_in

    def body(in_alias, sq_alias, full_in, full_sq, send_sems, recv_sems):
        del in_alias, sq_alias
        x, y, c, _ = _position()
        sibling = (x, y, 1 - c)

        def piece(p, core):
            if p < n_in:
                return full_in.at[pl.ds(core * HALF_IN + p * rows, rows), :]
            return full_sq.at[p - n_in, pl.ds(core * HALF_SQ, HALF_SQ), :]

        out = [_remote(piece(p, c), piece(p, c), send_sems.at[p], recv_sems.at[p], sibling) for p in range(n_piece)]
        for cp in out:
            cp.start()
        for p in range(n_piece):
            _remote(piece(p, 1 - c), piece(p, 1 - c), send_sems.at[p], recv_sems.at[p], sibling).wait_recv()
        for cp in out:
            cp.wait_send()

    return pl.pallas_call(
        body,
        name="join_halves",
        in_specs=[ANY, ANY],
        out_specs=[ANY, ANY],
        out_shape=[jax.ShapeDtypeStruct((D_MODEL, W_IN_SHARD), F32),
                   jax.ShapeDtypeStruct((3, ROW_SHARD, D_MODEL), F32)],
        input_output_aliases={0: 0, 1: 1},
        scratch_shapes=[pltpu.SemaphoreType.DMA((n_piece,)), pltpu.SemaphoreType.DMA((n_piece,))],
    )(r_in, r_sq)


SMALL_ROWS = 56
N_DEV = 8


def _sum_small(part):
    def body(part_ref, out_ref, slots, send_sems, recv_sems):
        x, y, c, _ = _position()
        me = 4 * x + 2 * y + c
        slots[me] = part_ref[...]
        out = []
        for r in range(1, N_DEV):
            rx, ry, rc = (r >> 2) & 1, (r >> 1) & 1, r & 1
            to = (1 - x if rx else x, 1 - y if ry else y, 1 - c if rc else c)
            out.append(_remote(part_ref, slots.at[me], send_sems.at[r - 1], recv_sems.at[r - 1], to))
        for cp in out:
            cp.start()
        for r in range(1, N_DEV):
            _remote(part_ref, slots.at[me ^ r], send_sems.at[r - 1], recv_sems.at[r - 1], (x, y, c)).wait_recv()
        for cp in out:
            cp.wait_send()
        total = slots[0]
        for d in range(1, N_DEV):
            total = total + slots[d]
        out_ref[...] = total

    vmem = pl.BlockSpec(memory_space=pltpu.VMEM)
    return pl.pallas_call(
        body,
        name="sum_small",
        in_specs=[vmem],
        out_specs=vmem,
        out_shape=jax.ShapeDtypeStruct((SMALL_ROWS, HEAD_DIM), F32),
        scratch_shapes=[pltpu.VMEM((N_DEV, SMALL_ROWS, HEAD_DIM), F32),
                        pltpu.SemaphoreType.DMA((N_DEV - 1,)), pltpu.SemaphoreType.DMA((N_DEV - 1,))],
    )(part)


def _prefetch_call(body, name, idx, grid, in_specs, out_specs, out_shape, args):
    grid_spec = pltpu.PrefetchScalarGridSpec(num_scalar_prefetch=1, grid=grid, in_specs=in_specs, out_specs=out_specs)
    return pl.pallas_call(body, name=name, grid_spec=grid_spec, out_shape=out_shape,
                          compiler_params=_cparams(("arbitrary",) * len(grid)))(idx, *args)


def _sum_a_in(idx, g_in, got_in):
    tr = 128
    nr = HALF_IN // tr

    def body(idx_ref, a_ref, b_ref, o_ref):
        o_ref[0] = (a_ref[...] + b_ref[...]).astype(WIRE)

    return _prefetch_call(
        body, "sum_a_in", idx, (N_CHIPS, nr),
        [pl.BlockSpec((tr, W_IN_SHARD), lambda j, r, idx: (idx[1] * nr + r, j)),
         pl.BlockSpec((tr, W_IN_SHARD), lambda j, r, idx: (r, j))],
        pl.BlockSpec((1, tr, W_IN_SHARD), lambda j, r, idx: (j, r, 0)),
        jax.ShapeDtypeStruct((N_CHIPS, HALF_IN, W_IN_SHARD), WIRE), (g_in, got_in))


def _sum_a_sq(idx, g_sq, got_sq):
    blk = (1, 1, HALF_SQ, D_MODEL)

    def body(idx_ref, a_ref, b_ref, o_ref):
        o_ref[...] = (a_ref[...] + b_ref[...]).astype(WIRE)

    return _prefetch_call(
        body, "sum_a_sq", idx, (3, N_CHIPS),
        [pl.BlockSpec(blk, lambda a, j, idx: (a, j, idx[1], 0)), pl.BlockSpec(blk, lambda a, j, idx: (a, j, 0, 0))],
        pl.BlockSpec(blk, lambda a, j, idx: (a, j, 0, 0)),
        jax.ShapeDtypeStruct((3, N_CHIPS, HALF_SQ, D_MODEL), WIRE), (g_sq, got_sq))


def _sum_b_in(idx, s_in, got_in):
    tr = 128
    nr = HALF_IN // tr

    def body(idx_ref, a_ref, b_ref, o_ref):
        o_ref[...] = ((a_ref[0].astype(F32) + b_ref[0].astype(F32)) + b_ref[1].astype(F32)) + b_ref[2].astype(F32)

    return _prefetch_call(
        body, "sum_b_in", idx, (nr,),
        [pl.BlockSpec((1, tr, W_IN_SHARD), lambda r, idx: (idx[0], r, 0)),
         pl.BlockSpec((3, tr, W_IN_SHARD), lambda r, idx: (0, r, 0))],
        pl.BlockSpec((tr, W_IN_SHARD), lambda r, idx: (idx[1] * nr + r, 0)),
        jax.ShapeDtypeStruct((D_MODEL, W_IN_SHARD), F32), (s_in, got_in))


def _sum_b_sq(idx, s_sq, got_sq):
    def body(idx_ref, a_ref, b_ref, o_ref):
        o_ref[0] = ((a_ref[0, 0].astype(F32) + b_ref[0, 0].astype(F32)) + b_ref[1, 0].astype(F32)) + b_ref[2, 0].astype(F32)

    return _prefetch_call(
        body, "sum_b_sq", idx, (3,),
        [pl.BlockSpec((1, 1, HALF_SQ, D_MODEL), lambda a, idx: (a, idx[0], 0, 0)),
         pl.BlockSpec((3, 1, HALF_SQ, D_MODEL), lambda a, idx: (0, a, 0, 0))],
        pl.BlockSpec((1, HALF_SQ, D_MODEL), lambda a, idx: (a, idx[1], 0)),
        jax.ShapeDtypeStruct((3, ROW_SHARD, D_MODEL), F32), (s_sq, got_sq))


def _adamw_math(w, g, m, v):
    m = ADAM_B1 * m + (1.0 - ADAM_B1) * g
    v = ADAM_B2 * v + (1.0 - ADAM_B2) * (g * g)
    m_hat = m / (1.0 - ADAM_B1 ** ADAM_STEP)
    v_hat = v / (1.0 - ADAM_B2 ** ADAM_STEP)
    delta = -ADAM_LR * (m_hat / (jnp.sqrt(v_hat) + ADAM_EPS) + ADAM_WD * w)
    return delta, m, v


def _adamw(w, g, m, v, name):
    rows, cols = w.shape
    tr = min(128, rows)

    def body(w_ref, g_ref, m_ref, v_ref, d_ref, nm_ref, nv_ref):
        d_ref[...], nm_ref[...], nv_ref[...] = _adamw_math(w_ref[...], g_ref[...], m_ref[...], v_ref[...])

    spec = pl.BlockSpec((tr, cols), lambda r: (r, 0))
    return pl.pallas_call(
        body,
        name=name,
        grid=(rows // tr,),
        in_specs=[spec] * 4,
        out_specs=[spec] * 3,
        out_shape=[jax.ShapeDtypeStruct((rows, cols), F32)] * 3,
        compiler_params=_cparams(("arbitrary",)),
    )(w, g, m, v)


def _adamw_small(sums, w, m, v):
    def body(s_ref, w_ref, m_ref, v_ref, loss_ref, g_ref, d_ref, nm_ref, nv_ref):
        s = s_ref[...]
        w = w_ref[...]
        loss_ref[...] = s[0:1, 0:1]
        l0, l1 = w[24:32], w[32:40]
        mx = jnp.maximum(l0, l1)
        e0, e1 = jnp.exp(l0 - mx), jnp.exp(l1 - mx)
        p0, p1 = e0 / (e0 + e1), e1 / (e0 + e1)
        d_lb = s[32:40]
        g = jnp.concatenate([s[8:16], s[16:32], d_lb * p0 * (1.0 - p0), -d_lb * p0 * p1, s[40:48], s[48:56]], axis=0)
        g_ref[...] = g
        d_ref[...], nm_ref[...], nv_ref[...] = _adamw_math(w, g, m_ref[...], v_ref[...])

    packed = jax.ShapeDtypeStruct((SMALL_ROWS, HEAD_DIM), F32)
    return pl.pallas_call(
        body,
        name="adamw_small",
        out_shape=[jax.ShapeDtypeStruct((1, 1), F32), packed, packed, packed, packed],
    )(sums, w, m, v)


def _pack_small(ng, bg, lbl, hgn, fg):
    return jnp.concatenate([a.reshape(-1, HEAD_DIM) for a in (ng, bg, lbl, hgn, fg)], axis=0)


def _unpack_small(p):
    return (p[0:8].reshape(1, D_MODEL), p[8:24].reshape(1, 2 * D_MODEL), p[24:40].reshape(2, HEADS, HEAD_DIM),
            p[40:48].reshape(1, HEADS, HEAD_DIM), p[48:56].reshape(D_MODEL))


def kernel(x, norm_g, w_in, b_gate, lb_logits, hg_norm_g, w_sb_proj, w_hg_proj, w_out, final_norm_g, loss_target, m_norm_g, m_w_in, m_b_gate, m_lb_logits, m_hg_norm_g, m_w_sb_proj, m_w_hg_proj, m_w_out, m_final_norm_g, v_norm_g, v_w_in, v_b_gate, v_lb_logits, v_hg_norm_g, v_w_sb_proj, v_w_hg_proj, v_w_out, v_final_norm_g):
    s_len = x.shape[1]
    w_sq = jnp.stack([w_sb_proj[0], w_hg_proj[0], w_out[0]])
    idx = jnp.stack([2 * lax.axis_index("x") + lax.axis_index("y"), lax.axis_index("c")]).astype(jnp.int32)
    w_in_b, w_sq_b = w_in[0].astype(BF16), w_sq.astype(BF16)
    w4, wsq = _place_own(idx, w_in_b, w_sq_b, *_gather_weights(w_in_b, w_sq_b))
    wsq = wsq.reshape(3, D_MODEL, D_MODEL)

    (g_in, g_sb, g_hg, g_out, segs, dout, loss, d_bg, d_lb, d_hgn, d_fg) = _local_grads(
        x[0], loss_target[0], norm_g, b_gate, lb_logits.reshape(2, D_MODEL), hg_norm_g.reshape(1, D_MODEL),
        final_norm_g.reshape(1, D_MODEL), w4, wsq[0], wsq[1], wsq[2])

    g_sq = jnp.stack([g_sb, g_hg, g_out]).reshape(3, N_CHIPS, ROW_SHARD, D_MODEL)
    got_in, got_sq = _swap_halves(g_in, g_sq)
    s_in, s_sq = _sum_a_in(idx, g_in, got_in), _sum_a_sq(idx, g_sq, got_sq)
    grad_x, d_ng, got_in, got_sq = _dx(segs, w4, x[0], norm_g, dout, s_in, s_sq)
    grad_in, grad_sq = _join_halves(_sum_b_in(idx, s_in, got_in), _sum_b_sq(idx, s_sq, got_sq))

    d_in, nm_in, nv_in = _adamw(w_in[0], grad_in, m_w_in[0], v_w_in[0], "adamw_in")
    flat = lambda a, b, c: jnp.concatenate([a[0], b[0], c[0]], axis=0)
    d_sq, nm_sq, nv_sq = _adamw(flat(w_sb_proj, w_hg_proj, w_out), grad_sq.reshape(3 * ROW_SHARD, D_MODEL),
                                flat(m_w_sb_proj, m_w_hg_proj, m_w_out), flat(v_w_sb_proj, v_w_hg_proj, v_w_out),
                                "adamw_sq")

    pad = jnp.zeros((8, HEAD_DIM), F32).at[0, 0].set(loss[0, 0])
    part = jnp.concatenate([pad] + [a.reshape(-1, HEAD_DIM) for a in (d_ng, d_bg, d_lb, d_hgn, d_fg)], axis=0)
    sums = _sum_small(part)
    loss_out, g_sm, d_sm, nm_sm, nv_sm = _adamw_small(
        sums, _pack_small(norm_g, b_gate, lb_logits, hg_norm_g, final_norm_g),
        _pack_small(m_norm_g, m_b_gate, m_lb_logits, m_hg_norm_g, m_final_norm_g),
        _pack_small(v_norm_g, v_b_gate, v_lb_logits, v_hg_norm_g, v_final_norm_g))

    def big(t_in, t_sq):
        sq = t_sq.reshape(3, 1, ROW_SHARD, D_MODEL)
        return t_in[None], sq[0], sq[1], sq[2]

    def order(small, in_, sb, hg, out):
        ng, bg, lbl, hgn, fg = small
        return [ng, in_, bg, lbl, hgn, sb, hg, out, fg]

    outs = [loss_out[0, 0], grad_x[None]]
    for small, (t_in, t_sq) in ((g_sm, (grad_in, grad_sq)), (d_sm, (d_in, d_sq)), (nm_sm, (nm_in, nm_sq)), (nv_sm, (nv_in, nv_sq))):
        outs += order(_unpack_small(small), *big(t_in, t_sq))
    return tuple(outs)
```

```python
import functools

import jax
import jax.numpy as jnp
from jax import lax
from jax.experimental import pallas as pl
from jax.experimental.pallas import tpu as pltpu

F32 = jnp.float32
BF16 = jnp.bfloat16

D_MODEL = 1024
HEADS = 8
HEAD_DIM = 128
IN_WIDTH = 10240
N_CHIPS = 4
W_IN_SHARD = IN_WIDTH // N_CHIPS
ROW_SHARD = D_MODEL // N_CHIPS
RMS_EPS = 1e-6

OFF_SB_Q, OFF_SB_K, OFF_SB_V, OFF_SB_Z = 0, 1024, 2048, 3072
OFF_HG_Q, OFF_HG_F, OFF_HG_I, OFF_HG_Z, OFF_GATE = 4096, 5120, 6144, 7168, 8192

QKV_COLS = 3840
SB_BLOCK = 256
SB_FWD_HEADS = 4
SB_BWD_HEADS = 2
SB_ROWS = 256
SB_DEAD = -110.0
SB_GONE = -1e30
HG_CHUNK = 32
HG_PAIR = 2 * HG_CHUNK
HG_STEP = 256
HG_MID = HG_CHUNK // 2 - 1

ADAM_LR, ADAM_B1, ADAM_B2, ADAM_EPS, ADAM_WD, ADAM_STEP = 0.001, 0.9, 0.999, 1e-08, 0.01, 10

VMEM_LIMIT = 56 * 1024 * 1024

MESH = pl.DeviceIdType.MESH


def _cparams(sem, vmem=VMEM_LIMIT):
    return pltpu.CompilerParams(dimension_semantics=sem, vmem_limit_bytes=vmem)


def _dot(a, b):
    return jnp.dot(a, b, preferred_element_type=F32)


def _dot_nt(a, b):
    return lax.dot_general(a, b, (((1,), (1,)), ((), ())), preferred_element_type=F32)


def _dot_tn(a, b):
    return lax.dot_general(a, b, (((0,), (0,)), ((), ())), preferred_element_type=F32)


def _split_dot(x, tri):
    hi = x.astype(BF16)
    lo = (x - hi.astype(F32)).astype(BF16)
    both = _dot(jnp.concatenate([hi, lo], axis=0), tri)
    return both[: x.shape[0]] + both[x.shape[0] :]


def _split_dot_left(tri, x):
    hi = x.astype(BF16)
    lo = (x - hi.astype(F32)).astype(BF16)
    return _dot(tri, hi) + _dot(tri, lo)


def _sigmoid(x):
    return 1.0 / (1.0 + jnp.exp(-x))


def _inproj(x, norm_g, w4):
    s_len = x.shape[0]
    ts = min(1024, s_len)
    tn = QKV_COLS // 3
    per = W_IN_SHARD // tn

    def body(x_ref, g_ref, w_ref, proj_ref, ht_ref, qkv_ref, h_scr):
        n = pl.program_id(1)

        @pl.when(n == 0)
        def _():
            xv = x_ref[...]
            r = lax.rsqrt(jnp.mean(xv * xv, axis=-1, keepdims=True) + RMS_EPS)
            hv = (xv * r) * g_ref[...]
            h_scr[...] = hv.astype(BF16)
            ht_ref[...] = hv.T.astype(BF16)

        p = _dot(h_scr[...], w_ref[0])
        proj_ref[...] = p

        @pl.when(n < 3)
        def _():
            qkv_ref[...] = p.astype(BF16)

    return pl.pallas_call(
        body,
        name="inproj",
        grid=(s_len // ts, IN_WIDTH // tn),
        in_specs=[
            pl.BlockSpec((ts, D_MODEL), lambda s, n: (s, 0)),
            pl.BlockSpec((1, D_MODEL), lambda s, n: (0, 0)),
            pl.BlockSpec((1, D_MODEL, tn), lambda s, n: (n // per, 0, n % per)),
        ],
        out_specs=[
            pl.BlockSpec((ts, tn), lambda s, n: (s, n)),
            pl.BlockSpec((D_MODEL, ts), lambda s, n: (0, s)),
            pl.BlockSpec((ts, tn), lambda s, n: (s, jnp.minimum(n, 2))),
        ],
        out_shape=[
            jax.ShapeDtypeStruct((s_len, IN_WIDTH), F32),
            jax.ShapeDtypeStruct((D_MODEL, s_len), BF16),
            jax.ShapeDtypeStruct((s_len, QKV_COLS), BF16),
        ],
        scratch_shapes=[pltpu.VMEM((ts, D_MODEL), BF16)],
        compiler_params=_cparams(("arbitrary", "arbitrary")),
    )(x, norm_g, w4)


def _sb_tile_fwd(qb, kb, row_gt_col, tri_excl, carry, diag):
    scale = HEAD_DIM ** -0.5
    z = _dot_nt(qb, kb) * scale
    ls_pos = jnp.minimum(z, 0.0) - jnp.log1p(jnp.exp(-jnp.abs(z)))
    log_not = ls_pos - z
    log_not_m = jnp.where(row_gt_col, log_not, 0.0) if diag else log_not
    surv = _split_dot(log_not_m, tri_excl) + carry
    w = jnp.exp(ls_pos + surv)
    if diag:
        w = jnp.where(row_gt_col, w, 0.0)
    return ls_pos, log_not, log_not_m, surv, w


def _sb_specs(s_len, blk, heads):
    width = heads * HEAD_DIM

    def blk_spec(off):
        return pl.BlockSpec((blk, width), lambda h, i: (i, off // width + h))

    def head_spec(off):
        return pl.BlockSpec((s_len, width), lambda h, i: (0, off // width + h))

    return blk_spec, head_spec


def _head_cols(p):
    return slice(p * HEAD_DIM, (p + 1) * HEAD_DIM)


def _sb_chains(blk, heads):
    rows = min(SB_ROWS, blk)
    return [(p, a) for p in range(heads) for a in range(blk // rows)], rows


def _sb_masks(blk, rows):
    row = lax.broadcasted_iota(jnp.int32, (rows, blk), 0)
    col = lax.broadcasted_iota(jnp.int32, (rows, blk), 1)
    causal = [row + a * rows > col for a in range(blk // rows)]
    row = lax.broadcasted_iota(jnp.int32, (blk, blk), 0)
    col = lax.broadcasted_iota(jnp.int32, (blk, blk), 1)
    tri_excl = (row > col).astype(BF16)
    tri_incl = (row >= col).astype(BF16)
    return causal, tri_excl, tri_incl


def _sb_alive(st, n_chain):
    alive = functools.reduce(jnp.maximum, [st[1 + 3 * c] for c in range(n_chain)])
    return jnp.max(alive) > SB_DEAD


def _sb_fwd(qkv):
    s_len = qkv.shape[0]
    blk = min(SB_BLOCK, s_len)
    nq = s_len // blk
    chains, rows = _sb_chains(blk, SB_FWD_HEADS)

    def body(q_ref, k_ref, v_ref, o_ref, of_ref):
        i = pl.program_id(1)
        causal, tri_excl, _ = _sb_masks(blk, rows)

        def tile(j, st, diag, valid=None):
            start = pl.multiple_of(j * blk, blk)
            new = []
            for c, (p, a) in enumerate(chains):
                carry, acc, acc_lo = st[3 * c : 3 * c + 3]
                if valid is not None:
                    carry = jnp.where(valid, carry, SB_GONE)
                kb = k_ref[pl.ds(start, blk), _head_cols(p)]
                vb = v_ref[pl.ds(start, blk), _head_cols(p)]
                qb = q_ref[a * rows : (a + 1) * rows, _head_cols(p)]
                _, _, log_not_m, surv, w = _sb_tile_fwd(qb, kb, causal[a], tri_excl, carry, diag)
                wb = w.astype(BF16)
                w_lo = (w - wb.astype(F32)).astype(BF16)
                both = _dot(jnp.concatenate([wb, w_lo], axis=0), vb)
                new += [surv[:, 0:1] + log_not_m[:, 0:1], acc + both[:rows], acc_lo + both[rows:]]
            return tuple(new)

        zero = jnp.zeros((rows, HEAD_DIM), F32)
        st = tile(i, (jnp.zeros((rows, 1), F32), zero, zero) * len(chains), True)
        st = tile(jnp.maximum(i - 1, 0), st, False, valid=i >= 1)

        def more(st):
            return (st[0] < i) & _sb_alive(st, len(chains))

        def step(st):
            return (st[0] + 1,) + tile(i - 1 - st[0], st[1:], False)

        st = lax.while_loop(more, step, (1,) + st)[1:]
        for c, (p, a) in enumerate(chains):
            o_ref[a * rows : (a + 1) * rows, _head_cols(p)] = st[3 * c + 1]
            of_ref[a * rows : (a + 1) * rows, _head_cols(p)] = st[3 * c + 1] + st[3 * c + 2]

    blk_spec, head_spec = _sb_specs(s_len, blk, SB_FWD_HEADS)
    return pl.pallas_call(
        body,
        name="sb_fwd",
        grid=(HEADS // SB_FWD_HEADS, nq),
        in_specs=[blk_spec(OFF_SB_Q), head_spec(OFF_SB_K), head_spec(OFF_SB_V)],
        out_specs=[blk_spec(0), blk_spec(0)],
        out_shape=[jax.ShapeDtypeStruct((s_len, D_MODEL), F32)] * 2,
        compiler_params=_cparams(("arbitrary", "arbitrary")),
    )(qkv, qkv, qkv)


def _sb_bwd(qkv, o_fine, d_o):
    s_len = qkv.shape[0]
    blk = min(SB_BLOCK, s_len)
    nq = s_len // blk
    scale = HEAD_DIM ** -0.5
    chains, rows = _sb_chains(blk, SB_BWD_HEADS)

    def body(q_ref, k_ref, v_ref, of_ref, do_ref, dq_ref, dk_ref, dv_ref, dk_acc, dv_acc):
        i = pl.program_id(1)

        @pl.when(i == 0)
        def _():
            dk_acc[...] = jnp.zeros_like(dk_acc)
            dv_acc[...] = jnp.zeros_like(dv_acc)

        dob = do_ref[...].astype(BF16)
        prod = dob.astype(F32) * of_ref[...]
        causal, tri_excl, tri_incl = _sb_masks(blk, rows)

        def group(x, p, a):
            return x[a * rows : (a + 1) * rows, _head_cols(p)]

        totals = [jnp.sum(group(prod, p, a), axis=-1, keepdims=True) for p, a in chains]

        def tile(j, st, diag, valid=None):
            start = pl.multiple_of(j * blk, blk)
            new = []
            dk_new = [None] * SB_BWD_HEADS
            dv_new = [None] * SB_BWD_HEADS
            for c, (p, a) in enumerate(chains):
                c_not, c_dlw, dq = st[3 * c : 3 * c + 3]
                if valid is not None:
                    c_not = jnp.where(valid, c_not, SB_GONE)
                qb, dob_c = group(q_ref, p, a), group(dob, p, a)
                kb = k_ref[pl.ds(start, blk), _head_cols(p)]
                vb = v_ref[pl.ds(start, blk), _head_cols(p)]
                ls_pos, log_not, log_not_m, surv, w = _sb_tile_fwd(qb, kb, causal[a], tri_excl, c_not, diag)
                dlw = _dot_nt(dob_c, vb) * w
                suffix = _split_dot(dlw, tri_incl)
                d_not = totals[c] - c_dlw - suffix
                dz = (dlw * jnp.exp(log_not) - d_not * jnp.exp(ls_pos)) * scale
                if diag:
                    dz = jnp.where(causal[a], dz, 0.0)
                if valid is not None:
                    dz = jnp.where(valid, dz, 0.0)
                dzb = dz.astype(BF16)
                dk_c, dv_c = _dot_tn(dzb, qb), _dot_tn(w.astype(BF16), dob_c)
                dk_new[p] = dk_c if dk_new[p] is None else dk_new[p] + dk_c
                dv_new[p] = dv_c if dv_new[p] is None else dv_new[p] + dv_c
                new += [surv[:, 0:1] + log_not_m[:, 0:1], c_dlw + suffix[:, 0:1], dq + _dot(dzb, kb)]
            for p in range(SB_BWD_HEADS):
                dk_acc[pl.ds(start, blk), _head_cols(p)] += dk_new[p]
                dv_acc[pl.ds(start, blk), _head_cols(p)] += dv_new[p]
            return tuple(new)

        zcol = jnp.zeros((rows, 1), F32)
        st = tile(i, (zcol, zcol, jnp.zeros((rows, HEAD_DIM), F32)) * len(chains), True)
        st = tile(jnp.maximum(i - 1, 0), st, False, valid=i >= 1)

        def more(st):
            return (st[0] < i) & _sb_alive(st, len(chains))

        def step(st):
            return (st[0] + 1,) + tile(i - 1 - st[0], st[1:], False)

        st = lax.while_loop(more, step, (1,) + st)[1:]
        for c, (p, a) in enumerate(chains):
            dq_ref[a * rows : (a + 1) * rows, _head_cols(p)] = st[3 * c + 2].astype(BF16)

        @pl.when(i == nq - 1)
        def _():
            dk_ref[...] = dk_acc[...].astype(BF16)
            dv_ref[...] = dv_acc[...].astype(BF16)

    blk_spec, head_spec = _sb_specs(s_len, blk, SB_BWD_HEADS)
    width = SB_BWD_HEADS * HEAD_DIM
    return pl.pallas_call(
        body,
        name="sb_bwd",
        grid=(HEADS // SB_BWD_HEADS, nq),
        in_specs=[blk_spec(OFF_SB_Q), head_spec(OFF_SB_K), head_spec(OFF_SB_V), blk_spec(0), blk_spec(0)],
        out_specs=[blk_spec(0), head_spec(0), head_spec(0)],
        out_shape=[jax.ShapeDtypeStruct((s_len, D_MODEL), BF16)] * 3,
        scratch_shapes=[pltpu.VMEM((s_len, width), F32), pltpu.VMEM((s_len, width), F32)],
        compiler_params=_cparams(("arbitrary", "arbitrary")),
    )(qkv, qkv, qkv, o_fine, d_o)


def _hg_lower_bound(lbl_ref):
    l0 = lbl_ref[0:1, :]
    l1 = lbl_ref[1:2, :]
    mx = jnp.maximum(l0, l1)
    e0 = jnp.exp(l0 - mx)
    e1 = jnp.exp(l1 - mx)
    return e0 / (e0 + e1)


def _hg_gates(hq, hf, lb):
    sig_f = _sigmoid(hf)
    f = lb + (1.0 - lb) * sig_f
    g = jnp.log(f)
    kk = 1.0 - f
    sig_q = _sigmoid(hq)
    qq = hq * sig_q
    return qq, kk, g, f, sig_f, sig_q


def _period_bcast(x, r, rows, period):
    w = x.shape[-1]
    x3 = x.reshape(rows // period, period, w)
    return jnp.broadcast_to(x3[:, r : r + 1, :], x3.shape).reshape(rows, w)


def _blockdiag(rows, kind):
    row = lax.broadcasted_iota(jnp.int32, (rows, rows), 0)
    col = lax.broadcasted_iota(jnp.int32, (rows, rows), 1)
    if kind in ("next", "prev"):
        first, second = (row, col) if kind == "next" else (col, row)
        keep = ((row // HG_PAIR) == (col // HG_PAIR)) & (first % HG_PAIR < HG_CHUNK) & (second % HG_PAIR >= HG_CHUNK)
    else:
        keep = (row // HG_CHUNK) == (col // HG_CHUNK)
        if kind == "lower":
            keep = keep & (row >= col)
        elif kind == "upper":
            keep = keep & (row <= col)
    return jnp.where(keep, 1.0, 0.0).astype(BF16)


def _hg_operands(hq, hf, lb, rows):
    qq, kk, g, f, sig_f, sig_q = _hg_gates(hq, hf, lb)
    cum = _split_dot_left(_blockdiag(rows, "lower"), g)
    mid = _period_bcast(cum, HG_MID, rows, HG_CHUNK)
    last = _period_bcast(cum, HG_CHUNK - 1, rows, HG_CHUNK)
    last0 = _period_bcast(cum, HG_CHUNK - 1, rows, HG_PAIR)
    last1 = _period_bcast(cum, HG_PAIR - 1, rows, HG_PAIR)
    second = (lax.broadcasted_iota(jnp.int32, cum.shape, 0) % HG_PAIR) >= HG_CHUNK
    e = dict(qm=jnp.exp(cum - mid), km=jnp.exp(mid - cum), qd=jnp.exp(cum), kl=jnp.exp(last - cum),
             q_in=jnp.where(second, jnp.exp(last0), 1.0), k_out=jnp.where(second, 1.0, jnp.exp(last1)),
             pair=jnp.exp(last0 + last1))
    v = dict(qm=qq * e["qm"], km=kk * e["km"], qd=qq * e["qd"], kl=kk * e["kl"])
    v["qp"] = v["qd"] * e["q_in"]
    v["kp"] = v["kl"] * e["k_out"]
    return v, e, second, (f, sig_f, sig_q)


def _hg_store_operands(v, second, hi, refs):
    zero = jnp.zeros_like(v["qm"])
    q_cat, k_cat, qp_b, kp_b, v_b = refs
    q_cat[:, 0:D_MODEL] = jnp.where(second, zero, v["qm"]).astype(BF16)
    q_cat[:, D_MODEL : 2 * D_MODEL] = jnp.where(second, v["qm"], zero).astype(BF16)
    q_cat[:, 2 * D_MODEL :] = jnp.where(second, v["qd"], zero).astype(BF16)
    k_cat[:, 0:D_MODEL] = jnp.where(second, zero, v["km"]).astype(BF16)
    k_cat[:, D_MODEL : 2 * D_MODEL] = jnp.where(second, v["km"], zero).astype(BF16)
    k_cat[:, 2 * D_MODEL :] = jnp.where(second, zero, v["kl"]).astype(BF16)
    qp_b[...] = v["qp"].astype(BF16)
    kp_b[...] = v["kp"].astype(BF16)
    v_b[...] = hi.astype(BF16)


def _hg_pair_operands(cat, r0, c0):
    return jnp.concatenate([cat[r0 : r0 + HG_PAIR, g * D_MODEL + c0 : g * D_MODEL + c0 + HEAD_DIM] for g in range(3)], axis=1)


def _hg_fwd(proj, lbl):
    s_len = proj.shape[0]
    rows = min(HG_STEP, s_len)
    n_pairs = rows // HG_PAIR

    def body(hq_ref, hf_ref, hi_ref, lbl_ref, o_ref, st_ref, state, q_cat, k_cat, qp_b, kp_b, v_b):
        @pl.when(pl.program_id(0) == 0)
        def _():
            state[...] = jnp.zeros_like(state)

        v, e, second, _ = _hg_operands(hq_ref[...], hf_ref[...], _hg_lower_bound(lbl_ref), rows)
        _hg_store_operands(v, second, hi_ref[...], (q_cat, k_cat, qp_b, kp_b, v_b))
        e_pair = e["pair"]
        row = lax.broadcasted_iota(jnp.int32, (HG_PAIR, HG_PAIR), 0)
        col = lax.broadcasted_iota(jnp.int32, (HG_PAIR, HG_PAIR), 1)
        causal = row >= col

        for u in range(n_pairs):
            r0 = u * HG_PAIR
            for h in range(HEADS):
                c0 = h * HEAD_DIM
                sl = (slice(r0, r0 + HG_PAIR), slice(c0, c0 + HEAD_DIM))
                st = state[h]
                st_ref[u, h] = st
                a = jnp.where(causal, _dot_nt(_hg_pair_operands(q_cat, r0, c0), _hg_pair_operands(k_cat, r0, c0)), 0.0)
                vb = v_b[sl]
                o_ref[sl] = _dot(a.astype(BF16), vb) + _dot_nt(qp_b[sl], st.astype(BF16))
                state[h] = st * e_pair[r0 : r0 + 1, c0 : c0 + HEAD_DIM] + _dot_tn(vb, kp_b[sl])

    def col_spec(off):
        return pl.BlockSpec((rows, D_MODEL), lambda s: (s, off // D_MODEL))

    bf_tile = pltpu.VMEM((rows, D_MODEL), BF16)
    bf_cat = pltpu.VMEM((rows, 3 * D_MODEL), BF16)
    scratch = [pltpu.VMEM((HEADS, HEAD_DIM, HEAD_DIM), F32), bf_cat, bf_cat, bf_tile, bf_tile, bf_tile]
    return pl.pallas_call(
        body,
        name="hg_fwd",
        grid=(s_len // rows,),
        in_specs=[col_spec(OFF_HG_Q), col_spec(OFF_HG_F), col_spec(OFF_HG_I), pl.BlockSpec((2, D_MODEL), lambda s: (0, 0))],
        out_specs=[
            pl.BlockSpec((rows, D_MODEL), lambda s: (s, 0)),
            pl.BlockSpec((n_pairs, HEADS, HEAD_DIM, HEAD_DIM), lambda s: (s, 0, 0, 0)),
        ],
        out_shape=[
            jax.ShapeDtypeStruct((s_len, D_MODEL), F32),
            jax.ShapeDtypeStruct((s_len // HG_PAIR, HEADS, HEAD_DIM, HEAD_DIM), F32),
        ],
        scratch_shapes=scratch,
        compiler_params=_cparams(("arbitrary",)),
    )(proj, proj, proj, lbl)


def _hg_bwd(proj, lbl, states, d_o):
    s_len = proj.shape[0]
    rows = min(HG_STEP, s_len)
    n_pairs = rows // HG_PAIR
    n_steps = s_len // rows

    def body(hq_ref, hf_ref, hi_ref, lbl_ref, st_ref, do_ref, dp_ref, dlb_ref,
             dstate, q_cat, k_cat, qp_b, kp_b, v_b, do_b, d_qcat, d_kcat, d_qp, d_kp, d_v, d_pair):
        @pl.when(pl.program_id(0) == 0)
        def _():
            dstate[...] = jnp.zeros_like(dstate)
            dlb_ref[...] = jnp.zeros_like(dlb_ref)

        lb = _hg_lower_bound(lbl_ref)
        hq = hq_ref[...]
        v, e, second, (f, sig_f, sig_q) = _hg_operands(hq, hf_ref[...], lb, rows)
        _hg_store_operands(v, second, hi_ref[...], (q_cat, k_cat, qp_b, kp_b, v_b))
        do_b[...] = do_ref[...].astype(BF16)
        e_pair = e["pair"]
        row = lax.broadcasted_iota(jnp.int32, (HG_PAIR, HG_PAIR), 0)
        col = lax.broadcasted_iota(jnp.int32, (HG_PAIR, HG_PAIR), 1)
        causal = row >= col

        for u in reversed(range(n_pairs)):
            r0 = u * HG_PAIR
            for h in range(HEADS):
                c0 = h * HEAD_DIM
                sl = (slice(r0, r0 + HG_PAIR), slice(c0, c0 + HEAD_DIM))
                st0 = st_ref[u, h]
                ds1 = dstate[h]
                ds1b = ds1.astype(BF16)
                dob, vb = do_b[sl], v_b[sl]
                lhs, rhs = _hg_pair_operands(q_cat, r0, c0), _hg_pair_operands(k_cat, r0, c0)
                a = jnp.where(causal, _dot_nt(lhs, rhs), 0.0).astype(BF16)
                da = jnp.where(causal, _dot_nt(dob, vb), 0.0).astype(BF16)
                d_v[sl] = _dot_tn(a, dob) + _dot_nt(kp_b[sl], ds1b)
                d_lhs = _dot(da, rhs)
                d_rhs = _dot_tn(da, lhs)
                for g in range(3):
                    gsl = (sl[0], slice(g * D_MODEL + c0, g * D_MODEL + c0 + HEAD_DIM))
                    d_qcat[gsl] = d_lhs[:, g * HEAD_DIM : (g + 1) * HEAD_DIM]
                    d_kcat[gsl] = d_rhs[:, g * HEAD_DIM : (g + 1) * HEAD_DIM]
                d_qp[sl] = _dot(dob, st0.astype(BF16))
                d_kp[sl] = _dot(vb, ds1b)
                decay = e_pair[r0 : r0 + 1, c0 : c0 + HEAD_DIM]
                d_pair[u : u + 1, c0 : c0 + HEAD_DIM] = decay * jnp.sum(ds1 * st0, axis=0, keepdims=True)
                dstate[h] = ds1 * decay + _dot_tn(dob, qp_b[sl])

        zero = jnp.zeros_like(hq)
        dqm = jnp.where(second, d_qcat[:, D_MODEL : 2 * D_MODEL], d_qcat[:, 0:D_MODEL])
        dkm = jnp.where(second, d_kcat[:, D_MODEL : 2 * D_MODEL], d_kcat[:, 0:D_MODEL])
        dqp, dkp = d_qp[...], d_kp[...]
        dqd = dqp * e["q_in"] + jnp.where(second, d_qcat[:, 2 * D_MODEL :], zero)
        dkl = dkp * e["k_out"] + jnp.where(second, zero, d_kcat[:, 2 * D_MODEL :])
        dq = dqm * e["qm"] + dqd * e["qd"]
        dk = dkm * e["km"] + dkl * e["kl"]
        t_kl = dkl * v["kl"]
        dcum = dqm * v["qm"] - dkm * v["km"] + dqd * v["qd"] - t_kl
        dp = d_pair[...]
        dp_b = jnp.broadcast_to(dp[:, None, :], (n_pairs, HG_PAIR, D_MODEL)).reshape(rows, D_MODEL)
        dg = (_split_dot_left(_blockdiag(rows, "upper"), dcum) + _split_dot_left(_blockdiag(rows, "all"), t_kl)
              + _split_dot_left(_blockdiag(rows, "next"), dqp * v["qp"])
              + _split_dot_left(_blockdiag(rows, "prev"), dkp * v["kp"]) + dp_b)
        df = dg / f - dk
        one_m = 1.0 - sig_f
        dp_ref[:, 0:D_MODEL] = (dq * (sig_q * (1.0 + hq * (1.0 - sig_q)))).astype(BF16)
        dp_ref[:, D_MODEL : 2 * D_MODEL] = (df * (1.0 - lb) * sig_f * one_m).astype(BF16)
        dp_ref[:, 2 * D_MODEL : 3 * D_MODEL] = d_v[...].astype(BF16)
        dlb_ref[...] += jnp.sum(df * one_m, axis=0, keepdims=True)

    def col_spec(off):
        return pl.BlockSpec((rows, D_MODEL), lambda s: (n_steps - 1 - s, off // D_MODEL))

    f32_tile = pltpu.VMEM((rows, D_MODEL), F32)
    f32_cat = pltpu.VMEM((rows, 3 * D_MODEL), F32)
    bf_tile = pltpu.VMEM((rows, D_MODEL), BF16)
    bf_cat = pltpu.VMEM((rows, 3 * D_MODEL), BF16)
    scratch = [pltpu.VMEM((HEADS, HEAD_DIM, HEAD_DIM), F32), bf_cat, bf_cat, bf_tile, bf_tile, bf_tile, bf_tile,
               f32_cat, f32_cat, f32_tile, f32_tile, f32_tile, pltpu.VMEM((n_pairs, D_MODEL), F32)]
    return pl.pallas_call(
        body,
        name="hg_bwd",
        grid=(n_steps,),
        in_specs=[
            col_spec(OFF_HG_Q), col_spec(OFF_HG_F), col_spec(OFF_HG_I),
            pl.BlockSpec((2, D_MODEL), lambda s: (0, 0)),
            pl.BlockSpec((n_pairs, HEADS, HEAD_DIM, HEAD_DIM), lambda s: (n_steps - 1 - s, 0, 0, 0)),
            pl.BlockSpec((rows, D_MODEL), lambda s: (n_steps - 1 - s, 0)),
        ],
        out_specs=[
            pl.BlockSpec((rows, 3 * D_MODEL), lambda s: (n_steps - 1 - s, 0)),
            pl.BlockSpec((1, D_MODEL), lambda s: (0, 0)),
        ],
        out_shape=[
            jax.ShapeDtypeStruct((s_len, 3 * D_MODEL), BF16),
            jax.ShapeDtypeStruct((1, D_MODEL), F32),
        ],
        scratch_shapes=scratch,
        compiler_params=_cparams(("arbitrary",)),
    )(proj, proj, proj, lbl, states, d_o)


def _mid(proj, sb_o, hg_o, x, target, b_gate, hg_gain, final_g, w_sb, w_hg, w_out):
    s_len = proj.shape[0]
    ts = min(256, s_len)
    inv_d = 1.0 / D_MODEL

    def body(zsb_ref, hz_ref, gl_ref, sbo_ref, hgo_ref, x_ref, tgt_ref, bg_ref, hgn_ref, fg_ref,
             wsb_ref, whg_ref, wout_ref,
             dout_ref, dsbo_ref, dhgo_ref, dzsb_ref, dhz_ref, dgl_ref,
             asb_ref, dusb_ref, ahg_ref, duhg_ref, y_ref, doutb_ref,
             loss_ref, dfg_ref, dbg_ref, dhgn_ref):
        @pl.when(pl.program_id(0) == 0)
        def _():
            loss_ref[...] = jnp.zeros_like(loss_ref)
            dfg_ref[...] = jnp.zeros_like(dfg_ref)
            dbg_ref[...] = jnp.zeros_like(dbg_ref)
            dhgn_ref[...] = jnp.zeros_like(dhgn_ref)

        z_sb = zsb_ref[...]
        sb_o = sbo_ref[...]
        sig_zsb = _sigmoid(z_sb)
        silu_zsb = z_sb * sig_zsb
        a_sb = (sb_o * silu_zsb).astype(BF16)
        u_sb = _dot(a_sb, wsb_ref[...])

        hg_o = hgo_ref[...]
        gain = hgn_ref[...]
        r_parts, yn_parts = [], []
        for h in range(HEADS):
            oh = hg_o[:, h * HEAD_DIM : (h + 1) * HEAD_DIM]
            r = lax.rsqrt(jnp.mean(oh * oh, axis=-1, keepdims=True) + RMS_EPS)
            r_parts.append(jnp.broadcast_to(r, oh.shape))
            yn_parts.append(oh * r)
        r_hg = jnp.concatenate(r_parts, axis=-1)
        yn_hg = jnp.concatenate(yn_parts, axis=-1)
        hn = yn_hg * gain
        hz = hz_ref[...]
        sig_hz = _sigmoid(hz)
        silu_hz = hz * sig_hz
        a_hg = (hn * silu_hz).astype(BF16)
        u_hg = _dot(a_hg, whg_ref[...])

        gates = _sigmoid(gl_ref[...] + bg_ref[...])
        g_sb = gates[:, 0:D_MODEL]
        g_hg = gates[:, D_MODEL:]
        y = (g_sb * u_sb + g_hg * u_hg).astype(BF16)
        out = x_ref[...] + _dot(y, wout_ref[...])
        r2 = lax.rsqrt(jnp.mean(out * out, axis=-1, keepdims=True) + RMS_EPS)
        yn = out * r2
        fg = fg_ref[...]
        diff = yn * fg - tgt_ref[...]
        loss_ref[...] += 0.5 * inv_d * jnp.sum(diff * diff)

        dyf = diff * inv_d
        dfg_ref[...] += jnp.sum(dyf * yn, axis=0, keepdims=True)
        dyn = dyf * fg
        dout = r2 * (dyn - yn * jnp.mean(dyn * yn, axis=-1, keepdims=True))
        dout_ref[...] = dout
        doutb = dout.astype(BF16)
        doutb_ref[...] = doutb
        dy = _dot_nt(doutb, wout_ref[...])
        du_sb = (dy * g_sb).astype(BF16)
        du_hg = (dy * g_hg).astype(BF16)
        dgl_sb = dy * u_sb * g_sb * (1.0 - g_sb)
        dgl_hg = dy * u_hg * g_hg * (1.0 - g_hg)
        dgl_ref[:, 0:D_MODEL] = dgl_sb.astype(BF16)
        dgl_ref[:, D_MODEL:] = dgl_hg.astype(BF16)
        dbg_ref[:, 0:D_MODEL] += jnp.sum(dgl_sb, axis=0, keepdims=True)
        dbg_ref[:, D_MODEL:] += jnp.sum(dgl_hg, axis=0, keepdims=True)

        da_sb = _dot_nt(du_sb, wsb_ref[...])
        dsbo_ref[...] = da_sb * silu_zsb
        dzsb_ref[...] = (da_sb * sb_o * (sig_zsb * (1.0 + z_sb * (1.0 - sig_zsb)))).astype(BF16)

        da_hg = _dot_nt(du_hg, whg_ref[...])
        dhn = da_hg * silu_hz
        dhz_ref[...] = (da_hg * hn * (sig_hz * (1.0 + hz * (1.0 - sig_hz)))).astype(BF16)
        dhgn_ref[...] += jnp.sum(dhn * yn_hg, axis=0, keepdims=True)
        dyn_hg = dhn * gain
        prod = dyn_hg * yn_hg
        m_parts = []
        for h in range(HEADS):
            ph = prod[:, h * HEAD_DIM : (h + 1) * HEAD_DIM]
            m_parts.append(jnp.broadcast_to(jnp.mean(ph, axis=-1, keepdims=True), ph.shape))
        dhgo_ref[...] = r_hg * (dyn_hg - yn_hg * jnp.concatenate(m_parts, axis=-1))

        asb_ref[...] = a_sb
        dusb_ref[...] = du_sb
        ahg_ref[...] = a_hg
        duhg_ref[...] = du_hg
        y_ref[...] = y

    def tile(width, off=0):
        return pl.BlockSpec((ts, width), lambda s: (s, off // width))

    def whole(shape):
        return pl.BlockSpec(shape, lambda s: (0,) * len(shape))

    def weight():
        return pl.BlockSpec((D_MODEL, D_MODEL), lambda s: (0, 0), pipeline_mode=pl.Buffered(1))

    f32_act = jax.ShapeDtypeStruct((s_len, D_MODEL), F32)
    bf_act = jax.ShapeDtypeStruct((s_len, D_MODEL), BF16)
    return pl.pallas_call(
        body,
        name="mid",
        grid=(s_len // ts,),
        in_specs=[
            tile(D_MODEL, OFF_SB_Z), tile(D_MODEL, OFF_HG_Z), tile(2 * D_MODEL, OFF_GATE),
            tile(D_MODEL), tile(D_MODEL), tile(D_MODEL), tile(D_MODEL),
            whole((1, 2 * D_MODEL)), whole((1, D_MODEL)), whole((1, D_MODEL)),
            weight(), weight(), weight(),
        ],
        out_specs=[
            tile(D_MODEL), tile(D_MODEL), tile(D_MODEL), tile(D_MODEL), tile(D_MODEL), tile(2 * D_MODEL),
            tile(D_MODEL), tile(D_MODEL), tile(D_MODEL), tile(D_MODEL), tile(D_MODEL), tile(D_MODEL),
            whole((1, 1)), whole((1, D_MODEL)), whole((1, 2 * D_MODEL)), whole((1, D_MODEL)),
        ],
        out_shape=[
            f32_act, f32_act, f32_act, bf_act, bf_act, jax.ShapeDtypeStruct((s_len, 2 * D_MODEL), BF16),
            bf_act, bf_act, bf_act, bf_act, bf_act, bf_act,
            jax.ShapeDtypeStruct((1, 1), F32), jax.ShapeDtypeStruct((1, D_MODEL), F32),
            jax.ShapeDtypeStruct((1, 2 * D_MODEL), F32), jax.ShapeDtypeStruct((1, D_MODEL), F32),
        ],
        compiler_params=_cparams(("arbitrary",)),
    )(proj, proj, proj, sb_o, hg_o, x, target, b_gate, hg_gain, final_g, w_sb, w_hg, w_out)


def _grad_matmul(a, b, name, tn):
    s_len, m = a.shape
    n = b.shape[1]
    tk = min(512, s_len)

    def body(a_ref, b_ref, o_ref):
        @pl.when(pl.program_id(1) == 0)
        def _():
            o_ref[...] = jnp.zeros_like(o_ref)

        o_ref[...] += _dot_tn(a_ref[...], b_ref[...])

    return pl.pallas_call(
        body,
        name=name,
        grid=(n // tn, s_len // tk),
        in_specs=[pl.BlockSpec((tk, m), lambda j, k: (k, 0)), pl.BlockSpec((tk, tn), lambda j, k: (k, j))],
        out_specs=pl.BlockSpec((m, tn), lambda j, k: (0, j)),
        out_shape=jax.ShapeDtypeStruct((m, n), F32),
        compiler_params=_cparams(("arbitrary", "arbitrary")),
    )(a, b)


SEG_WIDTHS = (1024, 1024, 1024, 1024, 3072, 1024, 2048)


def _seg_bounds(tile):
    bounds = [0]
    for w in SEG_WIDTHS:
        bounds.append(bounds[-1] + w // tile)
    return bounds


def _grad_w_in(h_t, segs):
    m, s_len = h_t.shape
    tk = min(1024, s_len)
    tn = 1024
    nk = s_len // tk
    bounds = _seg_bounds(tn)

    def body(a_ref, *refs):
        seg_refs, o_ref = refs[:-1], refs[-1]
        j = pl.program_id(0)

        @pl.when(pl.program_id(1) == 0)
        def _():
            o_ref[...] = jnp.zeros_like(o_ref)

        for i, ref in enumerate(seg_refs):
            @pl.when((j >= bounds[i]) & (j < bounds[i + 1]))
            def _(ref=ref):
                o_ref[...] += _dot(a_ref[...], ref[...])

    def seg_spec(lo, hi):
        def index(j, k):
            return (jnp.where(j < lo, 0, jnp.where(j >= hi, nk - 1, k)), jnp.clip(j - lo, 0, hi - lo - 1))
        return pl.BlockSpec((tk, tn), index)

    return pl.pallas_call(
        body,
        name="grad_w_in",
        grid=(IN_WIDTH // tn, nk),
        in_specs=[pl.BlockSpec((m, tk), lambda j, k: (0, k))] + [seg_spec(bounds[i], bounds[i + 1]) for i in range(7)],
        out_specs=pl.BlockSpec((m, tn), lambda j, k: (0, j)),
        out_shape=jax.ShapeDtypeStruct((m, IN_WIDTH), F32),
        compiler_params=_cparams(("arbitrary", "arbitrary")),
    )(h_t, *segs)


EXCHANGE_IN_PIECES = 8
EXCHANGE_PIECES = EXCHANGE_IN_PIECES + 3


def _exchange_copies(sin_ref, ssq_ref, got_in, got_sq, send_sems, recv_sems):
    _, _, c, chips = _position()
    rows = HALF_IN // EXCHANGE_IN_PIECES
    copies = []
    for k, (px, py) in enumerate(chips):
        chip = 2 * px + py
        for p in range(EXCHANGE_PIECES):
            if p < EXCHANGE_IN_PIECES:
                src, dst = sin_ref.at[chip, pl.ds(p * rows, rows), :], got_in.at[k, pl.ds(p * rows, rows), :]
            else:
                src, dst = ssq_ref.at[p - EXCHANGE_IN_PIECES, chip], got_sq.at[k, p - EXCHANGE_IN_PIECES]
            copies.append(_remote(src, dst, send_sems.at[k, p], recv_sems.at[k, p], (px, py, c)))
    return copies


def _dx(segs, w4, x, norm_g, dout, s_in, s_sq):
    s_len = x.shape[0]
    ts = min(1024, s_len)
    tk = 512
    per = W_IN_SHARD // tk
    nk = IN_WIDTH // tk
    ns = s_len // ts
    bounds = _seg_bounds(tk)

    def body(*refs):
        seg_refs = refs[:7]
        w_ref, x_ref, g_ref, dout_ref, sin_ref, ssq_ref, gx_ref, dg_ref, got_in, got_sq, acc, send_sems, recv_sems = refs[7:]
        s, k = pl.program_id(0), pl.program_id(1)

        @pl.when((s == 0) & (k == 0))
        def _():
            dg_ref[...] = jnp.zeros_like(dg_ref)
            for cp in _exchange_copies(sin_ref, ssq_ref, got_in, got_sq, send_sems, recv_sems):
                cp.start()

        @pl.when(k == 0)
        def _():
            acc[...] = jnp.zeros_like(acc)

        for i, ref in enumerate(seg_refs):
            @pl.when((k >= bounds[i]) & (k < bounds[i + 1]))
            def _(ref=ref):
                acc[...] += _dot_nt(ref[...], w_ref[0])

        @pl.when(k == nk - 1)
        def _():
            dh = acc[...]
            xv = x_ref[...]
            r = lax.rsqrt(jnp.mean(xv * xv, axis=-1, keepdims=True) + RMS_EPS)
            xn = xv * r
            dg_ref[...] += jnp.sum(dh * xn, axis=0, keepdims=True)
            dxn = dh * g_ref[...]
            gx_ref[...] = r * (dxn - xn * jnp.mean(dxn * xn, axis=-1, keepdims=True)) + dout_ref[...]

        @pl.when((s == ns - 1) & (k == nk - 1))
        def _():
            for cp in _exchange_copies(sin_ref, ssq_ref, got_in, got_sq, send_sems, recv_sems):
                cp.wait()

    def seg_spec(lo, hi):
        return pl.BlockSpec((ts, tk), lambda s, k: (s, jnp.clip(k - lo, 0, hi - lo - 1)))

    row_tile = pl.BlockSpec((ts, D_MODEL), lambda s, k: (s, 0))
    vec = pl.BlockSpec((1, D_MODEL), lambda s, k: (0, 0))
    return pl.pallas_call(
        body,
        name="dx",
        grid=(ns, nk),
        in_specs=[seg_spec(bounds[i], bounds[i + 1]) for i in range(7)] + [
            pl.BlockSpec((1, D_MODEL, tk), lambda s, k: (k // per, 0, k % per)),
            row_tile, vec, row_tile, ANY, ANY,
        ],
        out_specs=[row_tile, vec, ANY, ANY],
        out_shape=[jax.ShapeDtypeStruct((s_len, D_MODEL), F32), jax.ShapeDtypeStruct((1, D_MODEL), F32),
                   jax.ShapeDtypeStruct((3, HALF_IN, W_IN_SHARD), WIRE),
                   jax.ShapeDtypeStruct((3, 3, HALF_SQ, D_MODEL), WIRE)],
        scratch_shapes=[pltpu.VMEM((ts, D_MODEL), F32),
                        pltpu.SemaphoreType.DMA((3, EXCHANGE_PIECES)), pltpu.SemaphoreType.DMA((3, EXCHANGE_PIECES))],
        compiler_params=_cparams(("arbitrary", "arbitrary")),
    )(*segs, w4, x, norm_g, dout, s_in, s_sq)


def _local_grads(x, target, norm_g, b_gate, lbl, hg_gain, final_g, w4, w_sb, w_hg, w_out):
    proj, h_t, qkv = _inproj(x, norm_g, w4)
    sb_o, sb_o_fine = _sb_fwd(qkv)
    hg_o, states = _hg_fwd(proj, lbl)
    (dout, d_sbo, d_hgo, d_zsb, d_hz, d_gl, a_sb, du_sb, a_hg, du_hg, y, doutb,
     loss, d_fg, d_bg, d_hgn) = _mid(proj, sb_o, hg_o, x, target, b_gate, hg_gain, final_g, w_sb, w_hg, w_out)
    g_w_sb = _grad_matmul(a_sb, du_sb, "grad_w_sb", 512)
    g_w_hg = _grad_matmul(a_hg, du_hg, "grad_w_hg", 512)
    g_w_out = _grad_matmul(y, doutb, "grad_w_out", 512)
    d_q, d_k, d_v = _sb_bwd(qkv, sb_o_fine, d_sbo)
    d_hg, d_lb = _hg_bwd(proj, lbl, states, d_hgo)
    segs = (d_q, d_k, d_v, d_zsb, d_hg, d_hz, d_gl)
    g_w_in = _grad_w_in(h_t, segs)
    return g_w_in, g_w_sb, g_w_hg, g_w_out, segs, dout, loss, d_bg, d_lb, d_hgn, d_fg


ANY = pl.BlockSpec(memory_space=pl.ANY)
WIRE = BF16
HALF_IN = D_MODEL // 2
HALF_SQ = ROW_SHARD // 2


def _position():
    x, y, c = lax.axis_index("x"), lax.axis_index("y"), lax.axis_index("c")
    chips = [(1 - x, y), (x, 1 - y), (1 - x, 1 - y)]
    return x, y, c, chips


def _remote(src, dst, send_sem, recv_sem, to):
    return pltpu.make_async_remote_copy(src_ref=src, dst_ref=dst, send_sem=send_sem, recv_sem=recv_sem,
                                        device_id=to, device_id_type=MESH)


def _gather_weights(w_in_b, w_sq_b):
    n_in = 4
    n_piece = n_in + 3
    rows = HALF_IN // n_in

    def body(win_ref, wsq_ref, in_ref, sq_ref, send_sems, recv_sems):
        x, y, c, chips = _position()
        me = 2 * x + y
        sibling = (x, y, 1 - c)

        def src_piece(p):
            if p < n_in:
                return win_ref.at[pl.ds(c * HALF_IN + p * rows, rows), :]
            return wsq_ref.at[p - n_in, pl.ds(c * HALF_SQ, HALF_SQ), :]

        def piece(p, chip, core):
            if p < n_in:
                return in_ref.at[chip, pl.ds(core * HALF_IN + p * rows, rows), :]
            return sq_ref.at[p - n_in, chip, pl.ds(core * HALF_SQ, HALF_SQ), :]

        sends = []
        for k, (px, py) in enumerate(chips):
            for p in range(n_piece):
                sends.append(_remote(src_piece(p), piece(p, me, c), send_sems.at[k, p], recv_sems.at[k, p], (px, py, c)))
        for cp in sends:
            cp.start()
        for k, (px, py) in enumerate(chips):
            chip = 2 * px + py
            for p in range(n_piece):
                got = piece(p, chip, c)
                _remote(got, got, send_sems.at[k, p], recv_sems.at[k, p], (px, py, c)).wait_recv()
                fwd = _remote(got, got, send_sems.at[3 + k, p], recv_sems.at[3 + k, p], sibling)
                fwd.start()
                sends.append(fwd)
        for k, (px, py) in enumerate(chips):
            chip = 2 * px + py
            for p in range(n_piece):
                got = piece(p, chip, 1 - c)
                _remote(got, got, send_sems.at[3 + k, p], recv_sems.at[3 + k, p], sibling).wait_recv()
        for cp in sends:
            cp.wait_send()

    return pl.pallas_call(
        body,
        name="gather_weights",
        in_specs=[ANY, ANY],
        out_specs=[ANY, ANY],
        out_shape=[jax.ShapeDtypeStruct((N_CHIPS, D_MODEL, W_IN_SHARD), BF16),
                   jax.ShapeDtypeStruct((3, N_CHIPS, ROW_SHARD, D_MODEL), BF16)],
        scratch_shapes=[pltpu.SemaphoreType.DMA((6, n_piece)), pltpu.SemaphoreType.DMA((6, n_piece))],
    )(w_in_b, w_sq_b)


def _place_own(idx, w_in_b, w_sq_b, w4, wsq):
    n = 4
    r_in, r_sq = D_MODEL // n, ROW_SHARD // n

    def body(idx_ref, win_ref, wsq_ref, w4_in, wsq_in, w4_out, wsq_out):
        w4_out[0] = win_ref[...]
        wsq_out[:, 0] = wsq_ref[...]

    grid_spec = pltpu.PrefetchScalarGridSpec(
        num_scalar_prefetch=1,
        grid=(n,),
        in_specs=[pl.BlockSpec((r_in, W_IN_SHARD), lambda r, idx: (r, 0)),
                  pl.BlockSpec((3, r_sq, D_MODEL), lambda r, idx: (0, r, 0)), ANY, ANY],
        out_specs=[pl.BlockSpec((1, r_in, W_IN_SHARD), lambda r, idx: (idx[0], r, 0)),
                   pl.BlockSpec((3, 1, r_sq, D_MODEL), lambda r, idx: (0, idx[0], r, 0))],
    )
    return pl.pallas_call(
        body,
        name="place_own",
        grid_spec=grid_spec,
        out_shape=[jax.ShapeDtypeStruct(w4.shape, BF16), jax.ShapeDtypeStruct(wsq.shape, BF16)],
        input_output_aliases={3: 0, 4: 1},
        compiler_params=_cparams(("arbitrary",)),
    )(idx, w_in_b, w_sq_b, w4, wsq)


def _swap_halves(g_in, g_sq):
    n_in = 16
    n_piece = n_in + 3 * N_CHIPS
    rows = HALF_IN // n_in

    def body(gin_ref, gsq_ref, got_in, got_sq, send_sems, recv_sems):
        x, y, c, _ = _position()
        sibling = (x, y, 1 - c)

        def src_piece(p):
            if p < n_in:
                return gin_ref.at[pl.ds((1 - c) * HALF_IN + p * rows, rows), :]
            a, chip = divmod(p - n_in, N_CHIPS)
            return gsq_ref.at[a, chip, pl.ds((1 - c) * HALF_SQ, HALF_SQ), :]

        def dst_piece(p):
            if p < n_in:
                return got_in.at[pl.ds(p * rows, rows), :]
            a, chip = divmod(p - n_in, N_CHIPS)
            return got_sq.at[a, chip]

        out = [_remote(src_piece(p), dst_piece(p), send_sems.at[p], recv_sems.at[p], sibling) for p in range(n_piece)]
        for cp in out:
            cp.start()
        for cp in out:
            cp.wait()

    return pl.pallas_call(
        body,
        name="swap_halves",
        in_specs=[ANY, ANY],
        out_specs=[ANY, ANY],
        out_shape=[jax.ShapeDtypeStruct((HALF_IN, IN_WIDTH), F32),
                   jax.ShapeDtypeStruct((3, N_CHIPS, HALF_SQ, D_MODEL), F32)],
        scratch_shapes=[pltpu.SemaphoreType.DMA((n_piece,))] * 2,
    )(g_in, g_sq)


def _join_halves(r_in, r_sq):
    n_in = 16
    n_piece = n_in + 3
    rows = HALF_IN // n_in

    def body(in_alias, sq_alias, full_in, full_sq, send_sems, recv_sems):
        del in_alias, sq_alias
        x, y, c, _ = _position()
        sibling = (x, y, 1 - c)

        def piece(p, core):
            if p < n_in:
                return full_in.at[pl.ds(core * HALF_IN + p * rows, rows), :]
            return full_sq.at[p - n_in, pl.ds(core * HALF_SQ, HALF_SQ), :]

        out = [_remote(piece(p, c), piece(p, c), send_sems.at[p], recv_sems.at[p], sibling) for p in range(n_piece)]
        for cp in out:
            cp.start()
        for p in range(n_piece):
            _remote(piece(p, 1 - c), piece(p, 1 - c), send_sems.at[p], recv_sems.at[p], sibling).wait_recv()
        for cp in out:
            cp.wait_send()

    return pl.pallas_call(
        body,
        name="join_halves",
        in_specs=[ANY, ANY],
        out_specs=[ANY, ANY],
        out_shape=[jax.ShapeDtypeStruct((D_MODEL, W_IN_SHARD), F32),
                   jax.ShapeDtypeStruct((3, ROW_SHARD, D_MODEL), F32)],
        input_output_aliases={0: 0, 1: 1},
        scratch_shapes=[pltpu.SemaphoreType.DMA((n_piece,)), pltpu.SemaphoreType.DMA((n_piece,))],
    )(r_in, r_sq)


SMALL_ROWS = 56
N_DEV = 8


def _sum_small(part):
    def body(part_ref, out_ref, slots, send_sems, recv_sems):
        x, y, c, _ = _position()
        me = 4 * x + 2 * y + c
        slots[me] = part_ref[...]
        out = []
        for r in range(1, N_DEV):
            rx, ry, rc = (r >> 2) & 1, (r >> 1) & 1, r & 1
            to = (1 - x if rx else x, 1 - y if ry else y, 1 - c if rc else c)
            out.append(_remote(part_ref, slots.at[me], send_sems.at[r - 1], recv_sems.at[r - 1], to))
        for cp in out:
            cp.start()
        for r in range(1, N_DEV):
            _remote(part_ref, slots.at[me ^ r], send_sems.at[r - 1], recv_sems.at[r - 1], (x, y, c)).wait_recv()
        for cp in out:
            cp.wait_send()
        total = slots[0]
        for d in range(1, N_DEV):
            total = total + slots[d]
        out_ref[...] = total

    vmem = pl.BlockSpec(memory_space=pltpu.VMEM)
    return pl.pallas_call(
        body,
        name="sum_small",
        in_specs=[vmem],
        out_specs=vmem,
        out_shape=jax.ShapeDtypeStruct((SMALL_ROWS, HEAD_DIM), F32),
        scratch_shapes=[pltpu.VMEM((N_DEV, SMALL_ROWS, HEAD_DIM), F32),
                        pltpu.SemaphoreType.DMA((N_DEV - 1,)), pltpu.SemaphoreType.DMA((N_DEV - 1,))],
    )(part)


def _prefetch_call(body, name, idx, grid, in_specs, out_specs, out_shape, args):
    grid_spec = pltpu.PrefetchScalarGridSpec(num_scalar_prefetch=1, grid=grid, in_specs=in_specs, out_specs=out_specs)
    return pl.pallas_call(body, name=name, grid_spec=grid_spec, out_shape=out_shape,
                          compiler_params=_cparams(("arbitrary",) * len(grid)))(idx, *args)


def _sum_a_in(idx, g_in, got_in):
    tr = 128
    nr = HALF_IN // tr

    def body(idx_ref, a_ref, b_ref, o_ref):
        o_ref[0] = (a_ref[...] + b_ref[...]).astype(WIRE)

    return _prefetch_call(
        body, "sum_a_in", idx, (N_CHIPS, nr),
        [pl.BlockSpec((tr, W_IN_SHARD), lambda j, r, idx: (idx[1] * nr + r, j)),
         pl.BlockSpec((tr, W_IN_SHARD), lambda j, r, idx: (r, j))],
        pl.BlockSpec((1, tr, W_IN_SHARD), lambda j, r, idx: (j, r, 0)),
        jax.ShapeDtypeStruct((N_CHIPS, HALF_IN, W_IN_SHARD), WIRE), (g_in, got_in))


def _sum_a_sq(idx, g_sq, got_sq):
    blk = (1, 1, HALF_SQ, D_MODEL)

    def body(idx_ref, a_ref, b_ref, o_ref):
        o_ref[...] = (a_ref[...] + b_ref[...]).astype(WIRE)

    return _prefetch_call(
        body, "sum_a_sq", idx, (3, N_CHIPS),
        [pl.BlockSpec(blk, lambda a, j, idx: (a, j, idx[1], 0)), pl.BlockSpec(blk, lambda a, j, idx: (a, j, 0, 0))],
        pl.BlockSpec(blk, lambda a, j, idx: (a, j, 0, 0)),
        jax.ShapeDtypeStruct((3, N_CHIPS, HALF_SQ, D_MODEL), WIRE), (g_sq, got_sq))


def _sum_b_in(idx, s_in, got_in):
    tr = 128
    nr = HALF_IN // tr

    def body(idx_ref, a_ref, b_ref, o_ref):
        o_ref[...] = ((a_ref[0].astype(F32) + b_ref[0].astype(F32)) + b_ref[1].astype(F32)) + b_ref[2].astype(F32)

    return _prefetch_call(
        body, "sum_b_in", idx, (nr,),
        [pl.BlockSpec((1, tr, W_IN_SHARD), lambda r, idx: (idx[0], r, 0)),
         pl.BlockSpec((3, tr, W_IN_SHARD), lambda r, idx: (0, r, 0))],
        pl.BlockSpec((tr, W_IN_SHARD), lambda r, idx: (idx[1] * nr + r, 0)),
        jax.ShapeDtypeStruct((D_MODEL, W_IN_SHARD), F32), (s_in, got_in))


def _sum_b_sq(idx, s_sq, got_sq):
    def body(idx_ref, a_ref, b_ref, o_ref):
        o_ref[0] = ((a_ref[0, 0].astype(F32) + b_ref[0, 0].astype(F32)) + b_ref[1, 0].astype(F32)) + b_ref[2, 0].astype(F32)

    return _prefetch_call(
        body, "sum_b_sq", idx, (3,),
        [pl.BlockSpec((1, 1, HALF_SQ, D_MODEL), lambda a, idx: (a, idx[0], 0, 0)),
         pl.BlockSpec((3, 1, HALF_SQ, D_MODEL), lambda a, idx: (0, a, 0, 0))],
        pl.BlockSpec((1, HALF_SQ, D_MODEL), lambda a, idx: (a, idx[1], 0)),
        jax.ShapeDtypeStruct((3, ROW_SHARD, D_MODEL), F32), (s_sq, got_sq))


def _adamw_math(w, g, m, v):
    m = ADAM_B1 * m + (1.0 - ADAM_B1) * g
    v = ADAM_B2 * v + (1.0 - ADAM_B2) * (g * g)
    m_hat = m / (1.0 - ADAM_B1 ** ADAM_STEP)
    v_hat = v / (1.0 - ADAM_B2 ** ADAM_STEP)
    delta = -ADAM_LR * (m_hat / (jnp.sqrt(v_hat) + ADAM_EPS) + ADAM_WD * w)
    return delta, m, v


def _adamw(w, g, m, v, name):
    rows, cols = w.shape
    tr = min(128, rows)

    def body(w_ref, g_ref, m_ref, v_ref, d_ref, nm_ref, nv_ref):
        d_ref[...], nm_ref[...], nv_ref[...] = _adamw_math(w_ref[...], g_ref[...], m_ref[...], v_ref[...])

    spec = pl.BlockSpec((tr, cols), lambda r: (r, 0))
    return pl.pallas_call(
        body,
        name=name,
        grid=(rows // tr,),
        in_specs=[spec] * 4,
        out_specs=[spec] * 3,
        out_shape=[jax.ShapeDtypeStruct((rows, cols), F32)] * 3,
        compiler_params=_cparams(("arbitrary",)),
    )(w, g, m, v)


def _adamw_small(sums, w, m, v):
    def body(s_ref, w_ref, m_ref, v_ref, loss_ref, g_ref, d_ref, nm_ref, nv_ref):
        s = s_ref[...]
        w = w_ref[...]
        loss_ref[...] = s[0:1, 0:1]
        l0, l1 = w[24:32], w[32:40]
        mx = jnp.maximum(l0, l1)
        e0, e1 = jnp.exp(l0 - mx), jnp.exp(l1 - mx)
        p0, p1 = e0 / (e0 + e1), e1 / (e0 + e1)
        d_lb = s[32:40]
        g = jnp.concatenate([s[8:16], s[16:32], d_lb * p0 * (1.0 - p0), -d_lb * p0 * p1, s[40:48], s[48:56]], axis=0)
        g_ref[...] = g
        d_ref[...], nm_ref[...], nv_ref[...] = _adamw_math(w, g, m_ref[...], v_ref[...])

    packed = jax.ShapeDtypeStruct((SMALL_ROWS, HEAD_DIM), F32)
    return pl.pallas_call(
        body,
        name="adamw_small",
        out_shape=[jax.ShapeDtypeStruct((1, 1), F32), packed, packed, packed, packed],
    )(sums, w, m, v)


def _pack_small(ng, bg, lbl, hgn, fg):
    return jnp.concatenate([a.reshape(-1, HEAD_DIM) for a in (ng, bg, lbl, hgn, fg)], axis=0)


def _unpack_small(p):
    return (p[0:8].reshape(1, D_MODEL), p[8:24].reshape(1, 2 * D_MODEL), p[24:40].reshape(2, HEADS, HEAD_DIM),
            p[40:48].reshape(1, HEADS, HEAD_DIM), p[48:56].reshape(D_MODEL))


def kernel(x, norm_g, w_in, b_gate, lb_logits, hg_norm_g, w_sb_proj, w_hg_proj, w_out, final_norm_g, loss_target, m_norm_g, m_w_in, m_b_gate, m_lb_logits, m_hg_norm_g, m_w_sb_proj, m_w_hg_proj, m_w_out, m_final_norm_g, v_norm_g, v_w_in, v_b_gate, v_lb_logits, v_hg_norm_g, v_w_sb_proj, v_w_hg_proj, v_w_out, v_final_norm_g):
    s_len = x.shape[1]
    w_sq = jnp.stack([w_sb_proj[0], w_hg_proj[0], w_out[0]])
    idx = jnp.stack([2 * lax.axis_index("x") + lax.axis_index("y"), lax.axis_index("c")]).astype(jnp.int32)
    w_in_b, w_sq_b = w_in[0].astype(BF16), w_sq.astype(BF16)
    w4, wsq = _place_own(idx, w_in_b, w_sq_b, *_gather_weights(w_in_b, w_sq_b))
    wsq = wsq.reshape(3, D_MODEL, D_MODEL)

    (g_in, g_sb, g_hg, g_out, segs, dout, loss, d_bg, d_lb, d_hgn, d_fg) = _local_grads(
        x[0], loss_target[0], norm_g, b_gate, lb_logits.reshape(2, D_MODEL), hg_norm_g.reshape(1, D_MODEL),
        final_norm_g.reshape(1, D_MODEL), w4, wsq[0], wsq[1], wsq[2])

    g_sq = jnp.stack([g_sb, g_hg, g_out]).reshape(3, N_CHIPS, ROW_SHARD, D_MODEL)
    got_in, got_sq = _swap_halves(g_in, g_sq)
    s_in, s_sq = _sum_a_in(idx, g_in, got_in), _sum_a_sq(idx, g_sq, got_sq)
    grad_x, d_ng, got_in, got_sq = _dx(segs, w4, x[0], norm_g, dout, s_in, s_sq)
    grad_in, grad_sq = _join_halves(_sum_b_in(idx, s_in, got_in), _sum_b_sq(idx, s_sq, got_sq))

    d_in, nm_in, nv_in = _adamw(w_in[0], grad_in, m_w_in[0], v_w_in[0], "adamw_in")
    flat = lambda a, b, c: jnp.concatenate([a[0], b[0], c[0]], axis=0)
    d_sq, nm_sq, nv_sq = _adamw(flat(w_sb_proj, w_hg_proj, w_out), grad_sq.reshape(3 * ROW_SHARD, D_MODEL),
                                flat(m_w_sb_proj, m_w_hg_proj, m_w_out), flat(v_w_sb_proj, v_w_hg_proj, v_w_out),
                                "adamw_sq")

    pad = jnp.zeros((8, HEAD_DIM), F32).at[0, 0].set(loss[0, 0])
    part = jnp.concatenate([pad] + [a.reshape(-1, HEAD_DIM) for a in (d_ng, d_bg, d_lb, d_hgn, d_fg)], axis=0)
    sums = _sum_small(part)
    loss_out, g_sm, d_sm, nm_sm, nv_sm = _adamw_small(
        sums, _pack_small(norm_g, b_gate, lb_logits, hg_norm_g, final_norm_g),
        _pack_small(m_norm_g, m_b_gate, m_lb_logits, m_hg_norm_g, m_final_norm_g),
        _pack_small(v_norm_g, v_b_gate, v_lb_logits, v_hg_norm_g, v_final_norm_g))

    def big(t_in, t_sq):
        sq = t_sq.reshape(3, 1, ROW_SHARD, D_MODEL)
        return t_in[None], sq[0], sq[1], sq[2]

    def order(small, in_, sb, hg, out):
        ng, bg, lbl, hgn, fg = small
        return [ng, in_, bg, lbl, hgn, sb, hg, out, fg]

    outs = [loss_out[0, 0], grad_x[None]]
    for small, (t_in, t_sq) in ((g_sm, (grad_in, grad_sq)), (d_sm, (d_in, d_sq)), (nm_sm, (nm_in, nm_sq)), (nv_sm, (nv_in, nv_sq))):
        outs += order(_unpack_small(small), *big(t_in, t_sq))
    return tuple(outs)
```

```python
import functools

import jax
import jax.numpy as jnp
from jax import lax
from jax.experimental import pallas as pl
from jax.experimental.pallas import tpu as pltpu

F32 = jnp.float32
BF16 = jnp.bfloat16

D_MODEL = 1024
HEADS = 8
HEAD_DIM = 128
IN_WIDTH = 10240
N_CHIPS = 4
W_IN_SHARD = IN_WIDTH // N_CHIPS
ROW_SHARD = D_MODEL // N_CHIPS
RMS_EPS = 1e-6

OFF_SB_Q, OFF_SB_K, OFF_SB_V, OFF_SB_Z = 0, 1024, 2048, 3072
OFF_HG_Q, OFF_HG_F, OFF_HG_I, OFF_HG_Z, OFF_GATE = 4096, 5120, 6144, 7168, 8192

QKV_COLS = 3840
SB_BLOCK = 256
SB_FWD_HEADS = 4
SB_BWD_HEADS = 2
SB_ROWS = 256
SB_DEAD = -110.0
SB_GONE = -1e30
HG_CHUNK = 32
HG_PAIR = 2 * HG_CHUNK
HG_STEP = 256
HG_MID = HG_CHUNK // 2 - 1

ADAM_LR, ADAM_B1, ADAM_B2, ADAM_EPS, ADAM_WD, ADAM_STEP = 0.001, 0.9, 0.999, 1e-08, 0.01, 10

VMEM_LIMIT = 56 * 1024 * 1024

MESH = pl.DeviceIdType.MESH


def _cparams(sem, vmem=VMEM_LIMIT):
    return pltpu.CompilerParams(dimension_semantics=sem, vmem_limit_bytes=vmem)


def _dot(a, b):
    return jnp.dot(a, b, preferred_element_type=F32)


def _dot_nt(a, b):
    return lax.dot_general(a, b, (((1,), (1,)), ((), ())), preferred_element_type=F32)


def _dot_tn(a, b):
    return lax.dot_general(a, b, (((0,), (0,)), ((), ())), preferred_element_type=F32)


def _split_dot(x, tri):
    hi = x.astype(BF16)
    lo = (x - hi.astype(F32)).astype(BF16)
    both = _dot(jnp.concatenate([hi, lo], axis=0), tri)
    return both[: x.shape[0]] + both[x.shape[0] :]


def _split_dot_left(tri, x):
    hi = x.astype(BF16)
    lo = (x - hi.astype(F32)).astype(BF16)
    return _dot(tri, hi) + _dot(tri, lo)


def _sigmoid(x):
    return 1.0 / (1.0 + jnp.exp(-x))


def _inproj(x, norm_g, w4):
    s_len = x.shape[0]
    ts = min(1024, s_len)
    tn = QKV_COLS // 3
    per = W_IN_SHARD // tn

    def body(x_ref, g_ref, w_ref, proj_ref, ht_ref, qkv_ref, h_scr):
        n = pl.program_id(1)

        @pl.when(n == 0)
        def _():
            xv = x_ref[...]
            r = lax.rsqrt(jnp.mean(xv * xv, axis=-1, keepdims=True) + RMS_EPS)
            hv = (xv * r) * g_ref[...]
            h_scr[...] = hv.astype(BF16)
            ht_ref[...] = hv.T.astype(BF16)

        p = _dot(h_scr[...], w_ref[0])
        proj_ref[...] = p

        @pl.when(n < 3)
        def _():
            qkv_ref[...] = p.astype(BF16)

    return pl.pallas_call(
        body,
        name="inproj",
        grid=(s_len // ts, IN_WIDTH // tn),
        in_specs=[
            pl.BlockSpec((ts, D_MODEL), lambda s, n: (s, 0)),
            pl.BlockSpec((1, D_MODEL), lambda s, n: (0, 0)),
            pl.BlockSpec((1, D_MODEL, tn), lambda s, n: (n // per, 0, n % per)),
        ],
        out_specs=[
            pl.BlockSpec((ts, tn), lambda s, n: (s, n)),
            pl.BlockSpec((D_MODEL, ts), lambda s, n: (0, s)),
            pl.BlockSpec((ts, tn), lambda s, n: (s, jnp.minimum(n, 2))),
        ],
        out_shape=[
            jax.ShapeDtypeStruct((s_len, IN_WIDTH), F32),
            jax.ShapeDtypeStruct((D_MODEL, s_len), BF16),
            jax.ShapeDtypeStruct((s_len, QKV_COLS), BF16),
        ],
        scratch_shapes=[pltpu.VMEM((ts, D_MODEL), BF16)],
        compiler_params=_cparams(("arbitrary", "arbitrary")),
    )(x, norm_g, w4)


def _sb_scores(qb, kb, causal, tri_excl, diag):
    z = _dot_nt(qb, kb) * HEAD_DIM ** -0.5
    ls_pos = jnp.minimum(z, 0.0) - jnp.log1p(jnp.exp(-jnp.abs(z)))
    log_not = ls_pos - z
    log_not_m = jnp.where(causal, log_not, 0.0) if diag else log_not
    return ls_pos, log_not, log_not_m, _split_dot(log_not_m, tri_excl)


def _sb_weights(ls_pos, suffix, carry, causal, diag):
    surv = suffix + carry
    w = jnp.exp(ls_pos + surv)
    return surv, (jnp.where(causal, w, 0.0) if diag else w)


def _sb_specs(s_len, blk, heads):
    width = heads * HEAD_DIM

    def blk_spec(off):
        return pl.BlockSpec((blk, width), lambda h, i: (i, off // width + h))

    def head_spec(off, buffers=2):
        return pl.BlockSpec((s_len, width), lambda h, i: (0, off // width + h), pipeline_mode=pl.Buffered(buffers))

    return blk_spec, head_spec


def _head_cols(p):
    return slice(p * HEAD_DIM, (p + 1) * HEAD_DIM)


def _sb_chains(blk, heads):
    rows = min(SB_ROWS, blk)
    return [(p, a) for p in range(heads) for a in range(blk // rows)], rows


def _sb_masks(blk, rows):
    row = lax.broadcasted_iota(jnp.int32, (rows, blk), 0)
    col = lax.broadcasted_iota(jnp.int32, (rows, blk), 1)
    causal = [row + a * rows > col for a in range(blk // rows)]
    row = lax.broadcasted_iota(jnp.int32, (blk, blk), 0)
    col = lax.broadcasted_iota(jnp.int32, (blk, blk), 1)
    tri_excl = (row > col).astype(BF16)
    tri_incl = (row >= col).astype(BF16)
    return causal, tri_excl, tri_incl


def _sb_alive(st, n_chain):
    alive = functools.reduce(jnp.maximum, [st[1 + 3 * c] for c in range(n_chain)])
    return jnp.max(alive) > SB_DEAD


def _sb_fwd(qkv):
    s_len = qkv.shape[0]
    blk = min(SB_BLOCK, s_len)
    nq = s_len // blk
    chains, rows = _sb_chains(blk, SB_FWD_HEADS)

    def body(q_ref, k_ref, v_ref, o_ref, of_ref):
        i = pl.program_id(1)
        causal, tri_excl, _ = _sb_masks(blk, rows)

        def tiles(specs, st):
            pre = []
            for j, diag, _ in specs:
                start = pl.multiple_of(j * blk, blk)
                for p, a in chains:
                    kb = k_ref[pl.ds(start, blk), _head_cols(p)]
                    qb = q_ref[a * rows : (a + 1) * rows, _head_cols(p)]
                    pre.append(_sb_scores(qb, kb, causal[a], tri_excl, diag) + (v_ref[pl.ds(start, blk), _head_cols(p)],))
            for t, (j, diag, valid) in enumerate(specs):
                new = []
                for c, (p, a) in enumerate(chains):
                    carry, acc, acc_lo = st[3 * c : 3 * c + 3]
                    if valid is not None:
                        carry = jnp.where(valid, carry, SB_GONE)
                    ls_pos, _, log_not_m, suffix, vb = pre[t * len(chains) + c]
                    surv, w = _sb_weights(ls_pos, suffix, carry, causal[a], diag)
                    wb = w.astype(BF16)
                    w_lo = (w - wb.astype(F32)).astype(BF16)
                    both = _dot(jnp.concatenate([wb, w_lo], axis=0), vb)
                    new += [surv[:, 0:1] + log_not_m[:, 0:1], acc + both[:rows], acc_lo + both[rows:]]
                st = tuple(new)
            return st

        zero = jnp.zeros((rows, HEAD_DIM), F32)
        st = tiles([(i, True, None), (jnp.maximum(i - 1, 0), False, i >= 1)],
                   (jnp.zeros((rows, 1), F32), zero, zero) * len(chains))

        def more(st):
            return (st[0] < i) & _sb_alive(st, len(chains))

        def step(st):
            return (st[0] + 1,) + tiles([(i - 1 - st[0], False, None)], st[1:])

        st = lax.while_loop(more, step, (1,) + st)[1:]
        for c, (p, a) in enumerate(chains):
            o_ref[a * rows : (a + 1) * rows, _head_cols(p)] = st[3 * c + 1]
            of_ref[a * rows : (a + 1) * rows, _head_cols(p)] = st[3 * c + 1] + st[3 * c + 2]

    blk_spec, head_spec = _sb_specs(s_len, blk, SB_FWD_HEADS)
    return pl.pallas_call(
        body,
        name="sb_fwd",
        grid=(HEADS // SB_FWD_HEADS, nq),
        in_specs=[blk_spec(OFF_SB_Q), head_spec(OFF_SB_K), head_spec(OFF_SB_V)],
        out_specs=[blk_spec(0), blk_spec(0)],
        out_shape=[jax.ShapeDtypeStruct((s_len, D_MODEL), F32)] * 2,
        compiler_params=_cparams(("arbitrary", "arbitrary")),
    )(qkv, qkv, qkv)


def _sb_bwd(qkv, o_fine, d_o):
    s_len = qkv.shape[0]
    blk = min(SB_BLOCK, s_len)
    nq = s_len // blk
    scale = HEAD_DIM ** -0.5
    chains, rows = _sb_chains(blk, SB_BWD_HEADS)

    def body(q_ref, k_ref, v_ref, of_ref, do_ref, dq_ref, dk_ref, dv_ref, dk_acc, dv_acc):
        i = pl.program_id(1)

        @pl.when(i == 0)
        def _():
            dk_acc[...] = jnp.zeros_like(dk_acc)
            dv_acc[...] = jnp.zeros_like(dv_acc)

        dob = do_ref[...].astype(BF16)
        prod = dob.astype(F32) * of_ref[...]
        causal, tri_excl, tri_incl = _sb_masks(blk, rows)

        def group(x, p, a):
            return x[a * rows : (a + 1) * rows, _head_cols(p)]

        totals = [jnp.sum(group(prod, p, a), axis=-1, keepdims=True) for p, a in chains]

        def tiles(specs, st):
            pre = []
            for j, diag, _ in specs:
                start = pl.multiple_of(j * blk, blk)
                for p, a in chains:
                    kb = k_ref[pl.ds(start, blk), _head_cols(p)]
                    vb = v_ref[pl.ds(start, blk), _head_cols(p)]
                    qb, dob_c = group(q_ref, p, a), group(dob, p, a)
                    pre.append(_sb_scores(qb, kb, causal[a], tri_excl, diag) + (_dot_nt(dob_c, vb), qb, kb, dob_c))
            for t, (j, diag, valid) in enumerate(specs):
                start = pl.multiple_of(j * blk, blk)
                mids = []
                for c, (p, a) in enumerate(chains):
                    c_not = st[3 * c]
                    if valid is not None:
                        c_not = jnp.where(valid, c_not, SB_GONE)
                    ls_pos, _, _, suffix, d_w = pre[t * len(chains) + c][:5]
                    surv, w = _sb_weights(ls_pos, suffix, c_not, causal[a], diag)
                    dlw = d_w * w
                    mids.append((surv, w, dlw, _split_dot(dlw, tri_incl)))
                new = []
                dk_new = [None] * SB_BWD_HEADS
                dv_new = [None] * SB_BWD_HEADS
                for c, (p, a) in enumerate(chains):
                    c_dlw, dq = st[3 * c + 1 : 3 * c + 3]
                    ls_pos, log_not, log_not_m, _, _, qb, kb, dob_c = pre[t * len(chains) + c]
                    surv, w, dlw, suffix = mids[c]
                    d_not = totals[c] - c_dlw - suffix
                    dz = (dlw * jnp.exp(log_not) - d_not * jnp.exp(ls_pos)) * scale
                    if diag:
                        dz = jnp.where(causal[a], dz, 0.0)
                    if valid is not None:
                        dz = jnp.where(valid, dz, 0.0)
                    dzb = dz.astype(BF16)
                    dk_c, dv_c = _dot_tn(dzb, qb), _dot_tn(w.astype(BF16), dob_c)
                    dk_new[p] = dk_c if dk_new[p] is None else dk_new[p] + dk_c
                    dv_new[p] = dv_c if dv_new[p] is None else dv_new[p] + dv_c
                    new += [surv[:, 0:1] + log_not_m[:, 0:1], c_dlw + suffix[:, 0:1], dq + _dot(dzb, kb)]
                for p in range(SB_BWD_HEADS):
                    dk_acc[pl.ds(start, blk), _head_cols(p)] += dk_new[p]
                    dv_acc[pl.ds(start, blk), _head_cols(p)] += dv_new[p]
                st = tuple(new)
            return st

        zcol = jnp.zeros((rows, 1), F32)
        st = tiles([(i, True, None), (jnp.maximum(i - 1, 0), False, i >= 1)],
                   (zcol, zcol, jnp.zeros((rows, HEAD_DIM), F32)) * len(chains))

        def more(st):
            return (st[0] < i) & _sb_alive(st, len(chains))

        def step(st):
            return (st[0] + 1,) + tiles([(i - 1 - st[0], False, None)], st[1:])

        st = lax.while_loop(more, step, (1,) + st)[1:]
        for c, (p, a) in enumerate(chains):
            dq_ref[a * rows : (a + 1) * rows, _head_cols(p)] = st[3 * c + 2].astype(BF16)

        @pl.when(i == nq - 1)
        def _():
            dk_ref[...] = dk_acc[...].astype(BF16)
            dv_ref[...] = dv_acc[...].astype(BF16)

    blk_spec, head_spec = _sb_specs(s_len, blk, SB_BWD_HEADS)
    width = SB_BWD_HEADS * HEAD_DIM
    return pl.pallas_call(
        body,
        name="sb_bwd",
        grid=(HEADS // SB_BWD_HEADS, nq),
        in_specs=[blk_spec(OFF_SB_Q), head_spec(OFF_SB_K, 1), head_spec(OFF_SB_V, 1), blk_spec(0), blk_spec(0)],
        out_specs=[blk_spec(0), head_spec(0), head_spec(0)],
        out_shape=[jax.ShapeDtypeStruct((s_len, D_MODEL), BF16)] * 3,
        scratch_shapes=[pltpu.VMEM((s_len, width), F32), pltpu.VMEM((s_len, width), F32)],
        compiler_params=_cparams(("arbitrary", "arbitrary")),
    )(qkv, qkv, qkv, o_fine, d_o)


def _hg_lower_bound(lbl_ref):
    l0 = lbl_ref[0:1, :]
    l1 = lbl_ref[1:2, :]
    mx = jnp.maximum(l0, l1)
    e0 = jnp.exp(l0 - mx)
    e1 = jnp.exp(l1 - mx)
    return e0 / (e0 + e1)


def _hg_gates(hq, hf, lb):
    sig_f = _sigmoid(hf)
    f = lb + (1.0 - lb) * sig_f
    g = jnp.log(f)
    kk = 1.0 - f
    sig_q = _sigmoid(hq)
    qq = hq * sig_q
    return qq, kk, g, f, sig_f, sig_q


def _period_bcast(x, r, rows, period):
    w = x.shape[-1]
    x3 = x.reshape(rows // period, period, w)
    return jnp.broadcast_to(x3[:, r : r + 1, :], x3.shape).reshape(rows, w)


def _blockdiag(rows, kind):
    row = lax.broadcasted_iota(jnp.int32, (rows, rows), 0)
    col = lax.broadcasted_iota(jnp.int32, (rows, rows), 1)
    if kind in ("next", "prev"):
        first, second = (row, col) if kind == "next" else (col, row)
        keep = ((row // HG_PAIR) == (col // HG_PAIR)) & (first % HG_PAIR < HG_CHUNK) & (second % HG_PAIR >= HG_CHUNK)
    else:
        keep = (row // HG_CHUNK) == (col // HG_CHUNK)
        if kind == "lower":
            keep = keep & (row >= col)
        elif kind == "upper":
            keep = keep & (row <= col)
    return jnp.where(keep, 1.0, 0.0).astype(BF16)


def _hg_operands(hq, hf, lb, rows):
    qq, kk, g, f, sig_f, sig_q = _hg_gates(hq, hf, lb)
    cum = _split_dot_left(_blockdiag(rows, "lower"), g)
    mid = _period_bcast(cum, HG_MID, rows, HG_CHUNK)
    last = _period_bcast(cum, HG_CHUNK - 1, rows, HG_CHUNK)
    last0 = _period_bcast(cum, HG_CHUNK - 1, rows, HG_PAIR)
    last1 = _period_bcast(cum, HG_PAIR - 1, rows, HG_PAIR)
    second = (lax.broadcasted_iota(jnp.int32, cum.shape, 0) % HG_PAIR) >= HG_CHUNK
    e = dict(qm=jnp.exp(cum - mid), km=jnp.exp(mid - cum), qd=jnp.exp(cum), kl=jnp.exp(last - cum),
             q_in=jnp.where(second, jnp.exp(last0), 1.0), k_out=jnp.where(second, 1.0, jnp.exp(last1)),
             pair=jnp.exp(last0 + last1))
    v = dict(qm=qq * e["qm"], km=kk * e["km"], qd=qq * e["qd"], kl=kk * e["kl"])
    v["qp"] = v["qd"] * e["q_in"]
    v["kp"] = v["kl"] * e["k_out"]
    return v, e, second, (f, sig_f, sig_q)


def _hg_store_operands(v, second, hi, refs):
    zero = jnp.zeros_like(v["qm"])
    q_cat, k_cat, qp_b, kp_b, v_b = refs
    q_cat[:, 0:D_MODEL] = jnp.where(second, zero, v["qm"]).astype(BF16)
    q_cat[:, D_MODEL : 2 * D_MODEL] = jnp.where(second, v["qm"], zero).astype(BF16)
    q_cat[:, 2 * D_MODEL :] = jnp.where(second, v["qd"], zero).astype(BF16)
    k_cat[:, 0:D_MODEL] = jnp.where(second, zero, v["km"]).astype(BF16)
    k_cat[:, D_MODEL : 2 * D_MODEL] = jnp.where(second, v["km"], zero).astype(BF16)
    k_cat[:, 2 * D_MODEL :] = jnp.where(second, zero, v["kl"]).astype(BF16)
    qp_b[...] = v["qp"].astype(BF16)
    kp_b[...] = v["kp"].astype(BF16)
    v_b[...] = hi.astype(BF16)


def _hg_pair_operands(cat, r0, c0):
    return jnp.concatenate([cat[r0 : r0 + HG_PAIR, g * D_MODEL + c0 : g * D_MODEL + c0 + HEAD_DIM] for g in range(3)], axis=1)


def _hg_fwd(proj, lbl):
    s_len = proj.shape[0]
    rows = min(HG_STEP, s_len)
    n_pairs = rows // HG_PAIR

    def body(hq_ref, hf_ref, hi_ref, lbl_ref, o_ref, st_ref, state, q_cat, k_cat, qp_b, kp_b, v_b):
        @pl.when(pl.program_id(0) == 0)
        def _():
            state[...] = jnp.zeros_like(state)

        v, e, second, _ = _hg_operands(hq_ref[...], hf_ref[...], _hg_lower_bound(lbl_ref), rows)
        _hg_store_operands(v, second, hi_ref[...], (q_cat, k_cat, qp_b, kp_b, v_b))
        e_pair = e["pair"]
        row = lax.broadcasted_iota(jnp.int32, (HG_PAIR, HG_PAIR), 0)
        col = lax.broadcasted_iota(jnp.int32, (HG_PAIR, HG_PAIR), 1)
        causal = row >= col

        for u in range(n_pairs):
            r0 = u * HG_PAIR
            sls = [(slice(r0, r0 + HG_PAIR), slice(h * HEAD_DIM, (h + 1) * HEAD_DIM)) for h in range(HEADS)]
            a_s = [jnp.where(causal, _dot_nt(_hg_pair_operands(q_cat, r0, h * HEAD_DIM),
                                             _hg_pair_operands(k_cat, r0, h * HEAD_DIM)), 0.0).astype(BF16)
                   for h in range(HEADS)]
            st_s = [state[h] for h in range(HEADS)]
            for h, sl in enumerate(sls):
                st_ref[u, h] = st_s[h]
                state[h] = st_s[h] * e_pair[r0 : r0 + 1, sl[1]] + _dot_tn(v_b[sl], kp_b[sl])
            for h, sl in enumerate(sls):
                o_ref[sl] = _dot(a_s[h], v_b[sl]) + _dot_nt(qp_b[sl], st_s[h].astype(BF16))

    def col_spec(off):
        return pl.BlockSpec((rows, D_MODEL), lambda s: (s, off // D_MODEL))

    bf_tile = pltpu.VMEM((rows, D_MODEL), BF16)
    bf_cat = pltpu.VMEM((rows, 3 * D_MODEL), BF16)
    scratch = [pltpu.VMEM((HEADS, HEAD_DIM, HEAD_DIM), F32), bf_cat, bf_cat, bf_tile, bf_tile, bf_tile]
    return pl.pallas_call(
        body,
        name="hg_fwd",
        grid=(s_len // rows,),
        in_specs=[col_spec(OFF_HG_Q), col_spec(OFF_HG_F), col_spec(OFF_HG_I), pl.BlockSpec((2, D_MODEL), lambda s: (0, 0))],
        out_specs=[
            pl.BlockSpec((rows, D_MODEL), lambda s: (s, 0)),
            pl.BlockSpec((n_pairs, HEADS, HEAD_DIM, HEAD_DIM), lambda s: (s, 0, 0, 0)),
        ],
        out_shape=[
            jax.ShapeDtypeStruct((s_len, D_MODEL), F32),
            jax.ShapeDtypeStruct((s_len // HG_PAIR, HEADS, HEAD_DIM, HEAD_DIM), F32),
        ],
        scratch_shapes=scratch,
        compiler_params=_cparams(("arbitrary",)),
    )(proj, proj, proj, lbl)


def _hg_bwd(proj, lbl, states, d_o):
    s_len = proj.shape[0]
    rows = min(HG_STEP, s_len)
    n_pairs = rows // HG_PAIR
    n_steps = s_len // rows

    def body(hq_ref, hf_ref, hi_ref, lbl_ref, st_ref, do_ref, dp_ref, dlb_ref,
             dstate, q_cat, k_cat, qp_b, kp_b, v_b, do_b, d_qcat, d_kcat, d_qp, d_kp, d_v, d_pair):
        @pl.when(pl.program_id(0) == 0)
        def _():
            dstate[...] = jnp.zeros_like(dstate)
            dlb_ref[...] = jnp.zeros_like(dlb_ref)

        lb = _hg_lower_bound(lbl_ref)
        hq = hq_ref[...]
        v, e, second, (f, sig_f, sig_q) = _hg_operands(hq, hf_ref[...], lb, rows)
        _hg_store_operands(v, second, hi_ref[...], (q_cat, k_cat, qp_b, kp_b, v_b))
        do_b[...] = do_ref[...].astype(BF16)
        e_pair = e["pair"]
        row = lax.broadcasted_iota(jnp.int32, (HG_PAIR, HG_PAIR), 0)
        col = lax.broadcasted_iota(jnp.int32, (HG_PAIR, HG_PAIR), 1)
        causal = row >= col

        for u in reversed(range(n_pairs)):
            r0 = u * HG_PAIR
            sls = [(slice(r0, r0 + HG_PAIR), slice(h * HEAD_DIM, (h + 1) * HEAD_DIM)) for h in range(HEADS)]
            ops = [(_hg_pair_operands(q_cat, r0, h * HEAD_DIM), _hg_pair_operands(k_cat, r0, h * HEAD_DIM))
                   for h in range(HEADS)]
            a_s = [jnp.where(causal, _dot_nt(lhs, rhs), 0.0).astype(BF16) for lhs, rhs in ops]
            da_s = [jnp.where(causal, _dot_nt(do_b[sl], v_b[sl]), 0.0).astype(BF16) for sl in sls]
            st0_s = [st_ref[u, h] for h in range(HEADS)]
            ds1_s = [dstate[h] for h in range(HEADS)]
            ds1b_s = [ds1.astype(BF16) for ds1 in ds1_s]
            for h, sl in enumerate(sls):
                decay = e_pair[r0 : r0 + 1, sl[1]]
                d_pair[u : u + 1, sl[1]] = decay * jnp.sum(ds1_s[h] * st0_s[h], axis=0, keepdims=True)
                dstate[h] = ds1_s[h] * decay + _dot_tn(do_b[sl], qp_b[sl])
            for h, sl in enumerate(sls):
                d_qp[sl] = _dot(do_b[sl], st0_s[h].astype(BF16))
                d_kp[sl] = _dot(v_b[sl], ds1b_s[h])
            for h, sl in enumerate(sls):
                d_v[sl] = _dot_tn(a_s[h], do_b[sl]) + _dot_nt(kp_b[sl], ds1b_s[h])
            for h, sl in enumerate(sls):
                d_lhs = _dot(da_s[h], ops[h][1])
                d_rhs = _dot_tn(da_s[h], ops[h][0])
                for g in range(3):
                    gsl = (sl[0], slice(g * D_MODEL + h * HEAD_DIM, g * D_MODEL + (h + 1) * HEAD_DIM))
                    d_qcat[gsl] = d_lhs[:, g * HEAD_DIM : (g + 1) * HEAD_DIM]
                    d_kcat[gsl] = d_rhs[:, g * HEAD_DIM : (g + 1) * HEAD_DIM]

        zero = jnp.zeros_like(hq)
        dqm = jnp.where(second, d_qcat[:, D_MODEL : 2 * D_MODEL], d_qcat[:, 0:D_MODEL])
        dkm = jnp.where(second, d_kcat[:, D_MODEL : 2 * D_MODEL], d_kcat[:, 0:D_MODEL])
        dqp, dkp = d_qp[...], d_kp[...]
        dqd = dqp * e["q_in"] + jnp.where(second, d_qcat[:, 2 * D_MODEL :], zero)
        dkl = dkp * e["k_out"] + jnp.where(second, zero, d_kcat[:, 2 * D_MODEL :])
        dq = dqm * e["qm"] + dqd * e["qd"]
        dk = dkm * e["km"] + dkl * e["kl"]
        t_kl = dkl * v["kl"]
        dcum = dqm * v["qm"] - dkm * v["km"] + dqd * v["qd"] - t_kl
        dp = d_pair[...]
        dp_b = jnp.broadcast_to(dp[:, None, :], (n_pairs, HG_PAIR, D_MODEL)).reshape(rows, D_MODEL)
        dg = (_split_dot_left(_blockdiag(rows, "upper"), dcum) + _split_dot_left(_blockdiag(rows, "all"), t_kl)
              + _split_dot_left(_blockdiag(rows, "next"), dqp * v["qp"])
              + _split_dot_left(_blockdiag(rows, "prev"), dkp * v["kp"]) + dp_b)
        df = dg / f - dk
        one_m = 1.0 - sig_f
        dp_ref[:, 0:D_MODEL] = (dq * (sig_q * (1.0 + hq * (1.0 - sig_q)))).astype(BF16)
        dp_ref[:, D_MODEL : 2 * D_MODEL] = (df * (1.0 - lb) * sig_f * one_m).astype(BF16)
        dp_ref[:, 2 * D_MODEL : 3 * D_MODEL] = d_v[...].astype(BF16)
        dlb_ref[...] += jnp.sum(df * one_m, axis=0, keepdims=True)

    def col_spec(off):
        return pl.BlockSpec((rows, D_MODEL), lambda s: (n_steps - 1 - s, off // D_MODEL))

    f32_tile = pltpu.VMEM((rows, D_MODEL), F32)
    f32_cat = pltpu.VMEM((rows, 3 * D_MODEL), F32)
    bf_tile = pltpu.VMEM((rows, D_MODEL), BF16)
    bf_cat = pltpu.VMEM((rows, 3 * D_MODEL), BF16)
    scratch = [pltpu.VMEM((HEADS, HEAD_DIM, HEAD_DIM), F32), bf_cat, bf_cat, bf_tile, bf_tile, bf_tile, bf_tile,
               f32_cat, f32_cat, f32_tile, f32_tile, f32_tile, pltpu.VMEM((n_pairs, D_MODEL), F32)]
    return pl.pallas_call(
        body,
        name="hg_bwd",
        grid=(n_steps,),
        in_specs=[
            col_spec(OFF_HG_Q), col_spec(OFF_HG_F), col_spec(OFF_HG_I),
            pl.BlockSpec((2, D_MODEL), lambda s: (0, 0)),
            pl.BlockSpec((n_pairs, HEADS, HEAD_DIM, HEAD_DIM), lambda s: (n_steps - 1 - s, 0, 0, 0)),
            pl.BlockSpec((rows, D_MODEL), lambda s: (n_steps - 1 - s, 0)),
        ],
        out_specs=[
            pl.BlockSpec((rows, 3 * D_MODEL), lambda s: (n_steps - 1 - s, 0)),
            pl.BlockSpec((1, D_MODEL), lambda s: (0, 0)),
        ],
        out_shape=[
            jax.ShapeDtypeStruct((s_len, 3 * D_MODEL), BF16),
            jax.ShapeDtypeStruct((1, D_MODEL), F32),
        ],
        scratch_shapes=scratch,
        compiler_params=_cparams(("arbitrary",)),
    )(proj, proj, proj, lbl, states, d_o)


def _mid(proj, sb_o, hg_o, x, target, b_gate, hg_gain, final_g, w_sb, w_hg, w_out):
    s_len = proj.shape[0]
    ts = min(256, s_len)
    inv_d = 1.0 / D_MODEL

    def body(zsb_ref, hz_ref, gl_ref, sbo_ref, hgo_ref, x_ref, tgt_ref, bg_ref, hgn_ref, fg_ref,
             wsb_ref, whg_ref, wout_ref,
             dout_ref, dsbo_ref, dhgo_ref, dzsb_ref, dhz_ref, dgl_ref,
             asb_ref, dusb_ref, ahg_ref, duhg_ref, y_ref, doutb_ref,
             loss_ref, dfg_ref, dbg_ref, dhgn_ref):
        @pl.when(pl.program_id(0) == 0)
        def _():
            loss_ref[...] = jnp.zeros_like(loss_ref)
            dfg_ref[...] = jnp.zeros_like(dfg_ref)
            dbg_ref[...] = jnp.zeros_like(dbg_ref)
            dhgn_ref[...] = jnp.zeros_like(dhgn_ref)

        z_sb = zsb_ref[...]
        sb_o = sbo_ref[...]
        sig_zsb = _sigmoid(z_sb)
        silu_zsb = z_sb * sig_zsb
        a_sb = (sb_o * silu_zsb).astype(BF16)
        u_sb = _dot(a_sb, wsb_ref[...])

        hg_o = hgo_ref[...]
        gain = hgn_ref[...]
        r_parts, yn_parts = [], []
        for h in range(HEADS):
            oh = hg_o[:, h * HEAD_DIM : (h + 1) * HEAD_DIM]
            r = lax.rsqrt(jnp.mean(oh * oh, axis=-1, keepdims=True) + RMS_EPS)
            r_parts.append(jnp.broadcast_to(r, oh.shape))
            yn_parts.append(oh * r)
        r_hg = jnp.concatenate(r_parts, axis=-1)
        yn_hg = jnp.concatenate(yn_parts, axis=-1)
        hn = yn_hg * gain
        hz = hz_ref[...]
        sig_hz = _sigmoid(hz)
        silu_hz = hz * sig_hz
        a_hg = (hn * silu_hz).astype(BF16)
        u_hg = _dot(a_hg, whg_ref[...])

        gates = _sigmoid(gl_ref[...] + bg_ref[...])
        g_sb = gates[:, 0:D_MODEL]
        g_hg = gates[:, D_MODEL:]
        y = (g_sb * u_sb + g_hg * u_hg).astype(BF16)
        out = x_ref[...] + _dot(y, wout_ref[...])
        r2 = lax.rsqrt(jnp.mean(out * out, axis=-1, keepdims=True) + RMS_EPS)
        yn = out * r2
        fg = fg_ref[...]
        diff = yn * fg - tgt_ref[...]
        loss_ref[...] += 0.5 * inv_d * jnp.sum(diff * diff)

        dyf = diff * inv_d
        dfg_ref[...] += jnp.sum(dyf * yn, axis=0, keepdims=True)
        dyn = dyf * fg
        dout = r2 * (dyn - yn * jnp.mean(dyn * yn, axis=-1, keepdims=True))
        dout_ref[...] = dout
        doutb = dout.astype(BF16)
        doutb_ref[...] = doutb
        dy = _dot_nt(doutb, wout_ref[...])
        du_sb = (dy * g_sb).astype(BF16)
        du_hg = (dy * g_hg).astype(BF16)
        dgl_sb = dy * u_sb * g_sb * (1.0 - g_sb)
        dgl_hg = dy * u_hg * g_hg * (1.0 - g_hg)
        dgl_ref[:, 0:D_MODEL] = dgl_sb.astype(BF16)
        dgl_ref[:, D_MODEL:] = dgl_hg.astype(BF16)
        dbg_ref[:, 0:D_MODEL] += jnp.sum(dgl_sb, axis=0, keepdims=True)
        dbg_ref[:, D_MODEL:] += jnp.sum(dgl_hg, axis=0, keepdims=True)

        da_sb = _dot_nt(du_sb, wsb_ref[...])
        dsbo_ref[...] = da_sb * silu_zsb
        dzsb_ref[...] = (da_sb * sb_o * (sig_zsb * (1.0 + z_sb * (1.0 - sig_zsb)))).astype(BF16)

        da_hg = _dot_nt(du_hg, whg_ref[...])
        dhn = da_hg * silu_hz
        dhz_ref[...] = (da_hg * hn * (sig_hz * (1.0 + hz * (1.0 - sig_hz)))).astype(BF16)
        dhgn_ref[...] += jnp.sum(dhn * yn_hg, axis=0, keepdims=True)
        dyn_hg = dhn * gain
        prod = dyn_hg * yn_hg
        m_parts = []
        for h in range(HEADS):
            ph = prod[:, h * HEAD_DIM : (h + 1) * HEAD_DIM]
            m_parts.append(jnp.broadcast_to(jnp.mean(ph, axis=-1, keepdims=True), ph.shape))
        dhgo_ref[...] = r_hg * (dyn_hg - yn_hg * jnp.concatenate(m_parts, axis=-1))

        asb_ref[...] = a_sb
        dusb_ref[...] = du_sb
        ahg_ref[...] = a_hg
        duhg_ref[...] = du_hg
        y_ref[...] = y

    def tile(width, off=0):
        return pl.BlockSpec((ts, width), lambda s: (s, off // width))

    def whole(shape):
        return pl.BlockSpec(shape, lambda s: (0,) * len(shape))

    def weight():
        return pl.BlockSpec((D_MODEL, D_MODEL), lambda s: (0, 0), pipeline_mode=pl.Buffered(1))

    f32_act = jax.ShapeDtypeStruct((s_len, D_MODEL), F32)
    bf_act = jax.ShapeDtypeStruct((s_len, D_MODEL), BF16)
    return pl.pallas_call(
        body,
        name="mid",
        grid=(s_len // ts,),
        in_specs=[
            tile(D_MODEL, OFF_SB_Z), tile(D_MODEL, OFF_HG_Z), tile(2 * D_MODEL, OFF_GATE),
            tile(D_MODEL), tile(D_MODEL), tile(D_MODEL), tile(D_MODEL),
            whole((1, 2 * D_MODEL)), whole((1, D_MODEL)), whole((1, D_MODEL)),
            weight(), weight(), weight(),
        ],
        out_specs=[
            tile(D_MODEL), tile(D_MODEL), tile(D_MODEL), tile(D_MODEL), tile(D_MODEL), tile(2 * D_MODEL),
            tile(D_MODEL), tile(D_MODEL), tile(D_MODEL), tile(D_MODEL), tile(D_MODEL), tile(D_MODEL),
            whole((1, 1)), whole((1, D_MODEL)), whole((1, 2 * D_MODEL)), whole((1, D_MODEL)),
        ],
        out_shape=[
            f32_act, f32_act, f32_act, bf_act, bf_act, jax.ShapeDtypeStruct((s_len, 2 * D_MODEL), BF16),
            bf_act, bf_act, bf_act, bf_act, bf_act, bf_act,
            jax.ShapeDtypeStruct((1, 1), F32), jax.ShapeDtypeStruct((1, D_MODEL), F32),
            jax.ShapeDtypeStruct((1, 2 * D_MODEL), F32), jax.ShapeDtypeStruct((1, D_MODEL), F32),
        ],
        compiler_params=_cparams(("arbitrary",)),
    )(proj, proj, proj, sb_o, hg_o, x, target, b_gate, hg_gain, final_g, w_sb, w_hg, w_out)


def _grad_matmul(a, b, name, tn):
    s_len, m = a.shape
    n = b.shape[1]
    tk = min(512, s_len)

    def body(a_ref, b_ref, o_ref):
        @pl.when(pl.program_id(1) == 0)
        def _():
            o_ref[...] = jnp.zeros_like(o_ref)

        o_ref[...] += _dot_tn(a_ref[...], b_ref[...])

    return pl.pallas_call(
        body,
        name=name,
        grid=(n // tn, s_len // tk),
        in_specs=[pl.BlockSpec((tk, m), lambda j, k: (k, 0)), pl.BlockSpec((tk, tn), lambda j, k: (k, j))],
        out_specs=pl.BlockSpec((m, tn), lambda j, k: (0, j)),
        out_shape=jax.ShapeDtypeStruct((m, n), F32),
        compiler_params=_cparams(("arbitrary", "arbitrary")),
    )(a, b)


SEG_WIDTHS = (1024, 1024, 1024, 1024, 3072, 1024, 2048)


def _seg_bounds(tile):
    bounds = [0]
    for w in SEG_WIDTHS:
        bounds.append(bounds[-1] + w // tile)
    return bounds


def _grad_w_in(h_t, segs):
    m, s_len = h_t.shape
    tk = min(1024, s_len)
    tn = 1024
    nk = s_len // tk
    bounds = _seg_bounds(tn)

    def body(a_ref, *refs):
        seg_refs, o_ref = refs[:-1], refs[-1]
        j = pl.program_id(0)

        @pl.when(pl.program_id(1) == 0)
        def _():
            o_ref[...] = jnp.zeros_like(o_ref)

        for i, ref in enumerate(seg_refs):
            @pl.when((j >= bounds[i]) & (j < bounds[i + 1]))
            def _(ref=ref):
                o_ref[...] += _dot(a_ref[...], ref[...])

    def seg_spec(lo, hi):
        def index(j, k):
            return (jnp.where(j < lo, 0, jnp.where(j >= hi, nk - 1, k)), jnp.clip(j - lo, 0, hi - lo - 1))
        return pl.BlockSpec((tk, tn), index)

    return pl.pallas_call(
        body,
        name="grad_w_in",
        grid=(IN_WIDTH // tn, nk),
        in_specs=[pl.BlockSpec((m, tk), lambda j, k: (0, k))] + [seg_spec(bounds[i], bounds[i + 1]) for i in range(7)],
        out_specs=pl.BlockSpec((m, tn), lambda j, k: (0, j)),
        out_shape=jax.ShapeDtypeStruct((m, IN_WIDTH), F32),
        compiler_params=_cparams(("arbitrary", "arbitrary")),
    )(h_t, *segs)


EXCHANGE_IN_PIECES = 8
EXCHANGE_PIECES = EXCHANGE_IN_PIECES + 3


def _exchange_copies(sin_ref, ssq_ref, got_in, got_sq, send_sems, recv_sems):
    _, _, c, chips = _position()
    rows = HALF_IN // EXCHANGE_IN_PIECES
    copies = []
    for k, (px, py) in enumerate(chips):
        chip = 2 * px + py
        for p in range(EXCHANGE_PIECES):
            if p < EXCHANGE_IN_PIECES:
                src, dst = sin_ref.at[chip, pl.ds(p * rows, rows), :], got_in.at[k, pl.ds(p * rows, rows), :]
            else:
                src, dst = ssq_ref.at[p - EXCHANGE_IN_PIECES, chip], got_sq.at[k, p - EXCHANGE_IN_PIECES]
            copies.append(_remote(src, dst, send_sems.at[k, p], recv_sems.at[k, p], (px, py, c)))
    return copies


def _dx(segs, w4, x, norm_g, dout, s_in, s_sq):
    s_len = x.shape[0]
    ts = min(1024, s_len)
    tk = 512
    per = W_IN_SHARD // tk
    nk = IN_WIDTH // tk
    ns = s_len // ts
    bounds = _seg_bounds(tk)

    def body(*refs):
        seg_refs = refs[:7]
        w_ref, x_ref, g_ref, dout_ref, sin_ref, ssq_ref, gx_ref, dg_ref, got_in, got_sq, acc, send_sems, recv_sems = refs[7:]
        s, k = pl.program_id(0), pl.program_id(1)

        @pl.when((s == 0) & (k == 0))
        def _():
            dg_ref[...] = jnp.zeros_like(dg_ref)
            for cp in _exchange_copies(sin_ref, ssq_ref, got_in, got_sq, send_sems, recv_sems):
                cp.start()

        @pl.when(k == 0)
        def _():
            acc[...] = jnp.zeros_like(acc)

        for i, ref in enumerate(seg_refs):
            @pl.when((k >= bounds[i]) & (k < bounds[i + 1]))
            def _(ref=ref):
                acc[...] += _dot_nt(ref[...], w_ref[0])

        @pl.when(k == nk - 1)
        def _():
            dh = acc[...]
            xv = x_ref[...]
            r = lax.rsqrt(jnp.mean(xv * xv, axis=-1, keepdims=True) + RMS_EPS)
            xn = xv * r
            dg_ref[...] += jnp.sum(dh * xn, axis=0, keepdims=True)
            dxn = dh * g_ref[...]
            gx_ref[...] = r * (dxn - xn * jnp.mean(dxn * xn, axis=-1, keepdims=True)) + dout_ref[...]

        @pl.when((s == ns - 1) & (k == nk - 1))
        def _():
            for cp in _exchange_copies(sin_ref, ssq_ref, got_in, got_sq, send_sems, recv_sems):
                cp.wait()

    def seg_spec(lo, hi):
        return pl.BlockSpec((ts, tk), lambda s, k: (s, jnp.clip(k - lo, 0, hi - lo - 1)))

    row_tile = pl.BlockSpec((ts, D_MODEL), lambda s, k: (s, 0))
    vec = pl.BlockSpec((1, D_MODEL), lambda s, k: (0, 0))
    return pl.pallas_call(
        body,
        name="dx",
        grid=(ns, nk),
        in_specs=[seg_spec(bounds[i], bounds[i + 1]) for i in range(7)] + [
            pl.BlockSpec((1, D_MODEL, tk), lambda s, k: (k // per, 0, k % per)),
            row_tile, vec, row_tile, ANY, ANY,
        ],
        out_specs=[row_tile, vec, ANY, ANY],
        out_shape=[jax.ShapeDtypeStruct((s_len, D_MODEL), F32), jax.ShapeDtypeStruct((1, D_MODEL), F32),
                   jax.ShapeDtypeStruct((3, HALF_IN, W_IN_SHARD), WIRE),
                   jax.ShapeDtypeStruct((3, 3, HALF_SQ, D_MODEL), WIRE)],
        scratch_shapes=[pltpu.VMEM((ts, D_MODEL), F32),
                        pltpu.SemaphoreType.DMA((3, EXCHANGE_PIECES)), pltpu.SemaphoreType.DMA((3, EXCHANGE_PIECES))],
        compiler_params=_cparams(("arbitrary", "arbitrary")),
    )(*segs, w4, x, norm_g, dout, s_in, s_sq)


def _local_grads(x, target, norm_g, b_gate, lbl, hg_gain, final_g, w4, w_sb, w_hg, w_out):
    proj, h_t, qkv = _inproj(x, norm_g, w4)
    sb_o, sb_o_fine = _sb_fwd(qkv)
    hg_o, states = _hg_fwd(proj, lbl)
    (dout, d_sbo, d_hgo, d_zsb, d_hz, d_gl, a_sb, du_sb, a_hg, du_hg, y, doutb,
     loss, d_fg, d_bg, d_hgn) = _mid(proj, sb_o, hg_o, x, target, b_gate, hg_gain, final_g, w_sb, w_hg, w_out)
    g_w_sb = _grad_matmul(a_sb, du_sb, "grad_w_sb", 512)
    g_w_hg = _grad_matmul(a_hg, du_hg, "grad_w_hg", 512)
    g_w_out = _grad_matmul(y, doutb, "grad_w_out", 512)
    d_q, d_k, d_v = _sb_bwd(qkv, sb_o_fine, d_sbo)
    d_hg, d_lb = _hg_bwd(proj, lbl, states, d_hgo)
    segs = (d_q, d_k, d_v, d_zsb, d_hg, d_hz, d_gl)
    g_w_in = _grad_w_in(h_t, segs)
    return g_w_in, g_w_sb, g_w_hg, g_w_out, segs, dout, loss, d_bg, d_lb, d_hgn, d_fg


ANY = pl.BlockSpec(memory_space=pl.ANY)
WIRE = BF16
HALF_IN = D_MODEL // 2
HALF_SQ = ROW_SHARD // 2


def _position():
    x, y, c = lax.axis_index("x"), lax.axis_index("y"), lax.axis_index("c")
    chips = [(1 - x, y), (x, 1 - y), (1 - x, 1 - y)]
    return x, y, c, chips


def _remote(src, dst, send_sem, recv_sem, to):
    return pltpu.make_async_remote_copy(src_ref=src, dst_ref=dst, send_sem=send_sem, recv_sem=recv_sem,
                                        device_id=to, device_id_type=MESH)


def _gather_weights(w_in_b, w_sq_b):
    n_in = 4
    n_piece = n_in + 3
    rows = HALF_IN // n_in

    def body(win_ref, wsq_ref, in_ref, sq_ref, send_sems, recv_sems):
        x, y, c, chips = _position()
        me = 2 * x + y
        sibling = (x, y, 1 - c)

        def src_piece(p):
            if p < n_in:
                return win_ref.at[pl.ds(c * HALF_IN + p * rows, rows), :]
            return wsq_ref.at[p - n_in, pl.ds(c * HALF_SQ, HALF_SQ), :]

        def piece(p, chip, core):
            if p < n_in:
                return in_ref.at[chip, pl.ds(core * HALF_IN + p * rows, rows), :]
            return sq_ref.at[p - n_in, chip, pl.ds(core * HALF_SQ, HALF_SQ), :]

        sends = []
        for k, (px, py) in enumerate(chips):
            for p in range(n_piece):
                sends.append(_remote(src_piece(p), piece(p, me, c), send_sems.at[k, p], recv_sems.at[k, p], (px, py, c)))
        for cp in sends:
            cp.start()
        for k, (px, py) in enumerate(chips):
            chip = 2 * px + py
            for p in range(n_piece):
                got = piece(p, chip, c)
                _remote(got, got, send_sems.at[k, p], recv_sems.at[k, p], (px, py, c)).wait_recv()
                fwd = _remote(got, got, send_sems.at[3 + k, p], recv_sems.at[3 + k, p], sibling)
                fwd.start()
                sends.append(fwd)
        for k, (px, py) in enumerate(chips):
            chip = 2 * px + py
            for p in range(n_piece):
                got = piece(p, chip, 1 - c)
                _remote(got, got, send_sems.at[3 + k, p], recv_sems.at[3 + k, p], sibling).wait_recv()
        for cp in sends:
            cp.wait_send()

    return pl.pallas_call(
        body,
        name="gather_weights",
        in_specs=[ANY, ANY],
        out_specs=[ANY, ANY],
        out_shape=[jax.ShapeDtypeStruct((N_CHIPS, D_MODEL, W_IN_SHARD), BF16),
                   jax.ShapeDtypeStruct((3, N_CHIPS, ROW_SHARD, D_MODEL), BF16)],
        scratch_shapes=[pltpu.SemaphoreType.DMA((6, n_piece)), pltpu.SemaphoreType.DMA((6, n_piece))],
    )(w_in_b, w_sq_b)


def _place_own(idx, w_in_b, w_sq_b, w4, wsq):
    n = 4
    r_in, r_sq = D_MODEL // n, ROW_SHARD // n

    def body(idx_ref, win_ref, wsq_ref, w4_in, wsq_in, w4_out, wsq_out):
        w4_out[0] = win_ref[...]
        wsq_out[:, 0] = wsq_ref[...]

    grid_spec = pltpu.PrefetchScalarGridSpec(
        num_scalar_prefetch=1,
        grid=(n,),
        in_specs=[pl.BlockSpec((r_in, W_IN_SHARD), lambda r, idx: (r, 0)),
                  pl.BlockSpec((3, r_sq, D_MODEL), lambda r, idx: (0, r, 0)), ANY, ANY],
        out_specs=[pl.BlockSpec((1, r_in, W_IN_SHARD), lambda r, idx: (idx[0], r, 0)),
                   pl.BlockSpec((3, 1, r_sq, D_MODEL), lambda r, idx: (0, idx[0], r, 0))],
    )
    return pl.pallas_call(
        body,
        name="place_own",
        grid_spec=grid_spec,
        out_shape=[jax.ShapeDtypeStruct(w4.shape, BF16), jax.ShapeDtypeStruct(wsq.shape, BF16)],
        input_output_aliases={3: 0, 4: 1},
        compiler_params=_cparams(("arbitrary",)),
    )(idx, w_in_b, w_sq_b, w4, wsq)


def _swap_halves(g_in, g_sq):
    n_in = 16
    n_piece = n_in + 3 * N_CHIPS
    rows = HALF_IN // n_in

    def body(gin_ref, gsq_ref, got_in, got_sq, send_sems, recv_sems):
        x, y, c, _ = _position()
        sibling = (x, y, 1 - c)

        def src_piece(p):
            if p < n_in:
                return gin_ref.at[pl.ds((1 - c) * HALF_IN + p * rows, rows), :]
            a, chip = divmod(p - n_in, N_CHIPS)
            return gsq_ref.at[a, chip, pl.ds((1 - c) * HALF_SQ, HALF_SQ), :]

        def dst_piece(p):
            if p < n_in:
                return got_in.at[pl.ds(p * rows, rows), :]
            a, chip = divmod(p - n_in, N_CHIPS)
            return got_sq.at[a, chip]

        out = [_remote(src_piece(p), dst_piece(p), send_sems.at[p], recv_sems.at[p], sibling) for p in range(n_piece)]
        for cp in out:
            cp.start()
        for cp in out:
            cp.wait()

    return pl.pallas_call(
        body,
        name="swap_halves",
        in_specs=[ANY, ANY],
        out_specs=[ANY, ANY],
        out_shape=[jax.ShapeDtypeStruct((HALF_IN, IN_WIDTH), F32),
                   jax.ShapeDtypeStruct((3, N_CHIPS, HALF_SQ, D_MODEL), F32)],
        scratch_shapes=[pltpu.SemaphoreType.DMA((n_piece,))] * 2,
    )(g_in, g_sq)


def _join_halves(r_in, r_sq):
    n_in = 16
    n_piece = n_in + 3
    rows = HALF_IN // n_in

    def body(in_alias, sq_alias, full_in, full_sq, send_sems, recv_sems):
        del in_alias, sq_alias
        x, y, c, _ = _position()
        sibling = (x, y, 1 - c)

        def piece(p, core):
            if p < n_in:
                return full_in.at[pl.ds(core * HALF_IN + p * rows, rows), :]
            return full_sq.at[p - n_in, pl.ds(core * HALF_SQ, HALF_SQ), :]

        out = [_remote(piece(p, c), piece(p, c), send_sems.at[p], recv_sems.at[p], sibling) for p in range(n_piece)]
        for cp in out:
            cp.start()
        for p in range(n_piece):
            _remote(piece(p, 1 - c), piece(p, 1 - c), send_sems.at[p], recv_sems.at[p], sibling).wait_recv()
        for cp in out:
            cp.wait_send()

    return pl.pallas_call(
        body,
        name="join_halves",
        in_specs=[ANY, ANY],
        out_specs=[ANY, ANY],
        out_shape=[jax.ShapeDtypeStruct((D_MODEL, W_IN_SHARD), F32),
                   jax.ShapeDtypeStruct((3, ROW_SHARD, D_MODEL), F32)],
        input_output_aliases={0: 0, 1: 1},
        scratch_shapes=[pltpu.SemaphoreType.DMA((n_piece,)), pltpu.SemaphoreType.DMA((n_piece,))],
    )(r_in, r_sq)


SMALL_ROWS = 56
N_DEV = 8


def _sum_small(part):
    def body(part_ref, out_ref, slots, send_sems, recv_sems):
        x, y, c, _ = _position()
        me = 4 * x + 2 * y + c
        slots[me] = part_ref[...]
        out = []
        for r in range(1, N_DEV):
            rx, ry, rc = (r >> 2) & 1, (r >> 1) & 1, r & 1
            to = (1 - x if rx else x, 1 - y if ry else y, 1 - c if rc else c)
            out.append(_remote(part_ref, slots.at[me], send_sems.at[r - 1], recv_sems.at[r - 1], to))
        for cp in out:
            cp.start()
        for r in range(1, N_DEV):
            _remote(part_ref, slots.at[me ^ r], send_sems.at[r - 1], recv_sems.at[r - 1], (x, y, c)).wait_recv()
        for cp in out:
            cp.wait_send()
        total = slots[0]
        for d in range(1, N_DEV):
            total = total + slots[d]
        out_ref[...] = total

    vmem = pl.BlockSpec(memory_space=pltpu.VMEM)
    return pl.pallas_call(
        body,
        name="sum_small",
        in_specs=[vmem],
        out_specs=vmem,
        out_shape=jax.ShapeDtypeStruct((SMALL_ROWS, HEAD_DIM), F32),
        scratch_shapes=[pltpu.VMEM((N_DEV, SMALL_ROWS, HEAD_DIM), F32),
                        pltpu.SemaphoreType.DMA((N_DEV - 1,)), pltpu.SemaphoreType.DMA((N_DEV - 1,))],
    )(part)


def _prefetch_call(body, name, idx, grid, in_specs, out_specs, out_shape, args):
    grid_spec = pltpu.PrefetchScalarGridSpec(num_scalar_prefetch=1, grid=grid, in_specs=in_specs, out_specs=out_specs)
    return pl.pallas_call(body, name=name, grid_spec=grid_spec, out_shape=out_shape,
                          compiler_params=_cparams(("arbitrary",) * len(grid)))(idx, *args)


def _sum_a_in(idx, g_in, got_in):
    tr = 128
    nr = HALF_IN // tr

    def body(idx_ref, a_ref, b_ref, o_ref):
        o_ref[0] = (a_ref[...] + b_ref[...]).astype(WIRE)

    return _prefetch_call(
        body, "sum_a_in", idx, (N_CHIPS, nr),
        [pl.BlockSpec((tr, W_IN_SHARD), lambda j, r, idx: (idx[1] * nr + r, j)),
         pl.BlockSpec((tr, W_IN_SHARD), lambda j, r, idx: (r, j))],
        pl.BlockSpec((1, tr, W_IN_SHARD), lambda j, r, idx: (j, r, 0)),
        jax.ShapeDtypeStruct((N_CHIPS, HALF_IN, W_IN_SHARD), WIRE), (g_in, got_in))


def _sum_a_sq(idx, g_sq, got_sq):
    blk = (1, 1, HALF_SQ, D_MODEL)

    def body(idx_ref, a_ref, b_ref, o_ref):
        o_ref[...] = (a_ref[...] + b_ref[...]).astype(WIRE)

    return _prefetch_call(
        body, "sum_a_sq", idx, (3, N_CHIPS),
        [pl.BlockSpec(blk, lambda a, j, idx: (a, j, idx[1], 0)), pl.BlockSpec(blk, lambda a, j, idx: (a, j, 0, 0))],
        pl.BlockSpec(blk, lambda a, j, idx: (a, j, 0, 0)),
        jax.ShapeDtypeStruct((3, N_CHIPS, HALF_SQ, D_MODEL), WIRE), (g_sq, got_sq))


def _sum_b_in(idx, s_in, got_in):
    tr = 128
    nr = HALF_IN // tr

    def body(idx_ref, a_ref, b_ref, o_ref):
        o_ref[...] = ((a_ref[0].astype(F32) + b_ref[0].astype(F32)) + b_ref[1].astype(F32)) + b_ref[2].astype(F32)

    return _prefetch_call(
        body, "sum_b_in", idx, (nr,),
        [pl.BlockSpec((1, tr, W_IN_SHARD), lambda r, idx: (idx[0], r, 0)),
         pl.BlockSpec((3, tr, W_IN_SHARD), lambda r, idx: (0, r, 0))],
        pl.BlockSpec((tr, W_IN_SHARD), lambda r, idx: (idx[1] * nr + r, 0)),
        jax.ShapeDtypeStruct((D_MODEL, W_IN_SHARD), F32), (s_in, got_in))


def _sum_b_sq(idx, s_sq, got_sq):
    def body(idx_ref, a_ref, b_ref, o_ref):
        o_ref[0] = ((a_ref[0, 0].astype(F32) + b_ref[0, 0].astype(F32)) + b_ref[1, 0].astype(F32)) + b_ref[2, 0].astype(F32)

    return _prefetch_call(
        body, "sum_b_sq", idx, (3,),
        [pl.BlockSpec((1, 1, HALF_SQ, D_MODEL), lambda a, idx: (a, idx[0], 0, 0)),
         pl.BlockSpec((3, 1, HALF_SQ, D_MODEL), lambda a, idx: (0, a, 0, 0))],
        pl.BlockSpec((1, HALF_SQ, D_MODEL), lambda a, idx: (a, idx[1], 0)),
        jax.ShapeDtypeStruct((3, ROW_SHARD, D_MODEL), F32), (s_sq, got_sq))


def _adamw_math(w, g, m, v):
    m = ADAM_B1 * m + (1.0 - ADAM_B1) * g
    v = ADAM_B2 * v + (1.0 - ADAM_B2) * (g * g)
    m_hat = m / (1.0 - ADAM_B1 ** ADAM_STEP)
    v_hat = v / (1.0 - ADAM_B2 ** ADAM_STEP)
    delta = -ADAM_LR * (m_hat / (jnp.sqrt(v_hat) + ADAM_EPS) + ADAM_WD * w)
    return delta, m, v


def _adamw(w, g, m, v, name):
    rows, cols = w.shape
    tr = min(128, rows)

    def body(w_ref, g_ref, m_ref, v_ref, d_ref, nm_ref, nv_ref):
        d_ref[...], nm_ref[...], nv_ref[...] = _adamw_math(w_ref[...], g_ref[...], m_ref[...], v_ref[...])

    spec = pl.BlockSpec((tr, cols), lambda r: (r, 0))
    return pl.pallas_call(
        body,
        name=name,
        grid=(rows // tr,),
        in_specs=[spec] * 4,
        out_specs=[spec] * 3,
        out_shape=[jax.ShapeDtypeStruct((rows, cols), F32)] * 3,
        compiler_params=_cparams(("arbitrary",)),
    )(w, g, m, v)


def _adamw_small(sums, w, m, v):
    def body(s_ref, w_ref, m_ref, v_ref, loss_ref, g_ref, d_ref, nm_ref, nv_ref):
        s = s_ref[...]
        w = w_ref[...]
        loss_ref[...] = s[0:1, 0:1]
        l0, l1 = w[24:32], w[32:40]
        mx = jnp.maximum(l0, l1)
        e0, e1 = jnp.exp(l0 - mx), jnp.exp(l1 - mx)
        p0, p1 = e0 / (e0 + e1), e1 / (e0 + e1)
        d_lb = s[32:40]
        g = jnp.concatenate([s[8:16], s[16:32], d_lb * p0 * (1.0 - p0), -d_lb * p0 * p1, s[40:48], s[48:56]], axis=0)
        g_ref[...] = g
        d_ref[...], nm_ref[...], nv_ref[...] = _adamw_math(w, g, m_ref[...], v_ref[...])

    packed = jax.ShapeDtypeStruct((SMALL_ROWS, HEAD_DIM), F32)
    return pl.pallas_call(
        body,
        name="adamw_small",
        out_shape=[jax.ShapeDtypeStruct((1, 1), F32), packed, packed, packed, packed],
    )(sums, w, m, v)


def _pack_small(ng, bg, lbl, hgn, fg):
    return jnp.concatenate([a.reshape(-1, HEAD_DIM) for a in (ng, bg, lbl, hgn, fg)], axis=0)


def _unpack_small(p):
    return (p[0:8].reshape(1, D_MODEL), p[8:24].reshape(1, 2 * D_MODEL), p[24:40].reshape(2, HEADS, HEAD_DIM),
            p[40:48].reshape(1, HEADS, HEAD_DIM), p[48:56].reshape(D_MODEL))


def kernel(x, norm_g, w_in, b_gate, lb_logits, hg_norm_g, w_sb_proj, w_hg_proj, w_out, final_norm_g, loss_target, m_norm_g, m_w_in, m_b_gate, m_lb_logits, m_hg_norm_g, m_w_sb_proj, m_w_hg_proj, m_w_out, m_final_norm_g, v_norm_g, v_w_in, v_b_gate, v_lb_logits, v_hg_norm_g, v_w_sb_proj, v_w_hg_proj, v_w_out, v_final_norm_g):
    s_len = x.shape[1]
    w_sq = jnp.stack([w_sb_proj[0], w_hg_proj[0], w_out[0]])
    idx = jnp.stack([2 * lax.axis_index("x") + lax.axis_index("y"), lax.axis_index("c")]).astype(jnp.int32)
    w_in_b, w_sq_b = w_in[0].astype(BF16), w_sq.astype(BF16)
    w4, wsq = _place_own(idx, w_in_b, w_sq_b, *_gather_weights(w_in_b, w_sq_b))
    wsq = wsq.reshape(3, D_MODEL, D_MODEL)

    (g_in, g_sb, g_hg, g_out, segs, dout, loss, d_bg, d_lb, d_hgn, d_fg) = _local_grads(
        x[0], loss_target[0], norm_g, b_gate, lb_logits.reshape(2, D_MODEL), hg_norm_g.reshape(1, D_MODEL),
        final_norm_g.reshape(1, D_MODEL), w4, wsq[0], wsq[1], wsq[2])

    g_sq = jnp.stack([g_sb, g_hg, g_out]).reshape(3, N_CHIPS, ROW_SHARD, D_MODEL)
    got_in, got_sq = _swap_halves(g_in, g_sq)
    s_in, s_sq = _sum_a_in(idx, g_in, got_in), _sum_a_sq(idx, g_sq, got_sq)
    grad_x, d_ng, got_in, got_sq = _dx(segs, w4, x[0], norm_g, dout, s_in, s_sq)
    grad_in, grad_sq = _join_halves(_sum_b_in(idx, s_in, got_in), _sum_b_sq(idx, s_sq, got_sq))

    d_in, nm_in, nv_in = _adamw(w_in[0], grad_in, m_w_in[0], v_w_in[0], "adamw_in")
    flat = lambda a, b, c: jnp.concatenate([a[0], b[0], c[0]], axis=0)
    d_sq, nm_sq, nv_sq = _adamw(flat(w_sb_proj, w_hg_proj, w_out), grad_sq.reshape(3 * ROW_SHARD, D_MODEL),
                                flat(m_w_sb_proj, m_w_hg_proj, m_w_out), flat(v_w_sb_proj, v_w_hg_proj, v_w_out),
                                "adamw_sq")

    pad = jnp.zeros((8, HEAD_DIM), F32).at[0, 0].set(loss[0, 0])
    part = jnp.concatenate([pad] + [a.reshape(-1, HEAD_DIM) for a in (d_ng, d_bg, d_lb, d_hgn, d_fg)], axis=0)
    sums = _sum_small(part)
    loss_out, g_sm, d_sm, nm_sm, nv_sm = _adamw_small(
        sums, _pack_small(norm_g, b_gate, lb_logits, hg_norm_g, final_norm_g),
        _pack_small(m_norm_g, m_b_gate, m_lb_logits, m_hg_norm_g, m_final_norm_g),
        _pack_small(v_norm_g, v_b_gate, v_lb_logits, v_hg_norm_g, v_final_norm_g))

    def big(t_in, t_sq):
        sq = t_sq.reshape(3, 1, ROW_SHARD, D_MODEL)
        return t_in[None], sq[0], sq[1], sq[2]

    def order(small, in_, sb, hg, out):
        ng, bg, lbl, hgn, fg = small
        return [ng, in_, bg, lbl, hgn, sb, hg, out, fg]

    outs = [loss_out[0, 0], grad_x[None]]
    for small, (t_in, t_sq) in ((g_sm, (grad_in, grad_sq)), (d_sm, (d_in, d_sq)), (nm_sm, (nm_in, nm_sq)), (nv_sm, (nv_in, nv_sq))):
        outs += order(_unpack_small(small), *big(t_in, t_sq))
    return tuple(outs)
```

```python
import functools

import jax
import jax.numpy as jnp
from jax import lax
from jax.experimental import pallas as pl
from jax.experimental.pallas import tpu as pltpu

F32 = jnp.float32
BF16 = jnp.bfloat16

D_MODEL = 1024
HEADS = 8
HEAD_DIM = 128
IN_WIDTH = 10240
N_CHIPS = 4
W_IN_SHARD = IN_WIDTH // N_CHIPS
ROW_SHARD = D_MODEL // N_CHIPS
RMS_EPS = 1e-6

OFF_SB_Q, OFF_SB_K, OFF_SB_V, OFF_SB_Z = 0, 1024, 2048, 3072
OFF_HG_Q, OFF_HG_F, OFF_HG_I, OFF_HG_Z, OFF_GATE = 4096, 5120, 6144, 7168, 8192

QKV_COLS = 3840
SB_BLOCK = 256
SB_FWD_HEADS = 4
SB_BWD_HEADS = 2
SB_ROWS = 256
SB_DEAD = -110.0
SB_GONE = -1e30
HG_CHUNK = 32
HG_PAIR = 2 * HG_CHUNK
HG_STEP = 256
HG_MID = HG_CHUNK // 2 - 1

ADAM_LR, ADAM_B1, ADAM_B2, ADAM_EPS, ADAM_WD, ADAM_STEP = 0.001, 0.9, 0.999, 1e-08, 0.01, 10

VMEM_LIMIT = 56 * 1024 * 1024

MESH = pl.DeviceIdType.MESH


def _cparams(sem, vmem=VMEM_LIMIT):
    return pltpu.CompilerParams(dimension_semantics=sem, vmem_limit_bytes=vmem)


def _dot(a, b):
    return jnp.dot(a, b, preferred_element_type=F32)


def _dot_nt(a, b):
    return lax.dot_general(a, b, (((1,), (1,)), ((), ())), preferred_element_type=F32)


def _dot_tn(a, b):
    return lax.dot_general(a, b, (((0,), (0,)), ((), ())), preferred_element_type=F32)


def _split_dot(x, tri):
    hi = x.astype(BF16)
    lo = (x - hi.astype(F32)).astype(BF16)
    both = _dot(jnp.concatenate([hi, lo], axis=0), tri)
    return both[: x.shape[0]] + both[x.shape[0] :]


def _split_dot_left(tri, x):
    hi = x.astype(BF16)
    lo = (x - hi.astype(F32)).astype(BF16)
    return _dot(tri, hi) + _dot(tri, lo)


def _sigmoid(x):
    return 1.0 / (1.0 + jnp.exp(-x))


def _inproj(x, norm_g, w_all):
    s_len = x.shape[0]
    ts = min(1024, s_len)
    tn = QKV_COLS // 3

    def body(x_ref, g_ref, w_ref, proj_ref, ht_ref, qkv_ref, h_scr):
        n = pl.program_id(1)

        @pl.when(n == 0)
        def _():
            xv = x_ref[...]
            r = lax.rsqrt(jnp.mean(xv * xv, axis=-1, keepdims=True) + RMS_EPS)
            hv = (xv * r) * g_ref[...]
            h_scr[...] = hv.astype(BF16)
            ht_ref[...] = hv.T.astype(BF16)

        p = _dot(h_scr[...], w_ref[...])
        proj_ref[...] = p

        @pl.when(n < 3)
        def _():
            qkv_ref[...] = p.astype(BF16)

    return pl.pallas_call(
        body,
        name="inproj",
        grid=(s_len // ts, IN_WIDTH // tn),
        in_specs=[
            pl.BlockSpec((ts, D_MODEL), lambda s, n: (s, 0)),
            pl.BlockSpec((1, D_MODEL), lambda s, n: (0, 0)),
            pl.BlockSpec((D_MODEL, tn), lambda s, n: (0, n)),
        ],
        out_specs=[
            pl.BlockSpec((ts, tn), lambda s, n: (s, n)),
            pl.BlockSpec((D_MODEL, ts), lambda s, n: (0, s)),
            pl.BlockSpec((ts, tn), lambda s, n: (s, jnp.minimum(n, 2))),
        ],
        out_shape=[
            jax.ShapeDtypeStruct((s_len, IN_WIDTH), F32),
            jax.ShapeDtypeStruct((D_MODEL, s_len), BF16),
            jax.ShapeDtypeStruct((s_len, QKV_COLS), BF16),
        ],
        scratch_shapes=[pltpu.VMEM((ts, D_MODEL), BF16)],
        compiler_params=_cparams(("arbitrary", "arbitrary")),
    )(x, norm_g, w_all)


def _sb_scores(qb, kb, causal, tri_excl, diag):
    z = _dot_nt(qb, kb) * HEAD_DIM ** -0.5
    ls_pos = jnp.minimum(z, 0.0) - jnp.log1p(jnp.exp(-jnp.abs(z)))
    log_not = ls_pos - z
    log_not_m = jnp.where(causal, log_not, 0.0) if diag else log_not
    return ls_pos, log_not, log_not_m, _split_dot(log_not_m, tri_excl)


def _sb_weights(ls_pos, suffix, carry, causal, diag):
    surv = suffix + carry
    w = jnp.exp(ls_pos + surv)
    return surv, (jnp.where(causal, w, 0.0) if diag else w)


def _sb_specs(s_len, blk, heads):
    width = heads * HEAD_DIM

    def blk_spec(off):
        return pl.BlockSpec((blk, width), lambda h, i: (i, off // width + h))

    def head_spec(off, buffers=2):
        return pl.BlockSpec((s_len, width), lambda h, i: (0, off // width + h), pipeline_mode=pl.Buffered(buffers))

    return blk_spec, head_spec


def _head_cols(p):
    return slice(p * HEAD_DIM, (p + 1) * HEAD_DIM)


def _sb_chains(blk, heads):
    rows = min(SB_ROWS, blk)
    return [(p, a) for p in range(heads) for a in range(blk // rows)], rows


def _sb_masks(blk, rows):
    row = lax.broadcasted_iota(jnp.int32, (rows, blk), 0)
    col = lax.broadcasted_iota(jnp.int32, (rows, blk), 1)
    causal = [row + a * rows > col for a in range(blk // rows)]
    row = lax.broadcasted_iota(jnp.int32, (blk, blk), 0)
    col = lax.broadcasted_iota(jnp.int32, (blk, blk), 1)
    tri_excl = (row > col).astype(BF16)
    tri_incl = (row >= col).astype(BF16)
    return causal, tri_excl, tri_incl


def _sb_alive(st, n_chain):
    alive = functools.reduce(jnp.maximum, [st[1 + 3 * c] for c in range(n_chain)])
    return jnp.max(alive) > SB_DEAD


def _sb_fwd(qkv):
    s_len = qkv.shape[0]
    blk = min(SB_BLOCK, s_len)
    nq = s_len // blk
    chains, rows = _sb_chains(blk, SB_FWD_HEADS)

    def body(q_ref, k_ref, v_ref, o_ref, of_ref):
        i = pl.program_id(1)
        causal, tri_excl, _ = _sb_masks(blk, rows)

        def tiles(specs, st):
            pre = []
            for j, diag, _ in specs:
                start = pl.multiple_of(j * blk, blk)
                for p, a in chains:
                    kb = k_ref[pl.ds(start, blk), _head_cols(p)]
                    qb = q_ref[a * rows : (a + 1) * rows, _head_cols(p)]
                    pre.append(_sb_scores(qb, kb, causal[a], tri_excl, diag) + (v_ref[pl.ds(start, blk), _head_cols(p)],))
            for t, (j, diag, valid) in enumerate(specs):
                new = []
                for c, (p, a) in enumerate(chains):
                    carry, acc, acc_lo = st[3 * c : 3 * c + 3]
                    if valid is not None:
                        carry = jnp.where(valid, carry, SB_GONE)
                    ls_pos, _, log_not_m, suffix, vb = pre[t * len(chains) + c]
                    surv, w = _sb_weights(ls_pos, suffix, carry, causal[a], diag)
                    wb = w.astype(BF16)
                    w_lo = (w - wb.astype(F32)).astype(BF16)
                    both = _dot(jnp.concatenate([wb, w_lo], axis=0), vb)
                    new += [surv[:, 0:1] + log_not_m[:, 0:1], acc + both[:rows], acc_lo + both[rows:]]
                st = tuple(new)
            return st

        zero = jnp.zeros((rows, HEAD_DIM), F32)
        st = tiles([(i, True, None), (jnp.maximum(i - 1, 0), False, i >= 1)],
                   (jnp.zeros((rows, 1), F32), zero, zero) * len(chains))

        def more(st):
            return (st[0] < i) & _sb_alive(st, len(chains))

        def step(st):
            return (st[0] + 1,) + tiles([(i - 1 - st[0], False, None)], st[1:])

        st = lax.while_loop(more, step, (1,) + st)[1:]
        for c, (p, a) in enumerate(chains):
            o_ref[a * rows : (a + 1) * rows, _head_cols(p)] = st[3 * c + 1]
            of_ref[a * rows : (a + 1) * rows, _head_cols(p)] = st[3 * c + 1] + st[3 * c + 2]

    blk_spec, head_spec = _sb_specs(s_len, blk, SB_FWD_HEADS)
    return pl.pallas_call(
        body,
        name="sb_fwd",
        grid=(HEADS // SB_FWD_HEADS, nq),
        in_specs=[blk_spec(OFF_SB_Q), head_spec(OFF_SB_K), head_spec(OFF_SB_V)],
        out_specs=[blk_spec(0), blk_spec(0)],
        out_shape=[jax.ShapeDtypeStruct((s_len, D_MODEL), F32)] * 2,
        compiler_params=_cparams(("arbitrary", "arbitrary")),
    )(qkv, qkv, qkv)


def _sb_bwd(qkv, o_fine, d_o):
    s_len = qkv.shape[0]
    blk = min(SB_BLOCK, s_len)
    nq = s_len // blk
    scale = HEAD_DIM ** -0.5
    chains, rows = _sb_chains(blk, SB_BWD_HEADS)

    def body(q_ref, k_ref, v_ref, of_ref, do_ref, dq_ref, dk_ref, dv_ref, dk_acc, dv_acc):
        i = pl.program_id(1)

        @pl.when(i == 0)
        def _():
            dk_acc[...] = jnp.zeros_like(dk_acc)
            dv_acc[...] = jnp.zeros_like(dv_acc)

        dob = do_ref[...].astype(BF16)
        prod = dob.astype(F32) * of_ref[...]
        causal, tri_excl, tri_incl = _sb_masks(blk, rows)

        def group(x, p, a):
            return x[a * rows : (a + 1) * rows, _head_cols(p)]

        totals = [jnp.sum(group(prod, p, a), axis=-1, keepdims=True) for p, a in chains]

        def tiles(specs, st):
            pre = []
            for j, diag, _ in specs:
                start = pl.multiple_of(j * blk, blk)
                for p, a in chains:
                    kb = k_ref[pl.ds(start, blk), _head_cols(p)]
                    vb = v_ref[pl.ds(start, blk), _head_cols(p)]
                    qb, dob_c = group(q_ref, p, a), group(dob, p, a)
                    pre.append(_sb_scores(qb, kb, causal[a], tri_excl, diag) + (_dot_nt(dob_c, vb), qb, kb, dob_c))
            for t, (j, diag, valid) in enumerate(specs):
                start = pl.multiple_of(j * blk, blk)
                mids = []
                for c, (p, a) in enumerate(chains):
                    c_not = st[3 * c]
                    if valid is not None:
                        c_not = jnp.where(valid, c_not, SB_GONE)
                    ls_pos, _, _, suffix, d_w = pre[t * len(chains) + c][:5]
                    surv, w = _sb_weights(ls_pos, suffix, c_not, causal[a], diag)
                    dlw = d_w * w
                    mids.append((surv, w, dlw, _split_dot(dlw, tri_incl)))
                new = []
                dk_new = [None] * SB_BWD_HEADS
                dv_new = [None] * SB_BWD_HEADS
                for c, (p, a) in enumerate(chains):
                    c_dlw, dq = st[3 * c + 1 : 3 * c + 3]
                    ls_pos, log_not, log_not_m, _, _, qb, kb, dob_c = pre[t * len(chains) + c]
                    surv, w, dlw, suffix = mids[c]
                    d_not = totals[c] - c_dlw - suffix
                    dz = (dlw * jnp.exp(log_not) - d_not * jnp.exp(ls_pos)) * scale
                    if diag:
                        dz = jnp.where(causal[a], dz, 0.0)
                    if valid is not None:
                        dz = jnp.where(valid, dz, 0.0)
                    dzb = dz.astype(BF16)
                    dk_c, dv_c = _dot_tn(dzb, qb), _dot_tn(w.astype(BF16), dob_c)
                    dk_new[p] = dk_c if dk_new[p] is None else dk_new[p] + dk_c
                    dv_new[p] = dv_c if dv_new[p] is None else dv_new[p] + dv_c
                    new += [surv[:, 0:1] + log_not_m[:, 0:1], c_dlw + suffix[:, 0:1], dq + _dot(dzb, kb)]
                for p in range(SB_BWD_HEADS):
                    dk_acc[pl.ds(start, blk), _head_cols(p)] += dk_new[p]
                    dv_acc[pl.ds(start, blk), _head_cols(p)] += dv_new[p]
                st = tuple(new)
            return st

        zcol = jnp.zeros((rows, 1), F32)
        st = tiles([(i, True, None), (jnp.maximum(i - 1, 0), False, i >= 1)],
                   (zcol, zcol, jnp.zeros((rows, HEAD_DIM), F32)) * len(chains))

        def more(st):
            return (st[0] < i) & _sb_alive(st, len(chains))

        def step(st):
            return (st[0] + 1,) + tiles([(i - 1 - st[0], False, None)], st[1:])

        st = lax.while_loop(more, step, (1,) + st)[1:]
        for c, (p, a) in enumerate(chains):
            dq_ref[a * rows : (a + 1) * rows, _head_cols(p)] = st[3 * c + 2].astype(BF16)

        @pl.when(i == nq - 1)
        def _():
            dk_ref[...] = dk_acc[...].astype(BF16)
            dv_ref[...] = dv_acc[...].astype(BF16)

    blk_spec, head_spec = _sb_specs(s_len, blk, SB_BWD_HEADS)
    width = SB_BWD_HEADS * HEAD_DIM
    return pl.pallas_call(
        body,
        name="sb_bwd",
        grid=(HEADS // SB_BWD_HEADS, nq),
        in_specs=[blk_spec(OFF_SB_Q), head_spec(OFF_SB_K, 1), head_spec(OFF_SB_V, 1), blk_spec(0), blk_spec(0)],
        out_specs=[blk_spec(0), head_spec(0), head_spec(0)],
        out_shape=[jax.ShapeDtypeStruct((s_len, D_MODEL), BF16)] * 3,
        scratch_shapes=[pltpu.VMEM((s_len, width), F32), pltpu.VMEM((s_len, width), F32)],
        compiler_params=_cparams(("arbitrary", "arbitrary")),
    )(qkv, qkv, qkv, o_fine, d_o)


def _hg_lower_bound(lbl_ref):
    l0 = lbl_ref[0:1, :]
    l1 = lbl_ref[1:2, :]
    mx = jnp.maximum(l0, l1)
    e0 = jnp.exp(l0 - mx)
    e1 = jnp.exp(l1 - mx)
    return e0 / (e0 + e1)


def _hg_gates(hq, hf, lb):
    sig_f = _sigmoid(hf)
    f = lb + (1.0 - lb) * sig_f
    g = jnp.log(f)
    kk = 1.0 - f
    sig_q = _sigmoid(hq)
    qq = hq * sig_q
    return qq, kk, g, f, sig_f, sig_q


def _period_bcast(x, r, rows, period):
    w = x.shape[-1]
    x3 = x.reshape(rows // period, period, w)
    return jnp.broadcast_to(x3[:, r : r + 1, :], x3.shape).reshape(rows, w)


def _blockdiag(rows, kind):
    row = lax.broadcasted_iota(jnp.int32, (rows, rows), 0)
    col = lax.broadcasted_iota(jnp.int32, (rows, rows), 1)
    if kind in ("next", "prev"):
        first, second = (row, col) if kind == "next" else (col, row)
        keep = ((row // HG_PAIR) == (col // HG_PAIR)) & (first % HG_PAIR < HG_CHUNK) & (second % HG_PAIR >= HG_CHUNK)
    else:
        keep = (row // HG_CHUNK) == (col // HG_CHUNK)
        if kind == "lower":
            keep = keep & (row >= col)
        elif kind == "upper":
            keep = keep & (row <= col)
    return jnp.where(keep, 1.0, 0.0).astype(BF16)


def _hg_operands(hq, hf, lb, rows):
    qq, kk, g, f, sig_f, sig_q = _hg_gates(hq, hf, lb)
    cum = _split_dot_left(_blockdiag(rows, "lower"), g)
    mid = _period_bcast(cum, HG_MID, rows, HG_CHUNK)
    last = _period_bcast(cum, HG_CHUNK - 1, rows, HG_CHUNK)
    last0 = _period_bcast(cum, HG_CHUNK - 1, rows, HG_PAIR)
    last1 = _period_bcast(cum, HG_PAIR - 1, rows, HG_PAIR)
    second = (lax.broadcasted_iota(jnp.int32, cum.shape, 0) % HG_PAIR) >= HG_CHUNK
    e = dict(qm=jnp.exp(cum - mid), km=jnp.exp(mid - cum), qd=jnp.exp(cum), kl=jnp.exp(last - cum),
             q_in=jnp.where(second, jnp.exp(last0), 1.0), k_out=jnp.where(second, 1.0, jnp.exp(last1)),
             pair=jnp.exp(last0 + last1))
    v = dict(qm=qq * e["qm"], km=kk * e["km"], qd=qq * e["qd"], kl=kk * e["kl"])
    v["qp"] = v["qd"] * e["q_in"]
    v["kp"] = v["kl"] * e["k_out"]
    return v, e, second, (f, sig_f, sig_q)


def _hg_store_operands(v, second, hi, refs):
    zero = jnp.zeros_like(v["qm"])
    q_cat, k_cat, qp_b, kp_b, v_b = refs
    q_cat[:, 0:D_MODEL] = jnp.where(second, zero, v["qm"]).astype(BF16)
    q_cat[:, D_MODEL : 2 * D_MODEL] = jnp.where(second, v["qm"], zero).astype(BF16)
    q_cat[:, 2 * D_MODEL :] = jnp.where(second, v["qd"], zero).astype(BF16)
    k_cat[:, 0:D_MODEL] = jnp.where(second, zero, v["km"]).astype(BF16)
    k_cat[:, D_MODEL : 2 * D_MODEL] = jnp.where(second, v["km"], zero).astype(BF16)
    k_cat[:, 2 * D_MODEL :] = jnp.where(second, zero, v["kl"]).astype(BF16)
    qp_b[...] = v["qp"].astype(BF16)
    kp_b[...] = v["kp"].astype(BF16)
    v_b[...] = hi.astype(BF16)


def _hg_pair_operands(cat, r0, c0):
    return jnp.concatenate([cat[r0 : r0 + HG_PAIR, g * D_MODEL + c0 : g * D_MODEL + c0 + HEAD_DIM] for g in range(3)], axis=1)


def _hg_fwd(proj, lbl):
    s_len = proj.shape[0]
    rows = min(HG_STEP, s_len)
    n_pairs = rows // HG_PAIR

    def body(hq_ref, hf_ref, hi_ref, lbl_ref, o_ref, st_ref, state, q_cat, k_cat, qp_b, kp_b, v_b):
        @pl.when(pl.program_id(0) == 0)
        def _():
            state[...] = jnp.zeros_like(state)

        v, e, second, _ = _hg_operands(hq_ref[...], hf_ref[...], _hg_lower_bound(lbl_ref), rows)
        _hg_store_operands(v, second, hi_ref[...], (q_cat, k_cat, qp_b, kp_b, v_b))
        e_pair = e["pair"]
        row = lax.broadcasted_iota(jnp.int32, (HG_PAIR, HG_PAIR), 0)
        col = lax.broadcasted_iota(jnp.int32, (HG_PAIR, HG_PAIR), 1)
        causal = row >= col

        for u in range(n_pairs):
            r0 = u * HG_PAIR
            sls = [(slice(r0, r0 + HG_PAIR), slice(h * HEAD_DIM, (h + 1) * HEAD_DIM)) for h in range(HEADS)]
            a_s = [jnp.where(causal, _dot_nt(_hg_pair_operands(q_cat, r0, h * HEAD_DIM),
                                             _hg_pair_operands(k_cat, r0, h * HEAD_DIM)), 0.0).astype(BF16)
                   for h in range(HEADS)]
            st_s = [state[h] for h in range(HEADS)]
            for h, sl in enumerate(sls):
                st_ref[u, h] = st_s[h]
                state[h] = st_s[h] * e_pair[r0 : r0 + 1, sl[1]] + _dot_tn(v_b[sl], kp_b[sl])
            for h, sl in enumerate(sls):
                o_ref[sl] = _dot(a_s[h], v_b[sl]) + _dot_nt(qp_b[sl], st_s[h].astype(BF16))

    def col_spec(off):
        return pl.BlockSpec((rows, D_MODEL), lambda s: (s, off // D_MODEL))

    bf_tile = pltpu.VMEM((rows, D_MODEL), BF16)
    bf_cat = pltpu.VMEM((rows, 3 * D_MODEL), BF16)
    scratch = [pltpu.VMEM((HEADS, HEAD_DIM, HEAD_DIM), F32), bf_cat, bf_cat, bf_tile, bf_tile, bf_tile]
    return pl.pallas_call(
        body,
        name="hg_fwd",
        grid=(s_len // rows,),
        in_specs=[col_spec(OFF_HG_Q), col_spec(OFF_HG_F), col_spec(OFF_HG_I), pl.BlockSpec((2, D_MODEL), lambda s: (0, 0))],
        out_specs=[
            pl.BlockSpec((rows, D_MODEL), lambda s: (s, 0)),
            pl.BlockSpec((n_pairs, HEADS, HEAD_DIM, HEAD_DIM), lambda s: (s, 0, 0, 0)),
        ],
        out_shape=[
            jax.ShapeDtypeStruct((s_len, D_MODEL), F32),
            jax.ShapeDtypeStruct((s_len // HG_PAIR, HEADS, HEAD_DIM, HEAD_DIM), F32),
        ],
        scratch_shapes=scratch,
        compiler_params=_cparams(("arbitrary",)),
    )(proj, proj, proj, lbl)


def _hg_bwd(proj, lbl, states, d_o):
    s_len = proj.shape[0]
    rows = min(HG_STEP, s_len)
    n_pairs = rows // HG_PAIR
    n_steps = s_len // rows

    def body(hq_ref, hf_ref, hi_ref, lbl_ref, st_ref, do_ref, dp_ref, dlb_ref,
             dstate, q_cat, k_cat, qp_b, kp_b, v_b, do_b, d_qcat, d_kcat, d_qp, d_kp, d_v, d_pair):
        @pl.when(pl.program_id(0) == 0)
        def _():
            dstate[...] = jnp.zeros_like(dstate)
            dlb_ref[...] = jnp.zeros_like(dlb_ref)

        lb = _hg_lower_bound(lbl_ref)
        hq = hq_ref[...]
        v, e, second, (f, sig_f, sig_q) = _hg_operands(hq, hf_ref[...], lb, rows)
        _hg_store_operands(v, second, hi_ref[...], (q_cat, k_cat, qp_b, kp_b, v_b))
        do_b[...] = do_ref[...].astype(BF16)
        e_pair = e["pair"]
        row = lax.broadcasted_iota(jnp.int32, (HG_PAIR, HG_PAIR), 0)
        col = lax.broadcasted_iota(jnp.int32, (HG_PAIR, HG_PAIR), 1)
        causal = row >= col

        for u in reversed(range(n_pairs)):
            r0 = u * HG_PAIR
            sls = [(slice(r0, r0 + HG_PAIR), slice(h * HEAD_DIM, (h + 1) * HEAD_DIM)) for h in range(HEADS)]
            ops = [(_hg_pair_operands(q_cat, r0, h * HEAD_DIM), _hg_pair_operands(k_cat, r0, h * HEAD_DIM))
                   for h in range(HEADS)]
            a_s = [jnp.where(causal, _dot_nt(lhs, rhs), 0.0).astype(BF16) for lhs, rhs in ops]
            da_s = [jnp.where(causal, _dot_nt(do_b[sl], v_b[sl]), 0.0).astype(BF16) for sl in sls]
            st0_s = [st_ref[u, h] for h in range(HEADS)]
            ds1_s = [dstate[h] for h in range(HEADS)]
            ds1b_s = [ds1.astype(BF16) for ds1 in ds1_s]
            for h, sl in enumerate(sls):
                decay = e_pair[r0 : r0 + 1, sl[1]]
                d_pair[u : u + 1, sl[1]] = decay * jnp.sum(ds1_s[h] * st0_s[h], axis=0, keepdims=True)
                dstate[h] = ds1_s[h] * decay + _dot_tn(do_b[sl], qp_b[sl])
            for h, sl in enumerate(sls):
                d_qp[sl] = _dot(do_b[sl], st0_s[h].astype(BF16))
                d_kp[sl] = _dot(v_b[sl], ds1b_s[h])
            for h, sl in enumerate(sls):
                d_v[sl] = _dot_tn(a_s[h], do_b[sl]) + _dot_nt(kp_b[sl], ds1b_s[h])
            for h, sl in enumerate(sls):
                d_lhs = _dot(da_s[h], ops[h][1])
                d_rhs = _dot_tn(da_s[h], ops[h][0])
                for g in range(3):
                    gsl = (sl[0], slice(g * D_MODEL + h * HEAD_DIM, g * D_MODEL + (h + 1) * HEAD_DIM))
                    d_qcat[gsl] = d_lhs[:, g * HEAD_DIM : (g + 1) * HEAD_DIM]
                    d_kcat[gsl] = d_rhs[:, g * HEAD_DIM : (g + 1) * HEAD_DIM]

        zero = jnp.zeros_like(hq)
        dqm = jnp.where(second, d_qcat[:, D_MODEL : 2 * D_MODEL], d_qcat[:, 0:D_MODEL])
        dkm = jnp.where(second, d_kcat[:, D_MODEL : 2 * D_MODEL], d_kcat[:, 0:D_MODEL])
        dqp, dkp = d_qp[...], d_kp[...]
        dqd = dqp * e["q_in"] + jnp.where(second, d_qcat[:, 2 * D_MODEL :], zero)
        dkl = dkp * e["k_out"] + jnp.where(second, zero, d_kcat[:, 2 * D_MODEL :])
        dq = dqm * e["qm"] + dqd * e["qd"]
        dk = dkm * e["km"] + dkl * e["kl"]
        t_kl = dkl * v["kl"]
        dcum = dqm * v["qm"] - dkm * v["km"] + dqd * v["qd"] - t_kl
        dp = d_pair[...]
        dp_b = jnp.broadcast_to(dp[:, None, :], (n_pairs, HG_PAIR, D_MODEL)).reshape(rows, D_MODEL)
        dg = (_split_dot_left(_blockdiag(rows, "upper"), dcum) + _split_dot_left(_blockdiag(rows, "all"), t_kl)
              + _split_dot_left(_blockdiag(rows, "next"), dqp * v["qp"])
              + _split_dot_left(_blockdiag(rows, "prev"), dkp * v["kp"]) + dp_b)
        df = dg / f - dk
        one_m = 1.0 - sig_f
        dp_ref[:, 0:D_MODEL] = (dq * (sig_q * (1.0 + hq * (1.0 - sig_q)))).astype(BF16)
        dp_ref[:, D_MODEL : 2 * D_MODEL] = (df * (1.0 - lb) * sig_f * one_m).astype(BF16)
        dp_ref[:, 2 * D_MODEL : 3 * D_MODEL] = d_v[...].astype(BF16)
        dlb_ref[...] += jnp.sum(df * one_m, axis=0, keepdims=True)

    def col_spec(off):
        return pl.BlockSpec((rows, D_MODEL), lambda s: (n_steps - 1 - s, off // D_MODEL))

    f32_tile = pltpu.VMEM((rows, D_MODEL), F32)
    f32_cat = pltpu.VMEM((rows, 3 * D_MODEL), F32)
    bf_tile = pltpu.VMEM((rows, D_MODEL), BF16)
    bf_cat = pltpu.VMEM((rows, 3 * D_MODEL), BF16)
    scratch = [pltpu.VMEM((HEADS, HEAD_DIM, HEAD_DIM), F32), bf_cat, bf_cat, bf_tile, bf_tile, bf_tile, bf_tile,
               f32_cat, f32_cat, f32_tile, f32_tile, f32_tile, pltpu.VMEM((n_pairs, D_MODEL), F32)]
    return pl.pallas_call(
        body,
        name="hg_bwd",
        grid=(n_steps,),
        in_specs=[
            col_spec(OFF_HG_Q), col_spec(OFF_HG_F), col_spec(OFF_HG_I),
            pl.BlockSpec((2, D_MODEL), lambda s: (0, 0)),
            pl.BlockSpec((n_pairs, HEADS, HEAD_DIM, HEAD_DIM), lambda s: (n_steps - 1 - s, 0, 0, 0)),
            pl.BlockSpec((rows, D_MODEL), lambda s: (n_steps - 1 - s, 0)),
        ],
        out_specs=[
            pl.BlockSpec((rows, 3 * D_MODEL), lambda s: (n_steps - 1 - s, 0)),
            pl.BlockSpec((1, D_MODEL), lambda s: (0, 0)),
        ],
        out_shape=[
            jax.ShapeDtypeStruct((s_len, 3 * D_MODEL), BF16),
            jax.ShapeDtypeStruct((1, D_MODEL), F32),
        ],
        scratch_shapes=scratch,
        compiler_params=_cparams(("arbitrary",)),
    )(proj, proj, proj, lbl, states, d_o)


def _mid(proj, sb_o, hg_o, x, target, b_gate, hg_gain, final_g, w_sb, w_hg, w_out):
    s_len = proj.shape[0]
    ts = min(256, s_len)
    inv_d = 1.0 / D_MODEL

    def body(zsb_ref, hz_ref, gl_ref, sbo_ref, hgo_ref, x_ref, tgt_ref, bg_ref, hgn_ref, fg_ref,
             wsb_ref, whg_ref, wout_ref,
             dout_ref, dsbo_ref, dhgo_ref, dzsb_ref, dhz_ref, dgl_ref,
             asb_ref, dusb_ref, ahg_ref, duhg_ref, y_ref, doutb_ref,
             loss_ref, dfg_ref, dbg_ref, dhgn_ref):
        @pl.when(pl.program_id(0) == 0)
        def _():
            loss_ref[...] = jnp.zeros_like(loss_ref)
            dfg_ref[...] = jnp.zeros_like(dfg_ref)
            dbg_ref[...] = jnp.zeros_like(dbg_ref)
            dhgn_ref[...] = jnp.zeros_like(dhgn_ref)

        z_sb = zsb_ref[...]
        sb_o = sbo_ref[...]
        sig_zsb = _sigmoid(z_sb)
        silu_zsb = z_sb * sig_zsb
        a_sb_f = sb_o * silu_zsb
        a_sb = a_sb_f.astype(BF16)
        u_sb = _dot(a_sb, wsb_ref[...])

        hg_o = hgo_ref[...]
        gain = hgn_ref[...]
        r_parts, yn_parts = [], []
        for h in range(HEADS):
            oh = hg_o[:, h * HEAD_DIM : (h + 1) * HEAD_DIM]
            r = lax.rsqrt(jnp.mean(oh * oh, axis=-1, keepdims=True) + RMS_EPS)
            r_parts.append(jnp.broadcast_to(r, oh.shape))
            yn_parts.append(oh * r)
        r_hg = jnp.concatenate(r_parts, axis=-1)
        yn_hg = jnp.concatenate(yn_parts, axis=-1)
        hn = yn_hg * gain
        hz = hz_ref[...]
        sig_hz = _sigmoid(hz)
        silu_hz = hz * sig_hz
        a_hg_f = hn * silu_hz
        a_hg = a_hg_f.astype(BF16)
        u_hg = _dot(a_hg, whg_ref[...])

        gates = _sigmoid(gl_ref[...] + bg_ref[...])
        g_sb = gates[:, 0:D_MODEL]
        g_hg = gates[:, D_MODEL:]
        y_f = g_sb * u_sb + g_hg * u_hg
        y = y_f.astype(BF16)
        out = x_ref[...] + _dot(y, wout_ref[...])
        r2 = lax.rsqrt(jnp.mean(out * out, axis=-1, keepdims=True) + RMS_EPS)
        yn = out * r2
        fg = fg_ref[...]
        diff = yn * fg - tgt_ref[...]
        loss_ref[...] += 0.5 * inv_d * jnp.sum(diff * diff)

        dyf = diff * inv_d
        dfg_ref[...] += jnp.sum(dyf * yn, axis=0, keepdims=True)
        dyn = dyf * fg
        dout = r2 * (dyn - yn * jnp.mean(dyn * yn, axis=-1, keepdims=True))
        dout_ref[...] = dout
        doutb = dout.astype(BF16)
        doutb_ref[...] = doutb
        dy = _dot_nt(doutb, wout_ref[...])
        du_sb = (dy * g_sb).astype(BF16)
        du_hg = (dy * g_hg).astype(BF16)
        dgl_sb = dy * u_sb * g_sb * (1.0 - g_sb)
        dgl_hg = dy * u_hg * g_hg * (1.0 - g_hg)
        dgl_ref[:, 0:D_MODEL] = dgl_sb.astype(BF16)
        dgl_ref[:, D_MODEL:] = dgl_hg.astype(BF16)
        dbg_ref[:, 0:D_MODEL] += jnp.sum(dgl_sb, axis=0, keepdims=True)
        dbg_ref[:, D_MODEL:] += jnp.sum(dgl_hg, axis=0, keepdims=True)

        da_sb = _dot_nt(du_sb, wsb_ref[...])
        dsbo_ref[...] = da_sb * silu_zsb
        dzsb_ref[...] = (da_sb * sb_o * (sig_zsb * (1.0 + z_sb * (1.0 - sig_zsb)))).astype(BF16)

        da_hg = _dot_nt(du_hg, whg_ref[...])
        dhn = da_hg * silu_hz
        dhz_ref[...] = (da_hg * hn * (sig_hz * (1.0 + hz * (1.0 - sig_hz)))).astype(BF16)
        dhgn_ref[...] += jnp.sum(dhn * yn_hg, axis=0, keepdims=True)
        dyn_hg = dhn * gain
        prod = dyn_hg * yn_hg
        m_parts = []
        for h in range(HEADS):
            ph = prod[:, h * HEAD_DIM : (h + 1) * HEAD_DIM]
            m_parts.append(jnp.broadcast_to(jnp.mean(ph, axis=-1, keepdims=True), ph.shape))
        dhgo_ref[...] = r_hg * (dyn_hg - yn_hg * jnp.concatenate(m_parts, axis=-1))

        asb_ref[...] = a_sb_f.T.astype(BF16)
        dusb_ref[...] = du_sb
        ahg_ref[...] = a_hg_f.T.astype(BF16)
        duhg_ref[...] = du_hg
        y_ref[...] = y_f.T.astype(BF16)

    def tile(width, off=0):
        return pl.BlockSpec((ts, width), lambda s: (s, off // width))

    def across():
        return pl.BlockSpec((D_MODEL, ts), lambda s: (0, s))

    def whole(shape):
        return pl.BlockSpec(shape, lambda s: (0,) * len(shape))

    def weight():
        return pl.BlockSpec((D_MODEL, D_MODEL), lambda s: (0, 0), pipeline_mode=pl.Buffered(1))

    f32_act = jax.ShapeDtypeStruct((s_len, D_MODEL), F32)
    bf_act = jax.ShapeDtypeStruct((s_len, D_MODEL), BF16)
    bf_act_t = jax.ShapeDtypeStruct((D_MODEL, s_len), BF16)
    return pl.pallas_call(
        body,
        name="mid",
        grid=(s_len // ts,),
        in_specs=[
            tile(D_MODEL, OFF_SB_Z), tile(D_MODEL, OFF_HG_Z), tile(2 * D_MODEL, OFF_GATE),
            tile(D_MODEL), tile(D_MODEL), tile(D_MODEL), tile(D_MODEL),
            whole((1, 2 * D_MODEL)), whole((1, D_MODEL)), whole((1, D_MODEL)),
            weight(), weight(), weight(),
        ],
        out_specs=[
            tile(D_MODEL), tile(D_MODEL), tile(D_MODEL), tile(D_MODEL), tile(D_MODEL), tile(2 * D_MODEL),
            across(), tile(D_MODEL), across(), tile(D_MODEL), across(), tile(D_MODEL),
            whole((1, 1)), whole((1, D_MODEL)), whole((1, 2 * D_MODEL)), whole((1, D_MODEL)),
        ],
        out_shape=[
            f32_act, f32_act, f32_act, bf_act, bf_act, jax.ShapeDtypeStruct((s_len, 2 * D_MODEL), BF16),
            bf_act_t, bf_act, bf_act_t, bf_act, bf_act_t, bf_act,
            jax.ShapeDtypeStruct((1, 1), F32), jax.ShapeDtypeStruct((1, D_MODEL), F32),
            jax.ShapeDtypeStruct((1, 2 * D_MODEL), F32), jax.ShapeDtypeStruct((1, D_MODEL), F32),
        ],
        compiler_params=_cparams(("arbitrary",)),
    )(proj, proj, proj, sb_o, hg_o, x, target, b_gate, hg_gain, final_g, w_sb, w_hg, w_out)


def _grad_square(a_t, b, name):
    s_len = b.shape[0]
    tk = min(1024, s_len)

    def body(a_ref, b_ref, o_ref):
        @pl.when(pl.program_id(0) == 0)
        def _():
            o_ref[...] = jnp.zeros_like(o_ref)

        o_ref[...] += _dot(a_ref[...], b_ref[...])

    return pl.pallas_call(
        body,
        name=name,
        grid=(s_len // tk,),
        in_specs=[pl.BlockSpec((D_MODEL, tk), lambda k: (0, k)), pl.BlockSpec((tk, D_MODEL), lambda k: (k, 0))],
        out_specs=pl.BlockSpec((D_MODEL, D_MODEL), lambda k: (0, 0)),
        out_shape=jax.ShapeDtypeStruct((D_MODEL, D_MODEL), F32),
        compiler_params=_cparams(("arbitrary",)),
    )(a_t, b)


SEG_WIDTHS = (1024, 1024, 1024, 1024, 3072, 1024, 2048)


def _seg_bounds(tile):
    bounds = [0]
    for w in SEG_WIDTHS:
        bounds.append(bounds[-1] + w // tile)
    return bounds


def _grad_w_in(h_t, segs):
    m, s_len = h_t.shape
    tk = min(1024, s_len)
    tn = 1024
    nk = s_len // tk
    bounds = _seg_bounds(tn)

    def body(a_ref, *refs):
        seg_refs, o_ref = refs[:-1], refs[-1]
        j = pl.program_id(0)

        @pl.when(pl.program_id(1) == 0)
        def _():
            o_ref[...] = jnp.zeros_like(o_ref)

        for i, ref in enumerate(seg_refs):
            @pl.when((j >= bounds[i]) & (j < bounds[i + 1]))
            def _(ref=ref):
                o_ref[...] += _dot(a_ref[...], ref[...])

    def seg_spec(lo, hi):
        def index(j, k):
            return (jnp.where(j < lo, 0, jnp.where(j >= hi, nk - 1, k)), jnp.clip(j - lo, 0, hi - lo - 1))
        return pl.BlockSpec((tk, tn), index)

    return pl.pallas_call(
        body,
        name="grad_w_in",
        grid=(IN_WIDTH // tn, nk),
        in_specs=[pl.BlockSpec((m, tk), lambda j, k: (0, k))] + [seg_spec(bounds[i], bounds[i + 1]) for i in range(7)],
        out_specs=pl.BlockSpec((m, tn), lambda j, k: (0, j)),
        out_shape=jax.ShapeDtypeStruct((m, IN_WIDTH), F32),
        compiler_params=_cparams(("arbitrary", "arbitrary")),
    )(h_t, *segs)


EXCHANGE_IN_PIECES = 8
EXCHANGE_PIECES = EXCHANGE_IN_PIECES + 3


def _exchange_copies(sin_ref, ssq_ref, got_in, got_sq, send_sems, recv_sems):
    _, _, c, chips = _position()
    rows = HALF_IN // EXCHANGE_IN_PIECES
    copies = []
    for k, (px, py) in enumerate(chips):
        chip = 2 * px + py
        for p in range(EXCHANGE_PIECES):
            if p < EXCHANGE_IN_PIECES:
                src, dst = sin_ref.at[chip, pl.ds(p * rows, rows), :], got_in.at[k, pl.ds(p * rows, rows), :]
            else:
                src, dst = ssq_ref.at[p - EXCHANGE_IN_PIECES, chip], got_sq.at[k, p - EXCHANGE_IN_PIECES]
            copies.append(_remote(src, dst, send_sems.at[k, p], recv_sems.at[k, p], (px, py, c)))
    return copies


def _dx(segs, w_all, x, norm_g, dout, s_in, s_sq):
    s_len = x.shape[0]
    ts = min(512, s_len)
    tk = 1024
    nk = IN_WIDTH // tk
    ns = s_len // ts
    bounds = _seg_bounds(tk)

    def body(*refs):
        seg_refs = refs[:7]
        w_ref, x_ref, g_ref, dout_ref, sin_ref, ssq_ref, gx_ref, dg_ref, got_in, got_sq, acc, send_sems, recv_sems = refs[7:]
        s, k = pl.program_id(0), pl.program_id(1)

        @pl.when((s == 0) & (k == 0))
        def _():
            dg_ref[...] = jnp.zeros_like(dg_ref)
            for cp in _exchange_copies(sin_ref, ssq_ref, got_in, got_sq, send_sems, recv_sems):
                cp.start()

        @pl.when(k == 0)
        def _():
            acc[...] = jnp.zeros_like(acc)

        for i, ref in enumerate(seg_refs):
            @pl.when((k >= bounds[i]) & (k < bounds[i + 1]))
            def _(ref=ref):
                acc[...] += _dot_nt(ref[...], w_ref[...])

        @pl.when(k == nk - 1)
        def _():
            dh = acc[...]
            xv = x_ref[...]
            r = lax.rsqrt(jnp.mean(xv * xv, axis=-1, keepdims=True) + RMS_EPS)
            xn = xv * r
            dg_ref[...] += jnp.sum(dh * xn, axis=0, keepdims=True)
            dxn = dh * g_ref[...]
            gx_ref[...] = r * (dxn - xn * jnp.mean(dxn * xn, axis=-1, keepdims=True)) + dout_ref[...]

        @pl.when((s == ns - 1) & (k == nk - 1))
        def _():
            for cp in _exchange_copies(sin_ref, ssq_ref, got_in, got_sq, send_sems, recv_sems):
                cp.wait()

    def seg_spec(lo, hi):
        return pl.BlockSpec((ts, tk), lambda s, k: (s, jnp.clip(k - lo, 0, hi - lo - 1)))

    row_tile = pl.BlockSpec((ts, D_MODEL), lambda s, k: (s, 0))
    vec = pl.BlockSpec((1, D_MODEL), lambda s, k: (0, 0))
    return pl.pallas_call(
        body,
        name="dx",
        grid=(ns, nk),
        in_specs=[seg_spec(bounds[i], bounds[i + 1]) for i in range(7)] + [
            pl.BlockSpec((D_MODEL, tk), lambda s, k: (0, k)),
            row_tile, vec, row_tile, ANY, ANY,
        ],
        out_specs=[row_tile, vec, ANY, ANY],
        out_shape=[jax.ShapeDtypeStruct((s_len, D_MODEL), F32), jax.ShapeDtypeStruct((1, D_MODEL), F32),
                   jax.ShapeDtypeStruct((3, HALF_IN, W_IN_SHARD), WIRE),
                   jax.ShapeDtypeStruct((3, 3, HALF_SQ, D_MODEL), WIRE)],
        scratch_shapes=[pltpu.VMEM((ts, D_MODEL), F32),
                        pltpu.SemaphoreType.DMA((3, EXCHANGE_PIECES)), pltpu.SemaphoreType.DMA((3, EXCHANGE_PIECES))],
        compiler_params=_cparams(("arbitrary", "arbitrary")),
    )(*segs, w_all, x, norm_g, dout, s_in, s_sq)


def _local_grads(x, target, norm_g, b_gate, lbl, hg_gain, final_g, w_all, w_sb, w_hg, w_out):
    proj, h_t, qkv = _inproj(x, norm_g, w_all)
    sb_o, sb_o_fine = _sb_fwd(qkv)
    hg_o, states = _hg_fwd(proj, lbl)
    (dout, d_sbo, d_hgo, d_zsb, d_hz, d_gl, a_sb, du_sb, a_hg, du_hg, y, doutb,
     loss, d_fg, d_bg, d_hgn) = _mid(proj, sb_o, hg_o, x, target, b_gate, hg_gain, final_g, w_sb, w_hg, w_out)
    g_w_sb = _grad_square(a_sb, du_sb, "grad_w_sb")
    g_w_hg = _grad_square(a_hg, du_hg, "grad_w_hg")
    g_w_out = _grad_square(y, doutb, "grad_w_out")
    d_q, d_k, d_v = _sb_bwd(qkv, sb_o_fine, d_sbo)
    d_hg, d_lb = _hg_bwd(proj, lbl, states, d_hgo)
    segs = (d_q, d_k, d_v, d_zsb, d_hg, d_hz, d_gl)
    g_w_in = _grad_w_in(h_t, segs)
    return g_w_in, g_w_sb, g_w_hg, g_w_out, segs, dout, loss, d_bg, d_lb, d_hgn, d_fg


ANY = pl.BlockSpec(memory_space=pl.ANY)
WIRE = BF16
HALF_IN = D_MODEL // 2
HALF_SQ = ROW_SHARD // 2


def _position():
    x, y, c = lax.axis_index("x"), lax.axis_index("y"), lax.axis_index("c")
    chips = [(1 - x, y), (x, 1 - y), (1 - x, 1 - y)]
    return x, y, c, chips


def _remote(src, dst, send_sem, recv_sem, to):
    return pltpu.make_async_remote_copy(src_ref=src, dst_ref=dst, send_sem=send_sem, recv_sem=recv_sem,
                                        device_id=to, device_id_type=MESH)


def _gather_weights(w_in_b, w_sq_b):
    n_in = 4
    n_piece = n_in + 3
    rows = HALF_IN // n_in

    def body(win_ref, wsq_ref, in_ref, sq_ref, send_sems, recv_sems):
        x, y, c, chips = _position()
        me = 2 * x + y
        sibling = (x, y, 1 - c)

        def src_piece(p):
            if p < n_in:
                return win_ref.at[pl.ds(c * HALF_IN + p * rows, rows), :]
            return wsq_ref.at[p - n_in, pl.ds(c * HALF_SQ, HALF_SQ), :]

        def piece(p, chip, core):
            if p < n_in:
                cols = pl.ds(pl.multiple_of(chip * W_IN_SHARD, W_IN_SHARD), W_IN_SHARD)
                return in_ref.at[pl.ds(core * HALF_IN + p * rows, rows), cols]
            return sq_ref.at[p - n_in, chip, pl.ds(core * HALF_SQ, HALF_SQ), :]

        sends = []
        for k, (px, py) in enumerate(chips):
            for p in range(n_piece):
                sends.append(_remote(src_piece(p), piece(p, me, c), send_sems.at[k, p], recv_sems.at[k, p], (px, py, c)))
        for cp in sends:
            cp.start()
        for k, (px, py) in enumerate(chips):
            chip = 2 * px + py
            for p in range(n_piece):
                got = piece(p, chip, c)
                _remote(got, got, send_sems.at[k, p], recv_sems.at[k, p], (px, py, c)).wait_recv()
                fwd = _remote(got, got, send_sems.at[3 + k, p], recv_sems.at[3 + k, p], sibling)
                fwd.start()
                sends.append(fwd)
        for k, (px, py) in enumerate(chips):
            chip = 2 * px + py
            for p in range(n_piece):
                got = piece(p, chip, 1 - c)
                _remote(got, got, send_sems.at[3 + k, p], recv_sems.at[3 + k, p], sibling).wait_recv()
        for cp in sends:
            cp.wait_send()

    return pl.pallas_call(
        body,
        name="gather_weights",
        in_specs=[ANY, ANY],
        out_specs=[ANY, ANY],
        out_shape=[jax.ShapeDtypeStruct((D_MODEL, IN_WIDTH), BF16),
                   jax.ShapeDtypeStruct((3, N_CHIPS, ROW_SHARD, D_MODEL), BF16)],
        scratch_shapes=[pltpu.SemaphoreType.DMA((6, n_piece)), pltpu.SemaphoreType.DMA((6, n_piece))],
    )(w_in_b, w_sq_b)


def _place_own(idx, w_in_b, w_sq_b, w_all, wsq):
    n = 4
    r_in, r_sq = D_MODEL // n, ROW_SHARD // n

    def body(idx_ref, win_ref, wsq_ref, w_all_in, wsq_in, w_all_out, wsq_out):
        w_all_out[...] = win_ref[...]
        wsq_out[:, 0] = wsq_ref[...]

    grid_spec = pltpu.PrefetchScalarGridSpec(
        num_scalar_prefetch=1,
        grid=(n,),
        in_specs=[pl.BlockSpec((r_in, W_IN_SHARD), lambda r, idx: (r, 0)),
                  pl.BlockSpec((3, r_sq, D_MODEL), lambda r, idx: (0, r, 0)), ANY, ANY],
        out_specs=[pl.BlockSpec((r_in, W_IN_SHARD), lambda r, idx: (r, idx[0])),
                   pl.BlockSpec((3, 1, r_sq, D_MODEL), lambda r, idx: (0, idx[0], r, 0))],
    )
    return pl.pallas_call(
        body,
        name="place_own",
        grid_spec=grid_spec,
        out_shape=[jax.ShapeDtypeStruct(w_all.shape, BF16), jax.ShapeDtypeStruct(wsq.shape, BF16)],
        input_output_aliases={3: 0, 4: 1},
        compiler_params=_cparams(("arbitrary",)),
    )(idx, w_in_b, w_sq_b, w_all, wsq)


def _swap_halves(g_in, g_sq):
    n_in = 16
    n_piece = n_in + 3 * N_CHIPS
    rows = HALF_IN // n_in

    def body(gin_ref, gsq_ref, got_in, got_sq, send_sems, recv_sems):
        x, y, c, _ = _position()
        sibling = (x, y, 1 - c)

        def src_piece(p):
            if p < n_in:
                return gin_ref.at[pl.ds((1 - c) * HALF_IN + p * rows, rows), :]
            a, chip = divmod(p - n_in, N_CHIPS)
            return gsq_ref.at[a, chip, pl.ds((1 - c) * HALF_SQ, HALF_SQ), :]

        def dst_piece(p):
            if p < n_in:
                return got_in.at[pl.ds(p * rows, rows), :]
            a, chip = divmod(p - n_in, N_CHIPS)
            return got_sq.at[a, chip]

        out = [_remote(src_piece(p), dst_piece(p), send_sems.at[p], recv_sems.at[p], sibling) for p in range(n_piece)]
        for cp in out:
            cp.start()
        for cp in out:
            cp.wait()

    return pl.pallas_call(
        body,
        name="swap_halves",
        in_specs=[ANY, ANY],
        out_specs=[ANY, ANY],
        out_shape=[jax.ShapeDtypeStruct((HALF_IN, IN_WIDTH), F32),
                   jax.ShapeDtypeStruct((3, N_CHIPS, HALF_SQ, D_MODEL), F32)],
        scratch_shapes=[pltpu.SemaphoreType.DMA((n_piece,))] * 2,
    )(g_in, g_sq)


def _join_halves(r_in, r_sq):
    n_in = 16
    n_piece = n_in + 3
    rows = HALF_IN // n_in

    def body(in_alias, sq_alias, full_in, full_sq, send_sems, recv_sems):
        del in_alias, sq_alias
        x, y, c, _ = _position()
        sibling = (x, y, 1 - c)

        def piece(p, core):
            if p < n_in:
                return full_in.at[pl.ds(core * HALF_IN + p * rows, rows), :]
            return full_sq.at[p - n_in, pl.ds(core * HALF_SQ, HALF_SQ), :]

        out = [_remote(piece(p, c), piece(p, c), send_sems.at[p], recv_sems.at[p], sibling) for p in range(n_piece)]
        for cp in out:
            cp.start()
        for p in range(n_piece):
            _remote(piece(p, 1 - c), piece(p, 1 - c), send_sems.at[p], recv_sems.at[p], sibling).wait_recv()
        for cp in out:
            cp.wait_send()

    return pl.pallas_call(
        body,
        name="join_halves",
        in_specs=[ANY, ANY],
        out_specs=[ANY, ANY],
        out_shape=[jax.ShapeDtypeStruct((D_MODEL, W_IN_SHARD), F32),
                   jax.ShapeDtypeStruct((3, ROW_SHARD, D_MODEL), F32)],
        input_output_aliases={0: 0, 1: 1},
        scratch_shapes=[pltpu.SemaphoreType.DMA((n_piece,)), pltpu.SemaphoreType.DMA((n_piece,))],
    )(r_in, r_sq)


SMALL_ROWS = 56
N_DEV = 8


def _sum_small(part):
    def body(part_ref, out_ref, slots, send_sems, recv_sems):
        x, y, c, _ = _position()
        me = 4 * x + 2 * y + c
        slots[me] = part_ref[...]
        out = []
        for r in range(1, N_DEV):
            rx, ry, rc = (r >> 2) & 1, (r >> 1) & 1, r & 1
            to = (1 - x if rx else x, 1 - y if ry else y, 1 - c if rc else c)
            out.append(_remote(part_ref, slots.at[me], send_sems.at[r - 1], recv_sems.at[r - 1], to))
        for cp in out:
            cp.start()
        for r in range(1, N_DEV):
            _remote(part_ref, slots.at[me ^ r], send_sems.at[r - 1], recv_sems.at[r - 1], (x, y, c)).wait_recv()
        for cp in out:
            cp.wait_send()
        total = slots[0]
        for d in range(1, N_DEV):
            total = total + slots[d]
        out_ref[...] = total

    vmem = pl.BlockSpec(memory_space=pltpu.VMEM)
    return pl.pallas_call(
        body,
        name="sum_small",
        in_specs=[vmem],
        out_specs=vmem,
        out_shape=jax.ShapeDtypeStruct((SMALL_ROWS, HEAD_DIM), F32),
        scratch_shapes=[pltpu.VMEM((N_DEV, SMALL_ROWS, HEAD_DIM), F32),
                        pltpu.SemaphoreType.DMA((N_DEV - 1,)), pltpu.SemaphoreType.DMA((N_DEV - 1,))],
    )(part)


def _prefetch_call(body, name, idx, grid, in_specs, out_specs, out_shape, args):
    grid_spec = pltpu.PrefetchScalarGridSpec(num_scalar_prefetch=1, grid=grid, in_specs=in_specs, out_specs=out_specs)
    return pl.pallas_call(body, name=name, grid_spec=grid_spec, out_shape=out_shape,
                          compiler_params=_cparams(("arbitrary",) * len(grid)))(idx, *args)


def _sum_a_in(idx, g_in, got_in):
    tr = 128
    nr = HALF_IN // tr

    def body(idx_ref, a_ref, b_ref, o_ref):
        o_ref[0] = (a_ref[...] + b_ref[...]).astype(WIRE)

    return _prefetch_call(
        body, "sum_a_in", idx, (N_CHIPS, nr),
        [pl.BlockSpec((tr, W_IN_SHARD), lambda j, r, idx: (idx[1] * nr + r, j)),
         pl.BlockSpec((tr, W_IN_SHARD), lambda j, r, idx: (r, j))],
        pl.BlockSpec((1, tr, W_IN_SHARD), lambda j, r, idx: (j, r, 0)),
        jax.ShapeDtypeStruct((N_CHIPS, HALF_IN, W_IN_SHARD), WIRE), (g_in, got_in))


def _sum_a_sq(idx, g_sq, got_sq):
    blk = (1, 1, HALF_SQ, D_MODEL)

    def body(idx_ref, a_ref, b_ref, o_ref):
        o_ref[...] = (a_ref[...] + b_ref[...]).astype(WIRE)

    return _prefetch_call(
        body, "sum_a_sq", idx, (3, N_CHIPS),
        [pl.BlockSpec(blk, lambda a, j, idx: (a, j, idx[1], 0)), pl.BlockSpec(blk, lambda a, j, idx: (a, j, 0, 0))],
        pl.BlockSpec(blk, lambda a, j, idx: (a, j, 0, 0)),
        jax.ShapeDtypeStruct((3, N_CHIPS, HALF_SQ, D_MODEL), WIRE), (g_sq, got_sq))


def _sum_b_in(idx, s_in, got_in):
    tr = 128
    nr = HALF_IN // tr

    def body(idx_ref, a_ref, b_ref, o_ref):
        o_ref[...] = ((a_ref[0].astype(F32) + b_ref[0].astype(F32)) + b_ref[1].astype(F32)) + b_ref[2].astype(F32)

    return _prefetch_call(
        body, "sum_b_in", idx, (nr,),
        [pl.BlockSpec((1, tr, W_IN_SHARD), lambda r, idx: (idx[0], r, 0)),
         pl.BlockSpec((3, tr, W_IN_SHARD), lambda r, idx: (0, r, 0))],
        pl.BlockSpec((tr, W_IN_SHARD), lambda r, idx: (idx[1] * nr + r, 0)),
        jax.ShapeDtypeStruct((D_MODEL, W_IN_SHARD), F32), (s_in, got_in))


def _sum_b_sq(idx, s_sq, got_sq):
    def body(idx_ref, a_ref, b_ref, o_ref):
        o_ref[0] = ((a_ref[0, 0].astype(F32) + b_ref[0, 0].astype(F32)) + b_ref[1, 0].astype(F32)) + b_ref[2, 0].astype(F32)

    return _prefetch_call(
        body, "sum_b_sq", idx, (3,),
        [pl.BlockSpec((1, 1, HALF_SQ, D_MODEL), lambda a, idx: (a, idx[0], 0, 0)),
         pl.BlockSpec((3, 1, HALF_SQ, D_MODEL), lambda a, idx: (0, a, 0, 0))],
        pl.BlockSpec((1, HALF_SQ, D_MODEL), lambda a, idx: (a, idx[1], 0)),
        jax.ShapeDtypeStruct((3, ROW_SHARD, D_MODEL), F32), (s_sq, got_sq))


def _adamw_math(w, g, m, v):
    m = ADAM_B1 * m + (1.0 - ADAM_B1) * g
    v = ADAM_B2 * v + (1.0 - ADAM_B2) * (g * g)
    m_hat = m / (1.0 - ADAM_B1 ** ADAM_STEP)
    v_hat = v / (1.0 - ADAM_B2 ** ADAM_STEP)
    delta = -ADAM_LR * (m_hat / (jnp.sqrt(v_hat) + ADAM_EPS) + ADAM_WD * w)
    return delta, m, v


def _adamw(w, g, m, v, name):
    rows, cols = w.shape
    tr = min(128, rows)

    def body(w_ref, g_ref, m_ref, v_ref, d_ref, nm_ref, nv_ref):
        d_ref[...], nm_ref[...], nv_ref[...] = _adamw_math(w_ref[...], g_ref[...], m_ref[...], v_ref[...])

    spec = pl.BlockSpec((tr, cols), lambda r: (r, 0))
    return pl.pallas_call(
        body,
        name=name,
        grid=(rows // tr,),
        in_specs=[spec] * 4,
        out_specs=[spec] * 3,
        out_shape=[jax.ShapeDtypeStruct((rows, cols), F32)] * 3,
        compiler_params=_cparams(("arbitrary",)),
    )(w, g, m, v)


def _adamw_small(sums, w, m, v):
    def body(s_ref, w_ref, m_ref, v_ref, loss_ref, g_ref, d_ref, nm_ref, nv_ref):
        s = s_ref[...]
        w = w_ref[...]
        loss_ref[...] = s[0:1, 0:1]
        l0, l1 = w[24:32], w[32:40]
        mx = jnp.maximum(l0, l1)
        e0, e1 = jnp.exp(l0 - mx), jnp.exp(l1 - mx)
        p0, p1 = e0 / (e0 + e1), e1 / (e0 + e1)
        d_lb = s[32:40]
        g = jnp.concatenate([s[8:16], s[16:32], d_lb * p0 * (1.0 - p0), -d_lb * p0 * p1, s[40:48], s[48:56]], axis=0)
        g_ref[...] = g
        d_ref[...], nm_ref[...], nv_ref[...] = _adamw_math(w, g, m_ref[...], v_ref[...])

    packed = jax.ShapeDtypeStruct((SMALL_ROWS, HEAD_DIM), F32)
    return pl.pallas_call(
        body,
        name="adamw_small",
        out_shape=[jax.ShapeDtypeStruct((1, 1), F32), packed, packed, packed, packed],
    )(sums, w, m, v)


def _pack_small(ng, bg, lbl, hgn, fg):
    return jnp.concatenate([a.reshape(-1, HEAD_DIM) for a in (ng, bg, lbl, hgn, fg)], axis=0)


def _unpack_small(p):
    return (p[0:8].reshape(1, D_MODEL), p[8:24].reshape(1, 2 * D_MODEL), p[24:40].reshape(2, HEADS, HEAD_DIM),
            p[40:48].reshape(1, HEADS, HEAD_DIM), p[48:56].reshape(D_MODEL))


def kernel(x, norm_g, w_in, b_gate, lb_logits, hg_norm_g, w_sb_proj, w_hg_proj, w_out, final_norm_g, loss_target, m_norm_g, m_w_in, m_b_gate, m_lb_logits, m_hg_norm_g, m_w_sb_proj, m_w_hg_proj, m_w_out, m_final_norm_g, v_norm_g, v_w_in, v_b_gate, v_lb_logits, v_hg_norm_g, v_w_sb_proj, v_w_hg_proj, v_w_out, v_final_norm_g):
    s_len = x.shape[1]
    w_sq = jnp.stack([w_sb_proj[0], w_hg_proj[0], w_out[0]])
    idx = jnp.stack([2 * lax.axis_index("x") + lax.axis_index("y"), lax.axis_index("c")]).astype(jnp.int32)
    w_in_b, w_sq_b = w_in[0].astype(BF16), w_sq.astype(BF16)
    w_all, wsq = _place_own(idx, w_in_b, w_sq_b, *_gather_weights(w_in_b, w_sq_b))
    wsq = wsq.reshape(3, D_MODEL, D_MODEL)

    (g_in, g_sb, g_hg, g_out, segs, dout, loss, d_bg, d_lb, d_hgn, d_fg) = _local_grads(
        x[0], loss_target[0], norm_g, b_gate, lb_logits.reshape(2, D_MODEL), hg_norm_g.reshape(1, D_MODEL),
        final_norm_g.reshape(1, D_MODEL), w_all, wsq[0], wsq[1], wsq[2])

    g_sq = jnp.stack([g_sb, g_hg, g_out]).reshape(3, N_CHIPS, ROW_SHARD, D_MODEL)
    got_in, got_sq = _swap_halves(g_in, g_sq)
    s_in, s_sq = _sum_a_in(idx, g_in, got_in), _sum_a_sq(idx, g_sq, got_sq)
    grad_x, d_ng, got_in, got_sq = _dx(segs, w_all, x[0], norm_g, dout, s_in, s_sq)
    grad_in, grad_sq = _join_halves(_sum_b_in(idx, s_in, got_in), _sum_b_sq(idx, s_sq, got_sq))

    d_in, nm_in, nv_in = _adamw(w_in[0], grad_in, m_w_in[0], v_w_in[0], "adamw_in")
    flat = lambda a, b, c: jnp.concatenate([a[0], b[0], c[0]], axis=0)
    d_sq, nm_sq, nv_sq = _adamw(flat(w_sb_proj, w_hg_proj, w_out), grad_sq.reshape(3 * ROW_SHARD, D_MODEL),
                                flat(m_w_sb_proj, m_w_hg_proj, m_w_out), flat(v_w_sb_proj, v_w_hg_proj, v_w_out),
                                "adamw_sq")

    pad = jnp.zeros((8, HEAD_DIM), F32).at[0, 0].set(loss[0, 0])
    part = jnp.concatenate([pad] + [a.reshape(-1, HEAD_DIM) for a in (d_ng, d_bg, d_lb, d_hgn, d_fg)], axis=0)
    sums = _sum_small(part)
    loss_out, g_sm, d_sm, nm_sm, nv_sm = _adamw_small(
        sums, _pack_small(norm_g, b_gate, lb_logits, hg_norm_g, final_norm_g),
        _pack_small(m_norm_g, m_b_gate, m_lb_logits, m_hg_norm_g, m_final_norm_g),
        _pack_small(v_norm_g, v_b_gate, v_lb_logits, v_hg_norm_g, v_final_norm_g))

    def big(t_in, t_sq):
        sq = t_sq.reshape(3, 1, ROW_SHARD, D_MODEL)
        return t_in[None], sq[0], sq[1], sq[2]

    def order(small, in_, sb, hg, out):
        ng, bg, lbl, hgn, fg = small
        return [ng, in_, bg, lbl, hgn, sb, hg, out, fg]

    outs = [loss_out[0, 0], grad_x[None]]
    for small, (t_in, t_sq) in ((g_sm, (grad_in, grad_sq)), (d_sm, (d_in, d_sq)), (nm_sm, (nm_in, nm_sq)), (nv_sm, (nv_in, nv_sq))):
        outs += order(_unpack_small(small), *big(t_in, t_sq))
    return tuple(outs)
```

```python
import functools

import jax
import jax.numpy as jnp
from jax import lax
from jax.experimental import pallas as pl
from jax.experimental.pallas import tpu as pltpu

F32 = jnp.float32
BF16 = jnp.bfloat16

D_MODEL = 1024
HEADS = 8
HEAD_DIM = 128
IN_WIDTH = 10240
N_CHIPS = 4
W_IN_SHARD = IN_WIDTH // N_CHIPS
ROW_SHARD = D_MODEL // N_CHIPS
RMS_EPS = 1e-6

OFF_SB_Q, OFF_SB_K, OFF_SB_V, OFF_SB_Z = 0, 1024, 2048, 3072
OFF_HG_Q, OFF_HG_F, OFF_HG_I, OFF_HG_Z, OFF_GATE = 4096, 5120, 6144, 7168, 8192

SB_BLOCK = 256
SB_FWD_HEADS = 4
SB_BWD_HEADS = 2
SB_ROWS = 256
SB_DEAD = -110.0
SB_GONE = -1e30
HG_CHUNK = 32
HG_PAIR = 2 * HG_CHUNK
HG_STEP = 256
HG_MID = HG_CHUNK // 2 - 1

ADAM_LR, ADAM_B1, ADAM_B2, ADAM_EPS, ADAM_WD, ADAM_STEP = 0.001, 0.9, 0.999, 1e-08, 0.01, 10

VMEM_LIMIT = 56 * 1024 * 1024

MESH = pl.DeviceIdType.MESH


def _cparams(sem, vmem=VMEM_LIMIT):
    return pltpu.CompilerParams(dimension_semantics=sem, vmem_limit_bytes=vmem)


def _dot(a, b):
    return jnp.dot(a, b, preferred_element_type=F32)


def _dot_nt(a, b):
    return lax.dot_general(a, b, (((1,), (1,)), ((), ())), preferred_element_type=F32)


def _dot_tn(a, b):
    return lax.dot_general(a, b, (((0,), (0,)), ((), ())), preferred_element_type=F32)


def _split_dot(x, tri):
    hi = x.astype(BF16)
    lo = (x - hi.astype(F32)).astype(BF16)
    both = _dot(jnp.concatenate([hi, lo], axis=0), tri)
    return both[: x.shape[0]] + both[x.shape[0] :]


def _split_dot_left(tri, x):
    hi = x.astype(BF16)
    lo = (x - hi.astype(F32)).astype(BF16)
    return _dot(tri, hi) + _dot(tri, lo)


def _sigmoid(x):
    return 1.0 / (1.0 + jnp.exp(-x))


def _prenorm(x, norm_g):
    s_len = x.shape[0]
    ts = min(1024, s_len)

    def body(x_ref, g_ref, h_ref, ht_ref):
        xv = x_ref[...]
        r = lax.rsqrt(jnp.mean(xv * xv, axis=-1, keepdims=True) + RMS_EPS)
        hv = (xv * r) * g_ref[...]
        h_ref[...] = hv.astype(BF16)
        ht_ref[...] = hv.T.astype(BF16)

    return pl.pallas_call(
        body,
        name="prenorm",
        grid=(s_len // ts,),
        in_specs=[pl.BlockSpec((ts, D_MODEL), lambda s: (s, 0)), pl.BlockSpec((1, D_MODEL), lambda s: (0, 0))],
        out_specs=[pl.BlockSpec((ts, D_MODEL), lambda s: (s, 0)), pl.BlockSpec((D_MODEL, ts), lambda s: (0, s))],
        out_shape=[jax.ShapeDtypeStruct((s_len, D_MODEL), BF16), jax.ShapeDtypeStruct((D_MODEL, s_len), BF16)],
        compiler_params=_cparams(("arbitrary",)),
    )(x, norm_g)


def _sb_scores(qb, kb, causal, tri_excl, diag):
    z = _dot_nt(qb, kb) * HEAD_DIM ** -0.5
    ls_pos = jnp.minimum(z, 0.0) - jnp.log1p(jnp.exp(-jnp.abs(z)))
    log_not = ls_pos - z
    log_not_m = jnp.where(causal, log_not, 0.0) if diag else log_not
    return ls_pos, log_not, log_not_m, _split_dot(log_not_m, tri_excl)


def _sb_weights(ls_pos, suffix, carry, causal, diag):
    surv = suffix + carry
    w = jnp.exp(ls_pos + surv)
    return surv, (jnp.where(causal, w, 0.0) if diag else w)


def _sb_specs(s_len, blk, heads):
    width = heads * HEAD_DIM

    def blk_spec(off):
        return pl.BlockSpec((blk, width), lambda h, i: (i, off // width + h))

    def head_spec(off, buffers=2):
        return pl.BlockSpec((s_len, width), lambda h, i: (0, off // width + h), pipeline_mode=pl.Buffered(buffers))

    return blk_spec, head_spec


def _head_cols(p):
    return slice(p * HEAD_DIM, (p + 1) * HEAD_DIM)


def _sb_chains(blk, heads):
    rows = min(SB_ROWS, blk)
    return [(p, a) for p in range(heads) for a in range(blk // rows)], rows


def _sb_masks(blk, rows):
    row = lax.broadcasted_iota(jnp.int32, (rows, blk), 0)
    col = lax.broadcasted_iota(jnp.int32, (rows, blk), 1)
    causal = [row + a * rows > col for a in range(blk // rows)]
    row = lax.broadcasted_iota(jnp.int32, (blk, blk), 0)
    col = lax.broadcasted_iota(jnp.int32, (blk, blk), 1)
    tri_excl = (row > col).astype(BF16)
    tri_incl = (row >= col).astype(BF16)
    return causal, tri_excl, tri_incl


def _sb_alive(st, n_chain):
    alive = functools.reduce(jnp.maximum, [st[1 + 3 * c] for c in range(n_chain)])
    return jnp.max(alive) > SB_DEAD


def _sb_fwd(qkv):
    s_len = qkv.shape[0]
    blk = min(SB_BLOCK, s_len)
    nq = s_len // blk
    chains, rows = _sb_chains(blk, SB_FWD_HEADS)

    def body(q_ref, k_ref, v_ref, o_ref, of_ref):
        i = pl.program_id(1)
        causal, tri_excl, _ = _sb_masks(blk, rows)

        def tiles(specs, st):
            pre = []
            for j, diag, _ in specs:
                start = pl.multiple_of(j * blk, blk)
                for p, a in chains:
                    kb = k_ref[pl.ds(start, blk), _head_cols(p)]
                    qb = q_ref[a * rows : (a + 1) * rows, _head_cols(p)]
                    pre.append(_sb_scores(qb, kb, causal[a], tri_excl, diag) + (v_ref[pl.ds(start, blk), _head_cols(p)],))
            for t, (j, diag, valid) in enumerate(specs):
                new = []
                for c, (p, a) in enumerate(chains):
                    carry, acc, acc_lo = st[3 * c : 3 * c + 3]
                    if valid is not None:
                        carry = jnp.where(valid, carry, SB_GONE)
                    ls_pos, _, log_not_m, suffix, vb = pre[t * len(chains) + c]
                    surv, w = _sb_weights(ls_pos, suffix, carry, causal[a], diag)
                    wb = w.astype(BF16)
                    w_lo = (w - wb.astype(F32)).astype(BF16)
                    both = _dot(jnp.concatenate([wb, w_lo], axis=0), vb)
                    new += [surv[:, 0:1] + log_not_m[:, 0:1], acc + both[:rows], acc_lo + both[rows:]]
                st = tuple(new)
            return st

        zero = jnp.zeros((rows, HEAD_DIM), F32)
        st = tiles([(i, True, None), (jnp.maximum(i - 1, 0), False, i >= 1)],
                   (jnp.zeros((rows, 1), F32), zero, zero) * len(chains))

        def more(st):
            return (st[0] < i) & _sb_alive(st, len(chains))

        def step(st):
            return (st[0] + 1,) + tiles([(i - 1 - st[0], False, None)], st[1:])

        st = lax.while_loop(more, step, (1,) + st)[1:]
        for c, (p, a) in enumerate(chains):
            o_ref[a * rows : (a + 1) * rows, _head_cols(p)] = st[3 * c + 1]
            of_ref[a * rows : (a + 1) * rows, _head_cols(p)] = st[3 * c + 1] + st[3 * c + 2]

    blk_spec, head_spec = _sb_specs(s_len, blk, SB_FWD_HEADS)
    return pl.pallas_call(
        body,
        name="sb_fwd",
        grid=(HEADS // SB_FWD_HEADS, nq),
        in_specs=[blk_spec(OFF_SB_Q), head_spec(OFF_SB_K), head_spec(OFF_SB_V)],
        out_specs=[blk_spec(0), blk_spec(0)],
        out_shape=[jax.ShapeDtypeStruct((s_len, D_MODEL), F32)] * 2,
        compiler_params=_cparams(("arbitrary", "arbitrary")),
    )(qkv, qkv, qkv)


def _sb_bwd(qkv, o_fine, d_o):
    s_len = qkv.shape[0]
    blk = min(SB_BLOCK, s_len)
    nq = s_len // blk
    scale = HEAD_DIM ** -0.5
    chains, rows = _sb_chains(blk, SB_BWD_HEADS)

    def body(q_ref, k_ref, v_ref, of_ref, do_ref, dq_ref, dk_ref, dv_ref, dk_acc, dv_acc):
        i = pl.program_id(1)

        @pl.when(i == 0)
        def _():
            dk_acc[...] = jnp.zeros_like(dk_acc)
            dv_acc[...] = jnp.zeros_like(dv_acc)

        dob = do_ref[...].astype(BF16)
        prod = dob.astype(F32) * of_ref[...]
        causal, tri_excl, tri_incl = _sb_masks(blk, rows)

        def group(x, p, a):
            return x[a * rows : (a + 1) * rows, _head_cols(p)]

        totals = [jnp.sum(group(prod, p, a), axis=-1, keepdims=True) for p, a in chains]

        def tiles(specs, st):
            pre = []
            for j, diag, _ in specs:
                start = pl.multiple_of(j * blk, blk)
                for p, a in chains:
                    kb = k_ref[pl.ds(start, blk), _head_cols(p)]
                    vb = v_ref[pl.ds(start, blk), _head_cols(p)]
                    qb, dob_c = group(q_ref, p, a), group(dob, p, a)
                    pre.append(_sb_scores(qb, kb, causal[a], tri_excl, diag) + (_dot_nt(dob_c, vb), qb, kb, dob_c))
            for t, (j, diag, valid) in enumerate(specs):
                start = pl.multiple_of(j * blk, blk)
                mids = []
                for c, (p, a) in enumerate(chains):
                    c_not = st[3 * c]
                    if valid is not None:
                        c_not = jnp.where(valid, c_not, SB_GONE)
                    ls_pos, _, _, suffix, d_w = pre[t * len(chains) + c][:5]
                    surv, w = _sb_weights(ls_pos, suffix, c_not, causal[a], diag)
                    dlw = d_w * w
                    mids.append((surv, w, dlw, _split_dot(dlw, tri_incl)))
                new = []
                dk_new = [None] * SB_BWD_HEADS
                dv_new = [None] * SB_BWD_HEADS
                for c, (p, a) in enumerate(chains):
                    c_dlw, dq = st[3 * c + 1 : 3 * c + 3]
                    ls_pos, log_not, log_not_m, _, _, qb, kb, dob_c = pre[t * len(chains) + c]
                    surv, w, dlw, suffix = mids[c]
                    d_not = totals[c] - c_dlw - suffix
                    dz = (dlw * jnp.exp(log_not) - d_not * jnp.exp(ls_pos)) * scale
                    if diag:
                        dz = jnp.where(causal[a], dz, 0.0)
                    if valid is not None:
                        dz = jnp.where(valid, dz, 0.0)
                    dzb = dz.astype(BF16)
                    dk_c, dv_c = _dot_tn(dzb, qb), _dot_tn(w.astype(BF16), dob_c)
                    dk_new[p] = dk_c if dk_new[p] is None else dk_new[p] + dk_c
                    dv_new[p] = dv_c if dv_new[p] is None else dv_new[p] + dv_c
                    new += [surv[:, 0:1] + log_not_m[:, 0:1], c_dlw + suffix[:, 0:1], dq + _dot(dzb, kb)]
                for p in range(SB_BWD_HEADS):
                    dk_acc[pl.ds(start, blk), _head_cols(p)] += dk_new[p]
                    dv_acc[pl.ds(start, blk), _head_cols(p)] += dv_new[p]
                st = tuple(new)
            return st

        zcol = jnp.zeros((rows, 1), F32)
        st = tiles([(i, True, None), (jnp.maximum(i - 1, 0), False, i >= 1)],
                   (zcol, zcol, jnp.zeros((rows, HEAD_DIM), F32)) * len(chains))

        def more(st):
            return (st[0] < i) & _sb_alive(st, len(chains))

        def step(st):
            return (st[0] + 1,) + tiles([(i - 1 - st[0], False, None)], st[1:])

        st = lax.while_loop(more, step, (1,) + st)[1:]
        for c, (p, a) in enumerate(chains):
            dq_ref[a * rows : (a + 1) * rows, _head_cols(p)] = st[3 * c + 2].astype(BF16)

        @pl.when(i == nq - 1)
        def _():
            dk_ref[...] = dk_acc[...].astype(BF16)
            dv_ref[...] = dv_acc[...].astype(BF16)

    blk_spec, head_spec = _sb_specs(s_len, blk, SB_BWD_HEADS)
    width = SB_BWD_HEADS * HEAD_DIM
    return pl.pallas_call(
        body,
        name="sb_bwd",
        grid=(HEADS // SB_BWD_HEADS, nq),
        in_specs=[blk_spec(OFF_SB_Q), head_spec(OFF_SB_K, 1), head_spec(OFF_SB_V, 1), blk_spec(0), blk_spec(0)],
        out_specs=[blk_spec(0), head_spec(0), head_spec(0)],
        out_shape=[jax.ShapeDtypeStruct((s_len, D_MODEL), BF16)] * 3,
        scratch_shapes=[pltpu.VMEM((s_len, width), F32), pltpu.VMEM((s_len, width), F32)],
        compiler_params=_cparams(("arbitrary", "arbitrary")),
    )(qkv, qkv, qkv, o_fine, d_o)


def _hg_lower_bound(lbl_ref):
    l0 = lbl_ref[0:1, :]
    l1 = lbl_ref[1:2, :]
    mx = jnp.maximum(l0, l1)
    e0 = jnp.exp(l0 - mx)
    e1 = jnp.exp(l1 - mx)
    return e0 / (e0 + e1)


def _hg_gates(hq, hf, lb):
    sig_f = _sigmoid(hf)
    f = lb + (1.0 - lb) * sig_f
    g = jnp.log(f)
    kk = 1.0 - f
    sig_q = _sigmoid(hq)
    qq = hq * sig_q
    return qq, kk, g, f, sig_f, sig_q


def _period_bcast(x, r, rows, period):
    w = x.shape[-1]
    x3 = x.reshape(rows // period, period, w)
    return jnp.broadcast_to(x3[:, r : r + 1, :], x3.shape).reshape(rows, w)


def _blockdiag(rows, kind):
    row = lax.broadcasted_iota(jnp.int32, (rows, rows), 0)
    col = lax.broadcasted_iota(jnp.int32, (rows, rows), 1)
    if kind in ("next", "prev"):
        first, second = (row, col) if kind == "next" else (col, row)
        keep = ((row // HG_PAIR) == (col // HG_PAIR)) & (first % HG_PAIR < HG_CHUNK) & (second % HG_PAIR >= HG_CHUNK)
    else:
        keep = (row // HG_CHUNK) == (col // HG_CHUNK)
        if kind == "lower":
            keep = keep & (row >= col)
        elif kind == "upper":
            keep = keep & (row <= col)
    return jnp.where(keep, 1.0, 0.0).astype(BF16)


def _hg_operands(hq, hf, lb, rows):
    qq, kk, g, f, sig_f, sig_q = _hg_gates(hq, hf, lb)
    cum = _split_dot_left(_blockdiag(rows, "lower"), g)
    mid = _period_bcast(cum, HG_MID, rows, HG_CHUNK)
    last = _period_bcast(cum, HG_CHUNK - 1, rows, HG_CHUNK)
    last0 = _period_bcast(cum, HG_CHUNK - 1, rows, HG_PAIR)
    last1 = _period_bcast(cum, HG_PAIR - 1, rows, HG_PAIR)
    second = (lax.broadcasted_iota(jnp.int32, cum.shape, 0) % HG_PAIR) >= HG_CHUNK
    e = dict(qm=jnp.exp(cum - mid), km=jnp.exp(mid - cum), qd=jnp.exp(cum), kl=jnp.exp(last - cum),
             q_in=jnp.where(second, jnp.exp(last0), 1.0), k_out=jnp.where(second, 1.0, jnp.exp(last1)),
             pair=jnp.exp(last0 + last1))
    v = dict(qm=qq * e["qm"], km=kk * e["km"], qd=qq * e["qd"], kl=kk * e["kl"])
    v["qp"] = v["qd"] * e["q_in"]
    v["kp"] = v["kl"] * e["k_out"]
    return v, e, second, (f, sig_f, sig_q)


def _hg_store_operands(v, second, hi, refs):
    zero = jnp.zeros_like(v["qm"])
    q_cat, k_cat, qp_b, kp_b, v_b = refs
    q_cat[:, 0:D_MODEL] = jnp.where(second, zero, v["qm"]).astype(BF16)
    q_cat[:, D_MODEL : 2 * D_MODEL] = jnp.where(second, v["qm"], zero).astype(BF16)
    q_cat[:, 2 * D_MODEL :] = jnp.where(second, v["qd"], zero).astype(BF16)
    k_cat[:, 0:D_MODEL] = jnp.where(second, zero, v["km"]).astype(BF16)
    k_cat[:, D_MODEL : 2 * D_MODEL] = jnp.where(second, v["km"], zero).astype(BF16)
    k_cat[:, 2 * D_MODEL :] = jnp.where(second, zero, v["kl"]).astype(BF16)
    qp_b[...] = v["qp"].astype(BF16)
    kp_b[...] = v["kp"].astype(BF16)
    v_b[...] = hi.astype(BF16)


def _hg_pair_operands(cat, r0, c0):
    return jnp.concatenate([cat[r0 : r0 + HG_PAIR, g * D_MODEL + c0 : g * D_MODEL + c0 + HEAD_DIM] for g in range(3)], axis=1)


def _hg_fwd(proj, lbl):
    s_len = proj.shape[0]
    rows = min(HG_STEP, s_len)
    n_pairs = rows // HG_PAIR

    def body(hq_ref, hf_ref, hi_ref, lbl_ref, o_ref, st_ref, state, q_cat, k_cat, qp_b, kp_b, v_b):
        @pl.when(pl.program_id(0) == 0)
        def _():
            state[...] = jnp.zeros_like(state)

        v, e, second, _ = _hg_operands(hq_ref[...], hf_ref[...], _hg_lower_bound(lbl_ref), rows)
        _hg_store_operands(v, second, hi_ref[...], (q_cat, k_cat, qp_b, kp_b, v_b))
        e_pair = e["pair"]
        row = lax.broadcasted_iota(jnp.int32, (HG_PAIR, HG_PAIR), 0)
        col = lax.broadcasted_iota(jnp.int32, (HG_PAIR, HG_PAIR), 1)
        causal = row >= col

        for u in range(n_pairs):
            r0 = u * HG_PAIR
            sls = [(slice(r0, r0 + HG_PAIR), slice(h * HEAD_DIM, (h + 1) * HEAD_DIM)) for h in range(HEADS)]
            a_s = [jnp.where(causal, _dot_nt(_hg_pair_operands(q_cat, r0, h * HEAD_DIM),
                                             _hg_pair_operands(k_cat, r0, h * HEAD_DIM)), 0.0).astype(BF16)
                   for h in range(HEADS)]
            st_s = [state[h] for h in range(HEADS)]
            for h, sl in enumerate(sls):
                st_ref[u, h] = st_s[h]
                state[h] = st_s[h] * e_pair[r0 : r0 + 1, sl[1]] + _dot_tn(v_b[sl], kp_b[sl])
            for h, sl in enumerate(sls):
                o_ref[sl] = _dot(a_s[h], v_b[sl]) + _dot_nt(qp_b[sl], st_s[h].astype(BF16))

    def col_spec(off):
        return pl.BlockSpec((rows, D_MODEL), lambda s: (s, off // D_MODEL))

    bf_tile = pltpu.VMEM((rows, D_MODEL), BF16)
    bf_cat = pltpu.VMEM((rows, 3 * D_MODEL), BF16)
    scratch = [pltpu.VMEM((HEADS, HEAD_DIM, HEAD_DIM), F32), bf_cat, bf_cat, bf_tile, bf_tile, bf_tile]
    return pl.pallas_call(
        body,
        name="hg_fwd",
        grid=(s_len // rows,),
        in_specs=[col_spec(OFF_HG_Q), col_spec(OFF_HG_F), col_spec(OFF_HG_I), pl.BlockSpec((2, D_MODEL), lambda s: (0, 0))],
        out_specs=[
            pl.BlockSpec((rows, D_MODEL), lambda s: (s, 0)),
            pl.BlockSpec((n_pairs, HEADS, HEAD_DIM, HEAD_DIM), lambda s: (s, 0, 0, 0)),
        ],
        out_shape=[
            jax.ShapeDtypeStruct((s_len, D_MODEL), F32),
            jax.ShapeDtypeStruct((s_len // HG_PAIR, HEADS, HEAD_DIM, HEAD_DIM), F32),
        ],
        scratch_shapes=scratch,
        compiler_params=_cparams(("arbitrary",)),
    )(proj, proj, proj, lbl)


def _hg_bwd(proj, lbl, states, d_o):
    s_len = proj.shape[0]
    rows = min(HG_STEP, s_len)
    n_pairs = rows // HG_PAIR
    n_steps = s_len // rows

    def body(hq_ref, hf_ref, hi_ref, lbl_ref, st_ref, do_ref, dp_ref, dlb_ref,
             dstate, q_cat, k_cat, qp_b, kp_b, v_b, do_b, d_qcat, d_kcat, d_qp, d_kp, d_v, d_pair):
        @pl.when(pl.program_id(0) == 0)
        def _():
            dstate[...] = jnp.zeros_like(dstate)
            dlb_ref[...] = jnp.zeros_like(dlb_ref)

        lb = _hg_lower_bound(lbl_ref)
        hq = hq_ref[...]
        v, e, second, (f, sig_f, sig_q) = _hg_operands(hq, hf_ref[...], lb, rows)
        _hg_store_operands(v, second, hi_ref[...], (q_cat, k_cat, qp_b, kp_b, v_b))
        do_b[...] = do_ref[...].astype(BF16)
        e_pair = e["pair"]
        row = lax.broadcasted_iota(jnp.int32, (HG_PAIR, HG_PAIR), 0)
        col = lax.broadcasted_iota(jnp.int32, (HG_PAIR, HG_PAIR), 1)
        causal = row >= col

        for u in reversed(range(n_pairs)):
            r0 = u * HG_PAIR
            sls = [(slice(r0, r0 + HG_PAIR), slice(h * HEAD_DIM, (h + 1) * HEAD_DIM)) for h in range(HEADS)]
            ops = [(_hg_pair_operands(q_cat, r0, h * HEAD_DIM), _hg_pair_operands(k_cat, r0, h * HEAD_DIM))
                   for h in range(HEADS)]
            a_s = [jnp.where(causal, _dot_nt(lhs, rhs), 0.0).astype(BF16) for lhs, rhs in ops]
            da_s = [jnp.where(causal, _dot_nt(do_b[sl], v_b[sl]), 0.0).astype(BF16) for sl in sls]
            st0_s = [st_ref[u, h] for h in range(HEADS)]
            ds1_s = [dstate[h] for h in range(HEADS)]
            ds1b_s = [ds1.astype(BF16) for ds1 in ds1_s]
            for h, sl in enumerate(sls):
                decay = e_pair[r0 : r0 + 1, sl[1]]
                d_pair[u : u + 1, sl[1]] = decay * jnp.sum(ds1_s[h] * st0_s[h], axis=0, keepdims=True)
                dstate[h] = ds1_s[h] * decay + _dot_tn(do_b[sl], qp_b[sl])
            for h, sl in enumerate(sls):
                d_qp[sl] = _dot(do_b[sl], st0_s[h].astype(BF16))
                d_kp[sl] = _dot(v_b[sl], ds1b_s[h])
            for h, sl in enumerate(sls):
                d_v[sl] = _dot_tn(a_s[h], do_b[sl]) + _dot_nt(kp_b[sl], ds1b_s[h])
            for h, sl in enumerate(sls):
                d_lhs = _dot(da_s[h], ops[h][1])
                d_rhs = _dot_tn(da_s[h], ops[h][0])
                for g in range(3):
                    gsl = (sl[0], slice(g * D_MODEL + h * HEAD_DIM, g * D_MODEL + (h + 1) * HEAD_DIM))
                    d_qcat[gsl] = d_lhs[:, g * HEAD_DIM : (g + 1) * HEAD_DIM]
                    d_kcat[gsl] = d_rhs[:, g * HEAD_DIM : (g + 1) * HEAD_DIM]

        zero = jnp.zeros_like(hq)
        dqm = jnp.where(second, d_qcat[:, D_MODEL : 2 * D_MODEL], d_qcat[:, 0:D_MODEL])
        dkm = jnp.where(second, d_kcat[:, D_MODEL : 2 * D_MODEL], d_kcat[:, 0:D_MODEL])
        dqp, dkp = d_qp[...], d_kp[...]
        dqd = dqp * e["q_in"] + jnp.where(second, d_qcat[:, 2 * D_MODEL :], zero)
        dkl = dkp * e["k_out"] + jnp.where(second, zero, d_kcat[:, 2 * D_MODEL :])
        dq = dqm * e["qm"] + dqd * e["qd"]
        dk = dkm * e["km"] + dkl * e["kl"]
        t_kl = dkl * v["kl"]
        dcum = dqm * v["qm"] - dkm * v["km"] + dqd * v["qd"] - t_kl
        dp = d_pair[...]
        dp_b = jnp.broadcast_to(dp[:, None, :], (n_pairs, HG_PAIR, D_MODEL)).reshape(rows, D_MODEL)
        dg = (_split_dot_left(_blockdiag(rows, "upper"), dcum) + _split_dot_left(_blockdiag(rows, "all"), t_kl)
              + _split_dot_left(_blockdiag(rows, "next"), dqp * v["qp"])
              + _split_dot_left(_blockdiag(rows, "prev"), dkp * v["kp"]) + dp_b)
        df = dg / f - dk
        one_m = 1.0 - sig_f
        dp_ref[:, 0:D_MODEL] = (dq * (sig_q * (1.0 + hq * (1.0 - sig_q)))).astype(BF16)
        dp_ref[:, D_MODEL : 2 * D_MODEL] = (df * (1.0 - lb) * sig_f * one_m).astype(BF16)
        dp_ref[:, 2 * D_MODEL : 3 * D_MODEL] = d_v[...].astype(BF16)
        dlb_ref[...] += jnp.sum(df * one_m, axis=0, keepdims=True)

    def col_spec(off):
        return pl.BlockSpec((rows, D_MODEL), lambda s: (n_steps - 1 - s, off // D_MODEL))

    f32_tile = pltpu.VMEM((rows, D_MODEL), F32)
    f32_cat = pltpu.VMEM((rows, 3 * D_MODEL), F32)
    bf_tile = pltpu.VMEM((rows, D_MODEL), BF16)
    bf_cat = pltpu.VMEM((rows, 3 * D_MODEL), BF16)
    scratch = [pltpu.VMEM((HEADS, HEAD_DIM, HEAD_DIM), F32), bf_cat, bf_cat, bf_tile, bf_tile, bf_tile, bf_tile,
               f32_cat, f32_cat, f32_tile, f32_tile, f32_tile, pltpu.VMEM((n_pairs, D_MODEL), F32)]
    return pl.pallas_call(
        body,
        name="hg_bwd",
        grid=(n_steps,),
        in_specs=[
            col_spec(OFF_HG_Q), col_spec(OFF_HG_F), col_spec(OFF_HG_I),
            pl.BlockSpec((2, D_MODEL), lambda s: (0, 0)),
            pl.BlockSpec((n_pairs, HEADS, HEAD_DIM, HEAD_DIM), lambda s: (n_steps - 1 - s, 0, 0, 0)),
            pl.BlockSpec((rows, D_MODEL), lambda s: (n_steps - 1 - s, 0)),
        ],
        out_specs=[
            pl.BlockSpec((rows, 3 * D_MODEL), lambda s: (n_steps - 1 - s, 0)),
            pl.BlockSpec((1, D_MODEL), lambda s: (0, 0)),
        ],
        out_shape=[
            jax.ShapeDtypeStruct((s_len, 3 * D_MODEL), BF16),
            jax.ShapeDtypeStruct((1, D_MODEL), F32),
        ],
        scratch_shapes=scratch,
        compiler_params=_cparams(("arbitrary",)),
    )(proj, proj, proj, lbl, states, d_o)


def _mid(proj, sb_o, hg_o, x, target, b_gate, hg_gain, final_g, w_sb, w_hg, w_out):
    s_len = proj.shape[0]
    ts = min(256, s_len)
    inv_d = 1.0 / D_MODEL

    def body(zsb_ref, hz_ref, gl_ref, sbo_ref, hgo_ref, x_ref, tgt_ref, bg_ref, hgn_ref, fg_ref,
             wsb_ref, whg_ref, wout_ref,
             dout_ref, dsbo_ref, dhgo_ref, dzsb_ref, dhz_ref, dgl_ref,
             asb_ref, dusb_ref, ahg_ref, duhg_ref, y_ref, doutb_ref,
             loss_ref, dfg_ref, dbg_ref, dhgn_ref):
        @pl.when(pl.program_id(0) == 0)
        def _():
            loss_ref[...] = jnp.zeros_like(loss_ref)
            dfg_ref[...] = jnp.zeros_like(dfg_ref)
            dbg_ref[...] = jnp.zeros_like(dbg_ref)
            dhgn_ref[...] = jnp.zeros_like(dhgn_ref)

        z_sb = zsb_ref[...]
        sb_o = sbo_ref[...]
        sig_zsb = _sigmoid(z_sb)
        silu_zsb = z_sb * sig_zsb
        a_sb_f = sb_o * silu_zsb
        a_sb = a_sb_f.astype(BF16)
        u_sb = _dot(a_sb, wsb_ref[...])

        hg_o = hgo_ref[...]
        gain = hgn_ref[...]
        r_parts, yn_parts = [], []
        for h in range(HEADS):
            oh = hg_o[:, h * HEAD_DIM : (h + 1) * HEAD_DIM]
            r = lax.rsqrt(jnp.mean(oh * oh, axis=-1, keepdims=True) + RMS_EPS)
            r_parts.append(jnp.broadcast_to(r, oh.shape))
            yn_parts.append(oh * r)
        r_hg = jnp.concatenate(r_parts, axis=-1)
        yn_hg = jnp.concatenate(yn_parts, axis=-1)
        hn = yn_hg * gain
        hz = hz_ref[...]
        sig_hz = _sigmoid(hz)
        silu_hz = hz * sig_hz
        a_hg_f = hn * silu_hz
        a_hg = a_hg_f.astype(BF16)
        u_hg = _dot(a_hg, whg_ref[...])

        gates = _sigmoid(gl_ref[...] + bg_ref[...])
        g_sb = gates[:, 0:D_MODEL]
        g_hg = gates[:, D_MODEL:]
        y_f = g_sb * u_sb + g_hg * u_hg
        y = y_f.astype(BF16)
        out = x_ref[...] + _dot(y, wout_ref[...])
        r2 = lax.rsqrt(jnp.mean(out * out, axis=-1, keepdims=True) + RMS_EPS)
        yn = out * r2
        fg = fg_ref[...]
        diff = yn * fg - tgt_ref[...]
        loss_ref[...] += 0.5 * inv_d * jnp.sum(diff * diff)

        dyf = diff * inv_d
        dfg_ref[...] += jnp.sum(dyf * yn, axis=0, keepdims=True)
        dyn = dyf * fg
        dout = r2 * (dyn - yn * jnp.mean(dyn * yn, axis=-1, keepdims=True))
        dout_ref[...] = dout
        doutb = dout.astype(BF16)
        doutb_ref[...] = doutb
        dy = _dot_nt(doutb, wout_ref[...])
        du_sb = (dy * g_sb).astype(BF16)
        du_hg = (dy * g_hg).astype(BF16)
        dgl_sb = dy * u_sb * g_sb * (1.0 - g_sb)
        dgl_hg = dy * u_hg * g_hg * (1.0 - g_hg)
        dgl_ref[:, 0:D_MODEL] = dgl_sb.astype(BF16)
        dgl_ref[:, D_MODEL:] = dgl_hg.astype(BF16)
        dbg_ref[:, 0:D_MODEL] += jnp.sum(dgl_sb, axis=0, keepdims=True)
        dbg_ref[:, D_MODEL:] += jnp.sum(dgl_hg, axis=0, keepdims=True)

        da_sb = _dot_nt(du_sb, wsb_ref[...])
        dsbo_ref[...] = da_sb * silu_zsb
        dzsb_ref[...] = (da_sb * sb_o * (sig_zsb * (1.0 + z_sb * (1.0 - sig_zsb)))).astype(BF16)

        da_hg = _dot_nt(du_hg, whg_ref[...])
        dhn = da_hg * silu_hz
        dhz_ref[...] = (da_hg * hn * (sig_hz * (1.0 + hz * (1.0 - sig_hz)))).astype(BF16)
        dhgn_ref[...] += jnp.sum(dhn * yn_hg, axis=0, keepdims=True)
        dyn_hg = dhn * gain
        prod = dyn_hg * yn_hg
        m_parts = []
        for h in range(HEADS):
            ph = prod[:, h * HEAD_DIM : (h + 1) * HEAD_DIM]
            m_parts.append(jnp.broadcast_to(jnp.mean(ph, axis=-1, keepdims=True), ph.shape))
        dhgo_ref[...] = r_hg * (dyn_hg - yn_hg * jnp.concatenate(m_parts, axis=-1))

        asb_ref[...] = a_sb_f.T.astype(BF16)
        dusb_ref[...] = du_sb
        ahg_ref[...] = a_hg_f.T.astype(BF16)
        duhg_ref[...] = du_hg
        y_ref[...] = y_f.T.astype(BF16)

    def tile(width, off=0):
        return pl.BlockSpec((ts, width), lambda s: (s, off // width))

    def across():
        return pl.BlockSpec((D_MODEL, ts), lambda s: (0, s))

    def whole(shape):
        return pl.BlockSpec(shape, lambda s: (0,) * len(shape))

    def weight():
        return pl.BlockSpec((D_MODEL, D_MODEL), lambda s: (0, 0), pipeline_mode=pl.Buffered(1))

    f32_act = jax.ShapeDtypeStruct((s_len, D_MODEL), F32)
    bf_act = jax.ShapeDtypeStruct((s_len, D_MODEL), BF16)
    bf_act_t = jax.ShapeDtypeStruct((D_MODEL, s_len), BF16)
    return pl.pallas_call(
        body,
        name="mid",
        grid=(s_len // ts,),
        in_specs=[
            tile(D_MODEL, OFF_SB_Z), tile(D_MODEL, OFF_HG_Z), tile(2 * D_MODEL, OFF_GATE),
            tile(D_MODEL), tile(D_MODEL), tile(D_MODEL), tile(D_MODEL),
            whole((1, 2 * D_MODEL)), whole((1, D_MODEL)), whole((1, D_MODEL)),
            weight(), weight(), weight(),
        ],
        out_specs=[
            tile(D_MODEL), tile(D_MODEL), tile(D_MODEL), tile(D_MODEL), tile(D_MODEL), tile(2 * D_MODEL),
            across(), tile(D_MODEL), across(), tile(D_MODEL), across(), tile(D_MODEL),
            whole((1, 1)), whole((1, D_MODEL)), whole((1, 2 * D_MODEL)), whole((1, D_MODEL)),
        ],
        out_shape=[
            f32_act, f32_act, f32_act, bf_act, bf_act, jax.ShapeDtypeStruct((s_len, 2 * D_MODEL), BF16),
            bf_act_t, bf_act, bf_act_t, bf_act, bf_act_t, bf_act,
            jax.ShapeDtypeStruct((1, 1), F32), jax.ShapeDtypeStruct((1, D_MODEL), F32),
            jax.ShapeDtypeStruct((1, 2 * D_MODEL), F32), jax.ShapeDtypeStruct((1, D_MODEL), F32),
        ],
        compiler_params=_cparams(("arbitrary",)),
    )(proj, proj, proj, sb_o, hg_o, x, target, b_gate, hg_gain, final_g, w_sb, w_hg, w_out)


def _grad_square(a_t, b, name):
    s_len = b.shape[0]
    tk = min(1024, s_len)

    def body(a_ref, b_ref, o_ref):
        @pl.when(pl.program_id(0) == 0)
        def _():
            o_ref[...] = jnp.zeros_like(o_ref)

        o_ref[...] += _dot(a_ref[...], b_ref[...])

    return pl.pallas_call(
        body,
        name=name,
        grid=(s_len // tk,),
        in_specs=[pl.BlockSpec((D_MODEL, tk), lambda k: (0, k)), pl.BlockSpec((tk, D_MODEL), lambda k: (k, 0))],
        out_specs=pl.BlockSpec((D_MODEL, D_MODEL), lambda k: (0, 0)),
        out_shape=jax.ShapeDtypeStruct((D_MODEL, D_MODEL), F32),
        compiler_params=_cparams(("arbitrary",)),
    )(a_t, b)


SEG_WIDTHS = (1024, 1024, 1024, 1024, 3072, 1024, 2048)


def _seg_bounds(tile):
    bounds = [0]
    for w in SEG_WIDTHS:
        bounds.append(bounds[-1] + w // tile)
    return bounds


def _grad_w_in(h_t, segs):
    m, s_len = h_t.shape
    tk = min(1024, s_len)
    tn = 1024
    nk = s_len // tk
    bounds = _seg_bounds(tn)

    def body(a_ref, *refs):
        seg_refs, o_ref = refs[:-1], refs[-1]
        j = pl.program_id(0)

        @pl.when(pl.program_id(1) == 0)
        def _():
            o_ref[...] = jnp.zeros_like(o_ref)

        for i, ref in enumerate(seg_refs):
            @pl.when((j >= bounds[i]) & (j < bounds[i + 1]))
            def _(ref=ref):
                o_ref[...] += _dot(a_ref[...], ref[...])

    def seg_spec(lo, hi):
        def index(j, k):
            return (jnp.where(j < lo, 0, jnp.where(j >= hi, nk - 1, k)), jnp.clip(j - lo, 0, hi - lo - 1))
        return pl.BlockSpec((tk, tn), index)

    return pl.pallas_call(
        body,
        name="grad_w_in",
        grid=(IN_WIDTH // tn, nk),
        in_specs=[pl.BlockSpec((m, tk), lambda j, k: (0, k))] + [seg_spec(bounds[i], bounds[i + 1]) for i in range(7)],
        out_specs=pl.BlockSpec((m, tn), lambda j, k: (0, j)),
        out_shape=jax.ShapeDtypeStruct((m, IN_WIDTH), F32),
        compiler_params=_cparams(("arbitrary", "arbitrary")),
    )(h_t, *segs)


EXCHANGE_IN_PIECES = 8
EXCHANGE_PIECES = EXCHANGE_IN_PIECES + 3


def _exchange_copies(sin_ref, ssq_ref, got_in, got_sq, send_sems, recv_sems):
    _, _, c, chips = _position()
    rows = HALF_IN // EXCHANGE_IN_PIECES
    copies = []
    for k, (px, py) in enumerate(chips):
        chip = 2 * px + py
        for p in range(EXCHANGE_PIECES):
            if p < EXCHANGE_IN_PIECES:
                src, dst = sin_ref.at[chip, pl.ds(p * rows, rows), :], got_in.at[k, pl.ds(p * rows, rows), :]
            else:
                src, dst = ssq_ref.at[p - EXCHANGE_IN_PIECES, chip], got_sq.at[k, p - EXCHANGE_IN_PIECES]
            copies.append(_remote(src, dst, send_sems.at[k, p], recv_sems.at[k, p], (px, py, c)))
    return copies


def _dx(segs, w_all, x, norm_g, dout, s_in, s_sq):
    s_len = x.shape[0]
    ts = min(1024, s_len)
    tk = 512
    nk = IN_WIDTH // tk
    ns = s_len // ts
    bounds = _seg_bounds(tk)

    def body(*refs):
        seg_refs = refs[:7]
        w_ref, x_ref, g_ref, dout_ref, sin_ref, ssq_ref, gx_ref, dg_ref, got_in, got_sq, acc, send_sems, recv_sems = refs[7:]
        s, k = pl.program_id(0), pl.program_id(1)

        @pl.when((s == 0) & (k == 0))
        def _():
            dg_ref[...] = jnp.zeros_like(dg_ref)
            for cp in _exchange_copies(sin_ref, ssq_ref, got_in, got_sq, send_sems, recv_sems):
                cp.start()

        @pl.when(k == 0)
        def _():
            acc[...] = jnp.zeros_like(acc)

        for i, ref in enumerate(seg_refs):
            @pl.when((k >= bounds[i]) & (k < bounds[i + 1]))
            def _(ref=ref):
                acc[...] += _dot_nt(ref[...], w_ref[...])

        @pl.when(k == nk - 1)
        def _():
            dh = acc[...]
            xv = x_ref[...]
            r = lax.rsqrt(jnp.mean(xv * xv, axis=-1, keepdims=True) + RMS_EPS)
            xn = xv * r
            dg_ref[...] += jnp.sum(dh * xn, axis=0, keepdims=True)
            dxn = dh * g_ref[...]
            gx_ref[...] = r * (dxn - xn * jnp.mean(dxn * xn, axis=-1, keepdims=True)) + dout_ref[...]

        @pl.when((s == ns - 1) & (k == nk - 1))
        def _():
            for cp in _exchange_copies(sin_ref, ssq_ref, got_in, got_sq, send_sems, recv_sems):
                cp.wait()

    def seg_spec(lo, hi):
        return pl.BlockSpec((ts, tk), lambda s, k: (s, jnp.clip(k - lo, 0, hi - lo - 1)))

    row_tile = pl.BlockSpec((ts, D_MODEL), lambda s, k: (s, 0))
    vec = pl.BlockSpec((1, D_MODEL), lambda s, k: (0, 0))
    return pl.pallas_call(
        body,
        name="dx",
        grid=(ns, nk),
        in_specs=[seg_spec(bounds[i], bounds[i + 1]) for i in range(7)] + [
            pl.BlockSpec((D_MODEL, tk), lambda s, k: (0, k)),
            row_tile, vec, row_tile, ANY, ANY,
        ],
        out_specs=[row_tile, vec, ANY, ANY],
        out_shape=[jax.ShapeDtypeStruct((s_len, D_MODEL), F32), jax.ShapeDtypeStruct((1, D_MODEL), F32),
                   jax.ShapeDtypeStruct((3, HALF_IN, W_IN_SHARD), WIRE),
                   jax.ShapeDtypeStruct((3, 3, HALF_SQ, D_MODEL), WIRE)],
        scratch_shapes=[pltpu.VMEM((ts, D_MODEL), F32),
                        pltpu.SemaphoreType.DMA((3, EXCHANGE_PIECES)), pltpu.SemaphoreType.DMA((3, EXCHANGE_PIECES))],
        compiler_params=_cparams(("arbitrary", "arbitrary")),
    )(*segs, w_all, x, norm_g, dout, s_in, s_sq)


def _local_grads(x, target, proj, h_t, qkv, b_gate, lbl, hg_gain, final_g, w_sb, w_hg, w_out):
    sb_o, sb_o_fine = _sb_fwd(qkv)
    hg_o, states = _hg_fwd(proj, lbl)
    (dout, d_sbo, d_hgo, d_zsb, d_hz, d_gl, a_sb, du_sb, a_hg, du_hg, y, doutb,
     loss, d_fg, d_bg, d_hgn) = _mid(proj, sb_o, hg_o, x, target, b_gate, hg_gain, final_g, w_sb, w_hg, w_out)
    g_w_sb = _grad_square(a_sb, du_sb, "grad_w_sb")
    g_w_hg = _grad_square(a_hg, du_hg, "grad_w_hg")
    g_w_out = _grad_square(y, doutb, "grad_w_out")
    d_q, d_k, d_v = _sb_bwd(qkv, sb_o_fine, d_sbo)
    d_hg, d_lb = _hg_bwd(proj, lbl, states, d_hgo)
    segs = (d_q, d_k, d_v, d_zsb, d_hg, d_hz, d_gl)
    g_w_in = _grad_w_in(h_t, segs)
    return g_w_in, g_w_sb, g_w_hg, g_w_out, segs, dout, loss, d_bg, d_lb, d_hgn, d_fg


ANY = pl.BlockSpec(memory_space=pl.ANY)
WIRE = BF16
HALF_IN = D_MODEL // 2
HALF_SQ = ROW_SHARD // 2


def _position():
    x, y, c = lax.axis_index("x"), lax.axis_index("y"), lax.axis_index("c")
    chips = [(1 - x, y), (x, 1 - y), (1 - x, 1 - y)]
    return x, y, c, chips


def _remote(src, dst, send_sem, recv_sem, to):
    return pltpu.make_async_remote_copy(src_ref=src, dst_ref=dst, send_sem=send_sem, recv_sem=recv_sem,
                                        device_id=to, device_id_type=MESH)


PROJ_TILE = 1280


def _gather_inproj(idx, h, w_in_b, w_sq_b):
    s_len = h.shape[0]
    ts = min(1024, s_len)
    ns = s_len // ts
    per = W_IN_SHARD // PROJ_TILE
    n_in = 4
    n_piece = n_in + 3
    rows = HALF_IN // n_in

    def chip_at(r, me):
        return me ^ jnp.where(r == 1, 2, jnp.where(r == 2, 1, jnp.where(r == 3, 3, 0)))

    def body(idx_ref, h_ref, win_ref, wsqb_ref, proj_ref, qkv_ref, wall_ref, wsq_ref, wbuf, send_sems, recv_sems, w_sem):
        r, t, s = pl.program_id(0), pl.program_id(1), pl.program_id(2)
        x, y, c, chips = _position()
        me = 2 * x + y
        sibling = (x, y, 1 - c)
        first = (t == 0) & (s == 0)

        def src_piece(p):
            if p < n_in:
                return win_ref.at[pl.ds(c * HALF_IN + p * rows, rows), :]
            return wsqb_ref.at[p - n_in, pl.ds(c * HALF_SQ, HALF_SQ), :]

        def piece(p, chip, core):
            if p < n_in:
                cols = pl.ds(pl.multiple_of(chip * W_IN_SHARD, W_IN_SHARD), W_IN_SHARD)
                return wall_ref.at[pl.ds(core * HALF_IN + p * rows, rows), cols]
            return wsq_ref.at[p - n_in, chip, pl.ds(core * HALF_SQ, HALF_SQ), :]

        def send(k, p):
            px, py = chips[k]
            return _remote(src_piece(p), piece(p, me, c), send_sems.at[k, p], recv_sems.at[k, p], (px, py, c))

        def forward(k, p, core):
            px, py = chips[k]
            got = piece(p, 2 * px + py, core)
            return _remote(got, got, send_sems.at[3 + k, p], recv_sems.at[3 + k, p], sibling)

        @pl.when((r == 0) & first)
        def _():
            for k in range(3):
                for p in range(n_piece):
                    send(k, p).start()

        for k in range(3):
            @pl.when((r == k + 1) & first)
            def _(k=k):
                px, py = chips[k]
                for p in range(n_piece):
                    got = piece(p, 2 * px + py, c)
                    _remote(got, got, send_sems.at[k, p], recv_sems.at[k, p], (px, py, c)).wait_recv()
                    forward(k, p, c).start()
                for p in range(n_piece):
                    forward(k, p, 1 - c).wait_recv()

        @pl.when(s == 0)
        def _():
            col = pl.multiple_of(t * PROJ_TILE, PROJ_TILE)

            @pl.when(r == 0)
            def _():
                cp = pltpu.make_async_copy(win_ref.at[:, pl.ds(col, PROJ_TILE)], wbuf, w_sem)
                cp.start()
                cp.wait()

            @pl.when(r > 0)
            def _():
                off = pl.multiple_of(chip_at(r, me) * W_IN_SHARD + col, PROJ_TILE)
                cp = pltpu.make_async_copy(wall_ref.at[:, pl.ds(off, PROJ_TILE)], wbuf, w_sem)
                cp.start()
                cp.wait()

        p = _dot(h_ref[...], wbuf[...])
        proj_ref[...] = p
        qkv_ref[...] = p.astype(BF16)

        @pl.when((r == 3) & (t == per - 1) & (s == ns - 1))
        def _():
            for k in range(3):
                for p in range(n_piece):
                    send(k, p).wait_send()
                    forward(k, p, c).wait_send()

    def col_tile(r, t, idx):
        return per * chip_at(r, idx[0]) + t

    grid_spec = pltpu.PrefetchScalarGridSpec(
        num_scalar_prefetch=1,
        grid=(N_CHIPS, per, ns),
        in_specs=[pl.BlockSpec((ts, D_MODEL), lambda r, t, s, idx: (s, 0)), ANY, ANY],
        out_specs=[pl.BlockSpec((ts, PROJ_TILE), lambda r, t, s, idx: (s, col_tile(r, t, idx))),
                   pl.BlockSpec((ts, PROJ_TILE), lambda r, t, s, idx: (s, col_tile(r, t, idx))),
                   ANY, ANY],
        scratch_shapes=[pltpu.VMEM((D_MODEL, PROJ_TILE), BF16),
                        pltpu.SemaphoreType.DMA((6, n_piece)), pltpu.SemaphoreType.DMA((6, n_piece)),
                        pltpu.SemaphoreType.DMA(())],
    )
    return pl.pallas_call(
        body,
        name="gather_inproj",
        grid_spec=grid_spec,
        out_shape=[jax.ShapeDtypeStruct((s_len, IN_WIDTH), F32),
                   jax.ShapeDtypeStruct((s_len, IN_WIDTH), BF16),
                   jax.ShapeDtypeStruct((D_MODEL, IN_WIDTH), BF16),
                   jax.ShapeDtypeStruct((3, N_CHIPS, ROW_SHARD, D_MODEL), BF16)],
        compiler_params=_cparams(("arbitrary", "arbitrary", "arbitrary")),
    )(idx, h, w_in_b, w_sq_b)


def _place_own(idx, w_in_b, w_sq_b, w_all, wsq):
    n = 4
    r_in, r_sq = D_MODEL // n, ROW_SHARD // n

    def body(idx_ref, win_ref, wsq_ref, w_all_in, wsq_in, w_all_out, wsq_out):
        w_all_out[...] = win_ref[...]
        wsq_out[:, 0] = wsq_ref[...]

    grid_spec = pltpu.PrefetchScalarGridSpec(
        num_scalar_prefetch=1,
        grid=(n,),
        in_specs=[pl.BlockSpec((r_in, W_IN_SHARD), lambda r, idx: (r, 0)),
                  pl.BlockSpec((3, r_sq, D_MODEL), lambda r, idx: (0, r, 0)), ANY, ANY],
        out_specs=[pl.BlockSpec((r_in, W_IN_SHARD), lambda r, idx: (r, idx[0])),
                   pl.BlockSpec((3, 1, r_sq, D_MODEL), lambda r, idx: (0, idx[0], r, 0))],
    )
    return pl.pallas_call(
        body,
        name="place_own",
        grid_spec=grid_spec,
        out_shape=[jax.ShapeDtypeStruct(w_all.shape, BF16), jax.ShapeDtypeStruct(wsq.shape, BF16)],
        input_output_aliases={3: 0, 4: 1},
        compiler_params=_cparams(("arbitrary",)),
    )(idx, w_in_b, w_sq_b, w_all, wsq)


def _swap_halves(g_in, g_sq):
    n_in = 16
    n_piece = n_in + 3 * N_CHIPS
    rows = HALF_IN // n_in

    def body(gin_ref, gsq_ref, got_in, got_sq, send_sems, recv_sems):
        x, y, c, _ = _position()
        sibling = (x, y, 1 - c)

        def src_piece(p):
            if p < n_in:
                return gin_ref.at[pl.ds((1 - c) * HALF_IN + p * rows, rows), :]
            a, chip = divmod(p - n_in, N_CHIPS)
            return gsq_ref.at[a, chip, pl.ds((1 - c) * HALF_SQ, HALF_SQ), :]

        def dst_piece(p):
            if p < n_in:
                return got_in.at[pl.ds(p * rows, rows), :]
            a, chip = divmod(p - n_in, N_CHIPS)
            return got_sq.at[a, chip]

        out = [_remote(src_piece(p), dst_piece(p), send_sems.at[p], recv_sems.at[p], sibling) for p in range(n_piece)]
        for cp in out:
            cp.start()
        for cp in out:
            cp.wait()

    return pl.pallas_call(
        body,
        name="swap_halves",
        in_specs=[ANY, ANY],
        out_specs=[ANY, ANY],
        out_shape=[jax.ShapeDtypeStruct((HALF_IN, IN_WIDTH), F32),
                   jax.ShapeDtypeStruct((3, N_CHIPS, HALF_SQ, D_MODEL), F32)],
        scratch_shapes=[pltpu.SemaphoreType.DMA((n_piece,))] * 2,
    )(g_in, g_sq)


def _join_halves(r_in, r_sq):
    n_in = 16
    n_piece = n_in + 3
    rows = HALF_IN // n_in

    def body(in_alias, sq_alias, full_in, full_sq, send_sems, recv_sems):
        del in_alias, sq_alias
        x, y, c, _ = _position()
        sibling = (x, y, 1 - c)

        def piece(p, core):
            if p < n_in:
                return full_in.at[pl.ds(core * HALF_IN + p * rows, rows), :]
            return full_sq.at[p - n_in, pl.ds(core * HALF_SQ, HALF_SQ), :]

        out = [_remote(piece(p, c), piece(p, c), send_sems.at[p], recv_sems.at[p], sibling) for p in range(n_piece)]
        for cp in out:
            cp.start()
        for p in range(n_piece):
            _remote(piece(p, 1 - c), piece(p, 1 - c), send_sems.at[p], recv_sems.at[p], sibling).wait_recv()
        for cp in out:
            cp.wait_send()

    return pl.pallas_call(
        body,
        name="join_halves",
        in_specs=[ANY, ANY],
        out_specs=[ANY, ANY],
        out_shape=[jax.ShapeDtypeStruct((D_MODEL, W_IN_SHARD), F32),
                   jax.ShapeDtypeStruct((3, ROW_SHARD, D_MODEL), F32)],
        input_output_aliases={0: 0, 1: 1},
        scratch_shapes=[pltpu.SemaphoreType.DMA((n_piece,)), pltpu.SemaphoreType.DMA((n_piece,))],
    )(r_in, r_sq)


SMALL_ROWS = 56
N_DEV = 8


def _sum_small(part):
    def body(part_ref, out_ref, slots, send_sems, recv_sems):
        x, y, c, _ = _position()
        me = 4 * x + 2 * y + c
        slots[me] = part_ref[...]
        out = []
        for r in range(1, N_DEV):
            rx, ry, rc = (r >> 2) & 1, (r >> 1) & 1, r & 1
            to = (1 - x if rx else x, 1 - y if ry else y, 1 - c if rc else c)
            out.append(_remote(part_ref, slots.at[me], send_sems.at[r - 1], recv_sems.at[r - 1], to))
        for cp in out:
            cp.start()
        for r in range(1, N_DEV):
            _remote(part_ref, slots.at[me ^ r], send_sems.at[r - 1], recv_sems.at[r - 1], (x, y, c)).wait_recv()
        for cp in out:
            cp.wait_send()
        total = slots[0]
        for d in range(1, N_DEV):
            total = total + slots[d]
        out_ref[...] = total

    vmem = pl.BlockSpec(memory_space=pltpu.VMEM)
    return pl.pallas_call(
        body,
        name="sum_small",
        in_specs=[vmem],
        out_specs=vmem,
        out_shape=jax.ShapeDtypeStruct((SMALL_ROWS, HEAD_DIM), F32),
        scratch_shapes=[pltpu.VMEM((N_DEV, SMALL_ROWS, HEAD_DIM), F32),
                        pltpu.SemaphoreType.DMA((N_DEV - 1,)), pltpu.SemaphoreType.DMA((N_DEV - 1,))],
    )(part)


def _prefetch_call(body, name, idx, grid, in_specs, out_specs, out_shape, args):
    grid_spec = pltpu.PrefetchScalarGridSpec(num_scalar_prefetch=1, grid=grid, in_specs=in_specs, out_specs=out_specs)
    return pl.pallas_call(body, name=name, grid_spec=grid_spec, out_shape=out_shape,
                          compiler_params=_cparams(("arbitrary",) * len(grid)))(idx, *args)


def _sum_a_in(idx, g_in, got_in):
    tr = 128
    nr = HALF_IN // tr

    def body(idx_ref, a_ref, b_ref, o_ref):
        o_ref[0] = (a_ref[...] + b_ref[...]).astype(WIRE)

    return _prefetch_call(
        body, "sum_a_in", idx, (N_CHIPS, nr),
        [pl.BlockSpec((tr, W_IN_SHARD), lambda j, r, idx: (idx[1] * nr + r, j)),
         pl.BlockSpec((tr, W_IN_SHARD), lambda j, r, idx: (r, j))],
        pl.BlockSpec((1, tr, W_IN_SHARD), lambda j, r, idx: (j, r, 0)),
        jax.ShapeDtypeStruct((N_CHIPS, HALF_IN, W_IN_SHARD), WIRE), (g_in, got_in))


def _sum_a_sq(idx, g_sq, got_sq):
    blk = (1, 1, HALF_SQ, D_MODEL)

    def body(idx_ref, a_ref, b_ref, o_ref):
        o_ref[...] = (a_ref[...] + b_ref[...]).astype(WIRE)

    return _prefetch_call(
        body, "sum_a_sq", idx, (3, N_CHIPS),
        [pl.BlockSpec(blk, lambda a, j, idx: (a, j, idx[1], 0)), pl.BlockSpec(blk, lambda a, j, idx: (a, j, 0, 0))],
        pl.BlockSpec(blk, lambda a, j, idx: (a, j, 0, 0)),
        jax.ShapeDtypeStruct((3, N_CHIPS, HALF_SQ, D_MODEL), WIRE), (g_sq, got_sq))


def _sum_b_in(idx, s_in, got_in):
    tr = 128
    nr = HALF_IN // tr

    def body(idx_ref, a_ref, b_ref, o_ref):
        o_ref[...] = ((a_ref[0].astype(F32) + b_ref[0].astype(F32)) + b_ref[1].astype(F32)) + b_ref[2].astype(F32)

    return _prefetch_call(
        body, "sum_b_in", idx, (nr,),
        [pl.BlockSpec((1, tr, W_IN_SHARD), lambda r, idx: (idx[0], r, 0)),
         pl.BlockSpec((3, tr, W_IN_SHARD), lambda r, idx: (0, r, 0))],
        pl.BlockSpec((tr, W_IN_SHARD), lambda r, idx: (idx[1] * nr + r, 0)),
        jax.ShapeDtypeStruct((D_MODEL, W_IN_SHARD), F32), (s_in, got_in))


def _sum_b_sq(idx, s_sq, got_sq):
    def body(idx_ref, a_ref, b_ref, o_ref):
        o_ref[0] = ((a_ref[0, 0].astype(F32) + b_ref[0, 0].astype(F32)) + b_ref[1, 0].astype(F32)) + b_ref[2, 0].astype(F32)

    return _prefetch_call(
        body, "sum_b_sq", idx, (3,),
        [pl.BlockSpec((1, 1, HALF_SQ, D_MODEL), lambda a, idx: (a, idx[0], 0, 0)),
         pl.BlockSpec((3, 1, HALF_SQ, D_MODEL), lambda a, idx: (0, a, 0, 0))],
        pl.BlockSpec((1, HALF_SQ, D_MODEL), lambda a, idx: (a, idx[1], 0)),
        jax.ShapeDtypeStruct((3, ROW_SHARD, D_MODEL), F32), (s_sq, got_sq))


def _adamw_math(w, g, m, v):
    m = ADAM_B1 * m + (1.0 - ADAM_B1) * g
    v = ADAM_B2 * v + (1.0 - ADAM_B2) * (g * g)
    m_hat = m / (1.0 - ADAM_B1 ** ADAM_STEP)
    v_hat = v / (1.0 - ADAM_B2 ** ADAM_STEP)
    delta = -ADAM_LR * (m_hat / (jnp.sqrt(v_hat) + ADAM_EPS) + ADAM_WD * w)
    return delta, m, v


def _adamw(w, g, m, v, name):
    rows, cols = w.shape
    tr = min(128, rows)

    def body(w_ref, g_ref, m_ref, v_ref, d_ref, nm_ref, nv_ref):
        d_ref[...], nm_ref[...], nv_ref[...] = _adamw_math(w_ref[...], g_ref[...], m_ref[...], v_ref[...])

    spec = pl.BlockSpec((tr, cols), lambda r: (r, 0))
    return pl.pallas_call(
        body,
        name=name,
        grid=(rows // tr,),
        in_specs=[spec] * 4,
        out_specs=[spec] * 3,
        out_shape=[jax.ShapeDtypeStruct((rows, cols), F32)] * 3,
        compiler_params=_cparams(("arbitrary",)),
    )(w, g, m, v)


def _adamw_small(sums, w, m, v):
    def body(s_ref, w_ref, m_ref, v_ref, loss_ref, g_ref, d_ref, nm_ref, nv_ref):
        s = s_ref[...]
        w = w_ref[...]
        loss_ref[...] = s[0:1, 0:1]
        l0, l1 = w[24:32], w[32:40]
        mx = jnp.maximum(l0, l1)
        e0, e1 = jnp.exp(l0 - mx), jnp.exp(l1 - mx)
        p0, p1 = e0 / (e0 + e1), e1 / (e0 + e1)
        d_lb = s[32:40]
        g = jnp.concatenate([s[8:16], s[16:32], d_lb * p0 * (1.0 - p0), -d_lb * p0 * p1, s[40:48], s[48:56]], axis=0)
        g_ref[...] = g
        d_ref[...], nm_ref[...], nv_ref[...] = _adamw_math(w, g, m_ref[...], v_ref[...])

    packed = jax.ShapeDtypeStruct((SMALL_ROWS, HEAD_DIM), F32)
    return pl.pallas_call(
        body,
        name="adamw_small",
        out_shape=[jax.ShapeDtypeStruct((1, 1), F32), packed, packed, packed, packed],
    )(sums, w, m, v)


def _pack_small(ng, bg, lbl, hgn, fg):
    return jnp.concatenate([a.reshape(-1, HEAD_DIM) for a in (ng, bg, lbl, hgn, fg)], axis=0)


def _unpack_small(p):
    return (p[0:8].reshape(1, D_MODEL), p[8:24].reshape(1, 2 * D_MODEL), p[24:40].reshape(2, HEADS, HEAD_DIM),
            p[40:48].reshape(1, HEADS, HEAD_DIM), p[48:56].reshape(D_MODEL))


def kernel(x, norm_g, w_in, b_gate, lb_logits, hg_norm_g, w_sb_proj, w_hg_proj, w_out, final_norm_g, loss_target, m_norm_g, m_w_in, m_b_gate, m_lb_logits, m_hg_norm_g, m_w_sb_proj, m_w_hg_proj, m_w_out, m_final_norm_g, v_norm_g, v_w_in, v_b_gate, v_lb_logits, v_hg_norm_g, v_w_sb_proj, v_w_hg_proj, v_w_out, v_final_norm_g):
    s_len = x.shape[1]
    w_sq = jnp.stack([w_sb_proj[0], w_hg_proj[0], w_out[0]])
    idx = jnp.stack([2 * lax.axis_index("x") + lax.axis_index("y"), lax.axis_index("c")]).astype(jnp.int32)
    w_in_b, w_sq_b = w_in[0].astype(BF16), w_sq.astype(BF16)
    h, h_t = _prenorm(x[0], norm_g)
    proj, qkv, w_all, wsq = _gather_inproj(idx, h, w_in_b, w_sq_b)
    w_all, wsq = _place_own(idx, w_in_b, w_sq_b, w_all, wsq)
    wsq = wsq.reshape(3, D_MODEL, D_MODEL)

    (g_in, g_sb, g_hg, g_out, segs, dout, loss, d_bg, d_lb, d_hgn, d_fg) = _local_grads(
        x[0], loss_target[0], proj, h_t, qkv, b_gate, lb_logits.reshape(2, D_MODEL), hg_norm_g.reshape(1, D_MODEL),
        final_norm_g.reshape(1, D_MODEL), wsq[0], wsq[1], wsq[2])

    g_sq = jnp.stack([g_sb, g_hg, g_out]).reshape(3, N_CHIPS, ROW_SHARD, D_MODEL)
    got_in, got_sq = _swap_halves(g_in, g_sq)
    s_in, s_sq = _sum_a_in(idx, g_in, got_in), _sum_a_sq(idx, g_sq, got_sq)
    grad_x, d_ng, got_in, got_sq = _dx(segs, w_all, x[0], norm_g, dout, s_in, s_sq)
    grad_in, grad_sq = _join_halves(_sum_b_in(idx, s_in, got_in), _sum_b_sq(idx, s_sq, got_sq))

    d_in, nm_in, nv_in = _adamw(w_in[0], grad_in, m_w_in[0], v_w_in[0], "adamw_in")
    flat = lambda a, b, c: jnp.concatenate([a[0], b[0], c[0]], axis=0)
    d_sq, nm_sq, nv_sq = _adamw(flat(w_sb_proj, w_hg_proj, w_out), grad_sq.reshape(3 * ROW_SHARD, D_MODEL),
                                flat(m_w_sb_proj, m_w_hg_proj, m_w_out), flat(v_w_sb_proj, v_w_hg_proj, v_w_out),
                                "adamw_sq")

    pad = jnp.zeros((8, HEAD_DIM), F32).at[0, 0].set(loss[0, 0])
    part = jnp.concatenate([pad] + [a.reshape(-1, HEAD_DIM) for a in (d_ng, d_bg, d_lb, d_hgn, d_fg)], axis=0)
    sums = _sum_small(part)
    loss_out, g_sm, d_sm, nm_sm, nv_sm = _adamw_small(
        sums, _pack_small(norm_g, b_gate, lb_logits, hg_norm_g, final_norm_g),
        _pack_small(m_norm_g, m_b_gate, m_lb_logits, m_hg_norm_g, m_final_norm_g),
        _pack_small(v_norm_g, v_b_gate, v_lb_logits, v_hg_norm_g, v_final_norm_g))

    def big(t_in, t_sq):
        sq = t_sq.reshape(3, 1, ROW_SHARD, D_MODEL)
        return t_in[None], sq[0], sq[1], sq[2]

    def order(small, in_, sb, hg, out):
        ng, bg, lbl, hgn, fg = small
        return [ng, in_, bg, lbl, hgn, sb, hg, out, fg]

    outs = [loss_out[0, 0], grad_x[None]]
    for small, (t_in, t_sq) in ((g_sm, (grad_in, grad_sq)), (d_sm, (d_in, d_sq)), (nm_sm, (nm_in, nm_sq)), (nv_sm, (nv_in, nv_sq))):
        outs += order(_unpack_small(small), *big(t_in, t_sq))
    return tuple(outs)
```

```python
import functools

import jax
import jax.numpy as jnp
from jax import lax
from jax.experimental import pallas as pl
from jax.experimental.pallas import tpu as pltpu

F32 = jnp.float32
BF16 = jnp.bfloat16

D_MODEL = 1024
HEADS = 8
HEAD_DIM = 128
IN_WIDTH = 10240
N_CHIPS = 4
W_IN_SHARD = IN_WIDTH // N_CHIPS
ROW_SHARD = D_MODEL // N_CHIPS
RMS_EPS = 1e-6

OFF_SB_Q, OFF_SB_K, OFF_SB_V, OFF_SB_Z = 0, 1024, 2048, 3072
OFF_HG_Q, OFF_HG_F, OFF_HG_I, OFF_HG_Z, OFF_GATE = 4096, 5120, 6144, 7168, 8192

SB_BLOCK = 256
SB_FWD_HEADS = 4
SB_BWD_HEADS = 2
SB_ROWS = 256
SB_DEAD = -110.0
SB_GONE = -1e30
HG_CHUNK = 32
HG_PAIR = 2 * HG_CHUNK
HG_STEP = 256
HG_MID = HG_CHUNK // 2 - 1

ADAM_LR, ADAM_B1, ADAM_B2, ADAM_EPS, ADAM_WD, ADAM_STEP = 0.001, 0.9, 0.999, 1e-08, 0.01, 10

VMEM_LIMIT = 56 * 1024 * 1024
VMEM_LIMIT_DX = 60 * 1024 * 1024

MESH = pl.DeviceIdType.MESH


def _cparams(sem, vmem=VMEM_LIMIT):
    return pltpu.CompilerParams(dimension_semantics=sem, vmem_limit_bytes=vmem)


def _dot(a, b):
    return jnp.dot(a, b, preferred_element_type=F32)


def _dot_nt(a, b):
    return lax.dot_general(a, b, (((1,), (1,)), ((), ())), preferred_element_type=F32)


def _dot_tn(a, b):
    return lax.dot_general(a, b, (((0,), (0,)), ((), ())), preferred_element_type=F32)


def _split_dot(x, tri):
    hi = x.astype(BF16)
    lo = (x - hi.astype(F32)).astype(BF16)
    both = _dot(jnp.concatenate([hi, lo], axis=0), tri)
    return both[: x.shape[0]] + both[x.shape[0] :]


def _split_dot_left(tri, x):
    hi = x.astype(BF16)
    lo = (x - hi.astype(F32)).astype(BF16)
    return _dot(tri, hi) + _dot(tri, lo)


def _sigmoid(x):
    return 1.0 / (1.0 + jnp.exp(-x))


def _prenorm(x, norm_g):
    s_len = x.shape[0]
    ts = min(1024, s_len)

    def body(x_ref, g_ref, h_ref, ht_ref):
        xv = x_ref[...]
        r = lax.rsqrt(jnp.mean(xv * xv, axis=-1, keepdims=True) + RMS_EPS)
        hv = (xv * r) * g_ref[...]
        h_ref[...] = hv.astype(BF16)
        ht_ref[...] = hv.T.astype(BF16)

    return pl.pallas_call(
        body,
        name="prenorm",
        grid=(s_len // ts,),
        in_specs=[pl.BlockSpec((ts, D_MODEL), lambda s: (s, 0)), pl.BlockSpec((1, D_MODEL), lambda s: (0, 0))],
        out_specs=[pl.BlockSpec((ts, D_MODEL), lambda s: (s, 0)), pl.BlockSpec((D_MODEL, ts), lambda s: (0, s))],
        out_shape=[jax.ShapeDtypeStruct((s_len, D_MODEL), BF16), jax.ShapeDtypeStruct((D_MODEL, s_len), BF16)],
        compiler_params=_cparams(("arbitrary",)),
    )(x, norm_g)


def _sb_scores(qb, kb, causal, tri_excl, diag):
    z = _dot_nt(qb, kb) * HEAD_DIM ** -0.5
    ls_pos = jnp.minimum(z, 0.0) - jnp.log1p(jnp.exp(-jnp.abs(z)))
    log_not = ls_pos - z
    log_not_m = jnp.where(causal, log_not, 0.0) if diag else log_not
    return ls_pos, log_not, log_not_m, _split_dot(log_not_m, tri_excl)


def _sb_weights(ls_pos, suffix, carry, causal, diag):
    surv = suffix + carry
    w = jnp.exp(ls_pos + surv)
    return surv, (jnp.where(causal, w, 0.0) if diag else w)


def _sb_specs(s_len, blk, heads):
    width = heads * HEAD_DIM

    def blk_spec(off):
        return pl.BlockSpec((blk, width), lambda h, i: (i, off // width + h))

    def head_spec(off, buffers=2):
        return pl.BlockSpec((s_len, width), lambda h, i: (0, off // width + h), pipeline_mode=pl.Buffered(buffers))

    return blk_spec, head_spec


def _head_cols(p):
    return slice(p * HEAD_DIM, (p + 1) * HEAD_DIM)


def _sb_chains(blk, heads):
    rows = min(SB_ROWS, blk)
    return [(p, a) for p in range(heads) for a in range(blk // rows)], rows


def _sb_masks(blk, rows):
    row = lax.broadcasted_iota(jnp.int32, (rows, blk), 0)
    col = lax.broadcasted_iota(jnp.int32, (rows, blk), 1)
    causal = [row + a * rows > col for a in range(blk // rows)]
    row = lax.broadcasted_iota(jnp.int32, (blk, blk), 0)
    col = lax.broadcasted_iota(jnp.int32, (blk, blk), 1)
    tri_excl = (row > col).astype(BF16)
    tri_incl = (row >= col).astype(BF16)
    return causal, tri_excl, tri_incl


def _sb_alive(st, n_chain):
    alive = functools.reduce(jnp.maximum, [st[1 + 3 * c] for c in range(n_chain)])
    return jnp.max(alive) > SB_DEAD


def _sb_fwd(qkv):
    s_len = qkv.shape[0]
    blk = min(SB_BLOCK, s_len)
    nq = s_len // blk
    chains, rows = _sb_chains(blk, SB_FWD_HEADS)

    def body(q_ref, k_ref, v_ref, o_ref, of_ref):
        i = pl.program_id(1)
        causal, tri_excl, _ = _sb_masks(blk, rows)

        def tiles(specs, st):
            pre = []
            for j, diag, _ in specs:
                start = pl.multiple_of(j * blk, blk)
                for p, a in chains:
                    kb = k_ref[pl.ds(start, blk), _head_cols(p)]
                    qb = q_ref[a * rows : (a + 1) * rows, _head_cols(p)]
                    pre.append(_sb_scores(qb, kb, causal[a], tri_excl, diag) + (v_ref[pl.ds(start, blk), _head_cols(p)],))
            for t, (j, diag, valid) in enumerate(specs):
                new = []
                for c, (p, a) in enumerate(chains):
                    carry, acc, acc_lo = st[3 * c : 3 * c + 3]
                    if valid is not None:
                        carry = jnp.where(valid, carry, SB_GONE)
                    ls_pos, _, log_not_m, suffix, vb = pre[t * len(chains) + c]
                    surv, w = _sb_weights(ls_pos, suffix, carry, causal[a], diag)
                    wb = w.astype(BF16)
                    w_lo = (w - wb.astype(F32)).astype(BF16)
                    both = _dot(jnp.concatenate([wb, w_lo], axis=0), vb)
                    new += [surv[:, 0:1] + log_not_m[:, 0:1], acc + both[:rows], acc_lo + both[rows:]]
                st = tuple(new)
            return st

        zero = jnp.zeros((rows, HEAD_DIM), F32)
        st = tiles([(i, True, None), (jnp.maximum(i - 1, 0), False, i >= 1)],
                   (jnp.zeros((rows, 1), F32), zero, zero) * len(chains))

        def more(st):
            return (st[0] < i) & _sb_alive(st, len(chains))

        def step(st):
            return (st[0] + 1,) + tiles([(i - 1 - st[0], False, None)], st[1:])

        st = lax.while_loop(more, step, (1,) + st)[1:]
        for c, (p, a) in enumerate(chains):
            o_ref[a * rows : (a + 1) * rows, _head_cols(p)] = st[3 * c + 1]
            of_ref[a * rows : (a + 1) * rows, _head_cols(p)] = st[3 * c + 1] + st[3 * c + 2]

    blk_spec, head_spec = _sb_specs(s_len, blk, SB_FWD_HEADS)
    return pl.pallas_call(
        body,
        name="sb_fwd",
        grid=(HEADS // SB_FWD_HEADS, nq),
        in_specs=[blk_spec(OFF_SB_Q), head_spec(OFF_SB_K), head_spec(OFF_SB_V)],
        out_specs=[blk_spec(0), blk_spec(0)],
        out_shape=[jax.ShapeDtypeStruct((s_len, D_MODEL), F32)] * 2,
        compiler_params=_cparams(("arbitrary", "arbitrary")),
    )(qkv, qkv, qkv)


def _sb_bwd(qkv, o_fine, d_o):
    s_len = qkv.shape[0]
    blk = min(SB_BLOCK, s_len)
    nq = s_len // blk
    scale = HEAD_DIM ** -0.5
    chains, rows = _sb_chains(blk, SB_BWD_HEADS)

    def body(q_ref, k_ref, v_ref, of_ref, do_ref, dq_ref, dk_ref, dv_ref, dk_acc, dv_acc):
        i = pl.program_id(1)

        @pl.when(i == 0)
        def _():
            dk_acc[...] = jnp.zeros_like(dk_acc)
            dv_acc[...] = jnp.zeros_like(dv_acc)

        dob = do_ref[...].astype(BF16)
        prod = dob.astype(F32) * of_ref[...]
        causal, tri_excl, tri_incl = _sb_masks(blk, rows)

        def group(x, p, a):
            return x[a * rows : (a + 1) * rows, _head_cols(p)]

        totals = [jnp.sum(group(prod, p, a), axis=-1, keepdims=True) for p, a in chains]

        def tiles(specs, st):
            pre = []
            for j, diag, _ in specs:
                start = pl.multiple_of(j * blk, blk)
                for p, a in chains:
                    kb = k_ref[pl.ds(start, blk), _head_cols(p)]
                    vb = v_ref[pl.ds(start, blk), _head_cols(p)]
                    qb, dob_c = group(q_ref, p, a), group(dob, p, a)
                    pre.append(_sb_scores(qb, kb, causal[a], tri_excl, diag) + (_dot_nt(dob_c, vb), qb, kb, dob_c))
            for t, (j, diag, valid) in enumerate(specs):
                start = pl.multiple_of(j * blk, blk)
                mids = []
                for c, (p, a) in enumerate(chains):
                    c_not = st[3 * c]
                    if valid is not None:
                        c_not = jnp.where(valid, c_not, SB_GONE)
                    ls_pos, _, _, suffix, d_w = pre[t * len(chains) + c][:5]
                    surv, w = _sb_weights(ls_pos, suffix, c_not, causal[a], diag)
                    dlw = d_w * w
                    mids.append((surv, w, dlw, _split_dot(dlw, tri_incl)))
                new = []
                dk_new = [None] * SB_BWD_HEADS
                dv_new = [None] * SB_BWD_HEADS
                for c, (p, a) in enumerate(chains):
                    c_dlw, dq = st[3 * c + 1 : 3 * c + 3]
                    ls_pos, log_not, log_not_m, _, _, qb, kb, dob_c = pre[t * len(chains) + c]
                    surv, w, dlw, suffix = mids[c]
                    d_not = totals[c] - c_dlw - suffix
                    dz = (dlw * jnp.exp(log_not) - d_not * jnp.exp(ls_pos)) * scale
                    if diag:
                        dz = jnp.where(causal[a], dz, 0.0)
                    if valid is not None:
                        dz = jnp.where(valid, dz, 0.0)
                    dzb = dz.astype(BF16)
                    dk_c, dv_c = _dot_tn(dzb, qb), _dot_tn(w.astype(BF16), dob_c)
                    dk_new[p] = dk_c if dk_new[p] is None else dk_new[p] + dk_c
                    dv_new[p] = dv_c if dv_new[p] is None else dv_new[p] + dv_c
                    new += [surv[:, 0:1] + log_not_m[:, 0:1], c_dlw + suffix[:, 0:1], dq + _dot(dzb, kb)]
                for p in range(SB_BWD_HEADS):
                    dk_acc[pl.ds(start, blk), _head_cols(p)] += dk_new[p]
                    dv_acc[pl.ds(start, blk), _head_cols(p)] += dv_new[p]
                st = tuple(new)
            return st

        zcol = jnp.zeros((rows, 1), F32)
        st = tiles([(i, True, None), (jnp.maximum(i - 1, 0), False, i >= 1)],
                   (zcol, zcol, jnp.zeros((rows, HEAD_DIM), F32)) * len(chains))

        def more(st):
            return (st[0] < i) & _sb_alive(st, len(chains))

        def step(st):
            return (st[0] + 1,) + tiles([(i - 1 - st[0], False, None)], st[1:])

        st = lax.while_loop(more, step, (1,) + st)[1:]
        for c, (p, a) in enumerate(chains):
            dq_ref[a * rows : (a + 1) * rows, _head_cols(p)] = st[3 * c + 2].astype(BF16)

        @pl.when(i == nq - 1)
        def _():
            dk_ref[...] = dk_acc[...].astype(BF16)
            dv_ref[...] = dv_acc[...].astype(BF16)

    blk_spec, head_spec = _sb_specs(s_len, blk, SB_BWD_HEADS)
    width = SB_BWD_HEADS * HEAD_DIM
    return pl.pallas_call(
        body,
        name="sb_bwd",
        grid=(HEADS // SB_BWD_HEADS, nq),
        in_specs=[blk_spec(OFF_SB_Q), head_spec(OFF_SB_K, 1), head_spec(OFF_SB_V, 1), blk_spec(0), blk_spec(0)],
        out_specs=[blk_spec(0), head_spec(0), head_spec(0)],
        out_shape=[jax.ShapeDtypeStruct((s_len, D_MODEL), BF16)] * 3,
        scratch_shapes=[pltpu.VMEM((s_len, width), F32), pltpu.VMEM((s_len, width), F32)],
        compiler_params=_cparams(("arbitrary", "arbitrary")),
    )(qkv, qkv, qkv, o_fine, d_o)


def _hg_lower_bound(lbl_ref):
    l0 = lbl_ref[0:1, :]
    l1 = lbl_ref[1:2, :]
    mx = jnp.maximum(l0, l1)
    e0 = jnp.exp(l0 - mx)
    e1 = jnp.exp(l1 - mx)
    return e0 / (e0 + e1)


def _hg_gates(hq, hf, lb):
    sig_f = _sigmoid(hf)
    f = lb + (1.0 - lb) * sig_f
    g = jnp.log(f)
    kk = 1.0 - f
    sig_q = _sigmoid(hq)
    qq = hq * sig_q
    return qq, kk, g, f, sig_f, sig_q


def _period_bcast(x, r, rows, period):
    w = x.shape[-1]
    x3 = x.reshape(rows // period, period, w)
    return jnp.broadcast_to(x3[:, r : r + 1, :], x3.shape).reshape(rows, w)


def _blockdiag(rows, kind):
    row = lax.broadcasted_iota(jnp.int32, (rows, rows), 0)
    col = lax.broadcasted_iota(jnp.int32, (rows, rows), 1)
    if kind in ("next", "prev"):
        first, second = (row, col) if kind == "next" else (col, row)
        keep = ((row // HG_PAIR) == (col // HG_PAIR)) & (first % HG_PAIR < HG_CHUNK) & (second % HG_PAIR >= HG_CHUNK)
    else:
        keep = (row // HG_CHUNK) == (col // HG_CHUNK)
        if kind == "lower":
            keep = keep & (row >= col)
        elif kind == "upper":
            keep = keep & (row <= col)
    return jnp.where(keep, 1.0, 0.0).astype(BF16)


def _hg_operands(hq, hf, lb, rows):
    qq, kk, g, f, sig_f, sig_q = _hg_gates(hq, hf, lb)
    cum = _split_dot_left(_blockdiag(rows, "lower"), g)
    mid = _period_bcast(cum, HG_MID, rows, HG_CHUNK)
    last = _period_bcast(cum, HG_CHUNK - 1, rows, HG_CHUNK)
    last0 = _period_bcast(cum, HG_CHUNK - 1, rows, HG_PAIR)
    last1 = _period_bcast(cum, HG_PAIR - 1, rows, HG_PAIR)
    second = (lax.broadcasted_iota(jnp.int32, cum.shape, 0) % HG_PAIR) >= HG_CHUNK
    e = dict(qm=jnp.exp(cum - mid), km=jnp.exp(mid - cum), qd=jnp.exp(cum), kl=jnp.exp(last - cum),
             q_in=jnp.where(second, jnp.exp(last0), 1.0), k_out=jnp.where(second, 1.0, jnp.exp(last1)),
             pair=jnp.exp(last0 + last1))
    v = dict(qm=qq * e["qm"], km=kk * e["km"], qd=qq * e["qd"], kl=kk * e["kl"])
    v["qp"] = v["qd"] * e["q_in"]
    v["kp"] = v["kl"] * e["k_out"]
    return v, e, second, (f, sig_f, sig_q)


def _hg_store_operands(v, second, hi, refs):
    zero = jnp.zeros_like(v["qm"])
    q_cat, k_cat, qp_b, kp_b, v_b = refs
    q_cat[:, 0:D_MODEL] = jnp.where(second, zero, v["qm"]).astype(BF16)
    q_cat[:, D_MODEL : 2 * D_MODEL] = jnp.where(second, v["qm"], zero).astype(BF16)
    q_cat[:, 2 * D_MODEL :] = jnp.where(second, v["qd"], zero).astype(BF16)
    k_cat[:, 0:D_MODEL] = jnp.where(second, zero, v["km"]).astype(BF16)
    k_cat[:, D_MODEL : 2 * D_MODEL] = jnp.where(second, v["km"], zero).astype(BF16)
    k_cat[:, 2 * D_MODEL :] = jnp.where(second, zero, v["kl"]).astype(BF16)
    qp_b[...] = v["qp"].astype(BF16)
    kp_b[...] = v["kp"].astype(BF16)
    v_b[...] = hi.astype(BF16)


def _hg_pair_operands(cat, r0, c0):
    return jnp.concatenate([cat[r0 : r0 + HG_PAIR, g * D_MODEL + c0 : g * D_MODEL + c0 + HEAD_DIM] for g in range(3)], axis=1)


def _hg_fwd(proj, lbl):
    s_len = proj.shape[0]
    rows = min(HG_STEP, s_len)
    n_pairs = rows // HG_PAIR

    def body(hq_ref, hf_ref, hi_ref, lbl_ref, o_ref, st_ref, state, q_cat, k_cat, qp_b, kp_b, v_b):
        @pl.when(pl.program_id(0) == 0)
        def _():
            state[...] = jnp.zeros_like(state)

        v, e, second, _ = _hg_operands(hq_ref[...], hf_ref[...], _hg_lower_bound(lbl_ref), rows)
        _hg_store_operands(v, second, hi_ref[...], (q_cat, k_cat, qp_b, kp_b, v_b))
        e_pair = e["pair"]
        row = lax.broadcasted_iota(jnp.int32, (HG_PAIR, HG_PAIR), 0)
        col = lax.broadcasted_iota(jnp.int32, (HG_PAIR, HG_PAIR), 1)
        causal = row >= col

        for u in range(n_pairs):
            r0 = u * HG_PAIR
            sls = [(slice(r0, r0 + HG_PAIR), slice(h * HEAD_DIM, (h + 1) * HEAD_DIM)) for h in range(HEADS)]
            a_s = [jnp.where(causal, _dot_nt(_hg_pair_operands(q_cat, r0, h * HEAD_DIM),
                                             _hg_pair_operands(k_cat, r0, h * HEAD_DIM)), 0.0).astype(BF16)
                   for h in range(HEADS)]
            st_s = [state[h] for h in range(HEADS)]
            for h, sl in enumerate(sls):
                st_ref[u, h] = st_s[h]
                state[h] = st_s[h] * e_pair[r0 : r0 + 1, sl[1]] + _dot_tn(v_b[sl], kp_b[sl])
            for h, sl in enumerate(sls):
                o_ref[sl] = _dot(a_s[h], v_b[sl]) + _dot_nt(qp_b[sl], st_s[h].astype(BF16))

    def col_spec(off):
        return pl.BlockSpec((rows, D_MODEL), lambda s: (s, off // D_MODEL))

    bf_tile = pltpu.VMEM((rows, D_MODEL), BF16)
    bf_cat = pltpu.VMEM((rows, 3 * D_MODEL), BF16)
    scratch = [pltpu.VMEM((HEADS, HEAD_DIM, HEAD_DIM), F32), bf_cat, bf_cat, bf_tile, bf_tile, bf_tile]
    return pl.pallas_call(
        body,
        name="hg_fwd",
        grid=(s_len // rows,),
        in_specs=[col_spec(OFF_HG_Q), col_spec(OFF_HG_F), col_spec(OFF_HG_I), pl.BlockSpec((2, D_MODEL), lambda s: (0, 0))],
        out_specs=[
            pl.BlockSpec((rows, D_MODEL), lambda s: (s, 0)),
            pl.BlockSpec((n_pairs, HEADS, HEAD_DIM, HEAD_DIM), lambda s: (s, 0, 0, 0)),
        ],
        out_shape=[
            jax.ShapeDtypeStruct((s_len, D_MODEL), F32),
            jax.ShapeDtypeStruct((s_len // HG_PAIR, HEADS, HEAD_DIM, HEAD_DIM), F32),
        ],
        scratch_shapes=scratch,
        compiler_params=_cparams(("arbitrary",)),
    )(proj, proj, proj, lbl)


def _hg_bwd(proj, lbl, states, d_o):
    s_len = proj.shape[0]
    rows = min(HG_STEP, s_len)
    n_pairs = rows // HG_PAIR
    n_steps = s_len // rows

    def body(hq_ref, hf_ref, hi_ref, lbl_ref, st_ref, do_ref, dp_ref, dlb_ref,
             dstate, q_cat, k_cat, qp_b, kp_b, v_b, do_b, d_qcat, d_kcat, d_qp, d_kp, d_v, d_pair):
        @pl.when(pl.program_id(0) == 0)
        def _():
            dstate[...] = jnp.zeros_like(dstate)
            dlb_ref[...] = jnp.zeros_like(dlb_ref)

        lb = _hg_lower_bound(lbl_ref)
        hq = hq_ref[...]
        v, e, second, (f, sig_f, sig_q) = _hg_operands(hq, hf_ref[...], lb, rows)
        _hg_store_operands(v, second, hi_ref[...], (q_cat, k_cat, qp_b, kp_b, v_b))
        do_b[...] = do_ref[...].astype(BF16)
        e_pair = e["pair"]
        row = lax.broadcasted_iota(jnp.int32, (HG_PAIR, HG_PAIR), 0)
        col = lax.broadcasted_iota(jnp.int32, (HG_PAIR, HG_PAIR), 1)
        causal = row >= col

        for u in reversed(range(n_pairs)):
            r0 = u * HG_PAIR
            sls = [(slice(r0, r0 + HG_PAIR), slice(h * HEAD_DIM, (h + 1) * HEAD_DIM)) for h in range(HEADS)]
            ops = [(_hg_pair_operands(q_cat, r0, h * HEAD_DIM), _hg_pair_operands(k_cat, r0, h * HEAD_DIM))
                   for h in range(HEADS)]
            a_s = [jnp.where(causal, _dot_nt(lhs, rhs), 0.0).astype(BF16) for lhs, rhs in ops]
            da_s = [jnp.where(causal, _dot_nt(do_b[sl], v_b[sl]), 0.0).astype(BF16) for sl in sls]
            st0_s = [st_ref[u, h] for h in range(HEADS)]
            ds1_s = [dstate[h] for h in range(HEADS)]
            ds1b_s = [ds1.astype(BF16) for ds1 in ds1_s]
            for h, sl in enumerate(sls):
                decay = e_pair[r0 : r0 + 1, sl[1]]
                d_pair[u : u + 1, sl[1]] = decay * jnp.sum(ds1_s[h] * st0_s[h], axis=0, keepdims=True)
                dstate[h] = ds1_s[h] * decay + _dot_tn(do_b[sl], qp_b[sl])
            for h, sl in enumerate(sls):
                d_qp[sl] = _dot(do_b[sl], st0_s[h].astype(BF16))
                d_kp[sl] = _dot(v_b[sl], ds1b_s[h])
            for h, sl in enumerate(sls):
                d_v[sl] = _dot_tn(a_s[h], do_b[sl]) + _dot_nt(kp_b[sl], ds1b_s[h])
            for h, sl in enumerate(sls):
                d_lhs = _dot(da_s[h], ops[h][1])
                d_rhs = _dot_tn(da_s[h], ops[h][0])
                for g in range(3):
                    gsl = (sl[0], slice(g * D_MODEL + h * HEAD_DIM, g * D_MODEL + (h + 1) * HEAD_DIM))
                    d_qcat[gsl] = d_lhs[:, g * HEAD_DIM : (g + 1) * HEAD_DIM]
                    d_kcat[gsl] = d_rhs[:, g * HEAD_DIM : (g + 1) * HEAD_DIM]

        zero = jnp.zeros_like(hq)
        dqm = jnp.where(second, d_qcat[:, D_MODEL : 2 * D_MODEL], d_qcat[:, 0:D_MODEL])
        dkm = jnp.where(second, d_kcat[:, D_MODEL : 2 * D_MODEL], d_kcat[:, 0:D_MODEL])
        dqp, dkp = d_qp[...], d_kp[...]
        dqd = dqp * e["q_in"] + jnp.where(second, d_qcat[:, 2 * D_MODEL :], zero)
        dkl = dkp * e["k_out"] + jnp.where(second, zero, d_kcat[:, 2 * D_MODEL :])
        dq = dqm * e["qm"] + dqd * e["qd"]
        dk = dkm * e["km"] + dkl * e["kl"]
        t_kl = dkl * v["kl"]
        dcum = dqm * v["qm"] - dkm * v["km"] + dqd * v["qd"] - t_kl
        dp = d_pair[...]
        dp_b = jnp.broadcast_to(dp[:, None, :], (n_pairs, HG_PAIR, D_MODEL)).reshape(rows, D_MODEL)
        dg = (_split_dot_left(_blockdiag(rows, "upper"), dcum) + _split_dot_left(_blockdiag(rows, "all"), t_kl)
              + _split_dot_left(_blockdiag(rows, "next"), dqp * v["qp"])
              + _split_dot_left(_blockdiag(rows, "prev"), dkp * v["kp"]) + dp_b)
        df = dg / f - dk
        one_m = 1.0 - sig_f
        dp_ref[:, 0:D_MODEL] = (dq * (sig_q * (1.0 + hq * (1.0 - sig_q)))).astype(BF16)
        dp_ref[:, D_MODEL : 2 * D_MODEL] = (df * (1.0 - lb) * sig_f * one_m).astype(BF16)
        dp_ref[:, 2 * D_MODEL : 3 * D_MODEL] = d_v[...].astype(BF16)
        dlb_ref[...] += jnp.sum(df * one_m, axis=0, keepdims=True)

    def col_spec(off):
        return pl.BlockSpec((rows, D_MODEL), lambda s: (n_steps - 1 - s, off // D_MODEL))

    f32_tile = pltpu.VMEM((rows, D_MODEL), F32)
    f32_cat = pltpu.VMEM((rows, 3 * D_MODEL), F32)
    bf_tile = pltpu.VMEM((rows, D_MODEL), BF16)
    bf_cat = pltpu.VMEM((rows, 3 * D_MODEL), BF16)
    scratch = [pltpu.VMEM((HEADS, HEAD_DIM, HEAD_DIM), F32), bf_cat, bf_cat, bf_tile, bf_tile, bf_tile, bf_tile,
               f32_cat, f32_cat, f32_tile, f32_tile, f32_tile, pltpu.VMEM((n_pairs, D_MODEL), F32)]
    return pl.pallas_call(
        body,
        name="hg_bwd",
        grid=(n_steps,),
        in_specs=[
            col_spec(OFF_HG_Q), col_spec(OFF_HG_F), col_spec(OFF_HG_I),
            pl.BlockSpec((2, D_MODEL), lambda s: (0, 0)),
            pl.BlockSpec((n_pairs, HEADS, HEAD_DIM, HEAD_DIM), lambda s: (n_steps - 1 - s, 0, 0, 0)),
            pl.BlockSpec((rows, D_MODEL), lambda s: (n_steps - 1 - s, 0)),
        ],
        out_specs=[
            pl.BlockSpec((rows, 3 * D_MODEL), lambda s: (n_steps - 1 - s, 0)),
            pl.BlockSpec((1, D_MODEL), lambda s: (0, 0)),
        ],
        out_shape=[
            jax.ShapeDtypeStruct((s_len, 3 * D_MODEL), BF16),
            jax.ShapeDtypeStruct((1, D_MODEL), F32),
        ],
        scratch_shapes=scratch,
        compiler_params=_cparams(("arbitrary",)),
    )(proj, proj, proj, lbl, states, d_o)


def _mid(proj, sb_o, hg_o, x, target, b_gate, hg_gain, final_g, w_sb, w_hg, w_out):
    s_len = proj.shape[0]
    ts = min(256, s_len)
    inv_d = 1.0 / D_MODEL

    def body(zsb_ref, hz_ref, gl_ref, sbo_ref, hgo_ref, x_ref, tgt_ref, bg_ref, hgn_ref, fg_ref,
             wsb_ref, whg_ref, wout_ref,
             dout_ref, dsbo_ref, dhgo_ref, dmid_ref,
             asb_ref, dusb_ref, ahg_ref, duhg_ref, y_ref, doutb_ref,
             loss_ref, dfg_ref, dbg_ref, dhgn_ref):
        @pl.when(pl.program_id(0) == 0)
        def _():
            loss_ref[...] = jnp.zeros_like(loss_ref)
            dfg_ref[...] = jnp.zeros_like(dfg_ref)
            dbg_ref[...] = jnp.zeros_like(dbg_ref)
            dhgn_ref[...] = jnp.zeros_like(dhgn_ref)

        z_sb = zsb_ref[...]
        sb_o = sbo_ref[...]
        sig_zsb = _sigmoid(z_sb)
        silu_zsb = z_sb * sig_zsb
        a_sb_f = sb_o * silu_zsb
        a_sb = a_sb_f.astype(BF16)
        u_sb = _dot(a_sb, wsb_ref[...])

        hg_o = hgo_ref[...]
        gain = hgn_ref[...]
        r_parts, yn_parts = [], []
        for h in range(HEADS):
            oh = hg_o[:, h * HEAD_DIM : (h + 1) * HEAD_DIM]
            r = lax.rsqrt(jnp.mean(oh * oh, axis=-1, keepdims=True) + RMS_EPS)
            r_parts.append(jnp.broadcast_to(r, oh.shape))
            yn_parts.append(oh * r)
        r_hg = jnp.concatenate(r_parts, axis=-1)
        yn_hg = jnp.concatenate(yn_parts, axis=-1)
        hn = yn_hg * gain
        hz = hz_ref[...]
        sig_hz = _sigmoid(hz)
        silu_hz = hz * sig_hz
        a_hg_f = hn * silu_hz
        a_hg = a_hg_f.astype(BF16)
        u_hg = _dot(a_hg, whg_ref[...])

        gates = _sigmoid(gl_ref[...] + bg_ref[...])
        g_sb = gates[:, 0:D_MODEL]
        g_hg = gates[:, D_MODEL:]
        y_f = g_sb * u_sb + g_hg * u_hg
        y = y_f.astype(BF16)
        out = x_ref[...] + _dot(y, wout_ref[...])
        r2 = lax.rsqrt(jnp.mean(out * out, axis=-1, keepdims=True) + RMS_EPS)
        yn = out * r2
        fg = fg_ref[...]
        diff = yn * fg - tgt_ref[...]
        loss_ref[...] += 0.5 * inv_d * jnp.sum(diff * diff)

        dyf = diff * inv_d
        dfg_ref[...] += jnp.sum(dyf * yn, axis=0, keepdims=True)
        dyn = dyf * fg
        dout = r2 * (dyn - yn * jnp.mean(dyn * yn, axis=-1, keepdims=True))
        dout_ref[...] = dout
        doutb = dout.astype(BF16)
        doutb_ref[...] = doutb
        dy = _dot_nt(doutb, wout_ref[...])
        du_sb = (dy * g_sb).astype(BF16)
        du_hg = (dy * g_hg).astype(BF16)
        dgl_sb = dy * u_sb * g_sb * (1.0 - g_sb)
        dgl_hg = dy * u_hg * g_hg * (1.0 - g_hg)
        dmid_ref[:, 2 * D_MODEL : 3 * D_MODEL] = dgl_sb.astype(BF16)
        dmid_ref[:, 3 * D_MODEL :] = dgl_hg.astype(BF16)
        dbg_ref[:, 0:D_MODEL] += jnp.sum(dgl_sb, axis=0, keepdims=True)
        dbg_ref[:, D_MODEL:] += jnp.sum(dgl_hg, axis=0, keepdims=True)

        da_sb = _dot_nt(du_sb, wsb_ref[...])
        dsbo_ref[...] = (da_sb * silu_zsb).astype(BF16)
        dmid_ref[:, 0:D_MODEL] = (da_sb * sb_o * (sig_zsb * (1.0 + z_sb * (1.0 - sig_zsb)))).astype(BF16)

        da_hg = _dot_nt(du_hg, whg_ref[...])
        dhn = da_hg * silu_hz
        dmid_ref[:, D_MODEL : 2 * D_MODEL] = (da_hg * hn * (sig_hz * (1.0 + hz * (1.0 - sig_hz)))).astype(BF16)
        dhgn_ref[...] += jnp.sum(dhn * yn_hg, axis=0, keepdims=True)
        dyn_hg = dhn * gain
        prod = dyn_hg * yn_hg
        m_parts = []
        for h in range(HEADS):
            ph = prod[:, h * HEAD_DIM : (h + 1) * HEAD_DIM]
            m_parts.append(jnp.broadcast_to(jnp.mean(ph, axis=-1, keepdims=True), ph.shape))
        dhgo_ref[...] = (r_hg * (dyn_hg - yn_hg * jnp.concatenate(m_parts, axis=-1))).astype(BF16)

        asb_ref[...] = a_sb_f.T.astype(BF16)
        dusb_ref[...] = du_sb
        ahg_ref[...] = a_hg_f.T.astype(BF16)
        duhg_ref[...] = du_hg
        y_ref[...] = y_f.T.astype(BF16)

    def tile(width, off=0):
        return pl.BlockSpec((ts, width), lambda s: (s, off // width))

    def across():
        return pl.BlockSpec((D_MODEL, ts), lambda s: (0, s))

    def whole(shape):
        return pl.BlockSpec(shape, lambda s: (0,) * len(shape))

    def weight():
        return pl.BlockSpec((D_MODEL, D_MODEL), lambda s: (0, 0), pipeline_mode=pl.Buffered(1))

    f32_act = jax.ShapeDtypeStruct((s_len, D_MODEL), F32)
    bf_act = jax.ShapeDtypeStruct((s_len, D_MODEL), BF16)
    bf_act_t = jax.ShapeDtypeStruct((D_MODEL, s_len), BF16)
    return pl.pallas_call(
        body,
        name="mid",
        grid=(s_len // ts,),
        in_specs=[
            tile(D_MODEL, OFF_SB_Z), tile(D_MODEL, OFF_HG_Z), tile(2 * D_MODEL, OFF_GATE),
            tile(D_MODEL), tile(D_MODEL), tile(D_MODEL), tile(D_MODEL),
            whole((1, 2 * D_MODEL)), whole((1, D_MODEL)), whole((1, D_MODEL)),
            weight(), weight(), weight(),
        ],
        out_specs=[
            tile(D_MODEL), tile(D_MODEL), tile(D_MODEL), tile(4 * D_MODEL),
            across(), tile(D_MODEL), across(), tile(D_MODEL), across(), tile(D_MODEL),
            whole((1, 1)), whole((1, D_MODEL)), whole((1, 2 * D_MODEL)), whole((1, D_MODEL)),
        ],
        out_shape=[
            f32_act, bf_act, bf_act, jax.ShapeDtypeStruct((s_len, 4 * D_MODEL), BF16),
            bf_act_t, bf_act, bf_act_t, bf_act, bf_act_t, bf_act,
            jax.ShapeDtypeStruct((1, 1), F32), jax.ShapeDtypeStruct((1, D_MODEL), F32),
            jax.ShapeDtypeStruct((1, 2 * D_MODEL), F32), jax.ShapeDtypeStruct((1, D_MODEL), F32),
        ],
        compiler_params=_cparams(("arbitrary",)),
    )(proj, proj, proj, sb_o, hg_o, x, target, b_gate, hg_gain, final_g, w_sb, w_hg, w_out)


def _grad_square(a_t, b, name):
    s_len = b.shape[0]
    tk = min(1024, s_len)

    def body(a_ref, b_ref, o_ref):
        @pl.when(pl.program_id(0) == 0)
        def _():
            o_ref[...] = jnp.zeros_like(o_ref)

        o_ref[...] += _dot(a_ref[...], b_ref[...])

    return pl.pallas_call(
        body,
        name=name,
        grid=(s_len // tk,),
        in_specs=[pl.BlockSpec((D_MODEL, tk), lambda k: (0, k)), pl.BlockSpec((tk, D_MODEL), lambda k: (k, 0))],
        out_specs=pl.BlockSpec((D_MODEL, D_MODEL), lambda k: (0, 0)),
        out_shape=jax.ShapeDtypeStruct((D_MODEL, D_MODEL), F32),
        compiler_params=_cparams(("arbitrary",)),
    )(a_t, b)


SEG_WIDTHS = (1024, 1024, 1024, 4096, 3072)
SEG_TILE = 1024
SEG_BOUNDS = (0, 1, 2, 3, 7, 10)


def _w_in_tile(k):
    return jnp.where(k < 4, k, jnp.where(k < 7, k + 3, k - 3))


def _grad_w_in(h_t, segs):
    m, s_len = h_t.shape
    tk = min(1024, s_len)
    tn = SEG_TILE
    nk = s_len // tk
    bounds = SEG_BOUNDS

    def body(a_ref, *refs):
        seg_refs, o_ref = refs[:-1], refs[-1]
        j = pl.program_id(0)

        @pl.when(pl.program_id(1) == 0)
        def _():
            o_ref[...] = jnp.zeros_like(o_ref)

        for i, ref in enumerate(seg_refs):
            @pl.when((j >= bounds[i]) & (j < bounds[i + 1]))
            def _(ref=ref):
                o_ref[...] += _dot(a_ref[...], ref[...])

    def seg_spec(lo, hi):
        def index(j, k):
            return (jnp.where(j < lo, 0, jnp.where(j >= hi, nk - 1, k)), jnp.clip(j - lo, 0, hi - lo - 1))
        return pl.BlockSpec((tk, tn), index)

    return pl.pallas_call(
        body,
        name="grad_w_in",
        grid=(IN_WIDTH // tn, nk),
        in_specs=[pl.BlockSpec((m, tk), lambda j, k: (0, k))]
        + [seg_spec(bounds[i], bounds[i + 1]) for i in range(len(SEG_WIDTHS))],
        out_specs=pl.BlockSpec((m, tn), lambda j, k: (0, _w_in_tile(j))),
        out_shape=jax.ShapeDtypeStruct((m, IN_WIDTH), F32),
        compiler_params=_cparams(("arbitrary", "arbitrary")),
    )(h_t, *segs)


EXCHANGE_IN_PIECES = 8
EXCHANGE_PIECES = EXCHANGE_IN_PIECES + 3


def _exchange_copies(sin_ref, ssq_ref, got_in, got_sq, send_sems, recv_sems):
    _, _, c, chips = _position()
    rows = HALF_IN // EXCHANGE_IN_PIECES
    copies = []
    for k, (px, py) in enumerate(chips):
        chip = 2 * px + py
        for p in range(EXCHANGE_PIECES):
            if p < EXCHANGE_IN_PIECES:
                src, dst = sin_ref.at[chip, pl.ds(p * rows, rows), :], got_in.at[k, pl.ds(p * rows, rows), :]
            else:
                src, dst = ssq_ref.at[p - EXCHANGE_IN_PIECES, chip], got_sq.at[k, p - EXCHANGE_IN_PIECES]
            copies.append(_remote(src, dst, send_sems.at[k, p], recv_sems.at[k, p], (px, py, c)))
    return copies


def _dx(segs, w_all, x, norm_g, dout, s_in, s_sq):
    s_len = x.shape[0]
    ts = min(1024, s_len)
    tk = SEG_TILE
    nk = IN_WIDTH // tk
    ns = s_len // ts
    bounds = SEG_BOUNDS
    n_seg = len(SEG_WIDTHS)

    def body(*refs):
        seg_refs = refs[:n_seg]
        w_ref, x_ref, g_ref, dout_ref, sin_ref, ssq_ref, gx_ref, dg_ref, got_in, got_sq, acc, send_sems, recv_sems = refs[n_seg:]
        s, k = pl.program_id(0), pl.program_id(1)

        @pl.when((s == 0) & (k == 0))
        def _():
            dg_ref[...] = jnp.zeros_like(dg_ref)
            for cp in _exchange_copies(sin_ref, ssq_ref, got_in, got_sq, send_sems, recv_sems):
                cp.start()

        @pl.when(k == 0)
        def _():
            acc[...] = jnp.zeros_like(acc)

        for i, ref in enumerate(seg_refs):
            @pl.when((k >= bounds[i]) & (k < bounds[i + 1]))
            def _(ref=ref):
                acc[...] += _dot_nt(ref[...], w_ref[...])

        @pl.when(k == nk - 1)
        def _():
            dh = acc[...]
            xv = x_ref[...]
            r = lax.rsqrt(jnp.mean(xv * xv, axis=-1, keepdims=True) + RMS_EPS)
            xn = xv * r
            dg_ref[...] += jnp.sum(dh * xn, axis=0, keepdims=True)
            dxn = dh * g_ref[...]
            gx_ref[...] = r * (dxn - xn * jnp.mean(dxn * xn, axis=-1, keepdims=True)) + dout_ref[...]

        @pl.when((s == ns - 1) & (k == nk - 1))
        def _():
            for cp in _exchange_copies(sin_ref, ssq_ref, got_in, got_sq, send_sems, recv_sems):
                cp.wait()

    def seg_spec(lo, hi):
        return pl.BlockSpec((ts, tk), lambda s, k: (s, jnp.clip(k - lo, 0, hi - lo - 1)))

    row_tile = pl.BlockSpec((ts, D_MODEL), lambda s, k: (s, 0))
    vec = pl.BlockSpec((1, D_MODEL), lambda s, k: (0, 0))
    return pl.pallas_call(
        body,
        name="dx",
        grid=(ns, nk),
        in_specs=[seg_spec(bounds[i], bounds[i + 1]) for i in range(n_seg)] + [
            pl.BlockSpec((D_MODEL, tk), lambda s, k: (0, _w_in_tile(k))),
            row_tile, vec, row_tile, ANY, ANY,
        ],
        out_specs=[row_tile, vec, ANY, ANY],
        out_shape=[jax.ShapeDtypeStruct((s_len, D_MODEL), F32), jax.ShapeDtypeStruct((1, D_MODEL), F32),
                   jax.ShapeDtypeStruct((3, HALF_IN, W_IN_SHARD), WIRE),
                   jax.ShapeDtypeStruct((3, 3, HALF_SQ, D_MODEL), WIRE)],
        scratch_shapes=[pltpu.VMEM((ts, D_MODEL), F32),
                        pltpu.SemaphoreType.DMA((3, EXCHANGE_PIECES)), pltpu.SemaphoreType.DMA((3, EXCHANGE_PIECES))],
        compiler_params=_cparams(("arbitrary", "arbitrary"), vmem=VMEM_LIMIT_DX),
    )(*segs, w_all, x, norm_g, dout, s_in, s_sq)


def _local_grads(x, target, proj, h_t, qkv, b_gate, lbl, hg_gain, final_g, w_sb, w_hg, w_out):
    sb_o, sb_o_fine = _sb_fwd(qkv)
    hg_o, states = _hg_fwd(proj, lbl)
    (dout, d_sbo, d_hgo, d_mid, a_sb, du_sb, a_hg, du_hg, y, doutb,
     loss, d_fg, d_bg, d_hgn) = _mid(proj, sb_o, hg_o, x, target, b_gate, hg_gain, final_g, w_sb, w_hg, w_out)
    g_w_sb = _grad_square(a_sb, du_sb, "grad_w_sb")
    g_w_hg = _grad_square(a_hg, du_hg, "grad_w_hg")
    g_w_out = _grad_square(y, doutb, "grad_w_out")
    d_q, d_k, d_v = _sb_bwd(qkv, sb_o_fine, d_sbo)
    d_hg, d_lb = _hg_bwd(proj, lbl, states, d_hgo)
    segs = (d_q, d_k, d_v, d_mid, d_hg)
    g_w_in = _grad_w_in(h_t, segs)
    return g_w_in, g_w_sb, g_w_hg, g_w_out, segs, dout, loss, d_bg, d_lb, d_hgn, d_fg


ANY = pl.BlockSpec(memory_space=pl.ANY)
WIRE = BF16
HALF_IN = D_MODEL // 2
HALF_SQ = ROW_SHARD // 2


def _position():
    x, y, c = lax.axis_index("x"), lax.axis_index("y"), lax.axis_index("c")
    chips = [(1 - x, y), (x, 1 - y), (1 - x, 1 - y)]
    return x, y, c, chips


def _remote(src, dst, send_sem, recv_sem, to):
    return pltpu.make_async_remote_copy(src_ref=src, dst_ref=dst, send_sem=send_sem, recv_sem=recv_sem,
                                        device_id=to, device_id_type=MESH)


PROJ_TILE = 1280


def _gather_inproj(idx, h, w_in_b, w_sq_b):
    s_len = h.shape[0]
    ts = min(1024, s_len)
    ns = s_len // ts
    per = W_IN_SHARD // PROJ_TILE
    n_in = 4
    n_piece = n_in + 3
    rows = HALF_IN // n_in

    def chip_at(r, me):
        return me ^ jnp.where(r == 1, 2, jnp.where(r == 2, 1, jnp.where(r == 3, 3, 0)))

    def body(idx_ref, h_ref, win_ref, wsqb_ref, proj_ref, qkv_ref, wall_ref, wsq_ref, wbuf, send_sems, recv_sems, w_sem):
        r, t, s = pl.program_id(0), pl.program_id(1), pl.program_id(2)
        x, y, c, chips = _position()
        me = 2 * x + y
        sibling = (x, y, 1 - c)
        first = (t == 0) & (s == 0)

        def src_piece(p):
            if p < n_in:
                return win_ref.at[pl.ds(c * HALF_IN + p * rows, rows), :]
            return wsqb_ref.at[p - n_in, pl.ds(c * HALF_SQ, HALF_SQ), :]

        def piece(p, chip, core):
            if p < n_in:
                cols = pl.ds(pl.multiple_of(chip * W_IN_SHARD, W_IN_SHARD), W_IN_SHARD)
                return wall_ref.at[pl.ds(core * HALF_IN + p * rows, rows), cols]
            return wsq_ref.at[p - n_in, chip, pl.ds(core * HALF_SQ, HALF_SQ), :]

        def send(k, p):
            px, py = chips[k]
            return _remote(src_piece(p), piece(p, me, c), send_sems.at[k, p], recv_sems.at[k, p], (px, py, c))

        def forward(k, p, core):
            px, py = chips[k]
            got = piece(p, 2 * px + py, core)
            return _remote(got, got, send_sems.at[3 + k, p], recv_sems.at[3 + k, p], sibling)

        @pl.when((r == 0) & first)
        def _():
            for k in range(3):
                for p in range(n_piece):
                    send(k, p).start()

        for k in range(3):
            @pl.when((r == k + 1) & first)
            def _(k=k):
                px, py = chips[k]
                for p in range(n_piece):
                    got = piece(p, 2 * px + py, c)
                    _remote(got, got, send_sems.at[k, p], recv_sems.at[k, p], (px, py, c)).wait_recv()
                    forward(k, p, c).start()
                for p in range(n_piece):
                    forward(k, p, 1 - c).wait_recv()

        @pl.when(s == 0)
        def _():
            col = pl.multiple_of(t * PROJ_TILE, PROJ_TILE)

            @pl.when(r == 0)
            def _():
                cp = pltpu.make_async_copy(win_ref.at[:, pl.ds(col, PROJ_TILE)], wbuf, w_sem)
                cp.start()
                cp.wait()

            @pl.when(r > 0)
            def _():
                off = pl.multiple_of(chip_at(r, me) * W_IN_SHARD + col, PROJ_TILE)
                cp = pltpu.make_async_copy(wall_ref.at[:, pl.ds(off, PROJ_TILE)], wbuf, w_sem)
                cp.start()
                cp.wait()

        p = _dot(h_ref[...], wbuf[...])
        proj_ref[...] = p
        qkv_ref[...] = p.astype(BF16)

        @pl.when((r == 3) & (t == per - 1) & (s == ns - 1))
        def _():
            for k in range(3):
                for p in range(n_piece):
                    send(k, p).wait_send()
                    forward(k, p, c).wait_send()

    def col_tile(r, t, idx):
        return per * chip_at(r, idx[0]) + t

    grid_spec = pltpu.PrefetchScalarGridSpec(
        num_scalar_prefetch=1,
        grid=(N_CHIPS, per, ns),
        in_specs=[pl.BlockSpec((ts, D_MODEL), lambda r, t, s, idx: (s, 0)), ANY, ANY],
        out_specs=[pl.BlockSpec((ts, PROJ_TILE), lambda r, t, s, idx: (s, col_tile(r, t, idx))),
                   pl.BlockSpec((ts, PROJ_TILE), lambda r, t, s, idx: (s, col_tile(r, t, idx))),
                   ANY, ANY],
        scratch_shapes=[pltpu.VMEM((D_MODEL, PROJ_TILE), BF16),
                        pltpu.SemaphoreType.DMA((6, n_piece)), pltpu.SemaphoreType.DMA((6, n_piece)),
                        pltpu.SemaphoreType.DMA(())],
    )
    return pl.pallas_call(
        body,
        name="gather_inproj",
        grid_spec=grid_spec,
        out_shape=[jax.ShapeDtypeStruct((s_len, IN_WIDTH), F32),
                   jax.ShapeDtypeStruct((s_len, IN_WIDTH), BF16),
                   jax.ShapeDtypeStruct((D_MODEL, IN_WIDTH), BF16),
                   jax.ShapeDtypeStruct((3, N_CHIPS, ROW_SHARD, D_MODEL), BF16)],
        compiler_params=_cparams(("arbitrary", "arbitrary", "arbitrary")),
    )(idx, h, w_in_b, w_sq_b)


def _place_own(idx, w_in_b, w_sq_b, w_all, wsq):
    n = 4
    r_in, r_sq = D_MODEL // n, ROW_SHARD // n

    def body(idx_ref, win_ref, wsq_ref, w_all_in, wsq_in, w_all_out, wsq_out):
        w_all_out[...] = win_ref[...]
        wsq_out[:, 0] = wsq_ref[...]

    grid_spec = pltpu.PrefetchScalarGridSpec(
        num_scalar_prefetch=1,
        grid=(n,),
        in_specs=[pl.BlockSpec((r_in, W_IN_SHARD), lambda r, idx: (r, 0)),
                  pl.BlockSpec((3, r_sq, D_MODEL), lambda r, idx: (0, r, 0)), ANY, ANY],
        out_specs=[pl.BlockSpec((r_in, W_IN_SHARD), lambda r, idx: (r, idx[0])),
                   pl.BlockSpec((3, 1, r_sq, D_MODEL), lambda r, idx: (0, idx[0], r, 0))],
    )
    return pl.pallas_call(
        body,
        name="place_own",
        grid_spec=grid_spec,
        out_shape=[jax.ShapeDtypeStruct(w_all.shape, BF16), jax.ShapeDtypeStruct(wsq.shape, BF16)],
        input_output_aliases={3: 0, 4: 1},
        compiler_params=_cparams(("arbitrary",)),
    )(idx, w_in_b, w_sq_b, w_all, wsq)


def _swap_halves(g_in, g_sq):
    n_in = 16
    n_piece = n_in + 3 * N_CHIPS
    rows = HALF_IN // n_in

    def body(gin_ref, gsq_ref, got_in, got_sq, send_sems, recv_sems):
        x, y, c, _ = _position()
        sibling = (x, y, 1 - c)

        def src_piece(p):
            if p < n_in:
                return gin_ref.at[pl.ds((1 - c) * HALF_IN + p * rows, rows), :]
            a, chip = divmod(p - n_in, N_CHIPS)
            return gsq_ref.at[a, chip, pl.ds((1 - c) * HALF_SQ, HALF_SQ), :]

        def dst_piece(p):
            if p < n_in:
                return got_in.at[pl.ds(p * rows, rows), :]
            a, chip = divmod(p - n_in, N_CHIPS)
            return got_sq.at[a, chip]

        out = [_remote(src_piece(p), dst_piece(p), send_sems.at[p], recv_sems.at[p], sibling) for p in range(n_piece)]
        for cp in out:
            cp.start()
        for cp in out:
            cp.wait()

    return pl.pallas_call(
        body,
        name="swap_halves",
        in_specs=[ANY, ANY],
        out_specs=[ANY, ANY],
        out_shape=[jax.ShapeDtypeStruct((HALF_IN, IN_WIDTH), F32),
                   jax.ShapeDtypeStruct((3, N_CHIPS, HALF_SQ, D_MODEL), F32)],
        scratch_shapes=[pltpu.SemaphoreType.DMA((n_piece,))] * 2,
    )(g_in, g_sq)


def _join_halves(r_in, r_sq):
    n_in = 16
    n_piece = n_in + 3
    rows = HALF_IN // n_in

    def body(in_alias, sq_alias, full_in, full_sq, send_sems, recv_sems):
        del in_alias, sq_alias
        x, y, c, _ = _position()
        sibling = (x, y, 1 - c)

        def piece(p, core):
            if p < n_in:
                return full_in.at[pl.ds(core * HALF_IN + p * rows, rows), :]
            return full_sq.at[p - n_in, pl.ds(core * HALF_SQ, HALF_SQ), :]

        out = [_remote(piece(p, c), piece(p, c), send_sems.at[p], recv_sems.at[p], sibling) for p in range(n_piece)]
        for cp in out:
            cp.start()
        for p in range(n_piece):
            _remote(piece(p, 1 - c), piece(p, 1 - c), send_sems.at[p], recv_sems.at[p], sibling).wait_recv()
        for cp in out:
            cp.wait_send()

    return pl.pallas_call(
        body,
        name="join_halves",
        in_specs=[ANY, ANY],
        out_specs=[ANY, ANY],
        out_shape=[jax.ShapeDtypeStruct((D_MODEL, W_IN_SHARD), F32),
                   jax.ShapeDtypeStruct((3, ROW_SHARD, D_MODEL), F32)],
        input_output_aliases={0: 0, 1: 1},
        scratch_shapes=[pltpu.SemaphoreType.DMA((n_piece,)), pltpu.SemaphoreType.DMA((n_piece,))],
    )(r_in, r_sq)


SMALL_ROWS = 56
N_DEV = 8


def _sum_small(part):
    def body(part_ref, out_ref, slots, send_sems, recv_sems):
        x, y, c, _ = _position()
        me = 4 * x + 2 * y + c
        slots[me] = part_ref[...]
        out = []
        for r in range(1, N_DEV):
            rx, ry, rc = (r >> 2) & 1, (r >> 1) & 1, r & 1
            to = (1 - x if rx else x, 1 - y if ry else y, 1 - c if rc else c)
            out.append(_remote(part_ref, slots.at[me], send_sems.at[r - 1], recv_sems.at[r - 1], to))
        for cp in out:
            cp.start()
        for r in range(1, N_DEV):
            _remote(part_ref, slots.at[me ^ r], send_sems.at[r - 1], recv_sems.at[r - 1], (x, y, c)).wait_recv()
        for cp in out:
            cp.wait_send()
        total = slots[0]
        for d in range(1, N_DEV):
            total = total + slots[d]
        out_ref[...] = total

    vmem = pl.BlockSpec(memory_space=pltpu.VMEM)
    return pl.pallas_call(
        body,
        name="sum_small",
        in_specs=[vmem],
        out_specs=vmem,
        out_shape=jax.ShapeDtypeStruct((SMALL_ROWS, HEAD_DIM), F32),
        scratch_shapes=[pltpu.VMEM((N_DEV, SMALL_ROWS, HEAD_DIM), F32),
                        pltpu.SemaphoreType.DMA((N_DEV - 1,)), pltpu.SemaphoreType.DMA((N_DEV - 1,))],
    )(part)


def _prefetch_call(body, name, idx, grid, in_specs, out_specs, out_shape, args):
    grid_spec = pltpu.PrefetchScalarGridSpec(num_scalar_prefetch=1, grid=grid, in_specs=in_specs, out_specs=out_specs)
    return pl.pallas_call(body, name=name, grid_spec=grid_spec, out_shape=out_shape,
                          compiler_params=_cparams(("arbitrary",) * len(grid)))(idx, *args)


def _sum_a_in(idx, g_in, got_in):
    tr = 128
    nr = HALF_IN // tr

    def body(idx_ref, a_ref, b_ref, o_ref):
        o_ref[0] = (a_ref[...] + b_ref[...]).astype(WIRE)

    return _prefetch_call(
        body, "sum_a_in", idx, (N_CHIPS, nr),
        [pl.BlockSpec((tr, W_IN_SHARD), lambda j, r, idx: (idx[1] * nr + r, j)),
         pl.BlockSpec((tr, W_IN_SHARD), lambda j, r, idx: (r, j))],
        pl.BlockSpec((1, tr, W_IN_SHARD), lambda j, r, idx: (j, r, 0)),
        jax.ShapeDtypeStruct((N_CHIPS, HALF_IN, W_IN_SHARD), WIRE), (g_in, got_in))


def _sum_a_sq(idx, g_sq, got_sq):
    blk = (1, 1, HALF_SQ, D_MODEL)

    def body(idx_ref, a_ref, b_ref, o_ref):
        o_ref[...] = (a_ref[...] + b_ref[...]).astype(WIRE)

    return _prefetch_call(
        body, "sum_a_sq", idx, (3, N_CHIPS),
        [pl.BlockSpec(blk, lambda a, j, idx: (a, j, idx[1], 0)), pl.BlockSpec(blk, lambda a, j, idx: (a, j, 0, 0))],
        pl.BlockSpec(blk, lambda a, j, idx: (a, j, 0, 0)),
        jax.ShapeDtypeStruct((3, N_CHIPS, HALF_SQ, D_MODEL), WIRE), (g_sq, got_sq))


def _sum_b_in(idx, s_in, got_in):
    tr = 128
    nr = HALF_IN // tr

    def body(idx_ref, a_ref, b_ref, o_ref):
        o_ref[...] = ((a_ref[0].astype(F32) + b_ref[0].astype(F32)) + b_ref[1].astype(F32)) + b_ref[2].astype(F32)

    return _prefetch_call(
        body, "sum_b_in", idx, (nr,),
        [pl.BlockSpec((1, tr, W_IN_SHARD), lambda r, idx: (idx[0], r, 0)),
         pl.BlockSpec((3, tr, W_IN_SHARD), lambda r, idx: (0, r, 0))],
        pl.BlockSpec((tr, W_IN_SHARD), lambda r, idx: (idx[1] * nr + r, 0)),
        jax.ShapeDtypeStruct((D_MODEL, W_IN_SHARD), F32), (s_in, got_in))


def _sum_b_sq(idx, s_sq, got_sq):
    def body(idx_ref, a_ref, b_ref, o_ref):
        o_ref[0] = ((a_ref[0, 0].astype(F32) + b_ref[0, 0].astype(F32)) + b_ref[1, 0].astype(F32)) + b_ref[2, 0].astype(F32)

    return _prefetch_call(
        body, "sum_b_sq", idx, (3,),
        [pl.BlockSpec((1, 1, HALF_SQ, D_MODEL), lambda a, idx: (a, idx[0], 0, 0)),
         pl.BlockSpec((3, 1, HALF_SQ, D_MODEL), lambda a, idx: (0, a, 0, 0))],
        pl.BlockSpec((1, HALF_SQ, D_MODEL), lambda a, idx: (a, idx[1], 0)),
        jax.ShapeDtypeStruct((3, ROW_SHARD, D_MODEL), F32), (s_sq, got_sq))


def _adamw_math(w, g, m, v):
    m = ADAM_B1 * m + (1.0 - ADAM_B1) * g
    v = ADAM_B2 * v + (1.0 - ADAM_B2) * (g * g)
    m_hat = m / (1.0 - ADAM_B1 ** ADAM_STEP)
    v_hat = v / (1.0 - ADAM_B2 ** ADAM_STEP)
    delta = -ADAM_LR * (m_hat / (jnp.sqrt(v_hat) + ADAM_EPS) + ADAM_WD * w)
    return delta, m, v


def _adamw(w, g, m, v, name):
    rows, cols = w.shape
    tr = min(128, rows)

    def body(w_ref, g_ref, m_ref, v_ref, d_ref, nm_ref, nv_ref):
        d_ref[...], nm_ref[...], nv_ref[...] = _adamw_math(w_ref[...], g_ref[...], m_ref[...], v_ref[...])

    spec = pl.BlockSpec((tr, cols), lambda r: (r, 0))
    return pl.pallas_call(
        body,
        name=name,
        grid=(rows // tr,),
        in_specs=[spec] * 4,
        out_specs=[spec] * 3,
        out_shape=[jax.ShapeDtypeStruct((rows, cols), F32)] * 3,
        compiler_params=_cparams(("arbitrary",)),
    )(w, g, m, v)


def _adamw_small(sums, w, m, v):
    def body(s_ref, w_ref, m_ref, v_ref, loss_ref, g_ref, d_ref, nm_ref, nv_ref):
        s = s_ref[...]
        w = w_ref[...]
        loss_ref[...] = s[0:1, 0:1]
        l0, l1 = w[24:32], w[32:40]
        mx = jnp.maximum(l0, l1)
        e0, e1 = jnp.exp(l0 - mx), jnp.exp(l1 - mx)
        p0, p1 = e0 / (e0 + e1), e1 / (e0 + e1)
        d_lb = s[32:40]
        g = jnp.concatenate([s[8:16], s[16:32], d_lb * p0 * (1.0 - p0), -d_lb * p0 * p1, s[40:48], s[48:56]], axis=0)
        g_ref[...] = g
        d_ref[...], nm_ref[...], nv_ref[...] = _adamw_math(w, g, m_ref[...], v_ref[...])

    packed = jax.ShapeDtypeStruct((SMALL_ROWS, HEAD_DIM), F32)
    return pl.pallas_call(
        body,
        name="adamw_small",
        out_shape=[jax.ShapeDtypeStruct((1, 1), F32), packed, packed, packed, packed],
    )(sums, w, m, v)


def _pack_small(ng, bg, lbl, hgn, fg):
    return jnp.concatenate([a.reshape(-1, HEAD_DIM) for a in (ng, bg, lbl, hgn, fg)], axis=0)


def _unpack_small(p):
    return (p[0:8].reshape(1, D_MODEL), p[8:24].reshape(1, 2 * D_MODEL), p[24:40].reshape(2, HEADS, HEAD_DIM),
            p[40:48].reshape(1, HEADS, HEAD_DIM), p[48:56].reshape(D_MODEL))


def kernel(x, norm_g, w_in, b_gate, lb_logits, hg_norm_g, w_sb_proj, w_hg_proj, w_out, final_norm_g, loss_target, m_norm_g, m_w_in, m_b_gate, m_lb_logits, m_hg_norm_g, m_w_sb_proj, m_w_hg_proj, m_w_out, m_final_norm_g, v_norm_g, v_w_in, v_b_gate, v_lb_logits, v_hg_norm_g, v_w_sb_proj, v_w_hg_proj, v_w_out, v_final_norm_g):
    s_len = x.shape[1]
    w_sq = jnp.stack([w_sb_proj[0], w_hg_proj[0], w_out[0]])
    idx = jnp.stack([2 * lax.axis_index("x") + lax.axis_index("y"), lax.axis_index("c")]).astype(jnp.int32)
    w_in_b, w_sq_b = w_in[0].astype(BF16), w_sq.astype(BF16)
    h, h_t = _prenorm(x[0], norm_g)
    proj, qkv, w_all, wsq = _gather_inproj(idx, h, w_in_b, w_sq_b)
    w_all, wsq = _place_own(idx, w_in_b, w_sq_b, w_all, wsq)
    wsq = wsq.reshape(3, D_MODEL, D_MODEL)

    (g_in, g_sb, g_hg, g_out, segs, dout, loss, d_bg, d_lb, d_hgn, d_fg) = _local_grads(
        x[0], loss_target[0], proj, h_t, qkv, b_gate, lb_logits.reshape(2, D_MODEL), hg_norm_g.reshape(1, D_MODEL),
        final_norm_g.reshape(1, D_MODEL), wsq[0], wsq[1], wsq[2])

    g_sq = jnp.stack([g_sb, g_hg, g_out]).reshape(3, N_CHIPS, ROW_SHARD, D_MODEL)
    got_in, got_sq = _swap_halves(g_in, g_sq)
    s_in, s_sq = _sum_a_in(idx, g_in, got_in), _sum_a_sq(idx, g_sq, got_sq)
    grad_x, d_ng, got_in, got_sq = _dx(segs, w_all, x[0], norm_g, dout, s_in, s_sq)
    grad_in, grad_sq = _join_halves(_sum_b_in(idx, s_in, got_in), _sum_b_sq(idx, s_sq, got_sq))

    d_in, nm_in, nv_in = _adamw(w_in[0], grad_in, m_w_in[0], v_w_in[0], "adamw_in")
    flat = lambda a, b, c: jnp.concatenate([a[0], b[0], c[0]], axis=0)
    d_sq, nm_sq, nv_sq = _adamw(flat(w_sb_proj, w_hg_proj, w_out), grad_sq.reshape(3 * ROW_SHARD, D_MODEL),
                                flat(m_w_sb_proj, m_w_hg_proj, m_w_out), flat(v_w_sb_proj, v_w_hg_proj, v_w_out),
                                "adamw_sq")

    pad = jnp.zeros((8, HEAD_DIM), F32).at[0, 0].set(loss[0, 0])
    part = jnp.concatenate([pad] + [a.reshape(-1, HEAD_DIM) for a in (d_ng, d_bg, d_lb, d_hgn, d_fg)], axis=0)
    sums = _sum_small(part)
    loss_out, g_sm, d_sm, nm_sm, nv_sm = _adamw_small(
        sums, _pack_small(norm_g, b_gate, lb_logits, hg_norm_g, final_norm_g),
        _pack_small(m_norm_g, m_b_gate, m_lb_logits, m_hg_norm_g, m_final_norm_g),
        _pack_small(v_norm_g, v_b_gate, v_lb_logits, v_hg_norm_g, v_final_norm_g))

    def big(t_in, t_sq):
        sq = t_sq.reshape(3, 1, ROW_SHARD, D_MODEL)
        return t_in[None], sq[0], sq[1], sq[2]

    def order(small, in_, sb, hg, out):
        ng, bg, lbl, hgn, fg = small
        return [ng, in_, bg, lbl, hgn, sb, hg, out, fg]

    outs = [loss_out[0, 0], grad_x[None]]
    for small, (t_in, t_sq) in ((g_sm, (grad_in, grad_sq)), (d_sm, (d_in, d_sq)), (nm_sm, (nm_in, nm_sq)), (nv_sm, (nv_in, nv_sq))):
        outs += order(_unpack_small(small), *big(t_in, t_sq))
    return tuple(outs)
```

```python
import functools

import jax
import jax.numpy as jnp
from jax import lax
from jax.experimental import pallas as pl
from jax.experimental.pallas import tpu as pltpu

F32 = jnp.float32
BF16 = jnp.bfloat16

D_MODEL = 1024
HEADS = 8
HEAD_DIM = 128
IN_WIDTH = 10240
N_CHIPS = 4
W_IN_SHARD = IN_WIDTH // N_CHIPS
ROW_SHARD = D_MODEL // N_CHIPS
RMS_EPS = 1e-6

OFF_SB_Q, OFF_SB_K, OFF_SB_V, OFF_SB_Z = 0, 1024, 2048, 3072
OFF_HG_Q, OFF_HG_F, OFF_HG_I, OFF_HG_Z, OFF_GATE = 4096, 5120, 6144, 7168, 8192

SB_BLOCK = 256
SB_FWD_HEADS = 4
SB_BWD_HEADS = 2
SB_ROWS = 256
SB_DEAD = -110.0
SB_GONE = -1e30
HG_CHUNK = 32
HG_PAIR = 2 * HG_CHUNK
HG_STEP = 256
HG_MID = HG_CHUNK // 2 - 1

ADAM_LR, ADAM_B1, ADAM_B2, ADAM_EPS, ADAM_WD, ADAM_STEP = 0.001, 0.9, 0.999, 1e-08, 0.01, 10

VMEM_LIMIT = 56 * 1024 * 1024
VMEM_LIMIT_DX = 60 * 1024 * 1024

MESH = pl.DeviceIdType.MESH


def _cparams(sem, vmem=VMEM_LIMIT):
    return pltpu.CompilerParams(dimension_semantics=sem, vmem_limit_bytes=vmem)


def _dot(a, b):
    return jnp.dot(a, b, preferred_element_type=F32)


def _dot_nt(a, b):
    return lax.dot_general(a, b, (((1,), (1,)), ((), ())), preferred_element_type=F32)


def _dot_tn(a, b):
    return lax.dot_general(a, b, (((0,), (0,)), ((), ())), preferred_element_type=F32)


def _split_dot(x, tri):
    hi = x.astype(BF16)
    lo = (x - hi.astype(F32)).astype(BF16)
    both = _dot(jnp.concatenate([hi, lo], axis=0), tri)
    return both[: x.shape[0]] + both[x.shape[0] :]


def _split_dot_left(tri, x):
    hi = x.astype(BF16)
    lo = (x - hi.astype(F32)).astype(BF16)
    return _dot(tri, hi) + _dot(tri, lo)


def _sigmoid(x):
    return 1.0 / (1.0 + jnp.exp(-x))


def _prenorm(x, norm_g):
    s_len = x.shape[0]
    ts = min(1024, s_len)

    def body(x_ref, g_ref, h_ref, ht_ref):
        xv = x_ref[...]
        r = lax.rsqrt(jnp.mean(xv * xv, axis=-1, keepdims=True) + RMS_EPS)
        hv = (xv * r) * g_ref[...]
        h_ref[...] = hv.astype(BF16)
        ht_ref[...] = hv.T.astype(BF16)

    return pl.pallas_call(
        body,
        name="prenorm",
        grid=(s_len // ts,),
        in_specs=[pl.BlockSpec((ts, D_MODEL), lambda s: (s, 0)), pl.BlockSpec((1, D_MODEL), lambda s: (0, 0))],
        out_specs=[pl.BlockSpec((ts, D_MODEL), lambda s: (s, 0)), pl.BlockSpec((D_MODEL, ts), lambda s: (0, s))],
        out_shape=[jax.ShapeDtypeStruct((s_len, D_MODEL), BF16), jax.ShapeDtypeStruct((D_MODEL, s_len), BF16)],
        compiler_params=_cparams(("arbitrary",)),
    )(x, norm_g)


def _sb_scores(qb, kb, causal, tri_excl, diag):
    z = _dot_nt(qb, kb) * HEAD_DIM ** -0.5
    ls_pos = jnp.minimum(z, 0.0) - jnp.log1p(jnp.exp(-jnp.abs(z)))
    log_not = ls_pos - z
    log_not_m = jnp.where(causal, log_not, 0.0) if diag else log_not
    return ls_pos, log_not, log_not_m, _split_dot(log_not_m, tri_excl)


def _sb_weights(ls_pos, suffix, carry, causal, diag):
    surv = suffix + carry
    w = jnp.exp(ls_pos + surv)
    return surv, (jnp.where(causal, w, 0.0) if diag else w)


def _sb_specs(s_len, blk, heads):
    width = heads * HEAD_DIM

    def blk_spec(off):
        return pl.BlockSpec((blk, width), lambda h, i: (i, off // width + h))

    def head_spec(off, buffers=2):
        return pl.BlockSpec((s_len, width), lambda h, i: (0, off // width + h), pipeline_mode=pl.Buffered(buffers))

    return blk_spec, head_spec


def _head_cols(p):
    return slice(p * HEAD_DIM, (p + 1) * HEAD_DIM)


def _sb_chains(blk, heads):
    rows = min(SB_ROWS, blk)
    return [(p, a) for p in range(heads) for a in range(blk // rows)], rows


def _sb_masks(blk, rows):
    row = lax.broadcasted_iota(jnp.int32, (rows, blk), 0)
    col = lax.broadcasted_iota(jnp.int32, (rows, blk), 1)
    causal = [row + a * rows > col for a in range(blk // rows)]
    row = lax.broadcasted_iota(jnp.int32, (blk, blk), 0)
    col = lax.broadcasted_iota(jnp.int32, (blk, blk), 1)
    tri_excl = (row > col).astype(BF16)
    tri_incl = (row >= col).astype(BF16)
    return causal, tri_excl, tri_incl


def _sb_alive(st, n_chain):
    alive = functools.reduce(jnp.maximum, [st[1 + 3 * c] for c in range(n_chain)])
    return jnp.max(alive) > SB_DEAD


def _sb_fwd(qkv):
    s_len = qkv.shape[0]
    blk = min(SB_BLOCK, s_len)
    nq = s_len // blk
    chains, rows = _sb_chains(blk, SB_FWD_HEADS)

    def body(q_ref, k_ref, v_ref, o_ref, of_ref):
        i = pl.program_id(1)
        causal, tri_excl, _ = _sb_masks(blk, rows)

        def tiles(specs, st):
            pre = []
            for j, diag, _ in specs:
                start = pl.multiple_of(j * blk, blk)
                for p, a in chains:
                    kb = k_ref[pl.ds(start, blk), _head_cols(p)]
                    qb = q_ref[a * rows : (a + 1) * rows, _head_cols(p)]
                    pre.append(_sb_scores(qb, kb, causal[a], tri_excl, diag) + (v_ref[pl.ds(start, blk), _head_cols(p)],))
            for t, (j, diag, valid) in enumerate(specs):
                new = []
                for c, (p, a) in enumerate(chains):
                    carry, acc, acc_lo = st[3 * c : 3 * c + 3]
                    if valid is not None:
                        carry = jnp.where(valid, carry, SB_GONE)
                    ls_pos, _, log_not_m, suffix, vb = pre[t * len(chains) + c]
                    surv, w = _sb_weights(ls_pos, suffix, carry, causal[a], diag)
                    wb = w.astype(BF16)
                    w_lo = (w - wb.astype(F32)).astype(BF16)
                    both = _dot(jnp.concatenate([wb, w_lo], axis=0), vb)
                    new += [surv[:, 0:1] + log_not_m[:, 0:1], acc + both[:rows], acc_lo + both[rows:]]
                st = tuple(new)
            return st

        zero = jnp.zeros((rows, HEAD_DIM), F32)
        st = tiles([(i, True, None), (jnp.maximum(i - 1, 0), False, i >= 1)],
                   (jnp.zeros((rows, 1), F32), zero, zero) * len(chains))

        def more(st):
            return (st[0] < i) & _sb_alive(st, len(chains))

        def step(st):
            return (st[0] + 1,) + tiles([(i - 1 - st[0], False, None)], st[1:])

        st = lax.while_loop(more, step, (1,) + st)[1:]
        for c, (p, a) in enumerate(chains):
            o_ref[a * rows : (a + 1) * rows, _head_cols(p)] = st[3 * c + 1]
            of_ref[a * rows : (a + 1) * rows, _head_cols(p)] = st[3 * c + 1] + st[3 * c + 2]

    blk_spec, head_spec = _sb_specs(s_len, blk, SB_FWD_HEADS)
    return pl.pallas_call(
        body,
        name="sb_fwd",
        grid=(HEADS // SB_FWD_HEADS, nq),
        in_specs=[blk_spec(OFF_SB_Q), head_spec(OFF_SB_K), head_spec(OFF_SB_V)],
        out_specs=[blk_spec(0), blk_spec(0)],
        out_shape=[jax.ShapeDtypeStruct((s_len, D_MODEL), F32)] * 2,
        compiler_params=_cparams(("arbitrary", "arbitrary")),
    )(qkv, qkv, qkv)


def _sb_bwd(qkv, o_fine, d_o):
    s_len = qkv.shape[0]
    blk = min(SB_BLOCK, s_len)
    nq = s_len // blk
    scale = HEAD_DIM ** -0.5
    chains, rows = _sb_chains(blk, SB_BWD_HEADS)

    def body(q_ref, k_ref, v_ref, of_ref, do_ref, dq_ref, dk_ref, dv_ref, dk_acc, dv_acc):
        i = pl.program_id(1)

        @pl.when(i == 0)
        def _():
            dk_acc[...] = jnp.zeros_like(dk_acc)
            dv_acc[...] = jnp.zeros_like(dv_acc)

        dob = do_ref[...].astype(BF16)
        prod = dob.astype(F32) * of_ref[...]
        causal, tri_excl, tri_incl = _sb_masks(blk, rows)

        def group(x, p, a):
            return x[a * rows : (a + 1) * rows, _head_cols(p)]

        totals = [jnp.sum(group(prod, p, a), axis=-1, keepdims=True) for p, a in chains]

        def tiles(specs, st):
            pre = []
            for j, diag, _ in specs:
                start = pl.multiple_of(j * blk, blk)
                for p, a in chains:
                    kb = k_ref[pl.ds(start, blk), _head_cols(p)]
                    vb = v_ref[pl.ds(start, blk), _head_cols(p)]
                    qb, dob_c = group(q_ref, p, a), group(dob, p, a)
                    pre.append(_sb_scores(qb, kb, causal[a], tri_excl, diag) + (_dot_nt(dob_c, vb), qb, kb, dob_c))
            for t, (j, diag, valid) in enumerate(specs):
                start = pl.multiple_of(j * blk, blk)
                mids = []
                for c, (p, a) in enumerate(chains):
                    c_not = st[3 * c]
                    if valid is not None:
                        c_not = jnp.where(valid, c_not, SB_GONE)
                    ls_pos, _, _, suffix, d_w = pre[t * len(chains) + c][:5]
                    surv, w = _sb_weights(ls_pos, suffix, c_not, causal[a], diag)
                    dlw = d_w * w
                    mids.append((surv, w, dlw, _split_dot(dlw, tri_incl)))
                new = []
                dk_new = [None] * SB_BWD_HEADS
                dv_new = [None] * SB_BWD_HEADS
                for c, (p, a) in enumerate(chains):
                    c_dlw, dq = st[3 * c + 1 : 3 * c + 3]
                    ls_pos, log_not, log_not_m, _, _, qb, kb, dob_c = pre[t * len(chains) + c]
                    surv, w, dlw, suffix = mids[c]
                    d_not = totals[c] - c_dlw - suffix
                    dz = (dlw * jnp.exp(log_not) - d_not * jnp.exp(ls_pos)) * scale
                    if diag:
                        dz = jnp.where(causal[a], dz, 0.0)
                    if valid is not None:
                        dz = jnp.where(valid, dz, 0.0)
                    dzb = dz.astype(BF16)
                    dk_c, dv_c = _dot_tn(dzb, qb), _dot_tn(w.astype(BF16), dob_c)
                    dk_new[p] = dk_c if dk_new[p] is None else dk_new[p] + dk_c
                    dv_new[p] = dv_c if dv_new[p] is None else dv_new[p] + dv_c
                    new += [surv[:, 0:1] + log_not_m[:, 0:1], c_dlw + suffix[:, 0:1], dq + _dot(dzb, kb)]
                for p in range(SB_BWD_HEADS):
                    dk_acc[pl.ds(start, blk), _head_cols(p)] += dk_new[p]
                    dv_acc[pl.ds(start, blk), _head_cols(p)] += dv_new[p]
                st = tuple(new)
            return st

        zcol = jnp.zeros((rows, 1), F32)
        st = tiles([(i, True, None), (jnp.maximum(i - 1, 0), False, i >= 1)],
                   (zcol, zcol, jnp.zeros((rows, HEAD_DIM), F32)) * len(chains))

        def more(st):
            return (st[0] < i) & _sb_alive(st, len(chains))

        def step(st):
            return (st[0] + 1,) + tiles([(i - 1 - st[0], False, None)], st[1:])

        st = lax.while_loop(more, step, (1,) + st)[1:]
        for c, (p, a) in enumerate(chains):
            dq_ref[a * rows : (a + 1) * rows, _head_cols(p)] = st[3 * c + 2].astype(BF16)

        @pl.when(i == nq - 1)
        def _():
            dk_ref[...] = dk_acc[...].astype(BF16)
            dv_ref[...] = dv_acc[...].astype(BF16)

    blk_spec, head_spec = _sb_specs(s_len, blk, SB_BWD_HEADS)
    width = SB_BWD_HEADS * HEAD_DIM
    return pl.pallas_call(
        body,
        name="sb_bwd",
        grid=(HEADS // SB_BWD_HEADS, nq),
        in_specs=[blk_spec(OFF_SB_Q), head_spec(OFF_SB_K, 1), head_spec(OFF_SB_V, 1), blk_spec(0), blk_spec(0)],
        out_specs=[blk_spec(0), head_spec(0), head_spec(0)],
        out_shape=[jax.ShapeDtypeStruct((s_len, D_MODEL), BF16)] * 3,
        scratch_shapes=[pltpu.VMEM((s_len, width), F32), pltpu.VMEM((s_len, width), F32)],
        compiler_params=_cparams(("arbitrary", "arbitrary")),
    )(qkv, qkv, qkv, o_fine, d_o)


def _hg_lower_bound(lbl_ref):
    l0 = lbl_ref[0:1, :]
    l1 = lbl_ref[1:2, :]
    mx = jnp.maximum(l0, l1)
    e0 = jnp.exp(l0 - mx)
    e1 = jnp.exp(l1 - mx)
    return e0 / (e0 + e1)


def _hg_gates(hq, hf, lb):
    sig_f = _sigmoid(hf)
    f = lb + (1.0 - lb) * sig_f
    g = jnp.log(f)
    kk = 1.0 - f
    sig_q = _sigmoid(hq)
    qq = hq * sig_q
    return qq, kk, g, f, sig_f, sig_q


def _period_bcast(x, r, rows, period):
    w = x.shape[-1]
    x3 = x.reshape(rows // period, period, w)
    return jnp.broadcast_to(x3[:, r : r + 1, :], x3.shape).reshape(rows, w)


def _blockdiag(rows, kind):
    row = lax.broadcasted_iota(jnp.int32, (rows, rows), 0)
    col = lax.broadcasted_iota(jnp.int32, (rows, rows), 1)
    if kind in ("next", "prev"):
        first, second = (row, col) if kind == "next" else (col, row)
        keep = ((row // HG_PAIR) == (col // HG_PAIR)) & (first % HG_PAIR < HG_CHUNK) & (second % HG_PAIR >= HG_CHUNK)
    else:
        keep = (row // HG_CHUNK) == (col // HG_CHUNK)
        if kind == "lower":
            keep = keep & (row >= col)
        elif kind == "upper":
            keep = keep & (row <= col)
    return jnp.where(keep, 1.0, 0.0).astype(BF16)


def _hg_operands(hq, hf, lb, rows):
    qq, kk, g, f, sig_f, sig_q = _hg_gates(hq, hf, lb)
    cum = _split_dot_left(_blockdiag(rows, "lower"), g)
    mid = _period_bcast(cum, HG_MID, rows, HG_CHUNK)
    last = _period_bcast(cum, HG_CHUNK - 1, rows, HG_CHUNK)
    last0 = _period_bcast(cum, HG_CHUNK - 1, rows, HG_PAIR)
    last1 = _period_bcast(cum, HG_PAIR - 1, rows, HG_PAIR)
    second = (lax.broadcasted_iota(jnp.int32, cum.shape, 0) % HG_PAIR) >= HG_CHUNK
    e = dict(qm=jnp.exp(cum - mid), km=jnp.exp(mid - cum), qd=jnp.exp(cum), kl=jnp.exp(last - cum),
             q_in=jnp.where(second, jnp.exp(last0), 1.0), k_out=jnp.where(second, 1.0, jnp.exp(last1)),
             pair=jnp.exp(last0 + last1))
    v = dict(qm=qq * e["qm"], km=kk * e["km"], qd=qq * e["qd"], kl=kk * e["kl"])
    v["qp"] = v["qd"] * e["q_in"]
    v["kp"] = v["kl"] * e["k_out"]
    return v, e, second, (f, sig_f, sig_q)


def _hg_store_operands(v, second, hi, refs):
    zero = jnp.zeros_like(v["qm"])
    q_cat, k_cat, qp_b, kp_b, v_b = refs
    q_cat[:, 0:D_MODEL] = jnp.where(second, zero, v["qm"]).astype(BF16)
    q_cat[:, D_MODEL : 2 * D_MODEL] = jnp.where(second, v["qm"], zero).astype(BF16)
    q_cat[:, 2 * D_MODEL :] = jnp.where(second, v["qd"], zero).astype(BF16)
    k_cat[:, 0:D_MODEL] = jnp.where(second, zero, v["km"]).astype(BF16)
    k_cat[:, D_MODEL : 2 * D_MODEL] = jnp.where(second, v["km"], zero).astype(BF16)
    k_cat[:, 2 * D_MODEL :] = jnp.where(second, zero, v["kl"]).astype(BF16)
    qp_b[...] = v["qp"].astype(BF16)
    kp_b[...] = v["kp"].astype(BF16)
    v_b[...] = hi.astype(BF16)


def _hg_pair_operands(cat, r0, c0):
    return jnp.concatenate([cat[r0 : r0 + HG_PAIR, g * D_MODEL + c0 : g * D_MODEL + c0 + HEAD_DIM] for g in range(3)], axis=1)


def _hg_fwd(proj, lbl):
    s_len = proj.shape[0]
    rows = min(HG_STEP, s_len)
    n_pairs = rows // HG_PAIR

    def body(hq_ref, hf_ref, hi_ref, lbl_ref, o_ref, st_ref, state, q_cat, k_cat, qp_b, kp_b, v_b):
        @pl.when(pl.program_id(0) == 0)
        def _():
            state[...] = jnp.zeros_like(state)

        v, e, second, _ = _hg_operands(hq_ref[...], hf_ref[...], _hg_lower_bound(lbl_ref), rows)
        _hg_store_operands(v, second, hi_ref[...], (q_cat, k_cat, qp_b, kp_b, v_b))
        e_pair = e["pair"]
        row = lax.broadcasted_iota(jnp.int32, (HG_PAIR, HG_PAIR), 0)
        col = lax.broadcasted_iota(jnp.int32, (HG_PAIR, HG_PAIR), 1)
        causal = row >= col

        for u in range(n_pairs):
            r0 = u * HG_PAIR
            sls = [(slice(r0, r0 + HG_PAIR), slice(h * HEAD_DIM, (h + 1) * HEAD_DIM)) for h in range(HEADS)]
            a_s = [jnp.where(causal, _dot_nt(_hg_pair_operands(q_cat, r0, h * HEAD_DIM),
                                             _hg_pair_operands(k_cat, r0, h * HEAD_DIM)), 0.0).astype(BF16)
                   for h in range(HEADS)]
            st_s = [state[h] for h in range(HEADS)]
            for h, sl in enumerate(sls):
                st_ref[u, h] = st_s[h]
                state[h] = st_s[h] * e_pair[r0 : r0 + 1, sl[1]] + _dot_tn(v_b[sl], kp_b[sl])
            for h, sl in enumerate(sls):
                o_ref[sl] = _dot(a_s[h], v_b[sl]) + _dot_nt(qp_b[sl], st_s[h].astype(BF16))

    def col_spec(off):
        return pl.BlockSpec((rows, D_MODEL), lambda s: (s, off // D_MODEL))

    bf_tile = pltpu.VMEM((rows, D_MODEL), BF16)
    bf_cat = pltpu.VMEM((rows, 3 * D_MODEL), BF16)
    scratch = [pltpu.VMEM((HEADS, HEAD_DIM, HEAD_DIM), F32), bf_cat, bf_cat, bf_tile, bf_tile, bf_tile]
    return pl.pallas_call(
        body,
        name="hg_fwd",
        grid=(s_len // rows,),
        in_specs=[col_spec(OFF_HG_Q), col_spec(OFF_HG_F), col_spec(OFF_HG_I), pl.BlockSpec((2, D_MODEL), lambda s: (0, 0))],
        out_specs=[
            pl.BlockSpec((rows, D_MODEL), lambda s: (s, 0)),
            pl.BlockSpec((n_pairs, HEADS, HEAD_DIM, HEAD_DIM), lambda s: (s, 0, 0, 0)),
        ],
        out_shape=[
            jax.ShapeDtypeStruct((s_len, D_MODEL), F32),
            jax.ShapeDtypeStruct((s_len // HG_PAIR, HEADS, HEAD_DIM, HEAD_DIM), F32),
        ],
        scratch_shapes=scratch,
        compiler_params=_cparams(("arbitrary",)),
    )(proj, proj, proj, lbl)


def _hg_bwd(proj, lbl, states, d_o):
    s_len = proj.shape[0]
    rows = min(HG_STEP, s_len)
    n_pairs = rows // HG_PAIR
    n_steps = s_len // rows

    def body(hq_ref, hf_ref, hi_ref, lbl_ref, st_ref, do_ref, dp_ref, dlb_ref,
             dstate, q_cat, k_cat, qp_b, kp_b, v_b, do_b, d_qcat, d_kcat, d_qp, d_kp, d_v, d_pair):
        @pl.when(pl.program_id(0) == 0)
        def _():
            dstate[...] = jnp.zeros_like(dstate)
            dlb_ref[...] = jnp.zeros_like(dlb_ref)

        lb = _hg_lower_bound(lbl_ref)
        hq = hq_ref[...]
        v, e, second, (f, sig_f, sig_q) = _hg_operands(hq, hf_ref[...], lb, rows)
        _hg_store_operands(v, second, hi_ref[...], (q_cat, k_cat, qp_b, kp_b, v_b))
        do_b[...] = do_ref[...].astype(BF16)
        e_pair = e["pair"]
        row = lax.broadcasted_iota(jnp.int32, (HG_PAIR, HG_PAIR), 0)
        col = lax.broadcasted_iota(jnp.int32, (HG_PAIR, HG_PAIR), 1)
        causal = row >= col

        for u in reversed(range(n_pairs)):
            r0 = u * HG_PAIR
            sls = [(slice(r0, r0 + HG_PAIR), slice(h * HEAD_DIM, (h + 1) * HEAD_DIM)) for h in range(HEADS)]
            ops = [(_hg_pair_operands(q_cat, r0, h * HEAD_DIM), _hg_pair_operands(k_cat, r0, h * HEAD_DIM))
                   for h in range(HEADS)]
            a_s = [jnp.where(causal, _dot_nt(lhs, rhs), 0.0).astype(BF16) for lhs, rhs in ops]
            da_s = [jnp.where(causal, _dot_nt(do_b[sl], v_b[sl]), 0.0).astype(BF16) for sl in sls]
            st0_s = [st_ref[u, h] for h in range(HEADS)]
            ds1_s = [dstate[h] for h in range(HEADS)]
            ds1b_s = [ds1.astype(BF16) for ds1 in ds1_s]
            for h, sl in enumerate(sls):
                decay = e_pair[r0 : r0 + 1, sl[1]]
                d_pair[u : u + 1, sl[1]] = decay * jnp.sum(ds1_s[h] * st0_s[h], axis=0, keepdims=True)
                dstate[h] = ds1_s[h] * decay + _dot_tn(do_b[sl], qp_b[sl])
            for h, sl in enumerate(sls):
                d_qp[sl] = _dot(do_b[sl], st0_s[h].astype(BF16))
                d_kp[sl] = _dot(v_b[sl], ds1b_s[h])
            for h, sl in enumerate(sls):
                d_v[sl] = _dot_tn(a_s[h], do_b[sl]) + _dot_nt(kp_b[sl], ds1b_s[h])
            for h, sl in enumerate(sls):
                d_lhs = _dot(da_s[h], ops[h][1])
                d_rhs = _dot_tn(da_s[h], ops[h][0])
                for g in range(3):
                    gsl = (sl[0], slice(g * D_MODEL + h * HEAD_DIM, g * D_MODEL + (h + 1) * HEAD_DIM))
                    d_qcat[gsl] = d_lhs[:, g * HEAD_DIM : (g + 1) * HEAD_DIM]
                    d_kcat[gsl] = d_rhs[:, g * HEAD_DIM : (g + 1) * HEAD_DIM]

        zero = jnp.zeros_like(hq)
        dqm = jnp.where(second, d_qcat[:, D_MODEL : 2 * D_MODEL], d_qcat[:, 0:D_MODEL])
        dkm = jnp.where(second, d_kcat[:, D_MODEL : 2 * D_MODEL], d_kcat[:, 0:D_MODEL])
        dqp, dkp = d_qp[...], d_kp[...]
        dqd = dqp * e["q_in"] + jnp.where(second, d_qcat[:, 2 * D_MODEL :], zero)
        dkl = dkp * e["k_out"] + jnp.where(second, zero, d_kcat[:, 2 * D_MODEL :])
        dq = dqm * e["qm"] + dqd * e["qd"]
        dk = dkm * e["km"] + dkl * e["kl"]
        t_kl = dkl * v["kl"]
        dcum = dqm * v["qm"] - dkm * v["km"] + dqd * v["qd"] - t_kl
        dp = d_pair[...]
        dp_b = jnp.broadcast_to(dp[:, None, :], (n_pairs, HG_PAIR, D_MODEL)).reshape(rows, D_MODEL)
        dg = (_split_dot_left(_blockdiag(rows, "upper"), dcum) + _split_dot_left(_blockdiag(rows, "all"), t_kl)
              + _split_dot_left(_blockdiag(rows, "next"), dqp * v["qp"])
              + _split_dot_left(_blockdiag(rows, "prev"), dkp * v["kp"]) + dp_b)
        df = dg / f - dk
        one_m = 1.0 - sig_f
        dp_ref[:, 0:D_MODEL] = (dq * (sig_q * (1.0 + hq * (1.0 - sig_q)))).astype(BF16)
        dp_ref[:, D_MODEL : 2 * D_MODEL] = (df * (1.0 - lb) * sig_f * one_m).astype(BF16)
        dp_ref[:, 2 * D_MODEL : 3 * D_MODEL] = d_v[...].astype(BF16)
        dlb_ref[...] += jnp.sum(df * one_m, axis=0, keepdims=True)

    def col_spec(off):
        return pl.BlockSpec((rows, D_MODEL), lambda s: (n_steps - 1 - s, off // D_MODEL))

    f32_tile = pltpu.VMEM((rows, D_MODEL), F32)
    f32_cat = pltpu.VMEM((rows, 3 * D_MODEL), F32)
    bf_tile = pltpu.VMEM((rows, D_MODEL), BF16)
    bf_cat = pltpu.VMEM((rows, 3 * D_MODEL), BF16)
    scratch = [pltpu.VMEM((HEADS, HEAD_DIM, HEAD_DIM), F32), bf_cat, bf_cat, bf_tile, bf_tile, bf_tile, bf_tile,
               f32_cat, f32_cat, f32_tile, f32_tile, f32_tile, pltpu.VMEM((n_pairs, D_MODEL), F32)]
    return pl.pallas_call(
        body,
        name="hg_bwd",
        grid=(n_steps,),
        in_specs=[
            col_spec(OFF_HG_Q), col_spec(OFF_HG_F), col_spec(OFF_HG_I),
            pl.BlockSpec((2, D_MODEL), lambda s: (0, 0)),
            pl.BlockSpec((n_pairs, HEADS, HEAD_DIM, HEAD_DIM), lambda s: (n_steps - 1 - s, 0, 0, 0)),
            pl.BlockSpec((rows, D_MODEL), lambda s: (n_steps - 1 - s, 0)),
        ],
        out_specs=[
            pl.BlockSpec((rows, 3 * D_MODEL), lambda s: (n_steps - 1 - s, 0)),
            pl.BlockSpec((1, D_MODEL), lambda s: (0, 0)),
        ],
        out_shape=[
            jax.ShapeDtypeStruct((s_len, 3 * D_MODEL), BF16),
            jax.ShapeDtypeStruct((1, D_MODEL), F32),
        ],
        scratch_shapes=scratch,
        compiler_params=_cparams(("arbitrary",)),
    )(proj, proj, proj, lbl, states, d_o)


def _mid(proj, sb_o, hg_o, x, target, b_gate, hg_gain, final_g, w_sb, w_hg, w_out):
    s_len = proj.shape[0]
    ts = min(256, s_len)
    inv_d = 1.0 / D_MODEL

    def body(zsb_ref, hz_ref, gl_ref, sbo_ref, hgo_ref, x_ref, tgt_ref, bg_ref, hgn_ref, fg_ref,
             wsb_ref, whg_ref, wout_ref,
             dout_ref, dsbo_ref, dhgo_ref, dmid_ref,
             asb_ref, dusb_ref, ahg_ref, duhg_ref, y_ref, doutb_ref,
             loss_ref, dfg_ref, dbg_ref, dhgn_ref):
        @pl.when(pl.program_id(0) == 0)
        def _():
            loss_ref[...] = jnp.zeros_like(loss_ref)
            dfg_ref[...] = jnp.zeros_like(dfg_ref)
            dbg_ref[...] = jnp.zeros_like(dbg_ref)
            dhgn_ref[...] = jnp.zeros_like(dhgn_ref)

        z_sb = zsb_ref[...]
        sb_o = sbo_ref[...]
        sig_zsb = _sigmoid(z_sb)
        silu_zsb = z_sb * sig_zsb
        a_sb_f = sb_o * silu_zsb
        a_sb = a_sb_f.astype(BF16)
        u_sb = _dot(a_sb, wsb_ref[...])

        hg_o = hgo_ref[...]
        gain = hgn_ref[...]
        r_parts, yn_parts = [], []
        for h in range(HEADS):
            oh = hg_o[:, h * HEAD_DIM : (h + 1) * HEAD_DIM]
            r = lax.rsqrt(jnp.mean(oh * oh, axis=-1, keepdims=True) + RMS_EPS)
            r_parts.append(jnp.broadcast_to(r, oh.shape))
            yn_parts.append(oh * r)
        r_hg = jnp.concatenate(r_parts, axis=-1)
        yn_hg = jnp.concatenate(yn_parts, axis=-1)
        hn = yn_hg * gain
        hz = hz_ref[...]
        sig_hz = _sigmoid(hz)
        silu_hz = hz * sig_hz
        a_hg_f = hn * silu_hz
        a_hg = a_hg_f.astype(BF16)
        u_hg = _dot(a_hg, whg_ref[...])

        gates = _sigmoid(gl_ref[...] + bg_ref[...])
        g_sb = gates[:, 0:D_MODEL]
        g_hg = gates[:, D_MODEL:]
        y_f = g_sb * u_sb + g_hg * u_hg
        y = y_f.astype(BF16)
        out = x_ref[...] + _dot(y, wout_ref[...])
        r2 = lax.rsqrt(jnp.mean(out * out, axis=-1, keepdims=True) + RMS_EPS)
        yn = out * r2
        fg = fg_ref[...]
        diff = yn * fg - tgt_ref[...]
        loss_ref[...] += 0.5 * inv_d * jnp.sum(diff * diff)

        dyf = diff * inv_d
        dfg_ref[...] += jnp.sum(dyf * yn, axis=0, keepdims=True)
        dyn = dyf * fg
        dout = r2 * (dyn - yn * jnp.mean(dyn * yn, axis=-1, keepdims=True))
        dout_ref[...] = dout
        doutb = dout.astype(BF16)
        doutb_ref[...] = doutb
        dy = _dot_nt(doutb, wout_ref[...])
        du_sb = (dy * g_sb).astype(BF16)
        du_hg = (dy * g_hg).astype(BF16)
        dgl_sb = dy * u_sb * g_sb * (1.0 - g_sb)
        dgl_hg = dy * u_hg * g_hg * (1.0 - g_hg)
        dmid_ref[:, 2 * D_MODEL : 3 * D_MODEL] = dgl_sb.astype(BF16)
        dmid_ref[:, 3 * D_MODEL :] = dgl_hg.astype(BF16)
        dbg_ref[:, 0:D_MODEL] += jnp.sum(dgl_sb, axis=0, keepdims=True)
        dbg_ref[:, D_MODEL:] += jnp.sum(dgl_hg, axis=0, keepdims=True)

        da_sb = _dot_nt(du_sb, wsb_ref[...])
        dsbo_ref[...] = (da_sb * silu_zsb).astype(BF16)
        dmid_ref[:, 0:D_MODEL] = (da_sb * sb_o * (sig_zsb * (1.0 + z_sb * (1.0 - sig_zsb)))).astype(BF16)

        da_hg = _dot_nt(du_hg, whg_ref[...])
        dhn = da_hg * silu_hz
        dmid_ref[:, D_MODEL : 2 * D_MODEL] = (da_hg * hn * (sig_hz * (1.0 + hz * (1.0 - sig_hz)))).astype(BF16)
        dhgn_ref[...] += jnp.sum(dhn * yn_hg, axis=0, keepdims=True)
        dyn_hg = dhn * gain
        prod = dyn_hg * yn_hg
        m_parts = []
        for h in range(HEADS):
            ph = prod[:, h * HEAD_DIM : (h + 1) * HEAD_DIM]
            m_parts.append(jnp.broadcast_to(jnp.mean(ph, axis=-1, keepdims=True), ph.shape))
        dhgo_ref[...] = (r_hg * (dyn_hg - yn_hg * jnp.concatenate(m_parts, axis=-1))).astype(BF16)

        asb_ref[...] = a_sb_f.T.astype(BF16)
        dusb_ref[...] = du_sb
        ahg_ref[...] = a_hg_f.T.astype(BF16)
        duhg_ref[...] = du_hg
        y_ref[...] = y_f.T.astype(BF16)

    def tile(width, off=0):
        return pl.BlockSpec((ts, width), lambda s: (s, off // width))

    def across():
        return pl.BlockSpec((D_MODEL, ts), lambda s: (0, s))

    def whole(shape):
        return pl.BlockSpec(shape, lambda s: (0,) * len(shape))

    def weight():
        return pl.BlockSpec((D_MODEL, D_MODEL), lambda s: (0, 0), pipeline_mode=pl.Buffered(1))

    f32_act = jax.ShapeDtypeStruct((s_len, D_MODEL), F32)
    bf_act = jax.ShapeDtypeStruct((s_len, D_MODEL), BF16)
    bf_act_t = jax.ShapeDtypeStruct((D_MODEL, s_len), BF16)
    return pl.pallas_call(
        body,
        name="mid",
        grid=(s_len // ts,),
        in_specs=[
            tile(D_MODEL, OFF_SB_Z), tile(D_MODEL, OFF_HG_Z), tile(2 * D_MODEL, OFF_GATE),
            tile(D_MODEL), tile(D_MODEL), tile(D_MODEL), tile(D_MODEL),
            whole((1, 2 * D_MODEL)), whole((1, D_MODEL)), whole((1, D_MODEL)),
            weight(), weight(), weight(),
        ],
        out_specs=[
            tile(D_MODEL), tile(D_MODEL), tile(D_MODEL), tile(4 * D_MODEL),
            across(), tile(D_MODEL), across(), tile(D_MODEL), across(), tile(D_MODEL),
            whole((1, 1)), whole((1, D_MODEL)), whole((1, 2 * D_MODEL)), whole((1, D_MODEL)),
        ],
        out_shape=[
            f32_act, bf_act, bf_act, jax.ShapeDtypeStruct((s_len, 4 * D_MODEL), BF16),
            bf_act_t, bf_act, bf_act_t, bf_act, bf_act_t, bf_act,
            jax.ShapeDtypeStruct((1, 1), F32), jax.ShapeDtypeStruct((1, D_MODEL), F32),
            jax.ShapeDtypeStruct((1, 2 * D_MODEL), F32), jax.ShapeDtypeStruct((1, D_MODEL), F32),
        ],
        compiler_params=_cparams(("arbitrary",)),
    )(proj, proj, proj, sb_o, hg_o, x, target, b_gate, hg_gain, final_g, w_sb, w_hg, w_out)


def _grad_square(a_t, b, name):
    s_len = b.shape[0]
    tk = min(1024, s_len)

    def body(a_ref, b_ref, o_ref):
        @pl.when(pl.program_id(0) == 0)
        def _():
            o_ref[...] = jnp.zeros_like(o_ref)

        o_ref[...] += _dot(a_ref[...], b_ref[...])

    return pl.pallas_call(
        body,
        name=name,
        grid=(s_len // tk,),
        in_specs=[pl.BlockSpec((D_MODEL, tk), lambda k: (0, k)), pl.BlockSpec((tk, D_MODEL), lambda k: (k, 0))],
        out_specs=pl.BlockSpec((D_MODEL, D_MODEL), lambda k: (0, 0)),
        out_shape=jax.ShapeDtypeStruct((D_MODEL, D_MODEL), F32),
        compiler_params=_cparams(("arbitrary",)),
    )(a_t, b)


SEG_WIDTHS = (1024, 1024, 1024, 4096, 3072)
SEG_TILE = 1024
SEG_BOUNDS = (0, 1, 2, 3, 7, 10)


def _w_in_tile(k):
    return jnp.where(k < 4, k, jnp.where(k < 7, k + 3, k - 3))


def _grad_w_in(h_t, segs):
    m, s_len = h_t.shape
    tk = min(1024, s_len)
    tn = SEG_TILE
    nk = s_len // tk
    bounds = SEG_BOUNDS

    def body(a_ref, *refs):
        seg_refs, o_ref = refs[:-1], refs[-1]
        j = pl.program_id(0)

        @pl.when(pl.program_id(1) == 0)
        def _():
            o_ref[...] = jnp.zeros_like(o_ref)

        for i, ref in enumerate(seg_refs):
            @pl.when((j >= bounds[i]) & (j < bounds[i + 1]))
            def _(ref=ref):
                o_ref[...] += _dot(a_ref[...], ref[...])

    def seg_spec(lo, hi):
        def index(j, k):
            return (jnp.where(j < lo, 0, jnp.where(j >= hi, nk - 1, k)), jnp.clip(j - lo, 0, hi - lo - 1))
        return pl.BlockSpec((tk, tn), index)

    return pl.pallas_call(
        body,
        name="grad_w_in",
        grid=(IN_WIDTH // tn, nk),
        in_specs=[pl.BlockSpec((m, tk), lambda j, k: (0, k))]
        + [seg_spec(bounds[i], bounds[i + 1]) for i in range(len(SEG_WIDTHS))],
        out_specs=pl.BlockSpec((m, tn), lambda j, k: (0, _w_in_tile(j))),
        out_shape=jax.ShapeDtypeStruct((m, IN_WIDTH), F32),
        compiler_params=_cparams(("arbitrary", "arbitrary")),
    )(h_t, *segs)


EXCHANGE_IN_PIECES = 8
EXCHANGE_PIECES = EXCHANGE_IN_PIECES + 3


def _exchange_copies(sin_ref, ssq_ref, got_in, got_sq, send_sems, recv_sems):
    _, _, c, chips = _position()
    rows = HALF_IN // EXCHANGE_IN_PIECES
    copies = []
    for k, (px, py) in enumerate(chips):
        chip = 2 * px + py
        for p in range(EXCHANGE_PIECES):
            if p < EXCHANGE_IN_PIECES:
                src, dst = sin_ref.at[chip, pl.ds(p * rows, rows), :], got_in.at[k, pl.ds(p * rows, rows), :]
            else:
                src, dst = ssq_ref.at[p - EXCHANGE_IN_PIECES, chip], got_sq.at[k, p - EXCHANGE_IN_PIECES]
            copies.append(_remote(src, dst, send_sems.at[k, p], recv_sems.at[k, p], (px, py, c)))
    return copies


def _dx(segs, w_all, x, norm_g, dout, s_in, s_sq):
    s_len = x.shape[0]
    ts = min(1024, s_len)
    tk = SEG_TILE
    nk = IN_WIDTH // tk
    ns = s_len // ts
    bounds = SEG_BOUNDS
    n_seg = len(SEG_WIDTHS)

    def body(*refs):
        seg_refs = refs[:n_seg]
        w_ref, x_ref, g_ref, dout_ref, sin_ref, ssq_ref, gx_ref, dg_ref, got_in, got_sq, acc, send_sems, recv_sems = refs[n_seg:]
        s, k = pl.program_id(0), pl.program_id(1)

        @pl.when((s == 0) & (k == 0))
        def _():
            dg_ref[...] = jnp.zeros_like(dg_ref)
            for cp in _exchange_copies(sin_ref, ssq_ref, got_in, got_sq, send_sems, recv_sems):
                cp.start()

        @pl.when(k == 0)
        def _():
            acc[...] = jnp.zeros_like(acc)

        for i, ref in enumerate(seg_refs):
            @pl.when((k >= bounds[i]) & (k < bounds[i + 1]))
            def _(ref=ref):
                acc[...] += _dot_nt(ref[...], w_ref[...])

        @pl.when(k == nk - 1)
        def _():
            dh = acc[...]
            xv = x_ref[...]
            r = lax.rsqrt(jnp.mean(xv * xv, axis=-1, keepdims=True) + RMS_EPS)
            xn = xv * r
            dg_ref[...] += jnp.sum(dh * xn, axis=0, keepdims=True)
            dxn = dh * g_ref[...]
            gx_ref[...] = r * (dxn - xn * jnp.mean(dxn * xn, axis=-1, keepdims=True)) + dout_ref[...]

        @pl.when((s == ns - 1) & (k == nk - 1))
        def _():
            for cp in _exchange_copies(sin_ref, ssq_ref, got_in, got_sq, send_sems, recv_sems):
                cp.wait()

    def seg_spec(lo, hi):
        return pl.BlockSpec((ts, tk), lambda s, k: (s, jnp.clip(k - lo, 0, hi - lo - 1)))

    row_tile = pl.BlockSpec((ts, D_MODEL), lambda s, k: (s, 0))
    vec = pl.BlockSpec((1, D_MODEL), lambda s, k: (0, 0))
    return pl.pallas_call(
        body,
        name="dx",
        grid=(ns, nk),
        in_specs=[seg_spec(bounds[i], bounds[i + 1]) for i in range(n_seg)] + [
            pl.BlockSpec((D_MODEL, tk), lambda s, k: (0, _w_in_tile(k))),
            row_tile, vec, row_tile, ANY, ANY,
        ],
        out_specs=[row_tile, vec, ANY, ANY],
        out_shape=[jax.ShapeDtypeStruct((s_len, D_MODEL), F32), jax.ShapeDtypeStruct((1, D_MODEL), F32),
                   jax.ShapeDtypeStruct((3, HALF_IN, W_IN_SHARD), WIRE),
                   jax.ShapeDtypeStruct((3, 3, HALF_SQ, D_MODEL), WIRE)],
        scratch_shapes=[pltpu.VMEM((ts, D_MODEL), F32),
                        pltpu.SemaphoreType.DMA((3, EXCHANGE_PIECES)), pltpu.SemaphoreType.DMA((3, EXCHANGE_PIECES))],
        compiler_params=_cparams(("arbitrary", "arbitrary"), vmem=VMEM_LIMIT_DX),
    )(*segs, w_all, x, norm_g, dout, s_in, s_sq)


def _local_grads(x, target, proj, h_t, qkv, b_gate, lbl, hg_gain, final_g, w_sb, w_hg, w_out):
    sb_o, sb_o_fine = _sb_fwd(qkv)
    hg_o, states = _hg_fwd(proj, lbl)
    (dout, d_sbo, d_hgo, d_mid, a_sb, du_sb, a_hg, du_hg, y, doutb,
     loss, d_fg, d_bg, d_hgn) = _mid(proj, sb_o, hg_o, x, target, b_gate, hg_gain, final_g, w_sb, w_hg, w_out)
    g_w_sb = _grad_square(a_sb, du_sb, "grad_w_sb")
    g_w_hg = _grad_square(a_hg, du_hg, "grad_w_hg")
    g_w_out = _grad_square(y, doutb, "grad_w_out")
    d_q, d_k, d_v = _sb_bwd(qkv, sb_o_fine, d_sbo)
    d_hg, d_lb = _hg_bwd(proj, lbl, states, d_hgo)
    segs = (d_q, d_k, d_v, d_mid, d_hg)
    g_w_in = _grad_w_in(h_t, segs)
    return g_w_in, g_w_sb, g_w_hg, g_w_out, segs, dout, loss, d_bg, d_lb, d_hgn, d_fg


ANY = pl.BlockSpec(memory_space=pl.ANY)
WIRE = BF16
HALF_IN = D_MODEL // 2
HALF_SQ = ROW_SHARD // 2


def _position():
    x, y, c = lax.axis_index("x"), lax.axis_index("y"), lax.axis_index("c")
    chips = [(1 - x, y), (x, 1 - y), (1 - x, 1 - y)]
    return x, y, c, chips


def _remote(src, dst, send_sem, recv_sem, to):
    return pltpu.make_async_remote_copy(src_ref=src, dst_ref=dst, send_sem=send_sem, recv_sem=recv_sem,
                                        device_id=to, device_id_type=MESH)


PROJ_TILE = 1280
F32_FROM_TILE = 2
BF16_TO_TILE = 2


def _gather_inproj(idx, h, w_in_b, w_sq_b):
    s_len = h.shape[0]
    ts = min(1024, s_len)
    ns = s_len // ts
    per = W_IN_SHARD // PROJ_TILE
    n_in = 4
    n_piece = n_in + 3
    rows = HALF_IN // n_in

    def chip_at(r, me):
        return me ^ jnp.where(r == 1, 2, jnp.where(r == 2, 1, jnp.where(r == 3, 3, 0)))

    def body(idx_ref, h_ref, win_ref, wsqb_ref, proj_ref, qkv_ref, wall_ref, wsq_ref, wbuf, send_sems, recv_sems, w_sem):
        r, t, s = pl.program_id(0), pl.program_id(1), pl.program_id(2)
        x, y, c, chips = _position()
        me = 2 * x + y
        sibling = (x, y, 1 - c)
        first = (t == 0) & (s == 0)

        def src_piece(p):
            if p < n_in:
                return win_ref.at[pl.ds(c * HALF_IN + p * rows, rows), :]
            return wsqb_ref.at[p - n_in, pl.ds(c * HALF_SQ, HALF_SQ), :]

        def piece(p, chip, core):
            if p < n_in:
                cols = pl.ds(pl.multiple_of(chip * W_IN_SHARD, W_IN_SHARD), W_IN_SHARD)
                return wall_ref.at[pl.ds(core * HALF_IN + p * rows, rows), cols]
            return wsq_ref.at[p - n_in, chip, pl.ds(core * HALF_SQ, HALF_SQ), :]

        def send(k, p):
            px, py = chips[k]
            return _remote(src_piece(p), piece(p, me, c), send_sems.at[k, p], recv_sems.at[k, p], (px, py, c))

        def forward(k, p, core):
            px, py = chips[k]
            got = piece(p, 2 * px + py, core)
            return _remote(got, got, send_sems.at[3 + k, p], recv_sems.at[3 + k, p], sibling)

        @pl.when((r == 0) & first)
        def _():
            for k in range(3):
                for p in range(n_piece):
                    send(k, p).start()

        for k in range(3):
            @pl.when((r == k + 1) & first)
            def _(k=k):
                px, py = chips[k]
                for p in range(n_piece):
                    got = piece(p, 2 * px + py, c)
                    _remote(got, got, send_sems.at[k, p], recv_sems.at[k, p], (px, py, c)).wait_recv()
                    forward(k, p, c).start()
                for p in range(n_piece):
                    forward(k, p, 1 - c).wait_recv()

        @pl.when(s == 0)
        def _():
            col = pl.multiple_of(t * PROJ_TILE, PROJ_TILE)

            @pl.when(r == 0)
            def _():
                cp = pltpu.make_async_copy(win_ref.at[:, pl.ds(col, PROJ_TILE)], wbuf, w_sem)
                cp.start()
                cp.wait()

            @pl.when(r > 0)
            def _():
                off = pl.multiple_of(chip_at(r, me) * W_IN_SHARD + col, PROJ_TILE)
                cp = pltpu.make_async_copy(wall_ref.at[:, pl.ds(off, PROJ_TILE)], wbuf, w_sem)
                cp.start()
                cp.wait()

        p = _dot(h_ref[...], wbuf[...])
        tile_now = per * chip_at(r, me) + t

        @pl.when(tile_now >= F32_FROM_TILE)
        def _():
            proj_ref[...] = p

        @pl.when(tile_now <= BF16_TO_TILE)
        def _():
            qkv_ref[...] = p.astype(BF16)

        @pl.when((r == 3) & (t == per - 1) & (s == ns - 1))
        def _():
            for k in range(3):
                for p in range(n_piece):
                    send(k, p).wait_send()
                    forward(k, p, c).wait_send()

    def out_index(wanted):
        order = [0, 2, 1, 3]
        table = []
        for chip in range(N_CHIPS):
            tiles = [per * (chip ^ order[q // per]) + q % per for q in range(N_CHIPS * per)]
            row = []
            for q, tile in enumerate(tiles):
                if wanted(tile):
                    row.append((tile, None))
                    continue
                before = [u for u in tiles[:q] if wanted(u)]
                after = [u for u in tiles[q:] if wanted(u)]
                row.append((before[-1], ns - 1) if before else (after[0], 0))
            table.append(row)

        def index(r, t, s, idx):
            q = r * per + t
            col, fixed_s = jnp.int32(0), jnp.int32(-1)
            for chip in range(N_CHIPS):
                for pos, (tile, hold) in enumerate(table[chip]):
                    here = (idx[0] == chip) & (q == pos)
                    col = jnp.where(here, tile, col)
                    fixed_s = jnp.where(here, -1 if hold is None else hold, fixed_s)
            return jnp.where(fixed_s < 0, s, fixed_s), col

        return index

    grid_spec = pltpu.PrefetchScalarGridSpec(
        num_scalar_prefetch=1,
        grid=(N_CHIPS, per, ns),
        in_specs=[pl.BlockSpec((ts, D_MODEL), lambda r, t, s, idx: (s, 0)), ANY, ANY],
        out_specs=[pl.BlockSpec((ts, PROJ_TILE), out_index(lambda tile: tile >= F32_FROM_TILE)),
                   pl.BlockSpec((ts, PROJ_TILE), out_index(lambda tile: tile <= BF16_TO_TILE)),
                   ANY, ANY],
        scratch_shapes=[pltpu.VMEM((D_MODEL, PROJ_TILE), BF16),
                        pltpu.SemaphoreType.DMA((6, n_piece)), pltpu.SemaphoreType.DMA((6, n_piece)),
                        pltpu.SemaphoreType.DMA(())],
    )
    return pl.pallas_call(
        body,
        name="gather_inproj",
        grid_spec=grid_spec,
        out_shape=[jax.ShapeDtypeStruct((s_len, IN_WIDTH), F32),
                   jax.ShapeDtypeStruct((s_len, IN_WIDTH), BF16),
                   jax.ShapeDtypeStruct((D_MODEL, IN_WIDTH), BF16),
                   jax.ShapeDtypeStruct((3, N_CHIPS, ROW_SHARD, D_MODEL), BF16)],
        compiler_params=_cparams(("arbitrary", "arbitrary", "arbitrary")),
    )(idx, h, w_in_b, w_sq_b)


def _place_own(idx, w_in_b, w_sq_b, w_all, wsq):
    n = 4
    r_in, r_sq = D_MODEL // n, ROW_SHARD // n

    def body(idx_ref, win_ref, wsq_ref, w_all_in, wsq_in, w_all_out, wsq_out):
        w_all_out[...] = win_ref[...]
        wsq_out[:, 0] = wsq_ref[...]

    grid_spec = pltpu.PrefetchScalarGridSpec(
        num_scalar_prefetch=1,
        grid=(n,),
        in_specs=[pl.BlockSpec((r_in, W_IN_SHARD), lambda r, idx: (r, 0)),
                  pl.BlockSpec((3, r_sq, D_MODEL), lambda r, idx: (0, r, 0)), ANY, ANY],
        out_specs=[pl.BlockSpec((r_in, W_IN_SHARD), lambda r, idx: (r, idx[0])),
                   pl.BlockSpec((3, 1, r_sq, D_MODEL), lambda r, idx: (0, idx[0], r, 0))],
    )
    return pl.pallas_call(
        body,
        name="place_own",
        grid_spec=grid_spec,
        out_shape=[jax.ShapeDtypeStruct(w_all.shape, BF16), jax.ShapeDtypeStruct(wsq.shape, BF16)],
        input_output_aliases={3: 0, 4: 1},
        compiler_params=_cparams(("arbitrary",)),
    )(idx, w_in_b, w_sq_b, w_all, wsq)


def _swap_halves(g_in, g_sq):
    n_in = 16
    n_piece = n_in + 3 * N_CHIPS
    rows = HALF_IN // n_in

    def body(gin_ref, gsq_ref, got_in, got_sq, send_sems, recv_sems):
        x, y, c, _ = _position()
        sibling = (x, y, 1 - c)

        def src_piece(p):
            if p < n_in:
                return gin_ref.at[pl.ds((1 - c) * HALF_IN + p * rows, rows), :]
            a, chip = divmod(p - n_in, N_CHIPS)
            return gsq_ref.at[a, chip, pl.ds((1 - c) * HALF_SQ, HALF_SQ), :]

        def dst_piece(p):
            if p < n_in:
                return got_in.at[pl.ds(p * rows, rows), :]
            a, chip = divmod(p - n_in, N_CHIPS)
            return got_sq.at[a, chip]

        out = [_remote(src_piece(p), dst_piece(p), send_sems.at[p], recv_sems.at[p], sibling) for p in range(n_piece)]
        for cp in out:
            cp.start()
        for cp in out:
            cp.wait()

    return pl.pallas_call(
        body,
        name="swap_halves",
        in_specs=[ANY, ANY],
        out_specs=[ANY, ANY],
        out_shape=[jax.ShapeDtypeStruct((HALF_IN, IN_WIDTH), F32),
                   jax.ShapeDtypeStruct((3, N_CHIPS, HALF_SQ, D_MODEL), F32)],
        scratch_shapes=[pltpu.SemaphoreType.DMA((n_piece,))] * 2,
    )(g_in, g_sq)


def _join_halves(r_in, r_sq):
    n_in = 16
    n_piece = n_in + 3
    rows = HALF_IN // n_in

    def body(in_alias, sq_alias, full_in, full_sq, send_sems, recv_sems):
        del in_alias, sq_alias
        x, y, c, _ = _position()
        sibling = (x, y, 1 - c)

        def piece(p, core):
            if p < n_in:
                return full_in.at[pl.ds(core * HALF_IN + p * rows, rows), :]
            return full_sq.at[p - n_in, pl.ds(core * HALF_SQ, HALF_SQ), :]

        out = [_remote(piece(p, c), piece(p, c), send_sems.at[p], recv_sems.at[p], sibling) for p in range(n_piece)]
        for cp in out:
            cp.start()
        for p in range(n_piece):
            _remote(piece(p, 1 - c), piece(p, 1 - c), send_sems.at[p], recv_sems.at[p], sibling).wait_recv()
        for cp in out:
            cp.wait_send()

    return pl.pallas_call(
        body,
        name="join_halves",
        in_specs=[ANY, ANY],
        out_specs=[ANY, ANY],
        out_shape=[jax.ShapeDtypeStruct((D_MODEL, W_IN_SHARD), F32),
                   jax.ShapeDtypeStruct((3, ROW_SHARD, D_MODEL), F32)],
        input_output_aliases={0: 0, 1: 1},
        scratch_shapes=[pltpu.SemaphoreType.DMA((n_piece,)), pltpu.SemaphoreType.DMA((n_piece,))],
    )(r_in, r_sq)


SMALL_ROWS = 56
N_DEV = 8


def _sum_small(part):
    def body(part_ref, out_ref, slots, send_sems, recv_sems):
        x, y, c, _ = _position()
        me = 4 * x + 2 * y + c
        slots[me] = part_ref[...]
        out = []
        for r in range(1, N_DEV):
            rx, ry, rc = (r >> 2) & 1, (r >> 1) & 1, r & 1
            to = (1 - x if rx else x, 1 - y if ry else y, 1 - c if rc else c)
            out.append(_remote(part_ref, slots.at[me], send_sems.at[r - 1], recv_sems.at[r - 1], to))
        for cp in out:
            cp.start()
        for r in range(1, N_DEV):
            _remote(part_ref, slots.at[me ^ r], send_sems.at[r - 1], recv_sems.at[r - 1], (x, y, c)).wait_recv()
        for cp in out:
            cp.wait_send()
        total = slots[0]
        for d in range(1, N_DEV):
            total = total + slots[d]
        out_ref[...] = total

    vmem = pl.BlockSpec(memory_space=pltpu.VMEM)
    return pl.pallas_call(
        body,
        name="sum_small",
        in_specs=[vmem],
        out_specs=vmem,
        out_shape=jax.ShapeDtypeStruct((SMALL_ROWS, HEAD_DIM), F32),
        scratch_shapes=[pltpu.VMEM((N_DEV, SMALL_ROWS, HEAD_DIM), F32),
                        pltpu.SemaphoreType.DMA((N_DEV - 1,)), pltpu.SemaphoreType.DMA((N_DEV - 1,))],
    )(part)


def _prefetch_call(body, name, idx, grid, in_specs, out_specs, out_shape, args):
    grid_spec = pltpu.PrefetchScalarGridSpec(num_scalar_prefetch=1, grid=grid, in_specs=in_specs, out_specs=out_specs)
    return pl.pallas_call(body, name=name, grid_spec=grid_spec, out_shape=out_shape,
                          compiler_params=_cparams(("arbitrary",) * len(grid)))(idx, *args)


def _sum_a_in(idx, g_in, got_in):
    tr = 128
    nr = HALF_IN // tr

    def body(idx_ref, a_ref, b_ref, o_ref):
        o_ref[0] = (a_ref[...] + b_ref[...]).astype(WIRE)

    return _prefetch_call(
        body, "sum_a_in", idx, (N_CHIPS, nr),
        [pl.BlockSpec((tr, W_IN_SHARD), lambda j, r, idx: (idx[1] * nr + r, j)),
         pl.BlockSpec((tr, W_IN_SHARD), lambda j, r, idx: (r, j))],
        pl.BlockSpec((1, tr, W_IN_SHARD), lambda j, r, idx: (j, r, 0)),
        jax.ShapeDtypeStruct((N_CHIPS, HALF_IN, W_IN_SHARD), WIRE), (g_in, got_in))


def _sum_a_sq(idx, g_sq, got_sq):
    blk = (1, 1, HALF_SQ, D_MODEL)

    def body(idx_ref, a_ref, b_ref, o_ref):
        o_ref[...] = (a_ref[...] + b_ref[...]).astype(WIRE)

    return _prefetch_call(
        body, "sum_a_sq", idx, (3, N_CHIPS),
        [pl.BlockSpec(blk, lambda a, j, idx: (a, j, idx[1], 0)), pl.BlockSpec(blk, lambda a, j, idx: (a, j, 0, 0))],
        pl.BlockSpec(blk, lambda a, j, idx: (a, j, 0, 0)),
        jax.ShapeDtypeStruct((3, N_CHIPS, HALF_SQ, D_MODEL), WIRE), (g_sq, got_sq))


def _sum_b_in(idx, s_in, got_in):
    tr = 128
    nr = HALF_IN // tr

    def body(idx_ref, a_ref, b_ref, o_ref):
        o_ref[...] = ((a_ref[0].astype(F32) + b_ref[0].astype(F32)) + b_ref[1].astype(F32)) + b_ref[2].astype(F32)

    return _prefetch_call(
        body, "sum_b_in", idx, (nr,),
        [pl.BlockSpec((1, tr, W_IN_SHARD), lambda r, idx: (idx[0], r, 0)),
         pl.BlockSpec((3, tr, W_IN_SHARD), lambda r, idx: (0, r, 0))],
        pl.BlockSpec((tr, W_IN_SHARD), lambda r, idx: (idx[1] * nr + r, 0)),
        jax.ShapeDtypeStruct((D_MODEL, W_IN_SHARD), F32), (s_in, got_in))


def _sum_b_sq(idx, s_sq, got_sq):
    def body(idx_ref, a_ref, b_ref, o_ref):
        o_ref[0] = ((a_ref[0, 0].astype(F32) + b_ref[0, 0].astype(F32)) + b_ref[1, 0].astype(F32)) + b_ref[2, 0].astype(F32)

    return _prefetch_call(
        body, "sum_b_sq", idx, (3,),
        [pl.BlockSpec((1, 1, HALF_SQ, D_MODEL), lambda a, idx: (a, idx[0], 0, 0)),
         pl.BlockSpec((3, 1, HALF_SQ, D_MODEL), lambda a, idx: (0, a, 0, 0))],
        pl.BlockSpec((1, HALF_SQ, D_MODEL), lambda a, idx: (a, idx[1], 0)),
        jax.ShapeDtypeStruct((3, ROW_SHARD, D_MODEL), F32), (s_sq, got_sq))


def _adamw_math(w, g, m, v):
    m = ADAM_B1 * m + (1.0 - ADAM_B1) * g
    v = ADAM_B2 * v + (1.0 - ADAM_B2) * (g * g)
    m_hat = m / (1.0 - ADAM_B1 ** ADAM_STEP)
    v_hat = v / (1.0 - ADAM_B2 ** ADAM_STEP)
    delta = -ADAM_LR * (m_hat / (jnp.sqrt(v_hat) + ADAM_EPS) + ADAM_WD * w)
    return delta, m, v


def _adamw(w, g, m, v, name):
    rows, cols = w.shape
    tr = min(128, rows)

    def body(w_ref, g_ref, m_ref, v_ref, d_ref, nm_ref, nv_ref):
        d_ref[...], nm_ref[...], nv_ref[...] = _adamw_math(w_ref[...], g_ref[...], m_ref[...], v_ref[...])

    spec = pl.BlockSpec((tr, cols), lambda r: (r, 0))
    return pl.pallas_call(
        body,
        name=name,
        grid=(rows // tr,),
        in_specs=[spec] * 4,
        out_specs=[spec] * 3,
        out_shape=[jax.ShapeDtypeStruct((rows, cols), F32)] * 3,
        compiler_params=_cparams(("arbitrary",)),
    )(w, g, m, v)


def _adamw_small(sums, w, m, v):
    def body(s_ref, w_ref, m_ref, v_ref, loss_ref, g_ref, d_ref, nm_ref, nv_ref):
        s = s_ref[...]
        w = w_ref[...]
        loss_ref[...] = s[0:1, 0:1]
        l0, l1 = w[24:32], w[32:40]
        mx = jnp.maximum(l0, l1)
        e0, e1 = jnp.exp(l0 - mx), jnp.exp(l1 - mx)
        p0, p1 = e0 / (e0 + e1), e1 / (e0 + e1)
        d_lb = s[32:40]
        g = jnp.concatenate([s[8:16], s[16:32], d_lb * p0 * (1.0 - p0), -d_lb * p0 * p1, s[40:48], s[48:56]], axis=0)
        g_ref[...] = g
        d_ref[...], nm_ref[...], nv_ref[...] = _adamw_math(w, g, m_ref[...], v_ref[...])

    packed = jax.ShapeDtypeStruct((SMALL_ROWS, HEAD_DIM), F32)
    return pl.pallas_call(
        body,
        name="adamw_small",
        out_shape=[jax.ShapeDtypeStruct((1, 1), F32), packed, packed, packed, packed],
    )(sums, w, m, v)


def _pack_small(ng, bg, lbl, hgn, fg):
    return jnp.concatenate([a.reshape(-1, HEAD_DIM) for a in (ng, bg, lbl, hgn, fg)], axis=0)


def _unpack_small(p):
    return (p[0:8].reshape(1, D_MODEL), p[8:24].reshape(1, 2 * D_MODEL), p[24:40].reshape(2, HEADS, HEAD_DIM),
            p[40:48].reshape(1, HEADS, HEAD_DIM), p[48:56].reshape(D_MODEL))


def kernel(x, norm_g, w_in, b_gate, lb_logits, hg_norm_g, w_sb_proj, w_hg_proj, w_out, final_norm_g, loss_target, m_norm_g, m_w_in, m_b_gate, m_lb_logits, m_hg_norm_g, m_w_sb_proj, m_w_hg_proj, m_w_out, m_final_norm_g, v_norm_g, v_w_in, v_b_gate, v_lb_logits, v_hg_norm_g, v_w_sb_proj, v_w_hg_proj, v_w_out, v_final_norm_g):
    s_len = x.shape[1]
    w_sq = jnp.stack([w_sb_proj[0], w_hg_proj[0], w_out[0]])
    idx = jnp.stack([2 * lax.axis_index("x") + lax.axis_index("y"), lax.axis_index("c")]).astype(jnp.int32)
    w_in_b, w_sq_b = w_in[0].astype(BF16), w_sq.astype(BF16)
    h, h_t = _prenorm(x[0], norm_g)
    proj, qkv, w_all, wsq = _gather_inproj(idx, h, w_in_b, w_sq_b)
    w_all, wsq = _place_own(idx, w_in_b, w_sq_b, w_all, wsq)
    wsq = wsq.reshape(3, D_MODEL, D_MODEL)

    (g_in, g_sb, g_hg, g_out, segs, dout, loss, d_bg, d_lb, d_hgn, d_fg) = _local_grads(
        x[0], loss_target[0], proj, h_t, qkv, b_gate, lb_logits.reshape(2, D_MODEL), hg_norm_g.reshape(1, D_MODEL),
        final_norm_g.reshape(1, D_MODEL), wsq[0], wsq[1], wsq[2])

    g_sq = jnp.stack([g_sb, g_hg, g_out]).reshape(3, N_CHIPS, ROW_SHARD, D_MODEL)
    got_in, got_sq = _swap_halves(g_in, g_sq)
    s_in, s_sq = _sum_a_in(idx, g_in, got_in), _sum_a_sq(idx, g_sq, got_sq)
    grad_x, d_ng, got_in, got_sq = _dx(segs, w_all, x[0], norm_g, dout, s_in, s_sq)
    grad_in, grad_sq = _join_halves(_sum_b_in(idx, s_in, got_in), _sum_b_sq(idx, s_sq, got_sq))

    d_in, nm_in, nv_in = _adamw(w_in[0], grad_in, m_w_in[0], v_w_in[0], "adamw_in")
    flat = lambda a, b, c: jnp.concatenate([a[0], b[0], c[0]], axis=0)
    d_sq, nm_sq, nv_sq = _adamw(flat(w_sb_proj, w_hg_proj, w_out), grad_sq.reshape(3 * ROW_SHARD, D_MODEL),
                                flat(m_w_sb_proj, m_w_hg_proj, m_w_out), flat(v_w_sb_proj, v_w_hg_proj, v_w_out),
                                "adamw_sq")

    pad = jnp.zeros((8, HEAD_DIM), F32).at[0, 0].set(loss[0, 0])
    part = jnp.concatenate([pad] + [a.reshape(-1, HEAD_DIM) for a in (d_ng, d_bg, d_lb, d_hgn, d_fg)], axis=0)
    sums = _sum_small(part)
    loss_out, g_sm, d_sm, nm_sm, nv_sm = _adamw_small(
        sums, _pack_small(norm_g, b_gate, lb_logits, hg_norm_g, final_norm_g),
        _pack_small(m_norm_g, m_b_gate, m_lb_logits, m_hg_norm_g, m_final_norm_g),
        _pack_small(v_norm_g, v_b_gate, v_lb_logits, v_hg_norm_g, v_final_norm_g))

    def big(t_in, t_sq):
        sq = t_sq.reshape(3, 1, ROW_SHARD, D_MODEL)
        return t_in[None], sq[0], sq[1], sq[2]

    def order(small, in_, sb, hg, out):
        ng, bg, lbl, hgn, fg = small
        return [ng, in_, bg, lbl, hgn, sb, hg, out, fg]

    outs = [loss_out[0, 0], grad_x[None]]
    for small, (t_in, t_sq) in ((g_sm, (grad_in, grad_sq)), (d_sm, (d_in, d_sq)), (nm_sm, (nm_in, nm_sq)), (nv_sm, (nv_in, nv_sq))):
        outs += order(_unpack_small(small), *big(t_in, t_sq))
    return tuple(outs)
```

```python
import functools

import jax
import jax.numpy as jnp
from jax import lax
from jax.experimental import pallas as pl
from jax.experimental.pallas import tpu as pltpu

F32 = jnp.float32
BF16 = jnp.bfloat16

D_MODEL = 1024
HEADS = 8
HEAD_DIM = 128
IN_WIDTH = 10240
N_CHIPS = 4
W_IN_SHARD = IN_WIDTH // N_CHIPS
ROW_SHARD = D_MODEL // N_CHIPS
RMS_EPS = 1e-6

OFF_SB_Q, OFF_SB_K, OFF_SB_V, OFF_SB_Z = 0, 1024, 2048, 3072
OFF_HG_Q, OFF_HG_F, OFF_HG_I, OFF_HG_Z, OFF_GATE = 4096, 5120, 6144, 7168, 8192

SB_BLOCK = 256
SB_FWD_HEADS = 4
SB_BWD_HEADS = 2
SB_ROWS = 256
SB_DEAD = -110.0
SB_GONE = -1e30
HG_CHUNK = 32
HG_PAIR = 2 * HG_CHUNK
HG_STEP = 256
HG_MID = HG_CHUNK // 2 - 1

ADAM_LR, ADAM_B1, ADAM_B2, ADAM_EPS, ADAM_WD, ADAM_STEP = 0.001, 0.9, 0.999, 1e-08, 0.01, 10

VMEM_LIMIT = 56 * 1024 * 1024
VMEM_LIMIT_DX = 60 * 1024 * 1024

MESH = pl.DeviceIdType.MESH


def _cparams(sem, vmem=VMEM_LIMIT):
    return pltpu.CompilerParams(dimension_semantics=sem, vmem_limit_bytes=vmem)


def _dot(a, b):
    return jnp.dot(a, b, preferred_element_type=F32)


def _dot_nt(a, b):
    return lax.dot_general(a, b, (((1,), (1,)), ((), ())), preferred_element_type=F32)


def _dot_tn(a, b):
    return lax.dot_general(a, b, (((0,), (0,)), ((), ())), preferred_element_type=F32)


def _split_dot(x, tri):
    hi = x.astype(BF16)
    lo = (x - hi.astype(F32)).astype(BF16)
    both = _dot(jnp.concatenate([hi, lo], axis=0), tri)
    return both[: x.shape[0]] + both[x.shape[0] :]


def _split_dot_left(tri, x):
    hi = x.astype(BF16)
    lo = (x - hi.astype(F32)).astype(BF16)
    return _dot(tri, hi) + _dot(tri, lo)


def _sigmoid(x):
    return 1.0 / (1.0 + jnp.exp(-x))


def _prenorm(x, norm_g):
    s_len = x.shape[0]
    ts = min(1024, s_len)

    def body(x_ref, g_ref, h_ref, ht_ref):
        xv = x_ref[...]
        r = lax.rsqrt(jnp.mean(xv * xv, axis=-1, keepdims=True) + RMS_EPS)
        hv = (xv * r) * g_ref[...]
        h_ref[...] = hv.astype(BF16)
        ht_ref[...] = hv.T.astype(BF16)

    return pl.pallas_call(
        body,
        name="prenorm",
        grid=(s_len // ts,),
        in_specs=[pl.BlockSpec((ts, D_MODEL), lambda s: (s, 0)), pl.BlockSpec((1, D_MODEL), lambda s: (0, 0))],
        out_specs=[pl.BlockSpec((ts, D_MODEL), lambda s: (s, 0)), pl.BlockSpec((D_MODEL, ts), lambda s: (0, s))],
        out_shape=[jax.ShapeDtypeStruct((s_len, D_MODEL), BF16), jax.ShapeDtypeStruct((D_MODEL, s_len), BF16)],
        compiler_params=_cparams(("arbitrary",)),
    )(x, norm_g)


def _sb_scores(qb, kb, causal, tri_excl, diag):
    z = _dot_nt(qb, kb) * HEAD_DIM ** -0.5
    ls_pos = jnp.minimum(z, 0.0) - jnp.log1p(jnp.exp(-jnp.abs(z)))
    log_not = ls_pos - z
    log_not_m = jnp.where(causal, log_not, 0.0) if diag else log_not
    return ls_pos, log_not, log_not_m, _split_dot(log_not_m, tri_excl)


def _sb_weights(ls_pos, suffix, carry, causal, diag):
    surv = suffix + carry
    w = jnp.exp(ls_pos + surv)
    return surv, (jnp.where(causal, w, 0.0) if diag else w)


def _sb_specs(s_len, blk, heads):
    width = heads * HEAD_DIM

    def blk_spec(off):
        return pl.BlockSpec((blk, width), lambda h, i: (i, off // width + h))

    def head_spec(off, buffers=2):
        return pl.BlockSpec((s_len, width), lambda h, i: (0, off // width + h), pipeline_mode=pl.Buffered(buffers))

    return blk_spec, head_spec


def _head_cols(p):
    return slice(p * HEAD_DIM, (p + 1) * HEAD_DIM)


def _sb_chains(blk, heads):
    rows = min(SB_ROWS, blk)
    return [(p, a) for p in range(heads) for a in range(blk // rows)], rows


def _sb_masks(blk, rows):
    row = lax.broadcasted_iota(jnp.int32, (rows, blk), 0)
    col = lax.broadcasted_iota(jnp.int32, (rows, blk), 1)
    causal = [row + a * rows > col for a in range(blk // rows)]
    row = lax.broadcasted_iota(jnp.int32, (blk, blk), 0)
    col = lax.broadcasted_iota(jnp.int32, (blk, blk), 1)
    tri_excl = (row > col).astype(BF16)
    tri_incl = (row >= col).astype(BF16)
    return causal, tri_excl, tri_incl


def _sb_alive(st, n_chain):
    alive = functools.reduce(jnp.maximum, [st[1 + 3 * c] for c in range(n_chain)])
    return jnp.max(alive) > SB_DEAD


def _sb_fwd(qkv):
    s_len = qkv.shape[0]
    blk = min(SB_BLOCK, s_len)
    nq = s_len // blk
    chains, rows = _sb_chains(blk, SB_FWD_HEADS)

    def body(q_ref, k_ref, v_ref, o_ref, of_ref):
        i = pl.program_id(1)
        causal, tri_excl, _ = _sb_masks(blk, rows)

        def tiles(specs, st):
            pre = []
            for j, diag, _ in specs:
                start = pl.multiple_of(j * blk, blk)
                for p, a in chains:
                    kb = k_ref[pl.ds(start, blk), _head_cols(p)]
                    qb = q_ref[a * rows : (a + 1) * rows, _head_cols(p)]
                    pre.append(_sb_scores(qb, kb, causal[a], tri_excl, diag) + (v_ref[pl.ds(start, blk), _head_cols(p)],))
            for t, (j, diag, valid) in enumerate(specs):
                new = []
                for c, (p, a) in enumerate(chains):
                    carry, acc, acc_lo = st[3 * c : 3 * c + 3]
                    if valid is not None:
                        carry = jnp.where(valid, carry, SB_GONE)
                    ls_pos, _, log_not_m, suffix, vb = pre[t * len(chains) + c]
                    surv, w = _sb_weights(ls_pos, suffix, carry, causal[a], diag)
                    wb = w.astype(BF16)
                    w_lo = (w - wb.astype(F32)).astype(BF16)
                    both = _dot(jnp.concatenate([wb, w_lo], axis=0), vb)
                    new += [surv[:, 0:1] + log_not_m[:, 0:1], acc + both[:rows], acc_lo + both[rows:]]
                st = tuple(new)
            return st

        zero = jnp.zeros((rows, HEAD_DIM), F32)
        st = tiles([(i, True, None), (jnp.maximum(i - 1, 0), False, i >= 1)],
                   (jnp.zeros((rows, 1), F32), zero, zero) * len(chains))

        def more(st):
            return (st[0] < i) & _sb_alive(st, len(chains))

        def step(st):
            return (st[0] + 1,) + tiles([(i - 1 - st[0], False, None)], st[1:])

        st = lax.while_loop(more, step, (1,) + st)[1:]
        for c, (p, a) in enumerate(chains):
            o_ref[a * rows : (a + 1) * rows, _head_cols(p)] = st[3 * c + 1]
            of_ref[a * rows : (a + 1) * rows, _head_cols(p)] = st[3 * c + 1] + st[3 * c + 2]

    blk_spec, head_spec = _sb_specs(s_len, blk, SB_FWD_HEADS)
    return pl.pallas_call(
        body,
        name="sb_fwd",
        grid=(HEADS // SB_FWD_HEADS, nq),
        in_specs=[blk_spec(OFF_SB_Q), head_spec(OFF_SB_K), head_spec(OFF_SB_V)],
        out_specs=[blk_spec(0), blk_spec(0)],
        out_shape=[jax.ShapeDtypeStruct((s_len, D_MODEL), F32)] * 2,
        compiler_params=_cparams(("arbitrary", "arbitrary")),
    )(qkv, qkv, qkv)


def _sb_bwd(qkv, o_fine, d_o):
    s_len = qkv.shape[0]
    blk = min(SB_BLOCK, s_len)
    nq = s_len // blk
    scale = HEAD_DIM ** -0.5
    chains, rows = _sb_chains(blk, SB_BWD_HEADS)

    def body(q_ref, k_ref, v_ref, of_ref, do_ref, dq_ref, dk_ref, dv_ref, dk_acc, dv_acc):
        i = pl.program_id(1)

        @pl.when(i == 0)
        def _():
            dk_acc[...] = jnp.zeros_like(dk_acc)
            dv_acc[...] = jnp.zeros_like(dv_acc)

        dob = do_ref[...].astype(BF16)
        prod = dob.astype(F32) * of_ref[...]
        causal, tri_excl, tri_incl = _sb_masks(blk, rows)

        def group(x, p, a):
            return x[a * rows : (a + 1) * rows, _head_cols(p)]

        totals = [jnp.sum(group(prod, p, a), axis=-1, keepdims=True) for p, a in chains]

        def tiles(specs, st):
            pre = []
            for j, diag, _ in specs:
                start = pl.multiple_of(j * blk, blk)
                for p, a in chains:
                    kb = k_ref[pl.ds(start, blk), _head_cols(p)]
                    vb = v_ref[pl.ds(start, blk), _head_cols(p)]
                    qb, dob_c = group(q_ref, p, a), group(dob, p, a)
                    pre.append(_sb_scores(qb, kb, causal[a], tri_excl, diag) + (_dot_nt(dob_c, vb), qb, kb, dob_c))
            for t, (j, diag, valid) in enumerate(specs):
                start = pl.multiple_of(j * blk, blk)
                mids = []
                for c, (p, a) in enumerate(chains):
                    c_not = st[3 * c]
                    if valid is not None:
                        c_not = jnp.where(valid, c_not, SB_GONE)
                    ls_pos, _, _, suffix, d_w = pre[t * len(chains) + c][:5]
                    surv, w = _sb_weights(ls_pos, suffix, c_not, causal[a], diag)
                    dlw = d_w * w
                    mids.append((surv, w, dlw, _split_dot(dlw, tri_incl)))
                new = []
                dk_new = [None] * SB_BWD_HEADS
                dv_new = [None] * SB_BWD_HEADS
                for c, (p, a) in enumerate(chains):
                    c_dlw, dq = st[3 * c + 1 : 3 * c + 3]
                    ls_pos, log_not, log_not_m, _, _, qb, kb, dob_c = pre[t * len(chains) + c]
                    surv, w, dlw, suffix = mids[c]
                    d_not = totals[c] - c_dlw - suffix
                    dz = (dlw * jnp.exp(log_not) - d_not * jnp.exp(ls_pos)) * scale
                    if diag:
                        dz = jnp.where(causal[a], dz, 0.0)
                    if valid is not None:
                        dz = jnp.where(valid, dz, 0.0)
                    dzb = dz.astype(BF16)
                    dk_c, dv_c = _dot_tn(dzb, qb), _dot_tn(w.astype(BF16), dob_c)
                    dk_new[p] = dk_c if dk_new[p] is None else dk_new[p] + dk_c
                    dv_new[p] = dv_c if dv_new[p] is None else dv_new[p] + dv_c
                    new += [surv[:, 0:1] + log_not_m[:, 0:1], c_dlw + suffix[:, 0:1], dq + _dot(dzb, kb)]
                for p in range(SB_BWD_HEADS):
                    dk_acc[pl.ds(start, blk), _head_cols(p)] += dk_new[p]
                    dv_acc[pl.ds(start, blk), _head_cols(p)] += dv_new[p]
                st = tuple(new)
            return st

        zcol = jnp.zeros((rows, 1), F32)
        st = tiles([(i, True, None), (jnp.maximum(i - 1, 0), False, i >= 1)],
                   (zcol, zcol, jnp.zeros((rows, HEAD_DIM), F32)) * len(chains))

        def more(st):
            return (st[0] < i) & _sb_alive(st, len(chains))

        def step(st):
            return (st[0] + 1,) + tiles([(i - 1 - st[0], False, None)], st[1:])

        st = lax.while_loop(more, step, (1,) + st)[1:]
        for c, (p, a) in enumerate(chains):
            dq_ref[a * rows : (a + 1) * rows, _head_cols(p)] = st[3 * c + 2].astype(BF16)

        @pl.when(i == nq - 1)
        def _():
            dk_ref[...] = dk_acc[...].astype(BF16)
            dv_ref[...] = dv_acc[...].astype(BF16)

    blk_spec, head_spec = _sb_specs(s_len, blk, SB_BWD_HEADS)
    width = SB_BWD_HEADS * HEAD_DIM
    return pl.pallas_call(
        body,
        name="sb_bwd",
        grid=(HEADS // SB_BWD_HEADS, nq),
        in_specs=[blk_spec(OFF_SB_Q), head_spec(OFF_SB_K, 1), head_spec(OFF_SB_V, 1), blk_spec(0), blk_spec(0)],
        out_specs=[blk_spec(0), head_spec(0), head_spec(0)],
        out_shape=[jax.ShapeDtypeStruct((s_len, D_MODEL), BF16)] * 3,
        scratch_shapes=[pltpu.VMEM((s_len, width), F32), pltpu.VMEM((s_len, width), F32)],
        compiler_params=_cparams(("arbitrary", "arbitrary")),
    )(qkv, qkv, qkv, o_fine, d_o)


def _hg_lower_bound(lbl_ref):
    l0 = lbl_ref[0:1, :]
    l1 = lbl_ref[1:2, :]
    mx = jnp.maximum(l0, l1)
    e0 = jnp.exp(l0 - mx)
    e1 = jnp.exp(l1 - mx)
    return e0 / (e0 + e1)


def _hg_gates(hq, hf, lb):
    sig_f = _sigmoid(hf)
    f = lb + (1.0 - lb) * sig_f
    g = jnp.log(f)
    kk = 1.0 - f
    sig_q = _sigmoid(hq)
    qq = hq * sig_q
    return qq, kk, g, f, sig_f, sig_q


def _period_bcast(x, r, rows, period):
    w = x.shape[-1]
    x3 = x.reshape(rows // period, period, w)
    return jnp.broadcast_to(x3[:, r : r + 1, :], x3.shape).reshape(rows, w)


def _blockdiag(rows, kind):
    row = lax.broadcasted_iota(jnp.int32, (rows, rows), 0)
    col = lax.broadcasted_iota(jnp.int32, (rows, rows), 1)
    if kind in ("next", "prev"):
        first, second = (row, col) if kind == "next" else (col, row)
        keep = ((row // HG_PAIR) == (col // HG_PAIR)) & (first % HG_PAIR < HG_CHUNK) & (second % HG_PAIR >= HG_CHUNK)
    else:
        keep = (row // HG_CHUNK) == (col // HG_CHUNK)
        if kind == "lower":
            keep = keep & (row >= col)
        elif kind == "upper":
            keep = keep & (row <= col)
    return jnp.where(keep, 1.0, 0.0).astype(BF16)


def _hg_operands(hq, hf, lb, rows):
    qq, kk, g, f, sig_f, sig_q = _hg_gates(hq, hf, lb)
    cum = _split_dot_left(_blockdiag(rows, "lower"), g)
    mid = _period_bcast(cum, HG_MID, rows, HG_CHUNK)
    last = _period_bcast(cum, HG_CHUNK - 1, rows, HG_CHUNK)
    last0 = _period_bcast(cum, HG_CHUNK - 1, rows, HG_PAIR)
    last1 = _period_bcast(cum, HG_PAIR - 1, rows, HG_PAIR)
    second = (lax.broadcasted_iota(jnp.int32, cum.shape, 0) % HG_PAIR) >= HG_CHUNK
    e = dict(qm=jnp.exp(cum - mid), km=jnp.exp(mid - cum), qd=jnp.exp(cum), kl=jnp.exp(last - cum),
             q_in=jnp.where(second, jnp.exp(last0), 1.0), k_out=jnp.where(second, 1.0, jnp.exp(last1)),
             pair=jnp.exp(last0 + last1))
    v = dict(qm=qq * e["qm"], km=kk * e["km"], qd=qq * e["qd"], kl=kk * e["kl"])
    v["qp"] = v["qd"] * e["q_in"]
    v["kp"] = v["kl"] * e["k_out"]
    return v, e, second, (f, sig_f, sig_q)


def _hg_store_operands(v, second, hi, refs):
    zero = jnp.zeros_like(v["qm"])
    q_cat, k_cat, qp_b, kp_b, v_b = refs
    q_cat[:, 0:D_MODEL] = jnp.where(second, zero, v["qm"]).astype(BF16)
    q_cat[:, D_MODEL : 2 * D_MODEL] = jnp.where(second, v["qm"], zero).astype(BF16)
    q_cat[:, 2 * D_MODEL :] = jnp.where(second, v["qd"], zero).astype(BF16)
    k_cat[:, 0:D_MODEL] = jnp.where(second, zero, v["km"]).astype(BF16)
    k_cat[:, D_MODEL : 2 * D_MODEL] = jnp.where(second, v["km"], zero).astype(BF16)
    k_cat[:, 2 * D_MODEL :] = jnp.where(second, zero, v["kl"]).astype(BF16)
    qp_b[...] = v["qp"].astype(BF16)
    kp_b[...] = v["kp"].astype(BF16)
    v_b[...] = hi.astype(BF16)


def _hg_pair_operands(cat, r0, c0):
    return jnp.concatenate([cat[r0 : r0 + HG_PAIR, g * D_MODEL + c0 : g * D_MODEL + c0 + HEAD_DIM] for g in range(3)], axis=1)


def _hg_fwd(proj, lbl):
    s_len = proj.shape[0]
    rows = min(HG_STEP, s_len)
    n_pairs = rows // HG_PAIR

    def body(hq_ref, hf_ref, hi_ref, lbl_ref, o_ref, st_ref, state, q_cat, k_cat, qp_b, kp_b, v_b):
        @pl.when(pl.program_id(0) == 0)
        def _():
            state[...] = jnp.zeros_like(state)

        v, e, second, _ = _hg_operands(hq_ref[...], hf_ref[...], _hg_lower_bound(lbl_ref), rows)
        _hg_store_operands(v, second, hi_ref[...], (q_cat, k_cat, qp_b, kp_b, v_b))
        e_pair = e["pair"]
        row = lax.broadcasted_iota(jnp.int32, (HG_PAIR, HG_PAIR), 0)
        col = lax.broadcasted_iota(jnp.int32, (HG_PAIR, HG_PAIR), 1)
        causal = row >= col

        for u in range(n_pairs):
            r0 = u * HG_PAIR
            sls = [(slice(r0, r0 + HG_PAIR), slice(h * HEAD_DIM, (h + 1) * HEAD_DIM)) for h in range(HEADS)]
            a_s = [jnp.where(causal, _dot_nt(_hg_pair_operands(q_cat, r0, h * HEAD_DIM),
                                             _hg_pair_operands(k_cat, r0, h * HEAD_DIM)), 0.0).astype(BF16)
                   for h in range(HEADS)]
            st_s = [state[h] for h in range(HEADS)]
            for h, sl in enumerate(sls):
                st_ref[u, h] = st_s[h]
                state[h] = st_s[h] * e_pair[r0 : r0 + 1, sl[1]] + _dot_tn(v_b[sl], kp_b[sl])
            for h, sl in enumerate(sls):
                o_ref[sl] = _dot(a_s[h], v_b[sl]) + _dot_nt(qp_b[sl], st_s[h].astype(BF16))

    def col_spec(off):
        return pl.BlockSpec((rows, D_MODEL), lambda s: (s, off // D_MODEL))

    bf_tile = pltpu.VMEM((rows, D_MODEL), BF16)
    bf_cat = pltpu.VMEM((rows, 3 * D_MODEL), BF16)
    scratch = [pltpu.VMEM((HEADS, HEAD_DIM, HEAD_DIM), F32), bf_cat, bf_cat, bf_tile, bf_tile, bf_tile]
    return pl.pallas_call(
        body,
        name="hg_fwd",
        grid=(s_len // rows,),
        in_specs=[col_spec(OFF_HG_Q), col_spec(OFF_HG_F), col_spec(OFF_HG_I), pl.BlockSpec((2, D_MODEL), lambda s: (0, 0))],
        out_specs=[
            pl.BlockSpec((rows, D_MODEL), lambda s: (s, 0)),
            pl.BlockSpec((n_pairs, HEADS, HEAD_DIM, HEAD_DIM), lambda s: (s, 0, 0, 0)),
        ],
        out_shape=[
            jax.ShapeDtypeStruct((s_len, D_MODEL), F32),
            jax.ShapeDtypeStruct((s_len // HG_PAIR, HEADS, HEAD_DIM, HEAD_DIM), F32),
        ],
        scratch_shapes=scratch,
        compiler_params=_cparams(("arbitrary",)),
    )(proj, proj, proj, lbl)


def _hg_bwd(proj, lbl, states, d_o):
    s_len = proj.shape[0]
    rows = min(HG_STEP, s_len)
    n_pairs = rows // HG_PAIR
    n_steps = s_len // rows

    def body(hq_ref, hf_ref, hi_ref, lbl_ref, st_ref, do_ref, dp_ref, dlb_ref,
             dstate, q_cat, k_cat, qp_b, kp_b, v_b, do_b, d_qcat, d_kcat, d_qp, d_kp, d_v, d_pair):
        @pl.when(pl.program_id(0) == 0)
        def _():
            dstate[...] = jnp.zeros_like(dstate)
            dlb_ref[...] = jnp.zeros_like(dlb_ref)

        lb = _hg_lower_bound(lbl_ref)
        hq = hq_ref[...]
        v, e, second, (f, sig_f, sig_q) = _hg_operands(hq, hf_ref[...], lb, rows)
        _hg_store_operands(v, second, hi_ref[...], (q_cat, k_cat, qp_b, kp_b, v_b))
        do_b[...] = do_ref[...].astype(BF16)
        e_pair = e["pair"]
        row = lax.broadcasted_iota(jnp.int32, (HG_PAIR, HG_PAIR), 0)
        col = lax.broadcasted_iota(jnp.int32, (HG_PAIR, HG_PAIR), 1)
        causal = row >= col

        for u in reversed(range(n_pairs)):
            r0 = u * HG_PAIR
            sls = [(slice(r0, r0 + HG_PAIR), slice(h * HEAD_DIM, (h + 1) * HEAD_DIM)) for h in range(HEADS)]
            ops = [(_hg_pair_operands(q_cat, r0, h * HEAD_DIM), _hg_pair_operands(k_cat, r0, h * HEAD_DIM))
                   for h in range(HEADS)]
            a_s = [jnp.where(causal, _dot_nt(lhs, rhs), 0.0).astype(BF16) for lhs, rhs in ops]
            da_s = [jnp.where(causal, _dot_nt(do_b[sl], v_b[sl]), 0.0).astype(BF16) for sl in sls]
            st0_s = [st_ref[u, h] for h in range(HEADS)]
            ds1_s = [dstate[h] for h in range(HEADS)]
            ds1b_s = [ds1.astype(BF16) for ds1 in ds1_s]
            for h, sl in enumerate(sls):
                decay = e_pair[r0 : r0 + 1, sl[1]]
                d_pair[u : u + 1, sl[1]] = decay * jnp.sum(ds1_s[h] * st0_s[h], axis=0, keepdims=True)
                dstate[h] = ds1_s[h] * decay + _dot_tn(do_b[sl], qp_b[sl])
            for h, sl in enumerate(sls):
                d_qp[sl] = _dot(do_b[sl], st0_s[h].astype(BF16))
                d_kp[sl] = _dot(v_b[sl], ds1b_s[h])
            for h, sl in enumerate(sls):
                d_v[sl] = _dot_tn(a_s[h], do_b[sl]) + _dot_nt(kp_b[sl], ds1b_s[h])
            for h, sl in enumerate(sls):
                d_lhs = _dot(da_s[h], ops[h][1])
                d_rhs = _dot_tn(da_s[h], ops[h][0])
                for g in range(3):
                    gsl = (sl[0], slice(g * D_MODEL + h * HEAD_DIM, g * D_MODEL + (h + 1) * HEAD_DIM))
                    d_qcat[gsl] = d_lhs[:, g * HEAD_DIM : (g + 1) * HEAD_DIM]
                    d_kcat[gsl] = d_rhs[:, g * HEAD_DIM : (g + 1) * HEAD_DIM]

        zero = jnp.zeros_like(hq)
        dqm = jnp.where(second, d_qcat[:, D_MODEL : 2 * D_MODEL], d_qcat[:, 0:D_MODEL])
        dkm = jnp.where(second, d_kcat[:, D_MODEL : 2 * D_MODEL], d_kcat[:, 0:D_MODEL])
        dqp, dkp = d_qp[...], d_kp[...]
        dqd = dqp * e["q_in"] + jnp.where(second, d_qcat[:, 2 * D_MODEL :], zero)
        dkl = dkp * e["k_out"] + jnp.where(second, zero, d_kcat[:, 2 * D_MODEL :])
        dq = dqm * e["qm"] + dqd * e["qd"]
        dk = dkm * e["km"] + dkl * e["kl"]
        t_kl = dkl * v["kl"]
        dcum = dqm * v["qm"] - dkm * v["km"] + dqd * v["qd"] - t_kl
        dp = d_pair[...]
        dp_b = jnp.broadcast_to(dp[:, None, :], (n_pairs, HG_PAIR, D_MODEL)).reshape(rows, D_MODEL)
        dg = (_split_dot_left(_blockdiag(rows, "upper"), dcum) + _split_dot_left(_blockdiag(rows, "all"), t_kl)
              + _split_dot_left(_blockdiag(rows, "next"), dqp * v["qp"])
              + _split_dot_left(_blockdiag(rows, "prev"), dkp * v["kp"]) + dp_b)
        df = dg / f - dk
        one_m = 1.0 - sig_f
        dp_ref[:, 0:D_MODEL] = (dq * (sig_q * (1.0 + hq * (1.0 - sig_q)))).astype(BF16)
        dp_ref[:, D_MODEL : 2 * D_MODEL] = (df * (1.0 - lb) * sig_f * one_m).astype(BF16)
        dp_ref[:, 2 * D_MODEL : 3 * D_MODEL] = d_v[...].astype(BF16)
        dlb_ref[...] += jnp.sum(df * one_m, axis=0, keepdims=True)

    def col_spec(off):
        return pl.BlockSpec((rows, D_MODEL), lambda s: (n_steps - 1 - s, off // D_MODEL))

    f32_tile = pltpu.VMEM((rows, D_MODEL), F32)
    f32_cat = pltpu.VMEM((rows, 3 * D_MODEL), F32)
    bf_tile = pltpu.VMEM((rows, D_MODEL), BF16)
    bf_cat = pltpu.VMEM((rows, 3 * D_MODEL), BF16)
    scratch = [pltpu.VMEM((HEADS, HEAD_DIM, HEAD_DIM), F32), bf_cat, bf_cat, bf_tile, bf_tile, bf_tile, bf_tile,
               f32_cat, f32_cat, f32_tile, f32_tile, f32_tile, pltpu.VMEM((n_pairs, D_MODEL), F32)]
    return pl.pallas_call(
        body,
        name="hg_bwd",
        grid=(n_steps,),
        in_specs=[
            col_spec(OFF_HG_Q), col_spec(OFF_HG_F), col_spec(OFF_HG_I),
            pl.BlockSpec((2, D_MODEL), lambda s: (0, 0)),
            pl.BlockSpec((n_pairs, HEADS, HEAD_DIM, HEAD_DIM), lambda s: (n_steps - 1 - s, 0, 0, 0)),
            pl.BlockSpec((rows, D_MODEL), lambda s: (n_steps - 1 - s, 0)),
        ],
        out_specs=[
            pl.BlockSpec((rows, 3 * D_MODEL), lambda s: (n_steps - 1 - s, 0)),
            pl.BlockSpec((1, D_MODEL), lambda s: (0, 0)),
        ],
        out_shape=[
            jax.ShapeDtypeStruct((s_len, 3 * D_MODEL), BF16),
            jax.ShapeDtypeStruct((1, D_MODEL), F32),
        ],
        scratch_shapes=scratch,
        compiler_params=_cparams(("arbitrary",)),
    )(proj, proj, proj, lbl, states, d_o)


def _mid(proj, sb_o, hg_o, x, target, b_gate, hg_gain, final_g, w_sb, w_hg, w_out):
    s_len = proj.shape[0]
    ts = min(256, s_len)
    inv_d = 1.0 / D_MODEL

    def body(zsb_ref, hz_ref, gl_ref, sbo_ref, hgo_ref, x_ref, tgt_ref, bg_ref, hgn_ref, fg_ref,
             wsb_ref, whg_ref, wout_ref,
             dout_ref, dsbo_ref, dhgo_ref, dmid_ref,
             asb_ref, dusb_ref, ahg_ref, duhg_ref, y_ref, doutb_ref,
             loss_ref, dfg_ref, dbg_ref, dhgn_ref):
        @pl.when(pl.program_id(0) == 0)
        def _():
            loss_ref[...] = jnp.zeros_like(loss_ref)
            dfg_ref[...] = jnp.zeros_like(dfg_ref)
            dbg_ref[...] = jnp.zeros_like(dbg_ref)
            dhgn_ref[...] = jnp.zeros_like(dhgn_ref)

        z_sb = zsb_ref[...]
        sb_o = sbo_ref[...]
        sig_zsb = _sigmoid(z_sb)
        silu_zsb = z_sb * sig_zsb
        a_sb_f = sb_o * silu_zsb
        a_sb = a_sb_f.astype(BF16)
        u_sb = _dot(a_sb, wsb_ref[...])

        hg_o = hgo_ref[...]
        gain = hgn_ref[...]
        r_parts, yn_parts = [], []
        for h in range(HEADS):
            oh = hg_o[:, h * HEAD_DIM : (h + 1) * HEAD_DIM]
            r = lax.rsqrt(jnp.mean(oh * oh, axis=-1, keepdims=True) + RMS_EPS)
            r_parts.append(jnp.broadcast_to(r, oh.shape))
            yn_parts.append(oh * r)
        r_hg = jnp.concatenate(r_parts, axis=-1)
        yn_hg = jnp.concatenate(yn_parts, axis=-1)
        hn = yn_hg * gain
        hz = hz_ref[...]
        sig_hz = _sigmoid(hz)
        silu_hz = hz * sig_hz
        a_hg_f = hn * silu_hz
        a_hg = a_hg_f.astype(BF16)
        u_hg = _dot(a_hg, whg_ref[...])

        gates = _sigmoid(gl_ref[...] + bg_ref[...])
        g_sb = gates[:, 0:D_MODEL]
        g_hg = gates[:, D_MODEL:]
        y_f = g_sb * u_sb + g_hg * u_hg
        y = y_f.astype(BF16)
        out = x_ref[...] + _dot(y, wout_ref[...])
        r2 = lax.rsqrt(jnp.mean(out * out, axis=-1, keepdims=True) + RMS_EPS)
        yn = out * r2
        fg = fg_ref[...]
        diff = yn * fg - tgt_ref[...]
        loss_ref[...] += 0.5 * inv_d * jnp.sum(diff * diff)

        dyf = diff * inv_d
        dfg_ref[...] += jnp.sum(dyf * yn, axis=0, keepdims=True)
        dyn = dyf * fg
        dout = r2 * (dyn - yn * jnp.mean(dyn * yn, axis=-1, keepdims=True))
        dout_ref[...] = dout
        doutb = dout.astype(BF16)
        doutb_ref[...] = doutb
        dy = _dot_nt(doutb, wout_ref[...])
        du_sb = (dy * g_sb).astype(BF16)
        du_hg = (dy * g_hg).astype(BF16)
        dgl_sb = dy * u_sb * g_sb * (1.0 - g_sb)
        dgl_hg = dy * u_hg * g_hg * (1.0 - g_hg)
        dmid_ref[:, 2 * D_MODEL : 3 * D_MODEL] = dgl_sb.astype(BF16)
        dmid_ref[:, 3 * D_MODEL :] = dgl_hg.astype(BF16)
        dbg_ref[:, 0:D_MODEL] += jnp.sum(dgl_sb, axis=0, keepdims=True)
        dbg_ref[:, D_MODEL:] += jnp.sum(dgl_hg, axis=0, keepdims=True)

        da_sb = _dot_nt(du_sb, wsb_ref[...])
        dsbo_ref[...] = (da_sb * silu_zsb).astype(BF16)
        dmid_ref[:, 0:D_MODEL] = (da_sb * sb_o * (sig_zsb * (1.0 + z_sb * (1.0 - sig_zsb)))).astype(BF16)

        da_hg = _dot_nt(du_hg, whg_ref[...])
        dhn = da_hg * silu_hz
        dmid_ref[:, D_MODEL : 2 * D_MODEL] = (da_hg * hn * (sig_hz * (1.0 + hz * (1.0 - sig_hz)))).astype(BF16)
        dhgn_ref[...] += jnp.sum(dhn * yn_hg, axis=0, keepdims=True)
        dyn_hg = dhn * gain
        prod = dyn_hg * yn_hg
        m_parts = []
        for h in range(HEADS):
            ph = prod[:, h * HEAD_DIM : (h + 1) * HEAD_DIM]
            m_parts.append(jnp.broadcast_to(jnp.mean(ph, axis=-1, keepdims=True), ph.shape))
        dhgo_ref[...] = (r_hg * (dyn_hg - yn_hg * jnp.concatenate(m_parts, axis=-1))).astype(BF16)

        asb_ref[...] = a_sb_f.T.astype(BF16)
        dusb_ref[...] = du_sb
        ahg_ref[...] = a_hg_f.T.astype(BF16)
        duhg_ref[...] = du_hg
        y_ref[...] = y_f.T.astype(BF16)

    def tile(width, off=0):
        return pl.BlockSpec((ts, width), lambda s: (s, off // width))

    def across():
        return pl.BlockSpec((D_MODEL, ts), lambda s: (0, s))

    def whole(shape):
        return pl.BlockSpec(shape, lambda s: (0,) * len(shape))

    def weight():
        return pl.BlockSpec((D_MODEL, D_MODEL), lambda s: (0, 0), pipeline_mode=pl.Buffered(1))

    f32_act = jax.ShapeDtypeStruct((s_len, D_MODEL), F32)
    bf_act = jax.ShapeDtypeStruct((s_len, D_MODEL), BF16)
    bf_act_t = jax.ShapeDtypeStruct((D_MODEL, s_len), BF16)
    return pl.pallas_call(
        body,
        name="mid",
        grid=(s_len // ts,),
        in_specs=[
            tile(D_MODEL, OFF_SB_Z), tile(D_MODEL, OFF_HG_Z), tile(2 * D_MODEL, OFF_GATE),
            tile(D_MODEL), tile(D_MODEL), tile(D_MODEL), tile(D_MODEL),
            whole((1, 2 * D_MODEL)), whole((1, D_MODEL)), whole((1, D_MODEL)),
            weight(), weight(), weight(),
        ],
        out_specs=[
            tile(D_MODEL), tile(D_MODEL), tile(D_MODEL), tile(4 * D_MODEL),
            across(), tile(D_MODEL), across(), tile(D_MODEL), across(), tile(D_MODEL),
            whole((1, 1)), whole((1, D_MODEL)), whole((1, 2 * D_MODEL)), whole((1, D_MODEL)),
        ],
        out_shape=[
            f32_act, bf_act, bf_act, jax.ShapeDtypeStruct((s_len, 4 * D_MODEL), BF16),
            bf_act_t, bf_act, bf_act_t, bf_act, bf_act_t, bf_act,
            jax.ShapeDtypeStruct((1, 1), F32), jax.ShapeDtypeStruct((1, D_MODEL), F32),
            jax.ShapeDtypeStruct((1, 2 * D_MODEL), F32), jax.ShapeDtypeStruct((1, D_MODEL), F32),
        ],
        compiler_params=_cparams(("arbitrary",)),
    )(proj, proj, proj, sb_o, hg_o, x, target, b_gate, hg_gain, final_g, w_sb, w_hg, w_out)


def _grad_square(a_t, b, name):
    s_len = b.shape[0]
    tk = min(1024, s_len)

    def body(a_ref, b_ref, o_ref):
        @pl.when(pl.program_id(0) == 0)
        def _():
            o_ref[...] = jnp.zeros_like(o_ref)

        o_ref[...] += _dot(a_ref[...], b_ref[...])

    return pl.pallas_call(
        body,
        name=name,
        grid=(s_len // tk,),
        in_specs=[pl.BlockSpec((D_MODEL, tk), lambda k: (0, k)), pl.BlockSpec((tk, D_MODEL), lambda k: (k, 0))],
        out_specs=pl.BlockSpec((D_MODEL, D_MODEL), lambda k: (0, 0)),
        out_shape=jax.ShapeDtypeStruct((D_MODEL, D_MODEL), F32),
        compiler_params=_cparams(("arbitrary",)),
    )(a_t, b)


SEG_WIDTHS = (1024, 1024, 1024, 4096, 3072)
SEG_TILE = 1024
SEG_BOUNDS = (0, 1, 2, 3, 7, 10)


def _w_in_tile(k):
    return jnp.where(k < 4, k, jnp.where(k < 7, k + 3, k - 3))


def _grad_w_in(h_t, segs):
    m, s_len = h_t.shape
    tk = min(1024, s_len)
    tn = SEG_TILE
    nk = s_len // tk
    bounds = SEG_BOUNDS

    def body(a_ref, *refs):
        seg_refs, o_ref = refs[:-1], refs[-1]
        j = pl.program_id(0)

        @pl.when(pl.program_id(1) == 0)
        def _():
            o_ref[...] = jnp.zeros_like(o_ref)

        for i, ref in enumerate(seg_refs):
            @pl.when((j >= bounds[i]) & (j < bounds[i + 1]))
            def _(ref=ref):
                o_ref[...] += _dot(a_ref[...], ref[...])

    def seg_spec(lo, hi):
        def index(j, k):
            return (jnp.where(j < lo, 0, jnp.where(j >= hi, nk - 1, k)), jnp.clip(j - lo, 0, hi - lo - 1))
        return pl.BlockSpec((tk, tn), index)

    return pl.pallas_call(
        body,
        name="grad_w_in",
        grid=(IN_WIDTH // tn, nk),
        in_specs=[pl.BlockSpec((m, tk), lambda j, k: (0, k))]
        + [seg_spec(bounds[i], bounds[i + 1]) for i in range(len(SEG_WIDTHS))],
        out_specs=pl.BlockSpec((m, tn), lambda j, k: (0, _w_in_tile(j))),
        out_shape=jax.ShapeDtypeStruct((m, IN_WIDTH), F32),
        compiler_params=_cparams(("arbitrary", "arbitrary")),
    )(h_t, *segs)


EXCHANGE_IN_PIECES = 8
EXCHANGE_PIECES = EXCHANGE_IN_PIECES + 3


def _exchange_copies(sin_ref, ssq_ref, got_in, got_sq, send_sems, recv_sems):
    _, _, c, chips = _position()
    rows = HALF_IN // EXCHANGE_IN_PIECES
    copies = []
    for k, (px, py) in enumerate(chips):
        chip = 2 * px + py
        for p in range(EXCHANGE_PIECES):
            if p < EXCHANGE_IN_PIECES:
                src, dst = sin_ref.at[chip, pl.ds(p * rows, rows), :], got_in.at[k, pl.ds(p * rows, rows), :]
            else:
                src, dst = ssq_ref.at[p - EXCHANGE_IN_PIECES, chip], got_sq.at[k, p - EXCHANGE_IN_PIECES]
            copies.append(_remote(src, dst, send_sems.at[k, p], recv_sems.at[k, p], (px, py, c)))
    return copies


def _dx(segs, w_all, x, norm_g, dout, s_in, s_sq):
    s_len = x.shape[0]
    ts = min(1024, s_len)
    tk = SEG_TILE
    nk = IN_WIDTH // tk
    ns = s_len // ts
    bounds = SEG_BOUNDS
    n_seg = len(SEG_WIDTHS)

    def body(*refs):
        seg_refs = refs[:n_seg]
        w_ref, x_ref, g_ref, dout_ref, sin_ref, ssq_ref, gx_ref, dg_ref, got_in, got_sq, acc, send_sems, recv_sems = refs[n_seg:]
        s, k = pl.program_id(0), pl.program_id(1)

        @pl.when((s == 0) & (k == 0))
        def _():
            dg_ref[...] = jnp.zeros_like(dg_ref)
            for cp in _exchange_copies(sin_ref, ssq_ref, got_in, got_sq, send_sems, recv_sems):
                cp.start()

        @pl.when(k == 0)
        def _():
            acc[...] = jnp.zeros_like(acc)

        for i, ref in enumerate(seg_refs):
            @pl.when((k >= bounds[i]) & (k < bounds[i + 1]))
            def _(ref=ref):
                acc[...] += _dot_nt(ref[...], w_ref[...])

        @pl.when(k == nk - 1)
        def _():
            dh = acc[...]
            xv = x_ref[...]
            r = lax.rsqrt(jnp.mean(xv * xv, axis=-1, keepdims=True) + RMS_EPS)
            xn = xv * r
            dg_ref[...] += jnp.sum(dh * xn, axis=0, keepdims=True)
            dxn = dh * g_ref[...]
            gx_ref[...] = r * (dxn - xn * jnp.mean(dxn * xn, axis=-1, keepdims=True)) + dout_ref[...]

        @pl.when((s == ns - 1) & (k == nk - 1))
        def _():
            for cp in _exchange_copies(sin_ref, ssq_ref, got_in, got_sq, send_sems, recv_sems):
                cp.wait()

    def seg_spec(lo, hi):
        return pl.BlockSpec((ts, tk), lambda s, k: (s, jnp.clip(k - lo, 0, hi - lo - 1)))

    row_tile = pl.BlockSpec((ts, D_MODEL), lambda s, k: (s, 0))
    vec = pl.BlockSpec((1, D_MODEL), lambda s, k: (0, 0))
    return pl.pallas_call(
        body,
        name="dx",
        grid=(ns, nk),
        in_specs=[seg_spec(bounds[i], bounds[i + 1]) for i in range(n_seg)] + [
            pl.BlockSpec((D_MODEL, tk), lambda s, k: (0, _w_in_tile(k))),
            row_tile, vec, row_tile, ANY, ANY,
        ],
        out_specs=[row_tile, vec, ANY, ANY],
        out_shape=[jax.ShapeDtypeStruct((s_len, D_MODEL), F32), jax.ShapeDtypeStruct((1, D_MODEL), F32),
                   jax.ShapeDtypeStruct((3, HALF_IN, W_IN_SHARD), WIRE),
                   jax.ShapeDtypeStruct((3, 3, HALF_SQ, D_MODEL), WIRE)],
        scratch_shapes=[pltpu.VMEM((ts, D_MODEL), F32),
                        pltpu.SemaphoreType.DMA((3, EXCHANGE_PIECES)), pltpu.SemaphoreType.DMA((3, EXCHANGE_PIECES))],
        compiler_params=_cparams(("arbitrary", "arbitrary"), vmem=VMEM_LIMIT_DX),
    )(*segs, w_all, x, norm_g, dout, s_in, s_sq)


def _local_grads(x, target, proj, h_t, qkv, b_gate, lbl, hg_gain, final_g, w_sb, w_hg, w_out):
    sb_o, sb_o_fine = _sb_fwd(qkv)
    hg_o, states = _hg_fwd(proj, lbl)
    (dout, d_sbo, d_hgo, d_mid, a_sb, du_sb, a_hg, du_hg, y, doutb,
     loss, d_fg, d_bg, d_hgn) = _mid(proj, sb_o, hg_o, x, target, b_gate, hg_gain, final_g, w_sb, w_hg, w_out)
    g_w_sb = _grad_square(a_sb, du_sb, "grad_w_sb")
    g_w_hg = _grad_square(a_hg, du_hg, "grad_w_hg")
    g_w_out = _grad_square(y, doutb, "grad_w_out")
    d_q, d_k, d_v = _sb_bwd(qkv, sb_o_fine, d_sbo)
    d_hg, d_lb = _hg_bwd(proj, lbl, states, d_hgo)
    segs = (d_q, d_k, d_v, d_mid, d_hg)
    g_w_in = _grad_w_in(h_t, segs)
    return g_w_in, g_w_sb, g_w_hg, g_w_out, segs, dout, loss, d_bg, d_lb, d_hgn, d_fg


ANY = pl.BlockSpec(memory_space=pl.ANY)
WIRE = BF16
HALF_IN = D_MODEL // 2
HALF_SQ = ROW_SHARD // 2


def _position():
    x, y, c = lax.axis_index("x"), lax.axis_index("y"), lax.axis_index("c")
    chips = [(1 - x, y), (x, 1 - y), (1 - x, 1 - y)]
    return x, y, c, chips


def _remote(src, dst, send_sem, recv_sem, to):
    return pltpu.make_async_remote_copy(src_ref=src, dst_ref=dst, send_sem=send_sem, recv_sem=recv_sem,
                                        device_id=to, device_id_type=MESH)


PROJ_TILE = 1280
F32_FROM_TILE = 2
BF16_TO_TILE = 2


def _gather_inproj(idx, h, w_in_b, w_sq_b):
    s_len = h.shape[0]
    ts = min(1024, s_len)
    ns = s_len // ts
    per = W_IN_SHARD // PROJ_TILE
    n_in = 4
    n_piece = n_in + 3
    rows = HALF_IN // n_in

    def chip_at(r, me):
        return me ^ jnp.where(r == 1, 2, jnp.where(r == 2, 1, jnp.where(r == 3, 3, 0)))

    def body(idx_ref, h_ref, win_ref, wsqb_ref, proj_ref, qkv_ref, wall_ref, wsq_ref, wbuf, send_sems, recv_sems, w_sem):
        r, t, s = pl.program_id(0), pl.program_id(1), pl.program_id(2)
        x, y, c, chips = _position()
        me = 2 * x + y
        sibling = (x, y, 1 - c)
        first = (t == 0) & (s == 0)

        def src_piece(p):
            if p < n_in:
                return win_ref.at[pl.ds(c * HALF_IN + p * rows, rows), :]
            return wsqb_ref.at[p - n_in, pl.ds(c * HALF_SQ, HALF_SQ), :]

        def piece(p, chip, core):
            if p < n_in:
                cols = pl.ds(pl.multiple_of(chip * W_IN_SHARD, W_IN_SHARD), W_IN_SHARD)
                return wall_ref.at[pl.ds(core * HALF_IN + p * rows, rows), cols]
            return wsq_ref.at[p - n_in, chip, pl.ds(core * HALF_SQ, HALF_SQ), :]

        def send(k, p):
            px, py = chips[k]
            return _remote(src_piece(p), piece(p, me, c), send_sems.at[k, p], recv_sems.at[k, p], (px, py, c))

        def forward(k, p, core):
            px, py = chips[k]
            got = piece(p, 2 * px + py, core)
            return _remote(got, got, send_sems.at[3 + k, p], recv_sems.at[3 + k, p], sibling)

        @pl.when((r == 0) & first)
        def _():
            for k in range(3):
                for p in range(n_piece):
                    send(k, p).start()

        for k in range(3):
            @pl.when((r == k + 1) & first)
            def _(k=k):
                px, py = chips[k]
                for p in range(n_piece):
                    got = piece(p, 2 * px + py, c)
                    _remote(got, got, send_sems.at[k, p], recv_sems.at[k, p], (px, py, c)).wait_recv()
                    forward(k, p, c).start()
                for p in range(n_piece):
                    forward(k, p, 1 - c).wait_recv()

        @pl.when(s == 0)
        def _():
            col = pl.multiple_of(t * PROJ_TILE, PROJ_TILE)

            @pl.when(r == 0)
            def _():
                cp = pltpu.make_async_copy(win_ref.at[:, pl.ds(col, PROJ_TILE)], wbuf, w_sem)
                cp.start()
                cp.wait()

            @pl.when(r > 0)
            def _():
                off = pl.multiple_of(chip_at(r, me) * W_IN_SHARD + col, PROJ_TILE)
                cp = pltpu.make_async_copy(wall_ref.at[:, pl.ds(off, PROJ_TILE)], wbuf, w_sem)
                cp.start()
                cp.wait()

        tile_now = per * chip_at(r, me) + t
        want_f32, want_bf16 = tile_now >= F32_FROM_TILE, tile_now <= BF16_TO_TILE

        @pl.when(want_f32 & jnp.logical_not(want_bf16))
        def _():
            proj_ref[...] = _dot(h_ref[...], wbuf[...])

        @pl.when(want_bf16 & jnp.logical_not(want_f32))
        def _():
            qkv_ref[...] = _dot(h_ref[...], wbuf[...]).astype(BF16)

        @pl.when(want_f32 & want_bf16)
        def _():
            p = _dot(h_ref[...], wbuf[...])
            proj_ref[...] = p
            qkv_ref[...] = p.astype(BF16)

        @pl.when((r == 3) & (t == per - 1) & (s == ns - 1))
        def _():
            for k in range(3):
                for p in range(n_piece):
                    send(k, p).wait_send()
                    forward(k, p, c).wait_send()

    def out_index(wanted):
        order = [0, 2, 1, 3]
        table = []
        for chip in range(N_CHIPS):
            tiles = [per * (chip ^ order[q // per]) + q % per for q in range(N_CHIPS * per)]
            row = []
            for q, tile in enumerate(tiles):
                if wanted(tile):
                    row.append((tile, None))
                    continue
                before = [u for u in tiles[:q] if wanted(u)]
                after = [u for u in tiles[q:] if wanted(u)]
                row.append((before[-1], ns - 1) if before else (after[0], 0))
            table.append(row)

        def index(r, t, s, idx):
            q = r * per + t
            col, fixed_s = jnp.int32(0), jnp.int32(-1)
            for chip in range(N_CHIPS):
                for pos, (tile, hold) in enumerate(table[chip]):
                    here = (idx[0] == chip) & (q == pos)
                    col = jnp.where(here, tile, col)
                    fixed_s = jnp.where(here, -1 if hold is None else hold, fixed_s)
            return jnp.where(fixed_s < 0, s, fixed_s), col

        return index

    grid_spec = pltpu.PrefetchScalarGridSpec(
        num_scalar_prefetch=1,
        grid=(N_CHIPS, per, ns),
        in_specs=[pl.BlockSpec((ts, D_MODEL), lambda r, t, s, idx: (s, 0)), ANY, ANY],
        out_specs=[pl.BlockSpec((ts, PROJ_TILE), out_index(lambda tile: tile >= F32_FROM_TILE)),
                   pl.BlockSpec((ts, PROJ_TILE), out_index(lambda tile: tile <= BF16_TO_TILE)),
                   ANY, ANY],
        scratch_shapes=[pltpu.VMEM((D_MODEL, PROJ_TILE), BF16),
                        pltpu.SemaphoreType.DMA((6, n_piece)), pltpu.SemaphoreType.DMA((6, n_piece)),
                        pltpu.SemaphoreType.DMA(())],
    )
    return pl.pallas_call(
        body,
        name="gather_inproj",
        grid_spec=grid_spec,
        out_shape=[jax.ShapeDtypeStruct((s_len, IN_WIDTH), F32),
                   jax.ShapeDtypeStruct((s_len, IN_WIDTH), BF16),
                   jax.ShapeDtypeStruct((D_MODEL, IN_WIDTH), BF16),
                   jax.ShapeDtypeStruct((3, N_CHIPS, ROW_SHARD, D_MODEL), BF16)],
        compiler_params=_cparams(("arbitrary", "arbitrary", "arbitrary")),
    )(idx, h, w_in_b, w_sq_b)


def _place_own(idx, w_in_b, w_sq_b, w_all, wsq):
    n = 4
    r_in, r_sq = D_MODEL // n, ROW_SHARD // n

    def body(idx_ref, win_ref, wsq_ref, w_all_in, wsq_in, w_all_out, wsq_out):
        w_all_out[...] = win_ref[...]
        wsq_out[:, 0] = wsq_ref[...]

    grid_spec = pltpu.PrefetchScalarGridSpec(
        num_scalar_prefetch=1,
        grid=(n,),
        in_specs=[pl.BlockSpec((r_in, W_IN_SHARD), lambda r, idx: (r, 0)),
                  pl.BlockSpec((3, r_sq, D_MODEL), lambda r, idx: (0, r, 0)), ANY, ANY],
        out_specs=[pl.BlockSpec((r_in, W_IN_SHARD), lambda r, idx: (r, idx[0])),
                   pl.BlockSpec((3, 1, r_sq, D_MODEL), lambda r, idx: (0, idx[0], r, 0))],
    )
    return pl.pallas_call(
        body,
        name="place_own",
        grid_spec=grid_spec,
        out_shape=[jax.ShapeDtypeStruct(w_all.shape, BF16), jax.ShapeDtypeStruct(wsq.shape, BF16)],
        input_output_aliases={3: 0, 4: 1},
        compiler_params=_cparams(("arbitrary",)),
    )(idx, w_in_b, w_sq_b, w_all, wsq)


def _swap_halves(g_in, g_sq):
    n_in = 16
    n_piece = n_in + 3 * N_CHIPS
    rows = HALF_IN // n_in

    def body(gin_ref, gsq_ref, got_in, got_sq, send_sems, recv_sems):
        x, y, c, _ = _position()
        sibling = (x, y, 1 - c)

        def src_piece(p):
            if p < n_in:
                return gin_ref.at[pl.ds((1 - c) * HALF_IN + p * rows, rows), :]
            a, chip = divmod(p - n_in, N_CHIPS)
            return gsq_ref.at[a, chip, pl.ds((1 - c) * HALF_SQ, HALF_SQ), :]

        def dst_piece(p):
            if p < n_in:
                return got_in.at[pl.ds(p * rows, rows), :]
            a, chip = divmod(p - n_in, N_CHIPS)
            return got_sq.at[a, chip]

        out = [_remote(src_piece(p), dst_piece(p), send_sems.at[p], recv_sems.at[p], sibling) for p in range(n_piece)]
        for cp in out:
            cp.start()
        for cp in out:
            cp.wait()

    return pl.pallas_call(
        body,
        name="swap_halves",
        in_specs=[ANY, ANY],
        out_specs=[ANY, ANY],
        out_shape=[jax.ShapeDtypeStruct((HALF_IN, IN_WIDTH), F32),
                   jax.ShapeDtypeStruct((3, N_CHIPS, HALF_SQ, D_MODEL), F32)],
        scratch_shapes=[pltpu.SemaphoreType.DMA((n_piece,))] * 2,
    )(g_in, g_sq)


def _join_halves(r_in, r_sq):
    n_in = 16
    n_piece = n_in + 3
    rows = HALF_IN // n_in

    def body(in_alias, sq_alias, full_in, full_sq, send_sems, recv_sems):
        del in_alias, sq_alias
        x, y, c, _ = _position()
        sibling = (x, y, 1 - c)

        def piece(p, core):
            if p < n_in:
                return full_in.at[pl.ds(core * HALF_IN + p * rows, rows), :]
            return full_sq.at[p - n_in, pl.ds(core * HALF_SQ, HALF_SQ), :]

        out = [_remote(piece(p, c), piece(p, c), send_sems.at[p], recv_sems.at[p], sibling) for p in range(n_piece)]
        for cp in out:
            cp.start()
        for p in range(n_piece):
            _remote(piece(p, 1 - c), piece(p, 1 - c), send_sems.at[p], recv_sems.at[p], sibling).wait_recv()
        for cp in out:
            cp.wait_send()

    return pl.pallas_call(
        body,
        name="join_halves",
        in_specs=[ANY, ANY],
        out_specs=[ANY, ANY],
        out_shape=[jax.ShapeDtypeStruct((D_MODEL, W_IN_SHARD), F32),
                   jax.ShapeDtypeStruct((3, ROW_SHARD, D_MODEL), F32)],
        input_output_aliases={0: 0, 1: 1},
        scratch_shapes=[pltpu.SemaphoreType.DMA((n_piece,)), pltpu.SemaphoreType.DMA((n_piece,))],
    )(r_in, r_sq)


SMALL_ROWS = 56
N_DEV = 8


def _sum_small(part):
    def body(part_ref, out_ref, slots, send_sems, recv_sems):
        x, y, c, _ = _position()
        me = 4 * x + 2 * y + c
        slots[me] = part_ref[...]
        out = []
        for r in range(1, N_DEV):
            rx, ry, rc = (r >> 2) & 1, (r >> 1) & 1, r & 1
            to = (1 - x if rx else x, 1 - y if ry else y, 1 - c if rc else c)
            out.append(_remote(part_ref, slots.at[me], send_sems.at[r - 1], recv_sems.at[r - 1], to))
        for cp in out:
            cp.start()
        for r in range(1, N_DEV):
            _remote(part_ref, slots.at[me ^ r], send_sems.at[r - 1], recv_sems.at[r - 1], (x, y, c)).wait_recv()
        for cp in out:
            cp.wait_send()
        total = slots[0]
        for d in range(1, N_DEV):
            total = total + slots[d]
        out_ref[...] = total

    vmem = pl.BlockSpec(memory_space=pltpu.VMEM)
    return pl.pallas_call(
        body,
        name="sum_small",
        in_specs=[vmem],
        out_specs=vmem,
        out_shape=jax.ShapeDtypeStruct((SMALL_ROWS, HEAD_DIM), F32),
        scratch_shapes=[pltpu.VMEM((N_DEV, SMALL_ROWS, HEAD_DIM), F32),
                        pltpu.SemaphoreType.DMA((N_DEV - 1,)), pltpu.SemaphoreType.DMA((N_DEV - 1,))],
    )(part)


def _prefetch_call(body, name, idx, grid, in_specs, out_specs, out_shape, args):
    grid_spec = pltpu.PrefetchScalarGridSpec(num_scalar_prefetch=1, grid=grid, in_specs=in_specs, out_specs=out_specs)
    return pl.pallas_call(body, name=name, grid_spec=grid_spec, out_shape=out_shape,
                          compiler_params=_cparams(("arbitrary",) * len(grid)))(idx, *args)


def _sum_a_in(idx, g_in, got_in):
    tr = 128
    nr = HALF_IN // tr

    def body(idx_ref, a_ref, b_ref, o_ref):
        o_ref[0] = (a_ref[...] + b_ref[...]).astype(WIRE)

    return _prefetch_call(
        body, "sum_a_in", idx, (N_CHIPS, nr),
        [pl.BlockSpec((tr, W_IN_SHARD), lambda j, r, idx: (idx[1] * nr + r, j)),
         pl.BlockSpec((tr, W_IN_SHARD), lambda j, r, idx: (r, j))],
        pl.BlockSpec((1, tr, W_IN_SHARD), lambda j, r, idx: (j, r, 0)),
        jax.ShapeDtypeStruct((N_CHIPS, HALF_IN, W_IN_SHARD), WIRE), (g_in, got_in))


def _sum_a_sq(idx, g_sq, got_sq):
    blk = (1, 1, HALF_SQ, D_MODEL)

    def body(idx_ref, a_ref, b_ref, o_ref):
        o_ref[...] = (a_ref[...] + b_ref[...]).astype(WIRE)

    return _prefetch_call(
        body, "sum_a_sq", idx, (3, N_CHIPS),
        [pl.BlockSpec(blk, lambda a, j, idx: (a, j, idx[1], 0)), pl.BlockSpec(blk, lambda a, j, idx: (a, j, 0, 0))],
        pl.BlockSpec(blk, lambda a, j, idx: (a, j, 0, 0)),
        jax.ShapeDtypeStruct((3, N_CHIPS, HALF_SQ, D_MODEL), WIRE), (g_sq, got_sq))


def _sum_b_in(idx, s_in, got_in):
    tr = 128
    nr = HALF_IN // tr

    def body(idx_ref, a_ref, b_ref, o_ref):
        o_ref[...] = ((a_ref[0].astype(F32) + b_ref[0].astype(F32)) + b_ref[1].astype(F32)) + b_ref[2].astype(F32)

    return _prefetch_call(
        body, "sum_b_in", idx, (nr,),
        [pl.BlockSpec((1, tr, W_IN_SHARD), lambda r, idx: (idx[0], r, 0)),
         pl.BlockSpec((3, tr, W_IN_SHARD), lambda r, idx: (0, r, 0))],
        pl.BlockSpec((tr, W_IN_SHARD), lambda r, idx: (idx[1] * nr + r, 0)),
        jax.ShapeDtypeStruct((D_MODEL, W_IN_SHARD), F32), (s_in, got_in))


def _sum_b_sq(idx, s_sq, got_sq):
    def body(idx_ref, a_ref, b_ref, o_ref):
        o_ref[0] = ((a_ref[0, 0].astype(F32) + b_ref[0, 0].astype(F32)) + b_ref[1, 0].astype(F32)) + b_ref[2, 0].astype(F32)

    return _prefetch_call(
        body, "sum_b_sq", idx, (3,),
        [pl.BlockSpec((1, 1, HALF_SQ, D_MODEL), lambda a, idx: (a, idx[0], 0, 0)),
         pl.BlockSpec((3, 1, HALF_SQ, D_MODEL), lambda a, idx: (0, a, 0, 0))],
        pl.BlockSpec((1, HALF_SQ, D_MODEL), lambda a, idx: (a, idx[1], 0)),
        jax.ShapeDtypeStruct((3, ROW_SHARD, D_MODEL), F32), (s_sq, got_sq))


def _adamw_math(w, g, m, v):
    m = ADAM_B1 * m + (1.0 - ADAM_B1) * g
    v = ADAM_B2 * v + (1.0 - ADAM_B2) * (g * g)
    m_hat = m / (1.0 - ADAM_B1 ** ADAM_STEP)
    v_hat = v / (1.0 - ADAM_B2 ** ADAM_STEP)
    delta = -ADAM_LR * (m_hat / (jnp.sqrt(v_hat) + ADAM_EPS) + ADAM_WD * w)
    return delta, m, v


def _adamw(w, g, m, v, name):
    rows, cols = w.shape
    tr = min(128, rows)

    def body(w_ref, g_ref, m_ref, v_ref, d_ref, nm_ref, nv_ref):
        d_ref[...], nm_ref[...], nv_ref[...] = _adamw_math(w_ref[...], g_ref[...], m_ref[...], v_ref[...])

    spec = pl.BlockSpec((tr, cols), lambda r: (r, 0))
    return pl.pallas_call(
        body,
        name=name,
        grid=(rows // tr,),
        in_specs=[spec] * 4,
        out_specs=[spec] * 3,
        out_shape=[jax.ShapeDtypeStruct((rows, cols), F32)] * 3,
        compiler_params=_cparams(("arbitrary",)),
    )(w, g, m, v)


def _adamw_small(sums, w, m, v):
    def body(s_ref, w_ref, m_ref, v_ref, loss_ref, g_ref, d_ref, nm_ref, nv_ref):
        s = s_ref[...]
        w = w_ref[...]
        loss_ref[...] = s[0:1, 0:1]
        l0, l1 = w[24:32], w[32:40]
        mx = jnp.maximum(l0, l1)
        e0, e1 = jnp.exp(l0 - mx), jnp.exp(l1 - mx)
        p0, p1 = e0 / (e0 + e1), e1 / (e0 + e1)
        d_lb = s[32:40]
        g = jnp.concatenate([s[8:16], s[16:32], d_lb * p0 * (1.0 - p0), -d_lb * p0 * p1, s[40:48], s[48:56]], axis=0)
        g_ref[...] = g
        d_ref[...], nm_ref[...], nv_ref[...] = _adamw_math(w, g, m_ref[...], v_ref[...])

    packed = jax.ShapeDtypeStruct((SMALL_ROWS, HEAD_DIM), F32)
    return pl.pallas_call(
        body,
        name="adamw_small",
        out_shape=[jax.ShapeDtypeStruct((1, 1), F32), packed, packed, packed, packed],
    )(sums, w, m, v)


def _pack_small(ng, bg, lbl, hgn, fg):
    return jnp.concatenate([a.reshape(-1, HEAD_DIM) for a in (ng, bg, lbl, hgn, fg)], axis=0)


def _unpack_small(p):
    return (p[0:8].reshape(1, D_MODEL), p[8:24].reshape(1, 2 * D_MODEL), p[24:40].reshape(2, HEADS, HEAD_DIM),
            p[40:48].reshape(1, HEADS, HEAD_DIM), p[48:56].reshape(D_MODEL))


def kernel(x, norm_g, w_in, b_gate, lb_logits, hg_norm_g, w_sb_proj, w_hg_proj, w_out, final_norm_g, loss_target, m_norm_g, m_w_in, m_b_gate, m_lb_logits, m_hg_norm_g, m_w_sb_proj, m_w_hg_proj, m_w_out, m_final_norm_g, v_norm_g, v_w_in, v_b_gate, v_lb_logits, v_hg_norm_g, v_w_sb_proj, v_w_hg_proj, v_w_out, v_final_norm_g):
    s_len = x.shape[1]
    w_sq = jnp.stack([w_sb_proj[0], w_hg_proj[0], w_out[0]])
    idx = jnp.stack([2 * lax.axis_index("x") + lax.axis_index("y"), lax.axis_index("c")]).astype(jnp.int32)
    w_in_b, w_sq_b = w_in[0].astype(BF16), w_sq.astype(BF16)
    h, h_t = _prenorm(x[0], norm_g)
    proj, qkv, w_all, wsq = _gather_inproj(idx, h, w_in_b, w_sq_b)
    w_all, wsq = _place_own(idx, w_in_b, w_sq_b, w_all, wsq)
    wsq = wsq.reshape(3, D_MODEL, D_MODEL)

    (g_in, g_sb, g_hg, g_out, segs, dout, loss, d_bg, d_lb, d_hgn, d_fg) = _local_grads(
        x[0], loss_target[0], proj, h_t, qkv, b_gate, lb_logits.reshape(2, D_MODEL), hg_norm_g.reshape(1, D_MODEL),
        final_norm_g.reshape(1, D_MODEL), wsq[0], wsq[1], wsq[2])

    g_sq = jnp.stack([g_sb, g_hg, g_out]).reshape(3, N_CHIPS, ROW_SHARD, D_MODEL)
    got_in, got_sq = _swap_halves(g_in, g_sq)
    s_in, s_sq = _sum_a_in(idx, g_in, got_in), _sum_a_sq(idx, g_sq, got_sq)
    grad_x, d_ng, got_in, got_sq = _dx(segs, w_all, x[0], norm_g, dout, s_in, s_sq)
    grad_in, grad_sq = _join_halves(_sum_b_in(idx, s_in, got_in), _sum_b_sq(idx, s_sq, got_sq))

    d_in, nm_in, nv_in = _adamw(w_in[0], grad_in, m_w_in[0], v_w_in[0], "adamw_in")
    flat = lambda a, b, c: jnp.concatenate([a[0], b[0], c[0]], axis=0)
    d_sq, nm_sq, nv_sq = _adamw(flat(w_sb_proj, w_hg_proj, w_out), grad_sq.reshape(3 * ROW_SHARD, D_MODEL),
                                flat(m_w_sb_proj, m_w_hg_proj, m_w_out), flat(v_w_sb_proj, v_w_hg_proj, v_w_out),
                                "adamw_sq")

    pad = jnp.zeros((8, HEAD_DIM), F32).at[0, 0].set(loss[0, 0])
    part = jnp.concatenate([pad] + [a.reshape(-1, HEAD_DIM) for a in (d_ng, d_bg, d_lb, d_hgn, d_fg)], axis=0)
    sums = _sum_small(part)
    loss_out, g_sm, d_sm, nm_sm, nv_sm = _adamw_small(
        sums, _pack_small(norm_g, b_gate, lb_logits, hg_norm_g, final_norm_g),
        _pack_small(m_norm_g, m_b_gate, m_lb_logits, m_hg_norm_g, m_final_norm_g),
        _pack_small(v_norm_g, v_b_gate, v_lb_logits, v_hg_norm_g, v_final_norm_g))

    def big(t_in, t_sq):
        sq = t_sq.reshape(3, 1, ROW_SHARD, D_MODEL)
        return t_in[None], sq[0], sq[1], sq[2]

    def order(small, in_, sb, hg, out):
        ng, bg, lbl, hgn, fg = small
        return [ng, in_, bg, lbl, hgn, sb, hg, out, fg]

    outs = [loss_out[0, 0], grad_x[None]]
    for small, (t_in, t_sq) in ((g_sm, (grad_in, grad_sq)), (d_sm, (d_in, d_sq)), (nm_sm, (nm_in, nm_sq)), (nv_sm, (nv_in, nv_sq))):
        outs += order(_unpack_small(small), *big(t_in, t_sq))
    return tuple(outs)
```

```python
import functools

import jax
import jax.numpy as jnp
from jax import lax
from jax.experimental import pallas as pl
from jax.experimental.pallas import tpu as pltpu

F32 = jnp.float32
BF16 = jnp.bfloat16

D_MODEL = 1024
HEADS = 8
HEAD_DIM = 128
IN_WIDTH = 10240
N_CHIPS = 4
W_IN_SHARD = IN_WIDTH // N_CHIPS
ROW_SHARD = D_MODEL // N_CHIPS
RMS_EPS = 1e-6

OFF_SB_Q, OFF_SB_K, OFF_SB_V, OFF_SB_Z = 0, 1024, 2048, 3072
OFF_HG_Q, OFF_HG_F, OFF_HG_I, OFF_HG_Z, OFF_GATE = 4096, 5120, 6144, 7168, 8192

SB_BLOCK = 256
SB_FWD_HEADS = 4
SB_BWD_HEADS = 2
SB_ROWS = 256
SB_DEAD = -110.0
SB_GONE = -1e30
HG_CHUNK = 32
HG_PAIR = 2 * HG_CHUNK
HG_STEP = 256
HG_MID = HG_CHUNK // 2 - 1

ADAM_LR, ADAM_B1, ADAM_B2, ADAM_EPS, ADAM_WD, ADAM_STEP = 0.001, 0.9, 0.999, 1e-08, 0.01, 10

VMEM_LIMIT = 56 * 1024 * 1024
VMEM_LIMIT_DX = 60 * 1024 * 1024

MESH = pl.DeviceIdType.MESH


def _cparams(sem, vmem=VMEM_LIMIT):
    return pltpu.CompilerParams(dimension_semantics=sem, vmem_limit_bytes=vmem)


def _dot(a, b):
    return jnp.dot(a, b, preferred_element_type=F32)


def _dot_nt(a, b):
    return lax.dot_general(a, b, (((1,), (1,)), ((), ())), preferred_element_type=F32)


def _dot_tn(a, b):
    return lax.dot_general(a, b, (((0,), (0,)), ((), ())), preferred_element_type=F32)


def _split_dot(x, tri):
    hi = x.astype(BF16)
    lo = (x - hi.astype(F32)).astype(BF16)
    both = _dot(jnp.concatenate([hi, lo], axis=0), tri)
    return both[: x.shape[0]] + both[x.shape[0] :]


def _split_dot_left(tri, x):
    hi = x.astype(BF16)
    lo = (x - hi.astype(F32)).astype(BF16)
    return _dot(tri, hi) + _dot(tri, lo)


def _sigmoid(x):
    return 1.0 / (1.0 + jnp.exp(-x))


def _prenorm(x, norm_g):
    s_len = x.shape[0]
    ts = min(1024, s_len)

    def body(x_ref, g_ref, h_ref, ht_ref):
        xv = x_ref[...]
        r = lax.rsqrt(jnp.mean(xv * xv, axis=-1, keepdims=True) + RMS_EPS)
        hv = (xv * r) * g_ref[...]
        h_ref[...] = hv.astype(BF16)
        ht_ref[...] = hv.T.astype(BF16)

    return pl.pallas_call(
        body,
        name="prenorm",
        grid=(s_len // ts,),
        in_specs=[pl.BlockSpec((ts, D_MODEL), lambda s: (s, 0)), pl.BlockSpec((1, D_MODEL), lambda s: (0, 0))],
        out_specs=[pl.BlockSpec((ts, D_MODEL), lambda s: (s, 0)), pl.BlockSpec((D_MODEL, ts), lambda s: (0, s))],
        out_shape=[jax.ShapeDtypeStruct((s_len, D_MODEL), BF16), jax.ShapeDtypeStruct((D_MODEL, s_len), BF16)],
        compiler_params=_cparams(("arbitrary",)),
    )(x, norm_g)


def _sb_scores(qb, kb, causal, tri_excl, diag):
    z = _dot_nt(qb, kb) * HEAD_DIM ** -0.5
    ls_pos = jnp.minimum(z, 0.0) - jnp.log1p(jnp.exp(-jnp.abs(z)))
    log_not = ls_pos - z
    log_not_m = jnp.where(causal, log_not, 0.0) if diag else log_not
    return ls_pos, log_not, log_not_m, _split_dot(log_not_m, tri_excl)


def _sb_weights(ls_pos, suffix, carry, causal, diag):
    surv = suffix + carry
    w = jnp.exp(ls_pos + surv)
    return surv, (jnp.where(causal, w, 0.0) if diag else w)


def _sb_specs(s_len, blk, heads):
    width = heads * HEAD_DIM

    def blk_spec(off):
        return pl.BlockSpec((blk, width), lambda h, i: (i, off // width + h))

    def head_spec(off, buffers=2):
        return pl.BlockSpec((s_len, width), lambda h, i: (0, off // width + h), pipeline_mode=pl.Buffered(buffers))

    return blk_spec, head_spec


def _head_cols(p):
    return slice(p * HEAD_DIM, (p + 1) * HEAD_DIM)


def _sb_chains(blk, heads):
    rows = min(SB_ROWS, blk)
    return [(p, a) for p in range(heads) for a in range(blk // rows)], rows


def _sb_masks(blk, rows):
    row = lax.broadcasted_iota(jnp.int32, (rows, blk), 0)
    col = lax.broadcasted_iota(jnp.int32, (rows, blk), 1)
    causal = [row + a * rows > col for a in range(blk // rows)]
    row = lax.broadcasted_iota(jnp.int32, (blk, blk), 0)
    col = lax.broadcasted_iota(jnp.int32, (blk, blk), 1)
    tri_excl = (row > col).astype(BF16)
    tri_incl = (row >= col).astype(BF16)
    return causal, tri_excl, tri_incl


def _sb_alive(st, n_chain):
    alive = functools.reduce(jnp.maximum, [st[1 + 3 * c] for c in range(n_chain)])
    return jnp.max(alive) > SB_DEAD


def _sb_fwd(qkv):
    s_len = qkv.shape[0]
    blk = min(SB_BLOCK, s_len)
    nq = s_len // blk
    chains, rows = _sb_chains(blk, SB_FWD_HEADS)

    def body(q_ref, k_ref, v_ref, o_ref, of_ref):
        i = pl.program_id(1)
        causal, tri_excl, _ = _sb_masks(blk, rows)

        def tiles(specs, st):
            pre = []
            for j, diag, _ in specs:
                start = pl.multiple_of(j * blk, blk)
                for p, a in chains:
                    kb = k_ref[pl.ds(start, blk), _head_cols(p)]
                    qb = q_ref[a * rows : (a + 1) * rows, _head_cols(p)]
                    pre.append(_sb_scores(qb, kb, causal[a], tri_excl, diag) + (v_ref[pl.ds(start, blk), _head_cols(p)],))
            for t, (j, diag, valid) in enumerate(specs):
                new = []
                for c, (p, a) in enumerate(chains):
                    carry, acc, acc_lo = st[3 * c : 3 * c + 3]
                    if valid is not None:
                        carry = jnp.where(valid, carry, SB_GONE)
                    ls_pos, _, log_not_m, suffix, vb = pre[t * len(chains) + c]
                    surv, w = _sb_weights(ls_pos, suffix, carry, causal[a], diag)
                    wb = w.astype(BF16)
                    w_lo = (w - wb.astype(F32)).astype(BF16)
                    both = _dot(jnp.concatenate([wb, w_lo], axis=0), vb)
                    new += [surv[:, 0:1] + log_not_m[:, 0:1], acc + both[:rows], acc_lo + both[rows:]]
                st = tuple(new)
            return st

        zero = jnp.zeros((rows, HEAD_DIM), F32)
        st = tiles([(i, True, None), (jnp.maximum(i - 1, 0), False, i >= 1)],
                   (jnp.zeros((rows, 1), F32), zero, zero) * len(chains))

        def more(st):
            return (st[0] < i) & _sb_alive(st, len(chains))

        def step(st):
            return (st[0] + 1,) + tiles([(i - 1 - st[0], False, None)], st[1:])

        st = lax.while_loop(more, step, (1,) + st)[1:]
        for c, (p, a) in enumerate(chains):
            o_ref[a * rows : (a + 1) * rows, _head_cols(p)] = st[3 * c + 1]
            of_ref[a * rows : (a + 1) * rows, _head_cols(p)] = st[3 * c + 1] + st[3 * c + 2]

    blk_spec, head_spec = _sb_specs(s_len, blk, SB_FWD_HEADS)
    return pl.pallas_call(
        body,
        name="sb_fwd",
        grid=(HEADS // SB_FWD_HEADS, nq),
        in_specs=[blk_spec(OFF_SB_Q), head_spec(OFF_SB_K), head_spec(OFF_SB_V)],
        out_specs=[blk_spec(0), blk_spec(0)],
        out_shape=[jax.ShapeDtypeStruct((s_len, D_MODEL), F32)] * 2,
        compiler_params=_cparams(("arbitrary", "arbitrary")),
    )(qkv, qkv, qkv)


def _sb_bwd(qkv, o_fine, d_o):
    s_len = qkv.shape[0]
    blk = min(SB_BLOCK, s_len)
    nq = s_len // blk
    scale = HEAD_DIM ** -0.5
    chains, rows = _sb_chains(blk, SB_BWD_HEADS)

    def body(q_ref, k_ref, v_ref, of_ref, do_ref, dq_ref, dk_ref, dv_ref, dk_acc, dv_acc):
        i = pl.program_id(1)

        @pl.when(i == 0)
        def _():
            dk_acc[...] = jnp.zeros_like(dk_acc)
            dv_acc[...] = jnp.zeros_like(dv_acc)

        dob = do_ref[...].astype(BF16)
        prod = dob.astype(F32) * of_ref[...]
        causal, tri_excl, tri_incl = _sb_masks(blk, rows)

        def group(x, p, a):
            return x[a * rows : (a + 1) * rows, _head_cols(p)]

        totals = [jnp.sum(group(prod, p, a), axis=-1, keepdims=True) for p, a in chains]

        def tiles(specs, st):
            pre = []
            for j, diag, _ in specs:
                start = pl.multiple_of(j * blk, blk)
                for p, a in chains:
                    kb = k_ref[pl.ds(start, blk), _head_cols(p)]
                    vb = v_ref[pl.ds(start, blk), _head_cols(p)]
                    qb, dob_c = group(q_ref, p, a), group(dob, p, a)
                    pre.append(_sb_scores(qb, kb, causal[a], tri_excl, diag) + (_dot_nt(dob_c, vb), qb, kb, dob_c))
            for t, (j, diag, valid) in enumerate(specs):
                start = pl.multiple_of(j * blk, blk)
                mids = []
                for c, (p, a) in enumerate(chains):
                    c_not = st[3 * c]
                    if valid is not None:
                        c_not = jnp.where(valid, c_not, SB_GONE)
                    ls_pos, _, _, suffix, d_w = pre[t * len(chains) + c][:5]
                    surv, w = _sb_weights(ls_pos, suffix, c_not, causal[a], diag)
                    dlw = d_w * w
                    mids.append((surv, w, dlw, _split_dot(dlw, tri_incl)))
                new = []
                dk_new = [None] * SB_BWD_HEADS
                dv_new = [None] * SB_BWD_HEADS
                for c, (p, a) in enumerate(chains):
                    c_dlw, dq = st[3 * c + 1 : 3 * c + 3]
                    ls_pos, log_not, log_not_m, _, _, qb, kb, dob_c = pre[t * len(chains) + c]
                    surv, w, dlw, suffix = mids[c]
                    d_not = totals[c] - c_dlw - suffix
                    dz = (dlw * jnp.exp(log_not) - d_not * jnp.exp(ls_pos)) * scale
                    if diag:
                        dz = jnp.where(causal[a], dz, 0.0)
                    if valid is not None:
                        dz = jnp.where(valid, dz, 0.0)
                    dzb = dz.astype(BF16)
                    dk_c, dv_c = _dot_tn(dzb, qb), _dot_tn(w.astype(BF16), dob_c)
                    dk_new[p] = dk_c if dk_new[p] is None else dk_new[p] + dk_c
                    dv_new[p] = dv_c if dv_new[p] is None else dv_new[p] + dv_c
                    new += [surv[:, 0:1] + log_not_m[:, 0:1], c_dlw + suffix[:, 0:1], dq + _dot(dzb, kb)]
                for p in range(SB_BWD_HEADS):
                    dk_acc[pl.ds(start, blk), _head_cols(p)] += dk_new[p]
                    dv_acc[pl.ds(start, blk), _head_cols(p)] += dv_new[p]
                st = tuple(new)
            return st

        zcol = jnp.zeros((rows, 1), F32)
        st = tiles([(i, True, None), (jnp.maximum(i - 1, 0), False, i >= 1)],
                   (zcol, zcol, jnp.zeros((rows, HEAD_DIM), F32)) * len(chains))

        def more(st):
            return (st[0] < i) & _sb_alive(st, len(chains))

        def step(st):
            return (st[0] + 1,) + tiles([(i - 1 - st[0], False, None)], st[1:])

        st = lax.while_loop(more, step, (1,) + st)[1:]
        for c, (p, a) in enumerate(chains):
            dq_ref[a * rows : (a + 1) * rows, _head_cols(p)] = st[3 * c + 2].astype(BF16)

        @pl.when(i == nq - 1)
        def _():
            dk_ref[...] = dk_acc[...].astype(BF16)
            dv_ref[...] = dv_acc[...].astype(BF16)

    blk_spec, head_spec = _sb_specs(s_len, blk, SB_BWD_HEADS)
    width = SB_BWD_HEADS * HEAD_DIM
    return pl.pallas_call(
        body,
        name="sb_bwd",
        grid=(HEADS // SB_BWD_HEADS, nq),
        in_specs=[blk_spec(OFF_SB_Q), head_spec(OFF_SB_K, 1), head_spec(OFF_SB_V, 1), blk_spec(0), blk_spec(0)],
        out_specs=[blk_spec(0), head_spec(0), head_spec(0)],
        out_shape=[jax.ShapeDtypeStruct((s_len, D_MODEL), BF16)] * 3,
        scratch_shapes=[pltpu.VMEM((s_len, width), F32), pltpu.VMEM((s_len, width), F32)],
        compiler_params=_cparams(("arbitrary", "arbitrary")),
    )(qkv, qkv, qkv, o_fine, d_o)


def _hg_lower_bound(lbl_ref):
    l0 = lbl_ref[0:1, :]
    l1 = lbl_ref[1:2, :]
    mx = jnp.maximum(l0, l1)
    e0 = jnp.exp(l0 - mx)
    e1 = jnp.exp(l1 - mx)
    return e0 / (e0 + e1)


def _hg_gates(hq, hf, lb):
    sig_f = _sigmoid(hf)
    f = lb + (1.0 - lb) * sig_f
    g = jnp.log(f)
    kk = 1.0 - f
    sig_q = _sigmoid(hq)
    qq = hq * sig_q
    return qq, kk, g, f, sig_f, sig_q


def _period_bcast(x, r, rows, period):
    w = x.shape[-1]
    x3 = x.reshape(rows // period, period, w)
    return jnp.broadcast_to(x3[:, r : r + 1, :], x3.shape).reshape(rows, w)


def _blockdiag(rows, kind):
    row = lax.broadcasted_iota(jnp.int32, (rows, rows), 0)
    col = lax.broadcasted_iota(jnp.int32, (rows, rows), 1)
    if kind in ("next", "prev"):
        first, second = (row, col) if kind == "next" else (col, row)
        keep = ((row // HG_PAIR) == (col // HG_PAIR)) & (first % HG_PAIR < HG_CHUNK) & (second % HG_PAIR >= HG_CHUNK)
    else:
        keep = (row // HG_CHUNK) == (col // HG_CHUNK)
        if kind == "lower":
            keep = keep & (row >= col)
        elif kind == "upper":
            keep = keep & (row <= col)
    return jnp.where(keep, 1.0, 0.0).astype(BF16)


def _hg_operands(hq, hf, lb, rows):
    qq, kk, g, f, sig_f, sig_q = _hg_gates(hq, hf, lb)
    cum = _split_dot_left(_blockdiag(rows, "lower"), g)
    mid = _period_bcast(cum, HG_MID, rows, HG_CHUNK)
    last = _period_bcast(cum, HG_CHUNK - 1, rows, HG_CHUNK)
    last0 = _period_bcast(cum, HG_CHUNK - 1, rows, HG_PAIR)
    last1 = _period_bcast(cum, HG_PAIR - 1, rows, HG_PAIR)
    second = (lax.broadcasted_iota(jnp.int32, cum.shape, 0) % HG_PAIR) >= HG_CHUNK
    e = dict(qm=jnp.exp(cum - mid), km=jnp.exp(mid - cum), qd=jnp.exp(cum), kl=jnp.exp(last - cum),
             q_in=jnp.where(second, jnp.exp(last0), 1.0), k_out=jnp.where(second, 1.0, jnp.exp(last1)),
             pair=jnp.exp(last0 + last1))
    v = dict(qm=qq * e["qm"], km=kk * e["km"], qd=qq * e["qd"], kl=kk * e["kl"])
    v["qp"] = v["qd"] * e["q_in"]
    v["kp"] = v["kl"] * e["k_out"]
    return v, e, second, (f, sig_f, sig_q)


def _hg_store_operands(v, second, hi, refs):
    zero = jnp.zeros_like(v["qm"])
    q_cat, k_cat, qp_b, kp_b, v_b = refs
    q_cat[:, 0:D_MODEL] = jnp.where(second, zero, v["qm"]).astype(BF16)
    q_cat[:, D_MODEL : 2 * D_MODEL] = jnp.where(second, v["qm"], zero).astype(BF16)
    q_cat[:, 2 * D_MODEL :] = jnp.where(second, v["qd"], zero).astype(BF16)
    k_cat[:, 0:D_MODEL] = jnp.where(second, zero, v["km"]).astype(BF16)
    k_cat[:, D_MODEL : 2 * D_MODEL] = jnp.where(second, v["km"], zero).astype(BF16)
    k_cat[:, 2 * D_MODEL :] = jnp.where(second, zero, v["kl"]).astype(BF16)
    qp_b[...] = v["qp"].astype(BF16)
    kp_b[...] = v["kp"].astype(BF16)
    v_b[...] = hi.astype(BF16)


def _hg_pair_operands(cat, r0, c0):
    return jnp.concatenate([cat[r0 : r0 + HG_PAIR, g * D_MODEL + c0 : g * D_MODEL + c0 + HEAD_DIM] for g in range(3)], axis=1)


def _hg_fwd(proj, lbl):
    s_len = proj.shape[0]
    rows = min(HG_STEP, s_len)
    n_pairs = rows // HG_PAIR

    def body(hq_ref, hf_ref, hi_ref, lbl_ref, o_ref, st_ref, state, q_cat, k_cat, qp_b, kp_b, v_b):
        @pl.when(pl.program_id(0) == 0)
        def _():
            state[...] = jnp.zeros_like(state)

        v, e, second, _ = _hg_operands(hq_ref[...], hf_ref[...], _hg_lower_bound(lbl_ref), rows)
        _hg_store_operands(v, second, hi_ref[...], (q_cat, k_cat, qp_b, kp_b, v_b))
        e_pair = e["pair"]
        row = lax.broadcasted_iota(jnp.int32, (HG_PAIR, HG_PAIR), 0)
        col = lax.broadcasted_iota(jnp.int32, (HG_PAIR, HG_PAIR), 1)
        causal = row >= col

        for u in range(n_pairs):
            r0 = u * HG_PAIR
            sls = [(slice(r0, r0 + HG_PAIR), slice(h * HEAD_DIM, (h + 1) * HEAD_DIM)) for h in range(HEADS)]
            a_s = [jnp.where(causal, _dot_nt(_hg_pair_operands(q_cat, r0, h * HEAD_DIM),
                                             _hg_pair_operands(k_cat, r0, h * HEAD_DIM)), 0.0).astype(BF16)
                   for h in range(HEADS)]
            st_s = [state[h] for h in range(HEADS)]
            for h, sl in enumerate(sls):
                st_ref[u, h] = st_s[h]
                state[h] = st_s[h] * e_pair[r0 : r0 + 1, sl[1]] + _dot_tn(v_b[sl], kp_b[sl])
            for h, sl in enumerate(sls):
                o_ref[sl] = _dot(a_s[h], v_b[sl]) + _dot_nt(qp_b[sl], st_s[h].astype(BF16))

    def col_spec(off):
        return pl.BlockSpec((rows, D_MODEL), lambda s: (s, off // D_MODEL))

    bf_tile = pltpu.VMEM((rows, D_MODEL), BF16)
    bf_cat = pltpu.VMEM((rows, 3 * D_MODEL), BF16)
    scratch = [pltpu.VMEM((HEADS, HEAD_DIM, HEAD_DIM), F32), bf_cat, bf_cat, bf_tile, bf_tile, bf_tile]
    return pl.pallas_call(
        body,
        name="hg_fwd",
        grid=(s_len // rows,),
        in_specs=[col_spec(OFF_HG_Q), col_spec(OFF_HG_F), col_spec(OFF_HG_I), pl.BlockSpec((2, D_MODEL), lambda s: (0, 0))],
        out_specs=[
            pl.BlockSpec((rows, D_MODEL), lambda s: (s, 0)),
            pl.BlockSpec((n_pairs, HEADS, HEAD_DIM, HEAD_DIM), lambda s: (s, 0, 0, 0)),
        ],
        out_shape=[
            jax.ShapeDtypeStruct((s_len, D_MODEL), F32),
            jax.ShapeDtypeStruct((s_len // HG_PAIR, HEADS, HEAD_DIM, HEAD_DIM), F32),
        ],
        scratch_shapes=scratch,
        compiler_params=_cparams(("arbitrary",)),
    )(proj, proj, proj, lbl)


def _hg_bwd(proj, lbl, states, d_o):
    s_len = proj.shape[0]
    rows = min(HG_STEP, s_len)
    n_pairs = rows // HG_PAIR
    n_steps = s_len // rows

    def body(hq_ref, hf_ref, hi_ref, lbl_ref, st_ref, do_ref, dp_ref, dlb_ref,
             dstate, q_cat, k_cat, qp_b, kp_b, v_b, do_b, d_qcat, d_kcat, d_qp, d_kp, d_v, d_pair):
        @pl.when(pl.program_id(0) == 0)
        def _():
            dstate[...] = jnp.zeros_like(dstate)
            dlb_ref[...] = jnp.zeros_like(dlb_ref)

        lb = _hg_lower_bound(lbl_ref)
        hq = hq_ref[...]
        v, e, second, (f, sig_f, sig_q) = _hg_operands(hq, hf_ref[...], lb, rows)
        _hg_store_operands(v, second, hi_ref[...], (q_cat, k_cat, qp_b, kp_b, v_b))
        do_b[...] = do_ref[...].astype(BF16)
        e_pair = e["pair"]
        row = lax.broadcasted_iota(jnp.int32, (HG_PAIR, HG_PAIR), 0)
        col = lax.broadcasted_iota(jnp.int32, (HG_PAIR, HG_PAIR), 1)
        causal = row >= col

        for u in reversed(range(n_pairs)):
            r0 = u * HG_PAIR
            sls = [(slice(r0, r0 + HG_PAIR), slice(h * HEAD_DIM, (h + 1) * HEAD_DIM)) for h in range(HEADS)]
            ops = [(_hg_pair_operands(q_cat, r0, h * HEAD_DIM), _hg_pair_operands(k_cat, r0, h * HEAD_DIM))
                   for h in range(HEADS)]
            a_s = [jnp.where(causal, _dot_nt(lhs, rhs), 0.0).astype(BF16) for lhs, rhs in ops]
            da_s = [jnp.where(causal, _dot_nt(do_b[sl], v_b[sl]), 0.0).astype(BF16) for sl in sls]
            st0_s = [st_ref[u, h] for h in range(HEADS)]
            ds1_s = [dstate[h] for h in range(HEADS)]
            ds1b_s = [ds1.astype(BF16) for ds1 in ds1_s]
            for h, sl in enumerate(sls):
                decay = e_pair[r0 : r0 + 1, sl[1]]
                d_pair[u : u + 1, sl[1]] = decay * jnp.sum(ds1_s[h] * st0_s[h], axis=0, keepdims=True)
                dstate[h] = ds1_s[h] * decay + _dot_tn(do_b[sl], qp_b[sl])
            for h, sl in enumerate(sls):
                d_qp[sl] = _dot(do_b[sl], st0_s[h].astype(BF16))
                d_kp[sl] = _dot(v_b[sl], ds1b_s[h])
            for h, sl in enumerate(sls):
                d_v[sl] = _dot_tn(a_s[h], do_b[sl]) + _dot_nt(kp_b[sl], ds1b_s[h])
            for h, sl in enumerate(sls):
                d_lhs = _dot(da_s[h], ops[h][1])
                d_rhs = _dot_tn(da_s[h], ops[h][0])
                for g in range(3):
                    gsl = (sl[0], slice(g * D_MODEL + h * HEAD_DIM, g * D_MODEL + (h + 1) * HEAD_DIM))
                    d_qcat[gsl] = d_lhs[:, g * HEAD_DIM : (g + 1) * HEAD_DIM]
                    d_kcat[gsl] = d_rhs[:, g * HEAD_DIM : (g + 1) * HEAD_DIM]

        zero = jnp.zeros_like(hq)
        dqm = jnp.where(second, d_qcat[:, D_MODEL : 2 * D_MODEL], d_qcat[:, 0:D_MODEL])
        dkm = jnp.where(second, d_kcat[:, D_MODEL : 2 * D_MODEL], d_kcat[:, 0:D_MODEL])
        dqp, dkp = d_qp[...], d_kp[...]
        dqd = dqp * e["q_in"] + jnp.where(second, d_qcat[:, 2 * D_MODEL :], zero)
        dkl = dkp * e["k_out"] + jnp.where(second, zero, d_kcat[:, 2 * D_MODEL :])
        dq = dqm * e["qm"] + dqd * e["qd"]
        dk = dkm * e["km"] + dkl * e["kl"]
        t_kl = dkl * v["kl"]
        dcum = dqm * v["qm"] - dkm * v["km"] + dqd * v["qd"] - t_kl
        dp = d_pair[...]
        dp_b = jnp.broadcast_to(dp[:, None, :], (n_pairs, HG_PAIR, D_MODEL)).reshape(rows, D_MODEL)
        dg = (_split_dot_left(_blockdiag(rows, "upper"), dcum) + _split_dot_left(_blockdiag(rows, "all"), t_kl)
              + _split_dot_left(_blockdiag(rows, "next"), dqp * v["qp"])
              + _split_dot_left(_blockdiag(rows, "prev"), dkp * v["kp"]) + dp_b)
        df = dg / f - dk
        one_m = 1.0 - sig_f
        dp_ref[:, 0:D_MODEL] = (dq * (sig_q * (1.0 + hq * (1.0 - sig_q)))).astype(BF16)
        dp_ref[:, D_MODEL : 2 * D_MODEL] = (df * (1.0 - lb) * sig_f * one_m).astype(BF16)
        dp_ref[:, 2 * D_MODEL : 3 * D_MODEL] = d_v[...].astype(BF16)
        dlb_ref[...] += jnp.sum(df * one_m, axis=0, keepdims=True)

    def col_spec(off):
        return pl.BlockSpec((rows, D_MODEL), lambda s: (n_steps - 1 - s, off // D_MODEL))

    f32_tile = pltpu.VMEM((rows, D_MODEL), F32)
    f32_cat = pltpu.VMEM((rows, 3 * D_MODEL), F32)
    bf_tile = pltpu.VMEM((rows, D_MODEL), BF16)
    bf_cat = pltpu.VMEM((rows, 3 * D_MODEL), BF16)
    scratch = [pltpu.VMEM((HEADS, HEAD_DIM, HEAD_DIM), F32), bf_cat, bf_cat, bf_tile, bf_tile, bf_tile, bf_tile,
               f32_cat, f32_cat, f32_tile, f32_tile, f32_tile, pltpu.VMEM((n_pairs, D_MODEL), F32)]
    return pl.pallas_call(
        body,
        name="hg_bwd",
        grid=(n_steps,),
        in_specs=[
            col_spec(OFF_HG_Q), col_spec(OFF_HG_F), col_spec(OFF_HG_I),
            pl.BlockSpec((2, D_MODEL), lambda s: (0, 0)),
            pl.BlockSpec((n_pairs, HEADS, HEAD_DIM, HEAD_DIM), lambda s: (n_steps - 1 - s, 0, 0, 0)),
            pl.BlockSpec((rows, D_MODEL), lambda s: (n_steps - 1 - s, 0)),
        ],
        out_specs=[
            pl.BlockSpec((rows, 3 * D_MODEL), lambda s: (n_steps - 1 - s, 0)),
            pl.BlockSpec((1, D_MODEL), lambda s: (0, 0)),
        ],
        out_shape=[
            jax.ShapeDtypeStruct((s_len, 3 * D_MODEL), BF16),
            jax.ShapeDtypeStruct((1, D_MODEL), F32),
        ],
        scratch_shapes=scratch,
        compiler_params=_cparams(("arbitrary",)),
    )(proj, proj, proj, lbl, states, d_o)


def _mid(proj, sb_o, hg_o, x, target, b_gate, hg_gain, final_g, w_sb, w_hg, w_out):
    s_len = proj.shape[0]
    ts = min(256, s_len)
    inv_d = 1.0 / D_MODEL

    def body(zsb_ref, hz_ref, gl_ref, sbo_ref, hgo_ref, x_ref, tgt_ref, bg_ref, hgn_ref, fg_ref,
             wsb_ref, whg_ref, wout_ref,
             dout_ref, dsbo_ref, dhgo_ref, dmid_ref,
             asb_ref, dusb_ref, ahg_ref, duhg_ref, y_ref, doutb_ref,
             loss_ref, dfg_ref, dbg_ref, dhgn_ref):
        @pl.when(pl.program_id(0) == 0)
        def _():
            loss_ref[...] = jnp.zeros_like(loss_ref)
            dfg_ref[...] = jnp.zeros_like(dfg_ref)
            dbg_ref[...] = jnp.zeros_like(dbg_ref)
            dhgn_ref[...] = jnp.zeros_like(dhgn_ref)

        z_sb = zsb_ref[...]
        sb_o = sbo_ref[...]
        sig_zsb = _sigmoid(z_sb)
        silu_zsb = z_sb * sig_zsb
        a_sb_f = sb_o * silu_zsb
        a_sb = a_sb_f.astype(BF16)
        u_sb = _dot(a_sb, wsb_ref[...])

        hg_o = hgo_ref[...]
        gain = hgn_ref[...]
        r_parts, yn_parts = [], []
        for h in range(HEADS):
            oh = hg_o[:, h * HEAD_DIM : (h + 1) * HEAD_DIM]
            r = lax.rsqrt(jnp.mean(oh * oh, axis=-1, keepdims=True) + RMS_EPS)
            r_parts.append(jnp.broadcast_to(r, oh.shape))
            yn_parts.append(oh * r)
        r_hg = jnp.concatenate(r_parts, axis=-1)
        yn_hg = jnp.concatenate(yn_parts, axis=-1)
        hn = yn_hg * gain
        hz = hz_ref[...]
        sig_hz = _sigmoid(hz)
        silu_hz = hz * sig_hz
        a_hg_f = hn * silu_hz
        a_hg = a_hg_f.astype(BF16)
        u_hg = _dot(a_hg, whg_ref[...])

        gates = _sigmoid(gl_ref[...] + bg_ref[...])
        g_sb = gates[:, 0:D_MODEL]
        g_hg = gates[:, D_MODEL:]
        y_f = g_sb * u_sb + g_hg * u_hg
        y = y_f.astype(BF16)
        out = x_ref[...] + _dot(y, wout_ref[...])
        r2 = lax.rsqrt(jnp.mean(out * out, axis=-1, keepdims=True) + RMS_EPS)
        yn = out * r2
        fg = fg_ref[...]
        diff = yn * fg - tgt_ref[...]
        loss_ref[...] += 0.5 * inv_d * jnp.sum(diff * diff)

        dyf = diff * inv_d
        dfg_ref[...] += jnp.sum(dyf * yn, axis=0, keepdims=True)
        dyn = dyf * fg
        dout = r2 * (dyn - yn * jnp.mean(dyn * yn, axis=-1, keepdims=True))
        dout_ref[...] = dout
        doutb = dout.astype(BF16)
        doutb_ref[...] = doutb
        dy = _dot_nt(doutb, wout_ref[...])
        du_sb = (dy * g_sb).astype(BF16)
        du_hg = (dy * g_hg).astype(BF16)
        dgl_sb = dy * u_sb * g_sb * (1.0 - g_sb)
        dgl_hg = dy * u_hg * g_hg * (1.0 - g_hg)
        dmid_ref[:, 2 * D_MODEL : 3 * D_MODEL] = dgl_sb.astype(BF16)
        dmid_ref[:, 3 * D_MODEL :] = dgl_hg.astype(BF16)
        dbg_ref[:, 0:D_MODEL] += jnp.sum(dgl_sb, axis=0, keepdims=True)
        dbg_ref[:, D_MODEL:] += jnp.sum(dgl_hg, axis=0, keepdims=True)

        da_sb = _dot_nt(du_sb, wsb_ref[...])
        dsbo_ref[...] = (da_sb * silu_zsb).astype(BF16)
        dmid_ref[:, 0:D_MODEL] = (da_sb * sb_o * (sig_zsb * (1.0 + z_sb * (1.0 - sig_zsb)))).astype(BF16)

        da_hg = _dot_nt(du_hg, whg_ref[...])
        dhn = da_hg * silu_hz
        dmid_ref[:, D_MODEL : 2 * D_MODEL] = (da_hg * hn * (sig_hz * (1.0 + hz * (1.0 - sig_hz)))).astype(BF16)
        dhgn_ref[...] += jnp.sum(dhn * yn_hg, axis=0, keepdims=True)
        dyn_hg = dhn * gain
        prod = dyn_hg * yn_hg
        m_parts = []
        for h in range(HEADS):
            ph = prod[:, h * HEAD_DIM : (h + 1) * HEAD_DIM]
            m_parts.append(jnp.broadcast_to(jnp.mean(ph, axis=-1, keepdims=True), ph.shape))
        dhgo_ref[...] = (r_hg * (dyn_hg - yn_hg * jnp.concatenate(m_parts, axis=-1))).astype(BF16)

        asb_ref[...] = a_sb_f.T.astype(BF16)
        dusb_ref[...] = du_sb
        ahg_ref[...] = a_hg_f.T.astype(BF16)
        duhg_ref[...] = du_hg
        y_ref[...] = y_f.T.astype(BF16)

    def tile(width, off=0):
        return pl.BlockSpec((ts, width), lambda s: (s, off // width))

    def across():
        return pl.BlockSpec((D_MODEL, ts), lambda s: (0, s))

    def whole(shape):
        return pl.BlockSpec(shape, lambda s: (0,) * len(shape))

    def weight():
        return pl.BlockSpec((D_MODEL, D_MODEL), lambda s: (0, 0), pipeline_mode=pl.Buffered(1))

    f32_act = jax.ShapeDtypeStruct((s_len, D_MODEL), F32)
    bf_act = jax.ShapeDtypeStruct((s_len, D_MODEL), BF16)
    bf_act_t = jax.ShapeDtypeStruct((D_MODEL, s_len), BF16)
    return pl.pallas_call(
        body,
        name="mid",
        grid=(s_len // ts,),
        in_specs=[
            tile(D_MODEL, OFF_SB_Z), tile(D_MODEL, OFF_HG_Z), tile(2 * D_MODEL, OFF_GATE),
            tile(D_MODEL), tile(D_MODEL), tile(D_MODEL), tile(D_MODEL),
            whole((1, 2 * D_MODEL)), whole((1, D_MODEL)), whole((1, D_MODEL)),
            weight(), weight(), weight(),
        ],
        out_specs=[
            tile(D_MODEL), tile(D_MODEL), tile(D_MODEL), tile(4 * D_MODEL),
            across(), tile(D_MODEL), across(), tile(D_MODEL), across(), tile(D_MODEL),
            whole((1, 1)), whole((1, D_MODEL)), whole((1, 2 * D_MODEL)), whole((1, D_MODEL)),
        ],
        out_shape=[
            f32_act, bf_act, bf_act, jax.ShapeDtypeStruct((s_len, 4 * D_MODEL), BF16),
            bf_act_t, bf_act, bf_act_t, bf_act, bf_act_t, bf_act,
            jax.ShapeDtypeStruct((1, 1), F32), jax.ShapeDtypeStruct((1, D_MODEL), F32),
            jax.ShapeDtypeStruct((1, 2 * D_MODEL), F32), jax.ShapeDtypeStruct((1, D_MODEL), F32),
        ],
        compiler_params=_cparams(("arbitrary",)),
    )(proj, proj, proj, sb_o, hg_o, x, target, b_gate, hg_gain, final_g, w_sb, w_hg, w_out)


def _grad_square(a_t, b, name):
    s_len = b.shape[0]
    tk = min(1024, s_len)

    def body(a_ref, b_ref, o_ref):
        @pl.when(pl.program_id(0) == 0)
        def _():
            o_ref[...] = jnp.zeros_like(o_ref)

        o_ref[...] += _dot(a_ref[...], b_ref[...])

    return pl.pallas_call(
        body,
        name=name,
        grid=(s_len // tk,),
        in_specs=[pl.BlockSpec((D_MODEL, tk), lambda k: (0, k)), pl.BlockSpec((tk, D_MODEL), lambda k: (k, 0))],
        out_specs=pl.BlockSpec((D_MODEL, D_MODEL), lambda k: (0, 0)),
        out_shape=jax.ShapeDtypeStruct((D_MODEL, D_MODEL), F32),
        compiler_params=_cparams(("arbitrary",)),
    )(a_t, b)


SEG_WIDTHS = (1024, 1024, 1024, 4096, 3072)
SEG_TILE = 1024
SEG_BOUNDS = (0, 1, 2, 3, 7, 10)


def _w_in_tile(k):
    return jnp.where(k < 4, k, jnp.where(k < 7, k + 3, k - 3))


def _grad_w_in(h_t, segs):
    m, s_len = h_t.shape
    tk = min(1024, s_len)
    tn = SEG_TILE
    nk = s_len // tk
    bounds = SEG_BOUNDS

    def body(a_ref, *refs):
        seg_refs, o_ref = refs[:-1], refs[-1]
        j = pl.program_id(0)

        @pl.when(pl.program_id(1) == 0)
        def _():
            o_ref[...] = jnp.zeros_like(o_ref)

        for i, ref in enumerate(seg_refs):
            @pl.when((j >= bounds[i]) & (j < bounds[i + 1]))
            def _(ref=ref):
                o_ref[...] += _dot(a_ref[...], ref[...])

    def seg_spec(lo, hi):
        def index(j, k):
            return (jnp.where(j < lo, 0, jnp.where(j >= hi, nk - 1, k)), jnp.clip(j - lo, 0, hi - lo - 1))
        return pl.BlockSpec((tk, tn), index)

    return pl.pallas_call(
        body,
        name="grad_w_in",
        grid=(IN_WIDTH // tn, nk),
        in_specs=[pl.BlockSpec((m, tk), lambda j, k: (0, k))]
        + [seg_spec(bounds[i], bounds[i + 1]) for i in range(len(SEG_WIDTHS))],
        out_specs=pl.BlockSpec((m, tn), lambda j, k: (0, _w_in_tile(j))),
        out_shape=jax.ShapeDtypeStruct((m, IN_WIDTH), F32),
        compiler_params=_cparams(("arbitrary", "arbitrary")),
    )(h_t, *segs)


EXCHANGE_IN_PIECES = 8
EXCHANGE_PIECES = EXCHANGE_IN_PIECES + 3


def _exchange_copies(sin_ref, ssq_ref, got_in, got_sq, send_sems, recv_sems):
    _, _, c, chips = _position()
    rows = HALF_IN // EXCHANGE_IN_PIECES
    copies = []
    for k, (px, py) in enumerate(chips):
        chip = 2 * px + py
        for p in range(EXCHANGE_PIECES):
            if p < EXCHANGE_IN_PIECES:
                src, dst = sin_ref.at[chip, pl.ds(p * rows, rows), :], got_in.at[k, pl.ds(p * rows, rows), :]
            else:
                src, dst = ssq_ref.at[p - EXCHANGE_IN_PIECES, chip], got_sq.at[k, p - EXCHANGE_IN_PIECES]
            copies.append(_remote(src, dst, send_sems.at[k, p], recv_sems.at[k, p], (px, py, c)))
    return copies


def _dx(segs, w_all, x, norm_g, dout, s_in, s_sq):
    s_len = x.shape[0]
    ts = min(1024, s_len)
    tk = SEG_TILE
    nk = IN_WIDTH // tk
    ns = s_len // ts
    bounds = SEG_BOUNDS
    n_seg = len(SEG_WIDTHS)

    def body(*refs):
        seg_refs = refs[:n_seg]
        w_ref, x_ref, g_ref, dout_ref, sin_ref, ssq_ref, gx_ref, dg_ref, got_in, got_sq, acc, send_sems, recv_sems = refs[n_seg:]
        s, k = pl.program_id(0), pl.program_id(1)

        @pl.when((s == 0) & (k == 0))
        def _():
            dg_ref[...] = jnp.zeros_like(dg_ref)
            for cp in _exchange_copies(sin_ref, ssq_ref, got_in, got_sq, send_sems, recv_sems):
                cp.start()

        @pl.when(k == 0)
        def _():
            acc[...] = jnp.zeros_like(acc)

        for i, ref in enumerate(seg_refs):
            @pl.when((k >= bounds[i]) & (k < bounds[i + 1]))
            def _(ref=ref):
                acc[...] += _dot_nt(ref[...], w_ref[...])

        @pl.when(k == nk - 1)
        def _():
            dh = acc[...]
            xv = x_ref[...]
            r = lax.rsqrt(jnp.mean(xv * xv, axis=-1, keepdims=True) + RMS_EPS)
            xn = xv * r
            dg_ref[...] += jnp.sum(dh * xn, axis=0, keepdims=True)
            dxn = dh * g_ref[...]
            gx_ref[...] = r * (dxn - xn * jnp.mean(dxn * xn, axis=-1, keepdims=True)) + dout_ref[...]

        @pl.when((s == ns - 1) & (k == nk - 1))
        def _():
            for cp in _exchange_copies(sin_ref, ssq_ref, got_in, got_sq, send_sems, recv_sems):
                cp.wait()

    def seg_spec(lo, hi):
        return pl.BlockSpec((ts, tk), lambda s, k: (s, jnp.clip(k - lo, 0, hi - lo - 1)))

    row_tile = pl.BlockSpec((ts, D_MODEL), lambda s, k: (s, 0))
    vec = pl.BlockSpec((1, D_MODEL), lambda s, k: (0, 0))
    return pl.pallas_call(
        body,
        name="dx",
        grid=(ns, nk),
        in_specs=[seg_spec(bounds[i], bounds[i + 1]) for i in range(n_seg)] + [
            pl.BlockSpec((D_MODEL, tk), lambda s, k: (0, _w_in_tile(k))),
            row_tile, vec, row_tile, ANY, ANY,
        ],
        out_specs=[row_tile, vec, ANY, ANY],
        out_shape=[jax.ShapeDtypeStruct((s_len, D_MODEL), F32), jax.ShapeDtypeStruct((1, D_MODEL), F32),
                   jax.ShapeDtypeStruct((3, HALF_IN, W_IN_SHARD), WIRE),
                   jax.ShapeDtypeStruct((3, 3, HALF_SQ, D_MODEL), WIRE)],
        scratch_shapes=[pltpu.VMEM((ts, D_MODEL), F32),
                        pltpu.SemaphoreType.DMA((3, EXCHANGE_PIECES)), pltpu.SemaphoreType.DMA((3, EXCHANGE_PIECES))],
        compiler_params=_cparams(("arbitrary", "arbitrary"), vmem=VMEM_LIMIT_DX),
    )(*segs, w_all, x, norm_g, dout, s_in, s_sq)


def _local_grads(x, target, proj, h_t, qkv, b_gate, lbl, hg_gain, final_g, w_sb, w_hg, w_out):
    sb_o, sb_o_fine = _sb_fwd(qkv)
    hg_o, states = _hg_fwd(proj, lbl)
    (dout, d_sbo, d_hgo, d_mid, a_sb, du_sb, a_hg, du_hg, y, doutb,
     loss, d_fg, d_bg, d_hgn) = _mid(proj, sb_o, hg_o, x, target, b_gate, hg_gain, final_g, w_sb, w_hg, w_out)
    g_w_sb = _grad_square(a_sb, du_sb, "grad_w_sb")
    g_w_hg = _grad_square(a_hg, du_hg, "grad_w_hg")
    g_w_out = _grad_square(y, doutb, "grad_w_out")
    d_q, d_k, d_v = _sb_bwd(qkv, sb_o_fine, d_sbo)
    d_hg, d_lb = _hg_bwd(proj, lbl, states, d_hgo)
    segs = (d_q, d_k, d_v, d_mid, d_hg)
    g_w_in = _grad_w_in(h_t, segs)
    return g_w_in, g_w_sb, g_w_hg, g_w_out, segs, dout, loss, d_bg, d_lb, d_hgn, d_fg


ANY = pl.BlockSpec(memory_space=pl.ANY)
WIRE = BF16
HALF_IN = D_MODEL // 2
HALF_SQ = ROW_SHARD // 2


def _position():
    x, y, c = lax.axis_index("x"), lax.axis_index("y"), lax.axis_index("c")
    chips = [(1 - x, y), (x, 1 - y), (1 - x, 1 - y)]
    return x, y, c, chips


def _remote(src, dst, send_sem, recv_sem, to):
    return pltpu.make_async_remote_copy(src_ref=src, dst_ref=dst, send_sem=send_sem, recv_sem=recv_sem,
                                        device_id=to, device_id_type=MESH)


PROJ_TILE = 1280
F32_FROM_TILE = 2
BF16_TO_TILE = 2


def _gather_inproj(idx, h, w_in_b, w_sq_b):
    s_len = h.shape[0]
    ts = min(1024, s_len)
    ns = s_len // ts
    per = W_IN_SHARD // PROJ_TILE
    n_in = 4
    n_piece = n_in + 3
    rows = HALF_IN // n_in

    def chip_at(r, me):
        return me ^ jnp.where(r == 1, 2, jnp.where(r == 2, 1, jnp.where(r == 3, 3, 0)))

    def body(idx_ref, h_ref, win_ref, wsqb_ref, proj_ref, qkv_ref, wall_ref, wsq_ref, wbuf, send_sems, recv_sems, w_sem):
        r, t, s = pl.program_id(0), pl.program_id(1), pl.program_id(2)
        x, y, c, chips = _position()
        me = 2 * x + y
        sibling = (x, y, 1 - c)
        first = (t == 0) & (s == 0)

        def src_piece(p):
            if p < n_in:
                return win_ref.at[pl.ds(c * HALF_IN + p * rows, rows), :]
            return wsqb_ref.at[p - n_in, pl.ds(c * HALF_SQ, HALF_SQ), :]

        def piece(p, chip, core):
            if p < n_in:
                cols = pl.ds(pl.multiple_of(chip * W_IN_SHARD, W_IN_SHARD), W_IN_SHARD)
                return wall_ref.at[pl.ds(core * HALF_IN + p * rows, rows), cols]
            return wsq_ref.at[p - n_in, chip, pl.ds(core * HALF_SQ, HALF_SQ), :]

        def send(k, p):
            px, py = chips[k]
            return _remote(src_piece(p), piece(p, me, c), send_sems.at[k, p], recv_sems.at[k, p], (px, py, c))

        def forward(k, p, core):
            px, py = chips[k]
            got = piece(p, 2 * px + py, core)
            return _remote(got, got, send_sems.at[3 + k, p], recv_sems.at[3 + k, p], sibling)

        @pl.when((r == 0) & first)
        def _():
            for k in range(2):
                for p in range(n_piece):
                    send(k, p).start()

        for k in range(3):
            @pl.when((r == k + 1) & first)
            def _(k=k):
                px, py = chips[k]
                for p in range(n_piece):
                    got = piece(p, 2 * px + py, c)
                    _remote(got, got, send_sems.at[k, p], recv_sems.at[k, p], (px, py, c)).wait_recv()
                    forward(k, p, c).start()
                if k == 0:
                    for p in range(n_piece):
                        send(2, p).start()
                for p in range(n_piece):
                    forward(k, p, 1 - c).wait_recv()

        @pl.when(s == 0)
        def _():
            col = pl.multiple_of(t * PROJ_TILE, PROJ_TILE)

            @pl.when(r == 0)
            def _():
                cp = pltpu.make_async_copy(win_ref.at[:, pl.ds(col, PROJ_TILE)], wbuf, w_sem)
                cp.start()
                cp.wait()

            @pl.when(r > 0)
            def _():
                off = pl.multiple_of(chip_at(r, me) * W_IN_SHARD + col, PROJ_TILE)
                cp = pltpu.make_async_copy(wall_ref.at[:, pl.ds(off, PROJ_TILE)], wbuf, w_sem)
                cp.start()
                cp.wait()

        tile_now = per * chip_at(r, me) + t
        want_f32, want_bf16 = tile_now >= F32_FROM_TILE, tile_now <= BF16_TO_TILE

        @pl.when(want_f32 & jnp.logical_not(want_bf16))
        def _():
            proj_ref[...] = _dot(h_ref[...], wbuf[...])

        @pl.when(want_bf16 & jnp.logical_not(want_f32))
        def _():
            qkv_ref[...] = _dot(h_ref[...], wbuf[...]).astype(BF16)

        @pl.when(want_f32 & want_bf16)
        def _():
            p = _dot(h_ref[...], wbuf[...])
            proj_ref[...] = p
            qkv_ref[...] = p.astype(BF16)

        @pl.when((r == 3) & (t == per - 1) & (s == ns - 1))
        def _():
            for k in range(3):
                for p in range(n_piece):
                    send(k, p).wait_send()
                    forward(k, p, c).wait_send()

    def out_index(wanted):
        order = [0, 2, 1, 3]
        table = []
        for chip in range(N_CHIPS):
            tiles = [per * (chip ^ order[q // per]) + q % per for q in range(N_CHIPS * per)]
            row = []
            for q, tile in enumerate(tiles):
                if wanted(tile):
                    row.append((tile, None))
                    continue
                before = [u for u in tiles[:q] if wanted(u)]
                after = [u for u in tiles[q:] if wanted(u)]
                row.append((before[-1], ns - 1) if before else (after[0], 0))
            table.append(row)

        def index(r, t, s, idx):
            q = r * per + t
            col, fixed_s = jnp.int32(0), jnp.int32(-1)
            for chip in range(N_CHIPS):
                for pos, (tile, hold) in enumerate(table[chip]):
                    here = (idx[0] == chip) & (q == pos)
                    col = jnp.where(here, tile, col)
                    fixed_s = jnp.where(here, -1 if hold is None else hold, fixed_s)
            return jnp.where(fixed_s < 0, s, fixed_s), col

        return index

    grid_spec = pltpu.PrefetchScalarGridSpec(
        num_scalar_prefetch=1,
        grid=(N_CHIPS, per, ns),
        in_specs=[pl.BlockSpec((ts, D_MODEL), lambda r, t, s, idx: (s, 0)), ANY, ANY],
        out_specs=[pl.BlockSpec((ts, PROJ_TILE), out_index(lambda tile: tile >= F32_FROM_TILE)),
                   pl.BlockSpec((ts, PROJ_TILE), out_index(lambda tile: tile <= BF16_TO_TILE)),
                   ANY, ANY],
        scratch_shapes=[pltpu.VMEM((D_MODEL, PROJ_TILE), BF16),
                        pltpu.SemaphoreType.DMA((6, n_piece)), pltpu.SemaphoreType.DMA((6, n_piece)),
                        pltpu.SemaphoreType.DMA(())],
    )
    return pl.pallas_call(
        body,
        name="gather_inproj",
        grid_spec=grid_spec,
        out_shape=[jax.ShapeDtypeStruct((s_len, IN_WIDTH), F32),
                   jax.ShapeDtypeStruct((s_len, IN_WIDTH), BF16),
                   jax.ShapeDtypeStruct((D_MODEL, IN_WIDTH), BF16),
                   jax.ShapeDtypeStruct((3, N_CHIPS, ROW_SHARD, D_MODEL), BF16)],
        compiler_params=_cparams(("arbitrary", "arbitrary", "arbitrary")),
    )(idx, h, w_in_b, w_sq_b)


def _place_own(idx, w_in_b, w_sq_b, w_all, wsq):
    n = 4
    r_in, r_sq = D_MODEL // n, ROW_SHARD // n

    def body(idx_ref, win_ref, wsq_ref, w_all_in, wsq_in, w_all_out, wsq_out):
        w_all_out[...] = win_ref[...]
        wsq_out[:, 0] = wsq_ref[...]

    grid_spec = pltpu.PrefetchScalarGridSpec(
        num_scalar_prefetch=1,
        grid=(n,),
        in_specs=[pl.BlockSpec((r_in, W_IN_SHARD), lambda r, idx: (r, 0)),
                  pl.BlockSpec((3, r_sq, D_MODEL), lambda r, idx: (0, r, 0)), ANY, ANY],
        out_specs=[pl.BlockSpec((r_in, W_IN_SHARD), lambda r, idx: (r, idx[0])),
                   pl.BlockSpec((3, 1, r_sq, D_MODEL), lambda r, idx: (0, idx[0], r, 0))],
    )
    return pl.pallas_call(
        body,
        name="place_own",
        grid_spec=grid_spec,
        out_shape=[jax.ShapeDtypeStruct(w_all.shape, BF16), jax.ShapeDtypeStruct(wsq.shape, BF16)],
        input_output_aliases={3: 0, 4: 1},
        compiler_params=_cparams(("arbitrary",)),
    )(idx, w_in_b, w_sq_b, w_all, wsq)


def _swap_halves(g_in, g_sq):
    n_in = 16
    n_piece = n_in + 3 * N_CHIPS
    rows = HALF_IN // n_in

    def body(gin_ref, gsq_ref, got_in, got_sq, send_sems, recv_sems):
        x, y, c, _ = _position()
        sibling = (x, y, 1 - c)

        def src_piece(p):
            if p < n_in:
                return gin_ref.at[pl.ds((1 - c) * HALF_IN + p * rows, rows), :]
            a, chip = divmod(p - n_in, N_CHIPS)
            return gsq_ref.at[a, chip, pl.ds((1 - c) * HALF_SQ, HALF_SQ), :]

        def dst_piece(p):
            if p < n_in:
                return got_in.at[pl.ds(p * rows, rows), :]
            a, chip = divmod(p - n_in, N_CHIPS)
            return got_sq.at[a, chip]

        out = [_remote(src_piece(p), dst_piece(p), send_sems.at[p], recv_sems.at[p], sibling) for p in range(n_piece)]
        for cp in out:
            cp.start()
        for cp in out:
            cp.wait()

    return pl.pallas_call(
        body,
        name="swap_halves",
        in_specs=[ANY, ANY],
        out_specs=[ANY, ANY],
        out_shape=[jax.ShapeDtypeStruct((HALF_IN, IN_WIDTH), F32),
                   jax.ShapeDtypeStruct((3, N_CHIPS, HALF_SQ, D_MODEL), F32)],
        scratch_shapes=[pltpu.SemaphoreType.DMA((n_piece,))] * 2,
    )(g_in, g_sq)


def _join_halves(r_in, r_sq):
    n_in = 16
    n_piece = n_in + 3
    rows = HALF_IN // n_in

    def body(in_alias, sq_alias, full_in, full_sq, send_sems, recv_sems):
        del in_alias, sq_alias
        x, y, c, _ = _position()
        sibling = (x, y, 1 - c)

        def piece(p, core):
            if p < n_in:
                return full_in.at[pl.ds(core * HALF_IN + p * rows, rows), :]
            return full_sq.at[p - n_in, pl.ds(core * HALF_SQ, HALF_SQ), :]

        out = [_remote(piece(p, c), piece(p, c), send_sems.at[p], recv_sems.at[p], sibling) for p in range(n_piece)]
        for cp in out:
            cp.start()
        for p in range(n_piece):
            _remote(piece(p, 1 - c), piece(p, 1 - c), send_sems.at[p], recv_sems.at[p], sibling).wait_recv()
        for cp in out:
            cp.wait_send()

    return pl.pallas_call(
        body,
        name="join_halves",
        in_specs=[ANY, ANY],
        out_specs=[ANY, ANY],
        out_shape=[jax.ShapeDtypeStruct((D_MODEL, W_IN_SHARD), F32),
                   jax.ShapeDtypeStruct((3, ROW_SHARD, D_MODEL), F32)],
        input_output_aliases={0: 0, 1: 1},
        scratch_shapes=[pltpu.SemaphoreType.DMA((n_piece,)), pltpu.SemaphoreType.DMA((n_piece,))],
    )(r_in, r_sq)


SMALL_ROWS = 56
N_DEV = 8


def _sum_small(part):
    def body(part_ref, out_ref, slots, send_sems, recv_sems):
        x, y, c, _ = _position()
        me = 4 * x + 2 * y + c
        slots[me] = part_ref[...]
        out = []
        for r in range(1, N_DEV):
            rx, ry, rc = (r >> 2) & 1, (r >> 1) & 1, r & 1
            to = (1 - x if rx else x, 1 - y if ry else y, 1 - c if rc else c)
            out.append(_remote(part_ref, slots.at[me], send_sems.at[r - 1], recv_sems.at[r - 1], to))
        for cp in out:
            cp.start()
        for r in range(1, N_DEV):
            _remote(part_ref, slots.at[me ^ r], send_sems.at[r - 1], recv_sems.at[r - 1], (x, y, c)).wait_recv()
        for cp in out:
            cp.wait_send()
        total = slots[0]
        for d in range(1, N_DEV):
            total = total + slots[d]
        out_ref[...] = total

    vmem = pl.BlockSpec(memory_space=pltpu.VMEM)
    return pl.pallas_call(
        body,
        name="sum_small",
        in_specs=[vmem],
        out_specs=vmem,
        out_shape=jax.ShapeDtypeStruct((SMALL_ROWS, HEAD_DIM), F32),
        scratch_shapes=[pltpu.VMEM((N_DEV, SMALL_ROWS, HEAD_DIM), F32),
                        pltpu.SemaphoreType.DMA((N_DEV - 1,)), pltpu.SemaphoreType.DMA((N_DEV - 1,))],
    )(part)


def _prefetch_call(body, name, idx, grid, in_specs, out_specs, out_shape, args):
    grid_spec = pltpu.PrefetchScalarGridSpec(num_scalar_prefetch=1, grid=grid, in_specs=in_specs, out_specs=out_specs)
    return pl.pallas_call(body, name=name, grid_spec=grid_spec, out_shape=out_shape,
                          compiler_params=_cparams(("arbitrary",) * len(grid)))(idx, *args)


def _sum_a_in(idx, g_in, got_in):
    tr = 128
    nr = HALF_IN // tr

    def body(idx_ref, a_ref, b_ref, o_ref):
        o_ref[0] = (a_ref[...] + b_ref[...]).astype(WIRE)

    return _prefetch_call(
        body, "sum_a_in", idx, (N_CHIPS, nr),
        [pl.BlockSpec((tr, W_IN_SHARD), lambda j, r, idx: (idx[1] * nr + r, j)),
         pl.BlockSpec((tr, W_IN_SHARD), lambda j, r, idx: (r, j))],
        pl.BlockSpec((1, tr, W_IN_SHARD), lambda j, r, idx: (j, r, 0)),
        jax.ShapeDtypeStruct((N_CHIPS, HALF_IN, W_IN_SHARD), WIRE), (g_in, got_in))


def _sum_a_sq(idx, g_sq, got_sq):
    blk = (1, 1, HALF_SQ, D_MODEL)

    def body(idx_ref, a_ref, b_ref, o_ref):
        o_ref[...] = (a_ref[...] + b_ref[...]).astype(WIRE)

    return _prefetch_call(
        body, "sum_a_sq", idx, (3, N_CHIPS),
        [pl.BlockSpec(blk, lambda a, j, idx: (a, j, idx[1], 0)), pl.BlockSpec(blk, lambda a, j, idx: (a, j, 0, 0))],
        pl.BlockSpec(blk, lambda a, j, idx: (a, j, 0, 0)),
        jax.ShapeDtypeStruct((3, N_CHIPS, HALF_SQ, D_MODEL), WIRE), (g_sq, got_sq))


def _sum_b_in(idx, s_in, got_in):
    tr = 128
    nr = HALF_IN // tr

    def body(idx_ref, a_ref, b_ref, o_ref):
        o_ref[...] = ((a_ref[0].astype(F32) + b_ref[0].astype(F32)) + b_ref[1].astype(F32)) + b_ref[2].astype(F32)

    return _prefetch_call(
        body, "sum_b_in", idx, (nr,),
        [pl.BlockSpec((1, tr, W_IN_SHARD), lambda r, idx: (idx[0], r, 0)),
         pl.BlockSpec((3, tr, W_IN_SHARD), lambda r, idx: (0, r, 0))],
        pl.BlockSpec((tr, W_IN_SHARD), lambda r, idx: (idx[1] * nr + r, 0)),
        jax.ShapeDtypeStruct((D_MODEL, W_IN_SHARD), F32), (s_in, got_in))


def _sum_b_sq(idx, s_sq, got_sq):
    def body(idx_ref, a_ref, b_ref, o_ref):
        o_ref[0] = ((a_ref[0, 0].astype(F32) + b_ref[0, 0].astype(F32)) + b_ref[1, 0].astype(F32)) + b_ref[2, 0].astype(F32)

    return _prefetch_call(
        body, "sum_b_sq", idx, (3,),
        [pl.BlockSpec((1, 1, HALF_SQ, D_MODEL), lambda a, idx: (a, idx[0], 0, 0)),
         pl.BlockSpec((3, 1, HALF_SQ, D_MODEL), lambda a, idx: (0, a, 0, 0))],
        pl.BlockSpec((1, HALF_SQ, D_MODEL), lambda a, idx: (a, idx[1], 0)),
        jax.ShapeDtypeStruct((3, ROW_SHARD, D_MODEL), F32), (s_sq, got_sq))


def _adamw_math(w, g, m, v):
    m = ADAM_B1 * m + (1.0 - ADAM_B1) * g
    v = ADAM_B2 * v + (1.0 - ADAM_B2) * (g * g)
    m_hat = m / (1.0 - ADAM_B1 ** ADAM_STEP)
    v_hat = v / (1.0 - ADAM_B2 ** ADAM_STEP)
    delta = -ADAM_LR * (m_hat / (jnp.sqrt(v_hat) + ADAM_EPS) + ADAM_WD * w)
    return delta, m, v


def _adamw(w, g, m, v, name):
    rows, cols = w.shape
    tr = min(128, rows)

    def body(w_ref, g_ref, m_ref, v_ref, d_ref, nm_ref, nv_ref):
        d_ref[...], nm_ref[...], nv_ref[...] = _adamw_math(w_ref[...], g_ref[...], m_ref[...], v_ref[...])

    spec = pl.BlockSpec((tr, cols), lambda r: (r, 0))
    return pl.pallas_call(
        body,
        name=name,
        grid=(rows // tr,),
        in_specs=[spec] * 4,
        out_specs=[spec] * 3,
        out_shape=[jax.ShapeDtypeStruct((rows, cols), F32)] * 3,
        compiler_params=_cparams(("arbitrary",)),
    )(w, g, m, v)


def _adamw_small(sums, w, m, v):
    def body(s_ref, w_ref, m_ref, v_ref, loss_ref, g_ref, d_ref, nm_ref, nv_ref):
        s = s_ref[...]
        w = w_ref[...]
        loss_ref[...] = s[0:1, 0:1]
        l0, l1 = w[24:32], w[32:40]
        mx = jnp.maximum(l0, l1)
        e0, e1 = jnp.exp(l0 - mx), jnp.exp(l1 - mx)
        p0, p1 = e0 / (e0 + e1), e1 / (e0 + e1)
        d_lb = s[32:40]
        g = jnp.concatenate([s[8:16], s[16:32], d_lb * p0 * (1.0 - p0), -d_lb * p0 * p1, s[40:48], s[48:56]], axis=0)
        g_ref[...] = g
        d_ref[...], nm_ref[...], nv_ref[...] = _adamw_math(w, g, m_ref[...], v_ref[...])

    packed = jax.ShapeDtypeStruct((SMALL_ROWS, HEAD_DIM), F32)
    return pl.pallas_call(
        body,
        name="adamw_small",
        out_shape=[jax.ShapeDtypeStruct((1, 1), F32), packed, packed, packed, packed],
    )(sums, w, m, v)


def _pack_small(ng, bg, lbl, hgn, fg):
    return jnp.concatenate([a.reshape(-1, HEAD_DIM) for a in (ng, bg, lbl, hgn, fg)], axis=0)


def _unpack_small(p):
    return (p[0:8].reshape(1, D_MODEL), p[8:24].reshape(1, 2 * D_MODEL), p[24:40].reshape(2, HEADS, HEAD_DIM),
            p[40:48].reshape(1, HEADS, HEAD_DIM), p[48:56].reshape(D_MODEL))


def kernel(x, norm_g, w_in, b_gate, lb_logits, hg_norm_g, w_sb_proj, w_hg_proj, w_out, final_norm_g, loss_target, m_norm_g, m_w_in, m_b_gate, m_lb_logits, m_hg_norm_g, m_w_sb_proj, m_w_hg_proj, m_w_out, m_final_norm_g, v_norm_g, v_w_in, v_b_gate, v_lb_logits, v_hg_norm_g, v_w_sb_proj, v_w_hg_proj, v_w_out, v_final_norm_g):
    s_len = x.shape[1]
    w_sq = jnp.stack([w_sb_proj[0], w_hg_proj[0], w_out[0]])
    idx = jnp.stack([2 * lax.axis_index("x") + lax.axis_index("y"), lax.axis_index("c")]).astype(jnp.int32)
    w_in_b, w_sq_b = w_in[0].astype(BF16), w_sq.astype(BF16)
    h, h_t = _prenorm(x[0], norm_g)
    proj, qkv, w_all, wsq = _gather_inproj(idx, h, w_in_b, w_sq_b)
    w_all, wsq = _place_own(idx, w_in_b, w_sq_b, w_all, wsq)
    wsq = wsq.reshape(3, D_MODEL, D_MODEL)

    (g_in, g_sb, g_hg, g_out, segs, dout, loss, d_bg, d_lb, d_hgn, d_fg) = _local_grads(
        x[0], loss_target[0], proj, h_t, qkv, b_gate, lb_logits.reshape(2, D_MODEL), hg_norm_g.reshape(1, D_MODEL),
        final_norm_g.reshape(1, D_MODEL), wsq[0], wsq[1], wsq[2])

    g_sq = jnp.stack([g_sb, g_hg, g_out]).reshape(3, N_CHIPS, ROW_SHARD, D_MODEL)
    got_in, got_sq = _swap_halves(g_in, g_sq)
    s_in, s_sq = _sum_a_in(idx, g_in, got_in), _sum_a_sq(idx, g_sq, got_sq)
    grad_x, d_ng, got_in, got_sq = _dx(segs, w_all, x[0], norm_g, dout, s_in, s_sq)
    grad_in, grad_sq = _join_halves(_sum_b_in(idx, s_in, got_in), _sum_b_sq(idx, s_sq, got_sq))

    d_in, nm_in, nv_in = _adamw(w_in[0], grad_in, m_w_in[0], v_w_in[0], "adamw_in")
    flat = lambda a, b, c: jnp.concatenate([a[0], b[0], c[0]], axis=0)
    d_sq, nm_sq, nv_sq = _adamw(flat(w_sb_proj, w_hg_proj, w_out), grad_sq.reshape(3 * ROW_SHARD, D_MODEL),
                                flat(m_w_sb_proj, m_w_hg_proj, m_w_out), flat(v_w_sb_proj, v_w_hg_proj, v_w_out),
                                "adamw_sq")

    pad = jnp.zeros((8, HEAD_DIM), F32).at[0, 0].set(loss[0, 0])
    part = jnp.concatenate([pad] + [a.reshape(-1, HEAD_DIM) for a in (d_ng, d_bg, d_lb, d_hgn, d_fg)], axis=0)
    sums = _sum_small(part)
    loss_out, g_sm, d_sm, nm_sm, nv_sm = _adamw_small(
        sums, _pack_small(norm_g, b_gate, lb_logits, hg_norm_g, final_norm_g),
        _pack_small(m_norm_g, m_b_gate, m_lb_logits, m_hg_norm_g, m_final_norm_g),
        _pack_small(v_norm_g, v_b_gate, v_lb_logits, v_hg_norm_g, v_final_norm_g))

    def big(t_in, t_sq):
        sq = t_sq.reshape(3, 1, ROW_SHARD, D_MODEL)
        return t_in[None], sq[0], sq[1], sq[2]

    def order(small, in_, sb, hg, out):
        ng, bg, lbl, hgn, fg = small
        return [ng, in_, bg, lbl, hgn, sb, hg, out, fg]

    outs = [loss_out[0, 0], grad_x[None]]
    for small, (t_in, t_sq) in ((g_sm, (grad_in, grad_sq)), (d_sm, (d_in, d_sq)), (nm_sm, (nm_in, nm_sq)), (nv_sm, (nv_in, nv_sq))):
        outs += order(_unpack_small(small), *big(t_in, t_sq))
    return tuple(outs)
```

```python
import functools

import jax
import jax.numpy as jnp
from jax import lax
from jax.experimental import pallas as pl
from jax.experimental.pallas import tpu as pltpu

F32 = jnp.float32
BF16 = jnp.bfloat16

D_MODEL = 1024
HEADS = 8
HEAD_DIM = 128
IN_WIDTH = 10240
N_CHIPS = 4
W_IN_SHARD = IN_WIDTH // N_CHIPS
ROW_SHARD = D_MODEL // N_CHIPS
RMS_EPS = 1e-6

OFF_SB_Q, OFF_SB_K, OFF_SB_V, OFF_SB_Z = 0, 1024, 2048, 3072
OFF_HG_Q, OFF_HG_F, OFF_HG_I, OFF_HG_Z, OFF_GATE = 4096, 5120, 6144, 7168, 8192

SB_BLOCK = 256
SB_FWD_HEADS = 4
SB_BWD_HEADS = 2
SB_ROWS = 256
SB_DEAD = -110.0
SB_GONE = -1e30
HG_CHUNK = 32
HG_PAIR = 2 * HG_CHUNK
HG_STEP = 256
HG_MID = HG_CHUNK // 2 - 1

ADAM_LR, ADAM_B1, ADAM_B2, ADAM_EPS, ADAM_WD, ADAM_STEP = 0.001, 0.9, 0.999, 1e-08, 0.01, 10

VMEM_LIMIT = 56 * 1024 * 1024
VMEM_LIMIT_DX = 60 * 1024 * 1024

MESH = pl.DeviceIdType.MESH


def _cparams(sem, vmem=VMEM_LIMIT):
    return pltpu.CompilerParams(dimension_semantics=sem, vmem_limit_bytes=vmem)


def _dot(a, b):
    return jnp.dot(a, b, preferred_element_type=F32)


def _dot_nt(a, b):
    return lax.dot_general(a, b, (((1,), (1,)), ((), ())), preferred_element_type=F32)


def _dot_tn(a, b):
    return lax.dot_general(a, b, (((0,), (0,)), ((), ())), preferred_element_type=F32)


def _split_dot(x, tri):
    hi = x.astype(BF16)
    lo = (x - hi.astype(F32)).astype(BF16)
    both = _dot(jnp.concatenate([hi, lo], axis=0), tri)
    return both[: x.shape[0]] + both[x.shape[0] :]


def _split_dot_left(tri, x):
    hi = x.astype(BF16)
    lo = (x - hi.astype(F32)).astype(BF16)
    return _dot(tri, hi) + _dot(tri, lo)


def _sigmoid(x):
    return 1.0 / (1.0 + jnp.exp(-x))


def _prenorm(x, norm_g):
    s_len = x.shape[0]
    ts = min(1024, s_len)

    def body(x_ref, g_ref, h_ref, ht_ref):
        xv = x_ref[...]
        r = lax.rsqrt(jnp.mean(xv * xv, axis=-1, keepdims=True) + RMS_EPS)
        hv = (xv * r) * g_ref[...]
        h_ref[...] = hv.astype(BF16)
        ht_ref[...] = hv.T.astype(BF16)

    return pl.pallas_call(
        body,
        name="prenorm",
        grid=(s_len // ts,),
        in_specs=[pl.BlockSpec((ts, D_MODEL), lambda s: (s, 0)), pl.BlockSpec((1, D_MODEL), lambda s: (0, 0))],
        out_specs=[pl.BlockSpec((ts, D_MODEL), lambda s: (s, 0)), pl.BlockSpec((D_MODEL, ts), lambda s: (0, s))],
        out_shape=[jax.ShapeDtypeStruct((s_len, D_MODEL), BF16), jax.ShapeDtypeStruct((D_MODEL, s_len), BF16)],
        compiler_params=_cparams(("arbitrary",)),
    )(x, norm_g)


def _sb_scores(qb, kb, causal, tri_excl, diag):
    z = _dot_nt(qb, kb) * HEAD_DIM ** -0.5
    ls_pos = jnp.minimum(z, 0.0) - jnp.log1p(jnp.exp(-jnp.abs(z)))
    log_not = ls_pos - z
    log_not_m = jnp.where(causal, log_not, 0.0) if diag else log_not
    return ls_pos, log_not, log_not_m, _split_dot(log_not_m, tri_excl)


def _sb_weights(ls_pos, suffix, carry, causal, diag):
    surv = suffix + carry
    w = jnp.exp(ls_pos + surv)
    return surv, (jnp.where(causal, w, 0.0) if diag else w)


def _sb_specs(s_len, blk, heads):
    width = heads * HEAD_DIM

    def blk_spec(off):
        return pl.BlockSpec((blk, width), lambda h, i: (i, off // width + h))

    def head_spec(off, buffers=2):
        return pl.BlockSpec((s_len, width), lambda h, i: (0, off // width + h), pipeline_mode=pl.Buffered(buffers))

    return blk_spec, head_spec


def _head_cols(p):
    return slice(p * HEAD_DIM, (p + 1) * HEAD_DIM)


def _sb_chains(blk, heads):
    rows = min(SB_ROWS, blk)
    return [(p, a) for p in range(heads) for a in range(blk // rows)], rows


def _sb_masks(blk, rows):
    row = lax.broadcasted_iota(jnp.int32, (rows, blk), 0)
    col = lax.broadcasted_iota(jnp.int32, (rows, blk), 1)
    causal = [row + a * rows > col for a in range(blk // rows)]
    row = lax.broadcasted_iota(jnp.int32, (blk, blk), 0)
    col = lax.broadcasted_iota(jnp.int32, (blk, blk), 1)
    tri_excl = (row > col).astype(BF16)
    tri_incl = (row >= col).astype(BF16)
    return causal, tri_excl, tri_incl


def _sb_alive(st, n_chain):
    alive = functools.reduce(jnp.maximum, [st[1 + 3 * c] for c in range(n_chain)])
    return jnp.max(alive) > SB_DEAD


def _sb_fwd(qkv):
    s_len = qkv.shape[0]
    blk = min(SB_BLOCK, s_len)
    nq = s_len // blk
    chains, rows = _sb_chains(blk, SB_FWD_HEADS)

    def body(q_ref, k_ref, v_ref, o_ref, of_ref):
        i = pl.program_id(1)
        causal, tri_excl, _ = _sb_masks(blk, rows)

        def tiles(specs, st):
            pre = []
            for j, diag, _ in specs:
                start = pl.multiple_of(j * blk, blk)
                for p, a in chains:
                    kb = k_ref[pl.ds(start, blk), _head_cols(p)]
                    qb = q_ref[a * rows : (a + 1) * rows, _head_cols(p)]
                    pre.append(_sb_scores(qb, kb, causal[a], tri_excl, diag) + (v_ref[pl.ds(start, blk), _head_cols(p)],))
            for t, (j, diag, valid) in enumerate(specs):
                new = []
                for c, (p, a) in enumerate(chains):
                    carry, acc, acc_lo = st[3 * c : 3 * c + 3]
                    if valid is not None:
                        carry = jnp.where(valid, carry, SB_GONE)
                    ls_pos, _, log_not_m, suffix, vb = pre[t * len(chains) + c]
                    surv, w = _sb_weights(ls_pos, suffix, carry, causal[a], diag)
                    wb = w.astype(BF16)
                    w_lo = (w - wb.astype(F32)).astype(BF16)
                    both = _dot(jnp.concatenate([wb, w_lo], axis=0), vb)
                    new += [surv[:, 0:1] + log_not_m[:, 0:1], acc + both[:rows], acc_lo + both[rows:]]
                st = tuple(new)
            return st

        zero = jnp.zeros((rows, HEAD_DIM), F32)
        st = tiles([(i, True, None), (jnp.maximum(i - 1, 0), False, i >= 1)],
                   (jnp.zeros((rows, 1), F32), zero, zero) * len(chains))

        def more(st):
            return (st[0] < i) & _sb_alive(st, len(chains))

        def step(st):
            return (st[0] + 1,) + tiles([(i - 1 - st[0], False, None)], st[1:])

        st = lax.while_loop(more, step, (1,) + st)[1:]
        for c, (p, a) in enumerate(chains):
            o_ref[a * rows : (a + 1) * rows, _head_cols(p)] = st[3 * c + 1]
            of_ref[a * rows : (a + 1) * rows, _head_cols(p)] = st[3 * c + 1] + st[3 * c + 2]

    blk_spec, head_spec = _sb_specs(s_len, blk, SB_FWD_HEADS)
    return pl.pallas_call(
        body,
        name="sb_fwd",
        grid=(HEADS // SB_FWD_HEADS, nq),
        in_specs=[blk_spec(OFF_SB_Q), head_spec(OFF_SB_K), head_spec(OFF_SB_V)],
        out_specs=[blk_spec(0), blk_spec(0)],
        out_shape=[jax.ShapeDtypeStruct((s_len, D_MODEL), F32)] * 2,
        compiler_params=_cparams(("arbitrary", "arbitrary")),
    )(qkv, qkv, qkv)


def _sb_bwd(qkv, o_fine, d_o):
    s_len = qkv.shape[0]
    blk = min(SB_BLOCK, s_len)
    nq = s_len // blk
    scale = HEAD_DIM ** -0.5
    chains, rows = _sb_chains(blk, SB_BWD_HEADS)

    def body(q_ref, k_ref, v_ref, of_ref, do_ref, dq_ref, dk_ref, dv_ref, dk_acc, dv_acc):
        i = pl.program_id(1)

        @pl.when(i == 0)
        def _():
            dk_acc[...] = jnp.zeros_like(dk_acc)
            dv_acc[...] = jnp.zeros_like(dv_acc)

        dob = do_ref[...].astype(BF16)
        prod = dob.astype(F32) * of_ref[...]
        causal, tri_excl, tri_incl = _sb_masks(blk, rows)

        def group(x, p, a):
            return x[a * rows : (a + 1) * rows, _head_cols(p)]

        totals = [jnp.sum(group(prod, p, a), axis=-1, keepdims=True) for p, a in chains]

        def tiles(specs, st):
            pre = []
            for j, diag, _ in specs:
                start = pl.multiple_of(j * blk, blk)
                for p, a in chains:
                    kb = k_ref[pl.ds(start, blk), _head_cols(p)]
                    vb = v_ref[pl.ds(start, blk), _head_cols(p)]
                    qb, dob_c = group(q_ref, p, a), group(dob, p, a)
                    pre.append(_sb_scores(qb, kb, causal[a], tri_excl, diag) + (_dot_nt(dob_c, vb), qb, kb, dob_c))
            for t, (j, diag, valid) in enumerate(specs):
                start = pl.multiple_of(j * blk, blk)
                mids = []
                for c, (p, a) in enumerate(chains):
                    c_not = st[3 * c]
                    if valid is not None:
                        c_not = jnp.where(valid, c_not, SB_GONE)
                    ls_pos, _, _, suffix, d_w = pre[t * len(chains) + c][:5]
                    surv, w = _sb_weights(ls_pos, suffix, c_not, causal[a], diag)
                    dlw = d_w * w
                    mids.append((surv, w, dlw, _split_dot(dlw, tri_incl)))
                new = []
                dk_new = [None] * SB_BWD_HEADS
                dv_new = [None] * SB_BWD_HEADS
                for c, (p, a) in enumerate(chains):
                    c_dlw, dq = st[3 * c + 1 : 3 * c + 3]
                    ls_pos, log_not, log_not_m, _, _, qb, kb, dob_c = pre[t * len(chains) + c]
                    surv, w, dlw, suffix = mids[c]
                    d_not = totals[c] - c_dlw - suffix
                    dz = (dlw * jnp.exp(log_not) - d_not * jnp.exp(ls_pos)) * scale
                    if diag:
                        dz = jnp.where(causal[a], dz, 0.0)
                    if valid is not None:
                        dz = jnp.where(valid, dz, 0.0)
                    dzb = dz.astype(BF16)
                    dk_c, dv_c = _dot_tn(dzb, qb), _dot_tn(w.astype(BF16), dob_c)
                    dk_new[p] = dk_c if dk_new[p] is None else dk_new[p] + dk_c
                    dv_new[p] = dv_c if dv_new[p] is None else dv_new[p] + dv_c
                    new += [surv[:, 0:1] + log_not_m[:, 0:1], c_dlw + suffix[:, 0:1], dq + _dot(dzb, kb)]
                for p in range(SB_BWD_HEADS):
                    dk_acc[pl.ds(start, blk), _head_cols(p)] += dk_new[p]
                    dv_acc[pl.ds(start, blk), _head_cols(p)] += dv_new[p]
                st = tuple(new)
            return st

        zcol = jnp.zeros((rows, 1), F32)
        st = tiles([(i, True, None), (jnp.maximum(i - 1, 0), False, i >= 1)],
                   (zcol, zcol, jnp.zeros((rows, HEAD_DIM), F32)) * len(chains))

        def more(st):
            return (st[0] < i) & _sb_alive(st, len(chains))

        def step(st):
            return (st[0] + 1,) + tiles([(i - 1 - st[0], False, None)], st[1:])

        st = lax.while_loop(more, step, (1,) + st)[1:]
        for c, (p, a) in enumerate(chains):
            dq_ref[a * rows : (a + 1) * rows, _head_cols(p)] = st[3 * c + 2].astype(BF16)

        @pl.when(i == nq - 1)
        def _():
            dk_ref[...] = dk_acc[...].astype(BF16)
            dv_ref[...] = dv_acc[...].astype(BF16)

    blk_spec, head_spec = _sb_specs(s_len, blk, SB_BWD_HEADS)
    width = SB_BWD_HEADS * HEAD_DIM
    return pl.pallas_call(
        body,
        name="sb_bwd",
        grid=(HEADS // SB_BWD_HEADS, nq),
        in_specs=[blk_spec(OFF_SB_Q), head_spec(OFF_SB_K, 1), head_spec(OFF_SB_V, 1), blk_spec(0), blk_spec(0)],
        out_specs=[blk_spec(0), head_spec(0), head_spec(0)],
        out_shape=[jax.ShapeDtypeStruct((s_len, D_MODEL), BF16)] * 3,
        scratch_shapes=[pltpu.VMEM((s_len, width), F32), pltpu.VMEM((s_len, width), F32)],
        compiler_params=_cparams(("arbitrary", "arbitrary")),
    )(qkv, qkv, qkv, o_fine, d_o)


def _hg_lower_bound(lbl_ref):
    l0 = lbl_ref[0:1, :]
    l1 = lbl_ref[1:2, :]
    mx = jnp.maximum(l0, l1)
    e0 = jnp.exp(l0 - mx)
    e1 = jnp.exp(l1 - mx)
    return e0 / (e0 + e1)


def _hg_gates(hq, hf, lb):
    sig_f = _sigmoid(hf)
    f = lb + (1.0 - lb) * sig_f
    g = jnp.log(f)
    kk = 1.0 - f
    sig_q = _sigmoid(hq)
    qq = hq * sig_q
    return qq, kk, g, f, sig_f, sig_q


def _period_bcast(x, r, rows, period):
    w = x.shape[-1]
    x3 = x.reshape(rows // period, period, w)
    return jnp.broadcast_to(x3[:, r : r + 1, :], x3.shape).reshape(rows, w)


def _blockdiag(rows, kind):
    row = lax.broadcasted_iota(jnp.int32, (rows, rows), 0)
    col = lax.broadcasted_iota(jnp.int32, (rows, rows), 1)
    if kind in ("next", "prev"):
        first, second = (row, col) if kind == "next" else (col, row)
        keep = ((row // HG_PAIR) == (col // HG_PAIR)) & (first % HG_PAIR < HG_CHUNK) & (second % HG_PAIR >= HG_CHUNK)
    else:
        keep = (row // HG_CHUNK) == (col // HG_CHUNK)
        if kind == "lower":
            keep = keep & (row >= col)
        elif kind == "upper":
            keep = keep & (row <= col)
    return jnp.where(keep, 1.0, 0.0).astype(BF16)


def _hg_operands(hq, hf, lb, rows):
    qq, kk, g, f, sig_f, sig_q = _hg_gates(hq, hf, lb)
    cum = _split_dot_left(_blockdiag(rows, "lower"), g)
    mid = _period_bcast(cum, HG_MID, rows, HG_CHUNK)
    last = _period_bcast(cum, HG_CHUNK - 1, rows, HG_CHUNK)
    last0 = _period_bcast(cum, HG_CHUNK - 1, rows, HG_PAIR)
    last1 = _period_bcast(cum, HG_PAIR - 1, rows, HG_PAIR)
    second = (lax.broadcasted_iota(jnp.int32, cum.shape, 0) % HG_PAIR) >= HG_CHUNK
    e = dict(qm=jnp.exp(cum - mid), km=jnp.exp(mid - cum), qd=jnp.exp(cum), kl=jnp.exp(last - cum),
             q_in=jnp.where(second, jnp.exp(last0), 1.0), k_out=jnp.where(second, 1.0, jnp.exp(last1)),
             pair=jnp.exp(last0 + last1))
    v = dict(qm=qq * e["qm"], km=kk * e["km"], qd=qq * e["qd"], kl=kk * e["kl"])
    v["qp"] = v["qd"] * e["q_in"]
    v["kp"] = v["kl"] * e["k_out"]
    return v, e, second, (f, sig_f, sig_q)


def _hg_store_operands(v, second, hi, refs):
    zero = jnp.zeros_like(v["qm"])
    q_cat, k_cat, qp_b, kp_b, v_b = refs
    q_cat[:, 0:D_MODEL] = jnp.where(second, zero, v["qm"]).astype(BF16)
    q_cat[:, D_MODEL : 2 * D_MODEL] = jnp.where(second, v["qm"], zero).astype(BF16)
    q_cat[:, 2 * D_MODEL :] = jnp.where(second, v["qd"], zero).astype(BF16)
    k_cat[:, 0:D_MODEL] = jnp.where(second, zero, v["km"]).astype(BF16)
    k_cat[:, D_MODEL : 2 * D_MODEL] = jnp.where(second, v["km"], zero).astype(BF16)
    k_cat[:, 2 * D_MODEL :] = jnp.where(second, zero, v["kl"]).astype(BF16)
    qp_b[...] = v["qp"].astype(BF16)
    kp_b[...] = v["kp"].astype(BF16)
    v_b[...] = hi.astype(BF16)


def _hg_pair_operands(cat, r0, c0):
    return jnp.concatenate([cat[r0 : r0 + HG_PAIR, g * D_MODEL + c0 : g * D_MODEL + c0 + HEAD_DIM] for g in range(3)], axis=1)


def _hg_fwd(proj, lbl):
    s_len = proj.shape[0]
    rows = min(HG_STEP, s_len)
    n_pairs = rows // HG_PAIR

    def body(hq_ref, hf_ref, hi_ref, lbl_ref, o_ref, st_ref, state, q_cat, k_cat, qp_b, kp_b, v_b):
        @pl.when(pl.program_id(0) == 0)
        def _():
            state[...] = jnp.zeros_like(state)

        v, e, second, _ = _hg_operands(hq_ref[...], hf_ref[...], _hg_lower_bound(lbl_ref), rows)
        _hg_store_operands(v, second, hi_ref[...], (q_cat, k_cat, qp_b, kp_b, v_b))
        e_pair = e["pair"]
        row = lax.broadcasted_iota(jnp.int32, (HG_PAIR, HG_PAIR), 0)
        col = lax.broadcasted_iota(jnp.int32, (HG_PAIR, HG_PAIR), 1)
        causal = row >= col

        for u in range(n_pairs):
            r0 = u * HG_PAIR
            sls = [(slice(r0, r0 + HG_PAIR), slice(h * HEAD_DIM, (h + 1) * HEAD_DIM)) for h in range(HEADS)]
            a_s = [jnp.where(causal, _dot_nt(_hg_pair_operands(q_cat, r0, h * HEAD_DIM),
                                             _hg_pair_operands(k_cat, r0, h * HEAD_DIM)), 0.0).astype(BF16)
                   for h in range(HEADS)]
            st_s = [state[h] for h in range(HEADS)]
            for h, sl in enumerate(sls):
                st_ref[u, h] = st_s[h]
                state[h] = st_s[h] * e_pair[r0 : r0 + 1, sl[1]] + _dot_tn(v_b[sl], kp_b[sl])
            for h, sl in enumerate(sls):
                o_ref[sl] = _dot(a_s[h], v_b[sl]) + _dot_nt(qp_b[sl], st_s[h].astype(BF16))

    def col_spec(off):
        return pl.BlockSpec((rows, D_MODEL), lambda s: (s, off // D_MODEL))

    bf_tile = pltpu.VMEM((rows, D_MODEL), BF16)
    bf_cat = pltpu.VMEM((rows, 3 * D_MODEL), BF16)
    scratch = [pltpu.VMEM((HEADS, HEAD_DIM, HEAD_DIM), F32), bf_cat, bf_cat, bf_tile, bf_tile, bf_tile]
    return pl.pallas_call(
        body,
        name="hg_fwd",
        grid=(s_len // rows,),
        in_specs=[col_spec(OFF_HG_Q), col_spec(OFF_HG_F), col_spec(OFF_HG_I), pl.BlockSpec((2, D_MODEL), lambda s: (0, 0))],
        out_specs=[
            pl.BlockSpec((rows, D_MODEL), lambda s: (s, 0)),
            pl.BlockSpec((n_pairs, HEADS, HEAD_DIM, HEAD_DIM), lambda s: (s, 0, 0, 0)),
        ],
        out_shape=[
            jax.ShapeDtypeStruct((s_len, D_MODEL), F32),
            jax.ShapeDtypeStruct((s_len // HG_PAIR, HEADS, HEAD_DIM, HEAD_DIM), F32),
        ],
        scratch_shapes=scratch,
        compiler_params=_cparams(("arbitrary",)),
    )(proj, proj, proj, lbl)


def _hg_bwd(proj, lbl, states, d_o):
    s_len = proj.shape[0]
    rows = min(HG_STEP, s_len)
    n_pairs = rows // HG_PAIR
    n_steps = s_len // rows

    def body(hq_ref, hf_ref, hi_ref, lbl_ref, st_ref, do_ref, dp_ref, dlb_ref,
             dstate, q_cat, k_cat, qp_b, kp_b, v_b, do_b, d_qcat, d_kcat, d_qp, d_kp, d_v, d_pair):
        @pl.when(pl.program_id(0) == 0)
        def _():
            dstate[...] = jnp.zeros_like(dstate)
            dlb_ref[...] = jnp.zeros_like(dlb_ref)

        lb = _hg_lower_bound(lbl_ref)
        hq = hq_ref[...]
        v, e, second, (f, sig_f, sig_q) = _hg_operands(hq, hf_ref[...], lb, rows)
        _hg_store_operands(v, second, hi_ref[...], (q_cat, k_cat, qp_b, kp_b, v_b))
        do_b[...] = do_ref[...].astype(BF16)
        e_pair = e["pair"]
        row = lax.broadcasted_iota(jnp.int32, (HG_PAIR, HG_PAIR), 0)
        col = lax.broadcasted_iota(jnp.int32, (HG_PAIR, HG_PAIR), 1)
        causal = row >= col

        for u in reversed(range(n_pairs)):
            r0 = u * HG_PAIR
            sls = [(slice(r0, r0 + HG_PAIR), slice(h * HEAD_DIM, (h + 1) * HEAD_DIM)) for h in range(HEADS)]
            ops = [(_hg_pair_operands(q_cat, r0, h * HEAD_DIM), _hg_pair_operands(k_cat, r0, h * HEAD_DIM))
                   for h in range(HEADS)]
            a_s = [jnp.where(causal, _dot_nt(lhs, rhs), 0.0).astype(BF16) for lhs, rhs in ops]
            da_s = [jnp.where(causal, _dot_nt(do_b[sl], v_b[sl]), 0.0).astype(BF16) for sl in sls]
            st0_s = [st_ref[u, h] for h in range(HEADS)]
            ds1_s = [dstate[h] for h in range(HEADS)]
            ds1b_s = [ds1.astype(BF16) for ds1 in ds1_s]
            for h, sl in enumerate(sls):
                decay = e_pair[r0 : r0 + 1, sl[1]]
                d_pair[u : u + 1, sl[1]] = decay * jnp.sum(ds1_s[h] * st0_s[h], axis=0, keepdims=True)
                dstate[h] = ds1_s[h] * decay + _dot_tn(do_b[sl], qp_b[sl])
            for h, sl in enumerate(sls):
                d_qp[sl] = _dot(do_b[sl], st0_s[h].astype(BF16))
                d_kp[sl] = _dot(v_b[sl], ds1b_s[h])
            for h, sl in enumerate(sls):
                d_v[sl] = _dot_tn(a_s[h], do_b[sl]) + _dot_nt(kp_b[sl], ds1b_s[h])
            for h, sl in enumerate(sls):
                d_lhs = _dot(da_s[h], ops[h][1])
                d_rhs = _dot_tn(da_s[h], ops[h][0])
                for g in range(3):
                    gsl = (sl[0], slice(g * D_MODEL + h * HEAD_DIM, g * D_MODEL + (h + 1) * HEAD_DIM))
                    d_qcat[gsl] = d_lhs[:, g * HEAD_DIM : (g + 1) * HEAD_DIM]
                    d_kcat[gsl] = d_rhs[:, g * HEAD_DIM : (g + 1) * HEAD_DIM]

        zero = jnp.zeros_like(hq)
        dqm = jnp.where(second, d_qcat[:, D_MODEL : 2 * D_MODEL], d_qcat[:, 0:D_MODEL])
        dkm = jnp.where(second, d_kcat[:, D_MODEL : 2 * D_MODEL], d_kcat[:, 0:D_MODEL])
        dqp, dkp = d_qp[...], d_kp[...]
        dqd = dqp * e["q_in"] + jnp.where(second, d_qcat[:, 2 * D_MODEL :], zero)
        dkl = dkp * e["k_out"] + jnp.where(second, zero, d_kcat[:, 2 * D_MODEL :])
        dq = dqm * e["qm"] + dqd * e["qd"]
        dk = dkm * e["km"] + dkl * e["kl"]
        t_kl = dkl * v["kl"]
        dcum = dqm * v["qm"] - dkm * v["km"] + dqd * v["qd"] - t_kl
        dp = d_pair[...]
        dp_b = jnp.broadcast_to(dp[:, None, :], (n_pairs, HG_PAIR, D_MODEL)).reshape(rows, D_MODEL)
        dg = (_split_dot_left(_blockdiag(rows, "upper"), dcum) + _split_dot_left(_blockdiag(rows, "all"), t_kl)
              + _split_dot_left(_blockdiag(rows, "next"), dqp * v["qp"])
              + _split_dot_left(_blockdiag(rows, "prev"), dkp * v["kp"]) + dp_b)
        df = dg / f - dk
        one_m = 1.0 - sig_f
        dp_ref[:, 0:D_MODEL] = (dq * (sig_q * (1.0 + hq * (1.0 - sig_q)))).astype(BF16)
        dp_ref[:, D_MODEL : 2 * D_MODEL] = (df * (1.0 - lb) * sig_f * one_m).astype(BF16)
        dp_ref[:, 2 * D_MODEL : 3 * D_MODEL] = d_v[...].astype(BF16)
        dlb_ref[...] += jnp.sum(df * one_m, axis=0, keepdims=True)

    def col_spec(off):
        return pl.BlockSpec((rows, D_MODEL), lambda s: (n_steps - 1 - s, off // D_MODEL))

    f32_tile = pltpu.VMEM((rows, D_MODEL), F32)
    f32_cat = pltpu.VMEM((rows, 3 * D_MODEL), F32)
    bf_tile = pltpu.VMEM((rows, D_MODEL), BF16)
    bf_cat = pltpu.VMEM((rows, 3 * D_MODEL), BF16)
    scratch = [pltpu.VMEM((HEADS, HEAD_DIM, HEAD_DIM), F32), bf_cat, bf_cat, bf_tile, bf_tile, bf_tile, bf_tile,
               f32_cat, f32_cat, f32_tile, f32_tile, f32_tile, pltpu.VMEM((n_pairs, D_MODEL), F32)]
    return pl.pallas_call(
        body,
        name="hg_bwd",
        grid=(n_steps,),
        in_specs=[
            col_spec(OFF_HG_Q), col_spec(OFF_HG_F), col_spec(OFF_HG_I),
            pl.BlockSpec((2, D_MODEL), lambda s: (0, 0)),
            pl.BlockSpec((n_pairs, HEADS, HEAD_DIM, HEAD_DIM), lambda s: (n_steps - 1 - s, 0, 0, 0)),
            pl.BlockSpec((rows, D_MODEL), lambda s: (n_steps - 1 - s, 0)),
        ],
        out_specs=[
            pl.BlockSpec((rows, 3 * D_MODEL), lambda s: (n_steps - 1 - s, 0)),
            pl.BlockSpec((1, D_MODEL), lambda s: (0, 0)),
        ],
        out_shape=[
            jax.ShapeDtypeStruct((s_len, 3 * D_MODEL), BF16),
            jax.ShapeDtypeStruct((1, D_MODEL), F32),
        ],
        scratch_shapes=scratch,
        compiler_params=_cparams(("arbitrary",)),
    )(proj, proj, proj, lbl, states, d_o)


def _mid(proj, sb_o, hg_o, x, target, b_gate, hg_gain, final_g, w_sb, w_hg, w_out):
    s_len = proj.shape[0]
    ts = min(256, s_len)
    inv_d = 1.0 / D_MODEL

    def body(zsb_ref, hz_ref, gl_ref, sbo_ref, hgo_ref, x_ref, tgt_ref, bg_ref, hgn_ref, fg_ref,
             wsb_ref, whg_ref, wout_ref,
             dout_ref, dsbo_ref, dhgo_ref, dmid_ref,
             asb_ref, dusb_ref, ahg_ref, duhg_ref, y_ref, doutb_ref,
             loss_ref, dfg_ref, dbg_ref, dhgn_ref):
        @pl.when(pl.program_id(0) == 0)
        def _():
            loss_ref[...] = jnp.zeros_like(loss_ref)
            dfg_ref[...] = jnp.zeros_like(dfg_ref)
            dbg_ref[...] = jnp.zeros_like(dbg_ref)
            dhgn_ref[...] = jnp.zeros_like(dhgn_ref)

        z_sb = zsb_ref[...]
        sb_o = sbo_ref[...]
        sig_zsb = _sigmoid(z_sb)
        silu_zsb = z_sb * sig_zsb
        a_sb_f = sb_o * silu_zsb
        a_sb = a_sb_f.astype(BF16)
        u_sb = _dot(a_sb, wsb_ref[...])

        hg_o = hgo_ref[...]
        gain = hgn_ref[...]
        r_parts, yn_parts = [], []
        for h in range(HEADS):
            oh = hg_o[:, h * HEAD_DIM : (h + 1) * HEAD_DIM]
            r = lax.rsqrt(jnp.mean(oh * oh, axis=-1, keepdims=True) + RMS_EPS)
            r_parts.append(jnp.broadcast_to(r, oh.shape))
            yn_parts.append(oh * r)
        r_hg = jnp.concatenate(r_parts, axis=-1)
        yn_hg = jnp.concatenate(yn_parts, axis=-1)
        hn = yn_hg * gain
        hz = hz_ref[...]
        sig_hz = _sigmoid(hz)
        silu_hz = hz * sig_hz
        a_hg_f = hn * silu_hz
        a_hg = a_hg_f.astype(BF16)
        u_hg = _dot(a_hg, whg_ref[...])

        gates = _sigmoid(gl_ref[...] + bg_ref[...])
        g_sb = gates[:, 0:D_MODEL]
        g_hg = gates[:, D_MODEL:]
        y_f = g_sb * u_sb + g_hg * u_hg
        y = y_f.astype(BF16)
        out = x_ref[...] + _dot(y, wout_ref[...])
        r2 = lax.rsqrt(jnp.mean(out * out, axis=-1, keepdims=True) + RMS_EPS)
        yn = out * r2
        fg = fg_ref[...]
        diff = yn * fg - tgt_ref[...]
        loss_ref[...] += 0.5 * inv_d * jnp.sum(diff * diff)

        dyf = diff * inv_d
        dfg_ref[...] += jnp.sum(dyf * yn, axis=0, keepdims=True)
        dyn = dyf * fg
        dout = r2 * (dyn - yn * jnp.mean(dyn * yn, axis=-1, keepdims=True))
        dout_ref[...] = dout
        doutb = dout.astype(BF16)
        doutb_ref[...] = doutb
        dy = _dot_nt(doutb, wout_ref[...])
        du_sb = (dy * g_sb).astype(BF16)
        du_hg = (dy * g_hg).astype(BF16)
        dgl_sb = dy * u_sb * g_sb * (1.0 - g_sb)
        dgl_hg = dy * u_hg * g_hg * (1.0 - g_hg)
        dmid_ref[:, 2 * D_MODEL : 3 * D_MODEL] = dgl_sb.astype(BF16)
        dmid_ref[:, 3 * D_MODEL :] = dgl_hg.astype(BF16)
        dbg_ref[:, 0:D_MODEL] += jnp.sum(dgl_sb, axis=0, keepdims=True)
        dbg_ref[:, D_MODEL:] += jnp.sum(dgl_hg, axis=0, keepdims=True)

        da_sb = _dot_nt(du_sb, wsb_ref[...])
        dsbo_ref[...] = (da_sb * silu_zsb).astype(BF16)
        dmid_ref[:, 0:D_MODEL] = (da_sb * sb_o * (sig_zsb * (1.0 + z_sb * (1.0 - sig_zsb)))).astype(BF16)

        da_hg = _dot_nt(du_hg, whg_ref[...])
        dhn = da_hg * silu_hz
        dmid_ref[:, D_MODEL : 2 * D_MODEL] = (da_hg * hn * (sig_hz * (1.0 + hz * (1.0 - sig_hz)))).astype(BF16)
        dhgn_ref[...] += jnp.sum(dhn * yn_hg, axis=0, keepdims=True)
        dyn_hg = dhn * gain
        prod = dyn_hg * yn_hg
        m_parts = []
        for h in range(HEADS):
            ph = prod[:, h * HEAD_DIM : (h + 1) * HEAD_DIM]
            m_parts.append(jnp.broadcast_to(jnp.mean(ph, axis=-1, keepdims=True), ph.shape))
        dhgo_ref[...] = (r_hg * (dyn_hg - yn_hg * jnp.concatenate(m_parts, axis=-1))).astype(BF16)

        asb_ref[...] = a_sb_f.T.astype(BF16)
        dusb_ref[...] = du_sb
        ahg_ref[...] = a_hg_f.T.astype(BF16)
        duhg_ref[...] = du_hg
        y_ref[...] = y_f.T.astype(BF16)

    def tile(width, off=0):
        return pl.BlockSpec((ts, width), lambda s: (s, off // width))

    def across():
        return pl.BlockSpec((D_MODEL, ts), lambda s: (0, s))

    def whole(shape):
        return pl.BlockSpec(shape, lambda s: (0,) * len(shape))

    def weight():
        return pl.BlockSpec((D_MODEL, D_MODEL), lambda s: (0, 0), pipeline_mode=pl.Buffered(1))

    f32_act = jax.ShapeDtypeStruct((s_len, D_MODEL), F32)
    bf_act = jax.ShapeDtypeStruct((s_len, D_MODEL), BF16)
    bf_act_t = jax.ShapeDtypeStruct((D_MODEL, s_len), BF16)
    return pl.pallas_call(
        body,
        name="mid",
        grid=(s_len // ts,),
        in_specs=[
            tile(D_MODEL, OFF_SB_Z), tile(D_MODEL, OFF_HG_Z), tile(2 * D_MODEL, OFF_GATE),
            tile(D_MODEL), tile(D_MODEL), tile(D_MODEL), tile(D_MODEL),
            whole((1, 2 * D_MODEL)), whole((1, D_MODEL)), whole((1, D_MODEL)),
            weight(), weight(), weight(),
        ],
        out_specs=[
            tile(D_MODEL), tile(D_MODEL), tile(D_MODEL), tile(4 * D_MODEL),
            across(), tile(D_MODEL), across(), tile(D_MODEL), across(), tile(D_MODEL),
            whole((1, 1)), whole((1, D_MODEL)), whole((1, 2 * D_MODEL)), whole((1, D_MODEL)),
        ],
        out_shape=[
            f32_act, bf_act, bf_act, jax.ShapeDtypeStruct((s_len, 4 * D_MODEL), BF16),
            bf_act_t, bf_act, bf_act_t, bf_act, bf_act_t, bf_act,
            jax.ShapeDtypeStruct((1, 1), F32), jax.ShapeDtypeStruct((1, D_MODEL), F32),
            jax.ShapeDtypeStruct((1, 2 * D_MODEL), F32), jax.ShapeDtypeStruct((1, D_MODEL), F32),
        ],
        compiler_params=_cparams(("arbitrary",)),
    )(proj, proj, proj, sb_o, hg_o, x, target, b_gate, hg_gain, final_g, w_sb, w_hg, w_out)


def _grad_square(a_t, b, name):
    s_len = b.shape[0]
    tk = min(1024, s_len)

    def body(a_ref, b_ref, o_ref):
        @pl.when(pl.program_id(0) == 0)
        def _():
            o_ref[...] = jnp.zeros_like(o_ref)

        o_ref[...] += _dot(a_ref[...], b_ref[...])

    return pl.pallas_call(
        body,
        name=name,
        grid=(s_len // tk,),
        in_specs=[pl.BlockSpec((D_MODEL, tk), lambda k: (0, k)), pl.BlockSpec((tk, D_MODEL), lambda k: (k, 0))],
        out_specs=pl.BlockSpec((D_MODEL, D_MODEL), lambda k: (0, 0)),
        out_shape=jax.ShapeDtypeStruct((D_MODEL, D_MODEL), F32),
        compiler_params=_cparams(("arbitrary",)),
    )(a_t, b)


SEG_WIDTHS = (1024, 1024, 1024, 4096, 3072)
SEG_TILE = 1024
SEG_BOUNDS = (0, 1, 2, 3, 7, 10)


def _w_in_tile(k):
    return jnp.where(k < 4, k, jnp.where(k < 7, k + 3, k - 3))


def _grad_w_in(h_t, segs):
    m, s_len = h_t.shape
    tk = min(1024, s_len)
    tn = SEG_TILE
    nk = s_len // tk
    bounds = SEG_BOUNDS

    def body(a_ref, *refs):
        seg_refs, o_ref = refs[:-1], refs[-1]
        j = pl.program_id(0)

        @pl.when(pl.program_id(1) == 0)
        def _():
            o_ref[...] = jnp.zeros_like(o_ref)

        for i, ref in enumerate(seg_refs):
            @pl.when((j >= bounds[i]) & (j < bounds[i + 1]))
            def _(ref=ref):
                o_ref[...] += _dot(a_ref[...], ref[...])

    def seg_spec(lo, hi):
        def index(j, k):
            return (jnp.where(j < lo, 0, jnp.where(j >= hi, nk - 1, k)), jnp.clip(j - lo, 0, hi - lo - 1))
        return pl.BlockSpec((tk, tn), index)

    return pl.pallas_call(
        body,
        name="grad_w_in",
        grid=(IN_WIDTH // tn, nk),
        in_specs=[pl.BlockSpec((m, tk), lambda j, k: (0, k))]
        + [seg_spec(bounds[i], bounds[i + 1]) for i in range(len(SEG_WIDTHS))],
        out_specs=pl.BlockSpec((m, tn), lambda j, k: (0, _w_in_tile(j))),
        out_shape=jax.ShapeDtypeStruct((m, IN_WIDTH), F32),
        compiler_params=_cparams(("arbitrary", "arbitrary")),
    )(h_t, *segs)


EXCHANGE_IN_PIECES = 8
EXCHANGE_PIECES = EXCHANGE_IN_PIECES + 3


def _exchange_copies(sin_ref, ssq_ref, got_in, got_sq, send_sems, recv_sems):
    _, _, c, chips = _position()
    rows = HALF_IN // EXCHANGE_IN_PIECES
    copies = []
    for k, (px, py) in enumerate(chips):
        chip = 2 * px + py
        for p in range(EXCHANGE_PIECES):
            if p < EXCHANGE_IN_PIECES:
                src, dst = sin_ref.at[chip, pl.ds(p * rows, rows), :], got_in.at[k, pl.ds(p * rows, rows), :]
            else:
                src, dst = ssq_ref.at[p - EXCHANGE_IN_PIECES, chip], got_sq.at[k, p - EXCHANGE_IN_PIECES]
            copies.append(_remote(src, dst, send_sems.at[k, p], recv_sems.at[k, p], (px, py, c)))
    return copies


def _dx(segs, w_all, x, norm_g, dout, s_in, s_sq):
    s_len = x.shape[0]
    ts = min(1024, s_len)
    tk = SEG_TILE
    nk = IN_WIDTH // tk
    ns = s_len // ts
    bounds = SEG_BOUNDS
    n_seg = len(SEG_WIDTHS)

    def body(*refs):
        seg_refs = refs[:n_seg]
        w_ref, x_ref, g_ref, dout_ref, sin_ref, ssq_ref, gx_ref, dg_ref, got_in, got_sq, acc, send_sems, recv_sems = refs[n_seg:]
        s, k = pl.program_id(0), pl.program_id(1)

        @pl.when((s == 0) & (k == 0))
        def _():
            dg_ref[...] = jnp.zeros_like(dg_ref)
            for cp in _exchange_copies(sin_ref, ssq_ref, got_in, got_sq, send_sems, recv_sems):
                cp.start()

        @pl.when(k == 0)
        def _():
            acc[...] = jnp.zeros_like(acc)

        for i, ref in enumerate(seg_refs):
            @pl.when((k >= bounds[i]) & (k < bounds[i + 1]))
            def _(ref=ref):
                acc[...] += _dot_nt(ref[...], w_ref[...])

        @pl.when(k == nk - 1)
        def _():
            dh = acc[...]
            xv = x_ref[...]
            r = lax.rsqrt(jnp.mean(xv * xv, axis=-1, keepdims=True) + RMS_EPS)
            xn = xv * r
            dg_ref[...] += jnp.sum(dh * xn, axis=0, keepdims=True)
            dxn = dh * g_ref[...]
            gx_ref[...] = r * (dxn - xn * jnp.mean(dxn * xn, axis=-1, keepdims=True)) + dout_ref[...]

        @pl.when((s == ns - 1) & (k == nk - 1))
        def _():
            for cp in _exchange_copies(sin_ref, ssq_ref, got_in, got_sq, send_sems, recv_sems):
                cp.wait()

    def seg_spec(lo, hi):
        return pl.BlockSpec((ts, tk), lambda s, k: (s, jnp.clip(k - lo, 0, hi - lo - 1)))

    row_tile = pl.BlockSpec((ts, D_MODEL), lambda s, k: (s, 0))
    vec = pl.BlockSpec((1, D_MODEL), lambda s, k: (0, 0))
    return pl.pallas_call(
        body,
        name="dx",
        grid=(ns, nk),
        in_specs=[seg_spec(bounds[i], bounds[i + 1]) for i in range(n_seg)] + [
            pl.BlockSpec((D_MODEL, tk), lambda s, k: (0, _w_in_tile(k))),
            row_tile, vec, row_tile, ANY, ANY,
        ],
        out_specs=[row_tile, vec, ANY, ANY],
        out_shape=[jax.ShapeDtypeStruct((s_len, D_MODEL), F32), jax.ShapeDtypeStruct((1, D_MODEL), F32),
                   jax.ShapeDtypeStruct((3, HALF_IN, W_IN_SHARD), WIRE),
                   jax.ShapeDtypeStruct((3, 3, HALF_SQ, D_MODEL), WIRE)],
        scratch_shapes=[pltpu.VMEM((ts, D_MODEL), F32),
                        pltpu.SemaphoreType.DMA((3, EXCHANGE_PIECES)), pltpu.SemaphoreType.DMA((3, EXCHANGE_PIECES))],
        compiler_params=_cparams(("arbitrary", "arbitrary"), vmem=VMEM_LIMIT_DX),
    )(*segs, w_all, x, norm_g, dout, s_in, s_sq)


def _local_grads(x, target, proj, h_t, qkv, b_gate, lbl, hg_gain, final_g, w_sb, w_hg, w_out):
    sb_o, sb_o_fine = _sb_fwd(qkv)
    hg_o, states = _hg_fwd(proj, lbl)
    (dout, d_sbo, d_hgo, d_mid, a_sb, du_sb, a_hg, du_hg, y, doutb,
     loss, d_fg, d_bg, d_hgn) = _mid(proj, sb_o, hg_o, x, target, b_gate, hg_gain, final_g, w_sb, w_hg, w_out)
    g_w_sb = _grad_square(a_sb, du_sb, "grad_w_sb")
    g_w_hg = _grad_square(a_hg, du_hg, "grad_w_hg")
    g_w_out = _grad_square(y, doutb, "grad_w_out")
    d_q, d_k, d_v = _sb_bwd(qkv, sb_o_fine, d_sbo)
    d_hg, d_lb = _hg_bwd(proj, lbl, states, d_hgo)
    segs = (d_q, d_k, d_v, d_mid, d_hg)
    g_w_in = _grad_w_in(h_t, segs)
    return g_w_in, g_w_sb, g_w_hg, g_w_out, segs, dout, loss, d_bg, d_lb, d_hgn, d_fg


ANY = pl.BlockSpec(memory_space=pl.ANY)
WIRE = BF16
HALF_IN = D_MODEL // 2
HALF_SQ = ROW_SHARD // 2


def _position():
    x, y, c = lax.axis_index("x"), lax.axis_index("y"), lax.axis_index("c")
    chips = [(1 - x, y), (x, 1 - y), (1 - x, 1 - y)]
    return x, y, c, chips


def _remote(src, dst, send_sem, recv_sem, to):
    return pltpu.make_async_remote_copy(src_ref=src, dst_ref=dst, send_sem=send_sem, recv_sem=recv_sem,
                                        device_id=to, device_id_type=MESH)


PROJ_TILE = 1280
F32_FROM_TILE = 2
BF16_TO_TILE = 2
W_LOAD_PIECES = 8


def _gather_inproj(idx, h, w_in_b, w_sq_b):
    s_len = h.shape[0]
    ts = min(1024, s_len)
    ns = s_len // ts
    per = W_IN_SHARD // PROJ_TILE
    n_in = 4
    n_piece = n_in + 3
    rows = HALF_IN // n_in

    def chip_at(r, me):
        return me ^ jnp.where(r == 1, 2, jnp.where(r == 2, 1, jnp.where(r == 3, 3, 0)))

    def body(idx_ref, h_ref, win_ref, wsqb_ref, proj_ref, qkv_ref, wall_ref, wsq_ref, wbuf, send_sems, recv_sems, w_sems):
        r, t, s = pl.program_id(0), pl.program_id(1), pl.program_id(2)
        x, y, c, chips = _position()
        me = 2 * x + y
        sibling = (x, y, 1 - c)
        first = (t == 0) & (s == 0)

        def src_piece(p):
            if p < n_in:
                return win_ref.at[pl.ds(c * HALF_IN + p * rows, rows), :]
            return wsqb_ref.at[p - n_in, pl.ds(c * HALF_SQ, HALF_SQ), :]

        def piece(p, chip, core):
            if p < n_in:
                cols = pl.ds(pl.multiple_of(chip * W_IN_SHARD, W_IN_SHARD), W_IN_SHARD)
                return wall_ref.at[pl.ds(core * HALF_IN + p * rows, rows), cols]
            return wsq_ref.at[p - n_in, chip, pl.ds(core * HALF_SQ, HALF_SQ), :]

        def send(k, p):
            px, py = chips[k]
            return _remote(src_piece(p), piece(p, me, c), send_sems.at[k, p], recv_sems.at[k, p], (px, py, c))

        def forward(k, p, core):
            px, py = chips[k]
            got = piece(p, 2 * px + py, core)
            return _remote(got, got, send_sems.at[3 + k, p], recv_sems.at[3 + k, p], sibling)

        @pl.when((r == 0) & first)
        def _():
            for k in range(2):
                for p in range(n_piece):
                    send(k, p).start()

        for k in range(3):
            @pl.when((r == k + 1) & first)
            def _(k=k):
                px, py = chips[k]
                for p in range(n_piece):
                    got = piece(p, 2 * px + py, c)
                    _remote(got, got, send_sems.at[k, p], recv_sems.at[k, p], (px, py, c)).wait_recv()
                    forward(k, p, c).start()
                if k == 0:
                    for p in range(n_piece):
                        send(2, p).start()
                for p in range(n_piece):
                    forward(k, p, 1 - c).wait_recv()

        def tile_loads(slot, own):
            col = slot * PROJ_TILE
            if not own:
                col = pl.multiple_of(chip_at(r, me) * W_IN_SHARD + col, PROJ_TILE)
            src = win_ref if own else wall_ref
            part = D_MODEL // W_LOAD_PIECES
            return [pltpu.make_async_copy(src.at[pl.ds(q * part, part), pl.ds(col, PROJ_TILE)],
                                          wbuf.at[slot, pl.ds(q * part, part), :], w_sems.at[slot, q])
                    for q in range(W_LOAD_PIECES)]

        for own in (True, False):
            @pl.when(first & ((r == 0) if own else (r > 0)))
            def _(own=own):
                for slot in range(per):
                    for cp in tile_loads(slot, own):
                        cp.start()
                for cp in tile_loads(0, own):
                    cp.wait()

            @pl.when((t > 0) & (s == 0) & ((r == 0) if own else (r > 0)))
            def _(own=own):
                for cp in tile_loads(1, own):
                    cp.wait()

        tile_now = per * chip_at(r, me) + t
        want_f32, want_bf16 = tile_now >= F32_FROM_TILE, tile_now <= BF16_TO_TILE

        @pl.when(want_f32 & jnp.logical_not(want_bf16))
        def _():
            proj_ref[...] = _dot(h_ref[...], wbuf[t])

        @pl.when(want_bf16 & jnp.logical_not(want_f32))
        def _():
            qkv_ref[...] = _dot(h_ref[...], wbuf[t]).astype(BF16)

        @pl.when(want_f32 & want_bf16)
        def _():
            p = _dot(h_ref[...], wbuf[t])
            proj_ref[...] = p
            qkv_ref[...] = p.astype(BF16)

        @pl.when((r == 3) & (t == per - 1) & (s == ns - 1))
        def _():
            for k in range(3):
                for p in range(n_piece):
                    send(k, p).wait_send()
                    forward(k, p, c).wait_send()

    def out_index(wanted):
        order = [0, 2, 1, 3]
        table = []
        for chip in range(N_CHIPS):
            tiles = [per * (chip ^ order[q // per]) + q % per for q in range(N_CHIPS * per)]
            row = []
            for q, tile in enumerate(tiles):
                if wanted(tile):
                    row.append((tile, None))
                    continue
                before = [u for u in tiles[:q] if wanted(u)]
                after = [u for u in tiles[q:] if wanted(u)]
                row.append((before[-1], ns - 1) if before else (after[0], 0))
            table.append(row)

        def index(r, t, s, idx):
            q = r * per + t
            col, fixed_s = jnp.int32(0), jnp.int32(-1)
            for chip in range(N_CHIPS):
                for pos, (tile, hold) in enumerate(table[chip]):
                    here = (idx[0] == chip) & (q == pos)
                    col = jnp.where(here, tile, col)
                    fixed_s = jnp.where(here, -1 if hold is None else hold, fixed_s)
            return jnp.where(fixed_s < 0, s, fixed_s), col

        return index

    grid_spec = pltpu.PrefetchScalarGridSpec(
        num_scalar_prefetch=1,
        grid=(N_CHIPS, per, ns),
        in_specs=[pl.BlockSpec((ts, D_MODEL), lambda r, t, s, idx: (s, 0)), ANY, ANY],
        out_specs=[pl.BlockSpec((ts, PROJ_TILE), out_index(lambda tile: tile >= F32_FROM_TILE)),
                   pl.BlockSpec((ts, PROJ_TILE), out_index(lambda tile: tile <= BF16_TO_TILE)),
                   ANY, ANY],
        scratch_shapes=[pltpu.VMEM((per, D_MODEL, PROJ_TILE), BF16),
                        pltpu.SemaphoreType.DMA((6, n_piece)), pltpu.SemaphoreType.DMA((6, n_piece)),
                        pltpu.SemaphoreType.DMA((per, W_LOAD_PIECES))],
    )
    return pl.pallas_call(
        body,
        name="gather_inproj",
        grid_spec=grid_spec,
        out_shape=[jax.ShapeDtypeStruct((s_len, IN_WIDTH), F32),
                   jax.ShapeDtypeStruct((s_len, IN_WIDTH), BF16),
                   jax.ShapeDtypeStruct((D_MODEL, IN_WIDTH), BF16),
                   jax.ShapeDtypeStruct((3, N_CHIPS, ROW_SHARD, D_MODEL), BF16)],
        compiler_params=_cparams(("arbitrary", "arbitrary", "arbitrary")),
    )(idx, h, w_in_b, w_sq_b)


def _place_own(idx, w_in_b, w_sq_b, w_all, wsq):
    n = 4
    r_in, r_sq = D_MODEL // n, ROW_SHARD // n

    def body(idx_ref, win_ref, wsq_ref, w_all_in, wsq_in, w_all_out, wsq_out):
        w_all_out[...] = win_ref[...]
        wsq_out[:, 0] = wsq_ref[...]

    grid_spec = pltpu.PrefetchScalarGridSpec(
        num_scalar_prefetch=1,
        grid=(n,),
        in_specs=[pl.BlockSpec((r_in, W_IN_SHARD), lambda r, idx: (r, 0)),
                  pl.BlockSpec((3, r_sq, D_MODEL), lambda r, idx: (0, r, 0)), ANY, ANY],
        out_specs=[pl.BlockSpec((r_in, W_IN_SHARD), lambda r, idx: (r, idx[0])),
                   pl.BlockSpec((3, 1, r_sq, D_MODEL), lambda r, idx: (0, idx[0], r, 0))],
    )
    return pl.pallas_call(
        body,
        name="place_own",
        grid_spec=grid_spec,
        out_shape=[jax.ShapeDtypeStruct(w_all.shape, BF16), jax.ShapeDtypeStruct(wsq.shape, BF16)],
        input_output_aliases={3: 0, 4: 1},
        compiler_params=_cparams(("arbitrary",)),
    )(idx, w_in_b, w_sq_b, w_all, wsq)


def _swap_halves(g_in, g_sq):
    n_in = 16
    n_piece = n_in + 3 * N_CHIPS
    rows = HALF_IN // n_in

    def body(gin_ref, gsq_ref, got_in, got_sq, send_sems, recv_sems):
        x, y, c, _ = _position()
        sibling = (x, y, 1 - c)

        def src_piece(p):
            if p < n_in:
                return gin_ref.at[pl.ds((1 - c) * HALF_IN + p * rows, rows), :]
            a, chip = divmod(p - n_in, N_CHIPS)
            return gsq_ref.at[a, chip, pl.ds((1 - c) * HALF_SQ, HALF_SQ), :]

        def dst_piece(p):
            if p < n_in:
                return got_in.at[pl.ds(p * rows, rows), :]
            a, chip = divmod(p - n_in, N_CHIPS)
            return got_sq.at[a, chip]

        out = [_remote(src_piece(p), dst_piece(p), send_sems.at[p], recv_sems.at[p], sibling) for p in range(n_piece)]
        for cp in out:
            cp.start()
        for cp in out:
            cp.wait()

    return pl.pallas_call(
        body,
        name="swap_halves",
        in_specs=[ANY, ANY],
        out_specs=[ANY, ANY],
        out_shape=[jax.ShapeDtypeStruct((HALF_IN, IN_WIDTH), F32),
                   jax.ShapeDtypeStruct((3, N_CHIPS, HALF_SQ, D_MODEL), F32)],
        scratch_shapes=[pltpu.SemaphoreType.DMA((n_piece,))] * 2,
    )(g_in, g_sq)


def _join_halves(r_in, r_sq):
    n_in = 16
    n_piece = n_in + 3
    rows = HALF_IN // n_in

    def body(in_alias, sq_alias, full_in, full_sq, send_sems, recv_sems):
        del in_alias, sq_alias
        x, y, c, _ = _position()
        sibling = (x, y, 1 - c)

        def piece(p, core):
            if p < n_in:
                return full_in.at[pl.ds(core * HALF_IN + p * rows, rows), :]
            return full_sq.at[p - n_in, pl.ds(core * HALF_SQ, HALF_SQ), :]

        out = [_remote(piece(p, c), piece(p, c), send_sems.at[p], recv_sems.at[p], sibling) for p in range(n_piece)]
        for cp in out:
            cp.start()
        for p in range(n_piece):
            _remote(piece(p, 1 - c), piece(p, 1 - c), send_sems.at[p], recv_sems.at[p], sibling).wait_recv()
        for cp in out:
            cp.wait_send()

    return pl.pallas_call(
        body,
        name="join_halves",
        in_specs=[ANY, ANY],
        out_specs=[ANY, ANY],
        out_shape=[jax.ShapeDtypeStruct((D_MODEL, W_IN_SHARD), F32),
                   jax.ShapeDtypeStruct((3, ROW_SHARD, D_MODEL), F32)],
        input_output_aliases={0: 0, 1: 1},
        scratch_shapes=[pltpu.SemaphoreType.DMA((n_piece,)), pltpu.SemaphoreType.DMA((n_piece,))],
    )(r_in, r_sq)


SMALL_ROWS = 56
N_DEV = 8


def _sum_small(part):
    def body(part_ref, out_ref, slots, send_sems, recv_sems):
        x, y, c, _ = _position()
        me = 4 * x + 2 * y + c
        slots[me] = part_ref[...]
        out = []
        for r in range(1, N_DEV):
            rx, ry, rc = (r >> 2) & 1, (r >> 1) & 1, r & 1
            to = (1 - x if rx else x, 1 - y if ry else y, 1 - c if rc else c)
            out.append(_remote(part_ref, slots.at[me], send_sems.at[r - 1], recv_sems.at[r - 1], to))
        for cp in out:
            cp.start()
        for r in range(1, N_DEV):
            _remote(part_ref, slots.at[me ^ r], send_sems.at[r - 1], recv_sems.at[r - 1], (x, y, c)).wait_recv()
        for cp in out:
            cp.wait_send()
        total = slots[0]
        for d in range(1, N_DEV):
            total = total + slots[d]
        out_ref[...] = total

    vmem = pl.BlockSpec(memory_space=pltpu.VMEM)
    return pl.pallas_call(
        body,
        name="sum_small",
        in_specs=[vmem],
        out_specs=vmem,
        out_shape=jax.ShapeDtypeStruct((SMALL_ROWS, HEAD_DIM), F32),
        scratch_shapes=[pltpu.VMEM((N_DEV, SMALL_ROWS, HEAD_DIM), F32),
                        pltpu.SemaphoreType.DMA((N_DEV - 1,)), pltpu.SemaphoreType.DMA((N_DEV - 1,))],
    )(part)


def _prefetch_call(body, name, idx, grid, in_specs, out_specs, out_shape, args):
    grid_spec = pltpu.PrefetchScalarGridSpec(num_scalar_prefetch=1, grid=grid, in_specs=in_specs, out_specs=out_specs)
    return pl.pallas_call(body, name=name, grid_spec=grid_spec, out_shape=out_shape,
                          compiler_params=_cparams(("arbitrary",) * len(grid)))(idx, *args)


def _sum_a_in(idx, g_in, got_in):
    tr = 128
    nr = HALF_IN // tr

    def body(idx_ref, a_ref, b_ref, o_ref):
        o_ref[0] = (a_ref[...] + b_ref[...]).astype(WIRE)

    return _prefetch_call(
        body, "sum_a_in", idx, (N_CHIPS, nr),
        [pl.BlockSpec((tr, W_IN_SHARD), lambda j, r, idx: (idx[1] * nr + r, j)),
         pl.BlockSpec((tr, W_IN_SHARD), lambda j, r, idx: (r, j))],
        pl.BlockSpec((1, tr, W_IN_SHARD), lambda j, r, idx: (j, r, 0)),
        jax.ShapeDtypeStruct((N_CHIPS, HALF_IN, W_IN_SHARD), WIRE), (g_in, got_in))


def _sum_a_sq(idx, g_sq, got_sq):
    blk = (1, 1, HALF_SQ, D_MODEL)

    def body(idx_ref, a_ref, b_ref, o_ref):
        o_ref[...] = (a_ref[...] + b_ref[...]).astype(WIRE)

    return _prefetch_call(
        body, "sum_a_sq", idx, (3, N_CHIPS),
        [pl.BlockSpec(blk, lambda a, j, idx: (a, j, idx[1], 0)), pl.BlockSpec(blk, lambda a, j, idx: (a, j, 0, 0))],
        pl.BlockSpec(blk, lambda a, j, idx: (a, j, 0, 0)),
        jax.ShapeDtypeStruct((3, N_CHIPS, HALF_SQ, D_MODEL), WIRE), (g_sq, got_sq))


def _sum_b_in(idx, s_in, got_in):
    tr = 128
    nr = HALF_IN // tr

    def body(idx_ref, a_ref, b_ref, o_ref):
        o_ref[...] = ((a_ref[0].astype(F32) + b_ref[0].astype(F32)) + b_ref[1].astype(F32)) + b_ref[2].astype(F32)

    return _prefetch_call(
        body, "sum_b_in", idx, (nr,),
        [pl.BlockSpec((1, tr, W_IN_SHARD), lambda r, idx: (idx[0], r, 0)),
         pl.BlockSpec((3, tr, W_IN_SHARD), lambda r, idx: (0, r, 0))],
        pl.BlockSpec((tr, W_IN_SHARD), lambda r, idx: (idx[1] * nr + r, 0)),
        jax.ShapeDtypeStruct((D_MODEL, W_IN_SHARD), F32), (s_in, got_in))


def _sum_b_sq(idx, s_sq, got_sq):
    def body(idx_ref, a_ref, b_ref, o_ref):
        o_ref[0] = ((a_ref[0, 0].astype(F32) + b_ref[0, 0].astype(F32)) + b_ref[1, 0].astype(F32)) + b_ref[2, 0].astype(F32)

    return _prefetch_call(
        body, "sum_b_sq", idx, (3,),
        [pl.BlockSpec((1, 1, HALF_SQ, D_MODEL), lambda a, idx: (a, idx[0], 0, 0)),
         pl.BlockSpec((3, 1, HALF_SQ, D_MODEL), lambda a, idx: (0, a, 0, 0))],
        pl.BlockSpec((1, HALF_SQ, D_MODEL), lambda a, idx: (a, idx[1], 0)),
        jax.ShapeDtypeStruct((3, ROW_SHARD, D_MODEL), F32), (s_sq, got_sq))


def _adamw_math(w, g, m, v):
    m = ADAM_B1 * m + (1.0 - ADAM_B1) * g
    v = ADAM_B2 * v + (1.0 - ADAM_B2) * (g * g)
    m_hat = m / (1.0 - ADAM_B1 ** ADAM_STEP)
    v_hat = v / (1.0 - ADAM_B2 ** ADAM_STEP)
    delta = -ADAM_LR * (m_hat / (jnp.sqrt(v_hat) + ADAM_EPS) + ADAM_WD * w)
    return delta, m, v


def _adamw(w, g, m, v, name):
    rows, cols = w.shape
    tr = min(128, rows)

    def body(w_ref, g_ref, m_ref, v_ref, d_ref, nm_ref, nv_ref):
        d_ref[...], nm_ref[...], nv_ref[...] = _adamw_math(w_ref[...], g_ref[...], m_ref[...], v_ref[...])

    spec = pl.BlockSpec((tr, cols), lambda r: (r, 0))
    return pl.pallas_call(
        body,
        name=name,
        grid=(rows // tr,),
        in_specs=[spec] * 4,
        out_specs=[spec] * 3,
        out_shape=[jax.ShapeDtypeStruct((rows, cols), F32)] * 3,
        compiler_params=_cparams(("arbitrary",)),
    )(w, g, m, v)


def _adamw_small(sums, w, m, v):
    def body(s_ref, w_ref, m_ref, v_ref, loss_ref, g_ref, d_ref, nm_ref, nv_ref):
        s = s_ref[...]
        w = w_ref[...]
        loss_ref[...] = s[0:1, 0:1]
        l0, l1 = w[24:32], w[32:40]
        mx = jnp.maximum(l0, l1)
        e0, e1 = jnp.exp(l0 - mx), jnp.exp(l1 - mx)
        p0, p1 = e0 / (e0 + e1), e1 / (e0 + e1)
        d_lb = s[32:40]
        g = jnp.concatenate([s[8:16], s[16:32], d_lb * p0 * (1.0 - p0), -d_lb * p0 * p1, s[40:48], s[48:56]], axis=0)
        g_ref[...] = g
        d_ref[...], nm_ref[...], nv_ref[...] = _adamw_math(w, g, m_ref[...], v_ref[...])

    packed = jax.ShapeDtypeStruct((SMALL_ROWS, HEAD_DIM), F32)
    return pl.pallas_call(
        body,
        name="adamw_small",
        out_shape=[jax.ShapeDtypeStruct((1, 1), F32), packed, packed, packed, packed],
    )(sums, w, m, v)


def _pack_small(ng, bg, lbl, hgn, fg):
    return jnp.concatenate([a.reshape(-1, HEAD_DIM) for a in (ng, bg, lbl, hgn, fg)], axis=0)


def _unpack_small(p):
    return (p[0:8].reshape(1, D_MODEL), p[8:24].reshape(1, 2 * D_MODEL), p[24:40].reshape(2, HEADS, HEAD_DIM),
            p[40:48].reshape(1, HEADS, HEAD_DIM), p[48:56].reshape(D_MODEL))


def kernel(x, norm_g, w_in, b_gate, lb_logits, hg_norm_g, w_sb_proj, w_hg_proj, w_out, final_norm_g, loss_target, m_norm_g, m_w_in, m_b_gate, m_lb_logits, m_hg_norm_g, m_w_sb_proj, m_w_hg_proj, m_w_out, m_final_norm_g, v_norm_g, v_w_in, v_b_gate, v_lb_logits, v_hg_norm_g, v_w_sb_proj, v_w_hg_proj, v_w_out, v_final_norm_g):
    s_len = x.shape[1]
    w_sq = jnp.stack([w_sb_proj[0], w_hg_proj[0], w_out[0]])
    idx = jnp.stack([2 * lax.axis_index("x") + lax.axis_index("y"), lax.axis_index("c")]).astype(jnp.int32)
    w_in_b, w_sq_b = w_in[0].astype(BF16), w_sq.astype(BF16)
    h, h_t = _prenorm(x[0], norm_g)
    proj, qkv, w_all, wsq = _gather_inproj(idx, h, w_in_b, w_sq_b)
    w_all, wsq = _place_own(idx, w_in_b, w_sq_b, w_all, wsq)
    wsq = wsq.reshape(3, D_MODEL, D_MODEL)

    (g_in, g_sb, g_hg, g_out, segs, dout, loss, d_bg, d_lb, d_hgn, d_fg) = _local_grads(
        x[0], loss_target[0], proj, h_t, qkv, b_gate, lb_logits.reshape(2, D_MODEL), hg_norm_g.reshape(1, D_MODEL),
        final_norm_g.reshape(1, D_MODEL), wsq[0], wsq[1], wsq[2])

    g_sq = jnp.stack([g_sb, g_hg, g_out]).reshape(3, N_CHIPS, ROW_SHARD, D_MODEL)
    got_in, got_sq = _swap_halves(g_in, g_sq)
    s_in, s_sq = _sum_a_in(idx, g_in, got_in), _sum_a_sq(idx, g_sq, got_sq)
    grad_x, d_ng, got_in, got_sq = _dx(segs, w_all, x[0], norm_g, dout, s_in, s_sq)
    grad_in, grad_sq = _join_halves(_sum_b_in(idx, s_in, got_in), _sum_b_sq(idx, s_sq, got_sq))

    d_in, nm_in, nv_in = _adamw(w_in[0], grad_in, m_w_in[0], v_w_in[0], "adamw_in")
    flat = lambda a, b, c: jnp.concatenate([a[0], b[0], c[0]], axis=0)
    d_sq, nm_sq, nv_sq = _adamw(flat(w_sb_proj, w_hg_proj, w_out), grad_sq.reshape(3 * ROW_SHARD, D_MODEL),
                                flat(m_w_sb_proj, m_w_hg_proj, m_w_out), flat(v_w_sb_proj, v_w_hg_proj, v_w_out),
                                "adamw_sq")

    pad = jnp.zeros((8, HEAD_DIM), F32).at[0, 0].set(loss[0, 0])
    part = jnp.concatenate([pad] + [a.reshape(-1, HEAD_DIM) for a in (d_ng, d_bg, d_lb, d_hgn, d_fg)], axis=0)
    sums = _sum_small(part)
    loss_out, g_sm, d_sm, nm_sm, nv_sm = _adamw_small(
        sums, _pack_small(norm_g, b_gate, lb_logits, hg_norm_g, final_norm_g),
        _pack_small(m_norm_g, m_b_gate, m_lb_logits, m_hg_norm_g, m_final_norm_g),
        _pack_small(v_norm_g, v_b_gate, v_lb_logits, v_hg_norm_g, v_final_norm_g))

    def big(t_in, t_sq):
        sq = t_sq.reshape(3, 1, ROW_SHARD, D_MODEL)
        return t_in[None], sq[0], sq[1], sq[2]

    def order(small, in_, sb, hg, out):
        ng, bg, lbl, hgn, fg = small
        return [ng, in_, bg, lbl, hgn, sb, hg, out, fg]

    outs = [loss_out[0, 0], grad_x[None]]
    for small, (t_in, t_sq) in ((g_sm, (grad_in, grad_sq)), (d_sm, (d_in, d_sq)), (nm_sm, (nm_in, nm_sq)), (nv_sm, (nv_in, nv_sq))):
        outs += order(_unpack_small(small), *big(t_in, t_sq))
    return tuple(outs)
```

```python
import functools

import jax
import jax.numpy as jnp
from jax import lax
from jax.experimental import pallas as pl
from jax.experimental.pallas import tpu as pltpu

F32 = jnp.float32
BF16 = jnp.bfloat16

D_MODEL = 1024
HEADS = 8
HEAD_DIM = 128
IN_WIDTH = 10240
N_CHIPS = 4
W_IN_SHARD = IN_WIDTH // N_CHIPS
ROW_SHARD = D_MODEL // N_CHIPS
RMS_EPS = 1e-6

OFF_SB_Q, OFF_SB_K, OFF_SB_V, OFF_SB_Z = 0, 1024, 2048, 3072
OFF_HG_Q, OFF_HG_F, OFF_HG_I, OFF_HG_Z, OFF_GATE = 4096, 5120, 6144, 7168, 8192

SB_BLOCK = 256
SB_FWD_HEADS = 4
SB_BWD_HEADS = 2
SB_ROWS = 256
SB_DEAD = -110.0
SB_GONE = -1e30
HG_CHUNK = 32
HG_PAIR = 2 * HG_CHUNK
HG_STEP = 256
HG_MID = HG_CHUNK // 2 - 1

ADAM_LR, ADAM_B1, ADAM_B2, ADAM_EPS, ADAM_WD, ADAM_STEP = 0.001, 0.9, 0.999, 1e-08, 0.01, 10

VMEM_LIMIT = 56 * 1024 * 1024
VMEM_LIMIT_DX = 60 * 1024 * 1024

MESH = pl.DeviceIdType.MESH


def _cparams(sem, vmem=VMEM_LIMIT):
    return pltpu.CompilerParams(dimension_semantics=sem, vmem_limit_bytes=vmem)


def _dot(a, b):
    return jnp.dot(a, b, preferred_element_type=F32)


def _dot_nt(a, b):
    return lax.dot_general(a, b, (((1,), (1,)), ((), ())), preferred_element_type=F32)


def _dot_tn(a, b):
    return lax.dot_general(a, b, (((0,), (0,)), ((), ())), preferred_element_type=F32)


def _split_dot(x, tri):
    hi = x.astype(BF16)
    lo = (x - hi.astype(F32)).astype(BF16)
    both = _dot(jnp.concatenate([hi, lo], axis=0), tri)
    return both[: x.shape[0]] + both[x.shape[0] :]


def _split_dot_left(tri, x):
    hi = x.astype(BF16)
    lo = (x - hi.astype(F32)).astype(BF16)
    return _dot(tri, hi) + _dot(tri, lo)


def _sigmoid(x):
    return 1.0 / (1.0 + jnp.exp(-x))


def _prenorm(x, norm_g):
    s_len = x.shape[0]
    ts = min(1024, s_len)

    def body(x_ref, g_ref, h_ref, ht_ref):
        xv = x_ref[...]
        r = lax.rsqrt(jnp.mean(xv * xv, axis=-1, keepdims=True) + RMS_EPS)
        hv = (xv * r) * g_ref[...]
        h_ref[...] = hv.astype(BF16)
        ht_ref[...] = hv.T.astype(BF16)

    return pl.pallas_call(
        body,
        name="prenorm",
        grid=(s_len // ts,),
        in_specs=[pl.BlockSpec((ts, D_MODEL), lambda s: (s, 0)), pl.BlockSpec((1, D_MODEL), lambda s: (0, 0))],
        out_specs=[pl.BlockSpec((ts, D_MODEL), lambda s: (s, 0)), pl.BlockSpec((D_MODEL, ts), lambda s: (0, s))],
        out_shape=[jax.ShapeDtypeStruct((s_len, D_MODEL), BF16), jax.ShapeDtypeStruct((D_MODEL, s_len), BF16)],
        compiler_params=_cparams(("arbitrary",)),
    )(x, norm_g)


def _sb_scores(qb, kb, causal, tri_excl, diag):
    z = _dot_nt(qb, kb) * HEAD_DIM ** -0.5
    ls_pos = jnp.minimum(z, 0.0) - jnp.log1p(jnp.exp(-jnp.abs(z)))
    log_not = ls_pos - z
    log_not_m = jnp.where(causal, log_not, 0.0) if diag else log_not
    return ls_pos, log_not, log_not_m, _split_dot(log_not_m, tri_excl)


def _sb_weights(ls_pos, suffix, carry, causal, diag):
    surv = suffix + carry
    w = jnp.exp(ls_pos + surv)
    return surv, (jnp.where(causal, w, 0.0) if diag else w)


def _sb_specs(s_len, blk, heads):
    width = heads * HEAD_DIM

    def blk_spec(off):
        return pl.BlockSpec((blk, width), lambda h, i: (i, off // width + h))

    def head_spec(off, buffers=2):
        return pl.BlockSpec((s_len, width), lambda h, i: (0, off // width + h), pipeline_mode=pl.Buffered(buffers))

    return blk_spec, head_spec


def _head_cols(p):
    return slice(p * HEAD_DIM, (p + 1) * HEAD_DIM)


def _sb_chains(blk, heads):
    rows = min(SB_ROWS, blk)
    return [(p, a) for p in range(heads) for a in range(blk // rows)], rows


def _sb_masks(blk, rows):
    row = lax.broadcasted_iota(jnp.int32, (rows, blk), 0)
    col = lax.broadcasted_iota(jnp.int32, (rows, blk), 1)
    causal = [row + a * rows > col for a in range(blk // rows)]
    row = lax.broadcasted_iota(jnp.int32, (blk, blk), 0)
    col = lax.broadcasted_iota(jnp.int32, (blk, blk), 1)
    tri_excl = (row > col).astype(BF16)
    tri_incl = (row >= col).astype(BF16)
    return causal, tri_excl, tri_incl


def _sb_alive(st, n_chain):
    alive = functools.reduce(jnp.maximum, [st[1 + 3 * c] for c in range(n_chain)])
    return jnp.max(alive) > SB_DEAD


def _sb_fwd(qkv):
    s_len = qkv.shape[0]
    blk = min(SB_BLOCK, s_len)
    nq = s_len // blk
    chains, rows = _sb_chains(blk, SB_FWD_HEADS)

    def body(q_ref, k_ref, v_ref, o_ref, of_ref):
        i = pl.program_id(1)
        causal, tri_excl, _ = _sb_masks(blk, rows)

        def tiles(specs, st):
            pre = []
            for j, diag, _ in specs:
                start = pl.multiple_of(j * blk, blk)
                for p, a in chains:
                    kb = k_ref[pl.ds(start, blk), _head_cols(p)]
                    qb = q_ref[a * rows : (a + 1) * rows, _head_cols(p)]
                    pre.append(_sb_scores(qb, kb, causal[a], tri_excl, diag) + (v_ref[pl.ds(start, blk), _head_cols(p)],))
            for t, (j, diag, valid) in enumerate(specs):
                new = []
                for c, (p, a) in enumerate(chains):
                    carry, acc, acc_lo = st[3 * c : 3 * c + 3]
                    if valid is not None:
                        carry = jnp.where(valid, carry, SB_GONE)
                    ls_pos, _, log_not_m, suffix, vb = pre[t * len(chains) + c]
                    surv, w = _sb_weights(ls_pos, suffix, carry, causal[a], diag)
                    wb = w.astype(BF16)
                    w_lo = (w - wb.astype(F32)).astype(BF16)
                    both = _dot(jnp.concatenate([wb, w_lo], axis=0), vb)
                    new += [surv[:, 0:1] + log_not_m[:, 0:1], acc + both[:rows], acc_lo + both[rows:]]
                st = tuple(new)
            return st

        zero = jnp.zeros((rows, HEAD_DIM), F32)
        st = tiles([(i, True, None), (jnp.maximum(i - 1, 0), False, i >= 1)],
                   (jnp.zeros((rows, 1), F32), zero, zero) * len(chains))

        def more(st):
            return (st[0] < i) & _sb_alive(st, len(chains))

        def step(st):
            return (st[0] + 1,) + tiles([(i - 1 - st[0], False, None)], st[1:])

        st = lax.while_loop(more, step, (1,) + st)[1:]
        for c, (p, a) in enumerate(chains):
            o_ref[a * rows : (a + 1) * rows, _head_cols(p)] = st[3 * c + 1]
            of_ref[a * rows : (a + 1) * rows, _head_cols(p)] = st[3 * c + 1] + st[3 * c + 2]

    blk_spec, head_spec = _sb_specs(s_len, blk, SB_FWD_HEADS)
    return pl.pallas_call(
        body,
        name="sb_fwd",
        grid=(HEADS // SB_FWD_HEADS, nq),
        in_specs=[blk_spec(OFF_SB_Q), head_spec(OFF_SB_K), head_spec(OFF_SB_V)],
        out_specs=[blk_spec(0), blk_spec(0)],
        out_shape=[jax.ShapeDtypeStruct((s_len, D_MODEL), F32)] * 2,
        compiler_params=_cparams(("arbitrary", "arbitrary")),
    )(qkv, qkv, qkv)


def _sb_bwd(qkv, o_fine, d_o):
    s_len = qkv.shape[0]
    blk = min(SB_BLOCK, s_len)
    nq = s_len // blk
    scale = HEAD_DIM ** -0.5
    chains, rows = _sb_chains(blk, SB_BWD_HEADS)

    def body(q_ref, k_ref, v_ref, of_ref, do_ref, dq_ref, dk_ref, dv_ref, dk_acc, dv_acc):
        i = pl.program_id(1)

        @pl.when(i == 0)
        def _():
            dk_acc[...] = jnp.zeros_like(dk_acc)
            dv_acc[...] = jnp.zeros_like(dv_acc)

        dob = do_ref[...].astype(BF16)
        prod = dob.astype(F32) * of_ref[...]
        causal, tri_excl, tri_incl = _sb_masks(blk, rows)

        def group(x, p, a):
            return x[a * rows : (a + 1) * rows, _head_cols(p)]

        totals = [jnp.sum(group(prod, p, a), axis=-1, keepdims=True) for p, a in chains]

        def tiles(specs, st):
            pre = []
            for j, diag, _ in specs:
                start = pl.multiple_of(j * blk, blk)
                for p, a in chains:
                    kb = k_ref[pl.ds(start, blk), _head_cols(p)]
                    vb = v_ref[pl.ds(start, blk), _head_cols(p)]
                    qb, dob_c = group(q_ref, p, a), group(dob, p, a)
                    pre.append(_sb_scores(qb, kb, causal[a], tri_excl, diag) + (_dot_nt(dob_c, vb), qb, kb, dob_c))
            for t, (j, diag, valid) in enumerate(specs):
                start = pl.multiple_of(j * blk, blk)
                mids = []
                for c, (p, a) in enumerate(chains):
                    c_not = st[3 * c]
                    if valid is not None:
                        c_not = jnp.where(valid, c_not, SB_GONE)
                    ls_pos, _, _, suffix, d_w = pre[t * len(chains) + c][:5]
                    surv, w = _sb_weights(ls_pos, suffix, c_not, causal[a], diag)
                    dlw = d_w * w
                    mids.append((surv, w, dlw, _split_dot(dlw, tri_incl)))
                new = []
                dk_new = [None] * SB_BWD_HEADS
                dv_new = [None] * SB_BWD_HEADS
                for c, (p, a) in enumerate(chains):
                    c_dlw, dq = st[3 * c + 1 : 3 * c + 3]
                    ls_pos, log_not, log_not_m, _, _, qb, kb, dob_c = pre[t * len(chains) + c]
                    surv, w, dlw, suffix = mids[c]
                    d_not = totals[c] - c_dlw - suffix
                    dz = ((dlw + d_not) * jnp.exp(log_not) - d_not) * scale
                    if diag:
                        dz = jnp.where(causal[a], dz, 0.0)
                    if valid is not None:
                        dz = jnp.where(valid, dz, 0.0)
                    dzb = dz.astype(BF16)
                    dk_c, dv_c = _dot_tn(dzb, qb), _dot_tn(w.astype(BF16), dob_c)
                    dk_new[p] = dk_c if dk_new[p] is None else dk_new[p] + dk_c
                    dv_new[p] = dv_c if dv_new[p] is None else dv_new[p] + dv_c
                    new += [surv[:, 0:1] + log_not_m[:, 0:1], c_dlw + suffix[:, 0:1], dq + _dot(dzb, kb)]
                for p in range(SB_BWD_HEADS):
                    dk_acc[pl.ds(start, blk), _head_cols(p)] += dk_new[p]
                    dv_acc[pl.ds(start, blk), _head_cols(p)] += dv_new[p]
                st = tuple(new)
            return st

        zcol = jnp.zeros((rows, 1), F32)
        st = tiles([(i, True, None), (jnp.maximum(i - 1, 0), False, i >= 1)],
                   (zcol, zcol, jnp.zeros((rows, HEAD_DIM), F32)) * len(chains))

        def more(st):
            return (st[0] < i) & _sb_alive(st, len(chains))

        def step(st):
            return (st[0] + 1,) + tiles([(i - 1 - st[0], False, None)], st[1:])

        st = lax.while_loop(more, step, (1,) + st)[1:]
        for c, (p, a) in enumerate(chains):
            dq_ref[a * rows : (a + 1) * rows, _head_cols(p)] = st[3 * c + 2].astype(BF16)

        @pl.when(i == nq - 1)
        def _():
            dk_ref[...] = dk_acc[...].astype(BF16)
            dv_ref[...] = dv_acc[...].astype(BF16)

    blk_spec, head_spec = _sb_specs(s_len, blk, SB_BWD_HEADS)
    width = SB_BWD_HEADS * HEAD_DIM
    return pl.pallas_call(
        body,
        name="sb_bwd",
        grid=(HEADS // SB_BWD_HEADS, nq),
        in_specs=[blk_spec(OFF_SB_Q), head_spec(OFF_SB_K, 1), head_spec(OFF_SB_V, 1), blk_spec(0), blk_spec(0)],
        out_specs=[blk_spec(0), head_spec(0), head_spec(0)],
        out_shape=[jax.ShapeDtypeStruct((s_len, D_MODEL), BF16)] * 3,
        scratch_shapes=[pltpu.VMEM((s_len, width), F32), pltpu.VMEM((s_len, width), F32)],
        compiler_params=_cparams(("arbitrary", "arbitrary")),
    )(qkv, qkv, qkv, o_fine, d_o)


def _hg_lower_bound(lbl_ref):
    l0 = lbl_ref[0:1, :]
    l1 = lbl_ref[1:2, :]
    mx = jnp.maximum(l0, l1)
    e0 = jnp.exp(l0 - mx)
    e1 = jnp.exp(l1 - mx)
    return e0 / (e0 + e1)


def _hg_gates(hq, hf, lb):
    sig_f = _sigmoid(hf)
    f = lb + (1.0 - lb) * sig_f
    g = jnp.log(f)
    kk = 1.0 - f
    sig_q = _sigmoid(hq)
    qq = hq * sig_q
    return qq, kk, g, f, sig_f, sig_q


def _period_bcast(x, r, rows, period):
    w = x.shape[-1]
    x3 = x.reshape(rows // period, period, w)
    return jnp.broadcast_to(x3[:, r : r + 1, :], x3.shape).reshape(rows, w)


def _blockdiag(rows, kind):
    row = lax.broadcasted_iota(jnp.int32, (rows, rows), 0)
    col = lax.broadcasted_iota(jnp.int32, (rows, rows), 1)
    if kind in ("next", "prev"):
        first, second = (row, col) if kind == "next" else (col, row)
        keep = ((row // HG_PAIR) == (col // HG_PAIR)) & (first % HG_PAIR < HG_CHUNK) & (second % HG_PAIR >= HG_CHUNK)
    else:
        keep = (row // HG_CHUNK) == (col // HG_CHUNK)
        if kind == "lower":
            keep = keep & (row >= col)
        elif kind == "upper":
            keep = keep & (row <= col)
    return jnp.where(keep, 1.0, 0.0).astype(BF16)


def _hg_operands(hq, hf, lb, rows):
    qq, kk, g, f, sig_f, sig_q = _hg_gates(hq, hf, lb)
    cum = _split_dot_left(_blockdiag(rows, "lower"), g)
    mid = _period_bcast(cum, HG_MID, rows, HG_CHUNK)
    last = _period_bcast(cum, HG_CHUNK - 1, rows, HG_CHUNK)
    last0 = _period_bcast(cum, HG_CHUNK - 1, rows, HG_PAIR)
    last1 = _period_bcast(cum, HG_PAIR - 1, rows, HG_PAIR)
    second = (lax.broadcasted_iota(jnp.int32, cum.shape, 0) % HG_PAIR) >= HG_CHUNK
    e = dict(qm=jnp.exp(cum - mid), km=jnp.exp(mid - cum), qd=jnp.exp(cum), kl=jnp.exp(last - cum),
             q_in=jnp.where(second, jnp.exp(last0), 1.0), k_out=jnp.where(second, 1.0, jnp.exp(last1)),
             pair=jnp.exp(last0 + last1))
    v = dict(qm=qq * e["qm"], km=kk * e["km"], qd=qq * e["qd"], kl=kk * e["kl"])
    v["qp"] = v["qd"] * e["q_in"]
    v["kp"] = v["kl"] * e["k_out"]
    return v, e, second, (f, sig_f, sig_q)


def _hg_store_operands(v, second, hi, refs):
    zero = jnp.zeros_like(v["qm"])
    q_cat, k_cat, qp_b, kp_b, v_b = refs
    q_cat[:, 0:D_MODEL] = jnp.where(second, zero, v["qm"]).astype(BF16)
    q_cat[:, D_MODEL : 2 * D_MODEL] = jnp.where(second, v["qm"], zero).astype(BF16)
    q_cat[:, 2 * D_MODEL :] = jnp.where(second, v["qd"], zero).astype(BF16)
    k_cat[:, 0:D_MODEL] = jnp.where(second, zero, v["km"]).astype(BF16)
    k_cat[:, D_MODEL : 2 * D_MODEL] = jnp.where(second, v["km"], zero).astype(BF16)
    k_cat[:, 2 * D_MODEL :] = jnp.where(second, zero, v["kl"]).astype(BF16)
    qp_b[...] = v["qp"].astype(BF16)
    kp_b[...] = v["kp"].astype(BF16)
    v_b[...] = hi.astype(BF16)


def _hg_pair_operands(cat, r0, c0):
    return jnp.concatenate([cat[r0 : r0 + HG_PAIR, g * D_MODEL + c0 : g * D_MODEL + c0 + HEAD_DIM] for g in range(3)], axis=1)


def _hg_fwd(proj, lbl):
    s_len = proj.shape[0]
    rows = min(HG_STEP, s_len)
    n_pairs = rows // HG_PAIR

    def body(hq_ref, hf_ref, hi_ref, lbl_ref, o_ref, st_ref, state, q_cat, k_cat, qp_b, kp_b, v_b):
        @pl.when(pl.program_id(0) == 0)
        def _():
            state[...] = jnp.zeros_like(state)

        v, e, second, _ = _hg_operands(hq_ref[...], hf_ref[...], _hg_lower_bound(lbl_ref), rows)
        _hg_store_operands(v, second, hi_ref[...], (q_cat, k_cat, qp_b, kp_b, v_b))
        e_pair = e["pair"]
        row = lax.broadcasted_iota(jnp.int32, (HG_PAIR, HG_PAIR), 0)
        col = lax.broadcasted_iota(jnp.int32, (HG_PAIR, HG_PAIR), 1)
        causal = row >= col

        for u in range(n_pairs):
            r0 = u * HG_PAIR
            sls = [(slice(r0, r0 + HG_PAIR), slice(h * HEAD_DIM, (h + 1) * HEAD_DIM)) for h in range(HEADS)]
            a_s = [jnp.where(causal, _dot_nt(_hg_pair_operands(q_cat, r0, h * HEAD_DIM),
                                             _hg_pair_operands(k_cat, r0, h * HEAD_DIM)), 0.0).astype(BF16)
                   for h in range(HEADS)]
            st_s = [state[h] for h in range(HEADS)]
            for h, sl in enumerate(sls):
                st_ref[u, h] = st_s[h]
                state[h] = st_s[h] * e_pair[r0 : r0 + 1, sl[1]] + _dot_tn(v_b[sl], kp_b[sl])
            for h, sl in enumerate(sls):
                o_ref[sl] = _dot(a_s[h], v_b[sl]) + _dot_nt(qp_b[sl], st_s[h].astype(BF16))

    def col_spec(off):
        return pl.BlockSpec((rows, D_MODEL), lambda s: (s, off // D_MODEL))

    bf_tile = pltpu.VMEM((rows, D_MODEL), BF16)
    bf_cat = pltpu.VMEM((rows, 3 * D_MODEL), BF16)
    scratch = [pltpu.VMEM((HEADS, HEAD_DIM, HEAD_DIM), F32), bf_cat, bf_cat, bf_tile, bf_tile, bf_tile]
    return pl.pallas_call(
        body,
        name="hg_fwd",
        grid=(s_len // rows,),
        in_specs=[col_spec(OFF_HG_Q), col_spec(OFF_HG_F), col_spec(OFF_HG_I), pl.BlockSpec((2, D_MODEL), lambda s: (0, 0))],
        out_specs=[
            pl.BlockSpec((rows, D_MODEL), lambda s: (s, 0)),
            pl.BlockSpec((n_pairs, HEADS, HEAD_DIM, HEAD_DIM), lambda s: (s, 0, 0, 0)),
        ],
        out_shape=[
            jax.ShapeDtypeStruct((s_len, D_MODEL), F32),
            jax.ShapeDtypeStruct((s_len // HG_PAIR, HEADS, HEAD_DIM, HEAD_DIM), F32),
        ],
        scratch_shapes=scratch,
        compiler_params=_cparams(("arbitrary",)),
    )(proj, proj, proj, lbl)


def _hg_bwd(proj, lbl, states, d_o):
    s_len = proj.shape[0]
    rows = min(HG_STEP, s_len)
    n_pairs = rows // HG_PAIR
    n_steps = s_len // rows

    def body(hq_ref, hf_ref, hi_ref, lbl_ref, st_ref, do_ref, dp_ref, dlb_ref,
             dstate, q_cat, k_cat, qp_b, kp_b, v_b, do_b, d_qcat, d_kcat, d_qp, d_kp, d_v, d_pair):
        @pl.when(pl.program_id(0) == 0)
        def _():
            dstate[...] = jnp.zeros_like(dstate)
            dlb_ref[...] = jnp.zeros_like(dlb_ref)

        lb = _hg_lower_bound(lbl_ref)
        hq = hq_ref[...]
        v, e, second, (f, sig_f, sig_q) = _hg_operands(hq, hf_ref[...], lb, rows)
        _hg_store_operands(v, second, hi_ref[...], (q_cat, k_cat, qp_b, kp_b, v_b))
        do_b[...] = do_ref[...].astype(BF16)
        e_pair = e["pair"]
        row = lax.broadcasted_iota(jnp.int32, (HG_PAIR, HG_PAIR), 0)
        col = lax.broadcasted_iota(jnp.int32, (HG_PAIR, HG_PAIR), 1)
        causal = row >= col

        for u in reversed(range(n_pairs)):
            r0 = u * HG_PAIR
            sls = [(slice(r0, r0 + HG_PAIR), slice(h * HEAD_DIM, (h + 1) * HEAD_DIM)) for h in range(HEADS)]
            ops = [(_hg_pair_operands(q_cat, r0, h * HEAD_DIM), _hg_pair_operands(k_cat, r0, h * HEAD_DIM))
                   for h in range(HEADS)]
            a_s = [jnp.where(causal, _dot_nt(lhs, rhs), 0.0).astype(BF16) for lhs, rhs in ops]
            da_s = [jnp.where(causal, _dot_nt(do_b[sl], v_b[sl]), 0.0).astype(BF16) for sl in sls]
            st0_s = [st_ref[u, h] for h in range(HEADS)]
            ds1_s = [dstate[h] for h in range(HEADS)]
            ds1b_s = [ds1.astype(BF16) for ds1 in ds1_s]
            for h, sl in enumerate(sls):
                decay = e_pair[r0 : r0 + 1, sl[1]]
                d_pair[u : u + 1, sl[1]] = decay * jnp.sum(ds1_s[h] * st0_s[h], axis=0, keepdims=True)
                dstate[h] = ds1_s[h] * decay + _dot_tn(do_b[sl], qp_b[sl])
            for h, sl in enumerate(sls):
                d_qp[sl] = _dot(do_b[sl], st0_s[h].astype(BF16))
                d_kp[sl] = _dot(v_b[sl], ds1b_s[h])
            for h, sl in enumerate(sls):
                d_v[sl] = _dot_tn(a_s[h], do_b[sl]) + _dot_nt(kp_b[sl], ds1b_s[h])
            for h, sl in enumerate(sls):
                d_lhs = _dot(da_s[h], ops[h][1])
                d_rhs = _dot_tn(da_s[h], ops[h][0])
                for g in range(3):
                    gsl = (sl[0], slice(g * D_MODEL + h * HEAD_DIM, g * D_MODEL + (h + 1) * HEAD_DIM))
                    d_qcat[gsl] = d_lhs[:, g * HEAD_DIM : (g + 1) * HEAD_DIM]
                    d_kcat[gsl] = d_rhs[:, g * HEAD_DIM : (g + 1) * HEAD_DIM]

        zero = jnp.zeros_like(hq)
        dqm = jnp.where(second, d_qcat[:, D_MODEL : 2 * D_MODEL], d_qcat[:, 0:D_MODEL])
        dkm = jnp.where(second, d_kcat[:, D_MODEL : 2 * D_MODEL], d_kcat[:, 0:D_MODEL])
        dqp, dkp = d_qp[...], d_kp[...]
        dqd = dqp * e["q_in"] + jnp.where(second, d_qcat[:, 2 * D_MODEL :], zero)
        dkl = dkp * e["k_out"] + jnp.where(second, zero, d_kcat[:, 2 * D_MODEL :])
        dq = dqm * e["qm"] + dqd * e["qd"]
        dk = dkm * e["km"] + dkl * e["kl"]
        t_kl = dkl * v["kl"]
        dcum = dqm * v["qm"] - dkm * v["km"] + dqd * v["qd"] - t_kl
        dp = d_pair[...]
        dp_b = jnp.broadcast_to(dp[:, None, :], (n_pairs, HG_PAIR, D_MODEL)).reshape(rows, D_MODEL)
        dg = (_split_dot_left(_blockdiag(rows, "upper"), dcum) + _split_dot_left(_blockdiag(rows, "all"), t_kl)
              + _split_dot_left(_blockdiag(rows, "next"), dqp * v["qp"])
              + _split_dot_left(_blockdiag(rows, "prev"), dkp * v["kp"]) + dp_b)
        df = dg / f - dk
        one_m = 1.0 - sig_f
        dp_ref[:, 0:D_MODEL] = (dq * (sig_q * (1.0 + hq * (1.0 - sig_q)))).astype(BF16)
        dp_ref[:, D_MODEL : 2 * D_MODEL] = (df * (1.0 - lb) * sig_f * one_m).astype(BF16)
        dp_ref[:, 2 * D_MODEL : 3 * D_MODEL] = d_v[...].astype(BF16)
        dlb_ref[...] += jnp.sum(df * one_m, axis=0, keepdims=True)

    def col_spec(off):
        return pl.BlockSpec((rows, D_MODEL), lambda s: (n_steps - 1 - s, off // D_MODEL))

    f32_tile = pltpu.VMEM((rows, D_MODEL), F32)
    f32_cat = pltpu.VMEM((rows, 3 * D_MODEL), F32)
    bf_tile = pltpu.VMEM((rows, D_MODEL), BF16)
    bf_cat = pltpu.VMEM((rows, 3 * D_MODEL), BF16)
    scratch = [pltpu.VMEM((HEADS, HEAD_DIM, HEAD_DIM), F32), bf_cat, bf_cat, bf_tile, bf_tile, bf_tile, bf_tile,
               f32_cat, f32_cat, f32_tile, f32_tile, f32_tile, pltpu.VMEM((n_pairs, D_MODEL), F32)]
    return pl.pallas_call(
        body,
        name="hg_bwd",
        grid=(n_steps,),
        in_specs=[
            col_spec(OFF_HG_Q), col_spec(OFF_HG_F), col_spec(OFF_HG_I),
            pl.BlockSpec((2, D_MODEL), lambda s: (0, 0)),
            pl.BlockSpec((n_pairs, HEADS, HEAD_DIM, HEAD_DIM), lambda s: (n_steps - 1 - s, 0, 0, 0)),
            pl.BlockSpec((rows, D_MODEL), lambda s: (n_steps - 1 - s, 0)),
        ],
        out_specs=[
            pl.BlockSpec((rows, 3 * D_MODEL), lambda s: (n_steps - 1 - s, 0)),
            pl.BlockSpec((1, D_MODEL), lambda s: (0, 0)),
        ],
        out_shape=[
            jax.ShapeDtypeStruct((s_len, 3 * D_MODEL), BF16),
            jax.ShapeDtypeStruct((1, D_MODEL), F32),
        ],
        scratch_shapes=scratch,
        compiler_params=_cparams(("arbitrary",)),
    )(proj, proj, proj, lbl, states, d_o)


def _mid(proj, sb_o, hg_o, x, target, b_gate, hg_gain, final_g, w_sb, w_hg, w_out):
    s_len = proj.shape[0]
    ts = min(256, s_len)
    inv_d = 1.0 / D_MODEL

    def body(zsb_ref, hz_ref, gl_ref, sbo_ref, hgo_ref, x_ref, tgt_ref, bg_ref, hgn_ref, fg_ref,
             wsb_ref, whg_ref, wout_ref,
             dout_ref, dsbo_ref, dhgo_ref, dmid_ref,
             asb_ref, dusb_ref, ahg_ref, duhg_ref, y_ref, doutb_ref,
             loss_ref, dfg_ref, dbg_ref, dhgn_ref):
        @pl.when(pl.program_id(0) == 0)
        def _():
            loss_ref[...] = jnp.zeros_like(loss_ref)
            dfg_ref[...] = jnp.zeros_like(dfg_ref)
            dbg_ref[...] = jnp.zeros_like(dbg_ref)
            dhgn_ref[...] = jnp.zeros_like(dhgn_ref)

        z_sb = zsb_ref[...]
        sb_o = sbo_ref[...]
        sig_zsb = _sigmoid(z_sb)
        silu_zsb = z_sb * sig_zsb
        a_sb_f = sb_o * silu_zsb
        a_sb = a_sb_f.astype(BF16)
        u_sb = _dot(a_sb, wsb_ref[...])

        hg_o = hgo_ref[...]
        gain = hgn_ref[...]
        r_parts, yn_parts = [], []
        for h in range(HEADS):
            oh = hg_o[:, h * HEAD_DIM : (h + 1) * HEAD_DIM]
            r = lax.rsqrt(jnp.mean(oh * oh, axis=-1, keepdims=True) + RMS_EPS)
            r_parts.append(jnp.broadcast_to(r, oh.shape))
            yn_parts.append(oh * r)
        r_hg = jnp.concatenate(r_parts, axis=-1)
        yn_hg = jnp.concatenate(yn_parts, axis=-1)
        hn = yn_hg * gain
        hz = hz_ref[...]
        sig_hz = _sigmoid(hz)
        silu_hz = hz * sig_hz
        a_hg_f = hn * silu_hz
        a_hg = a_hg_f.astype(BF16)
        u_hg = _dot(a_hg, whg_ref[...])

        gates = _sigmoid(gl_ref[...] + bg_ref[...])
        g_sb = gates[:, 0:D_MODEL]
        g_hg = gates[:, D_MODEL:]
        y_f = g_sb * u_sb + g_hg * u_hg
        y = y_f.astype(BF16)
        out = x_ref[...] + _dot(y, wout_ref[...])
        r2 = lax.rsqrt(jnp.mean(out * out, axis=-1, keepdims=True) + RMS_EPS)
        yn = out * r2
        fg = fg_ref[...]
        diff = yn * fg - tgt_ref[...]
        loss_ref[...] += 0.5 * inv_d * jnp.sum(diff * diff)

        dyf = diff * inv_d
        dfg_ref[...] += jnp.sum(dyf * yn, axis=0, keepdims=True)
        dyn = dyf * fg
        dout = r2 * (dyn - yn * jnp.mean(dyn * yn, axis=-1, keepdims=True))
        dout_ref[...] = dout
        doutb = dout.astype(BF16)
        doutb_ref[...] = doutb
        dy = _dot_nt(doutb, wout_ref[...])
        du_sb = (dy * g_sb).astype(BF16)
        du_hg = (dy * g_hg).astype(BF16)
        dgl_sb = dy * u_sb * g_sb * (1.0 - g_sb)
        dgl_hg = dy * u_hg * g_hg * (1.0 - g_hg)
        dmid_ref[:, 2 * D_MODEL : 3 * D_MODEL] = dgl_sb.astype(BF16)
        dmid_ref[:, 3 * D_MODEL :] = dgl_hg.astype(BF16)
        dbg_ref[:, 0:D_MODEL] += jnp.sum(dgl_sb, axis=0, keepdims=True)
        dbg_ref[:, D_MODEL:] += jnp.sum(dgl_hg, axis=0, keepdims=True)

        da_sb = _dot_nt(du_sb, wsb_ref[...])
        dsbo_ref[...] = (da_sb * silu_zsb).astype(BF16)
        dmid_ref[:, 0:D_MODEL] = (da_sb * sb_o * (sig_zsb * (1.0 + z_sb * (1.0 - sig_zsb)))).astype(BF16)

        da_hg = _dot_nt(du_hg, whg_ref[...])
        dhn = da_hg * silu_hz
        dmid_ref[:, D_MODEL : 2 * D_MODEL] = (da_hg * hn * (sig_hz * (1.0 + hz * (1.0 - sig_hz)))).astype(BF16)
        dhgn_ref[...] += jnp.sum(dhn * yn_hg, axis=0, keepdims=True)
        dyn_hg = dhn * gain
        prod = dyn_hg * yn_hg
        m_parts = []
        for h in range(HEADS):
            ph = prod[:, h * HEAD_DIM : (h + 1) * HEAD_DIM]
            m_parts.append(jnp.broadcast_to(jnp.mean(ph, axis=-1, keepdims=True), ph.shape))
        dhgo_ref[...] = (r_hg * (dyn_hg - yn_hg * jnp.concatenate(m_parts, axis=-1))).astype(BF16)

        asb_ref[...] = a_sb_f.T.astype(BF16)
        dusb_ref[...] = du_sb
        ahg_ref[...] = a_hg_f.T.astype(BF16)
        duhg_ref[...] = du_hg
        y_ref[...] = y_f.T.astype(BF16)

    def tile(width, off=0):
        return pl.BlockSpec((ts, width), lambda s: (s, off // width))

    def across():
        return pl.BlockSpec((D_MODEL, ts), lambda s: (0, s))

    def whole(shape):
        return pl.BlockSpec(shape, lambda s: (0,) * len(shape))

    def weight():
        return pl.BlockSpec((D_MODEL, D_MODEL), lambda s: (0, 0), pipeline_mode=pl.Buffered(1))

    f32_act = jax.ShapeDtypeStruct((s_len, D_MODEL), F32)
    bf_act = jax.ShapeDtypeStruct((s_len, D_MODEL), BF16)
    bf_act_t = jax.ShapeDtypeStruct((D_MODEL, s_len), BF16)
    return pl.pallas_call(
        body,
        name="mid",
        grid=(s_len // ts,),
        in_specs=[
            tile(D_MODEL, OFF_SB_Z), tile(D_MODEL, OFF_HG_Z), tile(2 * D_MODEL, OFF_GATE),
            tile(D_MODEL), tile(D_MODEL), tile(D_MODEL), tile(D_MODEL),
            whole((1, 2 * D_MODEL)), whole((1, D_MODEL)), whole((1, D_MODEL)),
            weight(), weight(), weight(),
        ],
        out_specs=[
            tile(D_MODEL), tile(D_MODEL), tile(D_MODEL), tile(4 * D_MODEL),
            across(), tile(D_MODEL), across(), tile(D_MODEL), across(), tile(D_MODEL),
            whole((1, 1)), whole((1, D_MODEL)), whole((1, 2 * D_MODEL)), whole((1, D_MODEL)),
        ],
        out_shape=[
            f32_act, bf_act, bf_act, jax.ShapeDtypeStruct((s_len, 4 * D_MODEL), BF16),
            bf_act_t, bf_act, bf_act_t, bf_act, bf_act_t, bf_act,
            jax.ShapeDtypeStruct((1, 1), F32), jax.ShapeDtypeStruct((1, D_MODEL), F32),
            jax.ShapeDtypeStruct((1, 2 * D_MODEL), F32), jax.ShapeDtypeStruct((1, D_MODEL), F32),
        ],
        compiler_params=_cparams(("arbitrary",)),
    )(proj, proj, proj, sb_o, hg_o, x, target, b_gate, hg_gain, final_g, w_sb, w_hg, w_out)


def _grad_square(a_t, b, name):
    s_len = b.shape[0]
    tk = min(1024, s_len)

    def body(a_ref, b_ref, o_ref):
        @pl.when(pl.program_id(0) == 0)
        def _():
            o_ref[...] = jnp.zeros_like(o_ref)

        o_ref[...] += _dot(a_ref[...], b_ref[...])

    return pl.pallas_call(
        body,
        name=name,
        grid=(s_len // tk,),
        in_specs=[pl.BlockSpec((D_MODEL, tk), lambda k: (0, k)), pl.BlockSpec((tk, D_MODEL), lambda k: (k, 0))],
        out_specs=pl.BlockSpec((D_MODEL, D_MODEL), lambda k: (0, 0)),
        out_shape=jax.ShapeDtypeStruct((D_MODEL, D_MODEL), F32),
        compiler_params=_cparams(("arbitrary",)),
    )(a_t, b)


SEG_WIDTHS = (1024, 1024, 1024, 4096, 3072)
SEG_TILE = 1024
SEG_BOUNDS = (0, 1, 2, 3, 7, 10)


def _w_in_tile(k):
    return jnp.where(k < 4, k, jnp.where(k < 7, k + 3, k - 3))


def _grad_w_in(h_t, segs):
    m, s_len = h_t.shape
    tk = min(1024, s_len)
    tn = SEG_TILE
    nk = s_len // tk
    bounds = SEG_BOUNDS

    def body(a_ref, *refs):
        seg_refs, o_ref = refs[:-1], refs[-1]
        j = pl.program_id(0)

        @pl.when(pl.program_id(1) == 0)
        def _():
            o_ref[...] = jnp.zeros_like(o_ref)

        for i, ref in enumerate(seg_refs):
            @pl.when((j >= bounds[i]) & (j < bounds[i + 1]))
            def _(ref=ref):
                o_ref[...] += _dot(a_ref[...], ref[...])

    def seg_spec(lo, hi):
        def index(j, k):
            return (jnp.where(j < lo, 0, jnp.where(j >= hi, nk - 1, k)), jnp.clip(j - lo, 0, hi - lo - 1))
        return pl.BlockSpec((tk, tn), index)

    return pl.pallas_call(
        body,
        name="grad_w_in",
        grid=(IN_WIDTH // tn, nk),
        in_specs=[pl.BlockSpec((m, tk), lambda j, k: (0, k))]
        + [seg_spec(bounds[i], bounds[i + 1]) for i in range(len(SEG_WIDTHS))],
        out_specs=pl.BlockSpec((m, tn), lambda j, k: (0, _w_in_tile(j))),
        out_shape=jax.ShapeDtypeStruct((m, IN_WIDTH), F32),
        compiler_params=_cparams(("arbitrary", "arbitrary")),
    )(h_t, *segs)


EXCHANGE_IN_PIECES = 8
EXCHANGE_PIECES = EXCHANGE_IN_PIECES + 3


def _exchange_copies(sin_ref, ssq_ref, got_in, got_sq, send_sems, recv_sems):
    _, _, c, chips = _position()
    rows = HALF_IN // EXCHANGE_IN_PIECES
    copies = []
    for k, (px, py) in enumerate(chips):
        chip = 2 * px + py
        for p in range(EXCHANGE_PIECES):
            if p < EXCHANGE_IN_PIECES:
                src, dst = sin_ref.at[chip, pl.ds(p * rows, rows), :], got_in.at[k, pl.ds(p * rows, rows), :]
            else:
                src, dst = ssq_ref.at[p - EXCHANGE_IN_PIECES, chip], got_sq.at[k, p - EXCHANGE_IN_PIECES]
            copies.append(_remote(src, dst, send_sems.at[k, p], recv_sems.at[k, p], (px, py, c)))
    return copies


def _dx(segs, w_all, x, norm_g, dout, s_in, s_sq):
    s_len = x.shape[0]
    ts = min(1024, s_len)
    tk = SEG_TILE
    nk = IN_WIDTH // tk
    ns = s_len // ts
    bounds = SEG_BOUNDS
    n_seg = len(SEG_WIDTHS)

    def body(*refs):
        seg_refs = refs[:n_seg]
        w_ref, x_ref, g_ref, dout_ref, sin_ref, ssq_ref, gx_ref, dg_ref, got_in, got_sq, acc, send_sems, recv_sems = refs[n_seg:]
        s, k = pl.program_id(0), pl.program_id(1)

        @pl.when((s == 0) & (k == 0))
        def _():
            dg_ref[...] = jnp.zeros_like(dg_ref)
            for cp in _exchange_copies(sin_ref, ssq_ref, got_in, got_sq, send_sems, recv_sems):
                cp.start()

        @pl.when(k == 0)
        def _():
            acc[...] = jnp.zeros_like(acc)

        for i, ref in enumerate(seg_refs):
            @pl.when((k >= bounds[i]) & (k < bounds[i + 1]))
            def _(ref=ref):
                acc[...] += _dot_nt(ref[...], w_ref[...])

        @pl.when(k == nk - 1)
        def _():
            dh = acc[...]
            xv = x_ref[...]
            r = lax.rsqrt(jnp.mean(xv * xv, axis=-1, keepdims=True) + RMS_EPS)
            xn = xv * r
            dg_ref[...] += jnp.sum(dh * xn, axis=0, keepdims=True)
            dxn = dh * g_ref[...]
            gx_ref[...] = r * (dxn - xn * jnp.mean(dxn * xn, axis=-1, keepdims=True)) + dout_ref[...]

        @pl.when((s == ns - 1) & (k == nk - 1))
        def _():
            for cp in _exchange_copies(sin_ref, ssq_ref, got_in, got_sq, send_sems, recv_sems):
                cp.wait()

    def seg_spec(lo, hi):
        return pl.BlockSpec((ts, tk), lambda s, k: (s, jnp.clip(k - lo, 0, hi - lo - 1)))

    row_tile = pl.BlockSpec((ts, D_MODEL), lambda s, k: (s, 0))
    vec = pl.BlockSpec((1, D_MODEL), lambda s, k: (0, 0))
    return pl.pallas_call(
        body,
        name="dx",
        grid=(ns, nk),
        in_specs=[seg_spec(bounds[i], bounds[i + 1]) for i in range(n_seg)] + [
            pl.BlockSpec((D_MODEL, tk), lambda s, k: (0, _w_in_tile(k))),
            row_tile, vec, row_tile, ANY, ANY,
        ],
        out_specs=[row_tile, vec, ANY, ANY],
        out_shape=[jax.ShapeDtypeStruct((s_len, D_MODEL), F32), jax.ShapeDtypeStruct((1, D_MODEL), F32),
                   jax.ShapeDtypeStruct((3, HALF_IN, W_IN_SHARD), WIRE),
                   jax.ShapeDtypeStruct((3, 3, HALF_SQ, D_MODEL), WIRE)],
        scratch_shapes=[pltpu.VMEM((ts, D_MODEL), F32),
                        pltpu.SemaphoreType.DMA((3, EXCHANGE_PIECES)), pltpu.SemaphoreType.DMA((3, EXCHANGE_PIECES))],
        compiler_params=_cparams(("arbitrary", "arbitrary"), vmem=VMEM_LIMIT_DX),
    )(*segs, w_all, x, norm_g, dout, s_in, s_sq)


def _local_grads(x, target, proj, h_t, qkv, b_gate, lbl, hg_gain, final_g, w_sb, w_hg, w_out):
    sb_o, sb_o_fine = _sb_fwd(qkv)
    hg_o, states = _hg_fwd(proj, lbl)
    (dout, d_sbo, d_hgo, d_mid, a_sb, du_sb, a_hg, du_hg, y, doutb,
     loss, d_fg, d_bg, d_hgn) = _mid(proj, sb_o, hg_o, x, target, b_gate, hg_gain, final_g, w_sb, w_hg, w_out)
    g_w_sb = _grad_square(a_sb, du_sb, "grad_w_sb")
    g_w_hg = _grad_square(a_hg, du_hg, "grad_w_hg")
    g_w_out = _grad_square(y, doutb, "grad_w_out")
    d_q, d_k, d_v = _sb_bwd(qkv, sb_o_fine, d_sbo)
    d_hg, d_lb = _hg_bwd(proj, lbl, states, d_hgo)
    segs = (d_q, d_k, d_v, d_mid, d_hg)
    g_w_in = _grad_w_in(h_t, segs)
    return g_w_in, g_w_sb, g_w_hg, g_w_out, segs, dout, loss, d_bg, d_lb, d_hgn, d_fg


ANY = pl.BlockSpec(memory_space=pl.ANY)
WIRE = BF16
HALF_IN = D_MODEL // 2
HALF_SQ = ROW_SHARD // 2


def _position():
    x, y, c = lax.axis_index("x"), lax.axis_index("y"), lax.axis_index("c")
    chips = [(1 - x, y), (x, 1 - y), (1 - x, 1 - y)]
    return x, y, c, chips


def _remote(src, dst, send_sem, recv_sem, to):
    return pltpu.make_async_remote_copy(src_ref=src, dst_ref=dst, send_sem=send_sem, recv_sem=recv_sem,
                                        device_id=to, device_id_type=MESH)


PROJ_TILE = 1280
F32_FROM_TILE = 2
BF16_TO_TILE = 2
W_LOAD_PIECES = 8


def _gather_inproj(idx, h, w_in_b, w_sq_b):
    s_len = h.shape[0]
    ts = min(1024, s_len)
    ns = s_len // ts
    per = W_IN_SHARD // PROJ_TILE
    n_in = 4
    n_piece = n_in + 3
    rows = HALF_IN // n_in

    def chip_at(r, me):
        return me ^ jnp.where(r == 1, 2, jnp.where(r == 2, 1, jnp.where(r == 3, 3, 0)))

    def body(idx_ref, h_ref, win_ref, wsqb_ref, proj_ref, qkv_ref, wall_ref, wsq_ref, wbuf, send_sems, recv_sems, w_sems):
        r, t, s = pl.program_id(0), pl.program_id(1), pl.program_id(2)
        x, y, c, chips = _position()
        me = 2 * x + y
        sibling = (x, y, 1 - c)
        first = (t == 0) & (s == 0)

        def src_piece(p):
            if p < n_in:
                return win_ref.at[pl.ds(c * HALF_IN + p * rows, rows), :]
            return wsqb_ref.at[p - n_in, pl.ds(c * HALF_SQ, HALF_SQ), :]

        def piece(p, chip, core):
            if p < n_in:
                cols = pl.ds(pl.multiple_of(chip * W_IN_SHARD, W_IN_SHARD), W_IN_SHARD)
                return wall_ref.at[pl.ds(core * HALF_IN + p * rows, rows), cols]
            return wsq_ref.at[p - n_in, chip, pl.ds(core * HALF_SQ, HALF_SQ), :]

        def send(k, p):
            px, py = chips[k]
            return _remote(src_piece(p), piece(p, me, c), send_sems.at[k, p], recv_sems.at[k, p], (px, py, c))

        def forward(k, p, core):
            px, py = chips[k]
            got = piece(p, 2 * px + py, core)
            return _remote(got, got, send_sems.at[3 + k, p], recv_sems.at[3 + k, p], sibling)

        @pl.when((r == 0) & first)
        def _():
            for k in range(2):
                for p in range(n_piece):
                    send(k, p).start()

        for k in range(3):
            @pl.when((r == k + 1) & first)
            def _(k=k):
                px, py = chips[k]
                for p in range(n_piece):
                    got = piece(p, 2 * px + py, c)
                    _remote(got, got, send_sems.at[k, p], recv_sems.at[k, p], (px, py, c)).wait_recv()
                    forward(k, p, c).start()
                if k == 0:
                    for p in range(n_piece):
                        send(2, p).start()
                for p in range(n_piece):
                    forward(k, p, 1 - c).wait_recv()

        def tile_loads(slot, own):
            col = slot * PROJ_TILE
            if not own:
                col = pl.multiple_of(chip_at(r, me) * W_IN_SHARD + col, PROJ_TILE)
            src = win_ref if own else wall_ref
            part = D_MODEL // W_LOAD_PIECES
            return [pltpu.make_async_copy(src.at[pl.ds(q * part, part), pl.ds(col, PROJ_TILE)],
                                          wbuf.at[slot, pl.ds(q * part, part), :], w_sems.at[slot, q])
                    for q in range(W_LOAD_PIECES)]

        for own in (True, False):
            @pl.when(first & ((r == 0) if own else (r > 0)))
            def _(own=own):
                for slot in range(per):
                    for cp in tile_loads(slot, own):
                        cp.start()
                for cp in tile_loads(0, own):
                    cp.wait()

            @pl.when((t > 0) & (s == 0) & ((r == 0) if own else (r > 0)))
            def _(own=own):
                for cp in tile_loads(1, own):
                    cp.wait()

        tile_now = per * chip_at(r, me) + t
        want_f32, want_bf16 = tile_now >= F32_FROM_TILE, tile_now <= BF16_TO_TILE

        @pl.when(want_f32 & jnp.logical_not(want_bf16))
        def _():
            proj_ref[...] = _dot(h_ref[...], wbuf[t])

        @pl.when(want_bf16 & jnp.logical_not(want_f32))
        def _():
            qkv_ref[...] = _dot(h_ref[...], wbuf[t]).astype(BF16)

        @pl.when(want_f32 & want_bf16)
        def _():
            p = _dot(h_ref[...], wbuf[t])
            proj_ref[...] = p
            qkv_ref[...] = p.astype(BF16)

        @pl.when((r == 3) & (t == per - 1) & (s == ns - 1))
        def _():
            for k in range(3):
                for p in range(n_piece):
                    send(k, p).wait_send()
                    forward(k, p, c).wait_send()

    def out_index(wanted):
        order = [0, 2, 1, 3]
        table = []
        for chip in range(N_CHIPS):
            tiles = [per * (chip ^ order[q // per]) + q % per for q in range(N_CHIPS * per)]
            row = []
            for q, tile in enumerate(tiles):
                if wanted(tile):
                    row.append((tile, None))
                    continue
                before = [u for u in tiles[:q] if wanted(u)]
                after = [u for u in tiles[q:] if wanted(u)]
                row.append((before[-1], ns - 1) if before else (after[0], 0))
            table.append(row)

        def index(r, t, s, idx):
            q = r * per + t
            col, fixed_s = jnp.int32(0), jnp.int32(-1)
            for chip in range(N_CHIPS):
                for pos, (tile, hold) in enumerate(table[chip]):
                    here = (idx[0] == chip) & (q == pos)
                    col = jnp.where(here, tile, col)
                    fixed_s = jnp.where(here, -1 if hold is None else hold, fixed_s)
            return jnp.where(fixed_s < 0, s, fixed_s), col

        return index

    grid_spec = pltpu.PrefetchScalarGridSpec(
        num_scalar_prefetch=1,
        grid=(N_CHIPS, per, ns),
        in_specs=[pl.BlockSpec((ts, D_MODEL), lambda r, t, s, idx: (s, 0)), ANY, ANY],
        out_specs=[pl.BlockSpec((ts, PROJ_TILE), out_index(lambda tile: tile >= F32_FROM_TILE)),
                   pl.BlockSpec((ts, PROJ_TILE), out_index(lambda tile: tile <= BF16_TO_TILE)),
                   ANY, ANY],
        scratch_shapes=[pltpu.VMEM((per, D_MODEL, PROJ_TILE), BF16),
                        pltpu.SemaphoreType.DMA((6, n_piece)), pltpu.SemaphoreType.DMA((6, n_piece)),
                        pltpu.SemaphoreType.DMA((per, W_LOAD_PIECES))],
    )
    return pl.pallas_call(
        body,
        name="gather_inproj",
        grid_spec=grid_spec,
        out_shape=[jax.ShapeDtypeStruct((s_len, IN_WIDTH), F32),
                   jax.ShapeDtypeStruct((s_len, IN_WIDTH), BF16),
                   jax.ShapeDtypeStruct((D_MODEL, IN_WIDTH), BF16),
                   jax.ShapeDtypeStruct((3, N_CHIPS, ROW_SHARD, D_MODEL), BF16)],
        compiler_params=_cparams(("arbitrary", "arbitrary", "arbitrary")),
    )(idx, h, w_in_b, w_sq_b)


def _place_own(idx, w_in_b, w_sq_b, w_all, wsq):
    n = 4
    r_in, r_sq = D_MODEL // n, ROW_SHARD // n

    def body(idx_ref, win_ref, wsq_ref, w_all_in, wsq_in, w_all_out, wsq_out):
        w_all_out[...] = win_ref[...]
        wsq_out[:, 0] = wsq_ref[...]

    grid_spec = pltpu.PrefetchScalarGridSpec(
        num_scalar_prefetch=1,
        grid=(n,),
        in_specs=[pl.BlockSpec((r_in, W_IN_SHARD), lambda r, idx: (r, 0)),
                  pl.BlockSpec((3, r_sq, D_MODEL), lambda r, idx: (0, r, 0)), ANY, ANY],
        out_specs=[pl.BlockSpec((r_in, W_IN_SHARD), lambda r, idx: (r, idx[0])),
                   pl.BlockSpec((3, 1, r_sq, D_MODEL), lambda r, idx: (0, idx[0], r, 0))],
    )
    return pl.pallas_call(
        body,
        name="place_own",
        grid_spec=grid_spec,
        out_shape=[jax.ShapeDtypeStruct(w_all.shape, BF16), jax.ShapeDtypeStruct(wsq.shape, BF16)],
        input_output_aliases={3: 0, 4: 1},
        compiler_params=_cparams(("arbitrary",)),
    )(idx, w_in_b, w_sq_b, w_all, wsq)


def _swap_halves(g_in, g_sq):
    n_in = 16
    n_piece = n_in + 3 * N_CHIPS
    rows = HALF_IN // n_in

    def body(gin_ref, gsq_ref, got_in, got_sq, send_sems, recv_sems):
        x, y, c, _ = _position()
        sibling = (x, y, 1 - c)

        def src_piece(p):
            if p < n_in:
                return gin_ref.at[pl.ds((1 - c) * HALF_IN + p * rows, rows), :]
            a, chip = divmod(p - n_in, N_CHIPS)
            return gsq_ref.at[a, chip, pl.ds((1 - c) * HALF_SQ, HALF_SQ), :]

        def dst_piece(p):
            if p < n_in:
                return got_in.at[pl.ds(p * rows, rows), :]
            a, chip = divmod(p - n_in, N_CHIPS)
            return got_sq.at[a, chip]

        out = [_remote(src_piece(p), dst_piece(p), send_sems.at[p], recv_sems.at[p], sibling) for p in range(n_piece)]
        for cp in out:
            cp.start()
        for cp in out:
            cp.wait()

    return pl.pallas_call(
        body,
        name="swap_halves",
        in_specs=[ANY, ANY],
        out_specs=[ANY, ANY],
        out_shape=[jax.ShapeDtypeStruct((HALF_IN, IN_WIDTH), F32),
                   jax.ShapeDtypeStruct((3, N_CHIPS, HALF_SQ, D_MODEL), F32)],
        scratch_shapes=[pltpu.SemaphoreType.DMA((n_piece,))] * 2,
    )(g_in, g_sq)


def _join_halves(r_in, r_sq):
    n_in = 16
    n_piece = n_in + 3
    rows = HALF_IN // n_in

    def body(in_alias, sq_alias, full_in, full_sq, send_sems, recv_sems):
        del in_alias, sq_alias
        x, y, c, _ = _position()
        sibling = (x, y, 1 - c)

        def piece(p, core):
            if p < n_in:
                return full_in.at[pl.ds(core * HALF_IN + p * rows, rows), :]
            return full_sq.at[p - n_in, pl.ds(core * HALF_SQ, HALF_SQ), :]

        out = [_remote(piece(p, c), piece(p, c), send_sems.at[p], recv_sems.at[p], sibling) for p in range(n_piece)]
        for cp in out:
            cp.start()
        for p in range(n_piece):
            _remote(piece(p, 1 - c), piece(p, 1 - c), send_sems.at[p], recv_sems.at[p], sibling).wait_recv()
        for cp in out:
            cp.wait_send()

    return pl.pallas_call(
        body,
        name="join_halves",
        in_specs=[ANY, ANY],
        out_specs=[ANY, ANY],
        out_shape=[jax.ShapeDtypeStruct((D_MODEL, W_IN_SHARD), F32),
                   jax.ShapeDtypeStruct((3, ROW_SHARD, D_MODEL), F32)],
        input_output_aliases={0: 0, 1: 1},
        scratch_shapes=[pltpu.SemaphoreType.DMA((n_piece,)), pltpu.SemaphoreType.DMA((n_piece,))],
    )(r_in, r_sq)


SMALL_ROWS = 56
N_DEV = 8


def _sum_small(part):
    def body(part_ref, out_ref, slots, send_sems, recv_sems):
        x, y, c, _ = _position()
        me = 4 * x + 2 * y + c
        slots[me] = part_ref[...]
        out = []
        for r in range(1, N_DEV):
            rx, ry, rc = (r >> 2) & 1, (r >> 1) & 1, r & 1
            to = (1 - x if rx else x, 1 - y if ry else y, 1 - c if rc else c)
            out.append(_remote(part_ref, slots.at[me], send_sems.at[r - 1], recv_sems.at[r - 1], to))
        for cp in out:
            cp.start()
        for r in range(1, N_DEV):
            _remote(part_ref, slots.at[me ^ r], send_sems.at[r - 1], recv_sems.at[r - 1], (x, y, c)).wait_recv()
        for cp in out:
            cp.wait_send()
        total = slots[0]
        for d in range(1, N_DEV):
            total = total + slots[d]
        out_ref[...] = total

    vmem = pl.BlockSpec(memory_space=pltpu.VMEM)
    return pl.pallas_call(
        body,
        name="sum_small",
        in_specs=[vmem],
        out_specs=vmem,
        out_shape=jax.ShapeDtypeStruct((SMALL_ROWS, HEAD_DIM), F32),
        scratch_shapes=[pltpu.VMEM((N_DEV, SMALL_ROWS, HEAD_DIM), F32),
                        pltpu.SemaphoreType.DMA((N_DEV - 1,)), pltpu.SemaphoreType.DMA((N_DEV - 1,))],
    )(part)


def _prefetch_call(body, name, idx, grid, in_specs, out_specs, out_shape, args):
    grid_spec = pltpu.PrefetchScalarGridSpec(num_scalar_prefetch=1, grid=grid, in_specs=in_specs, out_specs=out_specs)
    return pl.pallas_call(body, name=name, grid_spec=grid_spec, out_shape=out_shape,
                          compiler_params=_cparams(("arbitrary",) * len(grid)))(idx, *args)


def _sum_a_in(idx, g_in, got_in):
    tr = 128
    nr = HALF_IN // tr

    def body(idx_ref, a_ref, b_ref, o_ref):
        o_ref[0] = (a_ref[...] + b_ref[...]).astype(WIRE)

    return _prefetch_call(
        body, "sum_a_in", idx, (N_CHIPS, nr),
        [pl.BlockSpec((tr, W_IN_SHARD), lambda j, r, idx: (idx[1] * nr + r, j)),
         pl.BlockSpec((tr, W_IN_SHARD), lambda j, r, idx: (r, j))],
        pl.BlockSpec((1, tr, W_IN_SHARD), lambda j, r, idx: (j, r, 0)),
        jax.ShapeDtypeStruct((N_CHIPS, HALF_IN, W_IN_SHARD), WIRE), (g_in, got_in))


def _sum_a_sq(idx, g_sq, got_sq):
    blk = (1, 1, HALF_SQ, D_MODEL)

    def body(idx_ref, a_ref, b_ref, o_ref):
        o_ref[...] = (a_ref[...] + b_ref[...]).astype(WIRE)

    return _prefetch_call(
        body, "sum_a_sq", idx, (3, N_CHIPS),
        [pl.BlockSpec(blk, lambda a, j, idx: (a, j, idx[1], 0)), pl.BlockSpec(blk, lambda a, j, idx: (a, j, 0, 0))],
        pl.BlockSpec(blk, lambda a, j, idx: (a, j, 0, 0)),
        jax.ShapeDtypeStruct((3, N_CHIPS, HALF_SQ, D_MODEL), WIRE), (g_sq, got_sq))


def _sum_b_in(idx, s_in, got_in):
    tr = 128
    nr = HALF_IN // tr

    def body(idx_ref, a_ref, b_ref, o_ref):
        o_ref[...] = ((a_ref[0].astype(F32) + b_ref[0].astype(F32)) + b_ref[1].astype(F32)) + b_ref[2].astype(F32)

    return _prefetch_call(
        body, "sum_b_in", idx, (nr,),
        [pl.BlockSpec((1, tr, W_IN_SHARD), lambda r, idx: (idx[0], r, 0)),
         pl.BlockSpec((3, tr, W_IN_SHARD), lambda r, idx: (0, r, 0))],
        pl.BlockSpec((tr, W_IN_SHARD), lambda r, idx: (idx[1] * nr + r, 0)),
        jax.ShapeDtypeStruct((D_MODEL, W_IN_SHARD), F32), (s_in, got_in))


def _sum_b_sq(idx, s_sq, got_sq):
    def body(idx_ref, a_ref, b_ref, o_ref):
        o_ref[0] = ((a_ref[0, 0].astype(F32) + b_ref[0, 0].astype(F32)) + b_ref[1, 0].astype(F32)) + b_ref[2, 0].astype(F32)

    return _prefetch_call(
        body, "sum_b_sq", idx, (3,),
        [pl.BlockSpec((1, 1, HALF_SQ, D_MODEL), lambda a, idx: (a, idx[0], 0, 0)),
         pl.BlockSpec((3, 1, HALF_SQ, D_MODEL), lambda a, idx: (0, a, 0, 0))],
        pl.BlockSpec((1, HALF_SQ, D_MODEL), lambda a, idx: (a, idx[1], 0)),
        jax.ShapeDtypeStruct((3, ROW_SHARD, D_MODEL), F32), (s_sq, got_sq))


def _adamw_math(w, g, m, v):
    m = ADAM_B1 * m + (1.0 - ADAM_B1) * g
    v = ADAM_B2 * v + (1.0 - ADAM_B2) * (g * g)
    m_hat = m / (1.0 - ADAM_B1 ** ADAM_STEP)
    v_hat = v / (1.0 - ADAM_B2 ** ADAM_STEP)
    delta = -ADAM_LR * (m_hat / (jnp.sqrt(v_hat) + ADAM_EPS) + ADAM_WD * w)
    return delta, m, v


def _adamw(w, g, m, v, name):
    rows, cols = w.shape
    tr = min(128, rows)

    def body(w_ref, g_ref, m_ref, v_ref, d_ref, nm_ref, nv_ref):
        d_ref[...], nm_ref[...], nv_ref[...] = _adamw_math(w_ref[...], g_ref[...], m_ref[...], v_ref[...])

    spec = pl.BlockSpec((tr, cols), lambda r: (r, 0))
    return pl.pallas_call(
        body,
        name=name,
        grid=(rows // tr,),
        in_specs=[spec] * 4,
        out_specs=[spec] * 3,
        out_shape=[jax.ShapeDtypeStruct((rows, cols), F32)] * 3,
        compiler_params=_cparams(("arbitrary",)),
    )(w, g, m, v)


def _adamw_small(sums, w, m, v):
    def body(s_ref, w_ref, m_ref, v_ref, loss_ref, g_ref, d_ref, nm_ref, nv_ref):
        s = s_ref[...]
        w = w_ref[...]
        loss_ref[...] = s[0:1, 0:1]
        l0, l1 = w[24:32], w[32:40]
        mx = jnp.maximum(l0, l1)
        e0, e1 = jnp.exp(l0 - mx), jnp.exp(l1 - mx)
        p0, p1 = e0 / (e0 + e1), e1 / (e0 + e1)
        d_lb = s[32:40]
        g = jnp.concatenate([s[8:16], s[16:32], d_lb * p0 * (1.0 - p0), -d_lb * p0 * p1, s[40:48], s[48:56]], axis=0)
        g_ref[...] = g
        d_ref[...], nm_ref[...], nv_ref[...] = _adamw_math(w, g, m_ref[...], v_ref[...])

    packed = jax.ShapeDtypeStruct((SMALL_ROWS, HEAD_DIM), F32)
    return pl.pallas_call(
        body,
        name="adamw_small",
        out_shape=[jax.ShapeDtypeStruct((1, 1), F32), packed, packed, packed, packed],
    )(sums, w, m, v)


def _pack_small(ng, bg, lbl, hgn, fg):
    return jnp.concatenate([a.reshape(-1, HEAD_DIM) for a in (ng, bg, lbl, hgn, fg)], axis=0)


def _unpack_small(p):
    return (p[0:8].reshape(1, D_MODEL), p[8:24].reshape(1, 2 * D_MODEL), p[24:40].reshape(2, HEADS, HEAD_DIM),
            p[40:48].reshape(1, HEADS, HEAD_DIM), p[48:56].reshape(D_MODEL))


def kernel(x, norm_g, w_in, b_gate, lb_logits, hg_norm_g, w_sb_proj, w_hg_proj, w_out, final_norm_g, loss_target, m_norm_g, m_w_in, m_b_gate, m_lb_logits, m_hg_norm_g, m_w_sb_proj, m_w_hg_proj, m_w_out, m_final_norm_g, v_norm_g, v_w_in, v_b_gate, v_lb_logits, v_hg_norm_g, v_w_sb_proj, v_w_hg_proj, v_w_out, v_final_norm_g):
    s_len = x.shape[1]
    w_sq = jnp.stack([w_sb_proj[0], w_hg_proj[0], w_out[0]])
    idx = jnp.stack([2 * lax.axis_index("x") + lax.axis_index("y"), lax.axis_index("c")]).astype(jnp.int32)
    w_in_b, w_sq_b = w_in[0].astype(BF16), w_sq.astype(BF16)
    h, h_t = _prenorm(x[0], norm_g)
    proj, qkv, w_all, wsq = _gather_inproj(idx, h, w_in_b, w_sq_b)
    w_all, wsq = _place_own(idx, w_in_b, w_sq_b, w_all, wsq)
    wsq = wsq.reshape(3, D_MODEL, D_MODEL)

    (g_in, g_sb, g_hg, g_out, segs, dout, loss, d_bg, d_lb, d_hgn, d_fg) = _local_grads(
        x[0], loss_target[0], proj, h_t, qkv, b_gate, lb_logits.reshape(2, D_MODEL), hg_norm_g.reshape(1, D_MODEL),
        final_norm_g.reshape(1, D_MODEL), wsq[0], wsq[1], wsq[2])

    g_sq = jnp.stack([g_sb, g_hg, g_out]).reshape(3, N_CHIPS, ROW_SHARD, D_MODEL)
    got_in, got_sq = _swap_halves(g_in, g_sq)
    s_in, s_sq = _sum_a_in(idx, g_in, got_in), _sum_a_sq(idx, g_sq, got_sq)
    grad_x, d_ng, got_in, got_sq = _dx(segs, w_all, x[0], norm_g, dout, s_in, s_sq)
    grad_in, grad_sq = _join_halves(_sum_b_in(idx, s_in, got_in), _sum_b_sq(idx, s_sq, got_sq))

    d_in, nm_in, nv_in = _adamw(w_in[0], grad_in, m_w_in[0], v_w_in[0], "adamw_in")
    flat = lambda a, b, c: jnp.concatenate([a[0], b[0], c[0]], axis=0)
    d_sq, nm_sq, nv_sq = _adamw(flat(w_sb_proj, w_hg_proj, w_out), grad_sq.reshape(3 * ROW_SHARD, D_MODEL),
                                flat(m_w_sb_proj, m_w_hg_proj, m_w_out), flat(v_w_sb_proj, v_w_hg_proj, v_w_out),
                                "adamw_sq")

    pad = jnp.zeros((8, HEAD_DIM), F32).at[0, 0].set(loss[0, 0])
    part = jnp.concatenate([pad] + [a.reshape(-1, HEAD_DIM) for a in (d_ng, d_bg, d_lb, d_hgn, d_fg)], axis=0)
    sums = _sum_small(part)
    loss_out, g_sm, d_sm, nm_sm, nv_sm = _adamw_small(
        sums, _pack_small(norm_g, b_gate, lb_logits, hg_norm_g, final_norm_g),
        _pack_small(m_norm_g, m_b_gate, m_lb_logits, m_hg_norm_g, m_final_norm_g),
        _pack_small(v_norm_g, v_b_gate, v_lb_logits, v_hg_norm_g, v_final_norm_g))

    def big(t_in, t_sq):
        sq = t_sq.reshape(3, 1, ROW_SHARD, D_MODEL)
        return t_in[None], sq[0], sq[1], sq[2]

    def order(small, in_, sb, hg, out):
        ng, bg, lbl, hgn, fg = small
        return [ng, in_, bg, lbl, hgn, sb, hg, out, fg]

    outs = [loss_out[0, 0], grad_x[None]]
    for small, (t_in, t_sq) in ((g_sm, (grad_in, grad_sq)), (d_sm, (d_in, d_sq)), (nm_sm, (nm_in, nm_sq)), (nv_sm, (nv_in, nv_sq))):
        outs += order(_unpack_small(small), *big(t_in, t_sq))
    return tuple(outs)
```

```python
import functools

import jax
import jax.numpy as jnp
from jax import lax
from jax.experimental import pallas as pl
from jax.experimental.pallas import tpu as pltpu

F32 = jnp.float32
BF16 = jnp.bfloat16

D_MODEL = 1024
HEADS = 8
HEAD_DIM = 128
IN_WIDTH = 10240
N_CHIPS = 4
W_IN_SHARD = IN_WIDTH // N_CHIPS
ROW_SHARD = D_MODEL // N_CHIPS
RMS_EPS = 1e-6

OFF_SB_Q, OFF_SB_K, OFF_SB_V, OFF_SB_Z = 0, 1024, 2048, 3072
OFF_HG_Q, OFF_HG_F, OFF_HG_I, OFF_HG_Z, OFF_GATE = 4096, 5120, 6144, 7168, 8192

SB_BLOCK = 256
SB_FWD_HEADS = 4
SB_BWD_HEADS = 2
SB_ROWS = 256
SB_DEAD = -110.0
SB_GONE = -1e30
HG_CHUNK = 32
HG_PAIR = 2 * HG_CHUNK
HG_STEP = 256
HG_MID = HG_CHUNK // 2 - 1

ADAM_LR, ADAM_B1, ADAM_B2, ADAM_EPS, ADAM_WD, ADAM_STEP = 0.001, 0.9, 0.999, 1e-08, 0.01, 10

VMEM_LIMIT = 56 * 1024 * 1024
VMEM_LIMIT_DX = 60 * 1024 * 1024

MESH = pl.DeviceIdType.MESH


def _cparams(sem, vmem=VMEM_LIMIT):
    return pltpu.CompilerParams(dimension_semantics=sem, vmem_limit_bytes=vmem)


def _dot(a, b):
    return jnp.dot(a, b, preferred_element_type=F32)


def _dot_nt(a, b):
    return lax.dot_general(a, b, (((1,), (1,)), ((), ())), preferred_element_type=F32)


def _dot_tn(a, b):
    return lax.dot_general(a, b, (((0,), (0,)), ((), ())), preferred_element_type=F32)


def _split_dot(x, tri):
    hi = x.astype(BF16)
    lo = (x - hi.astype(F32)).astype(BF16)
    both = _dot(jnp.concatenate([hi, lo], axis=0), tri)
    return both[: x.shape[0]] + both[x.shape[0] :]


def _split_dot_left(tri, x):
    hi = x.astype(BF16)
    lo = (x - hi.astype(F32)).astype(BF16)
    return _dot(tri, hi) + _dot(tri, lo)


def _sigmoid(x):
    return 1.0 / (1.0 + jnp.exp(-x))


def _prenorm(x, norm_g):
    s_len = x.shape[0]
    ts = min(1024, s_len)

    def body(x_ref, g_ref, h_ref, ht_ref):
        xv = x_ref[...]
        r = lax.rsqrt(jnp.mean(xv * xv, axis=-1, keepdims=True) + RMS_EPS)
        hv = (xv * r) * g_ref[...]
        h_ref[...] = hv.astype(BF16)
        ht_ref[...] = hv.T.astype(BF16)

    return pl.pallas_call(
        body,
        name="prenorm",
        grid=(s_len // ts,),
        in_specs=[pl.BlockSpec((ts, D_MODEL), lambda s: (s, 0)), pl.BlockSpec((1, D_MODEL), lambda s: (0, 0))],
        out_specs=[pl.BlockSpec((ts, D_MODEL), lambda s: (s, 0)), pl.BlockSpec((D_MODEL, ts), lambda s: (0, s))],
        out_shape=[jax.ShapeDtypeStruct((s_len, D_MODEL), BF16), jax.ShapeDtypeStruct((D_MODEL, s_len), BF16)],
        compiler_params=_cparams(("arbitrary",)),
    )(x, norm_g)


def _sb_scores(qb, kb, causal, tri_excl, diag):
    z = _dot_nt(qb, kb) * HEAD_DIM ** -0.5
    ls_pos = jnp.minimum(z, 0.0) - jnp.log1p(jnp.exp(-jnp.abs(z)))
    log_not = ls_pos - z
    log_not_m = jnp.where(causal, log_not, 0.0) if diag else log_not
    return ls_pos, log_not, log_not_m, _split_dot(log_not_m, tri_excl)


def _sb_weights(ls_pos, suffix, carry, causal, diag):
    surv = suffix + carry
    w = jnp.exp(ls_pos + surv)
    return surv, (jnp.where(causal, w, 0.0) if diag else w)


def _sb_specs(s_len, blk, heads):
    width = heads * HEAD_DIM

    def blk_spec(off):
        return pl.BlockSpec((blk, width), lambda h, i: (i, off // width + h))

    def head_spec(off, buffers=2):
        return pl.BlockSpec((s_len, width), lambda h, i: (0, off // width + h), pipeline_mode=pl.Buffered(buffers))

    tri_spec = pl.BlockSpec((2, blk, blk), lambda h, i: (0, 0, 0))
    return blk_spec, head_spec, tri_spec


def _head_cols(p):
    return slice(p * HEAD_DIM, (p + 1) * HEAD_DIM)


def _sb_chains(blk, heads):
    rows = min(SB_ROWS, blk)
    return [(p, a) for p in range(heads) for a in range(blk // rows)], rows


def _sb_masks(blk, rows):
    row = lax.broadcasted_iota(jnp.int32, (rows, blk), 0)
    col = lax.broadcasted_iota(jnp.int32, (rows, blk), 1)
    return [row + a * rows > col for a in range(blk // rows)]


def _sb_tri(blk):
    row = lax.broadcasted_iota(jnp.int32, (blk, blk), 0)
    col = lax.broadcasted_iota(jnp.int32, (blk, blk), 1)
    return jnp.stack([row > col, row >= col]).astype(BF16)


def _sb_alive(st, n_chain):
    alive = functools.reduce(jnp.maximum, [st[1 + 3 * c] for c in range(n_chain)])
    return jnp.max(alive) > SB_DEAD


def _sb_fwd(qkv):
    s_len = qkv.shape[0]
    blk = min(SB_BLOCK, s_len)
    nq = s_len // blk
    chains, rows = _sb_chains(blk, SB_FWD_HEADS)

    def body(q_ref, k_ref, v_ref, tri_ref, o_ref, of_ref):
        i = pl.program_id(1)
        causal, tri_excl = _sb_masks(blk, rows), tri_ref[0]

        def tiles(specs, st):
            pre = []
            for j, diag, _ in specs:
                start = pl.multiple_of(j * blk, blk)
                for p, a in chains:
                    kb = k_ref[pl.ds(start, blk), _head_cols(p)]
                    qb = q_ref[a * rows : (a + 1) * rows, _head_cols(p)]
                    pre.append(_sb_scores(qb, kb, causal[a], tri_excl, diag) + (v_ref[pl.ds(start, blk), _head_cols(p)],))
            for t, (j, diag, valid) in enumerate(specs):
                new = []
                for c, (p, a) in enumerate(chains):
                    carry, acc, acc_lo = st[3 * c : 3 * c + 3]
                    if valid is not None:
                        carry = jnp.where(valid, carry, SB_GONE)
                    ls_pos, _, log_not_m, suffix, vb = pre[t * len(chains) + c]
                    surv, w = _sb_weights(ls_pos, suffix, carry, causal[a], diag)
                    wb = w.astype(BF16)
                    w_lo = (w - wb.astype(F32)).astype(BF16)
                    both = _dot(jnp.concatenate([wb, w_lo], axis=0), vb)
                    new += [surv[:, 0:1] + log_not_m[:, 0:1], acc + both[:rows], acc_lo + both[rows:]]
                st = tuple(new)
            return st

        zero = jnp.zeros((rows, HEAD_DIM), F32)
        st = tiles([(i, True, None), (jnp.maximum(i - 1, 0), False, i >= 1)],
                   (jnp.zeros((rows, 1), F32), zero, zero) * len(chains))

        def more(st):
            return (st[0] < i) & _sb_alive(st, len(chains))

        def step(st):
            return (st[0] + 1,) + tiles([(i - 1 - st[0], False, None)], st[1:])

        st = lax.while_loop(more, step, (1,) + st)[1:]
        for c, (p, a) in enumerate(chains):
            o_ref[a * rows : (a + 1) * rows, _head_cols(p)] = st[3 * c + 1]
            of_ref[a * rows : (a + 1) * rows, _head_cols(p)] = st[3 * c + 1] + st[3 * c + 2]

    blk_spec, head_spec, tri_spec = _sb_specs(s_len, blk, SB_FWD_HEADS)
    return pl.pallas_call(
        body,
        name="sb_fwd",
        grid=(HEADS // SB_FWD_HEADS, nq),
        in_specs=[blk_spec(OFF_SB_Q), head_spec(OFF_SB_K), head_spec(OFF_SB_V), tri_spec],
        out_specs=[blk_spec(0), blk_spec(0)],
        out_shape=[jax.ShapeDtypeStruct((s_len, D_MODEL), F32)] * 2,
        compiler_params=_cparams(("arbitrary", "arbitrary")),
    )(qkv, qkv, qkv, _sb_tri(blk))


def _sb_bwd(qkv, o_fine, d_o):
    s_len = qkv.shape[0]
    blk = min(SB_BLOCK, s_len)
    nq = s_len // blk
    scale = HEAD_DIM ** -0.5
    chains, rows = _sb_chains(blk, SB_BWD_HEADS)

    def body(q_ref, k_ref, v_ref, of_ref, do_ref, tri_ref, dq_ref, dk_ref, dv_ref, dk_acc, dv_acc):
        i = pl.program_id(1)

        @pl.when(i == 0)
        def _():
            dk_acc[...] = jnp.zeros_like(dk_acc)
            dv_acc[...] = jnp.zeros_like(dv_acc)

        dob = do_ref[...].astype(BF16)
        prod = dob.astype(F32) * of_ref[...]
        causal, tri_excl, tri_incl = _sb_masks(blk, rows), tri_ref[0], tri_ref[1]

        def group(x, p, a):
            return x[a * rows : (a + 1) * rows, _head_cols(p)]

        totals = [jnp.sum(group(prod, p, a), axis=-1, keepdims=True) for p, a in chains]

        def tiles(specs, st):
            pre = []
            for j, diag, _ in specs:
                start = pl.multiple_of(j * blk, blk)
                for p, a in chains:
                    kb = k_ref[pl.ds(start, blk), _head_cols(p)]
                    vb = v_ref[pl.ds(start, blk), _head_cols(p)]
                    qb, dob_c = group(q_ref, p, a), group(dob, p, a)
                    pre.append(_sb_scores(qb, kb, causal[a], tri_excl, diag) + (_dot_nt(dob_c, vb), qb, kb, dob_c))
            for t, (j, diag, valid) in enumerate(specs):
                start = pl.multiple_of(j * blk, blk)
                mids = []
                for c, (p, a) in enumerate(chains):
                    c_not = st[3 * c]
                    if valid is not None:
                        c_not = jnp.where(valid, c_not, SB_GONE)
                    ls_pos, _, _, suffix, d_w = pre[t * len(chains) + c][:5]
                    surv, w = _sb_weights(ls_pos, suffix, c_not, causal[a], diag)
                    dlw = d_w * w
                    mids.append((surv, w, dlw, _split_dot(dlw, tri_incl)))
                new = []
                dk_new = [None] * SB_BWD_HEADS
                dv_new = [None] * SB_BWD_HEADS
                for c, (p, a) in enumerate(chains):
                    c_dlw, dq = st[3 * c + 1 : 3 * c + 3]
                    ls_pos, log_not, log_not_m, _, _, qb, kb, dob_c = pre[t * len(chains) + c]
                    surv, w, dlw, suffix = mids[c]
                    d_not = totals[c] - c_dlw - suffix
                    dz = ((dlw + d_not) * jnp.exp(log_not) - d_not) * scale
                    if diag:
                        dz = jnp.where(causal[a], dz, 0.0)
                    if valid is not None:
                        dz = jnp.where(valid, dz, 0.0)
                    dzb = dz.astype(BF16)
                    dk_c, dv_c = _dot_tn(dzb, qb), _dot_tn(w.astype(BF16), dob_c)
                    dk_new[p] = dk_c if dk_new[p] is None else dk_new[p] + dk_c
                    dv_new[p] = dv_c if dv_new[p] is None else dv_new[p] + dv_c
                    new += [surv[:, 0:1] + log_not_m[:, 0:1], c_dlw + suffix[:, 0:1], dq + _dot(dzb, kb)]
                for p in range(SB_BWD_HEADS):
                    dk_acc[pl.ds(start, blk), _head_cols(p)] += dk_new[p]
                    dv_acc[pl.ds(start, blk), _head_cols(p)] += dv_new[p]
                st = tuple(new)
            return st

        zcol = jnp.zeros((rows, 1), F32)
        st = tiles([(i, True, None), (jnp.maximum(i - 1, 0), False, i >= 1)],
                   (zcol, zcol, jnp.zeros((rows, HEAD_DIM), F32)) * len(chains))

        def more(st):
            return (st[0] < i) & _sb_alive(st, len(chains))

        def step(st):
            return (st[0] + 1,) + tiles([(i - 1 - st[0], False, None)], st[1:])

        st = lax.while_loop(more, step, (1,) + st)[1:]
        for c, (p, a) in enumerate(chains):
            dq_ref[a * rows : (a + 1) * rows, _head_cols(p)] = st[3 * c + 2].astype(BF16)

        @pl.when(i == nq - 1)
        def _():
            dk_ref[...] = dk_acc[...].astype(BF16)
            dv_ref[...] = dv_acc[...].astype(BF16)

    blk_spec, head_spec, tri_spec = _sb_specs(s_len, blk, SB_BWD_HEADS)
    width = SB_BWD_HEADS * HEAD_DIM
    return pl.pallas_call(
        body,
        name="sb_bwd",
        grid=(HEADS // SB_BWD_HEADS, nq),
        in_specs=[blk_spec(OFF_SB_Q), head_spec(OFF_SB_K, 1), head_spec(OFF_SB_V, 1), blk_spec(0), blk_spec(0), tri_spec],
        out_specs=[blk_spec(0), head_spec(0), head_spec(0)],
        out_shape=[jax.ShapeDtypeStruct((s_len, D_MODEL), BF16)] * 3,
        scratch_shapes=[pltpu.VMEM((s_len, width), F32), pltpu.VMEM((s_len, width), F32)],
        compiler_params=_cparams(("arbitrary", "arbitrary")),
    )(qkv, qkv, qkv, o_fine, d_o, _sb_tri(blk))


def _hg_lower_bound(lbl_ref):
    l0 = lbl_ref[0:1, :]
    l1 = lbl_ref[1:2, :]
    mx = jnp.maximum(l0, l1)
    e0 = jnp.exp(l0 - mx)
    e1 = jnp.exp(l1 - mx)
    return e0 / (e0 + e1)


def _hg_gates(hq, hf, lb):
    sig_f = _sigmoid(hf)
    f = lb + (1.0 - lb) * sig_f
    g = jnp.log(f)
    kk = 1.0 - f
    sig_q = _sigmoid(hq)
    qq = hq * sig_q
    return qq, kk, g, f, sig_f, sig_q


def _period_bcast(x, r, rows, period):
    w = x.shape[-1]
    x3 = x.reshape(rows // period, period, w)
    return jnp.broadcast_to(x3[:, r : r + 1, :], x3.shape).reshape(rows, w)


def _blockdiag(rows, kind):
    row = lax.broadcasted_iota(jnp.int32, (rows, rows), 0)
    col = lax.broadcasted_iota(jnp.int32, (rows, rows), 1)
    if kind in ("next", "prev"):
        first, second = (row, col) if kind == "next" else (col, row)
        keep = ((row // HG_PAIR) == (col // HG_PAIR)) & (first % HG_PAIR < HG_CHUNK) & (second % HG_PAIR >= HG_CHUNK)
    else:
        keep = (row // HG_CHUNK) == (col // HG_CHUNK)
        if kind == "lower":
            keep = keep & (row >= col)
        elif kind == "upper":
            keep = keep & (row <= col)
    return jnp.where(keep, 1.0, 0.0).astype(BF16)


HG_MATS = ("lower", "upper", "all", "next", "prev")


def _hg_mats(rows):
    return jnp.stack([_blockdiag(rows, kind) for kind in HG_MATS])


def _hg_operands(hq, hf, lb, rows, lower):
    qq, kk, g, f, sig_f, sig_q = _hg_gates(hq, hf, lb)
    cum = _split_dot_left(lower, g)
    mid = _period_bcast(cum, HG_MID, rows, HG_CHUNK)
    last = _period_bcast(cum, HG_CHUNK - 1, rows, HG_CHUNK)
    last0 = _period_bcast(cum, HG_CHUNK - 1, rows, HG_PAIR)
    last1 = _period_bcast(cum, HG_PAIR - 1, rows, HG_PAIR)
    second = (lax.broadcasted_iota(jnp.int32, cum.shape, 0) % HG_PAIR) >= HG_CHUNK
    e = dict(qm=jnp.exp(cum - mid), km=jnp.exp(mid - cum), qd=jnp.exp(cum), kl=jnp.exp(last - cum),
             q_in=jnp.where(second, jnp.exp(last0), 1.0), k_out=jnp.where(second, 1.0, jnp.exp(last1)),
             pair=jnp.exp(last0 + last1))
    v = dict(qm=qq * e["qm"], km=kk * e["km"], qd=qq * e["qd"], kl=kk * e["kl"])
    v["qp"] = v["qd"] * e["q_in"]
    v["kp"] = v["kl"] * e["k_out"]
    return v, e, second, (f, sig_f, sig_q)


def _hg_store_operands(v, second, hi, refs):
    zero = jnp.zeros_like(v["qm"])
    q_cat, k_cat, qp_b, kp_b, v_b = refs
    q_cat[:, 0:D_MODEL] = jnp.where(second, zero, v["qm"]).astype(BF16)
    q_cat[:, D_MODEL : 2 * D_MODEL] = jnp.where(second, v["qm"], zero).astype(BF16)
    q_cat[:, 2 * D_MODEL :] = jnp.where(second, v["qd"], zero).astype(BF16)
    k_cat[:, 0:D_MODEL] = jnp.where(second, zero, v["km"]).astype(BF16)
    k_cat[:, D_MODEL : 2 * D_MODEL] = jnp.where(second, v["km"], zero).astype(BF16)
    k_cat[:, 2 * D_MODEL :] = jnp.where(second, zero, v["kl"]).astype(BF16)
    qp_b[...] = v["qp"].astype(BF16)
    kp_b[...] = v["kp"].astype(BF16)
    v_b[...] = hi.astype(BF16)


def _hg_pair_operands(cat, r0, c0):
    return jnp.concatenate([cat[r0 : r0 + HG_PAIR, g * D_MODEL + c0 : g * D_MODEL + c0 + HEAD_DIM] for g in range(3)], axis=1)


def _hg_fwd(proj, lbl):
    s_len = proj.shape[0]
    rows = min(HG_STEP, s_len)
    n_pairs = rows // HG_PAIR

    def body(hq_ref, hf_ref, hi_ref, lbl_ref, mats_ref, o_ref, st_ref, state, q_cat, k_cat, qp_b, kp_b, v_b):
        @pl.when(pl.program_id(0) == 0)
        def _():
            state[...] = jnp.zeros_like(state)

        v, e, second, _ = _hg_operands(hq_ref[...], hf_ref[...], _hg_lower_bound(lbl_ref), rows, mats_ref[0])
        _hg_store_operands(v, second, hi_ref[...], (q_cat, k_cat, qp_b, kp_b, v_b))
        e_pair = e["pair"]
        row = lax.broadcasted_iota(jnp.int32, (HG_PAIR, HG_PAIR), 0)
        col = lax.broadcasted_iota(jnp.int32, (HG_PAIR, HG_PAIR), 1)
        causal = row >= col

        for u in range(n_pairs):
            r0 = u * HG_PAIR
            sls = [(slice(r0, r0 + HG_PAIR), slice(h * HEAD_DIM, (h + 1) * HEAD_DIM)) for h in range(HEADS)]
            a_s = [jnp.where(causal, _dot_nt(_hg_pair_operands(q_cat, r0, h * HEAD_DIM),
                                             _hg_pair_operands(k_cat, r0, h * HEAD_DIM)), 0.0).astype(BF16)
                   for h in range(HEADS)]
            st_s = [state[h] for h in range(HEADS)]
            for h, sl in enumerate(sls):
                st_ref[u, h] = st_s[h]
                state[h] = st_s[h] * e_pair[r0 : r0 + 1, sl[1]] + _dot_tn(v_b[sl], kp_b[sl])
            for h, sl in enumerate(sls):
                o_ref[sl] = _dot(a_s[h], v_b[sl]) + _dot_nt(qp_b[sl], st_s[h].astype(BF16))

    def col_spec(off):
        return pl.BlockSpec((rows, D_MODEL), lambda s: (s, off // D_MODEL))

    bf_tile = pltpu.VMEM((rows, D_MODEL), BF16)
    bf_cat = pltpu.VMEM((rows, 3 * D_MODEL), BF16)
    scratch = [pltpu.VMEM((HEADS, HEAD_DIM, HEAD_DIM), F32), bf_cat, bf_cat, bf_tile, bf_tile, bf_tile]
    return pl.pallas_call(
        body,
        name="hg_fwd",
        grid=(s_len // rows,),
        in_specs=[col_spec(OFF_HG_Q), col_spec(OFF_HG_F), col_spec(OFF_HG_I), pl.BlockSpec((2, D_MODEL), lambda s: (0, 0)),
                  pl.BlockSpec((1, rows, rows), lambda s: (0, 0, 0))],
        out_specs=[
            pl.BlockSpec((rows, D_MODEL), lambda s: (s, 0)),
            pl.BlockSpec((n_pairs, HEADS, HEAD_DIM, HEAD_DIM), lambda s: (s, 0, 0, 0)),
        ],
        out_shape=[
            jax.ShapeDtypeStruct((s_len, D_MODEL), F32),
            jax.ShapeDtypeStruct((s_len // HG_PAIR, HEADS, HEAD_DIM, HEAD_DIM), F32),
        ],
        scratch_shapes=scratch,
        compiler_params=_cparams(("arbitrary",)),
    )(proj, proj, proj, lbl, _hg_mats(rows))


def _hg_bwd(proj, lbl, states, d_o):
    s_len = proj.shape[0]
    rows = min(HG_STEP, s_len)
    n_pairs = rows // HG_PAIR
    n_steps = s_len // rows

    def body(hq_ref, hf_ref, hi_ref, lbl_ref, st_ref, do_ref, mats_ref, dp_ref, dlb_ref,
             dstate, q_cat, k_cat, qp_b, kp_b, v_b, do_b, d_qcat, d_kcat, d_qp, d_kp, d_v, d_pair):
        @pl.when(pl.program_id(0) == 0)
        def _():
            dstate[...] = jnp.zeros_like(dstate)
            dlb_ref[...] = jnp.zeros_like(dlb_ref)

        lb = _hg_lower_bound(lbl_ref)
        hq = hq_ref[...]
        v, e, second, (f, sig_f, sig_q) = _hg_operands(hq, hf_ref[...], lb, rows, mats_ref[0])
        _hg_store_operands(v, second, hi_ref[...], (q_cat, k_cat, qp_b, kp_b, v_b))
        do_b[...] = do_ref[...].astype(BF16)
        e_pair = e["pair"]
        row = lax.broadcasted_iota(jnp.int32, (HG_PAIR, HG_PAIR), 0)
        col = lax.broadcasted_iota(jnp.int32, (HG_PAIR, HG_PAIR), 1)
        causal = row >= col

        for u in reversed(range(n_pairs)):
            r0 = u * HG_PAIR
            sls = [(slice(r0, r0 + HG_PAIR), slice(h * HEAD_DIM, (h + 1) * HEAD_DIM)) for h in range(HEADS)]
            ops = [(_hg_pair_operands(q_cat, r0, h * HEAD_DIM), _hg_pair_operands(k_cat, r0, h * HEAD_DIM))
                   for h in range(HEADS)]
            a_s = [jnp.where(causal, _dot_nt(lhs, rhs), 0.0).astype(BF16) for lhs, rhs in ops]
            da_s = [jnp.where(causal, _dot_nt(do_b[sl], v_b[sl]), 0.0).astype(BF16) for sl in sls]
            st0_s = [st_ref[u, h] for h in range(HEADS)]
            ds1_s = [dstate[h] for h in range(HEADS)]
            ds1b_s = [ds1.astype(BF16) for ds1 in ds1_s]
            for h, sl in enumerate(sls):
                decay = e_pair[r0 : r0 + 1, sl[1]]
                d_pair[u : u + 1, sl[1]] = decay * jnp.sum(ds1_s[h] * st0_s[h], axis=0, keepdims=True)
                dstate[h] = ds1_s[h] * decay + _dot_tn(do_b[sl], qp_b[sl])
            for h, sl in enumerate(sls):
                d_qp[sl] = _dot(do_b[sl], st0_s[h].astype(BF16))
                d_kp[sl] = _dot(v_b[sl], ds1b_s[h])
            for h, sl in enumerate(sls):
                d_v[sl] = _dot_tn(a_s[h], do_b[sl]) + _dot_nt(kp_b[sl], ds1b_s[h])
            for h, sl in enumerate(sls):
                d_lhs = _dot(da_s[h], ops[h][1])
                d_rhs = _dot_tn(da_s[h], ops[h][0])
                for g in range(3):
                    gsl = (sl[0], slice(g * D_MODEL + h * HEAD_DIM, g * D_MODEL + (h + 1) * HEAD_DIM))
                    d_qcat[gsl] = d_lhs[:, g * HEAD_DIM : (g + 1) * HEAD_DIM]
                    d_kcat[gsl] = d_rhs[:, g * HEAD_DIM : (g + 1) * HEAD_DIM]

        zero = jnp.zeros_like(hq)
        dqm = jnp.where(second, d_qcat[:, D_MODEL : 2 * D_MODEL], d_qcat[:, 0:D_MODEL])
        dkm = jnp.where(second, d_kcat[:, D_MODEL : 2 * D_MODEL], d_kcat[:, 0:D_MODEL])
        dqp, dkp = d_qp[...], d_kp[...]
        dqd = dqp * e["q_in"] + jnp.where(second, d_qcat[:, 2 * D_MODEL :], zero)
        dkl = dkp * e["k_out"] + jnp.where(second, zero, d_kcat[:, 2 * D_MODEL :])
        dq = dqm * e["qm"] + dqd * e["qd"]
        dk = dkm * e["km"] + dkl * e["kl"]
        t_kl = dkl * v["kl"]
        dcum = dqm * v["qm"] - dkm * v["km"] + dqd * v["qd"] - t_kl
        dp = d_pair[...]
        dp_b = jnp.broadcast_to(dp[:, None, :], (n_pairs, HG_PAIR, D_MODEL)).reshape(rows, D_MODEL)
        upper, whole, nxt, prev = (mats_ref[HG_MATS.index(kind)] for kind in ("upper", "all", "next", "prev"))
        dg = (_split_dot_left(upper, dcum) + _split_dot_left(whole, t_kl)
              + _split_dot_left(nxt, dqp * v["qp"]) + _split_dot_left(prev, dkp * v["kp"]) + dp_b)
        df = dg / f - dk
        one_m = 1.0 - sig_f
        dp_ref[:, 0:D_MODEL] = (dq * (sig_q * (1.0 + hq * (1.0 - sig_q)))).astype(BF16)
        dp_ref[:, D_MODEL : 2 * D_MODEL] = (df * (1.0 - lb) * sig_f * one_m).astype(BF16)
        dp_ref[:, 2 * D_MODEL : 3 * D_MODEL] = d_v[...].astype(BF16)
        dlb_ref[...] += jnp.sum(df * one_m, axis=0, keepdims=True)

    def col_spec(off):
        return pl.BlockSpec((rows, D_MODEL), lambda s: (n_steps - 1 - s, off // D_MODEL))

    f32_tile = pltpu.VMEM((rows, D_MODEL), F32)
    f32_cat = pltpu.VMEM((rows, 3 * D_MODEL), F32)
    bf_tile = pltpu.VMEM((rows, D_MODEL), BF16)
    bf_cat = pltpu.VMEM((rows, 3 * D_MODEL), BF16)
    scratch = [pltpu.VMEM((HEADS, HEAD_DIM, HEAD_DIM), F32), bf_cat, bf_cat, bf_tile, bf_tile, bf_tile, bf_tile,
               f32_cat, f32_cat, f32_tile, f32_tile, f32_tile, pltpu.VMEM((n_pairs, D_MODEL), F32)]
    return pl.pallas_call(
        body,
        name="hg_bwd",
        grid=(n_steps,),
        in_specs=[
            col_spec(OFF_HG_Q), col_spec(OFF_HG_F), col_spec(OFF_HG_I),
            pl.BlockSpec((2, D_MODEL), lambda s: (0, 0)),
            pl.BlockSpec((n_pairs, HEADS, HEAD_DIM, HEAD_DIM), lambda s: (n_steps - 1 - s, 0, 0, 0)),
            pl.BlockSpec((rows, D_MODEL), lambda s: (n_steps - 1 - s, 0)),
            pl.BlockSpec((len(HG_MATS), rows, rows), lambda s: (0, 0, 0)),
        ],
        out_specs=[
            pl.BlockSpec((rows, 3 * D_MODEL), lambda s: (n_steps - 1 - s, 0)),
            pl.BlockSpec((1, D_MODEL), lambda s: (0, 0)),
        ],
        out_shape=[
            jax.ShapeDtypeStruct((s_len, 3 * D_MODEL), BF16),
            jax.ShapeDtypeStruct((1, D_MODEL), F32),
        ],
        scratch_shapes=scratch,
        compiler_params=_cparams(("arbitrary",)),
    )(proj, proj, proj, lbl, states, d_o, _hg_mats(rows))


def _mid(proj, sb_o, hg_o, x, target, b_gate, hg_gain, final_g, w_sb, w_hg, w_out):
    s_len = proj.shape[0]
    ts = min(256, s_len)
    inv_d = 1.0 / D_MODEL

    def body(zsb_ref, hz_ref, gl_ref, sbo_ref, hgo_ref, x_ref, tgt_ref, bg_ref, hgn_ref, fg_ref,
             wsb_ref, whg_ref, wout_ref,
             dout_ref, dsbo_ref, dhgo_ref, dmid_ref,
             asb_ref, dusb_ref, ahg_ref, duhg_ref, y_ref, doutb_ref,
             loss_ref, dfg_ref, dbg_ref, dhgn_ref):
        @pl.when(pl.program_id(0) == 0)
        def _():
            loss_ref[...] = jnp.zeros_like(loss_ref)
            dfg_ref[...] = jnp.zeros_like(dfg_ref)
            dbg_ref[...] = jnp.zeros_like(dbg_ref)
            dhgn_ref[...] = jnp.zeros_like(dhgn_ref)

        z_sb = zsb_ref[...]
        sb_o = sbo_ref[...]
        sig_zsb = _sigmoid(z_sb)
        silu_zsb = z_sb * sig_zsb
        a_sb_f = sb_o * silu_zsb
        a_sb = a_sb_f.astype(BF16)
        u_sb = _dot(a_sb, wsb_ref[...])

        hg_o = hgo_ref[...]
        gain = hgn_ref[...]
        r_parts, yn_parts = [], []
        for h in range(HEADS):
            oh = hg_o[:, h * HEAD_DIM : (h + 1) * HEAD_DIM]
            r = lax.rsqrt(jnp.mean(oh * oh, axis=-1, keepdims=True) + RMS_EPS)
            r_parts.append(jnp.broadcast_to(r, oh.shape))
            yn_parts.append(oh * r)
        r_hg = jnp.concatenate(r_parts, axis=-1)
        yn_hg = jnp.concatenate(yn_parts, axis=-1)
        hn = yn_hg * gain
        hz = hz_ref[...]
        sig_hz = _sigmoid(hz)
        silu_hz = hz * sig_hz
        a_hg_f = hn * silu_hz
        a_hg = a_hg_f.astype(BF16)
        u_hg = _dot(a_hg, whg_ref[...])

        gates = _sigmoid(gl_ref[...] + bg_ref[...])
        g_sb = gates[:, 0:D_MODEL]
        g_hg = gates[:, D_MODEL:]
        y_f = g_sb * u_sb + g_hg * u_hg
        y = y_f.astype(BF16)
        out = x_ref[...] + _dot(y, wout_ref[...])
        r2 = lax.rsqrt(jnp.mean(out * out, axis=-1, keepdims=True) + RMS_EPS)
        yn = out * r2
        fg = fg_ref[...]
        diff = yn * fg - tgt_ref[...]
        loss_ref[...] += 0.5 * inv_d * jnp.sum(diff * diff)

        dyf = diff * inv_d
        dfg_ref[...] += jnp.sum(dyf * yn, axis=0, keepdims=True)
        dyn = dyf * fg
        dout = r2 * (dyn - yn * jnp.mean(dyn * yn, axis=-1, keepdims=True))
        dout_ref[...] = dout
        doutb = dout.astype(BF16)
        doutb_ref[...] = doutb
        dy = _dot_nt(doutb, wout_ref[...])
        du_sb = (dy * g_sb).astype(BF16)
        du_hg = (dy * g_hg).astype(BF16)
        dgl_sb = dy * u_sb * g_sb * (1.0 - g_sb)
        dgl_hg = dy * u_hg * g_hg * (1.0 - g_hg)
        dmid_ref[:, 2 * D_MODEL : 3 * D_MODEL] = dgl_sb.astype(BF16)
        dmid_ref[:, 3 * D_MODEL :] = dgl_hg.astype(BF16)
        dbg_ref[:, 0:D_MODEL] += jnp.sum(dgl_sb, axis=0, keepdims=True)
        dbg_ref[:, D_MODEL:] += jnp.sum(dgl_hg, axis=0, keepdims=True)

        da_sb = _dot_nt(du_sb, wsb_ref[...])
        dsbo_ref[...] = (da_sb * silu_zsb).astype(BF16)
        dmid_ref[:, 0:D_MODEL] = (da_sb * sb_o * (sig_zsb * (1.0 + z_sb * (1.0 - sig_zsb)))).astype(BF16)

        da_hg = _dot_nt(du_hg, whg_ref[...])
        dhn = da_hg * silu_hz
        dmid_ref[:, D_MODEL : 2 * D_MODEL] = (da_hg * hn * (sig_hz * (1.0 + hz * (1.0 - sig_hz)))).astype(BF16)
        dhgn_ref[...] += jnp.sum(dhn * yn_hg, axis=0, keepdims=True)
        dyn_hg = dhn * gain
        prod = dyn_hg * yn_hg
        m_parts = []
        for h in range(HEADS):
            ph = prod[:, h * HEAD_DIM : (h + 1) * HEAD_DIM]
            m_parts.append(jnp.broadcast_to(jnp.mean(ph, axis=-1, keepdims=True), ph.shape))
        dhgo_ref[...] = (r_hg * (dyn_hg - yn_hg * jnp.concatenate(m_parts, axis=-1))).astype(BF16)

        asb_ref[...] = a_sb_f.T.astype(BF16)
        dusb_ref[...] = du_sb
        ahg_ref[...] = a_hg_f.T.astype(BF16)
        duhg_ref[...] = du_hg
        y_ref[...] = y_f.T.astype(BF16)

    def tile(width, off=0):
        return pl.BlockSpec((ts, width), lambda s: (s, off // width))

    def across():
        return pl.BlockSpec((D_MODEL, ts), lambda s: (0, s))

    def whole(shape):
        return pl.BlockSpec(shape, lambda s: (0,) * len(shape))

    def weight():
        return pl.BlockSpec((D_MODEL, D_MODEL), lambda s: (0, 0), pipeline_mode=pl.Buffered(1))

    f32_act = jax.ShapeDtypeStruct((s_len, D_MODEL), F32)
    bf_act = jax.ShapeDtypeStruct((s_len, D_MODEL), BF16)
    bf_act_t = jax.ShapeDtypeStruct((D_MODEL, s_len), BF16)
    return pl.pallas_call(
        body,
        name="mid",
        grid=(s_len // ts,),
        in_specs=[
            tile(D_MODEL, OFF_SB_Z), tile(D_MODEL, OFF_HG_Z), tile(2 * D_MODEL, OFF_GATE),
            tile(D_MODEL), tile(D_MODEL), tile(D_MODEL), tile(D_MODEL),
            whole((1, 2 * D_MODEL)), whole((1, D_MODEL)), whole((1, D_MODEL)),
            weight(), weight(), weight(),
        ],
        out_specs=[
            tile(D_MODEL), tile(D_MODEL), tile(D_MODEL), tile(4 * D_MODEL),
            across(), tile(D_MODEL), across(), tile(D_MODEL), across(), tile(D_MODEL),
            whole((1, 1)), whole((1, D_MODEL)), whole((1, 2 * D_MODEL)), whole((1, D_MODEL)),
        ],
        out_shape=[
            f32_act, bf_act, bf_act, jax.ShapeDtypeStruct((s_len, 4 * D_MODEL), BF16),
            bf_act_t, bf_act, bf_act_t, bf_act, bf_act_t, bf_act,
            jax.ShapeDtypeStruct((1, 1), F32), jax.ShapeDtypeStruct((1, D_MODEL), F32),
            jax.ShapeDtypeStruct((1, 2 * D_MODEL), F32), jax.ShapeDtypeStruct((1, D_MODEL), F32),
        ],
        compiler_params=_cparams(("arbitrary",)),
    )(proj, proj, proj, sb_o, hg_o, x, target, b_gate, hg_gain, final_g, w_sb, w_hg, w_out)


def _grad_square(a_t, b, name):
    s_len = b.shape[0]
    tk = min(1024, s_len)

    def body(a_ref, b_ref, o_ref):
        @pl.when(pl.program_id(0) == 0)
        def _():
            o_ref[...] = jnp.zeros_like(o_ref)

        o_ref[...] += _dot(a_ref[...], b_ref[...])

    return pl.pallas_call(
        body,
        name=name,
        grid=(s_len // tk,),
        in_specs=[pl.BlockSpec((D_MODEL, tk), lambda k: (0, k)), pl.BlockSpec((tk, D_MODEL), lambda k: (k, 0))],
        out_specs=pl.BlockSpec((D_MODEL, D_MODEL), lambda k: (0, 0)),
        out_shape=jax.ShapeDtypeStruct((D_MODEL, D_MODEL), F32),
        compiler_params=_cparams(("arbitrary",)),
    )(a_t, b)


SEG_WIDTHS = (1024, 1024, 1024, 4096, 3072)
SEG_TILE = 1024
SEG_BOUNDS = (0, 1, 2, 3, 7, 10)


def _w_in_tile(k):
    return jnp.where(k < 4, k, jnp.where(k < 7, k + 3, k - 3))


def _grad_w_in(h_t, segs):
    m, s_len = h_t.shape
    tk = min(1024, s_len)
    tn = SEG_TILE
    nk = s_len // tk
    bounds = SEG_BOUNDS

    def body(a_ref, *refs):
        seg_refs, o_ref = refs[:-1], refs[-1]
        j = pl.program_id(0)

        @pl.when(pl.program_id(1) == 0)
        def _():
            o_ref[...] = jnp.zeros_like(o_ref)

        for i, ref in enumerate(seg_refs):
            @pl.when((j >= bounds[i]) & (j < bounds[i + 1]))
            def _(ref=ref):
                o_ref[...] += _dot(a_ref[...], ref[...])

    def seg_spec(lo, hi):
        def index(j, k):
            return (jnp.where(j < lo, 0, jnp.where(j >= hi, nk - 1, k)), jnp.clip(j - lo, 0, hi - lo - 1))
        return pl.BlockSpec((tk, tn), index)

    return pl.pallas_call(
        body,
        name="grad_w_in",
        grid=(IN_WIDTH // tn, nk),
        in_specs=[pl.BlockSpec((m, tk), lambda j, k: (0, k))]
        + [seg_spec(bounds[i], bounds[i + 1]) for i in range(len(SEG_WIDTHS))],
        out_specs=pl.BlockSpec((m, tn), lambda j, k: (0, _w_in_tile(j))),
        out_shape=jax.ShapeDtypeStruct((m, IN_WIDTH), F32),
        compiler_params=_cparams(("arbitrary", "arbitrary")),
    )(h_t, *segs)


EXCHANGE_IN_PIECES = 8
EXCHANGE_PIECES = EXCHANGE_IN_PIECES + 3


def _exchange_copies(sin_ref, ssq_ref, got_in, got_sq, send_sems, recv_sems):
    _, _, c, chips = _position()
    rows = HALF_IN // EXCHANGE_IN_PIECES
    copies = []
    for k, (px, py) in enumerate(chips):
        chip = 2 * px + py
        for p in range(EXCHANGE_PIECES):
            if p < EXCHANGE_IN_PIECES:
                src, dst = sin_ref.at[chip, pl.ds(p * rows, rows), :], got_in.at[k, pl.ds(p * rows, rows), :]
            else:
                src, dst = ssq_ref.at[p - EXCHANGE_IN_PIECES, chip], got_sq.at[k, p - EXCHANGE_IN_PIECES]
            copies.append(_remote(src, dst, send_sems.at[k, p], recv_sems.at[k, p], (px, py, c)))
    return copies


def _dx(segs, w_all, x, norm_g, dout, s_in, s_sq):
    s_len = x.shape[0]
    ts = min(1024, s_len)
    tk = SEG_TILE
    nk = IN_WIDTH // tk
    ns = s_len // ts
    bounds = SEG_BOUNDS
    n_seg = len(SEG_WIDTHS)

    def body(*refs):
        seg_refs = refs[:n_seg]
        w_ref, x_ref, g_ref, dout_ref, sin_ref, ssq_ref, gx_ref, dg_ref, got_in, got_sq, acc, send_sems, recv_sems = refs[n_seg:]
        s, k = pl.program_id(0), pl.program_id(1)

        @pl.when((s == 0) & (k == 0))
        def _():
            dg_ref[...] = jnp.zeros_like(dg_ref)
            for cp in _exchange_copies(sin_ref, ssq_ref, got_in, got_sq, send_sems, recv_sems):
                cp.start()

        @pl.when(k == 0)
        def _():
            acc[...] = jnp.zeros_like(acc)

        for i, ref in enumerate(seg_refs):
            @pl.when((k >= bounds[i]) & (k < bounds[i + 1]))
            def _(ref=ref):
                acc[...] += _dot_nt(ref[...], w_ref[...])

        @pl.when(k == nk - 1)
        def _():
            dh = acc[...]
            xv = x_ref[...]
            r = lax.rsqrt(jnp.mean(xv * xv, axis=-1, keepdims=True) + RMS_EPS)
            xn = xv * r
            dg_ref[...] += jnp.sum(dh * xn, axis=0, keepdims=True)
            dxn = dh * g_ref[...]
            gx_ref[...] = r * (dxn - xn * jnp.mean(dxn * xn, axis=-1, keepdims=True)) + dout_ref[...]

        @pl.when((s == ns - 1) & (k == nk - 1))
        def _():
            for cp in _exchange_copies(sin_ref, ssq_ref, got_in, got_sq, send_sems, recv_sems):
                cp.wait()

    def seg_spec(lo, hi):
        return pl.BlockSpec((ts, tk), lambda s, k: (s, jnp.clip(k - lo, 0, hi - lo - 1)))

    row_tile = pl.BlockSpec((ts, D_MODEL), lambda s, k: (s, 0))
    vec = pl.BlockSpec((1, D_MODEL), lambda s, k: (0, 0))
    return pl.pallas_call(
        body,
        name="dx",
        grid=(ns, nk),
        in_specs=[seg_spec(bounds[i], bounds[i + 1]) for i in range(n_seg)] + [
            pl.BlockSpec((D_MODEL, tk), lambda s, k: (0, _w_in_tile(k))),
            row_tile, vec, row_tile, ANY, ANY,
        ],
        out_specs=[row_tile, vec, ANY, ANY],
        out_shape=[jax.ShapeDtypeStruct((s_len, D_MODEL), F32), jax.ShapeDtypeStruct((1, D_MODEL), F32),
                   jax.ShapeDtypeStruct((3, HALF_IN, W_IN_SHARD), WIRE),
                   jax.ShapeDtypeStruct((3, 3, HALF_SQ, D_MODEL), WIRE)],
        scratch_shapes=[pltpu.VMEM((ts, D_MODEL), F32),
                        pltpu.SemaphoreType.DMA((3, EXCHANGE_PIECES)), pltpu.SemaphoreType.DMA((3, EXCHANGE_PIECES))],
        compiler_params=_cparams(("arbitrary", "arbitrary"), vmem=VMEM_LIMIT_DX),
    )(*segs, w_all, x, norm_g, dout, s_in, s_sq)


def _local_grads(x, target, proj, h_t, qkv, b_gate, lbl, hg_gain, final_g, w_sb, w_hg, w_out):
    sb_o, sb_o_fine = _sb_fwd(qkv)
    hg_o, states = _hg_fwd(proj, lbl)
    (dout, d_sbo, d_hgo, d_mid, a_sb, du_sb, a_hg, du_hg, y, doutb,
     loss, d_fg, d_bg, d_hgn) = _mid(proj, sb_o, hg_o, x, target, b_gate, hg_gain, final_g, w_sb, w_hg, w_out)
    g_w_sb = _grad_square(a_sb, du_sb, "grad_w_sb")
    g_w_hg = _grad_square(a_hg, du_hg, "grad_w_hg")
    g_w_out = _grad_square(y, doutb, "grad_w_out")
    d_q, d_k, d_v = _sb_bwd(qkv, sb_o_fine, d_sbo)
    d_hg, d_lb = _hg_bwd(proj, lbl, states, d_hgo)
    segs = (d_q, d_k, d_v, d_mid, d_hg)
    g_w_in = _grad_w_in(h_t, segs)
    return g_w_in, g_w_sb, g_w_hg, g_w_out, segs, dout, loss, d_bg, d_lb, d_hgn, d_fg


ANY = pl.BlockSpec(memory_space=pl.ANY)
WIRE = BF16
HALF_IN = D_MODEL // 2
HALF_SQ = ROW_SHARD // 2


def _position():
    x, y, c = lax.axis_index("x"), lax.axis_index("y"), lax.axis_index("c")
    chips = [(1 - x, y), (x, 1 - y), (1 - x, 1 - y)]
    return x, y, c, chips


def _remote(src, dst, send_sem, recv_sem, to):
    return pltpu.make_async_remote_copy(src_ref=src, dst_ref=dst, send_sem=send_sem, recv_sem=recv_sem,
                                        device_id=to, device_id_type=MESH)


PROJ_TILE = 1280
F32_FROM_TILE = 2
BF16_TO_TILE = 2
W_LOAD_PIECES = 8


def _gather_inproj(idx, h, w_in_b, w_sq_b):
    s_len = h.shape[0]
    ts = min(1024, s_len)
    ns = s_len // ts
    per = W_IN_SHARD // PROJ_TILE
    n_in = 4
    n_piece = n_in + 3
    rows = HALF_IN // n_in

    def chip_at(r, me):
        return me ^ jnp.where(r == 1, 2, jnp.where(r == 2, 1, jnp.where(r == 3, 3, 0)))

    def body(idx_ref, h_ref, win_ref, wsqb_ref, proj_ref, qkv_ref, wall_ref, wsq_ref, wbuf, send_sems, recv_sems, w_sems):
        r, t, s = pl.program_id(0), pl.program_id(1), pl.program_id(2)
        x, y, c, chips = _position()
        me = 2 * x + y
        sibling = (x, y, 1 - c)
        first = (t == 0) & (s == 0)

        def src_piece(p):
            if p < n_in:
                return win_ref.at[pl.ds(c * HALF_IN + p * rows, rows), :]
            return wsqb_ref.at[p - n_in, pl.ds(c * HALF_SQ, HALF_SQ), :]

        def piece(p, chip, core):
            if p < n_in:
                cols = pl.ds(pl.multiple_of(chip * W_IN_SHARD, W_IN_SHARD), W_IN_SHARD)
                return wall_ref.at[pl.ds(core * HALF_IN + p * rows, rows), cols]
            return wsq_ref.at[p - n_in, chip, pl.ds(core * HALF_SQ, HALF_SQ), :]

        def send(k, p):
            px, py = chips[k]
            return _remote(src_piece(p), piece(p, me, c), send_sems.at[k, p], recv_sems.at[k, p], (px, py, c))

        def forward(k, p, core):
            px, py = chips[k]
            got = piece(p, 2 * px + py, core)
            return _remote(got, got, send_sems.at[3 + k, p], recv_sems.at[3 + k, p], sibling)

        @pl.when((r == 0) & first)
        def _():
            for k in range(2):
                for p in range(n_piece):
                    send(k, p).start()

        for k in range(3):
            @pl.when((r == k + 1) & first)
            def _(k=k):
                px, py = chips[k]
                for p in range(n_piece):
                    got = piece(p, 2 * px + py, c)
                    _remote(got, got, send_sems.at[k, p], recv_sems.at[k, p], (px, py, c)).wait_recv()
                    forward(k, p, c).start()
                if k == 0:
                    for p in range(n_piece):
                        send(2, p).start()
                for p in range(n_piece):
                    forward(k, p, 1 - c).wait_recv()

        def tile_loads(slot, own):
            col = slot * PROJ_TILE
            if not own:
                col = pl.multiple_of(chip_at(r, me) * W_IN_SHARD + col, PROJ_TILE)
            src = win_ref if own else wall_ref
            part = D_MODEL // W_LOAD_PIECES
            return [pltpu.make_async_copy(src.at[pl.ds(q * part, part), pl.ds(col, PROJ_TILE)],
                                          wbuf.at[slot, pl.ds(q * part, part), :], w_sems.at[slot, q])
                    for q in range(W_LOAD_PIECES)]

        for own in (True, False):
            @pl.when(first & ((r == 0) if own else (r > 0)))
            def _(own=own):
                for slot in range(per):
                    for cp in tile_loads(slot, own):
                        cp.start()
                for cp in tile_loads(0, own):
                    cp.wait()

            @pl.when((t > 0) & (s == 0) & ((r == 0) if own else (r > 0)))
            def _(own=own):
                for cp in tile_loads(1, own):
                    cp.wait()

        tile_now = per * chip_at(r, me) + t
        want_f32, want_bf16 = tile_now >= F32_FROM_TILE, tile_now <= BF16_TO_TILE

        @pl.when(want_f32 & jnp.logical_not(want_bf16))
        def _():
            proj_ref[...] = _dot(h_ref[...], wbuf[t])

        @pl.when(want_bf16 & jnp.logical_not(want_f32))
        def _():
            qkv_ref[...] = _dot(h_ref[...], wbuf[t]).astype(BF16)

        @pl.when(want_f32 & want_bf16)
        def _():
            p = _dot(h_ref[...], wbuf[t])
            proj_ref[...] = p
            qkv_ref[...] = p.astype(BF16)

        @pl.when((r == 3) & (t == per - 1) & (s == ns - 1))
        def _():
            for k in range(3):
                for p in range(n_piece):
                    send(k, p).wait_send()
                    forward(k, p, c).wait_send()

    def out_index(wanted):
        order = [0, 2, 1, 3]
        table = []
        for chip in range(N_CHIPS):
            tiles = [per * (chip ^ order[q // per]) + q % per for q in range(N_CHIPS * per)]
            row = []
            for q, tile in enumerate(tiles):
                if wanted(tile):
                    row.append((tile, None))
                    continue
                before = [u for u in tiles[:q] if wanted(u)]
                after = [u for u in tiles[q:] if wanted(u)]
                row.append((before[-1], ns - 1) if before else (after[0], 0))
            table.append(row)

        def index(r, t, s, idx):
            q = r * per + t
            col, fixed_s = jnp.int32(0), jnp.int32(-1)
            for chip in range(N_CHIPS):
                for pos, (tile, hold) in enumerate(table[chip]):
                    here = (idx[0] == chip) & (q == pos)
                    col = jnp.where(here, tile, col)
                    fixed_s = jnp.where(here, -1 if hold is None else hold, fixed_s)
            return jnp.where(fixed_s < 0, s, fixed_s), col

        return index

    grid_spec = pltpu.PrefetchScalarGridSpec(
        num_scalar_prefetch=1,
        grid=(N_CHIPS, per, ns),
        in_specs=[pl.BlockSpec((ts, D_MODEL), lambda r, t, s, idx: (s, 0)), ANY, ANY],
        out_specs=[pl.BlockSpec((ts, PROJ_TILE), out_index(lambda tile: tile >= F32_FROM_TILE)),
                   pl.BlockSpec((ts, PROJ_TILE), out_index(lambda tile: tile <= BF16_TO_TILE)),
                   ANY, ANY],
        scratch_shapes=[pltpu.VMEM((per, D_MODEL, PROJ_TILE), BF16),
                        pltpu.SemaphoreType.DMA((6, n_piece)), pltpu.SemaphoreType.DMA((6, n_piece)),
                        pltpu.SemaphoreType.DMA((per, W_LOAD_PIECES))],
    )
    return pl.pallas_call(
        body,
        name="gather_inproj",
        grid_spec=grid_spec,
        out_shape=[jax.ShapeDtypeStruct((s_len, IN_WIDTH), F32),
                   jax.ShapeDtypeStruct((s_len, IN_WIDTH), BF16),
                   jax.ShapeDtypeStruct((D_MODEL, IN_WIDTH), BF16),
                   jax.ShapeDtypeStruct((3, N_CHIPS, ROW_SHARD, D_MODEL), BF16)],
        compiler_params=_cparams(("arbitrary", "arbitrary", "arbitrary")),
    )(idx, h, w_in_b, w_sq_b)


def _place_own(idx, w_in_b, w_sq_b, w_all, wsq):
    n = 4
    r_in, r_sq = D_MODEL // n, ROW_SHARD // n

    def body(idx_ref, win_ref, wsq_ref, w_all_in, wsq_in, w_all_out, wsq_out):
        w_all_out[...] = win_ref[...]
        wsq_out[:, 0] = wsq_ref[...]

    grid_spec = pltpu.PrefetchScalarGridSpec(
        num_scalar_prefetch=1,
        grid=(n,),
        in_specs=[pl.BlockSpec((r_in, W_IN_SHARD), lambda r, idx: (r, 0)),
                  pl.BlockSpec((3, r_sq, D_MODEL), lambda r, idx: (0, r, 0)), ANY, ANY],
        out_specs=[pl.BlockSpec((r_in, W_IN_SHARD), lambda r, idx: (r, idx[0])),
                   pl.BlockSpec((3, 1, r_sq, D_MODEL), lambda r, idx: (0, idx[0], r, 0))],
    )
    return pl.pallas_call(
        body,
        name="place_own",
        grid_spec=grid_spec,
        out_shape=[jax.ShapeDtypeStruct(w_all.shape, BF16), jax.ShapeDtypeStruct(wsq.shape, BF16)],
        input_output_aliases={3: 0, 4: 1},
        compiler_params=_cparams(("arbitrary",)),
    )(idx, w_in_b, w_sq_b, w_all, wsq)


def _swap_halves(g_in, g_sq):
    n_in = 16
    n_piece = n_in + 3 * N_CHIPS
    rows = HALF_IN // n_in

    def body(gin_ref, gsq_ref, got_in, got_sq, send_sems, recv_sems):
        x, y, c, _ = _position()
        sibling = (x, y, 1 - c)

        def src_piece(p):
            if p < n_in:
                return gin_ref.at[pl.ds((1 - c) * HALF_IN + p * rows, rows), :]
            a, chip = divmod(p - n_in, N_CHIPS)
            return gsq_ref.at[a, chip, pl.ds((1 - c) * HALF_SQ, HALF_SQ), :]

        def dst_piece(p):
            if p < n_in:
                return got_in.at[pl.ds(p * rows, rows), :]
            a, chip = divmod(p - n_in, N_CHIPS)
            return got_sq.at[a, chip]

        out = [_remote(src_piece(p), dst_piece(p), send_sems.at[p], recv_sems.at[p], sibling) for p in range(n_piece)]
        for cp in out:
            cp.start()
        for cp in out:
            cp.wait()

    return pl.pallas_call(
        body,
        name="swap_halves",
        in_specs=[ANY, ANY],
        out_specs=[ANY, ANY],
        out_shape=[jax.ShapeDtypeStruct((HALF_IN, IN_WIDTH), F32),
                   jax.ShapeDtypeStruct((3, N_CHIPS, HALF_SQ, D_MODEL), F32)],
        scratch_shapes=[pltpu.SemaphoreType.DMA((n_piece,))] * 2,
    )(g_in, g_sq)


def _join_halves(r_in, r_sq):
    n_in = 16
    n_piece = n_in + 3
    rows = HALF_IN // n_in

    def body(in_alias, sq_alias, full_in, full_sq, send_sems, recv_sems):
        del in_alias, sq_alias
        x, y, c, _ = _position()
        sibling = (x, y, 1 - c)

        def piece(p, core):
            if p < n_in:
                return full_in.at[pl.ds(core * HALF_IN + p * rows, rows), :]
            return full_sq.at[p - n_in, pl.ds(core * HALF_SQ, HALF_SQ), :]

        out = [_remote(piece(p, c), piece(p, c), send_sems.at[p], recv_sems.at[p], sibling) for p in range(n_piece)]
        for cp in out:
            cp.start()
        for p in range(n_piece):
            _remote(piece(p, 1 - c), piece(p, 1 - c), send_sems.at[p], recv_sems.at[p], sibling).wait_recv()
        for cp in out:
            cp.wait_send()

    return pl.pallas_call(
        body,
        name="join_halves",
        in_specs=[ANY, ANY],
        out_specs=[ANY, ANY],
        out_shape=[jax.ShapeDtypeStruct((D_MODEL, W_IN_SHARD), F32),
                   jax.ShapeDtypeStruct((3, ROW_SHARD, D_MODEL), F32)],
        input_output_aliases={0: 0, 1: 1},
        scratch_shapes=[pltpu.SemaphoreType.DMA((n_piece,)), pltpu.SemaphoreType.DMA((n_piece,))],
    )(r_in, r_sq)


SMALL_ROWS = 56
N_DEV = 8


def _sum_small(part):
    def body(part_ref, out_ref, slots, send_sems, recv_sems):
        x, y, c, _ = _position()
        me = 4 * x + 2 * y + c
        slots[me] = part_ref[...]
        out = []
        for r in range(1, N_DEV):
            rx, ry, rc = (r >> 2) & 1, (r >> 1) & 1, r & 1
            to = (1 - x if rx else x, 1 - y if ry else y, 1 - c if rc else c)
            out.append(_remote(part_ref, slots.at[me], send_sems.at[r - 1], recv_sems.at[r - 1], to))
        for cp in out:
            cp.start()
        for r in range(1, N_DEV):
            _remote(part_ref, slots.at[me ^ r], send_sems.at[r - 1], recv_sems.at[r - 1], (x, y, c)).wait_recv()
        for cp in out:
            cp.wait_send()
        total = slots[0]
        for d in range(1, N_DEV):
            total = total + slots[d]
        out_ref[...] = total

    vmem = pl.BlockSpec(memory_space=pltpu.VMEM)
    return pl.pallas_call(
        body,
        name="sum_small",
        in_specs=[vmem],
        out_specs=vmem,
        out_shape=jax.ShapeDtypeStruct((SMALL_ROWS, HEAD_DIM), F32),
        scratch_shapes=[pltpu.VMEM((N_DEV, SMALL_ROWS, HEAD_DIM), F32),
                        pltpu.SemaphoreType.DMA((N_DEV - 1,)), pltpu.SemaphoreType.DMA((N_DEV - 1,))],
    )(part)


def _prefetch_call(body, name, idx, grid, in_specs, out_specs, out_shape, args):
    grid_spec = pltpu.PrefetchScalarGridSpec(num_scalar_prefetch=1, grid=grid, in_specs=in_specs, out_specs=out_specs)
    return pl.pallas_call(body, name=name, grid_spec=grid_spec, out_shape=out_shape,
                          compiler_params=_cparams(("arbitrary",) * len(grid)))(idx, *args)


def _sum_a_in(idx, g_in, got_in):
    tr = 128
    nr = HALF_IN // tr

    def body(idx_ref, a_ref, b_ref, o_ref):
        o_ref[0] = (a_ref[...] + b_ref[...]).astype(WIRE)

    return _prefetch_call(
        body, "sum_a_in", idx, (N_CHIPS, nr),
        [pl.BlockSpec((tr, W_IN_SHARD), lambda j, r, idx: (idx[1] * nr + r, j)),
         pl.BlockSpec((tr, W_IN_SHARD), lambda j, r, idx: (r, j))],
        pl.BlockSpec((1, tr, W_IN_SHARD), lambda j, r, idx: (j, r, 0)),
        jax.ShapeDtypeStruct((N_CHIPS, HALF_IN, W_IN_SHARD), WIRE), (g_in, got_in))


def _sum_a_sq(idx, g_sq, got_sq):
    blk = (1, 1, HALF_SQ, D_MODEL)

    def body(idx_ref, a_ref, b_ref, o_ref):
        o_ref[...] = (a_ref[...] + b_ref[...]).astype(WIRE)

    return _prefetch_call(
        body, "sum_a_sq", idx, (3, N_CHIPS),
        [pl.BlockSpec(blk, lambda a, j, idx: (a, j, idx[1], 0)), pl.BlockSpec(blk, lambda a, j, idx: (a, j, 0, 0))],
        pl.BlockSpec(blk, lambda a, j, idx: (a, j, 0, 0)),
        jax.ShapeDtypeStruct((3, N_CHIPS, HALF_SQ, D_MODEL), WIRE), (g_sq, got_sq))


def _sum_b_in(idx, s_in, got_in):
    tr = 128
    nr = HALF_IN // tr

    def body(idx_ref, a_ref, b_ref, o_ref):
        o_ref[...] = ((a_ref[0].astype(F32) + b_ref[0].astype(F32)) + b_ref[1].astype(F32)) + b_ref[2].astype(F32)

    return _prefetch_call(
        body, "sum_b_in", idx, (nr,),
        [pl.BlockSpec((1, tr, W_IN_SHARD), lambda r, idx: (idx[0], r, 0)),
         pl.BlockSpec((3, tr, W_IN_SHARD), lambda r, idx: (0, r, 0))],
        pl.BlockSpec((tr, W_IN_SHARD), lambda r, idx: (idx[1] * nr + r, 0)),
        jax.ShapeDtypeStruct((D_MODEL, W_IN_SHARD), F32), (s_in, got_in))


def _sum_b_sq(idx, s_sq, got_sq):
    def body(idx_ref, a_ref, b_ref, o_ref):
        o_ref[0] = ((a_ref[0, 0].astype(F32) + b_ref[0, 0].astype(F32)) + b_ref[1, 0].astype(F32)) + b_ref[2, 0].astype(F32)

    return _prefetch_call(
        body, "sum_b_sq", idx, (3,),
        [pl.BlockSpec((1, 1, HALF_SQ, D_MODEL), lambda a, idx: (a, idx[0], 0, 0)),
         pl.BlockSpec((3, 1, HALF_SQ, D_MODEL), lambda a, idx: (0, a, 0, 0))],
        pl.BlockSpec((1, HALF_SQ, D_MODEL), lambda a, idx: (a, idx[1], 0)),
        jax.ShapeDtypeStruct((3, ROW_SHARD, D_MODEL), F32), (s_sq, got_sq))


def _adamw_math(w, g, m, v):
    m = ADAM_B1 * m + (1.0 - ADAM_B1) * g
    v = ADAM_B2 * v + (1.0 - ADAM_B2) * (g * g)
    m_hat = m / (1.0 - ADAM_B1 ** ADAM_STEP)
    v_hat = v / (1.0 - ADAM_B2 ** ADAM_STEP)
    delta = -ADAM_LR * (m_hat / (jnp.sqrt(v_hat) + ADAM_EPS) + ADAM_WD * w)
    return delta, m, v


def _adamw(w, g, m, v, name):
    rows, cols = w.shape
    tr = min(128, rows)

    def body(w_ref, g_ref, m_ref, v_ref, d_ref, nm_ref, nv_ref):
        d_ref[...], nm_ref[...], nv_ref[...] = _adamw_math(w_ref[...], g_ref[...], m_ref[...], v_ref[...])

    spec = pl.BlockSpec((tr, cols), lambda r: (r, 0))
    return pl.pallas_call(
        body,
        name=name,
        grid=(rows // tr,),
        in_specs=[spec] * 4,
        out_specs=[spec] * 3,
        out_shape=[jax.ShapeDtypeStruct((rows, cols), F32)] * 3,
        compiler_params=_cparams(("arbitrary",)),
    )(w, g, m, v)


def _adamw_small(sums, w, m, v):
    def body(s_ref, w_ref, m_ref, v_ref, loss_ref, g_ref, d_ref, nm_ref, nv_ref):
        s = s_ref[...]
        w = w_ref[...]
        loss_ref[...] = s[0:1, 0:1]
        l0, l1 = w[24:32], w[32:40]
        mx = jnp.maximum(l0, l1)
        e0, e1 = jnp.exp(l0 - mx), jnp.exp(l1 - mx)
        p0, p1 = e0 / (e0 + e1), e1 / (e0 + e1)
        d_lb = s[32:40]
        g = jnp.concatenate([s[8:16], s[16:32], d_lb * p0 * (1.0 - p0), -d_lb * p0 * p1, s[40:48], s[48:56]], axis=0)
        g_ref[...] = g
        d_ref[...], nm_ref[...], nv_ref[...] = _adamw_math(w, g, m_ref[...], v_ref[...])

    packed = jax.ShapeDtypeStruct((SMALL_ROWS, HEAD_DIM), F32)
    return pl.pallas_call(
        body,
        name="adamw_small",
        out_shape=[jax.ShapeDtypeStruct((1, 1), F32), packed, packed, packed, packed],
    )(sums, w, m, v)


def _pack_small(ng, bg, lbl, hgn, fg):
    return jnp.concatenate([a.reshape(-1, HEAD_DIM) for a in (ng, bg, lbl, hgn, fg)], axis=0)


def _unpack_small(p):
    return (p[0:8].reshape(1, D_MODEL), p[8:24].reshape(1, 2 * D_MODEL), p[24:40].reshape(2, HEADS, HEAD_DIM),
            p[40:48].reshape(1, HEADS, HEAD_DIM), p[48:56].reshape(D_MODEL))


def kernel(x, norm_g, w_in, b_gate, lb_logits, hg_norm_g, w_sb_proj, w_hg_proj, w_out, final_norm_g, loss_target, m_norm_g, m_w_in, m_b_gate, m_lb_logits, m_hg_norm_g, m_w_sb_proj, m_w_hg_proj, m_w_out, m_final_norm_g, v_norm_g, v_w_in, v_b_gate, v_lb_logits, v_hg_norm_g, v_w_sb_proj, v_w_hg_proj, v_w_out, v_final_norm_g):
    s_len = x.shape[1]
    w_sq = jnp.stack([w_sb_proj[0], w_hg_proj[0], w_out[0]])
    idx = jnp.stack([2 * lax.axis_index("x") + lax.axis_index("y"), lax.axis_index("c")]).astype(jnp.int32)
    w_in_b, w_sq_b = w_in[0].astype(BF16), w_sq.astype(BF16)
    h, h_t = _prenorm(x[0], norm_g)
    proj, qkv, w_all, wsq = _gather_inproj(idx, h, w_in_b, w_sq_b)
    w_all, wsq = _place_own(idx, w_in_b, w_sq_b, w_all, wsq)
    wsq = wsq.reshape(3, D_MODEL, D_MODEL)

    (g_in, g_sb, g_hg, g_out, segs, dout, loss, d_bg, d_lb, d_hgn, d_fg) = _local_grads(
        x[0], loss_target[0], proj, h_t, qkv, b_gate, lb_logits.reshape(2, D_MODEL), hg_norm_g.reshape(1, D_MODEL),
        final_norm_g.reshape(1, D_MODEL), wsq[0], wsq[1], wsq[2])

    g_sq = jnp.stack([g_sb, g_hg, g_out]).reshape(3, N_CHIPS, ROW_SHARD, D_MODEL)
    got_in, got_sq = _swap_halves(g_in, g_sq)
    s_in, s_sq = _sum_a_in(idx, g_in, got_in), _sum_a_sq(idx, g_sq, got_sq)
    grad_x, d_ng, got_in, got_sq = _dx(segs, w_all, x[0], norm_g, dout, s_in, s_sq)
    grad_in, grad_sq = _join_halves(_sum_b_in(idx, s_in, got_in), _sum_b_sq(idx, s_sq, got_sq))

    d_in, nm_in, nv_in = _adamw(w_in[0], grad_in, m_w_in[0], v_w_in[0], "adamw_in")
    flat = lambda a, b, c: jnp.concatenate([a[0], b[0], c[0]], axis=0)
    d_sq, nm_sq, nv_sq = _adamw(flat(w_sb_proj, w_hg_proj, w_out), grad_sq.reshape(3 * ROW_SHARD, D_MODEL),
                                flat(m_w_sb_proj, m_w_hg_proj, m_w_out), flat(v_w_sb_proj, v_w_hg_proj, v_w_out),
                                "adamw_sq")

    pad = jnp.zeros((8, HEAD_DIM), F32).at[0, 0].set(loss[0, 0])
    part = jnp.concatenate([pad] + [a.reshape(-1, HEAD_DIM) for a in (d_ng, d_bg, d_lb, d_hgn, d_fg)], axis=0)
    sums = _sum_small(part)
    loss_out, g_sm, d_sm, nm_sm, nv_sm = _adamw_small(
        sums, _pack_small(norm_g, b_gate, lb_logits, hg_norm_g, final_norm_g),
        _pack_small(m_norm_g, m_b_gate, m_lb_logits, m_hg_norm_g, m_final_norm_g),
        _pack_small(v_norm_g, v_b_gate, v_lb_logits, v_hg_norm_g, v_final_norm_g))

    def big(t_in, t_sq):
        sq = t_sq.reshape(3, 1, ROW_SHARD, D_MODEL)
        return t_in[None], sq[0], sq[1], sq[2]

    def order(small, in_, sb, hg, out):
        ng, bg, lbl, hgn, fg = small
        return [ng, in_, bg, lbl, hgn, sb, hg, out, fg]

    outs = [loss_out[0, 0], grad_x[None]]
    for small, (t_in, t_sq) in ((g_sm, (grad_in, grad_sq)), (d_sm, (d_in, d_sq)), (nm_sm, (nm_in, nm_sq)), (nv_sm, (nv_in, nv_sq))):
        outs += order(_unpack_small(small), *big(t_in, t_sq))
    return tuple(outs)
```

```python
import functools

import jax
import jax.numpy as jnp
from jax import lax
from jax.experimental import pallas as pl
from jax.experimental.pallas import tpu as pltpu

F32 = jnp.float32
BF16 = jnp.bfloat16

D_MODEL = 1024
HEADS = 8
HEAD_DIM = 128
IN_WIDTH = 10240
N_CHIPS = 4
W_IN_SHARD = IN_WIDTH // N_CHIPS
ROW_SHARD = D_MODEL // N_CHIPS
RMS_EPS = 1e-6

OFF_SB_Q, OFF_SB_K, OFF_SB_V, OFF_SB_Z = 0, 1024, 2048, 3072
OFF_HG_Q, OFF_HG_F, OFF_HG_I, OFF_HG_Z, OFF_GATE = 4096, 5120, 6144, 7168, 8192

SB_BLOCK = 256
SB_FWD_HEADS = 4
SB_BWD_HEADS = 2
SB_BWD_GROUPS = 2
SB_ROWS = 256
SB_DEAD = -110.0
SB_GONE = -1e30
HG_CHUNK = 32
HG_PAIR = 2 * HG_CHUNK
HG_STEP = 256
HG_MID = HG_CHUNK // 2 - 1

ADAM_LR, ADAM_B1, ADAM_B2, ADAM_EPS, ADAM_WD, ADAM_STEP = 0.001, 0.9, 0.999, 1e-08, 0.01, 10

VMEM_LIMIT = 56 * 1024 * 1024
VMEM_LIMIT_DX = 60 * 1024 * 1024

MESH = pl.DeviceIdType.MESH


def _cparams(sem, vmem=VMEM_LIMIT):
    return pltpu.CompilerParams(dimension_semantics=sem, vmem_limit_bytes=vmem)


def _dot(a, b):
    return jnp.dot(a, b, preferred_element_type=F32)


def _dot_nt(a, b):
    return lax.dot_general(a, b, (((1,), (1,)), ((), ())), preferred_element_type=F32)


def _dot_tn(a, b):
    return lax.dot_general(a, b, (((0,), (0,)), ((), ())), preferred_element_type=F32)


def _split_dot(x, tri):
    hi = x.astype(BF16)
    lo = (x - hi.astype(F32)).astype(BF16)
    both = _dot(jnp.concatenate([hi, lo], axis=0), tri)
    return both[: x.shape[0]] + both[x.shape[0] :]


def _split_dot_left(tri, x):
    hi = x.astype(BF16)
    lo = (x - hi.astype(F32)).astype(BF16)
    return _dot(tri, hi) + _dot(tri, lo)


def _sigmoid(x):
    return 1.0 / (1.0 + jnp.exp(-x))


def _prenorm(x, norm_g):
    s_len = x.shape[0]
    ts = min(1024, s_len)

    def body(x_ref, g_ref, h_ref, ht_ref):
        xv = x_ref[...]
        r = lax.rsqrt(jnp.mean(xv * xv, axis=-1, keepdims=True) + RMS_EPS)
        hv = (xv * r) * g_ref[...]
        h_ref[...] = hv.astype(BF16)
        ht_ref[...] = hv.T.astype(BF16)

    return pl.pallas_call(
        body,
        name="prenorm",
        grid=(s_len // ts,),
        in_specs=[pl.BlockSpec((ts, D_MODEL), lambda s: (s, 0)), pl.BlockSpec((1, D_MODEL), lambda s: (0, 0))],
        out_specs=[pl.BlockSpec((ts, D_MODEL), lambda s: (s, 0)), pl.BlockSpec((D_MODEL, ts), lambda s: (0, s))],
        out_shape=[jax.ShapeDtypeStruct((s_len, D_MODEL), BF16), jax.ShapeDtypeStruct((D_MODEL, s_len), BF16)],
        compiler_params=_cparams(("arbitrary",)),
    )(x, norm_g)


def _sb_scores(qb, kb, causal, tri_excl, diag):
    z = _dot_nt(qb, kb) * HEAD_DIM ** -0.5
    ls_pos = jnp.minimum(z, 0.0) - jnp.log1p(jnp.exp(-jnp.abs(z)))
    log_not = ls_pos - z
    log_not_m = jnp.where(causal, log_not, 0.0) if diag else log_not
    return ls_pos, log_not, log_not_m, _split_dot(log_not_m, tri_excl)


def _sb_weights(ls_pos, suffix, carry, causal, diag):
    surv = suffix + carry
    w = jnp.exp(ls_pos + surv)
    return surv, (jnp.where(causal, w, 0.0) if diag else w)


def _sb_specs(s_len, blk, heads):
    width = heads * HEAD_DIM

    def blk_spec(off):
        return pl.BlockSpec((blk, width), lambda h, i: (i, off // width + h))

    def head_spec(off, buffers=2):
        return pl.BlockSpec((s_len, width), lambda h, i: (0, off // width + h), pipeline_mode=pl.Buffered(buffers))

    return blk_spec, head_spec


def _head_cols(p):
    return slice(p * HEAD_DIM, (p + 1) * HEAD_DIM)


def _sb_chains(blk, heads):
    rows = min(SB_ROWS, blk)
    return [(p, a) for p in range(heads) for a in range(blk // rows)], rows


def _sb_masks(blk, rows):
    row = lax.broadcasted_iota(jnp.int32, (rows, blk), 0)
    col = lax.broadcasted_iota(jnp.int32, (rows, blk), 1)
    causal = [row + a * rows > col for a in range(blk // rows)]
    row = lax.broadcasted_iota(jnp.int32, (blk, blk), 0)
    col = lax.broadcasted_iota(jnp.int32, (blk, blk), 1)
    tri_excl = (row > col).astype(BF16)
    tri_incl = (row >= col).astype(BF16)
    return causal, tri_excl, tri_incl


def _sb_alive(st, n_chain):
    alive = functools.reduce(jnp.maximum, [st[1 + 3 * c] for c in range(n_chain)])
    return jnp.max(alive) > SB_DEAD


def _sb_fwd(qkv):
    s_len = qkv.shape[0]
    blk = min(SB_BLOCK, s_len)
    nq = s_len // blk
    chains, rows = _sb_chains(blk, SB_FWD_HEADS)

    def body(q_ref, k_ref, v_ref, o_ref, of_ref):
        i = pl.program_id(1)
        causal, tri_excl, _ = _sb_masks(blk, rows)

        def tiles(specs, st):
            pre = []
            for j, diag, _ in specs:
                start = pl.multiple_of(j * blk, blk)
                for p, a in chains:
                    kb = k_ref[pl.ds(start, blk), _head_cols(p)]
                    qb = q_ref[a * rows : (a + 1) * rows, _head_cols(p)]
                    pre.append(_sb_scores(qb, kb, causal[a], tri_excl, diag) + (v_ref[pl.ds(start, blk), _head_cols(p)],))
            for t, (j, diag, valid) in enumerate(specs):
                new = []
                for c, (p, a) in enumerate(chains):
                    carry, acc, acc_lo = st[3 * c : 3 * c + 3]
                    if valid is not None:
                        carry = jnp.where(valid, carry, SB_GONE)
                    ls_pos, _, log_not_m, suffix, vb = pre[t * len(chains) + c]
                    surv, w = _sb_weights(ls_pos, suffix, carry, causal[a], diag)
                    wb = w.astype(BF16)
                    w_lo = (w - wb.astype(F32)).astype(BF16)
                    both = _dot(jnp.concatenate([wb, w_lo], axis=0), vb)
                    new += [surv[:, 0:1] + log_not_m[:, 0:1], acc + both[:rows], acc_lo + both[rows:]]
                st = tuple(new)
            return st

        zero = jnp.zeros((rows, HEAD_DIM), F32)
        st = tiles([(i, True, None), (jnp.maximum(i - 1, 0), False, i >= 1)],
                   (jnp.zeros((rows, 1), F32), zero, zero) * len(chains))

        def more(st):
            return (st[0] < i) & _sb_alive(st, len(chains))

        def step(st):
            return (st[0] + 1,) + tiles([(i - 1 - st[0], False, None)], st[1:])

        st = lax.while_loop(more, step, (1,) + st)[1:]
        for c, (p, a) in enumerate(chains):
            o_ref[a * rows : (a + 1) * rows, _head_cols(p)] = st[3 * c + 1]
            of_ref[a * rows : (a + 1) * rows, _head_cols(p)] = st[3 * c + 1] + st[3 * c + 2]

    blk_spec, head_spec = _sb_specs(s_len, blk, SB_FWD_HEADS)
    return pl.pallas_call(
        body,
        name="sb_fwd",
        grid=(HEADS // SB_FWD_HEADS, nq),
        in_specs=[blk_spec(OFF_SB_Q), head_spec(OFF_SB_K), head_spec(OFF_SB_V)],
        out_specs=[blk_spec(0), blk_spec(0)],
        out_shape=[jax.ShapeDtypeStruct((s_len, D_MODEL), F32)] * 2,
        compiler_params=_cparams(("arbitrary", "arbitrary")),
    )(qkv, qkv, qkv)


def _sb_bwd(qkv, o_fine, d_o):
    s_len = qkv.shape[0]
    kb_rows = min(SB_BLOCK, s_len)
    groups = min(SB_BWD_GROUPS, s_len // kb_rows)
    blk = groups * kb_rows
    nq = s_len // blk
    scale = HEAD_DIM ** -0.5
    rows = kb_rows
    chains = [(p, a) for p in range(SB_BWD_HEADS) for a in range(groups)]

    def body(q_ref, k_ref, v_ref, of_ref, do_ref, dq_ref, dk_ref, dv_ref, dk_acc, dv_acc):
        i = pl.program_id(1)

        @pl.when(i == 0)
        def _():
            dk_acc[...] = jnp.zeros_like(dk_acc)
            dv_acc[...] = jnp.zeros_like(dv_acc)

        dob = do_ref[...].astype(BF16)
        prod = dob.astype(F32) * of_ref[...]
        masks, tri_excl, tri_incl = _sb_masks(kb_rows, rows)
        causal = masks[0]

        def group(x, p, a):
            return x[a * rows : (a + 1) * rows, _head_cols(p)]

        totals = [jnp.sum(group(prod, p, a), axis=-1, keepdims=True) for p, a in chains]

        def tiles(steps, st):
            pre = []
            for n, diag in steps:
                for p, a in chains:
                    j = groups * i + a - n
                    start = pl.multiple_of(jnp.maximum(j, 0) * kb_rows, kb_rows)
                    kb = k_ref[pl.ds(start, kb_rows), _head_cols(p)]
                    vb = v_ref[pl.ds(start, kb_rows), _head_cols(p)]
                    qb, dob_c = group(q_ref, p, a), group(dob, p, a)
                    pre.append(_sb_scores(qb, kb, causal, tri_excl, diag) + (_dot_nt(dob_c, vb), qb, kb, dob_c, j, start))
            for t, (_, diag) in enumerate(steps):
                mids = []
                for c, (p, a) in enumerate(chains):
                    ls_pos, _, _, suffix, d_w = pre[t * len(chains) + c][:5]
                    c_not = st[3 * c]
                    if not diag:
                        c_not = jnp.where(pre[t * len(chains) + c][8] >= 0, c_not, SB_GONE)
                    surv, w = _sb_weights(ls_pos, suffix, c_not, causal, diag)
                    dlw = d_w * w
                    mids.append((surv, w, dlw, _split_dot(dlw, tri_incl)))
                new = []
                for c, (p, a) in enumerate(chains):
                    c_dlw, dq = st[3 * c + 1 : 3 * c + 3]
                    ls_pos, log_not, log_not_m, _, _, qb, kb, dob_c, j, start = pre[t * len(chains) + c]
                    surv, w, dlw, suffix = mids[c]
                    d_not = totals[c] - c_dlw - suffix
                    dz = ((dlw + d_not) * jnp.exp(log_not) - d_not) * scale
                    dz = jnp.where(causal, dz, 0.0) if diag else jnp.where(j >= 0, dz, 0.0)
                    dzb = dz.astype(BF16)
                    dk_acc[pl.ds(start, kb_rows), _head_cols(p)] += _dot_tn(dzb, qb)
                    dv_acc[pl.ds(start, kb_rows), _head_cols(p)] += _dot_tn(w.astype(BF16), dob_c)
                    new += [surv[:, 0:1] + log_not_m[:, 0:1], c_dlw + suffix[:, 0:1], dq + _dot(dzb, kb)]
                st = tuple(new)
            return st

        zcol = jnp.zeros((rows, 1), F32)
        st = tiles([(0, True), (1, False)], (zcol, zcol, jnp.zeros((rows, HEAD_DIM), F32)) * len(chains))

        def more(st):
            return (st[0] <= groups * i + groups - 1) & _sb_alive(st, len(chains))

        def step(st):
            return (st[0] + 1,) + tiles([(st[0], False)], st[1:])

        st = lax.while_loop(more, step, (2,) + st)[1:]
        for c, (p, a) in enumerate(chains):
            dq_ref[a * rows : (a + 1) * rows, _head_cols(p)] = st[3 * c + 2].astype(BF16)

        @pl.when(i == nq - 1)
        def _():
            dk_ref[...] = dk_acc[...].astype(BF16)
            dv_ref[...] = dv_acc[...].astype(BF16)

    blk_spec, head_spec = _sb_specs(s_len, blk, SB_BWD_HEADS)
    width = SB_BWD_HEADS * HEAD_DIM
    return pl.pallas_call(
        body,
        name="sb_bwd",
        grid=(HEADS // SB_BWD_HEADS, nq),
        in_specs=[blk_spec(OFF_SB_Q), head_spec(OFF_SB_K, 1), head_spec(OFF_SB_V, 1), blk_spec(0), blk_spec(0)],
        out_specs=[blk_spec(0), head_spec(0), head_spec(0)],
        out_shape=[jax.ShapeDtypeStruct((s_len, D_MODEL), BF16)] * 3,
        scratch_shapes=[pltpu.VMEM((s_len, width), F32), pltpu.VMEM((s_len, width), F32)],
        compiler_params=_cparams(("arbitrary", "arbitrary"), vmem=VMEM_LIMIT_DX),
    )(qkv, qkv, qkv, o_fine, d_o)


def _hg_lower_bound(lbl_ref):
    l0 = lbl_ref[0:1, :]
    l1 = lbl_ref[1:2, :]
    mx = jnp.maximum(l0, l1)
    e0 = jnp.exp(l0 - mx)
    e1 = jnp.exp(l1 - mx)
    return e0 / (e0 + e1)


def _hg_gates(hq, hf, lb):
    sig_f = _sigmoid(hf)
    f = lb + (1.0 - lb) * sig_f
    g = jnp.log(f)
    kk = 1.0 - f
    sig_q = _sigmoid(hq)
    qq = hq * sig_q
    return qq, kk, g, f, sig_f, sig_q


def _period_bcast(x, r, rows, period):
    w = x.shape[-1]
    x3 = x.reshape(rows // period, period, w)
    return jnp.broadcast_to(x3[:, r : r + 1, :], x3.shape).reshape(rows, w)


def _blockdiag(rows, kind):
    row = lax.broadcasted_iota(jnp.int32, (rows, rows), 0)
    col = lax.broadcasted_iota(jnp.int32, (rows, rows), 1)
    if kind in ("next", "prev"):
        first, second = (row, col) if kind == "next" else (col, row)
        keep = ((row // HG_PAIR) == (col // HG_PAIR)) & (first % HG_PAIR < HG_CHUNK) & (second % HG_PAIR >= HG_CHUNK)
    else:
        keep = (row // HG_CHUNK) == (col // HG_CHUNK)
        if kind == "lower":
            keep = keep & (row >= col)
        elif kind == "upper":
            keep = keep & (row <= col)
    return jnp.where(keep, 1.0, 0.0).astype(BF16)


def _hg_operands(hq, hf, lb, rows):
    qq, kk, g, f, sig_f, sig_q = _hg_gates(hq, hf, lb)
    cum = _split_dot_left(_blockdiag(rows, "lower"), g)
    mid = _period_bcast(cum, HG_MID, rows, HG_CHUNK)
    last = _period_bcast(cum, HG_CHUNK - 1, rows, HG_CHUNK)
    last0 = _period_bcast(cum, HG_CHUNK - 1, rows, HG_PAIR)
    last1 = _period_bcast(cum, HG_PAIR - 1, rows, HG_PAIR)
    second = (lax.broadcasted_iota(jnp.int32, cum.shape, 0) % HG_PAIR) >= HG_CHUNK
    e = dict(qm=jnp.exp(cum - mid), km=jnp.exp(mid - cum), qd=jnp.exp(cum), kl=jnp.exp(last - cum),
             q_in=jnp.where(second, jnp.exp(last0), 1.0), k_out=jnp.where(second, 1.0, jnp.exp(last1)),
             pair=jnp.exp(last0 + last1))
    v = dict(qm=qq * e["qm"], km=kk * e["km"], qd=qq * e["qd"], kl=kk * e["kl"])
    v["qp"] = v["qd"] * e["q_in"]
    v["kp"] = v["kl"] * e["k_out"]
    return v, e, second, (f, sig_f, sig_q)


def _hg_store_operands(v, second, hi, refs):
    zero = jnp.zeros_like(v["qm"])
    q_cat, k_cat, qp_b, kp_b, v_b = refs
    q_cat[:, 0:D_MODEL] = jnp.where(second, zero, v["qm"]).astype(BF16)
    q_cat[:, D_MODEL : 2 * D_MODEL] = jnp.where(second, v["qm"], zero).astype(BF16)
    q_cat[:, 2 * D_MODEL :] = jnp.where(second, v["qd"], zero).astype(BF16)
    k_cat[:, 0:D_MODEL] = jnp.where(second, zero, v["km"]).astype(BF16)
    k_cat[:, D_MODEL : 2 * D_MODEL] = jnp.where(second, v["km"], zero).astype(BF16)
    k_cat[:, 2 * D_MODEL :] = jnp.where(second, zero, v["kl"]).astype(BF16)
    qp_b[...] = v["qp"].astype(BF16)
    kp_b[...] = v["kp"].astype(BF16)
    v_b[...] = hi.astype(BF16)


def _hg_pair_operands(cat, r0, c0):
    return jnp.concatenate([cat[r0 : r0 + HG_PAIR, g * D_MODEL + c0 : g * D_MODEL + c0 + HEAD_DIM] for g in range(3)], axis=1)


def _hg_fwd(proj, lbl):
    s_len = proj.shape[0]
    rows = min(HG_STEP, s_len)
    n_pairs = rows // HG_PAIR

    def body(hq_ref, hf_ref, hi_ref, lbl_ref, o_ref, st_ref, state, q_cat, k_cat, qp_b, kp_b, v_b):
        @pl.when(pl.program_id(0) == 0)
        def _():
            state[...] = jnp.zeros_like(state)

        v, e, second, _ = _hg_operands(hq_ref[...], hf_ref[...], _hg_lower_bound(lbl_ref), rows)
        _hg_store_operands(v, second, hi_ref[...], (q_cat, k_cat, qp_b, kp_b, v_b))
        e_pair = e["pair"]
        row = lax.broadcasted_iota(jnp.int32, (HG_PAIR, HG_PAIR), 0)
        col = lax.broadcasted_iota(jnp.int32, (HG_PAIR, HG_PAIR), 1)
        causal = row >= col

        for u in range(n_pairs):
            r0 = u * HG_PAIR
            sls = [(slice(r0, r0 + HG_PAIR), slice(h * HEAD_DIM, (h + 1) * HEAD_DIM)) for h in range(HEADS)]
            a_s = [jnp.where(causal, _dot_nt(_hg_pair_operands(q_cat, r0, h * HEAD_DIM),
                                             _hg_pair_operands(k_cat, r0, h * HEAD_DIM)), 0.0).astype(BF16)
                   for h in range(HEADS)]
            st_s = [state[h] for h in range(HEADS)]
            for h, sl in enumerate(sls):
                st_ref[u, h] = st_s[h]
                state[h] = st_s[h] * e_pair[r0 : r0 + 1, sl[1]] + _dot_tn(v_b[sl], kp_b[sl])
            for h, sl in enumerate(sls):
                o_ref[sl] = _dot(a_s[h], v_b[sl]) + _dot_nt(qp_b[sl], st_s[h].astype(BF16))

    def col_spec(off):
        return pl.BlockSpec((rows, D_MODEL), lambda s: (s, off // D_MODEL))

    bf_tile = pltpu.VMEM((rows, D_MODEL), BF16)
    bf_cat = pltpu.VMEM((rows, 3 * D_MODEL), BF16)
    scratch = [pltpu.VMEM((HEADS, HEAD_DIM, HEAD_DIM), F32), bf_cat, bf_cat, bf_tile, bf_tile, bf_tile]
    return pl.pallas_call(
        body,
        name="hg_fwd",
        grid=(s_len // rows,),
        in_specs=[col_spec(OFF_HG_Q), col_spec(OFF_HG_F), col_spec(OFF_HG_I), pl.BlockSpec((2, D_MODEL), lambda s: (0, 0))],
        out_specs=[
            pl.BlockSpec((rows, D_MODEL), lambda s: (s, 0)),
            pl.BlockSpec((n_pairs, HEADS, HEAD_DIM, HEAD_DIM), lambda s: (s, 0, 0, 0)),
        ],
        out_shape=[
            jax.ShapeDtypeStruct((s_len, D_MODEL), F32),
            jax.ShapeDtypeStruct((s_len // HG_PAIR, HEADS, HEAD_DIM, HEAD_DIM), F32),
        ],
        scratch_shapes=scratch,
        compiler_params=_cparams(("arbitrary",)),
    )(proj, proj, proj, lbl)


def _hg_bwd(proj, lbl, states, d_o):
    s_len = proj.shape[0]
    rows = min(HG_STEP, s_len)
    n_pairs = rows // HG_PAIR
    n_steps = s_len // rows

    def body(hq_ref, hf_ref, hi_ref, lbl_ref, st_ref, do_ref, dp_ref, dlb_ref,
             dstate, q_cat, k_cat, qp_b, kp_b, v_b, do_b, d_qcat, d_kcat, d_qp, d_kp, d_v, d_pair):
        @pl.when(pl.program_id(0) == 0)
        def _():
            dstate[...] = jnp.zeros_like(dstate)
            dlb_ref[...] = jnp.zeros_like(dlb_ref)

        lb = _hg_lower_bound(lbl_ref)
        hq = hq_ref[...]
        v, e, second, (f, sig_f, sig_q) = _hg_operands(hq, hf_ref[...], lb, rows)
        _hg_store_operands(v, second, hi_ref[...], (q_cat, k_cat, qp_b, kp_b, v_b))
        do_b[...] = do_ref[...].astype(BF16)
        e_pair = e["pair"]
        row = lax.broadcasted_iota(jnp.int32, (HG_PAIR, HG_PAIR), 0)
        col = lax.broadcasted_iota(jnp.int32, (HG_PAIR, HG_PAIR), 1)
        causal = row >= col

        for u in reversed(range(n_pairs)):
            r0 = u * HG_PAIR
            sls = [(slice(r0, r0 + HG_PAIR), slice(h * HEAD_DIM, (h + 1) * HEAD_DIM)) for h in range(HEADS)]
            ops = [(_hg_pair_operands(q_cat, r0, h * HEAD_DIM), _hg_pair_operands(k_cat, r0, h * HEAD_DIM))
                   for h in range(HEADS)]
            a_s = [jnp.where(causal, _dot_nt(lhs, rhs), 0.0).astype(BF16) for lhs, rhs in ops]
            da_s = [jnp.where(causal, _dot_nt(do_b[sl], v_b[sl]), 0.0).astype(BF16) for sl in sls]
            st0_s = [st_ref[u, h] for h in range(HEADS)]
            ds1_s = [dstate[h] for h in range(HEADS)]
            ds1b_s = [ds1.astype(BF16) for ds1 in ds1_s]
            for h, sl in enumerate(sls):
                decay = e_pair[r0 : r0 + 1, sl[1]]
                d_pair[u : u + 1, sl[1]] = decay * jnp.sum(ds1_s[h] * st0_s[h], axis=0, keepdims=True)
                dstate[h] = ds1_s[h] * decay + _dot_tn(do_b[sl], qp_b[sl])
            for h, sl in enumerate(sls):
                d_qp[sl] = _dot(do_b[sl], st0_s[h].astype(BF16))
                d_kp[sl] = _dot(v_b[sl], ds1b_s[h])
            for h, sl in enumerate(sls):
                d_v[sl] = _dot_tn(a_s[h], do_b[sl]) + _dot_nt(kp_b[sl], ds1b_s[h])
            for h, sl in enumerate(sls):
                d_lhs = _dot(da_s[h], ops[h][1])
                d_rhs = _dot_tn(da_s[h], ops[h][0])
                for g in range(3):
                    gsl = (sl[0], slice(g * D_MODEL + h * HEAD_DIM, g * D_MODEL + (h + 1) * HEAD_DIM))
                    d_qcat[gsl] = d_lhs[:, g * HEAD_DIM : (g + 1) * HEAD_DIM]
                    d_kcat[gsl] = d_rhs[:, g * HEAD_DIM : (g + 1) * HEAD_DIM]

        zero = jnp.zeros_like(hq)
        dqm = jnp.where(second, d_qcat[:, D_MODEL : 2 * D_MODEL], d_qcat[:, 0:D_MODEL])
        dkm = jnp.where(second, d_kcat[:, D_MODEL : 2 * D_MODEL], d_kcat[:, 0:D_MODEL])
        dqp, dkp = d_qp[...], d_kp[...]
        dqd = dqp * e["q_in"] + jnp.where(second, d_qcat[:, 2 * D_MODEL :], zero)
        dkl = dkp * e["k_out"] + jnp.where(second, zero, d_kcat[:, 2 * D_MODEL :])
        dq = dqm * e["qm"] + dqd * e["qd"]
        dk = dkm * e["km"] + dkl * e["kl"]
        t_kl = dkl * v["kl"]
        dcum = dqm * v["qm"] - dkm * v["km"] + dqd * v["qd"] - t_kl
        dp = d_pair[...]
        dp_b = jnp.broadcast_to(dp[:, None, :], (n_pairs, HG_PAIR, D_MODEL)).reshape(rows, D_MODEL)
        dg = (_split_dot_left(_blockdiag(rows, "upper"), dcum) + _split_dot_left(_blockdiag(rows, "all"), t_kl)
              + _split_dot_left(_blockdiag(rows, "next"), dqp * v["qp"])
              + _split_dot_left(_blockdiag(rows, "prev"), dkp * v["kp"]) + dp_b)
        df = dg / f - dk
        one_m = 1.0 - sig_f
        dp_ref[:, 0:D_MODEL] = (dq * (sig_q * (1.0 + hq * (1.0 - sig_q)))).astype(BF16)
        dp_ref[:, D_MODEL : 2 * D_MODEL] = (df * (1.0 - lb) * sig_f * one_m).astype(BF16)
        dp_ref[:, 2 * D_MODEL : 3 * D_MODEL] = d_v[...].astype(BF16)
        dlb_ref[...] += jnp.sum(df * one_m, axis=0, keepdims=True)

    def col_spec(off):
        return pl.BlockSpec((rows, D_MODEL), lambda s: (n_steps - 1 - s, off // D_MODEL))

    f32_tile = pltpu.VMEM((rows, D_MODEL), F32)
    f32_cat = pltpu.VMEM((rows, 3 * D_MODEL), F32)
    bf_tile = pltpu.VMEM((rows, D_MODEL), BF16)
    bf_cat = pltpu.VMEM((rows, 3 * D_MODEL), BF16)
    scratch = [pltpu.VMEM((HEADS, HEAD_DIM, HEAD_DIM), F32), bf_cat, bf_cat, bf_tile, bf_tile, bf_tile, bf_tile,
               f32_cat, f32_cat, f32_tile, f32_tile, f32_tile, pltpu.VMEM((n_pairs, D_MODEL), F32)]
    return pl.pallas_call(
        body,
        name="hg_bwd",
        grid=(n_steps,),
        in_specs=[
            col_spec(OFF_HG_Q), col_spec(OFF_HG_F), col_spec(OFF_HG_I),
            pl.BlockSpec((2, D_MODEL), lambda s: (0, 0)),
            pl.BlockSpec((n_pairs, HEADS, HEAD_DIM, HEAD_DIM), lambda s: (n_steps - 1 - s, 0, 0, 0)),
            pl.BlockSpec((rows, D_MODEL), lambda s: (n_steps - 1 - s, 0)),
        ],
        out_specs=[
            pl.BlockSpec((rows, 3 * D_MODEL), lambda s: (n_steps - 1 - s, 0)),
            pl.BlockSpec((1, D_MODEL), lambda s: (0, 0)),
        ],
        out_shape=[
            jax.ShapeDtypeStruct((s_len, 3 * D_MODEL), BF16),
            jax.ShapeDtypeStruct((1, D_MODEL), F32),
        ],
        scratch_shapes=scratch,
        compiler_params=_cparams(("arbitrary",)),
    )(proj, proj, proj, lbl, states, d_o)


def _mid(proj, sb_o, hg_o, x, target, b_gate, hg_gain, final_g, w_sb, w_hg, w_out):
    s_len = proj.shape[0]
    ts = min(256, s_len)
    inv_d = 1.0 / D_MODEL

    def body(zsb_ref, hz_ref, gl_ref, sbo_ref, hgo_ref, x_ref, tgt_ref, bg_ref, hgn_ref, fg_ref,
             wsb_ref, whg_ref, wout_ref,
             dout_ref, dsbo_ref, dhgo_ref, dmid_ref,
             asb_ref, dusb_ref, ahg_ref, duhg_ref, y_ref, doutb_ref,
             loss_ref, dfg_ref, dbg_ref, dhgn_ref):
        @pl.when(pl.program_id(0) == 0)
        def _():
            loss_ref[...] = jnp.zeros_like(loss_ref)
            dfg_ref[...] = jnp.zeros_like(dfg_ref)
            dbg_ref[...] = jnp.zeros_like(dbg_ref)
            dhgn_ref[...] = jnp.zeros_like(dhgn_ref)

        z_sb = zsb_ref[...]
        sb_o = sbo_ref[...]
        sig_zsb = _sigmoid(z_sb)
        silu_zsb = z_sb * sig_zsb
        a_sb_f = sb_o * silu_zsb
        a_sb = a_sb_f.astype(BF16)
        u_sb = _dot(a_sb, wsb_ref[...])

        hg_o = hgo_ref[...]
        gain = hgn_ref[...]
        r_parts, yn_parts = [], []
        for h in range(HEADS):
            oh = hg_o[:, h * HEAD_DIM : (h + 1) * HEAD_DIM]
            r = lax.rsqrt(jnp.mean(oh * oh, axis=-1, keepdims=True) + RMS_EPS)
            r_parts.append(jnp.broadcast_to(r, oh.shape))
            yn_parts.append(oh * r)
        r_hg = jnp.concatenate(r_parts, axis=-1)
        yn_hg = jnp.concatenate(yn_parts, axis=-1)
        hn = yn_hg * gain
        hz = hz_ref[...]
        sig_hz = _sigmoid(hz)
        silu_hz = hz * sig_hz
        a_hg_f = hn * silu_hz
        a_hg = a_hg_f.astype(BF16)
        u_hg = _dot(a_hg, whg_ref[...])

        gates = _sigmoid(gl_ref[...] + bg_ref[...])
        g_sb = gates[:, 0:D_MODEL]
        g_hg = gates[:, D_MODEL:]
        y_f = g_sb * u_sb + g_hg * u_hg
        y = y_f.astype(BF16)
        out = x_ref[...] + _dot(y, wout_ref[...])
        r2 = lax.rsqrt(jnp.mean(out * out, axis=-1, keepdims=True) + RMS_EPS)
        yn = out * r2
        fg = fg_ref[...]
        diff = yn * fg - tgt_ref[...]
        loss_ref[...] += 0.5 * inv_d * jnp.sum(diff * diff)

        dyf = diff * inv_d
        dfg_ref[...] += jnp.sum(dyf * yn, axis=0, keepdims=True)
        dyn = dyf * fg
        dout = r2 * (dyn - yn * jnp.mean(dyn * yn, axis=-1, keepdims=True))
        dout_ref[...] = dout
        doutb = dout.astype(BF16)
        doutb_ref[...] = doutb
        dy = _dot_nt(doutb, wout_ref[...])
        du_sb = (dy * g_sb).astype(BF16)
        du_hg = (dy * g_hg).astype(BF16)
        dgl_sb = dy * u_sb * g_sb * (1.0 - g_sb)
        dgl_hg = dy * u_hg * g_hg * (1.0 - g_hg)
        dmid_ref[:, 2 * D_MODEL : 3 * D_MODEL] = dgl_sb.astype(BF16)
        dmid_ref[:, 3 * D_MODEL :] = dgl_hg.astype(BF16)
        dbg_ref[:, 0:D_MODEL] += jnp.sum(dgl_sb, axis=0, keepdims=True)
        dbg_ref[:, D_MODEL:] += jnp.sum(dgl_hg, axis=0, keepdims=True)

        da_sb = _dot_nt(du_sb, wsb_ref[...])
        dsbo_ref[...] = (da_sb * silu_zsb).astype(BF16)
        dmid_ref[:, 0:D_MODEL] = (da_sb * sb_o * (sig_zsb * (1.0 + z_sb * (1.0 - sig_zsb)))).astype(BF16)

        da_hg = _dot_nt(du_hg, whg_ref[...])
        dhn = da_hg * silu_hz
        dmid_ref[:, D_MODEL : 2 * D_MODEL] = (da_hg * hn * (sig_hz * (1.0 + hz * (1.0 - sig_hz)))).astype(BF16)
        dhgn_ref[...] += jnp.sum(dhn * yn_hg, axis=0, keepdims=True)
        dyn_hg = dhn * gain
        prod = dyn_hg * yn_hg
        m_parts = []
        for h in range(HEADS):
            ph = prod[:, h * HEAD_DIM : (h + 1) * HEAD_DIM]
            m_parts.append(jnp.broadcast_to(jnp.mean(ph, axis=-1, keepdims=True), ph.shape))
        dhgo_ref[...] = (r_hg * (dyn_hg - yn_hg * jnp.concatenate(m_parts, axis=-1))).astype(BF16)

        asb_ref[...] = a_sb_f.T.astype(BF16)
        dusb_ref[...] = du_sb
        ahg_ref[...] = a_hg_f.T.astype(BF16)
        duhg_ref[...] = du_hg
        y_ref[...] = y_f.T.astype(BF16)

    def tile(width, off=0):
        return pl.BlockSpec((ts, width), lambda s: (s, off // width))

    def across():
        return pl.BlockSpec((D_MODEL, ts), lambda s: (0, s))

    def whole(shape):
        return pl.BlockSpec(shape, lambda s: (0,) * len(shape))

    def weight():
        return pl.BlockSpec((D_MODEL, D_MODEL), lambda s: (0, 0), pipeline_mode=pl.Buffered(1))

    f32_act = jax.ShapeDtypeStruct((s_len, D_MODEL), F32)
    bf_act = jax.ShapeDtypeStruct((s_len, D_MODEL), BF16)
    bf_act_t = jax.ShapeDtypeStruct((D_MODEL, s_len), BF16)
    return pl.pallas_call(
        body,
        name="mid",
        grid=(s_len // ts,),
        in_specs=[
            tile(D_MODEL, OFF_SB_Z), tile(D_MODEL, OFF_HG_Z), tile(2 * D_MODEL, OFF_GATE),
            tile(D_MODEL), tile(D_MODEL), tile(D_MODEL), tile(D_MODEL),
            whole((1, 2 * D_MODEL)), whole((1, D_MODEL)), whole((1, D_MODEL)),
            weight(), weight(), weight(),
        ],
        out_specs=[
            tile(D_MODEL), tile(D_MODEL), tile(D_MODEL), tile(4 * D_MODEL),
            across(), tile(D_MODEL), across(), tile(D_MODEL), across(), tile(D_MODEL),
            whole((1, 1)), whole((1, D_MODEL)), whole((1, 2 * D_MODEL)), whole((1, D_MODEL)),
        ],
        out_shape=[
            f32_act, bf_act, bf_act, jax.ShapeDtypeStruct((s_len, 4 * D_MODEL), BF16),
            bf_act_t, bf_act, bf_act_t, bf_act, bf_act_t, bf_act,
            jax.ShapeDtypeStruct((1, 1), F32), jax.ShapeDtypeStruct((1, D_MODEL), F32),
            jax.ShapeDtypeStruct((1, 2 * D_MODEL), F32), jax.ShapeDtypeStruct((1, D_MODEL), F32),
        ],
        compiler_params=_cparams(("arbitrary",)),
    )(proj, proj, proj, sb_o, hg_o, x, target, b_gate, hg_gain, final_g, w_sb, w_hg, w_out)


def _grad_square(a_t, b, name):
    s_len = b.shape[0]
    tk = min(1024, s_len)

    def body(a_ref, b_ref, o_ref):
        @pl.when(pl.program_id(0) == 0)
        def _():
            o_ref[...] = jnp.zeros_like(o_ref)

        o_ref[...] += _dot(a_ref[...], b_ref[...])

    return pl.pallas_call(
        body,
        name=name,
        grid=(s_len // tk,),
        in_specs=[pl.BlockSpec((D_MODEL, tk), lambda k: (0, k)), pl.BlockSpec((tk, D_MODEL), lambda k: (k, 0))],
        out_specs=pl.BlockSpec((D_MODEL, D_MODEL), lambda k: (0, 0)),
        out_shape=jax.ShapeDtypeStruct((D_MODEL, D_MODEL), F32),
        compiler_params=_cparams(("arbitrary",)),
    )(a_t, b)


SEG_WIDTHS = (1024, 1024, 1024, 4096, 3072)
SEG_TILE = 1024
SEG_BOUNDS = (0, 1, 2, 3, 7, 10)


def _w_in_tile(k):
    return jnp.where(k < 4, k, jnp.where(k < 7, k + 3, k - 3))


def _grad_w_in(h_t, segs):
    m, s_len = h_t.shape
    tk = min(1024, s_len)
    tn = SEG_TILE
    nk = s_len // tk
    bounds = SEG_BOUNDS

    def body(a_ref, *refs):
        seg_refs, o_ref = refs[:-1], refs[-1]
        j = pl.program_id(0)

        @pl.when(pl.program_id(1) == 0)
        def _():
            o_ref[...] = jnp.zeros_like(o_ref)

        for i, ref in enumerate(seg_refs):
            @pl.when((j >= bounds[i]) & (j < bounds[i + 1]))
            def _(ref=ref):
                o_ref[...] += _dot(a_ref[...], ref[...])

    def seg_spec(lo, hi):
        def index(j, k):
            return (jnp.where(j < lo, 0, jnp.where(j >= hi, nk - 1, k)), jnp.clip(j - lo, 0, hi - lo - 1))
        return pl.BlockSpec((tk, tn), index)

    return pl.pallas_call(
        body,
        name="grad_w_in",
        grid=(IN_WIDTH // tn, nk),
        in_specs=[pl.BlockSpec((m, tk), lambda j, k: (0, k))]
        + [seg_spec(bounds[i], bounds[i + 1]) for i in range(len(SEG_WIDTHS))],
        out_specs=pl.BlockSpec((m, tn), lambda j, k: (0, _w_in_tile(j))),
        out_shape=jax.ShapeDtypeStruct((m, IN_WIDTH), F32),
        compiler_params=_cparams(("arbitrary", "arbitrary")),
    )(h_t, *segs)


EXCHANGE_IN_PIECES = 8
EXCHANGE_PIECES = EXCHANGE_IN_PIECES + 3


def _exchange_copies(sin_ref, ssq_ref, got_in, got_sq, send_sems, recv_sems):
    _, _, c, chips = _position()
    rows = HALF_IN // EXCHANGE_IN_PIECES
    copies = []
    for k, (px, py) in enumerate(chips):
        chip = 2 * px + py
        for p in range(EXCHANGE_PIECES):
            if p < EXCHANGE_IN_PIECES:
                src, dst = sin_ref.at[chip, pl.ds(p * rows, rows), :], got_in.at[k, pl.ds(p * rows, rows), :]
            else:
                src, dst = ssq_ref.at[p - EXCHANGE_IN_PIECES, chip], got_sq.at[k, p - EXCHANGE_IN_PIECES]
            copies.append(_remote(src, dst, send_sems.at[k, p], recv_sems.at[k, p], (px, py, c)))
    return copies


def _dx(segs, w_all, x, norm_g, dout, s_in, s_sq):
    s_len = x.shape[0]
    ts = min(1024, s_len)
    tk = SEG_TILE
    nk = IN_WIDTH // tk
    ns = s_len // ts
    bounds = SEG_BOUNDS
    n_seg = len(SEG_WIDTHS)

    def body(*refs):
        seg_refs = refs[:n_seg]
        w_ref, x_ref, g_ref, dout_ref, sin_ref, ssq_ref, gx_ref, dg_ref, got_in, got_sq, acc, send_sems, recv_sems = refs[n_seg:]
        s, k = pl.program_id(0), pl.program_id(1)

        @pl.when((s == 0) & (k == 0))
        def _():
            dg_ref[...] = jnp.zeros_like(dg_ref)
            for cp in _exchange_copies(sin_ref, ssq_ref, got_in, got_sq, send_sems, recv_sems):
                cp.start()

        @pl.when(k == 0)
        def _():
            acc[...] = jnp.zeros_like(acc)

        for i, ref in enumerate(seg_refs):
            @pl.when((k >= bounds[i]) & (k < bounds[i + 1]))
            def _(ref=ref):
                acc[...] += _dot_nt(ref[...], w_ref[...])

        @pl.when(k == nk - 1)
        def _():
            dh = acc[...]
            xv = x_ref[...]
            r = lax.rsqrt(jnp.mean(xv * xv, axis=-1, keepdims=True) + RMS_EPS)
            xn = xv * r
            dg_ref[...] += jnp.sum(dh * xn, axis=0, keepdims=True)
            dxn = dh * g_ref[...]
            gx_ref[...] = r * (dxn - xn * jnp.mean(dxn * xn, axis=-1, keepdims=True)) + dout_ref[...]

        @pl.when((s == ns - 1) & (k == nk - 1))
        def _():
            for cp in _exchange_copies(sin_ref, ssq_ref, got_in, got_sq, send_sems, recv_sems):
                cp.wait()

    def seg_spec(lo, hi):
        return pl.BlockSpec((ts, tk), lambda s, k: (s, jnp.clip(k - lo, 0, hi - lo - 1)))

    row_tile = pl.BlockSpec((ts, D_MODEL), lambda s, k: (s, 0))
    vec = pl.BlockSpec((1, D_MODEL), lambda s, k: (0, 0))
    return pl.pallas_call(
        body,
        name="dx",
        grid=(ns, nk),
        in_specs=[seg_spec(bounds[i], bounds[i + 1]) for i in range(n_seg)] + [
            pl.BlockSpec((D_MODEL, tk), lambda s, k: (0, _w_in_tile(k))),
            row_tile, vec, row_tile, ANY, ANY,
        ],
        out_specs=[row_tile, vec, ANY, ANY],
        out_shape=[jax.ShapeDtypeStruct((s_len, D_MODEL), F32), jax.ShapeDtypeStruct((1, D_MODEL), F32),
                   jax.ShapeDtypeStruct((3, HALF_IN, W_IN_SHARD), WIRE),
                   jax.ShapeDtypeStruct((3, 3, HALF_SQ, D_MODEL), WIRE)],
        scratch_shapes=[pltpu.VMEM((ts, D_MODEL), F32),
                        pltpu.SemaphoreType.DMA((3, EXCHANGE_PIECES)), pltpu.SemaphoreType.DMA((3, EXCHANGE_PIECES))],
        compiler_params=_cparams(("arbitrary", "arbitrary"), vmem=VMEM_LIMIT_DX),
    )(*segs, w_all, x, norm_g, dout, s_in, s_sq)


def _local_grads(x, target, proj, h_t, qkv, b_gate, lbl, hg_gain, final_g, w_sb, w_hg, w_out):
    sb_o, sb_o_fine = _sb_fwd(qkv)
    hg_o, states = _hg_fwd(proj, lbl)
    (dout, d_sbo, d_hgo, d_mid, a_sb, du_sb, a_hg, du_hg, y, doutb,
     loss, d_fg, d_bg, d_hgn) = _mid(proj, sb_o, hg_o, x, target, b_gate, hg_gain, final_g, w_sb, w_hg, w_out)
    g_w_sb = _grad_square(a_sb, du_sb, "grad_w_sb")
    g_w_hg = _grad_square(a_hg, du_hg, "grad_w_hg")
    g_w_out = _grad_square(y, doutb, "grad_w_out")
    d_q, d_k, d_v = _sb_bwd(qkv, sb_o_fine, d_sbo)
    d_hg, d_lb = _hg_bwd(proj, lbl, states, d_hgo)
    segs = (d_q, d_k, d_v, d_mid, d_hg)
    g_w_in = _grad_w_in(h_t, segs)
    return g_w_in, g_w_sb, g_w_hg, g_w_out, segs, dout, loss, d_bg, d_lb, d_hgn, d_fg


ANY = pl.BlockSpec(memory_space=pl.ANY)
WIRE = BF16
HALF_IN = D_MODEL // 2
HALF_SQ = ROW_SHARD // 2


def _position():
    x, y, c = lax.axis_index("x"), lax.axis_index("y"), lax.axis_index("c")
    chips = [(1 - x, y), (x, 1 - y), (1 - x, 1 - y)]
    return x, y, c, chips


def _remote(src, dst, send_sem, recv_sem, to):
    return pltpu.make_async_remote_copy(src_ref=src, dst_ref=dst, send_sem=send_sem, recv_sem=recv_sem,
                                        device_id=to, device_id_type=MESH)


PROJ_TILE = 1280
F32_FROM_TILE = 2
BF16_TO_TILE = 2
W_LOAD_PIECES = 8


def _gather_inproj(idx, h, w_in_b, w_sq_b):
    s_len = h.shape[0]
    ts = min(1024, s_len)
    ns = s_len // ts
    per = W_IN_SHARD // PROJ_TILE
    n_in = 4
    n_piece = n_in + 3
    rows = HALF_IN // n_in

    def chip_at(r, me):
        return me ^ jnp.where(r == 1, 2, jnp.where(r == 2, 1, jnp.where(r == 3, 3, 0)))

    def body(idx_ref, h_ref, win_ref, wsqb_ref, proj_ref, qkv_ref, wall_ref, wsq_ref, wbuf, send_sems, recv_sems, w_sems):
        r, t, s = pl.program_id(0), pl.program_id(1), pl.program_id(2)
        x, y, c, chips = _position()
        me = 2 * x + y
        sibling = (x, y, 1 - c)
        first = (t == 0) & (s == 0)

        def src_piece(p):
            if p < n_in:
                return win_ref.at[pl.ds(c * HALF_IN + p * rows, rows), :]
            return wsqb_ref.at[p - n_in, pl.ds(c * HALF_SQ, HALF_SQ), :]

        def piece(p, chip, core):
            if p < n_in:
                cols = pl.ds(pl.multiple_of(chip * W_IN_SHARD, W_IN_SHARD), W_IN_SHARD)
                return wall_ref.at[pl.ds(core * HALF_IN + p * rows, rows), cols]
            return wsq_ref.at[p - n_in, chip, pl.ds(core * HALF_SQ, HALF_SQ), :]

        def send(k, p):
            px, py = chips[k]
            return _remote(src_piece(p), piece(p, me, c), send_sems.at[k, p], recv_sems.at[k, p], (px, py, c))

        def forward(k, p, core):
            px, py = chips[k]
            got = piece(p, 2 * px + py, core)
            return _remote(got, got, send_sems.at[3 + k, p], recv_sems.at[3 + k, p], sibling)

        @pl.when((r == 0) & first)
        def _():
            for k in range(2):
                for p in range(n_piece):
                    send(k, p).start()

        for k in range(3):
            @pl.when((r == k + 1) & first)
            def _(k=k):
                px, py = chips[k]
                for p in range(n_piece):
                    got = piece(p, 2 * px + py, c)
                    _remote(got, got, send_sems.at[k, p], recv_sems.at[k, p], (px, py, c)).wait_recv()
                    forward(k, p, c).start()
                if k == 0:
                    for p in range(n_piece):
                        send(2, p).start()
                for p in range(n_piece):
                    forward(k, p, 1 - c).wait_recv()

        def tile_loads(slot, own):
            col = slot * PROJ_TILE
            if not own:
                col = pl.multiple_of(chip_at(r, me) * W_IN_SHARD + col, PROJ_TILE)
            src = win_ref if own else wall_ref
            part = D_MODEL // W_LOAD_PIECES
            return [pltpu.make_async_copy(src.at[pl.ds(q * part, part), pl.ds(col, PROJ_TILE)],
                                          wbuf.at[slot, pl.ds(q * part, part), :], w_sems.at[slot, q])
                    for q in range(W_LOAD_PIECES)]

        for own in (True, False):
            @pl.when(first & ((r == 0) if own else (r > 0)))
            def _(own=own):
                for slot in range(per):
                    for cp in tile_loads(slot, own):
                        cp.start()
                for cp in tile_loads(0, own):
                    cp.wait()

            @pl.when((t > 0) & (s == 0) & ((r == 0) if own else (r > 0)))
            def _(own=own):
                for cp in tile_loads(1, own):
                    cp.wait()

        tile_now = per * chip_at(r, me) + t
        want_f32, want_bf16 = tile_now >= F32_FROM_TILE, tile_now <= BF16_TO_TILE

        @pl.when(want_f32 & jnp.logical_not(want_bf16))
        def _():
            proj_ref[...] = _dot(h_ref[...], wbuf[t])

        @pl.when(want_bf16 & jnp.logical_not(want_f32))
        def _():
            qkv_ref[...] = _dot(h_ref[...], wbuf[t]).astype(BF16)

        @pl.when(want_f32 & want_bf16)
        def _():
            p = _dot(h_ref[...], wbuf[t])
            proj_ref[...] = p
            qkv_ref[...] = p.astype(BF16)

        @pl.when((r == 3) & (t == per - 1) & (s == ns - 1))
        def _():
            for k in range(3):
                for p in range(n_piece):
                    send(k, p).wait_send()
                    forward(k, p, c).wait_send()

    def out_index(wanted):
        order = [0, 2, 1, 3]
        table = []
        for chip in range(N_CHIPS):
            tiles = [per * (chip ^ order[q // per]) + q % per for q in range(N_CHIPS * per)]
            row = []
            for q, tile in enumerate(tiles):
                if wanted(tile):
                    row.append((tile, None))
                    continue
                before = [u for u in tiles[:q] if wanted(u)]
                after = [u for u in tiles[q:] if wanted(u)]
                row.append((before[-1], ns - 1) if before else (after[0], 0))
            table.append(row)

        def index(r, t, s, idx):
            q = r * per + t
            col, fixed_s = jnp.int32(0), jnp.int32(-1)
            for chip in range(N_CHIPS):
                for pos, (tile, hold) in enumerate(table[chip]):
                    here = (idx[0] == chip) & (q == pos)
                    col = jnp.where(here, tile, col)
                    fixed_s = jnp.where(here, -1 if hold is None else hold, fixed_s)
            return jnp.where(fixed_s < 0, s, fixed_s), col

        return index

    grid_spec = pltpu.PrefetchScalarGridSpec(
        num_scalar_prefetch=1,
        grid=(N_CHIPS, per, ns),
        in_specs=[pl.BlockSpec((ts, D_MODEL), lambda r, t, s, idx: (s, 0)), ANY, ANY],
        out_specs=[pl.BlockSpec((ts, PROJ_TILE), out_index(lambda tile: tile >= F32_FROM_TILE)),
                   pl.BlockSpec((ts, PROJ_TILE), out_index(lambda tile: tile <= BF16_TO_TILE)),
                   ANY, ANY],
        scratch_shapes=[pltpu.VMEM((per, D_MODEL, PROJ_TILE), BF16),
                        pltpu.SemaphoreType.DMA((6, n_piece)), pltpu.SemaphoreType.DMA((6, n_piece)),
                        pltpu.SemaphoreType.DMA((per, W_LOAD_PIECES))],
    )
    return pl.pallas_call(
        body,
        name="gather_inproj",
        grid_spec=grid_spec,
        out_shape=[jax.ShapeDtypeStruct((s_len, IN_WIDTH), F32),
                   jax.ShapeDtypeStruct((s_len, IN_WIDTH), BF16),
                   jax.ShapeDtypeStruct((D_MODEL, IN_WIDTH), BF16),
                   jax.ShapeDtypeStruct((3, N_CHIPS, ROW_SHARD, D_MODEL), BF16)],
        compiler_params=_cparams(("arbitrary", "arbitrary", "arbitrary")),
    )(idx, h, w_in_b, w_sq_b)


def _place_own(idx, w_in_b, w_sq_b, w_all, wsq):
    n = 4
    r_in, r_sq = D_MODEL // n, ROW_SHARD // n

    def body(idx_ref, win_ref, wsq_ref, w_all_in, wsq_in, w_all_out, wsq_out):
        w_all_out[...] = win_ref[...]
        wsq_out[:, 0] = wsq_ref[...]

    grid_spec = pltpu.PrefetchScalarGridSpec(
        num_scalar_prefetch=1,
        grid=(n,),
        in_specs=[pl.BlockSpec((r_in, W_IN_SHARD), lambda r, idx: (r, 0)),
                  pl.BlockSpec((3, r_sq, D_MODEL), lambda r, idx: (0, r, 0)), ANY, ANY],
        out_specs=[pl.BlockSpec((r_in, W_IN_SHARD), lambda r, idx: (r, idx[0])),
                   pl.BlockSpec((3, 1, r_sq, D_MODEL), lambda r, idx: (0, idx[0], r, 0))],
    )
    return pl.pallas_call(
        body,
        name="place_own",
        grid_spec=grid_spec,
        out_shape=[jax.ShapeDtypeStruct(w_all.shape, BF16), jax.ShapeDtypeStruct(wsq.shape, BF16)],
        input_output_aliases={3: 0, 4: 1},
        compiler_params=_cparams(("arbitrary",)),
    )(idx, w_in_b, w_sq_b, w_all, wsq)


def _swap_halves(g_in, g_sq):
    n_in = 16
    n_piece = n_in + 3 * N_CHIPS
    rows = HALF_IN // n_in

    def body(gin_ref, gsq_ref, got_in, got_sq, send_sems, recv_sems):
        x, y, c, _ = _position()
        sibling = (x, y, 1 - c)

        def src_piece(p):
            if p < n_in:
                return gin_ref.at[pl.ds((1 - c) * HALF_IN + p * rows, rows), :]
            a, chip = divmod(p - n_in, N_CHIPS)
            return gsq_ref.at[a, chip, pl.ds((1 - c) * HALF_SQ, HALF_SQ), :]

        def dst_piece(p):
            if p < n_in:
                return got_in.at[pl.ds(p * rows, rows), :]
            a, chip = divmod(p - n_in, N_CHIPS)
            return got_sq.at[a, chip]

        out = [_remote(src_piece(p), dst_piece(p), send_sems.at[p], recv_sems.at[p], sibling) for p in range(n_piece)]
        for cp in out:
            cp.start()
        for cp in out:
            cp.wait()

    return pl.pallas_call(
        body,
        name="swap_halves",
        in_specs=[ANY, ANY],
        out_specs=[ANY, ANY],
        out_shape=[jax.ShapeDtypeStruct((HALF_IN, IN_WIDTH), F32),
                   jax.ShapeDtypeStruct((3, N_CHIPS, HALF_SQ, D_MODEL), F32)],
        scratch_shapes=[pltpu.SemaphoreType.DMA((n_piece,))] * 2,
    )(g_in, g_sq)


def _join_halves(r_in, r_sq):
    n_in = 16
    n_piece = n_in + 3
    rows = HALF_IN // n_in

    def body(in_alias, sq_alias, full_in, full_sq, send_sems, recv_sems):
        del in_alias, sq_alias
        x, y, c, _ = _position()
        sibling = (x, y, 1 - c)

        def piece(p, core):
            if p < n_in:
                return full_in.at[pl.ds(core * HALF_IN + p * rows, rows), :]
            return full_sq.at[p - n_in, pl.ds(core * HALF_SQ, HALF_SQ), :]

        out = [_remote(piece(p, c), piece(p, c), send_sems.at[p], recv_sems.at[p], sibling) for p in range(n_piece)]
        for cp in out:
            cp.start()
        for p in range(n_piece):
            _remote(piece(p, 1 - c), piece(p, 1 - c), send_sems.at[p], recv_sems.at[p], sibling).wait_recv()
        for cp in out:
            cp.wait_send()

    return pl.pallas_call(
        body,
        name="join_halves",
        in_specs=[ANY, ANY],
        out_specs=[ANY, ANY],
        out_shape=[jax.ShapeDtypeStruct((D_MODEL, W_IN_SHARD), F32),
                   jax.ShapeDtypeStruct((3, ROW_SHARD, D_MODEL), F32)],
        input_output_aliases={0: 0, 1: 1},
        scratch_shapes=[pltpu.SemaphoreType.DMA((n_piece,)), pltpu.SemaphoreType.DMA((n_piece,))],
    )(r_in, r_sq)


SMALL_ROWS = 56
N_DEV = 8


def _sum_small(part):
    def body(part_ref, out_ref, slots, send_sems, recv_sems):
        x, y, c, _ = _position()
        me = 4 * x + 2 * y + c
        slots[me] = part_ref[...]
        out = []
        for r in range(1, N_DEV):
            rx, ry, rc = (r >> 2) & 1, (r >> 1) & 1, r & 1
            to = (1 - x if rx else x, 1 - y if ry else y, 1 - c if rc else c)
            out.append(_remote(part_ref, slots.at[me], send_sems.at[r - 1], recv_sems.at[r - 1], to))
        for cp in out:
            cp.start()
        for r in range(1, N_DEV):
            _remote(part_ref, slots.at[me ^ r], send_sems.at[r - 1], recv_sems.at[r - 1], (x, y, c)).wait_recv()
        for cp in out:
            cp.wait_send()
        total = slots[0]
        for d in range(1, N_DEV):
            total = total + slots[d]
        out_ref[...] = total

    vmem = pl.BlockSpec(memory_space=pltpu.VMEM)
    return pl.pallas_call(
        body,
        name="sum_small",
        in_specs=[vmem],
        out_specs=vmem,
        out_shape=jax.ShapeDtypeStruct((SMALL_ROWS, HEAD_DIM), F32),
        scratch_shapes=[pltpu.VMEM((N_DEV, SMALL_ROWS, HEAD_DIM), F32),
                        pltpu.SemaphoreType.DMA((N_DEV - 1,)), pltpu.SemaphoreType.DMA((N_DEV - 1,))],
    )(part)


def _prefetch_call(body, name, idx, grid, in_specs, out_specs, out_shape, args):
    grid_spec = pltpu.PrefetchScalarGridSpec(num_scalar_prefetch=1, grid=grid, in_specs=in_specs, out_specs=out_specs)
    return pl.pallas_call(body, name=name, grid_spec=grid_spec, out_shape=out_shape,
                          compiler_params=_cparams(("arbitrary",) * len(grid)))(idx, *args)


def _sum_a_in(idx, g_in, got_in):
    tr = 128
    nr = HALF_IN // tr

    def body(idx_ref, a_ref, b_ref, o_ref):
        o_ref[0] = (a_ref[...] + b_ref[...]).astype(WIRE)

    return _prefetch_call(
        body, "sum_a_in", idx, (N_CHIPS, nr),
        [pl.BlockSpec((tr, W_IN_SHARD), lambda j, r, idx: (idx[1] * nr + r, j)),
         pl.BlockSpec((tr, W_IN_SHARD), lambda j, r, idx: (r, j))],
        pl.BlockSpec((1, tr, W_IN_SHARD), lambda j, r, idx: (j, r, 0)),
        jax.ShapeDtypeStruct((N_CHIPS, HALF_IN, W_IN_SHARD), WIRE), (g_in, got_in))


def _sum_a_sq(idx, g_sq, got_sq):
    blk = (1, 1, HALF_SQ, D_MODEL)

    def body(idx_ref, a_ref, b_ref, o_ref):
        o_ref[...] = (a_ref[...] + b_ref[...]).astype(WIRE)

    return _prefetch_call(
        body, "sum_a_sq", idx, (3, N_CHIPS),
        [pl.BlockSpec(blk, lambda a, j, idx: (a, j, idx[1], 0)), pl.BlockSpec(blk, lambda a, j, idx: (a, j, 0, 0))],
        pl.BlockSpec(blk, lambda a, j, idx: (a, j, 0, 0)),
        jax.ShapeDtypeStruct((3, N_CHIPS, HALF_SQ, D_MODEL), WIRE), (g_sq, got_sq))


def _sum_b_in(idx, s_in, got_in):
    tr = 128
    nr = HALF_IN // tr

    def body(idx_ref, a_ref, b_ref, o_ref):
        o_ref[...] = ((a_ref[0].astype(F32) + b_ref[0].astype(F32)) + b_ref[1].astype(F32)) + b_ref[2].astype(F32)

    return _prefetch_call(
        body, "sum_b_in", idx, (nr,),
        [pl.BlockSpec((1, tr, W_IN_SHARD), lambda r, idx: (idx[0], r, 0)),
         pl.BlockSpec((3, tr, W_IN_SHARD), lambda r, idx: (0, r, 0))],
        pl.BlockSpec((tr, W_IN_SHARD), lambda r, idx: (idx[1] * nr + r, 0)),
        jax.ShapeDtypeStruct((D_MODEL, W_IN_SHARD), F32), (s_in, got_in))


def _sum_b_sq(idx, s_sq, got_sq):
    def body(idx_ref, a_ref, b_ref, o_ref):
        o_ref[0] = ((a_ref[0, 0].astype(F32) + b_ref[0, 0].astype(F32)) + b_ref[1, 0].astype(F32)) + b_ref[2, 0].astype(F32)

    return _prefetch_call(
        body, "sum_b_sq", idx, (3,),
        [pl.BlockSpec((1, 1, HALF_SQ, D_MODEL), lambda a, idx: (a, idx[0], 0, 0)),
         pl.BlockSpec((3, 1, HALF_SQ, D_MODEL), lambda a, idx: (0, a, 0, 0))],
        pl.BlockSpec((1, HALF_SQ, D_MODEL), lambda a, idx: (a, idx[1], 0)),
        jax.ShapeDtypeStruct((3, ROW_SHARD, D_MODEL), F32), (s_sq, got_sq))


def _adamw_math(w, g, m, v):
    m = ADAM_B1 * m + (1.0 - ADAM_B1) * g
    v = ADAM_B2 * v + (1.0 - ADAM_B2) * (g * g)
    m_hat = m / (1.0 - ADAM_B1 ** ADAM_STEP)
    v_hat = v / (1.0 - ADAM_B2 ** ADAM_STEP)
    delta = -ADAM_LR * (m_hat / (jnp.sqrt(v_hat) + ADAM_EPS) + ADAM_WD * w)
    return delta, m, v


def _adamw(w, g, m, v, name):
    rows, cols = w.shape
    tr = min(128, rows)

    def body(w_ref, g_ref, m_ref, v_ref, d_ref, nm_ref, nv_ref):
        d_ref[...], nm_ref[...], nv_ref[...] = _adamw_math(w_ref[...], g_ref[...], m_ref[...], v_ref[...])

    spec = pl.BlockSpec((tr, cols), lambda r: (r, 0))
    return pl.pallas_call(
        body,
        name=name,
        grid=(rows // tr,),
        in_specs=[spec] * 4,
        out_specs=[spec] * 3,
        out_shape=[jax.ShapeDtypeStruct((rows, cols), F32)] * 3,
        compiler_params=_cparams(("arbitrary",)),
    )(w, g, m, v)


def _adamw_small(sums, w, m, v):
    def body(s_ref, w_ref, m_ref, v_ref, loss_ref, g_ref, d_ref, nm_ref, nv_ref):
        s = s_ref[...]
        w = w_ref[...]
        loss_ref[...] = s[0:1, 0:1]
        l0, l1 = w[24:32], w[32:40]
        mx = jnp.maximum(l0, l1)
        e0, e1 = jnp.exp(l0 - mx), jnp.exp(l1 - mx)
        p0, p1 = e0 / (e0 + e1), e1 / (e0 + e1)
        d_lb = s[32:40]
        g = jnp.concatenate([s[8:16], s[16:32], d_lb * p0 * (1.0 - p0), -d_lb * p0 * p1, s[40:48], s[48:56]], axis=0)
        g_ref[...] = g
        d_ref[...], nm_ref[...], nv_ref[...] = _adamw_math(w, g, m_ref[...], v_ref[...])

    packed = jax.ShapeDtypeStruct((SMALL_ROWS, HEAD_DIM), F32)
    return pl.pallas_call(
        body,
        name="adamw_small",
        out_shape=[jax.ShapeDtypeStruct((1, 1), F32), packed, packed, packed, packed],
    )(sums, w, m, v)


def _pack_small(ng, bg, lbl, hgn, fg):
    return jnp.concatenate([a.reshape(-1, HEAD_DIM) for a in (ng, bg, lbl, hgn, fg)], axis=0)


def _unpack_small(p):
    return (p[0:8].reshape(1, D_MODEL), p[8:24].reshape(1, 2 * D_MODEL), p[24:40].reshape(2, HEADS, HEAD_DIM),
            p[40:48].reshape(1, HEADS, HEAD_DIM), p[48:56].reshape(D_MODEL))


def kernel(x, norm_g, w_in, b_gate, lb_logits, hg_norm_g, w_sb_proj, w_hg_proj, w_out, final_norm_g, loss_target, m_norm_g, m_w_in, m_b_gate, m_lb_logits, m_hg_norm_g, m_w_sb_proj, m_w_hg_proj, m_w_out, m_final_norm_g, v_norm_g, v_w_in, v_b_gate, v_lb_logits, v_hg_norm_g, v_w_sb_proj, v_w_hg_proj, v_w_out, v_final_norm_g):
    s_len = x.shape[1]
    w_sq = jnp.stack([w_sb_proj[0], w_hg_proj[0], w_out[0]])
    idx = jnp.stack([2 * lax.axis_index("x") + lax.axis_index("y"), lax.axis_index("c")]).astype(jnp.int32)
    w_in_b, w_sq_b = w_in[0].astype(BF16), w_sq.astype(BF16)
    h, h_t = _prenorm(x[0], norm_g)
    proj, qkv, w_all, wsq = _gather_inproj(idx, h, w_in_b, w_sq_b)
    w_all, wsq = _place_own(idx, w_in_b, w_sq_b, w_all, wsq)
    wsq = wsq.reshape(3, D_MODEL, D_MODEL)

    (g_in, g_sb, g_hg, g_out, segs, dout, loss, d_bg, d_lb, d_hgn, d_fg) = _local_grads(
        x[0], loss_target[0], proj, h_t, qkv, b_gate, lb_logits.reshape(2, D_MODEL), hg_norm_g.reshape(1, D_MODEL),
        final_norm_g.reshape(1, D_MODEL), wsq[0], wsq[1], wsq[2])

    g_sq = jnp.stack([g_sb, g_hg, g_out]).reshape(3, N_CHIPS, ROW_SHARD, D_MODEL)
    got_in, got_sq = _swap_halves(g_in, g_sq)
    s_in, s_sq = _sum_a_in(idx, g_in, got_in), _sum_a_sq(idx, g_sq, got_sq)
    grad_x, d_ng, got_in, got_sq = _dx(segs, w_all, x[0], norm_g, dout, s_in, s_sq)
    grad_in, grad_sq = _join_halves(_sum_b_in(idx, s_in, got_in), _sum_b_sq(idx, s_sq, got_sq))

    d_in, nm_in, nv_in = _adamw(w_in[0], grad_in, m_w_in[0], v_w_in[0], "adamw_in")
    flat = lambda a, b, c: jnp.concatenate([a[0], b[0], c[0]], axis=0)
    d_sq, nm_sq, nv_sq = _adamw(flat(w_sb_proj, w_hg_proj, w_out), grad_sq.reshape(3 * ROW_SHARD, D_MODEL),
                                flat(m_w_sb_proj, m_w_hg_proj, m_w_out), flat(v_w_sb_proj, v_w_hg_proj, v_w_out),
                                "adamw_sq")

    pad = jnp.zeros((8, HEAD_DIM), F32).at[0, 0].set(loss[0, 0])
    part = jnp.concatenate([pad] + [a.reshape(-1, HEAD_DIM) for a in (d_ng, d_bg, d_lb, d_hgn, d_fg)], axis=0)
    sums = _sum_small(part)
    loss_out, g_sm, d_sm, nm_sm, nv_sm = _adamw_small(
        sums, _pack_small(norm_g, b_gate, lb_logits, hg_norm_g, final_norm_g),
        _pack_small(m_norm_g, m_b_gate, m_lb_logits, m_hg_norm_g, m_final_norm_g),
        _pack_small(v_norm_g, v_b_gate, v_lb_logits, v_hg_norm_g, v_final_norm_g))

    def big(t_in, t_sq):
        sq = t_sq.reshape(3, 1, ROW_SHARD, D_MODEL)
        return t_in[None], sq[0], sq[1], sq[2]

    def order(small, in_, sb, hg, out):
        ng, bg, lbl, hgn, fg = small
        return [ng, in_, bg, lbl, hgn, sb, hg, out, fg]

    outs = [loss_out[0, 0], grad_x[None]]
    for small, (t_in, t_sq) in ((g_sm, (grad_in, grad_sq)), (d_sm, (d_in, d_sq)), (nm_sm, (nm_in, nm_sq)), (nv_sm, (nv_in, nv_sq))):
        outs += order(_unpack_small(small), *big(t_in, t_sq))
    return tuple(outs)
```

```python
import functools

import jax
import jax.numpy as jnp
from jax import lax
from jax.experimental import pallas as pl
from jax.experimental.pallas import tpu as pltpu

F32 = jnp.float32
BF16 = jnp.bfloat16

D_MODEL = 1024
HEADS = 8
HEAD_DIM = 128
IN_WIDTH = 10240
N_CHIPS = 4
W_IN_SHARD = IN_WIDTH // N_CHIPS
ROW_SHARD = D_MODEL // N_CHIPS
RMS_EPS = 1e-6

OFF_SB_Q, OFF_SB_K, OFF_SB_V, OFF_SB_Z = 0, 1024, 2048, 3072
OFF_HG_Q, OFF_HG_F, OFF_HG_I, OFF_HG_Z, OFF_GATE = 4096, 5120, 6144, 7168, 8192

SB_BLOCK = 256
SB_FWD_HEADS = 4
SB_FWD_GROUPS = 2
SB_BWD_HEADS = 2
SB_BWD_GROUPS = 2
SB_DEAD = -110.0
SB_GONE = -1e30
HG_CHUNK = 32
HG_PAIR = 2 * HG_CHUNK
HG_STEP = 256
HG_MID = HG_CHUNK // 2 - 1

ADAM_LR, ADAM_B1, ADAM_B2, ADAM_EPS, ADAM_WD, ADAM_STEP = 0.001, 0.9, 0.999, 1e-08, 0.01, 10

VMEM_LIMIT = 56 * 1024 * 1024
VMEM_LIMIT_DX = 60 * 1024 * 1024

MESH = pl.DeviceIdType.MESH


def _cparams(sem, vmem=VMEM_LIMIT):
    return pltpu.CompilerParams(dimension_semantics=sem, vmem_limit_bytes=vmem)


def _dot(a, b):
    return jnp.dot(a, b, preferred_element_type=F32)


def _dot_nt(a, b):
    return lax.dot_general(a, b, (((1,), (1,)), ((), ())), preferred_element_type=F32)


def _dot_tn(a, b):
    return lax.dot_general(a, b, (((0,), (0,)), ((), ())), preferred_element_type=F32)


def _split_dot(x, tri):
    hi = x.astype(BF16)
    lo = (x - hi.astype(F32)).astype(BF16)
    both = _dot(jnp.concatenate([hi, lo], axis=0), tri)
    return both[: x.shape[0]] + both[x.shape[0] :]


def _split_dot_left(tri, x):
    hi = x.astype(BF16)
    lo = (x - hi.astype(F32)).astype(BF16)
    return _dot(tri, hi) + _dot(tri, lo)


def _sigmoid(x):
    return 1.0 / (1.0 + jnp.exp(-x))


def _prenorm(x, norm_g):
    s_len = x.shape[0]
    ts = min(1024, s_len)

    def body(x_ref, g_ref, h_ref, ht_ref):
        xv = x_ref[...]
        r = lax.rsqrt(jnp.mean(xv * xv, axis=-1, keepdims=True) + RMS_EPS)
        hv = (xv * r) * g_ref[...]
        h_ref[...] = hv.astype(BF16)
        ht_ref[...] = hv.T.astype(BF16)

    return pl.pallas_call(
        body,
        name="prenorm",
        grid=(s_len // ts,),
        in_specs=[pl.BlockSpec((ts, D_MODEL), lambda s: (s, 0)), pl.BlockSpec((1, D_MODEL), lambda s: (0, 0))],
        out_specs=[pl.BlockSpec((ts, D_MODEL), lambda s: (s, 0)), pl.BlockSpec((D_MODEL, ts), lambda s: (0, s))],
        out_shape=[jax.ShapeDtypeStruct((s_len, D_MODEL), BF16), jax.ShapeDtypeStruct((D_MODEL, s_len), BF16)],
        compiler_params=_cparams(("arbitrary",)),
    )(x, norm_g)


def _sb_scores(qb, kb, causal, tri_excl, diag):
    z = _dot_nt(qb, kb) * HEAD_DIM ** -0.5
    ls_pos = jnp.minimum(z, 0.0) - jnp.log1p(jnp.exp(-jnp.abs(z)))
    log_not = ls_pos - z
    log_not_m = jnp.where(causal, log_not, 0.0) if diag else log_not
    return ls_pos, log_not, log_not_m, _split_dot(log_not_m, tri_excl)


def _sb_weights(ls_pos, suffix, carry, causal, diag):
    surv = suffix + carry
    w = jnp.exp(ls_pos + surv)
    return surv, (jnp.where(causal, w, 0.0) if diag else w)


def _sb_specs(s_len, blk, heads):
    width = heads * HEAD_DIM

    def blk_spec(off):
        return pl.BlockSpec((blk, width), lambda h, i: (i, off // width + h))

    def head_spec(off, buffers=2):
        return pl.BlockSpec((s_len, width), lambda h, i: (0, off // width + h), pipeline_mode=pl.Buffered(buffers))

    return blk_spec, head_spec


def _head_cols(p):
    return slice(p * HEAD_DIM, (p + 1) * HEAD_DIM)


def _sb_masks(blk, rows):
    row = lax.broadcasted_iota(jnp.int32, (rows, blk), 0)
    col = lax.broadcasted_iota(jnp.int32, (rows, blk), 1)
    causal = [row + a * rows > col for a in range(blk // rows)]
    row = lax.broadcasted_iota(jnp.int32, (blk, blk), 0)
    col = lax.broadcasted_iota(jnp.int32, (blk, blk), 1)
    tri_excl = (row > col).astype(BF16)
    tri_incl = (row >= col).astype(BF16)
    return causal, tri_excl, tri_incl


def _sb_alive(st, n_chain):
    alive = functools.reduce(jnp.maximum, [st[1 + 3 * c] for c in range(n_chain)])
    return jnp.max(alive) > SB_DEAD


def _sb_fwd(qkv):
    s_len = qkv.shape[0]
    kb_rows = min(SB_BLOCK, s_len)
    groups = min(SB_FWD_GROUPS, s_len // kb_rows)
    blk = groups * kb_rows
    nq = s_len // blk
    rows = kb_rows
    chains = [(p, a) for p in range(SB_FWD_HEADS) for a in range(groups)]

    def body(q_ref, k_ref, v_ref, o_ref, of_ref):
        i = pl.program_id(1)
        masks, tri_excl, _ = _sb_masks(kb_rows, rows)
        causal = masks[0]

        def tiles(steps, st):
            pre = []
            for n, diag in steps:
                for p, a in chains:
                    j = groups * i + a - n
                    start = pl.multiple_of(jnp.maximum(j, 0) * kb_rows, kb_rows)
                    kb = k_ref[pl.ds(start, kb_rows), _head_cols(p)]
                    qb = q_ref[a * rows : (a + 1) * rows, _head_cols(p)]
                    pre.append(_sb_scores(qb, kb, causal, tri_excl, diag) + (v_ref[pl.ds(start, kb_rows), _head_cols(p)], j))
            for t, (_, diag) in enumerate(steps):
                new = []
                for c, (p, a) in enumerate(chains):
                    carry, acc, acc_lo = st[3 * c : 3 * c + 3]
                    ls_pos, _, log_not_m, suffix, vb, j = pre[t * len(chains) + c]
                    if not diag:
                        carry = jnp.where(j >= 0, carry, SB_GONE)
                    surv, w = _sb_weights(ls_pos, suffix, carry, causal, diag)
                    wb = w.astype(BF16)
                    w_lo = (w - wb.astype(F32)).astype(BF16)
                    both = _dot(jnp.concatenate([wb, w_lo], axis=0), vb)
                    new += [surv[:, 0:1] + log_not_m[:, 0:1], acc + both[:rows], acc_lo + both[rows:]]
                st = tuple(new)
            return st

        zero = jnp.zeros((rows, HEAD_DIM), F32)
        st = tiles([(0, True), (1, False)], (jnp.zeros((rows, 1), F32), zero, zero) * len(chains))

        def more(st):
            return (st[0] <= groups * i + groups - 1) & _sb_alive(st, len(chains))

        def step(st):
            return (st[0] + 1,) + tiles([(st[0], False)], st[1:])

        st = lax.while_loop(more, step, (2,) + st)[1:]
        for c, (p, a) in enumerate(chains):
            o_ref[a * rows : (a + 1) * rows, _head_cols(p)] = st[3 * c + 1]
            of_ref[a * rows : (a + 1) * rows, _head_cols(p)] = st[3 * c + 1] + st[3 * c + 2]

    blk_spec, head_spec = _sb_specs(s_len, blk, SB_FWD_HEADS)
    return pl.pallas_call(
        body,
        name="sb_fwd",
        grid=(HEADS // SB_FWD_HEADS, nq),
        in_specs=[blk_spec(OFF_SB_Q), head_spec(OFF_SB_K), head_spec(OFF_SB_V)],
        out_specs=[blk_spec(0), blk_spec(0)],
        out_shape=[jax.ShapeDtypeStruct((s_len, D_MODEL), F32)] * 2,
        compiler_params=_cparams(("arbitrary", "arbitrary")),
    )(qkv, qkv, qkv)


def _sb_bwd(qkv, o_fine, d_o):
    s_len = qkv.shape[0]
    kb_rows = min(SB_BLOCK, s_len)
    groups = min(SB_BWD_GROUPS, s_len // kb_rows)
    blk = groups * kb_rows
    nq = s_len // blk
    scale = HEAD_DIM ** -0.5
    rows = kb_rows
    chains = [(p, a) for p in range(SB_BWD_HEADS) for a in range(groups)]

    def body(q_ref, k_ref, v_ref, of_ref, do_ref, dq_ref, dk_ref, dv_ref, dk_acc, dv_acc):
        i = pl.program_id(1)

        @pl.when(i == 0)
        def _():
            dk_acc[...] = jnp.zeros_like(dk_acc)
            dv_acc[...] = jnp.zeros_like(dv_acc)

        dob = do_ref[...].astype(BF16)
        prod = dob.astype(F32) * of_ref[...]
        masks, tri_excl, tri_incl = _sb_masks(kb_rows, rows)
        causal = masks[0]

        def group(x, p, a):
            return x[a * rows : (a + 1) * rows, _head_cols(p)]

        totals = [jnp.sum(group(prod, p, a), axis=-1, keepdims=True) for p, a in chains]

        def tiles(steps, st):
            pre = []
            for n, diag in steps:
                for p, a in chains:
                    j = groups * i + a - n
                    start = pl.multiple_of(jnp.maximum(j, 0) * kb_rows, kb_rows)
                    kb = k_ref[pl.ds(start, kb_rows), _head_cols(p)]
                    vb = v_ref[pl.ds(start, kb_rows), _head_cols(p)]
                    qb, dob_c = group(q_ref, p, a), group(dob, p, a)
                    pre.append(_sb_scores(qb, kb, causal, tri_excl, diag) + (_dot_nt(dob_c, vb), qb, kb, dob_c, j, start))
            for t, (_, diag) in enumerate(steps):
                mids = []
                for c, (p, a) in enumerate(chains):
                    ls_pos, _, _, suffix, d_w = pre[t * len(chains) + c][:5]
                    c_not = st[3 * c]
                    if not diag:
                        c_not = jnp.where(pre[t * len(chains) + c][8] >= 0, c_not, SB_GONE)
                    surv, w = _sb_weights(ls_pos, suffix, c_not, causal, diag)
                    dlw = d_w * w
                    mids.append((surv, w, dlw, _split_dot(dlw, tri_incl)))
                new = []
                for c, (p, a) in enumerate(chains):
                    c_dlw, dq = st[3 * c + 1 : 3 * c + 3]
                    ls_pos, log_not, log_not_m, _, _, qb, kb, dob_c, j, start = pre[t * len(chains) + c]
                    surv, w, dlw, suffix = mids[c]
                    d_not = totals[c] - c_dlw - suffix
                    dz = ((dlw + d_not) * jnp.exp(log_not) - d_not) * scale
                    dz = jnp.where(causal, dz, 0.0) if diag else jnp.where(j >= 0, dz, 0.0)
                    dzb = dz.astype(BF16)
                    dk_acc[pl.ds(start, kb_rows), _head_cols(p)] += _dot_tn(dzb, qb)
                    dv_acc[pl.ds(start, kb_rows), _head_cols(p)] += _dot_tn(w.astype(BF16), dob_c)
                    new += [surv[:, 0:1] + log_not_m[:, 0:1], c_dlw + suffix[:, 0:1], dq + _dot(dzb, kb)]
                st = tuple(new)
            return st

        zcol = jnp.zeros((rows, 1), F32)
        st = tiles([(0, True), (1, False)], (zcol, zcol, jnp.zeros((rows, HEAD_DIM), F32)) * len(chains))

        def more(st):
            return (st[0] <= groups * i + groups - 1) & _sb_alive(st, len(chains))

        def step(st):
            return (st[0] + 1,) + tiles([(st[0], False)], st[1:])

        st = lax.while_loop(more, step, (2,) + st)[1:]
        for c, (p, a) in enumerate(chains):
            dq_ref[a * rows : (a + 1) * rows, _head_cols(p)] = st[3 * c + 2].astype(BF16)

        @pl.when(i == nq - 1)
        def _():
            dk_ref[...] = dk_acc[...].astype(BF16)
            dv_ref[...] = dv_acc[...].astype(BF16)

    blk_spec, head_spec = _sb_specs(s_len, blk, SB_BWD_HEADS)
    width = SB_BWD_HEADS * HEAD_DIM
    return pl.pallas_call(
        body,
        name="sb_bwd",
        grid=(HEADS // SB_BWD_HEADS, nq),
        in_specs=[blk_spec(OFF_SB_Q), head_spec(OFF_SB_K, 1), head_spec(OFF_SB_V, 1), blk_spec(0), blk_spec(0)],
        out_specs=[blk_spec(0), head_spec(0), head_spec(0)],
        out_shape=[jax.ShapeDtypeStruct((s_len, D_MODEL), BF16)] * 3,
        scratch_shapes=[pltpu.VMEM((s_len, width), F32), pltpu.VMEM((s_len, width), F32)],
        compiler_params=_cparams(("arbitrary", "arbitrary"), vmem=VMEM_LIMIT_DX),
    )(qkv, qkv, qkv, o_fine, d_o)


def _hg_lower_bound(lbl_ref):
    l0 = lbl_ref[0:1, :]
    l1 = lbl_ref[1:2, :]
    mx = jnp.maximum(l0, l1)
    e0 = jnp.exp(l0 - mx)
    e1 = jnp.exp(l1 - mx)
    return e0 / (e0 + e1)


def _hg_gates(hq, hf, lb):
    sig_f = _sigmoid(hf)
    f = lb + (1.0 - lb) * sig_f
    g = jnp.log(f)
    kk = 1.0 - f
    sig_q = _sigmoid(hq)
    qq = hq * sig_q
    return qq, kk, g, f, sig_f, sig_q


def _period_bcast(x, r, rows, period):
    w = x.shape[-1]
    x3 = x.reshape(rows // period, period, w)
    return jnp.broadcast_to(x3[:, r : r + 1, :], x3.shape).reshape(rows, w)


def _blockdiag(rows, kind):
    row = lax.broadcasted_iota(jnp.int32, (rows, rows), 0)
    col = lax.broadcasted_iota(jnp.int32, (rows, rows), 1)
    if kind in ("next", "prev"):
        first, second = (row, col) if kind == "next" else (col, row)
        keep = ((row // HG_PAIR) == (col // HG_PAIR)) & (first % HG_PAIR < HG_CHUNK) & (second % HG_PAIR >= HG_CHUNK)
    else:
        keep = (row // HG_CHUNK) == (col // HG_CHUNK)
        if kind == "lower":
            keep = keep & (row >= col)
        elif kind == "upper":
            keep = keep & (row <= col)
    return jnp.where(keep, 1.0, 0.0).astype(BF16)


def _hg_operands(hq, hf, lb, rows):
    qq, kk, g, f, sig_f, sig_q = _hg_gates(hq, hf, lb)
    cum = _split_dot_left(_blockdiag(rows, "lower"), g)
    mid = _period_bcast(cum, HG_MID, rows, HG_CHUNK)
    last = _period_bcast(cum, HG_CHUNK - 1, rows, HG_CHUNK)
    last0 = _period_bcast(cum, HG_CHUNK - 1, rows, HG_PAIR)
    last1 = _period_bcast(cum, HG_PAIR - 1, rows, HG_PAIR)
    second = (lax.broadcasted_iota(jnp.int32, cum.shape, 0) % HG_PAIR) >= HG_CHUNK
    e = dict(qm=jnp.exp(cum - mid), km=jnp.exp(mid - cum), qd=jnp.exp(cum), kl=jnp.exp(last - cum),
             q_in=jnp.where(second, jnp.exp(last0), 1.0), k_out=jnp.where(second, 1.0, jnp.exp(last1)),
             pair=jnp.exp(last0 + last1))
    v = dict(qm=qq * e["qm"], km=kk * e["km"], qd=qq * e["qd"], kl=kk * e["kl"])
    v["qp"] = v["qd"] * e["q_in"]
    v["kp"] = v["kl"] * e["k_out"]
    return v, e, second, (f, sig_f, sig_q)


def _hg_store_operands(v, second, hi, refs):
    zero = jnp.zeros_like(v["qm"])
    q_cat, k_cat, qp_b, kp_b, v_b = refs
    q_cat[:, 0:D_MODEL] = jnp.where(second, zero, v["qm"]).astype(BF16)
    q_cat[:, D_MODEL : 2 * D_MODEL] = jnp.where(second, v["qm"], zero).astype(BF16)
    q_cat[:, 2 * D_MODEL :] = jnp.where(second, v["qd"], zero).astype(BF16)
    k_cat[:, 0:D_MODEL] = jnp.where(second, zero, v["km"]).astype(BF16)
    k_cat[:, D_MODEL : 2 * D_MODEL] = jnp.where(second, v["km"], zero).astype(BF16)
    k_cat[:, 2 * D_MODEL :] = jnp.where(second, zero, v["kl"]).astype(BF16)
    qp_b[...] = v["qp"].astype(BF16)
    kp_b[...] = v["kp"].astype(BF16)
    v_b[...] = hi.astype(BF16)


def _hg_pair_operands(cat, r0, c0):
    return jnp.concatenate([cat[r0 : r0 + HG_PAIR, g * D_MODEL + c0 : g * D_MODEL + c0 + HEAD_DIM] for g in range(3)], axis=1)


def _hg_fwd(proj, lbl):
    s_len = proj.shape[0]
    rows = min(HG_STEP, s_len)
    n_pairs = rows // HG_PAIR

    def body(hq_ref, hf_ref, hi_ref, lbl_ref, o_ref, st_ref, state, q_cat, k_cat, qp_b, kp_b, v_b):
        @pl.when(pl.program_id(0) == 0)
        def _():
            state[...] = jnp.zeros_like(state)

        v, e, second, _ = _hg_operands(hq_ref[...], hf_ref[...], _hg_lower_bound(lbl_ref), rows)
        _hg_store_operands(v, second, hi_ref[...], (q_cat, k_cat, qp_b, kp_b, v_b))
        e_pair = e["pair"]
        row = lax.broadcasted_iota(jnp.int32, (HG_PAIR, HG_PAIR), 0)
        col = lax.broadcasted_iota(jnp.int32, (HG_PAIR, HG_PAIR), 1)
        causal = row >= col

        for u in range(n_pairs):
            r0 = u * HG_PAIR
            sls = [(slice(r0, r0 + HG_PAIR), slice(h * HEAD_DIM, (h + 1) * HEAD_DIM)) for h in range(HEADS)]
            a_s = [jnp.where(causal, _dot_nt(_hg_pair_operands(q_cat, r0, h * HEAD_DIM),
                                             _hg_pair_operands(k_cat, r0, h * HEAD_DIM)), 0.0).astype(BF16)
                   for h in range(HEADS)]
            st_s = [state[h] for h in range(HEADS)]
            for h, sl in enumerate(sls):
                st_ref[u, h] = st_s[h]
                state[h] = st_s[h] * e_pair[r0 : r0 + 1, sl[1]] + _dot_tn(v_b[sl], kp_b[sl])
            for h, sl in enumerate(sls):
                o_ref[sl] = _dot(a_s[h], v_b[sl]) + _dot_nt(qp_b[sl], st_s[h].astype(BF16))

    def col_spec(off):
        return pl.BlockSpec((rows, D_MODEL), lambda s: (s, off // D_MODEL))

    bf_tile = pltpu.VMEM((rows, D_MODEL), BF16)
    bf_cat = pltpu.VMEM((rows, 3 * D_MODEL), BF16)
    scratch = [pltpu.VMEM((HEADS, HEAD_DIM, HEAD_DIM), F32), bf_cat, bf_cat, bf_tile, bf_tile, bf_tile]
    return pl.pallas_call(
        body,
        name="hg_fwd",
        grid=(s_len // rows,),
        in_specs=[col_spec(OFF_HG_Q), col_spec(OFF_HG_F), col_spec(OFF_HG_I), pl.BlockSpec((2, D_MODEL), lambda s: (0, 0))],
        out_specs=[
            pl.BlockSpec((rows, D_MODEL), lambda s: (s, 0)),
            pl.BlockSpec((n_pairs, HEADS, HEAD_DIM, HEAD_DIM), lambda s: (s, 0, 0, 0)),
        ],
        out_shape=[
            jax.ShapeDtypeStruct((s_len, D_MODEL), F32),
            jax.ShapeDtypeStruct((s_len // HG_PAIR, HEADS, HEAD_DIM, HEAD_DIM), F32),
        ],
        scratch_shapes=scratch,
        compiler_params=_cparams(("arbitrary",)),
    )(proj, proj, proj, lbl)


def _hg_bwd(proj, lbl, states, d_o):
    s_len = proj.shape[0]
    rows = min(HG_STEP, s_len)
    n_pairs = rows // HG_PAIR
    n_steps = s_len // rows

    def body(hq_ref, hf_ref, hi_ref, lbl_ref, st_ref, do_ref, dp_ref, dlb_ref,
             dstate, q_cat, k_cat, qp_b, kp_b, v_b, do_b, d_qcat, d_kcat, d_qp, d_kp, d_v, d_pair):
        @pl.when(pl.program_id(0) == 0)
        def _():
            dstate[...] = jnp.zeros_like(dstate)
            dlb_ref[...] = jnp.zeros_like(dlb_ref)

        lb = _hg_lower_bound(lbl_ref)
        hq = hq_ref[...]
        v, e, second, (f, sig_f, sig_q) = _hg_operands(hq, hf_ref[...], lb, rows)
        _hg_store_operands(v, second, hi_ref[...], (q_cat, k_cat, qp_b, kp_b, v_b))
        do_b[...] = do_ref[...].astype(BF16)
        e_pair = e["pair"]
        row = lax.broadcasted_iota(jnp.int32, (HG_PAIR, HG_PAIR), 0)
        col = lax.broadcasted_iota(jnp.int32, (HG_PAIR, HG_PAIR), 1)
        causal = row >= col

        for u in reversed(range(n_pairs)):
            r0 = u * HG_PAIR
            sls = [(slice(r0, r0 + HG_PAIR), slice(h * HEAD_DIM, (h + 1) * HEAD_DIM)) for h in range(HEADS)]
            ops = [(_hg_pair_operands(q_cat, r0, h * HEAD_DIM), _hg_pair_operands(k_cat, r0, h * HEAD_DIM))
                   for h in range(HEADS)]
            a_s = [jnp.where(causal, _dot_nt(lhs, rhs), 0.0).astype(BF16) for lhs, rhs in ops]
            da_s = [jnp.where(causal, _dot_nt(do_b[sl], v_b[sl]), 0.0).astype(BF16) for sl in sls]
            st0_s = [st_ref[u, h] for h in range(HEADS)]
            ds1_s = [dstate[h] for h in range(HEADS)]
            ds1b_s = [ds1.astype(BF16) for ds1 in ds1_s]
            for h, sl in enumerate(sls):
                decay = e_pair[r0 : r0 + 1, sl[1]]
                d_pair[u : u + 1, sl[1]] = decay * jnp.sum(ds1_s[h] * st0_s[h], axis=0, keepdims=True)
                dstate[h] = ds1_s[h] * decay + _dot_tn(do_b[sl], qp_b[sl])
            for h, sl in enumerate(sls):
                d_qp[sl] = _dot(do_b[sl], st0_s[h].astype(BF16))
                d_kp[sl] = _dot(v_b[sl], ds1b_s[h])
            for h, sl in enumerate(sls):
                d_v[sl] = _dot_tn(a_s[h], do_b[sl]) + _dot_nt(kp_b[sl], ds1b_s[h])
            for h, sl in enumerate(sls):
                d_lhs = _dot(da_s[h], ops[h][1])
                d_rhs = _dot_tn(da_s[h], ops[h][0])
                for g in range(3):
                    gsl = (sl[0], slice(g * D_MODEL + h * HEAD_DIM, g * D_MODEL + (h + 1) * HEAD_DIM))
                    d_qcat[gsl] = d_lhs[:, g * HEAD_DIM : (g + 1) * HEAD_DIM]
                    d_kcat[gsl] = d_rhs[:, g * HEAD_DIM : (g + 1) * HEAD_DIM]

        zero = jnp.zeros_like(hq)
        dqm = jnp.where(second, d_qcat[:, D_MODEL : 2 * D_MODEL], d_qcat[:, 0:D_MODEL])
        dkm = jnp.where(second, d_kcat[:, D_MODEL : 2 * D_MODEL], d_kcat[:, 0:D_MODEL])
        dqp, dkp = d_qp[...], d_kp[...]
        dqd = dqp * e["q_in"] + jnp.where(second, d_qcat[:, 2 * D_MODEL :], zero)
        dkl = dkp * e["k_out"] + jnp.where(second, zero, d_kcat[:, 2 * D_MODEL :])
        dq = dqm * e["qm"] + dqd * e["qd"]
        dk = dkm * e["km"] + dkl * e["kl"]
        t_kl = dkl * v["kl"]
        dcum = dqm * v["qm"] - dkm * v["km"] + dqd * v["qd"] - t_kl
        dp = d_pair[...]
        dp_b = jnp.broadcast_to(dp[:, None, :], (n_pairs, HG_PAIR, D_MODEL)).reshape(rows, D_MODEL)
        dg = (_split_dot_left(_blockdiag(rows, "upper"), dcum) + _split_dot_left(_blockdiag(rows, "all"), t_kl)
              + _split_dot_left(_blockdiag(rows, "next"), dqp * v["qp"])
              + _split_dot_left(_blockdiag(rows, "prev"), dkp * v["kp"]) + dp_b)
        df = dg / f - dk
        one_m = 1.0 - sig_f
        dp_ref[:, 0:D_MODEL] = (dq * (sig_q * (1.0 + hq * (1.0 - sig_q)))).astype(BF16)
        dp_ref[:, D_MODEL : 2 * D_MODEL] = (df * (1.0 - lb) * sig_f * one_m).astype(BF16)
        dp_ref[:, 2 * D_MODEL : 3 * D_MODEL] = d_v[...].astype(BF16)
        dlb_ref[...] += jnp.sum(df * one_m, axis=0, keepdims=True)

    def col_spec(off):
        return pl.BlockSpec((rows, D_MODEL), lambda s: (n_steps - 1 - s, off // D_MODEL))

    f32_tile = pltpu.VMEM((rows, D_MODEL), F32)
    f32_cat = pltpu.VMEM((rows, 3 * D_MODEL), F32)
    bf_tile = pltpu.VMEM((rows, D_MODEL), BF16)
    bf_cat = pltpu.VMEM((rows, 3 * D_MODEL), BF16)
    scratch = [pltpu.VMEM((HEADS, HEAD_DIM, HEAD_DIM), F32), bf_cat, bf_cat, bf_tile, bf_tile, bf_tile, bf_tile,
               f32_cat, f32_cat, f32_tile, f32_tile, f32_tile, pltpu.VMEM((n_pairs, D_MODEL), F32)]
    return pl.pallas_call(
        body,
        name="hg_bwd",
        grid=(n_steps,),
        in_specs=[
            col_spec(OFF_HG_Q), col_spec(OFF_HG_F), col_spec(OFF_HG_I),
            pl.BlockSpec((2, D_MODEL), lambda s: (0, 0)),
            pl.BlockSpec((n_pairs, HEADS, HEAD_DIM, HEAD_DIM), lambda s: (n_steps - 1 - s, 0, 0, 0)),
            pl.BlockSpec((rows, D_MODEL), lambda s: (n_steps - 1 - s, 0)),
        ],
        out_specs=[
            pl.BlockSpec((rows, 3 * D_MODEL), lambda s: (n_steps - 1 - s, 0)),
            pl.BlockSpec((1, D_MODEL), lambda s: (0, 0)),
        ],
        out_shape=[
            jax.ShapeDtypeStruct((s_len, 3 * D_MODEL), BF16),
            jax.ShapeDtypeStruct((1, D_MODEL), F32),
        ],
        scratch_shapes=scratch,
        compiler_params=_cparams(("arbitrary",)),
    )(proj, proj, proj, lbl, states, d_o)


def _mid(proj, sb_o, hg_o, x, target, b_gate, hg_gain, final_g, w_sb, w_hg, w_out):
    s_len = proj.shape[0]
    ts = min(256, s_len)
    inv_d = 1.0 / D_MODEL

    def body(zsb_ref, hz_ref, gl_ref, sbo_ref, hgo_ref, x_ref, tgt_ref, bg_ref, hgn_ref, fg_ref,
             wsb_ref, whg_ref, wout_ref,
             dout_ref, dsbo_ref, dhgo_ref, dmid_ref,
             asb_ref, dusb_ref, ahg_ref, duhg_ref, y_ref, doutb_ref,
             loss_ref, dfg_ref, dbg_ref, dhgn_ref):
        @pl.when(pl.program_id(0) == 0)
        def _():
            loss_ref[...] = jnp.zeros_like(loss_ref)
            dfg_ref[...] = jnp.zeros_like(dfg_ref)
            dbg_ref[...] = jnp.zeros_like(dbg_ref)
            dhgn_ref[...] = jnp.zeros_like(dhgn_ref)

        z_sb = zsb_ref[...]
        sb_o = sbo_ref[...]
        sig_zsb = _sigmoid(z_sb)
        silu_zsb = z_sb * sig_zsb
        a_sb_f = sb_o * silu_zsb
        a_sb = a_sb_f.astype(BF16)
        u_sb = _dot(a_sb, wsb_ref[...])

        hg_o = hgo_ref[...]
        gain = hgn_ref[...]
        r_parts, yn_parts = [], []
        for h in range(HEADS):
            oh = hg_o[:, h * HEAD_DIM : (h + 1) * HEAD_DIM]
            r = lax.rsqrt(jnp.mean(oh * oh, axis=-1, keepdims=True) + RMS_EPS)
            r_parts.append(jnp.broadcast_to(r, oh.shape))
            yn_parts.append(oh * r)
        r_hg = jnp.concatenate(r_parts, axis=-1)
        yn_hg = jnp.concatenate(yn_parts, axis=-1)
        hn = yn_hg * gain
        hz = hz_ref[...]
        sig_hz = _sigmoid(hz)
        silu_hz = hz * sig_hz
        a_hg_f = hn * silu_hz
        a_hg = a_hg_f.astype(BF16)
        u_hg = _dot(a_hg, whg_ref[...])

        gates = _sigmoid(gl_ref[...] + bg_ref[...])
        g_sb = gates[:, 0:D_MODEL]
        g_hg = gates[:, D_MODEL:]
        y_f = g_sb * u_sb + g_hg * u_hg
        y = y_f.astype(BF16)
        out = x_ref[...] + _dot(y, wout_ref[...])
        r2 = lax.rsqrt(jnp.mean(out * out, axis=-1, keepdims=True) + RMS_EPS)
        yn = out * r2
        fg = fg_ref[...]
        diff = yn * fg - tgt_ref[...]
        loss_ref[...] += 0.5 * inv_d * jnp.sum(diff * diff)

        dyf = diff * inv_d
        dfg_ref[...] += jnp.sum(dyf * yn, axis=0, keepdims=True)
        dyn = dyf * fg
        dout = r2 * (dyn - yn * jnp.mean(dyn * yn, axis=-1, keepdims=True))
        dout_ref[...] = dout
        doutb = dout.astype(BF16)
        doutb_ref[...] = doutb
        dy = _dot_nt(doutb, wout_ref[...])
        du_sb = (dy * g_sb).astype(BF16)
        du_hg = (dy * g_hg).astype(BF16)
        dgl_sb = dy * u_sb * g_sb * (1.0 - g_sb)
        dgl_hg = dy * u_hg * g_hg * (1.0 - g_hg)
        dmid_ref[:, 2 * D_MODEL : 3 * D_MODEL] = dgl_sb.astype(BF16)
        dmid_ref[:, 3 * D_MODEL :] = dgl_hg.astype(BF16)
        dbg_ref[:, 0:D_MODEL] += jnp.sum(dgl_sb, axis=0, keepdims=True)
        dbg_ref[:, D_MODEL:] += jnp.sum(dgl_hg, axis=0, keepdims=True)

        da_sb = _dot_nt(du_sb, wsb_ref[...])
        dsbo_ref[...] = (da_sb * silu_zsb).astype(BF16)
        dmid_ref[:, 0:D_MODEL] = (da_sb * sb_o * (sig_zsb * (1.0 + z_sb * (1.0 - sig_zsb)))).astype(BF16)

        da_hg = _dot_nt(du_hg, whg_ref[...])
        dhn = da_hg * silu_hz
        dmid_ref[:, D_MODEL : 2 * D_MODEL] = (da_hg * hn * (sig_hz * (1.0 + hz * (1.0 - sig_hz)))).astype(BF16)
        dhgn_ref[...] += jnp.sum(dhn * yn_hg, axis=0, keepdims=True)
        dyn_hg = dhn * gain
        prod = dyn_hg * yn_hg
        m_parts = []
        for h in range(HEADS):
            ph = prod[:, h * HEAD_DIM : (h + 1) * HEAD_DIM]
            m_parts.append(jnp.broadcast_to(jnp.mean(ph, axis=-1, keepdims=True), ph.shape))
        dhgo_ref[...] = (r_hg * (dyn_hg - yn_hg * jnp.concatenate(m_parts, axis=-1))).astype(BF16)

        asb_ref[...] = a_sb_f.T.astype(BF16)
        dusb_ref[...] = du_sb
        ahg_ref[...] = a_hg_f.T.astype(BF16)
        duhg_ref[...] = du_hg
        y_ref[...] = y_f.T.astype(BF16)

    def tile(width, off=0):
        return pl.BlockSpec((ts, width), lambda s: (s, off // width))

    def across():
        return pl.BlockSpec((D_MODEL, ts), lambda s: (0, s))

    def whole(shape):
        return pl.BlockSpec(shape, lambda s: (0,) * len(shape))

    def weight():
        return pl.BlockSpec((D_MODEL, D_MODEL), lambda s: (0, 0), pipeline_mode=pl.Buffered(1))

    f32_act = jax.ShapeDtypeStruct((s_len, D_MODEL), F32)
    bf_act = jax.ShapeDtypeStruct((s_len, D_MODEL), BF16)
    bf_act_t = jax.ShapeDtypeStruct((D_MODEL, s_len), BF16)
    return pl.pallas_call(
        body,
        name="mid",
        grid=(s_len // ts,),
        in_specs=[
            tile(D_MODEL, OFF_SB_Z), tile(D_MODEL, OFF_HG_Z), tile(2 * D_MODEL, OFF_GATE),
            tile(D_MODEL), tile(D_MODEL), tile(D_MODEL), tile(D_MODEL),
            whole((1, 2 * D_MODEL)), whole((1, D_MODEL)), whole((1, D_MODEL)),
            weight(), weight(), weight(),
        ],
        out_specs=[
            tile(D_MODEL), tile(D_MODEL), tile(D_MODEL), tile(4 * D_MODEL),
            across(), tile(D_MODEL), across(), tile(D_MODEL), across(), tile(D_MODEL),
            whole((1, 1)), whole((1, D_MODEL)), whole((1, 2 * D_MODEL)), whole((1, D_MODEL)),
        ],
        out_shape=[
            f32_act, bf_act, bf_act, jax.ShapeDtypeStruct((s_len, 4 * D_MODEL), BF16),
            bf_act_t, bf_act, bf_act_t, bf_act, bf_act_t, bf_act,
            jax.ShapeDtypeStruct((1, 1), F32), jax.ShapeDtypeStruct((1, D_MODEL), F32),
            jax.ShapeDtypeStruct((1, 2 * D_MODEL), F32), jax.ShapeDtypeStruct((1, D_MODEL), F32),
        ],
        compiler_params=_cparams(("arbitrary",)),
    )(proj, proj, proj, sb_o, hg_o, x, target, b_gate, hg_gain, final_g, w_sb, w_hg, w_out)


def _grad_square(a_t, b, name):
    s_len = b.shape[0]
    tk = min(1024, s_len)

    def body(a_ref, b_ref, o_ref):
        @pl.when(pl.program_id(0) == 0)
        def _():
            o_ref[...] = jnp.zeros_like(o_ref)

        o_ref[...] += _dot(a_ref[...], b_ref[...])

    return pl.pallas_call(
        body,
        name=name,
        grid=(s_len // tk,),
        in_specs=[pl.BlockSpec((D_MODEL, tk), lambda k: (0, k)), pl.BlockSpec((tk, D_MODEL), lambda k: (k, 0))],
        out_specs=pl.BlockSpec((D_MODEL, D_MODEL), lambda k: (0, 0)),
        out_shape=jax.ShapeDtypeStruct((D_MODEL, D_MODEL), F32),
        compiler_params=_cparams(("arbitrary",)),
    )(a_t, b)


SEG_WIDTHS = (1024, 1024, 1024, 4096, 3072)
SEG_TILE = 1024
SEG_BOUNDS = (0, 1, 2, 3, 7, 10)


def _w_in_tile(k):
    return jnp.where(k < 4, k, jnp.where(k < 7, k + 3, k - 3))


def _grad_w_in(h_t, segs):
    m, s_len = h_t.shape
    tk = min(1024, s_len)
    tn = SEG_TILE
    nk = s_len // tk
    bounds = SEG_BOUNDS

    def body(a_ref, *refs):
        seg_refs, o_ref = refs[:-1], refs[-1]
        j = pl.program_id(0)

        @pl.when(pl.program_id(1) == 0)
        def _():
            o_ref[...] = jnp.zeros_like(o_ref)

        for i, ref in enumerate(seg_refs):
            @pl.when((j >= bounds[i]) & (j < bounds[i + 1]))
            def _(ref=ref):
                o_ref[...] += _dot(a_ref[...], ref[...])

    def seg_spec(lo, hi):
        def index(j, k):
            return (jnp.where(j < lo, 0, jnp.where(j >= hi, nk - 1, k)), jnp.clip(j - lo, 0, hi - lo - 1))
        return pl.BlockSpec((tk, tn), index)

    return pl.pallas_call(
        body,
        name="grad_w_in",
        grid=(IN_WIDTH // tn, nk),
        in_specs=[pl.BlockSpec((m, tk), lambda j, k: (0, k))]
        + [seg_spec(bounds[i], bounds[i + 1]) for i in range(len(SEG_WIDTHS))],
        out_specs=pl.BlockSpec((m, tn), lambda j, k: (0, _w_in_tile(j))),
        out_shape=jax.ShapeDtypeStruct((m, IN_WIDTH), F32),
        compiler_params=_cparams(("arbitrary", "arbitrary")),
    )(h_t, *segs)


EXCHANGE_IN_PIECES = 8
EXCHANGE_PIECES = EXCHANGE_IN_PIECES + 3


def _exchange_copies(sin_ref, ssq_ref, got_in, got_sq, send_sems, recv_sems):
    _, _, c, chips = _position()
    rows = HALF_IN // EXCHANGE_IN_PIECES
    copies = []
    for k, (px, py) in enumerate(chips):
        chip = 2 * px + py
        for p in range(EXCHANGE_PIECES):
            if p < EXCHANGE_IN_PIECES:
                src, dst = sin_ref.at[chip, pl.ds(p * rows, rows), :], got_in.at[k, pl.ds(p * rows, rows), :]
            else:
                src, dst = ssq_ref.at[p - EXCHANGE_IN_PIECES, chip], got_sq.at[k, p - EXCHANGE_IN_PIECES]
            copies.append(_remote(src, dst, send_sems.at[k, p], recv_sems.at[k, p], (px, py, c)))
    return copies


def _dx(segs, w_all, x, norm_g, dout, s_in, s_sq):
    s_len = x.shape[0]
    ts = min(1024, s_len)
    tk = SEG_TILE
    nk = IN_WIDTH // tk
    ns = s_len // ts
    bounds = SEG_BOUNDS
    n_seg = len(SEG_WIDTHS)

    def body(*refs):
        seg_refs = refs[:n_seg]
        w_ref, x_ref, g_ref, dout_ref, sin_ref, ssq_ref, gx_ref, dg_ref, got_in, got_sq, acc, send_sems, recv_sems = refs[n_seg:]
        s, k = pl.program_id(0), pl.program_id(1)

        @pl.when((s == 0) & (k == 0))
        def _():
            dg_ref[...] = jnp.zeros_like(dg_ref)
            for cp in _exchange_copies(sin_ref, ssq_ref, got_in, got_sq, send_sems, recv_sems):
                cp.start()

        @pl.when(k == 0)
        def _():
            acc[...] = jnp.zeros_like(acc)

        for i, ref in enumerate(seg_refs):
            @pl.when((k >= bounds[i]) & (k < bounds[i + 1]))
            def _(ref=ref):
                acc[...] += _dot_nt(ref[...], w_ref[...])

        @pl.when(k == nk - 1)
        def _():
            dh = acc[...]
            xv = x_ref[...]
            r = lax.rsqrt(jnp.mean(xv * xv, axis=-1, keepdims=True) + RMS_EPS)
            xn = xv * r
            dg_ref[...] += jnp.sum(dh * xn, axis=0, keepdims=True)
            dxn = dh * g_ref[...]
            gx_ref[...] = r * (dxn - xn * jnp.mean(dxn * xn, axis=-1, keepdims=True)) + dout_ref[...]

        @pl.when((s == ns - 1) & (k == nk - 1))
        def _():
            for cp in _exchange_copies(sin_ref, ssq_ref, got_in, got_sq, send_sems, recv_sems):
                cp.wait()

    def seg_spec(lo, hi):
        return pl.BlockSpec((ts, tk), lambda s, k: (s, jnp.clip(k - lo, 0, hi - lo - 1)))

    row_tile = pl.BlockSpec((ts, D_MODEL), lambda s, k: (s, 0))
    vec = pl.BlockSpec((1, D_MODEL), lambda s, k: (0, 0))
    return pl.pallas_call(
        body,
        name="dx",
        grid=(ns, nk),
        in_specs=[seg_spec(bounds[i], bounds[i + 1]) for i in range(n_seg)] + [
            pl.BlockSpec((D_MODEL, tk), lambda s, k: (0, _w_in_tile(k))),
            row_tile, vec, row_tile, ANY, ANY,
        ],
        out_specs=[row_tile, vec, ANY, ANY],
        out_shape=[jax.ShapeDtypeStruct((s_len, D_MODEL), F32), jax.ShapeDtypeStruct((1, D_MODEL), F32),
                   jax.ShapeDtypeStruct((3, HALF_IN, W_IN_SHARD), WIRE),
                   jax.ShapeDtypeStruct((3, 3, HALF_SQ, D_MODEL), WIRE)],
        scratch_shapes=[pltpu.VMEM((ts, D_MODEL), F32),
                        pltpu.SemaphoreType.DMA((3, EXCHANGE_PIECES)), pltpu.SemaphoreType.DMA((3, EXCHANGE_PIECES))],
        compiler_params=_cparams(("arbitrary", "arbitrary"), vmem=VMEM_LIMIT_DX),
    )(*segs, w_all, x, norm_g, dout, s_in, s_sq)


def _local_grads(x, target, proj, h_t, qkv, b_gate, lbl, hg_gain, final_g, w_sb, w_hg, w_out):
    sb_o, sb_o_fine = _sb_fwd(qkv)
    hg_o, states = _hg_fwd(proj, lbl)
    (dout, d_sbo, d_hgo, d_mid, a_sb, du_sb, a_hg, du_hg, y, doutb,
     loss, d_fg, d_bg, d_hgn) = _mid(proj, sb_o, hg_o, x, target, b_gate, hg_gain, final_g, w_sb, w_hg, w_out)
    g_w_sb = _grad_square(a_sb, du_sb, "grad_w_sb")
    g_w_hg = _grad_square(a_hg, du_hg, "grad_w_hg")
    g_w_out = _grad_square(y, doutb, "grad_w_out")
    d_q, d_k, d_v = _sb_bwd(qkv, sb_o_fine, d_sbo)
    d_hg, d_lb = _hg_bwd(proj, lbl, states, d_hgo)
    segs = (d_q, d_k, d_v, d_mid, d_hg)
    g_w_in = _grad_w_in(h_t, segs)
    return g_w_in, g_w_sb, g_w_hg, g_w_out, segs, dout, loss, d_bg, d_lb, d_hgn, d_fg


ANY = pl.BlockSpec(memory_space=pl.ANY)
WIRE = BF16
HALF_IN = D_MODEL // 2
HALF_SQ = ROW_SHARD // 2


def _position():
    x, y, c = lax.axis_index("x"), lax.axis_index("y"), lax.axis_index("c")
    chips = [(1 - x, y), (x, 1 - y), (1 - x, 1 - y)]
    return x, y, c, chips


def _remote(src, dst, send_sem, recv_sem, to):
    return pltpu.make_async_remote_copy(src_ref=src, dst_ref=dst, send_sem=send_sem, recv_sem=recv_sem,
                                        device_id=to, device_id_type=MESH)


PROJ_TILE = 1280
F32_FROM_TILE = 2
BF16_TO_TILE = 2
W_LOAD_PIECES = 8


def _gather_inproj(idx, h, w_in_b, w_sq_b):
    s_len = h.shape[0]
    ts = min(1024, s_len)
    ns = s_len // ts
    per = W_IN_SHARD // PROJ_TILE
    n_in = 4
    n_piece = n_in + 3
    rows = HALF_IN // n_in

    def chip_at(r, me):
        return me ^ jnp.where(r == 1, 2, jnp.where(r == 2, 1, jnp.where(r == 3, 3, 0)))

    def body(idx_ref, h_ref, win_ref, wsqb_ref, proj_ref, qkv_ref, wall_ref, wsq_ref, wbuf, send_sems, recv_sems, w_sems):
        r, t, s = pl.program_id(0), pl.program_id(1), pl.program_id(2)
        x, y, c, chips = _position()
        me = 2 * x + y
        sibling = (x, y, 1 - c)
        first = (t == 0) & (s == 0)

        def src_piece(p):
            if p < n_in:
                return win_ref.at[pl.ds(c * HALF_IN + p * rows, rows), :]
            return wsqb_ref.at[p - n_in, pl.ds(c * HALF_SQ, HALF_SQ), :]

        def piece(p, chip, core):
            if p < n_in:
                cols = pl.ds(pl.multiple_of(chip * W_IN_SHARD, W_IN_SHARD), W_IN_SHARD)
                return wall_ref.at[pl.ds(core * HALF_IN + p * rows, rows), cols]
            return wsq_ref.at[p - n_in, chip, pl.ds(core * HALF_SQ, HALF_SQ), :]

        def send(k, p):
            px, py = chips[k]
            return _remote(src_piece(p), piece(p, me, c), send_sems.at[k, p], recv_sems.at[k, p], (px, py, c))

        def forward(k, p, core):
            px, py = chips[k]
            got = piece(p, 2 * px + py, core)
            return _remote(got, got, send_sems.at[3 + k, p], recv_sems.at[3 + k, p], sibling)

        @pl.when((r == 0) & first)
        def _():
            for k in range(2):
                for p in range(n_piece):
                    send(k, p).start()

        for k in range(3):
            @pl.when((r == k + 1) & first)
            def _(k=k):
                px, py = chips[k]
                for p in range(n_piece):
                    got = piece(p, 2 * px + py, c)
                    _remote(got, got, send_sems.at[k, p], recv_sems.at[k, p], (px, py, c)).wait_recv()
                    forward(k, p, c).start()
                if k == 0:
                    for p in range(n_piece):
                        send(2, p).start()
                for p in range(n_piece):
                    forward(k, p, 1 - c).wait_recv()

        def tile_loads(slot, own):
            col = slot * PROJ_TILE
            if not own:
                col = pl.multiple_of(chip_at(r, me) * W_IN_SHARD + col, PROJ_TILE)
            src = win_ref if own else wall_ref
            part = D_MODEL // W_LOAD_PIECES
            return [pltpu.make_async_copy(src.at[pl.ds(q * part, part), pl.ds(col, PROJ_TILE)],
                                          wbuf.at[slot, pl.ds(q * part, part), :], w_sems.at[slot, q])
                    for q in range(W_LOAD_PIECES)]

        for own in (True, False):
            @pl.when(first & ((r == 0) if own else (r > 0)))
            def _(own=own):
                for slot in range(per):
                    for cp in tile_loads(slot, own):
                        cp.start()
                for cp in tile_loads(0, own):
                    cp.wait()

            @pl.when((t > 0) & (s == 0) & ((r == 0) if own else (r > 0)))
            def _(own=own):
                for cp in tile_loads(1, own):
                    cp.wait()

        tile_now = per * chip_at(r, me) + t
        want_f32, want_bf16 = tile_now >= F32_FROM_TILE, tile_now <= BF16_TO_TILE

        @pl.when(want_f32 & jnp.logical_not(want_bf16))
        def _():
            proj_ref[...] = _dot(h_ref[...], wbuf[t])

        @pl.when(want_bf16 & jnp.logical_not(want_f32))
        def _():
            qkv_ref[...] = _dot(h_ref[...], wbuf[t]).astype(BF16)

        @pl.when(want_f32 & want_bf16)
        def _():
            p = _dot(h_ref[...], wbuf[t])
            proj_ref[...] = p
            qkv_ref[...] = p.astype(BF16)

        @pl.when((r == 3) & (t == per - 1) & (s == ns - 1))
        def _():
            for k in range(3):
                for p in range(n_piece):
                    send(k, p).wait_send()
                    forward(k, p, c).wait_send()

    def out_index(wanted):
        order = [0, 2, 1, 3]
        table = []
        for chip in range(N_CHIPS):
            tiles = [per * (chip ^ order[q // per]) + q % per for q in range(N_CHIPS * per)]
            row = []
            for q, tile in enumerate(tiles):
                if wanted(tile):
                    row.append((tile, None))
                    continue
                before = [u for u in tiles[:q] if wanted(u)]
                after = [u for u in tiles[q:] if wanted(u)]
                row.append((before[-1], ns - 1) if before else (after[0], 0))
            table.append(row)

        def index(r, t, s, idx):
            q = r * per + t
            col, fixed_s = jnp.int32(0), jnp.int32(-1)
            for chip in range(N_CHIPS):
                for pos, (tile, hold) in enumerate(table[chip]):
                    here = (idx[0] == chip) & (q == pos)
                    col = jnp.where(here, tile, col)
                    fixed_s = jnp.where(here, -1 if hold is None else hold, fixed_s)
            return jnp.where(fixed_s < 0, s, fixed_s), col

        return index

    grid_spec = pltpu.PrefetchScalarGridSpec(
        num_scalar_prefetch=1,
        grid=(N_CHIPS, per, ns),
        in_specs=[pl.BlockSpec((ts, D_MODEL), lambda r, t, s, idx: (s, 0)), ANY, ANY],
        out_specs=[pl.BlockSpec((ts, PROJ_TILE), out_index(lambda tile: tile >= F32_FROM_TILE)),
                   pl.BlockSpec((ts, PROJ_TILE), out_index(lambda tile: tile <= BF16_TO_TILE)),
                   ANY, ANY],
        scratch_shapes=[pltpu.VMEM((per, D_MODEL, PROJ_TILE), BF16),
                        pltpu.SemaphoreType.DMA((6, n_piece)), pltpu.SemaphoreType.DMA((6, n_piece)),
                        pltpu.SemaphoreType.DMA((per, W_LOAD_PIECES))],
    )
    return pl.pallas_call(
        body,
        name="gather_inproj",
        grid_spec=grid_spec,
        out_shape=[jax.ShapeDtypeStruct((s_len, IN_WIDTH), F32),
                   jax.ShapeDtypeStruct((s_len, IN_WIDTH), BF16),
                   jax.ShapeDtypeStruct((D_MODEL, IN_WIDTH), BF16),
                   jax.ShapeDtypeStruct((3, N_CHIPS, ROW_SHARD, D_MODEL), BF16)],
        compiler_params=_cparams(("arbitrary", "arbitrary", "arbitrary")),
    )(idx, h, w_in_b, w_sq_b)


def _place_own(idx, w_in_b, w_sq_b, w_all, wsq):
    n = 4
    r_in, r_sq = D_MODEL // n, ROW_SHARD // n

    def body(idx_ref, win_ref, wsq_ref, w_all_in, wsq_in, w_all_out, wsq_out):
        w_all_out[...] = win_ref[...]
        wsq_out[:, 0] = wsq_ref[...]

    grid_spec = pltpu.PrefetchScalarGridSpec(
        num_scalar_prefetch=1,
        grid=(n,),
        in_specs=[pl.BlockSpec((r_in, W_IN_SHARD), lambda r, idx: (r, 0)),
                  pl.BlockSpec((3, r_sq, D_MODEL), lambda r, idx: (0, r, 0)), ANY, ANY],
        out_specs=[pl.BlockSpec((r_in, W_IN_SHARD), lambda r, idx: (r, idx[0])),
                   pl.BlockSpec((3, 1, r_sq, D_MODEL), lambda r, idx: (0, idx[0], r, 0))],
    )
    return pl.pallas_call(
        body,
        name="place_own",
        grid_spec=grid_spec,
        out_shape=[jax.ShapeDtypeStruct(w_all.shape, BF16), jax.ShapeDtypeStruct(wsq.shape, BF16)],
        input_output_aliases={3: 0, 4: 1},
        compiler_params=_cparams(("arbitrary",)),
    )(idx, w_in_b, w_sq_b, w_all, wsq)


def _swap_halves(g_in, g_sq):
    n_in = 16
    n_piece = n_in + 3 * N_CHIPS
    rows = HALF_IN // n_in

    def body(gin_ref, gsq_ref, got_in, got_sq, send_sems, recv_sems):
        x, y, c, _ = _position()
        sibling = (x, y, 1 - c)

        def src_piece(p):
            if p < n_in:
                return gin_ref.at[pl.ds((1 - c) * HALF_IN + p * rows, rows), :]
            a, chip = divmod(p - n_in, N_CHIPS)
            return gsq_ref.at[a, chip, pl.ds((1 - c) * HALF_SQ, HALF_SQ), :]

        def dst_piece(p):
            if p < n_in:
                return got_in.at[pl.ds(p * rows, rows), :]
            a, chip = divmod(p - n_in, N_CHIPS)
            return got_sq.at[a, chip]

        out = [_remote(src_piece(p), dst_piece(p), send_sems.at[p], recv_sems.at[p], sibling) for p in range(n_piece)]
        for cp in out:
            cp.start()
        for cp in out:
            cp.wait()

    return pl.pallas_call(
        body,
        name="swap_halves",
        in_specs=[ANY, ANY],
        out_specs=[ANY, ANY],
        out_shape=[jax.ShapeDtypeStruct((HALF_IN, IN_WIDTH), F32),
                   jax.ShapeDtypeStruct((3, N_CHIPS, HALF_SQ, D_MODEL), F32)],
        scratch_shapes=[pltpu.SemaphoreType.DMA((n_piece,))] * 2,
    )(g_in, g_sq)


def _join_halves(r_in, r_sq):
    n_in = 16
    n_piece = n_in + 3
    rows = HALF_IN // n_in

    def body(in_alias, sq_alias, full_in, full_sq, send_sems, recv_sems):
        del in_alias, sq_alias
        x, y, c, _ = _position()
        sibling = (x, y, 1 - c)

        def piece(p, core):
            if p < n_in:
                return full_in.at[pl.ds(core * HALF_IN + p * rows, rows), :]
            return full_sq.at[p - n_in, pl.ds(core * HALF_SQ, HALF_SQ), :]

        out = [_remote(piece(p, c), piece(p, c), send_sems.at[p], recv_sems.at[p], sibling) for p in range(n_piece)]
        for cp in out:
            cp.start()
        for p in range(n_piece):
            _remote(piece(p, 1 - c), piece(p, 1 - c), send_sems.at[p], recv_sems.at[p], sibling).wait_recv()
        for cp in out:
            cp.wait_send()

    return pl.pallas_call(
        body,
        name="join_halves",
        in_specs=[ANY, ANY],
        out_specs=[ANY, ANY],
        out_shape=[jax.ShapeDtypeStruct((D_MODEL, W_IN_SHARD), F32),
                   jax.ShapeDtypeStruct((3, ROW_SHARD, D_MODEL), F32)],
        input_output_aliases={0: 0, 1: 1},
        scratch_shapes=[pltpu.SemaphoreType.DMA((n_piece,)), pltpu.SemaphoreType.DMA((n_piece,))],
    )(r_in, r_sq)


SMALL_ROWS = 56
N_DEV = 8


def _sum_small(part):
    def body(part_ref, out_ref, slots, send_sems, recv_sems):
        x, y, c, _ = _position()
        me = 4 * x + 2 * y + c
        slots[me] = part_ref[...]
        out = []
        for r in range(1, N_DEV):
            rx, ry, rc = (r >> 2) & 1, (r >> 1) & 1, r & 1
            to = (1 - x if rx else x, 1 - y if ry else y, 1 - c if rc else c)
            out.append(_remote(part_ref, slots.at[me], send_sems.at[r - 1], recv_sems.at[r - 1], to))
        for cp in out:
            cp.start()
        for r in range(1, N_DEV):
            _remote(part_ref, slots.at[me ^ r], send_sems.at[r - 1], recv_sems.at[r - 1], (x, y, c)).wait_recv()
        for cp in out:
            cp.wait_send()
        total = slots[0]
        for d in range(1, N_DEV):
            total = total + slots[d]
        out_ref[...] = total

    vmem = pl.BlockSpec(memory_space=pltpu.VMEM)
    return pl.pallas_call(
        body,
        name="sum_small",
        in_specs=[vmem],
        out_specs=vmem,
        out_shape=jax.ShapeDtypeStruct((SMALL_ROWS, HEAD_DIM), F32),
        scratch_shapes=[pltpu.VMEM((N_DEV, SMALL_ROWS, HEAD_DIM), F32),
                        pltpu.SemaphoreType.DMA((N_DEV - 1,)), pltpu.SemaphoreType.DMA((N_DEV - 1,))],
    )(part)


def _prefetch_call(body, name, idx, grid, in_specs, out_specs, out_shape, args):
    grid_spec = pltpu.PrefetchScalarGridSpec(num_scalar_prefetch=1, grid=grid, in_specs=in_specs, out_specs=out_specs)
    return pl.pallas_call(body, name=name, grid_spec=grid_spec, out_shape=out_shape,
                          compiler_params=_cparams(("arbitrary",) * len(grid)))(idx, *args)


def _sum_a_in(idx, g_in, got_in):
    tr = 128
    nr = HALF_IN // tr

    def body(idx_ref, a_ref, b_ref, o_ref):
        o_ref[0] = (a_ref[...] + b_ref[...]).astype(WIRE)

    return _prefetch_call(
        body, "sum_a_in", idx, (N_CHIPS, nr),
        [pl.BlockSpec((tr, W_IN_SHARD), lambda j, r, idx: (idx[1] * nr + r, j)),
         pl.BlockSpec((tr, W_IN_SHARD), lambda j, r, idx: (r, j))],
        pl.BlockSpec((1, tr, W_IN_SHARD), lambda j, r, idx: (j, r, 0)),
        jax.ShapeDtypeStruct((N_CHIPS, HALF_IN, W_IN_SHARD), WIRE), (g_in, got_in))


def _sum_a_sq(idx, g_sq, got_sq):
    blk = (1, 1, HALF_SQ, D_MODEL)

    def body(idx_ref, a_ref, b_ref, o_ref):
        o_ref[...] = (a_ref[...] + b_ref[...]).astype(WIRE)

    return _prefetch_call(
        body, "sum_a_sq", idx, (3, N_CHIPS),
        [pl.BlockSpec(blk, lambda a, j, idx: (a, j, idx[1], 0)), pl.BlockSpec(blk, lambda a, j, idx: (a, j, 0, 0))],
        pl.BlockSpec(blk, lambda a, j, idx: (a, j, 0, 0)),
        jax.ShapeDtypeStruct((3, N_CHIPS, HALF_SQ, D_MODEL), WIRE), (g_sq, got_sq))


def _sum_b_in(idx, s_in, got_in):
    tr = 128
    nr = HALF_IN // tr

    def body(idx_ref, a_ref, b_ref, o_ref):
        o_ref[...] = ((a_ref[0].astype(F32) + b_ref[0].astype(F32)) + b_ref[1].astype(F32)) + b_ref[2].astype(F32)

    return _prefetch_call(
        body, "sum_b_in", idx, (nr,),
        [pl.BlockSpec((1, tr, W_IN_SHARD), lambda r, idx: (idx[0], r, 0)),
         pl.BlockSpec((3, tr, W_IN_SHARD), lambda r, idx: (0, r, 0))],
        pl.BlockSpec((tr, W_IN_SHARD), lambda r, idx: (idx[1] * nr + r, 0)),
        jax.ShapeDtypeStruct((D_MODEL, W_IN_SHARD), F32), (s_in, got_in))


def _sum_b_sq(idx, s_sq, got_sq):
    def body(idx_ref, a_ref, b_ref, o_ref):
        o_ref[0] = ((a_ref[0, 0].astype(F32) + b_ref[0, 0].astype(F32)) + b_ref[1, 0].astype(F32)) + b_ref[2, 0].astype(F32)

    return _prefetch_call(
        body, "sum_b_sq", idx, (3,),
        [pl.BlockSpec((1, 1, HALF_SQ, D_MODEL), lambda a, idx: (a, idx[0], 0, 0)),
         pl.BlockSpec((3, 1, HALF_SQ, D_MODEL), lambda a, idx: (0, a, 0, 0))],
        pl.BlockSpec((1, HALF_SQ, D_MODEL), lambda a, idx: (a, idx[1], 0)),
        jax.ShapeDtypeStruct((3, ROW_SHARD, D_MODEL), F32), (s_sq, got_sq))


def _adamw_math(w, g, m, v):
    m = ADAM_B1 * m + (1.0 - ADAM_B1) * g
    v = ADAM_B2 * v + (1.0 - ADAM_B2) * (g * g)
    m_hat = m / (1.0 - ADAM_B1 ** ADAM_STEP)
    v_hat = v / (1.0 - ADAM_B2 ** ADAM_STEP)
    delta = -ADAM_LR * (m_hat / (jnp.sqrt(v_hat) + ADAM_EPS) + ADAM_WD * w)
    return delta, m, v


def _adamw(w, g, m, v, name):
    rows, cols = w.shape
    tr = min(128, rows)

    def body(w_ref, g_ref, m_ref, v_ref, d_ref, nm_ref, nv_ref):
        d_ref[...], nm_ref[...], nv_ref[...] = _adamw_math(w_ref[...], g_ref[...], m_ref[...], v_ref[...])

    spec = pl.BlockSpec((tr, cols), lambda r: (r, 0))
    return pl.pallas_call(
        body,
        name=name,
        grid=(rows // tr,),
        in_specs=[spec] * 4,
        out_specs=[spec] * 3,
        out_shape=[jax.ShapeDtypeStruct((rows, cols), F32)] * 3,
        compiler_params=_cparams(("arbitrary",)),
    )(w, g, m, v)


def _adamw_small(sums, w, m, v):
    def body(s_ref, w_ref, m_ref, v_ref, loss_ref, g_ref, d_ref, nm_ref, nv_ref):
        s = s_ref[...]
        w = w_ref[...]
        loss_ref[...] = s[0:1, 0:1]
        l0, l1 = w[24:32], w[32:40]
        mx = jnp.maximum(l0, l1)
        e0, e1 = jnp.exp(l0 - mx), jnp.exp(l1 - mx)
        p0, p1 = e0 / (e0 + e1), e1 / (e0 + e1)
        d_lb = s[32:40]
        g = jnp.concatenate([s[8:16], s[16:32], d_lb * p0 * (1.0 - p0), -d_lb * p0 * p1, s[40:48], s[48:56]], axis=0)
        g_ref[...] = g
        d_ref[...], nm_ref[...], nv_ref[...] = _adamw_math(w, g, m_ref[...], v_ref[...])

    packed = jax.ShapeDtypeStruct((SMALL_ROWS, HEAD_DIM), F32)
    return pl.pallas_call(
        body,
        name="adamw_small",
        out_shape=[jax.ShapeDtypeStruct((1, 1), F32), packed, packed, packed, packed],
    )(sums, w, m, v)


def _pack_small(ng, bg, lbl, hgn, fg):
    return jnp.concatenate([a.reshape(-1, HEAD_DIM) for a in (ng, bg, lbl, hgn, fg)], axis=0)


def _unpack_small(p):
    return (p[0:8].reshape(1, D_MODEL), p[8:24].reshape(1, 2 * D_MODEL), p[24:40].reshape(2, HEADS, HEAD_DIM),
            p[40:48].reshape(1, HEADS, HEAD_DIM), p[48:56].reshape(D_MODEL))


def kernel(x, norm_g, w_in, b_gate, lb_logits, hg_norm_g, w_sb_proj, w_hg_proj, w_out, final_norm_g, loss_target, m_norm_g, m_w_in, m_b_gate, m_lb_logits, m_hg_norm_g, m_w_sb_proj, m_w_hg_proj, m_w_out, m_final_norm_g, v_norm_g, v_w_in, v_b_gate, v_lb_logits, v_hg_norm_g, v_w_sb_proj, v_w_hg_proj, v_w_out, v_final_norm_g):
    s_len = x.shape[1]
    w_sq = jnp.stack([w_sb_proj[0], w_hg_proj[0], w_out[0]])
    idx = jnp.stack([2 * lax.axis_index("x") + lax.axis_index("y"), lax.axis_index("c")]).astype(jnp.int32)
    w_in_b, w_sq_b = w_in[0].astype(BF16), w_sq.astype(BF16)
    h, h_t = _prenorm(x[0], norm_g)
    proj, qkv, w_all, wsq = _gather_inproj(idx, h, w_in_b, w_sq_b)
    w_all, wsq = _place_own(idx, w_in_b, w_sq_b, w_all, wsq)
    wsq = wsq.reshape(3, D_MODEL, D_MODEL)

    (g_in, g_sb, g_hg, g_out, segs, dout, loss, d_bg, d_lb, d_hgn, d_fg) = _local_grads(
        x[0], loss_target[0], proj, h_t, qkv, b_gate, lb_logits.reshape(2, D_MODEL), hg_norm_g.reshape(1, D_MODEL),
        final_norm_g.reshape(1, D_MODEL), wsq[0], wsq[1], wsq[2])

    g_sq = jnp.stack([g_sb, g_hg, g_out]).reshape(3, N_CHIPS, ROW_SHARD, D_MODEL)
    got_in, got_sq = _swap_halves(g_in, g_sq)
    s_in, s_sq = _sum_a_in(idx, g_in, got_in), _sum_a_sq(idx, g_sq, got_sq)
    grad_x, d_ng, got_in, got_sq = _dx(segs, w_all, x[0], norm_g, dout, s_in, s_sq)
    grad_in, grad_sq = _join_halves(_sum_b_in(idx, s_in, got_in), _sum_b_sq(idx, s_sq, got_sq))

    d_in, nm_in, nv_in = _adamw(w_in[0], grad_in, m_w_in[0], v_w_in[0], "adamw_in")
    flat = lambda a, b, c: jnp.concatenate([a[0], b[0], c[0]], axis=0)
    d_sq, nm_sq, nv_sq = _adamw(flat(w_sb_proj, w_hg_proj, w_out), grad_sq.reshape(3 * ROW_SHARD, D_MODEL),
                                flat(m_w_sb_proj, m_w_hg_proj, m_w_out), flat(v_w_sb_proj, v_w_hg_proj, v_w_out),
                                "adamw_sq")

    pad = jnp.zeros((8, HEAD_DIM), F32).at[0, 0].set(loss[0, 0])
    part = jnp.concatenate([pad] + [a.reshape(-1, HEAD_DIM) for a in (d_ng, d_bg, d_lb, d_hgn, d_fg)], axis=0)
    sums = _sum_small(part)
    loss_out, g_sm, d_sm, nm_sm, nv_sm = _adamw_small(
        sums, _pack_small(norm_g, b_gate, lb_logits, hg_norm_g, final_norm_g),
        _pack_small(m_norm_g, m_b_gate, m_lb_logits, m_hg_norm_g, m_final_norm_g),
        _pack_small(v_norm_g, v_b_gate, v_lb_logits, v_hg_norm_g, v_final_norm_g))

    def big(t_in, t_sq):
        sq = t_sq.reshape(3, 1, ROW_SHARD, D_MODEL)
        return t_in[None], sq[0], sq[1], sq[2]

    def order(small, in_, sb, hg, out):
        ng, bg, lbl, hgn, fg = small
        return [ng, in_, bg, lbl, hgn, sb, hg, out, fg]

    outs = [loss_out[0, 0], grad_x[None]]
    for small, (t_in, t_sq) in ((g_sm, (grad_in, grad_sq)), (d_sm, (d_in, d_sq)), (nm_sm, (nm_in, nm_sq)), (nv_sm, (nv_in, nv_sq))):
        outs += order(_unpack_small(small), *big(t_in, t_sq))
    return tuple(outs)
```

```python
import functools

import jax
import jax.numpy as jnp
from jax import lax
from jax.experimental import pallas as pl
from jax.experimental.pallas import tpu as pltpu

F32 = jnp.float32
BF16 = jnp.bfloat16

D_MODEL = 1024
HEADS = 8
HEAD_DIM = 128
IN_WIDTH = 10240
N_CHIPS = 4
W_IN_SHARD = IN_WIDTH // N_CHIPS
ROW_SHARD = D_MODEL // N_CHIPS
RMS_EPS = 1e-6

OFF_SB_Q, OFF_SB_K, OFF_SB_V, OFF_SB_Z = 0, 1024, 2048, 3072
OFF_HG_Q, OFF_HG_F, OFF_HG_I, OFF_HG_Z, OFF_GATE = 4096, 5120, 6144, 7168, 8192

SB_BLOCK = 256
SB_FWD_HEADS = 4
SB_FWD_GROUPS = 2
SB_BWD_HEADS = 2
SB_BWD_GROUPS = 2
SB_DEAD = -110.0
SB_GONE = -1e30
HG_CHUNK = 32
HG_PAIR = 2 * HG_CHUNK
HG_STEP = 256
HG_MID = HG_CHUNK // 2 - 1

ADAM_LR, ADAM_B1, ADAM_B2, ADAM_EPS, ADAM_WD, ADAM_STEP = 0.001, 0.9, 0.999, 1e-08, 0.01, 10

VMEM_LIMIT = 56 * 1024 * 1024
VMEM_LIMIT_DX = 60 * 1024 * 1024

MESH = pl.DeviceIdType.MESH


def _cparams(sem, vmem=VMEM_LIMIT):
    return pltpu.CompilerParams(dimension_semantics=sem, vmem_limit_bytes=vmem)


def _dot(a, b):
    return jnp.dot(a, b, preferred_element_type=F32)


def _dot_nt(a, b):
    return lax.dot_general(a, b, (((1,), (1,)), ((), ())), preferred_element_type=F32)


def _dot_tn(a, b):
    return lax.dot_general(a, b, (((0,), (0,)), ((), ())), preferred_element_type=F32)


def _split_dot(x, tri):
    hi = x.astype(BF16)
    lo = (x - hi.astype(F32)).astype(BF16)
    both = _dot(jnp.concatenate([hi, lo], axis=0), tri)
    return both[: x.shape[0]] + both[x.shape[0] :]


def _split_dot_left(tri, x):
    hi = x.astype(BF16)
    lo = (x - hi.astype(F32)).astype(BF16)
    return _dot(tri, hi) + _dot(tri, lo)


def _sigmoid(x):
    return 1.0 / (1.0 + jnp.exp(-x))


def _prenorm(x, norm_g):
    s_len = x.shape[0]
    ts = min(1024, s_len)

    def body(x_ref, g_ref, h_ref, ht_ref):
        xv = x_ref[...]
        r = lax.rsqrt(jnp.mean(xv * xv, axis=-1, keepdims=True) + RMS_EPS)
        hv = (xv * r) * g_ref[...]
        h_ref[...] = hv.astype(BF16)
        ht_ref[...] = hv.T.astype(BF16)

    return pl.pallas_call(
        body,
        name="prenorm",
        grid=(s_len // ts,),
        in_specs=[pl.BlockSpec((ts, D_MODEL), lambda s: (s, 0)), pl.BlockSpec((1, D_MODEL), lambda s: (0, 0))],
        out_specs=[pl.BlockSpec((ts, D_MODEL), lambda s: (s, 0)), pl.BlockSpec((D_MODEL, ts), lambda s: (0, s))],
        out_shape=[jax.ShapeDtypeStruct((s_len, D_MODEL), BF16), jax.ShapeDtypeStruct((D_MODEL, s_len), BF16)],
        compiler_params=_cparams(("arbitrary",)),
    )(x, norm_g)


def _sb_scores(qb, kb, causal, tri_excl, diag):
    z = _dot_nt(qb, kb) * HEAD_DIM ** -0.5
    ls_pos = jnp.minimum(z, 0.0) - jnp.log1p(jnp.exp(-jnp.abs(z)))
    log_not = ls_pos - z
    log_not_m = jnp.where(causal, log_not, 0.0) if diag else log_not
    return ls_pos, log_not, log_not_m, _split_dot(log_not_m, tri_excl)


def _sb_weights(ls_pos, suffix, carry, causal, diag):
    surv = suffix + carry
    w = jnp.exp(ls_pos + surv)
    return surv, (jnp.where(causal, w, 0.0) if diag else w)


def _sb_specs(s_len, blk, heads):
    width = heads * HEAD_DIM

    def blk_spec(off):
        return pl.BlockSpec((blk, width), lambda h, i: (i, off // width + h))

    def head_spec(off, buffers=2):
        return pl.BlockSpec((s_len, width), lambda h, i: (0, off // width + h), pipeline_mode=pl.Buffered(buffers))

    return blk_spec, head_spec


def _head_cols(p):
    return slice(p * HEAD_DIM, (p + 1) * HEAD_DIM)


def _sb_masks(blk, rows):
    row = lax.broadcasted_iota(jnp.int32, (rows, blk), 0)
    col = lax.broadcasted_iota(jnp.int32, (rows, blk), 1)
    causal = [row + a * rows > col for a in range(blk // rows)]
    row = lax.broadcasted_iota(jnp.int32, (blk, blk), 0)
    col = lax.broadcasted_iota(jnp.int32, (blk, blk), 1)
    tri_excl = (row > col).astype(BF16)
    tri_incl = (row >= col).astype(BF16)
    return causal, tri_excl, tri_incl


def _sb_alive(st, n_chain):
    alive = functools.reduce(jnp.maximum, [st[1 + 3 * c] for c in range(n_chain)])
    return jnp.max(alive) > SB_DEAD


def _sb_fwd(qkv):
    s_len = qkv.shape[0]
    kb_rows = min(SB_BLOCK, s_len)
    groups = min(SB_FWD_GROUPS, s_len // kb_rows)
    blk = groups * kb_rows
    nq = s_len // blk
    rows = kb_rows
    chains = [(p, a) for p in range(SB_FWD_HEADS) for a in range(groups)]

    def body(q_ref, k_ref, v_ref, o_ref, of_ref):
        i = pl.program_id(1)
        masks, tri_excl, _ = _sb_masks(kb_rows, rows)
        causal = masks[0]

        def tiles(steps, st):
            pre = []
            for n, diag in steps:
                for p, a in chains:
                    j = groups * i + a - n
                    start = pl.multiple_of(jnp.maximum(j, 0) * kb_rows, kb_rows)
                    kb = k_ref[pl.ds(start, kb_rows), _head_cols(p)]
                    qb = q_ref[a * rows : (a + 1) * rows, _head_cols(p)]
                    pre.append(_sb_scores(qb, kb, causal, tri_excl, diag) + (v_ref[pl.ds(start, kb_rows), _head_cols(p)], j))
            for t, (_, diag) in enumerate(steps):
                new = []
                for c, (p, a) in enumerate(chains):
                    carry, acc, acc_lo = st[3 * c : 3 * c + 3]
                    ls_pos, _, log_not_m, suffix, vb, j = pre[t * len(chains) + c]
                    if not diag:
                        carry = jnp.where(j >= 0, carry, SB_GONE)
                    surv, w = _sb_weights(ls_pos, suffix, carry, causal, diag)
                    wb = w.astype(BF16)
                    w_lo = (w - wb.astype(F32)).astype(BF16)
                    both = _dot(jnp.concatenate([wb, w_lo], axis=0), vb)
                    new += [surv[:, 0:1] + log_not_m[:, 0:1], acc + both[:rows], acc_lo + both[rows:]]
                st = tuple(new)
            return st

        zero = jnp.zeros((rows, HEAD_DIM), F32)
        st = tiles([(0, True), (1, False)], (jnp.zeros((rows, 1), F32), zero, zero) * len(chains))

        def more(st):
            return (st[0] <= groups * i + groups - 1) & _sb_alive(st, len(chains))

        def step(st):
            return (st[0] + 1,) + tiles([(st[0], False)], st[1:])

        st = lax.while_loop(more, step, (2,) + st)[1:]
        for c, (p, a) in enumerate(chains):
            o_ref[a * rows : (a + 1) * rows, _head_cols(p)] = st[3 * c + 1]
            of_ref[a * rows : (a + 1) * rows, _head_cols(p)] = st[3 * c + 1] + st[3 * c + 2]

    blk_spec, head_spec = _sb_specs(s_len, blk, SB_FWD_HEADS)
    return pl.pallas_call(
        body,
        name="sb_fwd",
        grid=(HEADS // SB_FWD_HEADS, nq),
        in_specs=[blk_spec(OFF_SB_Q), head_spec(OFF_SB_K), head_spec(OFF_SB_V)],
        out_specs=[blk_spec(0), blk_spec(0)],
        out_shape=[jax.ShapeDtypeStruct((s_len, D_MODEL), F32)] * 2,
        compiler_params=_cparams(("arbitrary", "arbitrary")),
    )(qkv, qkv, qkv)


def _sb_bwd(qkv, o_fine, d_o):
    s_len = qkv.shape[0]
    kb_rows = min(SB_BLOCK, s_len)
    groups = min(SB_BWD_GROUPS, s_len // kb_rows)
    blk = groups * kb_rows
    nq = s_len // blk
    scale = HEAD_DIM ** -0.5
    rows = kb_rows
    chains = [(p, a) for p in range(SB_BWD_HEADS) for a in range(groups)]

    def body(q_ref, k_ref, v_ref, of_ref, do_ref, dq_ref, dk_ref, dv_ref, dk_acc, dv_acc):
        i = pl.program_id(1)

        @pl.when(i == 0)
        def _():
            dk_acc[...] = jnp.zeros_like(dk_acc)
            dv_acc[...] = jnp.zeros_like(dv_acc)

        dob = do_ref[...].astype(BF16)
        prod = dob.astype(F32) * of_ref[...]
        masks, tri_excl, tri_incl = _sb_masks(kb_rows, rows)
        causal = masks[0]

        def group(x, p, a):
            return x[a * rows : (a + 1) * rows, _head_cols(p)]

        totals = [jnp.sum(group(prod, p, a), axis=-1, keepdims=True) for p, a in chains]

        def tiles(steps, st):
            pre = []
            for n, diag in steps:
                for p, a in chains:
                    j = groups * i + a - n
                    start = pl.multiple_of(jnp.maximum(j, 0) * kb_rows, kb_rows)
                    kb = k_ref[pl.ds(start, kb_rows), _head_cols(p)]
                    vb = v_ref[pl.ds(start, kb_rows), _head_cols(p)]
                    qb, dob_c = group(q_ref, p, a), group(dob, p, a)
                    pre.append(_sb_scores(qb, kb, causal, tri_excl, diag) + (_dot_nt(dob_c, vb), qb, kb, dob_c, j, start))
            for t, (_, diag) in enumerate(steps):
                mids = []
                for c, (p, a) in enumerate(chains):
                    ls_pos, _, _, suffix, d_w = pre[t * len(chains) + c][:5]
                    c_not = st[3 * c]
                    if not diag:
                        c_not = jnp.where(pre[t * len(chains) + c][8] >= 0, c_not, SB_GONE)
                    surv, w = _sb_weights(ls_pos, suffix, c_not, causal, diag)
                    dlw = d_w * w
                    mids.append((surv, w, dlw, _split_dot(dlw, tri_incl)))
                new = []
                for c, (p, a) in enumerate(chains):
                    c_dlw, dq = st[3 * c + 1 : 3 * c + 3]
                    ls_pos, log_not, log_not_m, _, _, qb, kb, dob_c, j, start = pre[t * len(chains) + c]
                    surv, w, dlw, suffix = mids[c]
                    d_not = totals[c] - c_dlw - suffix
                    dz = ((dlw + d_not) * jnp.exp(log_not) - d_not) * scale
                    dz = jnp.where(causal, dz, 0.0) if diag else jnp.where(j >= 0, dz, 0.0)
                    dzb = dz.astype(BF16)
                    dk_acc[pl.ds(start, kb_rows), _head_cols(p)] += _dot_tn(dzb, qb)
                    dv_acc[pl.ds(start, kb_rows), _head_cols(p)] += _dot_tn(w.astype(BF16), dob_c)
                    new += [surv[:, 0:1] + log_not_m[:, 0:1], c_dlw + suffix[:, 0:1], dq + _dot(dzb, kb)]
                st = tuple(new)
            return st

        zcol = jnp.zeros((rows, 1), F32)
        st = tiles([(0, True), (1, False)], (zcol, zcol, jnp.zeros((rows, HEAD_DIM), F32)) * len(chains))

        def more(st):
            return (st[0] <= groups * i + groups - 1) & _sb_alive(st, len(chains))

        def step(st):
            return (st[0] + 1,) + tiles([(st[0], False)], st[1:])

        st = lax.while_loop(more, step, (2,) + st)[1:]
        for c, (p, a) in enumerate(chains):
            dq_ref[a * rows : (a + 1) * rows, _head_cols(p)] = st[3 * c + 2].astype(BF16)

        @pl.when(i == nq - 1)
        def _():
            dk_ref[...] = dk_acc[...].astype(BF16)
            dv_ref[...] = dv_acc[...].astype(BF16)

    blk_spec, head_spec = _sb_specs(s_len, blk, SB_BWD_HEADS)
    width = SB_BWD_HEADS * HEAD_DIM
    return pl.pallas_call(
        body,
        name="sb_bwd",
        grid=(HEADS // SB_BWD_HEADS, nq),
        in_specs=[blk_spec(OFF_SB_Q), head_spec(OFF_SB_K, 1), head_spec(OFF_SB_V, 1), blk_spec(0), blk_spec(0)],
        out_specs=[blk_spec(0), head_spec(0), head_spec(0)],
        out_shape=[jax.ShapeDtypeStruct((s_len, D_MODEL), BF16)] * 3,
        scratch_shapes=[pltpu.VMEM((s_len, width), F32), pltpu.VMEM((s_len, width), F32)],
        compiler_params=_cparams(("arbitrary", "arbitrary"), vmem=VMEM_LIMIT_DX),
    )(qkv, qkv, qkv, o_fine, d_o)


def _hg_lower_bound(lbl_ref):
    l0 = lbl_ref[0:1, :]
    l1 = lbl_ref[1:2, :]
    mx = jnp.maximum(l0, l1)
    e0 = jnp.exp(l0 - mx)
    e1 = jnp.exp(l1 - mx)
    return e0 / (e0 + e1)


def _hg_gates(hq, hf, lb):
    sig_f = _sigmoid(hf)
    f = lb + (1.0 - lb) * sig_f
    g = jnp.log(f)
    kk = 1.0 - f
    sig_q = _sigmoid(hq)
    qq = hq * sig_q
    return qq, kk, g, f, sig_f, sig_q


def _period_bcast(x, r, rows, period):
    w = x.shape[-1]
    x3 = x.reshape(rows // period, period, w)
    return jnp.broadcast_to(x3[:, r : r + 1, :], x3.shape).reshape(rows, w)


def _blockdiag(rows, kind):
    row = lax.broadcasted_iota(jnp.int32, (rows, rows), 0)
    col = lax.broadcasted_iota(jnp.int32, (rows, rows), 1)
    if kind in ("next", "prev"):
        first, second = (row, col) if kind == "next" else (col, row)
        keep = ((row // HG_PAIR) == (col // HG_PAIR)) & (first % HG_PAIR < HG_CHUNK) & (second % HG_PAIR >= HG_CHUNK)
    else:
        keep = (row // HG_CHUNK) == (col // HG_CHUNK)
        if kind == "lower":
            keep = keep & (row >= col)
        elif kind == "upper":
            keep = keep & (row <= col)
    return jnp.where(keep, 1.0, 0.0).astype(BF16)


def _hg_operands(hq, hf, lb, rows):
    qq, kk, g, f, sig_f, sig_q = _hg_gates(hq, hf, lb)
    cum = _split_dot_left(_blockdiag(rows, "lower"), g)
    mid = _period_bcast(cum, HG_MID, rows, HG_CHUNK)
    last = _period_bcast(cum, HG_CHUNK - 1, rows, HG_CHUNK)
    last0 = _period_bcast(cum, HG_CHUNK - 1, rows, HG_PAIR)
    last1 = _period_bcast(cum, HG_PAIR - 1, rows, HG_PAIR)
    second = (lax.broadcasted_iota(jnp.int32, cum.shape, 0) % HG_PAIR) >= HG_CHUNK
    e = dict(qm=jnp.exp(cum - mid), km=jnp.exp(mid - cum), qd=jnp.exp(cum), kl=jnp.exp(last - cum),
             q_in=jnp.where(second, jnp.exp(last0), 1.0), k_out=jnp.where(second, 1.0, jnp.exp(last1)),
             pair=jnp.exp(last0 + last1))
    v = dict(qm=qq * e["qm"], km=kk * e["km"], qd=qq * e["qd"], kl=kk * e["kl"])
    v["qp"] = v["qd"] * e["q_in"]
    v["kp"] = v["kl"] * e["k_out"]
    return v, e, second, (f, sig_f, sig_q)


def _hg_store_operands(v, second, hi, refs):
    zero = jnp.zeros_like(v["qm"])
    q_cat, k_cat, qp_b, kp_b, v_b = refs
    q_cat[:, 0:D_MODEL] = jnp.where(second, zero, v["qm"]).astype(BF16)
    q_cat[:, D_MODEL : 2 * D_MODEL] = jnp.where(second, v["qm"], zero).astype(BF16)
    q_cat[:, 2 * D_MODEL :] = jnp.where(second, v["qd"], zero).astype(BF16)
    k_cat[:, 0:D_MODEL] = jnp.where(second, zero, v["km"]).astype(BF16)
    k_cat[:, D_MODEL : 2 * D_MODEL] = jnp.where(second, v["km"], zero).astype(BF16)
    k_cat[:, 2 * D_MODEL :] = jnp.where(second, zero, v["kl"]).astype(BF16)
    qp_b[...] = v["qp"].astype(BF16)
    kp_b[...] = v["kp"].astype(BF16)
    v_b[...] = hi.astype(BF16)


def _hg_pair_operands(cat, r0, c0):
    return jnp.concatenate([cat[r0 : r0 + HG_PAIR, g * D_MODEL + c0 : g * D_MODEL + c0 + HEAD_DIM] for g in range(3)], axis=1)


def _hg_fwd(proj, lbl):
    s_len = proj.shape[0]
    rows = min(HG_STEP, s_len)
    n_pairs = rows // HG_PAIR

    def body(hq_ref, hf_ref, hi_ref, lbl_ref, o_ref, st_ref, state, q_cat, k_cat, qp_b, kp_b, v_b):
        @pl.when(pl.program_id(0) == 0)
        def _():
            state[...] = jnp.zeros_like(state)

        v, e, second, _ = _hg_operands(hq_ref[...], hf_ref[...], _hg_lower_bound(lbl_ref), rows)
        _hg_store_operands(v, second, hi_ref[...], (q_cat, k_cat, qp_b, kp_b, v_b))
        e_pair = e["pair"]
        row = lax.broadcasted_iota(jnp.int32, (HG_PAIR, HG_PAIR), 0)
        col = lax.broadcasted_iota(jnp.int32, (HG_PAIR, HG_PAIR), 1)
        causal = row >= col

        for u in range(n_pairs):
            r0 = u * HG_PAIR
            sls = [(slice(r0, r0 + HG_PAIR), slice(h * HEAD_DIM, (h + 1) * HEAD_DIM)) for h in range(HEADS)]
            a_s = [jnp.where(causal, _dot_nt(_hg_pair_operands(q_cat, r0, h * HEAD_DIM),
                                             _hg_pair_operands(k_cat, r0, h * HEAD_DIM)), 0.0).astype(BF16)
                   for h in range(HEADS)]
            st_s = [state[h] for h in range(HEADS)]
            for h, sl in enumerate(sls):
                st_ref[u, h] = st_s[h]
                state[h] = st_s[h] * e_pair[r0 : r0 + 1, sl[1]] + _dot_tn(v_b[sl], kp_b[sl])
            for h, sl in enumerate(sls):
                o_ref[sl] = _dot(a_s[h], v_b[sl]) + _dot_nt(qp_b[sl], st_s[h].astype(BF16))

    def col_spec(off):
        return pl.BlockSpec((rows, D_MODEL), lambda s: (s, off // D_MODEL))

    bf_tile = pltpu.VMEM((rows, D_MODEL), BF16)
    bf_cat = pltpu.VMEM((rows, 3 * D_MODEL), BF16)
    scratch = [pltpu.VMEM((HEADS, HEAD_DIM, HEAD_DIM), F32), bf_cat, bf_cat, bf_tile, bf_tile, bf_tile]
    return pl.pallas_call(
        body,
        name="hg_fwd",
        grid=(s_len // rows,),
        in_specs=[col_spec(OFF_HG_Q), col_spec(OFF_HG_F), col_spec(OFF_HG_I), pl.BlockSpec((2, D_MODEL), lambda s: (0, 0))],
        out_specs=[
            pl.BlockSpec((rows, D_MODEL), lambda s: (s, 0)),
            pl.BlockSpec((n_pairs, HEADS, HEAD_DIM, HEAD_DIM), lambda s: (s, 0, 0, 0)),
        ],
        out_shape=[
            jax.ShapeDtypeStruct((s_len, D_MODEL), F32),
            jax.ShapeDtypeStruct((s_len // HG_PAIR, HEADS, HEAD_DIM, HEAD_DIM), F32),
        ],
        scratch_shapes=scratch,
        compiler_params=_cparams(("arbitrary",)),
    )(proj, proj, proj, lbl)


def _hg_bwd(proj, lbl, states, d_o):
    s_len = proj.shape[0]
    rows = min(HG_STEP, s_len)
    n_pairs = rows // HG_PAIR
    n_steps = s_len // rows

    def body(hq_ref, hf_ref, hi_ref, lbl_ref, st_ref, do_ref, dp_ref, dlb_ref,
             dstate, q_cat, k_cat, qp_b, kp_b, v_b, do_b, d_qcat, d_kcat, d_qp, d_kp, d_v, d_pair):
        @pl.when(pl.program_id(0) == 0)
        def _():
            dstate[...] = jnp.zeros_like(dstate)
            dlb_ref[...] = jnp.zeros_like(dlb_ref)

        lb = _hg_lower_bound(lbl_ref)
        hq = hq_ref[...]
        v, e, second, (f, sig_f, sig_q) = _hg_operands(hq, hf_ref[...], lb, rows)
        _hg_store_operands(v, second, hi_ref[...], (q_cat, k_cat, qp_b, kp_b, v_b))
        do_b[...] = do_ref[...].astype(BF16)
        e_pair = e["pair"]
        row = lax.broadcasted_iota(jnp.int32, (HG_PAIR, HG_PAIR), 0)
        col = lax.broadcasted_iota(jnp.int32, (HG_PAIR, HG_PAIR), 1)
        causal = row >= col

        for u in reversed(range(n_pairs)):
            r0 = u * HG_PAIR
            sls = [(slice(r0, r0 + HG_PAIR), slice(h * HEAD_DIM, (h + 1) * HEAD_DIM)) for h in range(HEADS)]
            ops = [(_hg_pair_operands(q_cat, r0, h * HEAD_DIM), _hg_pair_operands(k_cat, r0, h * HEAD_DIM))
                   for h in range(HEADS)]
            a_s = [jnp.where(causal, _dot_nt(lhs, rhs), 0.0).astype(BF16) for lhs, rhs in ops]
            da_s = [jnp.where(causal, _dot_nt(do_b[sl], v_b[sl]), 0.0).astype(BF16) for sl in sls]
            st0_s = [st_ref[u, h] for h in range(HEADS)]
            ds1_s = [dstate[h] for h in range(HEADS)]
            ds1b_s = [ds1.astype(BF16) for ds1 in ds1_s]
            for h, sl in enumerate(sls):
                decay = e_pair[r0 : r0 + 1, sl[1]]
                d_pair[u : u + 1, sl[1]] = decay * jnp.sum(ds1_s[h] * st0_s[h], axis=0, keepdims=True)
                dstate[h] = ds1_s[h] * decay + _dot_tn(do_b[sl], qp_b[sl])
            for h, sl in enumerate(sls):
                d_qp[sl] = _dot(do_b[sl], st0_s[h].astype(BF16))
                d_kp[sl] = _dot(v_b[sl], ds1b_s[h])
            for h, sl in enumerate(sls):
                d_v[sl] = _dot_tn(a_s[h], do_b[sl]) + _dot_nt(kp_b[sl], ds1b_s[h])
            for h, sl in enumerate(sls):
                d_lhs = _dot(da_s[h], ops[h][1])
                d_rhs = _dot_tn(da_s[h], ops[h][0])
                for g in range(3):
                    gsl = (sl[0], slice(g * D_MODEL + h * HEAD_DIM, g * D_MODEL + (h + 1) * HEAD_DIM))
                    d_qcat[gsl] = d_lhs[:, g * HEAD_DIM : (g + 1) * HEAD_DIM]
                    d_kcat[gsl] = d_rhs[:, g * HEAD_DIM : (g + 1) * HEAD_DIM]

        zero = jnp.zeros_like(hq)
        dqm = jnp.where(second, d_qcat[:, D_MODEL : 2 * D_MODEL], d_qcat[:, 0:D_MODEL])
        dkm = jnp.where(second, d_kcat[:, D_MODEL : 2 * D_MODEL], d_kcat[:, 0:D_MODEL])
        dqp, dkp = d_qp[...], d_kp[...]
        dqd = dqp * e["q_in"] + jnp.where(second, d_qcat[:, 2 * D_MODEL :], zero)
        dkl = dkp * e["k_out"] + jnp.where(second, zero, d_kcat[:, 2 * D_MODEL :])
        dq = dqm * e["qm"] + dqd * e["qd"]
        dk = dkm * e["km"] + dkl * e["kl"]
        t_kl = dkl * v["kl"]
        dcum = dqm * v["qm"] - dkm * v["km"] + dqd * v["qd"] - t_kl
        dp = d_pair[...]
        dp_b = jnp.broadcast_to(dp[:, None, :], (n_pairs, HG_PAIR, D_MODEL)).reshape(rows, D_MODEL)
        dg = (_split_dot_left(_blockdiag(rows, "upper"), dcum) + _split_dot_left(_blockdiag(rows, "all"), t_kl)
              + _split_dot_left(_blockdiag(rows, "next"), dqp * v["qp"])
              + _split_dot_left(_blockdiag(rows, "prev"), dkp * v["kp"]) + dp_b)
        df = dg / f - dk
        one_m = 1.0 - sig_f
        dp_ref[:, 0:D_MODEL] = (dq * (sig_q * (1.0 + hq * (1.0 - sig_q)))).astype(BF16)
        dp_ref[:, D_MODEL : 2 * D_MODEL] = (df * (1.0 - lb) * sig_f * one_m).astype(BF16)
        dp_ref[:, 2 * D_MODEL : 3 * D_MODEL] = d_v[...].astype(BF16)
        dlb_ref[...] += jnp.sum(df * one_m, axis=0, keepdims=True)

    def col_spec(off):
        return pl.BlockSpec((rows, D_MODEL), lambda s: (n_steps - 1 - s, off // D_MODEL))

    f32_tile = pltpu.VMEM((rows, D_MODEL), F32)
    f32_cat = pltpu.VMEM((rows, 3 * D_MODEL), F32)
    bf_tile = pltpu.VMEM((rows, D_MODEL), BF16)
    bf_cat = pltpu.VMEM((rows, 3 * D_MODEL), BF16)
    scratch = [pltpu.VMEM((HEADS, HEAD_DIM, HEAD_DIM), F32), bf_cat, bf_cat, bf_tile, bf_tile, bf_tile, bf_tile,
               f32_cat, f32_cat, f32_tile, f32_tile, f32_tile, pltpu.VMEM((n_pairs, D_MODEL), F32)]
    return pl.pallas_call(
        body,
        name="hg_bwd",
        grid=(n_steps,),
        in_specs=[
            col_spec(OFF_HG_Q), col_spec(OFF_HG_F), col_spec(OFF_HG_I),
            pl.BlockSpec((2, D_MODEL), lambda s: (0, 0)),
            pl.BlockSpec((n_pairs, HEADS, HEAD_DIM, HEAD_DIM), lambda s: (n_steps - 1 - s, 0, 0, 0)),
            pl.BlockSpec((rows, D_MODEL), lambda s: (n_steps - 1 - s, 0)),
        ],
        out_specs=[
            pl.BlockSpec((rows, 3 * D_MODEL), lambda s: (n_steps - 1 - s, 0)),
            pl.BlockSpec((1, D_MODEL), lambda s: (0, 0)),
        ],
        out_shape=[
            jax.ShapeDtypeStruct((s_len, 3 * D_MODEL), BF16),
            jax.ShapeDtypeStruct((1, D_MODEL), F32),
        ],
        scratch_shapes=scratch,
        compiler_params=_cparams(("arbitrary",)),
    )(proj, proj, proj, lbl, states, d_o)


def _mid(proj, sb_o, hg_o, x, target, b_gate, hg_gain, final_g, w_sb, w_hg, w_out):
    s_len = proj.shape[0]
    ts = min(256, s_len)
    inv_d = 1.0 / D_MODEL

    def body(zsb_ref, hz_ref, gl_ref, sbo_ref, hgo_ref, x_ref, tgt_ref, bg_ref, hgn_ref, fg_ref,
             wsb_ref, whg_ref, wout_ref,
             dout_ref, dsbo_ref, dhgo_ref, dmid_ref,
             asb_ref, dusb_ref, ahg_ref, duhg_ref, y_ref, doutb_ref,
             loss_ref, dfg_ref, dbg_ref, dhgn_ref):
        @pl.when(pl.program_id(0) == 0)
        def _():
            loss_ref[...] = jnp.zeros_like(loss_ref)
            dfg_ref[...] = jnp.zeros_like(dfg_ref)
            dbg_ref[...] = jnp.zeros_like(dbg_ref)
            dhgn_ref[...] = jnp.zeros_like(dhgn_ref)

        z_sb = zsb_ref[...]
        sb_o = sbo_ref[...]
        sig_zsb = _sigmoid(z_sb)
        silu_zsb = z_sb * sig_zsb
        a_sb_f = sb_o * silu_zsb
        a_sb = a_sb_f.astype(BF16)
        u_sb = _dot(a_sb, wsb_ref[...])

        hg_o = hgo_ref[...]
        gain = hgn_ref[...]
        r_parts, yn_parts = [], []
        for h in range(HEADS):
            oh = hg_o[:, h * HEAD_DIM : (h + 1) * HEAD_DIM]
            r = lax.rsqrt(jnp.mean(oh * oh, axis=-1, keepdims=True) + RMS_EPS)
            r_parts.append(jnp.broadcast_to(r, oh.shape))
            yn_parts.append(oh * r)
        r_hg = jnp.concatenate(r_parts, axis=-1)
        yn_hg = jnp.concatenate(yn_parts, axis=-1)
        hn = yn_hg * gain
        hz = hz_ref[...]
        sig_hz = _sigmoid(hz)
        silu_hz = hz * sig_hz
        a_hg_f = hn * silu_hz
        a_hg = a_hg_f.astype(BF16)
        u_hg = _dot(a_hg, whg_ref[...])

        gates = _sigmoid(gl_ref[...] + bg_ref[...])
        g_sb = gates[:, 0:D_MODEL]
        g_hg = gates[:, D_MODEL:]
        y_f = g_sb * u_sb + g_hg * u_hg
        y = y_f.astype(BF16)
        out = x_ref[...] + _dot(y, wout_ref[...])
        r2 = lax.rsqrt(jnp.mean(out * out, axis=-1, keepdims=True) + RMS_EPS)
        yn = out * r2
        fg = fg_ref[...]
        diff = yn * fg - tgt_ref[...]
        loss_ref[...] += 0.5 * inv_d * jnp.sum(diff * diff)

        dyf = diff * inv_d
        dfg_ref[...] += jnp.sum(dyf * yn, axis=0, keepdims=True)
        dyn = dyf * fg
        dout = r2 * (dyn - yn * jnp.mean(dyn * yn, axis=-1, keepdims=True))
        dout_ref[...] = dout
        doutb = dout.astype(BF16)
        doutb_ref[...] = doutb
        dy = _dot_nt(doutb, wout_ref[...])
        du_sb = (dy * g_sb).astype(BF16)
        du_hg = (dy * g_hg).astype(BF16)
        dgl_sb = dy * u_sb * g_sb * (1.0 - g_sb)
        dgl_hg = dy * u_hg * g_hg * (1.0 - g_hg)
        dmid_ref[:, 2 * D_MODEL : 3 * D_MODEL] = dgl_sb.astype(BF16)
        dmid_ref[:, 3 * D_MODEL :] = dgl_hg.astype(BF16)
        dbg_ref[:, 0:D_MODEL] += jnp.sum(dgl_sb, axis=0, keepdims=True)
        dbg_ref[:, D_MODEL:] += jnp.sum(dgl_hg, axis=0, keepdims=True)

        da_sb = _dot_nt(du_sb, wsb_ref[...])
        dsbo_ref[...] = (da_sb * silu_zsb).astype(BF16)
        dmid_ref[:, 0:D_MODEL] = (da_sb * sb_o * (sig_zsb * (1.0 + z_sb * (1.0 - sig_zsb)))).astype(BF16)

        da_hg = _dot_nt(du_hg, whg_ref[...])
        dhn = da_hg * silu_hz
        dmid_ref[:, D_MODEL : 2 * D_MODEL] = (da_hg * hn * (sig_hz * (1.0 + hz * (1.0 - sig_hz)))).astype(BF16)
        dhgn_ref[...] += jnp.sum(dhn * yn_hg, axis=0, keepdims=True)
        dyn_hg = dhn * gain
        prod = dyn_hg * yn_hg
        m_parts = []
        for h in range(HEADS):
            ph = prod[:, h * HEAD_DIM : (h + 1) * HEAD_DIM]
            m_parts.append(jnp.broadcast_to(jnp.mean(ph, axis=-1, keepdims=True), ph.shape))
        dhgo_ref[...] = (r_hg * (dyn_hg - yn_hg * jnp.concatenate(m_parts, axis=-1))).astype(BF16)

        asb_ref[...] = a_sb_f.T.astype(BF16)
        dusb_ref[...] = du_sb
        ahg_ref[...] = a_hg_f.T.astype(BF16)
        duhg_ref[...] = du_hg
        y_ref[...] = y_f.T.astype(BF16)

    def tile(width, off=0):
        return pl.BlockSpec((ts, width), lambda s: (s, off // width))

    def across():
        return pl.BlockSpec((D_MODEL, ts), lambda s: (0, s))

    def whole(shape):
        return pl.BlockSpec(shape, lambda s: (0,) * len(shape))

    def weight():
        return pl.BlockSpec((D_MODEL, D_MODEL), lambda s: (0, 0), pipeline_mode=pl.Buffered(1))

    f32_act = jax.ShapeDtypeStruct((s_len, D_MODEL), F32)
    bf_act = jax.ShapeDtypeStruct((s_len, D_MODEL), BF16)
    bf_act_t = jax.ShapeDtypeStruct((D_MODEL, s_len), BF16)
    return pl.pallas_call(
        body,
        name="mid",
        grid=(s_len // ts,),
        in_specs=[
            tile(D_MODEL, OFF_SB_Z), tile(D_MODEL, OFF_HG_Z), tile(2 * D_MODEL, OFF_GATE),
            tile(D_MODEL), tile(D_MODEL), tile(D_MODEL), tile(D_MODEL),
            whole((1, 2 * D_MODEL)), whole((1, D_MODEL)), whole((1, D_MODEL)),
            weight(), weight(), weight(),
        ],
        out_specs=[
            tile(D_MODEL), tile(D_MODEL), tile(D_MODEL), tile(4 * D_MODEL),
            across(), tile(D_MODEL), across(), tile(D_MODEL), across(), tile(D_MODEL),
            whole((1, 1)), whole((1, D_MODEL)), whole((1, 2 * D_MODEL)), whole((1, D_MODEL)),
        ],
        out_shape=[
            f32_act, bf_act, bf_act, jax.ShapeDtypeStruct((s_len, 4 * D_MODEL), BF16),
            bf_act_t, bf_act, bf_act_t, bf_act, bf_act_t, bf_act,
            jax.ShapeDtypeStruct((1, 1), F32), jax.ShapeDtypeStruct((1, D_MODEL), F32),
            jax.ShapeDtypeStruct((1, 2 * D_MODEL), F32), jax.ShapeDtypeStruct((1, D_MODEL), F32),
        ],
        compiler_params=_cparams(("arbitrary",)),
    )(proj, proj, proj, sb_o, hg_o, x, target, b_gate, hg_gain, final_g, w_sb, w_hg, w_out)


def _grad_square(a_t, b, name):
    s_len = b.shape[0]
    tk = min(1024, s_len)

    def body(a_ref, b_ref, o_ref):
        @pl.when(pl.program_id(0) == 0)
        def _():
            o_ref[...] = jnp.zeros_like(o_ref)

        o_ref[...] += _dot(a_ref[...], b_ref[...])

    return pl.pallas_call(
        body,
        name=name,
        grid=(s_len // tk,),
        in_specs=[pl.BlockSpec((D_MODEL, tk), lambda k: (0, k)), pl.BlockSpec((tk, D_MODEL), lambda k: (k, 0))],
        out_specs=pl.BlockSpec((D_MODEL, D_MODEL), lambda k: (0, 0)),
        out_shape=jax.ShapeDtypeStruct((D_MODEL, D_MODEL), F32),
        compiler_params=_cparams(("arbitrary",)),
    )(a_t, b)


SEG_WIDTHS = (1024, 1024, 1024, 4096, 3072)
SEG_TILE = 1024
SEG_BOUNDS = (0, 1, 2, 3, 7, 10)


def _w_in_tile(k):
    return jnp.where(k < 4, k, jnp.where(k < 7, k + 3, k - 3))


def _grad_w_in(h_t, segs):
    m, s_len = h_t.shape
    tk = min(1024, s_len)
    tn = SEG_TILE
    nk = s_len // tk
    bounds = SEG_BOUNDS

    def body(a_ref, *refs):
        seg_refs, o_ref = refs[:-1], refs[-1]
        j = pl.program_id(0)

        @pl.when(pl.program_id(1) == 0)
        def _():
            o_ref[...] = jnp.zeros_like(o_ref)

        for i, ref in enumerate(seg_refs):
            @pl.when((j >= bounds[i]) & (j < bounds[i + 1]))
            def _(ref=ref):
                o_ref[...] += _dot(a_ref[...], ref[...])

    def seg_spec(lo, hi):
        def index(j, k):
            return (jnp.where(j < lo, 0, jnp.where(j >= hi, nk - 1, k)), jnp.clip(j - lo, 0, hi - lo - 1))
        return pl.BlockSpec((tk, tn), index)

    return pl.pallas_call(
        body,
        name="grad_w_in",
        grid=(IN_WIDTH // tn, nk),
        in_specs=[pl.BlockSpec((m, tk), lambda j, k: (0, k))]
        + [seg_spec(bounds[i], bounds[i + 1]) for i in range(len(SEG_WIDTHS))],
        out_specs=pl.BlockSpec((m, tn), lambda j, k: (0, _w_in_tile(j))),
        out_shape=jax.ShapeDtypeStruct((m, IN_WIDTH), F32),
        compiler_params=_cparams(("arbitrary", "arbitrary")),
    )(h_t, *segs)


EXCHANGE_IN_PIECES = 8
EXCHANGE_PIECES = EXCHANGE_IN_PIECES + 3


def _exchange_copies(sin_ref, ssq_ref, got_in, got_sq, send_sems, recv_sems):
    _, _, c, chips = _position()
    rows = HALF_IN // EXCHANGE_IN_PIECES
    copies = []
    for k, (px, py) in enumerate(chips):
        chip = 2 * px + py
        for p in range(EXCHANGE_PIECES):
            if p < EXCHANGE_IN_PIECES:
                src, dst = sin_ref.at[chip, pl.ds(p * rows, rows), :], got_in.at[k, pl.ds(p * rows, rows), :]
            else:
                src, dst = ssq_ref.at[p - EXCHANGE_IN_PIECES, chip], got_sq.at[k, p - EXCHANGE_IN_PIECES]
            copies.append(_remote(src, dst, send_sems.at[k, p], recv_sems.at[k, p], (px, py, c)))
    return copies


def _dx(segs, w_all, x, norm_g, dout, s_in, s_sq):
    s_len = x.shape[0]
    ts = min(1024, s_len)
    tk = SEG_TILE
    nk = IN_WIDTH // tk
    ns = s_len // ts
    bounds = SEG_BOUNDS
    n_seg = len(SEG_WIDTHS)

    def body(*refs):
        seg_refs = refs[:n_seg]
        w_ref, x_ref, g_ref, dout_ref, sin_ref, ssq_ref, gx_ref, dg_ref, got_in, got_sq, acc, send_sems, recv_sems = refs[n_seg:]
        s, k = pl.program_id(0), pl.program_id(1)

        @pl.when((s == 0) & (k == 0))
        def _():
            dg_ref[...] = jnp.zeros_like(dg_ref)

        copies = _exchange_copies(sin_ref, ssq_ref, got_in, got_sq, send_sems, recv_sems)
        per_step = -(-len(copies) // (ns * nk))
        for n in range(0, len(copies), per_step):
            @pl.when(s * nk + k == n // per_step)
            def _(n=n):
                for cp in copies[n : n + per_step]:
                    cp.start()

        @pl.when(k == 0)
        def _():
            acc[...] = jnp.zeros_like(acc)

        for i, ref in enumerate(seg_refs):
            @pl.when((k >= bounds[i]) & (k < bounds[i + 1]))
            def _(ref=ref):
                acc[...] += _dot_nt(ref[...], w_ref[...])

        @pl.when(k == nk - 1)
        def _():
            dh = acc[...]
            xv = x_ref[...]
            r = lax.rsqrt(jnp.mean(xv * xv, axis=-1, keepdims=True) + RMS_EPS)
            xn = xv * r
            dg_ref[...] += jnp.sum(dh * xn, axis=0, keepdims=True)
            dxn = dh * g_ref[...]
            gx_ref[...] = r * (dxn - xn * jnp.mean(dxn * xn, axis=-1, keepdims=True)) + dout_ref[...]

        @pl.when((s == ns - 1) & (k == nk - 1))
        def _():
            for cp in _exchange_copies(sin_ref, ssq_ref, got_in, got_sq, send_sems, recv_sems):
                cp.wait()

    def seg_spec(lo, hi):
        return pl.BlockSpec((ts, tk), lambda s, k: (s, jnp.clip(k - lo, 0, hi - lo - 1)))

    row_tile = pl.BlockSpec((ts, D_MODEL), lambda s, k: (s, 0))
    vec = pl.BlockSpec((1, D_MODEL), lambda s, k: (0, 0))
    return pl.pallas_call(
        body,
        name="dx",
        grid=(ns, nk),
        in_specs=[seg_spec(bounds[i], bounds[i + 1]) for i in range(n_seg)] + [
            pl.BlockSpec((D_MODEL, tk), lambda s, k: (0, _w_in_tile(k))),
            row_tile, vec, row_tile, ANY, ANY,
        ],
        out_specs=[row_tile, vec, ANY, ANY],
        out_shape=[jax.ShapeDtypeStruct((s_len, D_MODEL), F32), jax.ShapeDtypeStruct((1, D_MODEL), F32),
                   jax.ShapeDtypeStruct((3, HALF_IN, W_IN_SHARD), WIRE),
                   jax.ShapeDtypeStruct((3, 3, HALF_SQ, D_MODEL), WIRE)],
        scratch_shapes=[pltpu.VMEM((ts, D_MODEL), F32),
                        pltpu.SemaphoreType.DMA((3, EXCHANGE_PIECES)), pltpu.SemaphoreType.DMA((3, EXCHANGE_PIECES))],
        compiler_params=_cparams(("arbitrary", "arbitrary"), vmem=VMEM_LIMIT_DX),
    )(*segs, w_all, x, norm_g, dout, s_in, s_sq)


def _local_grads(x, target, proj, h_t, qkv, b_gate, lbl, hg_gain, final_g, w_sb, w_hg, w_out):
    sb_o, sb_o_fine = _sb_fwd(qkv)
    hg_o, states = _hg_fwd(proj, lbl)
    (dout, d_sbo, d_hgo, d_mid, a_sb, du_sb, a_hg, du_hg, y, doutb,
     loss, d_fg, d_bg, d_hgn) = _mid(proj, sb_o, hg_o, x, target, b_gate, hg_gain, final_g, w_sb, w_hg, w_out)
    g_w_sb = _grad_square(a_sb, du_sb, "grad_w_sb")
    g_w_hg = _grad_square(a_hg, du_hg, "grad_w_hg")
    g_w_out = _grad_square(y, doutb, "grad_w_out")
    d_q, d_k, d_v = _sb_bwd(qkv, sb_o_fine, d_sbo)
    d_hg, d_lb = _hg_bwd(proj, lbl, states, d_hgo)
    segs = (d_q, d_k, d_v, d_mid, d_hg)
    g_w_in = _grad_w_in(h_t, segs)
    return g_w_in, g_w_sb, g_w_hg, g_w_out, segs, dout, loss, d_bg, d_lb, d_hgn, d_fg


ANY = pl.BlockSpec(memory_space=pl.ANY)
WIRE = BF16
HALF_IN = D_MODEL // 2
HALF_SQ = ROW_SHARD // 2


def _position():
    x, y, c = lax.axis_index("x"), lax.axis_index("y"), lax.axis_index("c")
    chips = [(1 - x, y), (x, 1 - y), (1 - x, 1 - y)]
    return x, y, c, chips


def _remote(src, dst, send_sem, recv_sem, to):
    return pltpu.make_async_remote_copy(src_ref=src, dst_ref=dst, send_sem=send_sem, recv_sem=recv_sem,
                                        device_id=to, device_id_type=MESH)


PROJ_TILE = 1280
F32_FROM_TILE = 2
BF16_TO_TILE = 2
W_LOAD_PIECES = 8


def _gather_inproj(idx, h, w_in_b, w_sq_b):
    s_len = h.shape[0]
    ts = min(1024, s_len)
    ns = s_len // ts
    per = W_IN_SHARD // PROJ_TILE
    n_in = 4
    n_piece = n_in + 3
    rows = HALF_IN // n_in

    def chip_at(r, me):
        return me ^ jnp.where(r == 1, 2, jnp.where(r == 2, 1, jnp.where(r == 3, 3, 0)))

    def body(idx_ref, h_ref, win_ref, wsqb_ref, proj_ref, qkv_ref, wall_ref, wsq_ref, wbuf, send_sems, recv_sems, w_sems):
        r, t, s = pl.program_id(0), pl.program_id(1), pl.program_id(2)
        x, y, c, chips = _position()
        me = 2 * x + y
        sibling = (x, y, 1 - c)
        first = (t == 0) & (s == 0)

        def src_piece(p):
            if p < n_in:
                return win_ref.at[pl.ds(c * HALF_IN + p * rows, rows), :]
            return wsqb_ref.at[p - n_in, pl.ds(c * HALF_SQ, HALF_SQ), :]

        def piece(p, chip, core):
            if p < n_in:
                cols = pl.ds(pl.multiple_of(chip * W_IN_SHARD, W_IN_SHARD), W_IN_SHARD)
                return wall_ref.at[pl.ds(core * HALF_IN + p * rows, rows), cols]
            return wsq_ref.at[p - n_in, chip, pl.ds(core * HALF_SQ, HALF_SQ), :]

        def send(k, p):
            px, py = chips[k]
            return _remote(src_piece(p), piece(p, me, c), send_sems.at[k, p], recv_sems.at[k, p], (px, py, c))

        def forward(k, p, core):
            px, py = chips[k]
            got = piece(p, 2 * px + py, core)
            return _remote(got, got, send_sems.at[3 + k, p], recv_sems.at[3 + k, p], sibling)

        @pl.when((r == 0) & first)
        def _():
            for k in range(2):
                for p in range(n_piece):
                    send(k, p).start()

        for k in range(3):
            @pl.when((r == k + 1) & first)
            def _(k=k):
                px, py = chips[k]
                for p in range(n_piece):
                    got = piece(p, 2 * px + py, c)
                    _remote(got, got, send_sems.at[k, p], recv_sems.at[k, p], (px, py, c)).wait_recv()
                    forward(k, p, c).start()
                if k == 0:
                    for p in range(n_piece):
                        send(2, p).start()
                for p in range(n_piece):
                    forward(k, p, 1 - c).wait_recv()

        def tile_loads(slot, own):
            col = slot * PROJ_TILE
            if not own:
                col = pl.multiple_of(chip_at(r, me) * W_IN_SHARD + col, PROJ_TILE)
            src = win_ref if own else wall_ref
            part = D_MODEL // W_LOAD_PIECES
            return [pltpu.make_async_copy(src.at[pl.ds(q * part, part), pl.ds(col, PROJ_TILE)],
                                          wbuf.at[slot, pl.ds(q * part, part), :], w_sems.at[slot, q])
                    for q in range(W_LOAD_PIECES)]

        for own in (True, False):
            @pl.when(first & ((r == 0) if own else (r > 0)))
            def _(own=own):
                for slot in range(per):
                    for cp in tile_loads(slot, own):
                        cp.start()
                for cp in tile_loads(0, own):
                    cp.wait()

            @pl.when((t > 0) & (s == 0) & ((r == 0) if own else (r > 0)))
            def _(own=own):
                for cp in tile_loads(1, own):
                    cp.wait()

        tile_now = per * chip_at(r, me) + t
        want_f32, want_bf16 = tile_now >= F32_FROM_TILE, tile_now <= BF16_TO_TILE

        @pl.when(want_f32 & jnp.logical_not(want_bf16))
        def _():
            proj_ref[...] = _dot(h_ref[...], wbuf[t])

        @pl.when(want_bf16 & jnp.logical_not(want_f32))
        def _():
            qkv_ref[...] = _dot(h_ref[...], wbuf[t]).astype(BF16)

        @pl.when(want_f32 & want_bf16)
        def _():
            p = _dot(h_ref[...], wbuf[t])
            proj_ref[...] = p
            qkv_ref[...] = p.astype(BF16)

        @pl.when((r == 3) & (t == per - 1) & (s == ns - 1))
        def _():
            for k in range(3):
                for p in range(n_piece):
                    send(k, p).wait_send()
                    forward(k, p, c).wait_send()

    def out_index(wanted):
        order = [0, 2, 1, 3]
        table = []
        for chip in range(N_CHIPS):
            tiles = [per * (chip ^ order[q // per]) + q % per for q in range(N_CHIPS * per)]
            row = []
            for q, tile in enumerate(tiles):
                if wanted(tile):
                    row.append((tile, None))
                    continue
                before = [u for u in tiles[:q] if wanted(u)]
                after = [u for u in tiles[q:] if wanted(u)]
                row.append((before[-1], ns - 1) if before else (after[0], 0))
            table.append(row)

        def index(r, t, s, idx):
            q = r * per + t
            col, fixed_s = jnp.int32(0), jnp.int32(-1)
            for chip in range(N_CHIPS):
                for pos, (tile, hold) in enumerate(table[chip]):
                    here = (idx[0] == chip) & (q == pos)
                    col = jnp.where(here, tile, col)
                    fixed_s = jnp.where(here, -1 if hold is None else hold, fixed_s)
            return jnp.where(fixed_s < 0, s, fixed_s), col

        return index

    grid_spec = pltpu.PrefetchScalarGridSpec(
        num_scalar_prefetch=1,
        grid=(N_CHIPS, per, ns),
        in_specs=[pl.BlockSpec((ts, D_MODEL), lambda r, t, s, idx: (s, 0)), ANY, ANY],
        out_specs=[pl.BlockSpec((ts, PROJ_TILE), out_index(lambda tile: tile >= F32_FROM_TILE)),
                   pl.BlockSpec((ts, PROJ_TILE), out_index(lambda tile: tile <= BF16_TO_TILE)),
                   ANY, ANY],
        scratch_shapes=[pltpu.VMEM((per, D_MODEL, PROJ_TILE), BF16),
                        pltpu.SemaphoreType.DMA((6, n_piece)), pltpu.SemaphoreType.DMA((6, n_piece)),
                        pltpu.SemaphoreType.DMA((per, W_LOAD_PIECES))],
    )
    return pl.pallas_call(
        body,
        name="gather_inproj",
        grid_spec=grid_spec,
        out_shape=[jax.ShapeDtypeStruct((s_len, IN_WIDTH), F32),
                   jax.ShapeDtypeStruct((s_len, IN_WIDTH), BF16),
                   jax.ShapeDtypeStruct((D_MODEL, IN_WIDTH), BF16),
                   jax.ShapeDtypeStruct((3, N_CHIPS, ROW_SHARD, D_MODEL), BF16)],
        compiler_params=_cparams(("arbitrary", "arbitrary", "arbitrary")),
    )(idx, h, w_in_b, w_sq_b)


def _place_own(idx, w_in_b, w_sq_b, w_all, wsq):
    n = 4
    r_in, r_sq = D_MODEL // n, ROW_SHARD // n

    def body(idx_ref, win_ref, wsq_ref, w_all_in, wsq_in, w_all_out, wsq_out):
        w_all_out[...] = win_ref[...]
        wsq_out[:, 0] = wsq_ref[...]

    grid_spec = pltpu.PrefetchScalarGridSpec(
        num_scalar_prefetch=1,
        grid=(n,),
        in_specs=[pl.BlockSpec((r_in, W_IN_SHARD), lambda r, idx: (r, 0)),
                  pl.BlockSpec((3, r_sq, D_MODEL), lambda r, idx: (0, r, 0)), ANY, ANY],
        out_specs=[pl.BlockSpec((r_in, W_IN_SHARD), lambda r, idx: (r, idx[0])),
                   pl.BlockSpec((3, 1, r_sq, D_MODEL), lambda r, idx: (0, idx[0], r, 0))],
    )
    return pl.pallas_call(
        body,
        name="place_own",
        grid_spec=grid_spec,
        out_shape=[jax.ShapeDtypeStruct(w_all.shape, BF16), jax.ShapeDtypeStruct(wsq.shape, BF16)],
        input_output_aliases={3: 0, 4: 1},
        compiler_params=_cparams(("arbitrary",)),
    )(idx, w_in_b, w_sq_b, w_all, wsq)


def _swap_halves(g_in, g_sq):
    n_in = 16
    n_piece = n_in + 3 * N_CHIPS
    rows = HALF_IN // n_in

    def body(gin_ref, gsq_ref, got_in, got_sq, send_sems, recv_sems):
        x, y, c, _ = _position()
        sibling = (x, y, 1 - c)

        def src_piece(p):
            if p < n_in:
                return gin_ref.at[pl.ds((1 - c) * HALF_IN + p * rows, rows), :]
            a, chip = divmod(p - n_in, N_CHIPS)
            return gsq_ref.at[a, chip, pl.ds((1 - c) * HALF_SQ, HALF_SQ), :]

        def dst_piece(p):
            if p < n_in:
                return got_in.at[pl.ds(p * rows, rows), :]
            a, chip = divmod(p - n_in, N_CHIPS)
            return got_sq.at[a, chip]

        out = [_remote(src_piece(p), dst_piece(p), send_sems.at[p], recv_sems.at[p], sibling) for p in range(n_piece)]
        for cp in out:
            cp.start()
        for cp in out:
            cp.wait()

    return pl.pallas_call(
        body,
        name="swap_halves",
        in_specs=[ANY, ANY],
        out_specs=[ANY, ANY],
        out_shape=[jax.ShapeDtypeStruct((HALF_IN, IN_WIDTH), F32),
                   jax.ShapeDtypeStruct((3, N_CHIPS, HALF_SQ, D_MODEL), F32)],
        scratch_shapes=[pltpu.SemaphoreType.DMA((n_piece,))] * 2,
    )(g_in, g_sq)


def _join_halves(r_in, r_sq):
    n_in = 16
    n_piece = n_in + 3
    rows = HALF_IN // n_in

    def body(in_alias, sq_alias, full_in, full_sq, send_sems, recv_sems):
        del in_alias, sq_alias
        x, y, c, _ = _position()
        sibling = (x, y, 1 - c)

        def piece(p, core):
            if p < n_in:
                return full_in.at[pl.ds(core * HALF_IN + p * rows, rows), :]
            return full_sq.at[p - n_in, pl.ds(core * HALF_SQ, HALF_SQ), :]

        out = [_remote(piece(p, c), piece(p, c), send_sems.at[p], recv_sems.at[p], sibling) for p in range(n_piece)]
        for cp in out:
            cp.start()
        for p in range(n_piece):
            _remote(piece(p, 1 - c), piece(p, 1 - c), send_sems.at[p], recv_sems.at[p], sibling).wait_recv()
        for cp in out:
            cp.wait_send()

    return pl.pallas_call(
        body,
        name="join_halves",
        in_specs=[ANY, ANY],
        out_specs=[ANY, ANY],
        out_shape=[jax.ShapeDtypeStruct((D_MODEL, W_IN_SHARD), F32),
                   jax.ShapeDtypeStruct((3, ROW_SHARD, D_MODEL), F32)],
        input_output_aliases={0: 0, 1: 1},
        scratch_shapes=[pltpu.SemaphoreType.DMA((n_piece,)), pltpu.SemaphoreType.DMA((n_piece,))],
    )(r_in, r_sq)


SMALL_ROWS = 56
N_DEV = 8


def _sum_small(part):
    def body(part_ref, out_ref, slots, send_sems, recv_sems):
        x, y, c, _ = _position()
        me = 4 * x + 2 * y + c
        slots[me] = part_ref[...]
        out = []
        for r in range(1, N_DEV):
            rx, ry, rc = (r >> 2) & 1, (r >> 1) & 1, r & 1
            to = (1 - x if rx else x, 1 - y if ry else y, 1 - c if rc else c)
            out.append(_remote(part_ref, slots.at[me], send_sems.at[r - 1], recv_sems.at[r - 1], to))
        for cp in out:
            cp.start()
        for r in range(1, N_DEV):
            _remote(part_ref, slots.at[me ^ r], send_sems.at[r - 1], recv_sems.at[r - 1], (x, y, c)).wait_recv()
        for cp in out:
            cp.wait_send()
        total = slots[0]
        for d in range(1, N_DEV):
            total = total + slots[d]
        out_ref[...] = total

    vmem = pl.BlockSpec(memory_space=pltpu.VMEM)
    return pl.pallas_call(
        body,
        name="sum_small",
        in_specs=[vmem],
        out_specs=vmem,
        out_shape=jax.ShapeDtypeStruct((SMALL_ROWS, HEAD_DIM), F32),
        scratch_shapes=[pltpu.VMEM((N_DEV, SMALL_ROWS, HEAD_DIM), F32),
                        pltpu.SemaphoreType.DMA((N_DEV - 1,)), pltpu.SemaphoreType.DMA((N_DEV - 1,))],
    )(part)


def _prefetch_call(body, name, idx, grid, in_specs, out_specs, out_shape, args):
    grid_spec = pltpu.PrefetchScalarGridSpec(num_scalar_prefetch=1, grid=grid, in_specs=in_specs, out_specs=out_specs)
    return pl.pallas_call(body, name=name, grid_spec=grid_spec, out_shape=out_shape,
                          compiler_params=_cparams(("arbitrary",) * len(grid)))(idx, *args)


def _sum_a_in(idx, g_in, got_in):
    tr = 128
    nr = HALF_IN // tr

    def body(idx_ref, a_ref, b_ref, o_ref):
        o_ref[0] = (a_ref[...] + b_ref[...]).astype(WIRE)

    return _prefetch_call(
        body, "sum_a_in", idx, (N_CHIPS, nr),
        [pl.BlockSpec((tr, W_IN_SHARD), lambda j, r, idx: (idx[1] * nr + r, j)),
         pl.BlockSpec((tr, W_IN_SHARD), lambda j, r, idx: (r, j))],
        pl.BlockSpec((1, tr, W_IN_SHARD), lambda j, r, idx: (j, r, 0)),
        jax.ShapeDtypeStruct((N_CHIPS, HALF_IN, W_IN_SHARD), WIRE), (g_in, got_in))


def _sum_a_sq(idx, g_sq, got_sq):
    blk = (1, 1, HALF_SQ, D_MODEL)

    def body(idx_ref, a_ref, b_ref, o_ref):
        o_ref[...] = (a_ref[...] + b_ref[...]).astype(WIRE)

    return _prefetch_call(
        body, "sum_a_sq", idx, (3, N_CHIPS),
        [pl.BlockSpec(blk, lambda a, j, idx: (a, j, idx[1], 0)), pl.BlockSpec(blk, lambda a, j, idx: (a, j, 0, 0))],
        pl.BlockSpec(blk, lambda a, j, idx: (a, j, 0, 0)),
        jax.ShapeDtypeStruct((3, N_CHIPS, HALF_SQ, D_MODEL), WIRE), (g_sq, got_sq))


def _sum_b_in(idx, s_in, got_in):
    tr = 128
    nr = HALF_IN // tr

    def body(idx_ref, a_ref, b_ref, o_ref):
        o_ref[...] = ((a_ref[0].astype(F32) + b_ref[0].astype(F32)) + b_ref[1].astype(F32)) + b_ref[2].astype(F32)

    return _prefetch_call(
        body, "sum_b_in", idx, (nr,),
        [pl.BlockSpec((1, tr, W_IN_SHARD), lambda r, idx: (idx[0], r, 0)),
         pl.BlockSpec((3, tr, W_IN_SHARD), lambda r, idx: (0, r, 0))],
        pl.BlockSpec((tr, W_IN_SHARD), lambda r, idx: (idx[1] * nr + r, 0)),
        jax.ShapeDtypeStruct((D_MODEL, W_IN_SHARD), F32), (s_in, got_in))


def _sum_b_sq(idx, s_sq, got_sq):
    def body(idx_ref, a_ref, b_ref, o_ref):
        o_ref[0] = ((a_ref[0, 0].astype(F32) + b_ref[0, 0].astype(F32)) + b_ref[1, 0].astype(F32)) + b_ref[2, 0].astype(F32)

    return _prefetch_call(
        body, "sum_b_sq", idx, (3,),
        [pl.BlockSpec((1, 1, HALF_SQ, D_MODEL), lambda a, idx: (a, idx[0], 0, 0)),
         pl.BlockSpec((3, 1, HALF_SQ, D_MODEL), lambda a, idx: (0, a, 0, 0))],
        pl.BlockSpec((1, HALF_SQ, D_MODEL), lambda a, idx: (a, idx[1], 0)),
        jax.ShapeDtypeStruct((3, ROW_SHARD, D_MODEL), F32), (s_sq, got_sq))


def _adamw_math(w, g, m, v):
    m = ADAM_B1 * m + (1.0 - ADAM_B1) * g
    v = ADAM_B2 * v + (1.0 - ADAM_B2) * (g * g)
    m_hat = m / (1.0 - ADAM_B1 ** ADAM_STEP)
    v_hat = v / (1.0 - ADAM_B2 ** ADAM_STEP)
    delta = -ADAM_LR * (m_hat / (jnp.sqrt(v_hat) + ADAM_EPS) + ADAM_WD * w)
    return delta, m, v


def _adamw(w, g, m, v, name):
    rows, cols = w.shape
    tr = min(128, rows)

    def body(w_ref, g_ref, m_ref, v_ref, d_ref, nm_ref, nv_ref):
        d_ref[...], nm_ref[...], nv_ref[...] = _adamw_math(w_ref[...], g_ref[...], m_ref[...], v_ref[...])

    spec = pl.BlockSpec((tr, cols), lambda r: (r, 0))
    return pl.pallas_call(
        body,
        name=name,
        grid=(rows // tr,),
        in_specs=[spec] * 4,
        out_specs=[spec] * 3,
        out_shape=[jax.ShapeDtypeStruct((rows, cols), F32)] * 3,
        compiler_params=_cparams(("arbitrary",)),
    )(w, g, m, v)


def _adamw_small(sums, w, m, v):
    def body(s_ref, w_ref, m_ref, v_ref, loss_ref, g_ref, d_ref, nm_ref, nv_ref):
        s = s_ref[...]
        w = w_ref[...]
        loss_ref[...] = s[0:1, 0:1]
        l0, l1 = w[24:32], w[32:40]
        mx = jnp.maximum(l0, l1)
        e0, e1 = jnp.exp(l0 - mx), jnp.exp(l1 - mx)
        p0, p1 = e0 / (e0 + e1), e1 / (e0 + e1)
        d_lb = s[32:40]
        g = jnp.concatenate([s[8:16], s[16:32], d_lb * p0 * (1.0 - p0), -d_lb * p0 * p1, s[40:48], s[48:56]], axis=0)
        g_ref[...] = g
        d_ref[...], nm_ref[...], nv_ref[...] = _adamw_math(w, g, m_ref[...], v_ref[...])

    packed = jax.ShapeDtypeStruct((SMALL_ROWS, HEAD_DIM), F32)
    return pl.pallas_call(
        body,
        name="adamw_small",
        out_shape=[jax.ShapeDtypeStruct((1, 1), F32), packed, packed, packed, packed],
    )(sums, w, m, v)


def _pack_small(ng, bg, lbl, hgn, fg):
    return jnp.concatenate([a.reshape(-1, HEAD_DIM) for a in (ng, bg, lbl, hgn, fg)], axis=0)


def _unpack_small(p):
    return (p[0:8].reshape(1, D_MODEL), p[8:24].reshape(1, 2 * D_MODEL), p[24:40].reshape(2, HEADS, HEAD_DIM),
            p[40:48].reshape(1, HEADS, HEAD_DIM), p[48:56].reshape(D_MODEL))


def kernel(x, norm_g, w_in, b_gate, lb_logits, hg_norm_g, w_sb_proj, w_hg_proj, w_out, final_norm_g, loss_target, m_norm_g, m_w_in, m_b_gate, m_lb_logits, m_hg_norm_g, m_w_sb_proj, m_w_hg_proj, m_w_out, m_final_norm_g, v_norm_g, v_w_in, v_b_gate, v_lb_logits, v_hg_norm_g, v_w_sb_proj, v_w_hg_proj, v_w_out, v_final_norm_g):
    s_len = x.shape[1]
    w_sq = jnp.stack([w_sb_proj[0], w_hg_proj[0], w_out[0]])
    idx = jnp.stack([2 * lax.axis_index("x") + lax.axis_index("y"), lax.axis_index("c")]).astype(jnp.int32)
    w_in_b, w_sq_b = w_in[0].astype(BF16), w_sq.astype(BF16)
    h, h_t = _prenorm(x[0], norm_g)
    proj, qkv, w_all, wsq = _gather_inproj(idx, h, w_in_b, w_sq_b)
    w_all, wsq = _place_own(idx, w_in_b, w_sq_b, w_all, wsq)
    wsq = wsq.reshape(3, D_MODEL, D_MODEL)

    (g_in, g_sb, g_hg, g_out, segs, dout, loss, d_bg, d_lb, d_hgn, d_fg) = _local_grads(
        x[0], loss_target[0], proj, h_t, qkv, b_gate, lb_logits.reshape(2, D_MODEL), hg_norm_g.reshape(1, D_MODEL),
        final_norm_g.reshape(1, D_MODEL), wsq[0], wsq[1], wsq[2])

    g_sq = jnp.stack([g_sb, g_hg, g_out]).reshape(3, N_CHIPS, ROW_SHARD, D_MODEL)
    got_in, got_sq = _swap_halves(g_in, g_sq)
    s_in, s_sq = _sum_a_in(idx, g_in, got_in), _sum_a_sq(idx, g_sq, got_sq)
    grad_x, d_ng, got_in, got_sq = _dx(segs, w_all, x[0], norm_g, dout, s_in, s_sq)
    grad_in, grad_sq = _join_halves(_sum_b_in(idx, s_in, got_in), _sum_b_sq(idx, s_sq, got_sq))

    d_in, nm_in, nv_in = _adamw(w_in[0], grad_in, m_w_in[0], v_w_in[0], "adamw_in")
    flat = lambda a, b, c: jnp.concatenate([a[0], b[0], c[0]], axis=0)
    d_sq, nm_sq, nv_sq = _adamw(flat(w_sb_proj, w_hg_proj, w_out), grad_sq.reshape(3 * ROW_SHARD, D_MODEL),
                                flat(m_w_sb_proj, m_w_hg_proj, m_w_out), flat(v_w_sb_proj, v_w_hg_proj, v_w_out),
                                "adamw_sq")

    pad = jnp.zeros((8, HEAD_DIM), F32).at[0, 0].set(loss[0, 0])
    part = jnp.concatenate([pad] + [a.reshape(-1, HEAD_DIM) for a in (d_ng, d_bg, d_lb, d_hgn, d_fg)], axis=0)
    sums = _sum_small(part)
    loss_out, g_sm, d_sm, nm_sm, nv_sm = _adamw_small(
        sums, _pack_small(norm_g, b_gate, lb_logits, hg_norm_g, final_norm_g),
        _pack_small(m_norm_g, m_b_gate, m_lb_logits, m_hg_norm_g, m_final_norm_g),
        _pack_small(v_norm_g, v_b_gate, v_lb_logits, v_hg_norm_g, v_final_norm_g))

    def big(t_in, t_sq):
        sq = t_sq.reshape(3, 1, ROW_SHARD, D_MODEL)
        return t_in[None], sq[0], sq[1], sq[2]

    def order(small, in_, sb, hg, out):
        ng, bg, lbl, hgn, fg = small
        return [ng, in_, bg, lbl, hgn, sb, hg, out, fg]

    outs = [loss_out[0, 0], grad_x[None]]
    for small, (t_in, t_sq) in ((g_sm, (grad_in, grad_sq)), (d_sm, (d_in, d_sq)), (nm_sm, (nm_in, nm_sq)), (nv_sm, (nv_in, nv_sq))):
        outs += order(_unpack_small(small), *big(t_in, t_sq))
    return tuple(outs)
```

```python
import functools

import jax
import jax.numpy as jnp
from jax import lax
from jax.experimental import pallas as pl
from jax.experimental.pallas import tpu as pltpu

F32 = jnp.float32
BF16 = jnp.bfloat16

D_MODEL = 1024
HEADS = 8
HEAD_DIM = 128
IN_WIDTH = 10240
N_CHIPS = 4
W_IN_SHARD = IN_WIDTH // N_CHIPS
ROW_SHARD = D_MODEL // N_CHIPS
RMS_EPS = 1e-6

OFF_SB_Q, OFF_SB_K, OFF_SB_V, OFF_SB_Z = 0, 1024, 2048, 3072
OFF_HG_Q, OFF_HG_F, OFF_HG_I, OFF_HG_Z, OFF_GATE = 4096, 5120, 6144, 7168, 8192

SB_BLOCK = 256
SB_FWD_HEADS = 4
SB_FWD_GROUPS = 2
SB_BWD_HEADS = 2
SB_BWD_GROUPS = 2
SB_DEAD = -110.0
SB_GONE = -1e30
HG_CHUNK = 32
HG_PAIR = 2 * HG_CHUNK
HG_STEP = 256
HG_MID = HG_CHUNK // 2 - 1

ADAM_LR, ADAM_B1, ADAM_B2, ADAM_EPS, ADAM_WD, ADAM_STEP = 0.001, 0.9, 0.999, 1e-08, 0.01, 10

VMEM_LIMIT = 56 * 1024 * 1024
VMEM_LIMIT_DX = 60 * 1024 * 1024

MESH = pl.DeviceIdType.MESH


def _cparams(sem, vmem=VMEM_LIMIT):
    return pltpu.CompilerParams(dimension_semantics=sem, vmem_limit_bytes=vmem)


def _dot(a, b):
    return jnp.dot(a, b, preferred_element_type=F32)


def _dot_nt(a, b):
    return lax.dot_general(a, b, (((1,), (1,)), ((), ())), preferred_element_type=F32)


def _dot_tn(a, b):
    return lax.dot_general(a, b, (((0,), (0,)), ((), ())), preferred_element_type=F32)


def _split_dot(x, tri):
    hi = x.astype(BF16)
    lo = (x - hi.astype(F32)).astype(BF16)
    both = _dot(jnp.concatenate([hi, lo], axis=0), tri)
    return both[: x.shape[0]] + both[x.shape[0] :]


def _split_dot_left(tri, x):
    hi = x.astype(BF16)
    lo = (x - hi.astype(F32)).astype(BF16)
    return _dot(tri, hi) + _dot(tri, lo)


def _sigmoid(x):
    return 1.0 / (1.0 + jnp.exp(-x))


def _prenorm(x, norm_g):
    s_len = x.shape[0]
    ts = min(1024, s_len)

    def body(x_ref, g_ref, h_ref, ht_ref):
        xv = x_ref[...]
        r = lax.rsqrt(jnp.mean(xv * xv, axis=-1, keepdims=True) + RMS_EPS)
        hv = (xv * r) * g_ref[...]
        h_ref[...] = hv.astype(BF16)
        ht_ref[...] = hv.T.astype(BF16)

    return pl.pallas_call(
        body,
        name="prenorm",
        grid=(s_len // ts,),
        in_specs=[pl.BlockSpec((ts, D_MODEL), lambda s: (s, 0)), pl.BlockSpec((1, D_MODEL), lambda s: (0, 0))],
        out_specs=[pl.BlockSpec((ts, D_MODEL), lambda s: (s, 0)), pl.BlockSpec((D_MODEL, ts), lambda s: (0, s))],
        out_shape=[jax.ShapeDtypeStruct((s_len, D_MODEL), BF16), jax.ShapeDtypeStruct((D_MODEL, s_len), BF16)],
        compiler_params=_cparams(("arbitrary",)),
    )(x, norm_g)


def _sb_scores(qb, kb, causal, tri_excl, diag):
    z = _dot_nt(qb, kb) * HEAD_DIM ** -0.5
    ls_pos = jnp.minimum(z, 0.0) - jnp.log1p(jnp.exp(-jnp.abs(z)))
    log_not = ls_pos - z
    log_not_m = jnp.where(causal, log_not, 0.0) if diag else log_not
    return ls_pos, log_not, log_not_m, _split_dot(log_not_m, tri_excl)


def _sb_weights(ls_pos, suffix, carry, causal, diag):
    surv = suffix + carry
    w = jnp.exp(ls_pos + surv)
    return surv, (jnp.where(causal, w, 0.0) if diag else w)


def _sb_specs(s_len, blk, heads):
    width = heads * HEAD_DIM

    def blk_spec(off):
        return pl.BlockSpec((blk, width), lambda h, i: (i, off // width + h))

    def head_spec(off, buffers=2):
        return pl.BlockSpec((s_len, width), lambda h, i: (0, off // width + h), pipeline_mode=pl.Buffered(buffers))

    return blk_spec, head_spec


def _head_cols(p):
    return slice(p * HEAD_DIM, (p + 1) * HEAD_DIM)


def _sb_masks(blk, rows):
    row = lax.broadcasted_iota(jnp.int32, (rows, blk), 0)
    col = lax.broadcasted_iota(jnp.int32, (rows, blk), 1)
    causal = [row + a * rows > col for a in range(blk // rows)]
    row = lax.broadcasted_iota(jnp.int32, (blk, blk), 0)
    col = lax.broadcasted_iota(jnp.int32, (blk, blk), 1)
    tri_excl = (row > col).astype(BF16)
    tri_incl = (row >= col).astype(BF16)
    return causal, tri_excl, tri_incl


def _sb_alive(st, n_chain):
    alive = functools.reduce(jnp.maximum, [st[1 + 3 * c] for c in range(n_chain)])
    return jnp.max(alive) > SB_DEAD


def _sb_fwd(qkv):
    s_len = qkv.shape[0]
    kb_rows = min(SB_BLOCK, s_len)
    groups = min(SB_FWD_GROUPS, s_len // kb_rows)
    blk = groups * kb_rows
    nq = s_len // blk
    rows = kb_rows
    chains = [(p, a) for p in range(SB_FWD_HEADS) for a in range(groups)]

    def body(q_ref, k_ref, v_ref, o_ref, of_ref):
        i = pl.program_id(1)
        masks, tri_excl, _ = _sb_masks(kb_rows, rows)
        causal = masks[0]

        def tiles(steps, st):
            pre = []
            for n, diag in steps:
                for p, a in chains:
                    j = groups * i + a - n
                    start = pl.multiple_of(jnp.maximum(j, 0) * kb_rows, kb_rows)
                    kb = k_ref[pl.ds(start, kb_rows), _head_cols(p)]
                    qb = q_ref[a * rows : (a + 1) * rows, _head_cols(p)]
                    pre.append(_sb_scores(qb, kb, causal, tri_excl, diag) + (v_ref[pl.ds(start, kb_rows), _head_cols(p)], j))
            for t, (_, diag) in enumerate(steps):
                new = []
                for c, (p, a) in enumerate(chains):
                    carry, acc, acc_lo = st[3 * c : 3 * c + 3]
                    ls_pos, _, log_not_m, suffix, vb, j = pre[t * len(chains) + c]
                    if not diag:
                        carry = jnp.where(j >= 0, carry, SB_GONE)
                    surv, w = _sb_weights(ls_pos, suffix, carry, causal, diag)
                    wb = w.astype(BF16)
                    w_lo = (w - wb.astype(F32)).astype(BF16)
                    both = _dot(jnp.concatenate([wb, w_lo], axis=0), vb)
                    new += [surv[:, 0:1] + log_not_m[:, 0:1], acc + both[:rows], acc_lo + both[rows:]]
                st = tuple(new)
            return st

        zero = jnp.zeros((rows, HEAD_DIM), F32)
        st = tiles([(0, True), (1, False)], (jnp.zeros((rows, 1), F32), zero, zero) * len(chains))

        def more(st):
            return (st[0] <= groups * i + groups - 1) & _sb_alive(st, len(chains))

        def step(st):
            return (st[0] + 1,) + tiles([(st[0], False)], st[1:])

        st = lax.while_loop(more, step, (2,) + st)[1:]
        for c, (p, a) in enumerate(chains):
            o_ref[a * rows : (a + 1) * rows, _head_cols(p)] = st[3 * c + 1]
            of_ref[a * rows : (a + 1) * rows, _head_cols(p)] = st[3 * c + 1] + st[3 * c + 2]

    blk_spec, head_spec = _sb_specs(s_len, blk, SB_FWD_HEADS)
    return pl.pallas_call(
        body,
        name="sb_fwd",
        grid=(HEADS // SB_FWD_HEADS, nq),
        in_specs=[blk_spec(OFF_SB_Q), head_spec(OFF_SB_K), head_spec(OFF_SB_V)],
        out_specs=[blk_spec(0), blk_spec(0)],
        out_shape=[jax.ShapeDtypeStruct((s_len, D_MODEL), F32)] * 2,
        compiler_params=_cparams(("arbitrary", "arbitrary")),
    )(qkv, qkv, qkv)


def _sb_bwd(qkv, o_fine, d_o):
    s_len = qkv.shape[0]
    kb_rows = min(SB_BLOCK, s_len)
    groups = min(SB_BWD_GROUPS, s_len // kb_rows)
    blk = groups * kb_rows
    nq = s_len // blk
    scale = HEAD_DIM ** -0.5
    rows = kb_rows
    chains = [(p, a) for p in range(SB_BWD_HEADS) for a in range(groups)]

    def body(q_ref, k_ref, v_ref, of_ref, do_ref, dq_ref, dk_ref, dv_ref, dk_acc, dv_acc):
        i = pl.program_id(1)

        @pl.when(i == 0)
        def _():
            dk_acc[...] = jnp.zeros_like(dk_acc)
            dv_acc[...] = jnp.zeros_like(dv_acc)

        dob = do_ref[...].astype(BF16)
        prod = dob.astype(F32) * of_ref[...]
        masks, tri_excl, tri_incl = _sb_masks(kb_rows, rows)
        causal = masks[0]

        def group(x, p, a):
            return x[a * rows : (a + 1) * rows, _head_cols(p)]

        totals = [jnp.sum(group(prod, p, a), axis=-1, keepdims=True) for p, a in chains]

        def tiles(steps, st):
            pre = []
            for n, diag in steps:
                for p, a in chains:
                    j = groups * i + a - n
                    start = pl.multiple_of(jnp.maximum(j, 0) * kb_rows, kb_rows)
                    kb = k_ref[pl.ds(start, kb_rows), _head_cols(p)]
                    vb = v_ref[pl.ds(start, kb_rows), _head_cols(p)]
                    qb, dob_c = group(q_ref, p, a), group(dob, p, a)
                    pre.append(_sb_scores(qb, kb, causal, tri_excl, diag) + (_dot_nt(dob_c, vb), qb, kb, dob_c, j, start))
            for t, (_, diag) in enumerate(steps):
                mids = []
                for c, (p, a) in enumerate(chains):
                    ls_pos, _, _, suffix, d_w = pre[t * len(chains) + c][:5]
                    c_not = st[3 * c]
                    if not diag:
                        c_not = jnp.where(pre[t * len(chains) + c][8] >= 0, c_not, SB_GONE)
                    surv, w = _sb_weights(ls_pos, suffix, c_not, causal, diag)
                    dlw = d_w * w
                    mids.append((surv, w, dlw, _split_dot(dlw, tri_incl)))
                new = []
                for c, (p, a) in enumerate(chains):
                    c_dlw, dq = st[3 * c + 1 : 3 * c + 3]
                    ls_pos, log_not, log_not_m, _, _, qb, kb, dob_c, j, start = pre[t * len(chains) + c]
                    surv, w, dlw, suffix = mids[c]
                    d_not = totals[c] - c_dlw - suffix
                    dz = ((dlw + d_not) * jnp.exp(log_not) - d_not) * scale
                    dz = jnp.where(causal, dz, 0.0) if diag else jnp.where(j >= 0, dz, 0.0)
                    dzb = dz.astype(BF16)
                    dk_acc[pl.ds(start, kb_rows), _head_cols(p)] += _dot_tn(dzb, qb)
                    dv_acc[pl.ds(start, kb_rows), _head_cols(p)] += _dot_tn(w.astype(BF16), dob_c)
                    new += [surv[:, 0:1] + log_not_m[:, 0:1], c_dlw + suffix[:, 0:1], dq + _dot(dzb, kb)]
                st = tuple(new)
            return st

        zcol = jnp.zeros((rows, 1), F32)
        st = tiles([(0, True), (1, False)], (zcol, zcol, jnp.zeros((rows, HEAD_DIM), F32)) * len(chains))

        def more(st):
            return (st[0] <= groups * i + groups - 1) & _sb_alive(st, len(chains))

        def step(st):
            return (st[0] + 1,) + tiles([(st[0], False)], st[1:])

        st = lax.while_loop(more, step, (2,) + st)[1:]
        for c, (p, a) in enumerate(chains):
            dq_ref[a * rows : (a + 1) * rows, _head_cols(p)] = st[3 * c + 2].astype(BF16)

        @pl.when(i == nq - 1)
        def _():
            dk_ref[...] = dk_acc[...].astype(BF16)
            dv_ref[...] = dv_acc[...].astype(BF16)

    blk_spec, head_spec = _sb_specs(s_len, blk, SB_BWD_HEADS)
    width = SB_BWD_HEADS * HEAD_DIM
    return pl.pallas_call(
        body,
        name="sb_bwd",
        grid=(HEADS // SB_BWD_HEADS, nq),
        in_specs=[blk_spec(OFF_SB_Q), head_spec(OFF_SB_K, 1), head_spec(OFF_SB_V, 1), blk_spec(0), blk_spec(0)],
        out_specs=[blk_spec(0), head_spec(0), head_spec(0)],
        out_shape=[jax.ShapeDtypeStruct((s_len, D_MODEL), BF16)] * 3,
        scratch_shapes=[pltpu.VMEM((s_len, width), F32), pltpu.VMEM((s_len, width), F32)],
        compiler_params=_cparams(("arbitrary", "arbitrary"), vmem=VMEM_LIMIT_DX),
    )(qkv, qkv, qkv, o_fine, d_o)


def _hg_lower_bound(lbl_ref):
    l0 = lbl_ref[0:1, :]
    l1 = lbl_ref[1:2, :]
    mx = jnp.maximum(l0, l1)
    e0 = jnp.exp(l0 - mx)
    e1 = jnp.exp(l1 - mx)
    return e0 / (e0 + e1)


def _hg_gates(hq, hf, lb):
    sig_f = _sigmoid(hf)
    f = lb + (1.0 - lb) * sig_f
    g = jnp.log(f)
    kk = 1.0 - f
    sig_q = _sigmoid(hq)
    qq = hq * sig_q
    return qq, kk, g, f, sig_f, sig_q


def _period_bcast(x, r, rows, period):
    w = x.shape[-1]
    x3 = x.reshape(rows // period, period, w)
    return jnp.broadcast_to(x3[:, r : r + 1, :], x3.shape).reshape(rows, w)


def _blockdiag(rows, kind):
    row = lax.broadcasted_iota(jnp.int32, (rows, rows), 0)
    col = lax.broadcasted_iota(jnp.int32, (rows, rows), 1)
    if kind in ("next", "prev"):
        first, second = (row, col) if kind == "next" else (col, row)
        keep = ((row // HG_PAIR) == (col // HG_PAIR)) & (first % HG_PAIR < HG_CHUNK) & (second % HG_PAIR >= HG_CHUNK)
    else:
        keep = (row // HG_CHUNK) == (col // HG_CHUNK)
        if kind == "lower":
            keep = keep & (row >= col)
        elif kind == "upper":
            keep = keep & (row <= col)
    return jnp.where(keep, 1.0, 0.0).astype(BF16)


def _hg_operands(hq, hf, lb, rows):
    qq, kk, g, f, sig_f, sig_q = _hg_gates(hq, hf, lb)
    cum = _split_dot_left(_blockdiag(rows, "lower"), g)
    mid = _period_bcast(cum, HG_MID, rows, HG_CHUNK)
    last = _period_bcast(cum, HG_CHUNK - 1, rows, HG_CHUNK)
    last0 = _period_bcast(cum, HG_CHUNK - 1, rows, HG_PAIR)
    last1 = _period_bcast(cum, HG_PAIR - 1, rows, HG_PAIR)
    second = (lax.broadcasted_iota(jnp.int32, cum.shape, 0) % HG_PAIR) >= HG_CHUNK
    e = dict(qm=jnp.exp(cum - mid), km=jnp.exp(mid - cum), qd=jnp.exp(cum), kl=jnp.exp(last - cum),
             q_in=jnp.where(second, jnp.exp(last0), 1.0), k_out=jnp.where(second, 1.0, jnp.exp(last1)),
             pair=jnp.exp(last0 + last1))
    v = dict(qm=qq * e["qm"], km=kk * e["km"], qd=qq * e["qd"], kl=kk * e["kl"])
    v["qp"] = v["qd"] * e["q_in"]
    v["kp"] = v["kl"] * e["k_out"]
    return v, e, second, (f, sig_f, sig_q)


def _hg_store_operands(v, second, hi, refs):
    zero = jnp.zeros_like(v["qm"])
    q_cat, k_cat, qp_b, kp_b, v_b = refs
    q_cat[:, 0:D_MODEL] = jnp.where(second, zero, v["qm"]).astype(BF16)
    q_cat[:, D_MODEL : 2 * D_MODEL] = jnp.where(second, v["qm"], zero).astype(BF16)
    q_cat[:, 2 * D_MODEL :] = jnp.where(second, v["qd"], zero).astype(BF16)
    k_cat[:, 0:D_MODEL] = jnp.where(second, zero, v["km"]).astype(BF16)
    k_cat[:, D_MODEL : 2 * D_MODEL] = jnp.where(second, v["km"], zero).astype(BF16)
    k_cat[:, 2 * D_MODEL :] = jnp.where(second, zero, v["kl"]).astype(BF16)
    qp_b[...] = v["qp"].astype(BF16)
    kp_b[...] = v["kp"].astype(BF16)
    v_b[...] = hi.astype(BF16)


def _hg_pair_operands(cat, r0, c0):
    return jnp.concatenate([cat[r0 : r0 + HG_PAIR, g * D_MODEL + c0 : g * D_MODEL + c0 + HEAD_DIM] for g in range(3)], axis=1)


def _hg_fwd(proj, lbl):
    s_len = proj.shape[0]
    rows = min(HG_STEP, s_len)
    n_pairs = rows // HG_PAIR

    def body(hq_ref, hf_ref, hi_ref, lbl_ref, o_ref, st_ref, state, q_cat, k_cat, qp_b, kp_b, v_b):
        @pl.when(pl.program_id(0) == 0)
        def _():
            state[...] = jnp.zeros_like(state)

        v, e, second, _ = _hg_operands(hq_ref[...], hf_ref[...], _hg_lower_bound(lbl_ref), rows)
        _hg_store_operands(v, second, hi_ref[...], (q_cat, k_cat, qp_b, kp_b, v_b))
        e_pair = e["pair"]
        row = lax.broadcasted_iota(jnp.int32, (HG_PAIR, HG_PAIR), 0)
        col = lax.broadcasted_iota(jnp.int32, (HG_PAIR, HG_PAIR), 1)
        causal = row >= col

        for u in range(n_pairs):
            r0 = u * HG_PAIR
            sls = [(slice(r0, r0 + HG_PAIR), slice(h * HEAD_DIM, (h + 1) * HEAD_DIM)) for h in range(HEADS)]
            a_s = [jnp.where(causal, _dot_nt(_hg_pair_operands(q_cat, r0, h * HEAD_DIM),
                                             _hg_pair_operands(k_cat, r0, h * HEAD_DIM)), 0.0).astype(BF16)
                   for h in range(HEADS)]
            st_s = [state[h] for h in range(HEADS)]
            for h, sl in enumerate(sls):
                st_ref[u, h] = st_s[h]
                state[h] = st_s[h] * e_pair[r0 : r0 + 1, sl[1]] + _dot_tn(v_b[sl], kp_b[sl])
            for h, sl in enumerate(sls):
                o_ref[sl] = _dot(a_s[h], v_b[sl]) + _dot_nt(qp_b[sl], st_s[h].astype(BF16))

    def col_spec(off):
        return pl.BlockSpec((rows, D_MODEL), lambda s: (s, off // D_MODEL))

    bf_tile = pltpu.VMEM((rows, D_MODEL), BF16)
    bf_cat = pltpu.VMEM((rows, 3 * D_MODEL), BF16)
    scratch = [pltpu.VMEM((HEADS, HEAD_DIM, HEAD_DIM), F32), bf_cat, bf_cat, bf_tile, bf_tile, bf_tile]
    return pl.pallas_call(
        body,
        name="hg_fwd",
        grid=(s_len // rows,),
        in_specs=[col_spec(OFF_HG_Q), col_spec(OFF_HG_F), col_spec(OFF_HG_I), pl.BlockSpec((2, D_MODEL), lambda s: (0, 0))],
        out_specs=[
            pl.BlockSpec((rows, D_MODEL), lambda s: (s, 0)),
            pl.BlockSpec((n_pairs, HEADS, HEAD_DIM, HEAD_DIM), lambda s: (s, 0, 0, 0)),
        ],
        out_shape=[
            jax.ShapeDtypeStruct((s_len, D_MODEL), F32),
            jax.ShapeDtypeStruct((s_len // HG_PAIR, HEADS, HEAD_DIM, HEAD_DIM), F32),
        ],
        scratch_shapes=scratch,
        compiler_params=_cparams(("arbitrary",)),
    )(proj, proj, proj, lbl)


def _hg_bwd(proj, lbl, states, d_o):
    s_len = proj.shape[0]
    rows = min(HG_STEP, s_len)
    n_pairs = rows // HG_PAIR
    n_steps = s_len // rows

    def body(hq_ref, hf_ref, hi_ref, lbl_ref, st_ref, do_ref, dp_ref, dlb_ref,
             dstate, q_cat, k_cat, qp_b, kp_b, v_b, do_b, d_qcat, d_kcat, d_qp, d_kp, d_v, d_pair):
        @pl.when(pl.program_id(0) == 0)
        def _():
            dstate[...] = jnp.zeros_like(dstate)
            dlb_ref[...] = jnp.zeros_like(dlb_ref)

        lb = _hg_lower_bound(lbl_ref)
        hq = hq_ref[...]
        v, e, second, (f, sig_f, sig_q) = _hg_operands(hq, hf_ref[...], lb, rows)
        _hg_store_operands(v, second, hi_ref[...], (q_cat, k_cat, qp_b, kp_b, v_b))
        do_b[...] = do_ref[...].astype(BF16)
        e_pair = e["pair"]
        row = lax.broadcasted_iota(jnp.int32, (HG_PAIR, HG_PAIR), 0)
        col = lax.broadcasted_iota(jnp.int32, (HG_PAIR, HG_PAIR), 1)
        causal = row >= col

        for u in reversed(range(n_pairs)):
            r0 = u * HG_PAIR
            sls = [(slice(r0, r0 + HG_PAIR), slice(h * HEAD_DIM, (h + 1) * HEAD_DIM)) for h in range(HEADS)]
            ops = [(_hg_pair_operands(q_cat, r0, h * HEAD_DIM), _hg_pair_operands(k_cat, r0, h * HEAD_DIM))
                   for h in range(HEADS)]
            a_s = [jnp.where(causal, _dot_nt(lhs, rhs), 0.0).astype(BF16) for lhs, rhs in ops]
            da_s = [jnp.where(causal, _dot_nt(do_b[sl], v_b[sl]), 0.0).astype(BF16) for sl in sls]
            st0_s = [st_ref[u, h] for h in range(HEADS)]
            ds1_s = [dstate[h] for h in range(HEADS)]
            ds1b_s = [ds1.astype(BF16) for ds1 in ds1_s]
            for h, sl in enumerate(sls):
                decay = e_pair[r0 : r0 + 1, sl[1]]
                d_pair[u : u + 1, sl[1]] = decay * jnp.sum(ds1_s[h] * st0_s[h], axis=0, keepdims=True)
                dstate[h] = ds1_s[h] * decay + _dot_tn(do_b[sl], qp_b[sl])
            for h, sl in enumerate(sls):
                d_qp[sl] = _dot(do_b[sl], st0_s[h].astype(BF16))
                d_kp[sl] = _dot(v_b[sl], ds1b_s[h])
            for h, sl in enumerate(sls):
                d_v[sl] = _dot_tn(a_s[h], do_b[sl]) + _dot_nt(kp_b[sl], ds1b_s[h])
            for h, sl in enumerate(sls):
                d_lhs = _dot(da_s[h], ops[h][1])
                d_rhs = _dot_tn(da_s[h], ops[h][0])
                for g in range(3):
                    gsl = (sl[0], slice(g * D_MODEL + h * HEAD_DIM, g * D_MODEL + (h + 1) * HEAD_DIM))
                    d_qcat[gsl] = d_lhs[:, g * HEAD_DIM : (g + 1) * HEAD_DIM]
                    d_kcat[gsl] = d_rhs[:, g * HEAD_DIM : (g + 1) * HEAD_DIM]

        zero = jnp.zeros_like(hq)
        dqm = jnp.where(second, d_qcat[:, D_MODEL : 2 * D_MODEL], d_qcat[:, 0:D_MODEL])
        dkm = jnp.where(second, d_kcat[:, D_MODEL : 2 * D_MODEL], d_kcat[:, 0:D_MODEL])
        dqp, dkp = d_qp[...], d_kp[...]
        dqd = dqp * e["q_in"] + jnp.where(second, d_qcat[:, 2 * D_MODEL :], zero)
        dkl = dkp * e["k_out"] + jnp.where(second, zero, d_kcat[:, 2 * D_MODEL :])
        dq = dqm * e["qm"] + dqd * e["qd"]
        dk = dkm * e["km"] + dkl * e["kl"]
        t_kl = dkl * v["kl"]
        dcum = dqm * v["qm"] - dkm * v["km"] + dqd * v["qd"] - t_kl
        dp = d_pair[...]
        dp_b = jnp.broadcast_to(dp[:, None, :], (n_pairs, HG_PAIR, D_MODEL)).reshape(rows, D_MODEL)
        dg = (_split_dot_left(_blockdiag(rows, "upper"), dcum) + _split_dot_left(_blockdiag(rows, "all"), t_kl)
              + _split_dot_left(_blockdiag(rows, "next"), dqp * v["qp"])
              + _split_dot_left(_blockdiag(rows, "prev"), dkp * v["kp"]) + dp_b)
        df = dg / f - dk
        one_m = 1.0 - sig_f
        dp_ref[:, 0:D_MODEL] = (dq * (sig_q * (1.0 + hq * (1.0 - sig_q)))).astype(BF16)
        dp_ref[:, D_MODEL : 2 * D_MODEL] = (df * (1.0 - lb) * sig_f * one_m).astype(BF16)
        dp_ref[:, 2 * D_MODEL : 3 * D_MODEL] = d_v[...].astype(BF16)
        dlb_ref[...] += jnp.sum(df * one_m, axis=0, keepdims=True)

    def col_spec(off):
        return pl.BlockSpec((rows, D_MODEL), lambda s: (n_steps - 1 - s, off // D_MODEL))

    f32_tile = pltpu.VMEM((rows, D_MODEL), F32)
    f32_cat = pltpu.VMEM((rows, 3 * D_MODEL), F32)
    bf_tile = pltpu.VMEM((rows, D_MODEL), BF16)
    bf_cat = pltpu.VMEM((rows, 3 * D_MODEL), BF16)
    scratch = [pltpu.VMEM((HEADS, HEAD_DIM, HEAD_DIM), F32), bf_cat, bf_cat, bf_tile, bf_tile, bf_tile, bf_tile,
               f32_cat, f32_cat, f32_tile, f32_tile, f32_tile, pltpu.VMEM((n_pairs, D_MODEL), F32)]
    return pl.pallas_call(
        body,
        name="hg_bwd",
        grid=(n_steps,),
        in_specs=[
            col_spec(OFF_HG_Q), col_spec(OFF_HG_F), col_spec(OFF_HG_I),
            pl.BlockSpec((2, D_MODEL), lambda s: (0, 0)),
            pl.BlockSpec((n_pairs, HEADS, HEAD_DIM, HEAD_DIM), lambda s: (n_steps - 1 - s, 0, 0, 0)),
            pl.BlockSpec((rows, D_MODEL), lambda s: (n_steps - 1 - s, 0)),
        ],
        out_specs=[
            pl.BlockSpec((rows, 3 * D_MODEL), lambda s: (n_steps - 1 - s, 0)),
            pl.BlockSpec((1, D_MODEL), lambda s: (0, 0)),
        ],
        out_shape=[
            jax.ShapeDtypeStruct((s_len, 3 * D_MODEL), BF16),
            jax.ShapeDtypeStruct((1, D_MODEL), F32),
        ],
        scratch_shapes=scratch,
        compiler_params=_cparams(("arbitrary",)),
    )(proj, proj, proj, lbl, states, d_o)


def _mid(proj, sb_o, hg_o, x, target, b_gate, hg_gain, final_g, w_sb, w_hg, w_out):
    s_len = proj.shape[0]
    ts = min(256, s_len)
    inv_d = 1.0 / D_MODEL

    def body(zsb_ref, hz_ref, gl_ref, sbo_ref, hgo_ref, x_ref, tgt_ref, bg_ref, hgn_ref, fg_ref,
             wsb_ref, whg_ref, wout_ref,
             dout_ref, dsbo_ref, dhgo_ref, dmid_ref,
             asb_ref, dusb_ref, ahg_ref, duhg_ref, y_ref, doutb_ref,
             loss_ref, dfg_ref, dbg_ref, dhgn_ref):
        @pl.when(pl.program_id(0) == 0)
        def _():
            loss_ref[...] = jnp.zeros_like(loss_ref)
            dfg_ref[...] = jnp.zeros_like(dfg_ref)
            dbg_ref[...] = jnp.zeros_like(dbg_ref)
            dhgn_ref[...] = jnp.zeros_like(dhgn_ref)

        z_sb = zsb_ref[...]
        sb_o = sbo_ref[...]
        sig_zsb = _sigmoid(z_sb)
        silu_zsb = z_sb * sig_zsb
        a_sb_f = sb_o * silu_zsb
        a_sb = a_sb_f.astype(BF16)
        u_sb = _dot(a_sb, wsb_ref[...])

        hg_o = hgo_ref[...]
        gain = hgn_ref[...]
        r_parts, yn_parts = [], []
        for h in range(HEADS):
            oh = hg_o[:, h * HEAD_DIM : (h + 1) * HEAD_DIM]
            r = lax.rsqrt(jnp.mean(oh * oh, axis=-1, keepdims=True) + RMS_EPS)
            r_parts.append(jnp.broadcast_to(r, oh.shape))
            yn_parts.append(oh * r)
        r_hg = jnp.concatenate(r_parts, axis=-1)
        yn_hg = jnp.concatenate(yn_parts, axis=-1)
        hn = yn_hg * gain
        hz = hz_ref[...]
        sig_hz = _sigmoid(hz)
        silu_hz = hz * sig_hz
        a_hg_f = hn * silu_hz
        a_hg = a_hg_f.astype(BF16)
        u_hg = _dot(a_hg, whg_ref[...])

        gates = _sigmoid(gl_ref[...] + bg_ref[...])
        g_sb = gates[:, 0:D_MODEL]
        g_hg = gates[:, D_MODEL:]
        y_f = g_sb * u_sb + g_hg * u_hg
        y = y_f.astype(BF16)
        out = x_ref[...] + _dot(y, wout_ref[...])
        r2 = lax.rsqrt(jnp.mean(out * out, axis=-1, keepdims=True) + RMS_EPS)
        yn = out * r2
        fg = fg_ref[...]
        diff = yn * fg - tgt_ref[...]
        loss_ref[...] += 0.5 * inv_d * jnp.sum(diff * diff)

        dyf = diff * inv_d
        dfg_ref[...] += jnp.sum(dyf * yn, axis=0, keepdims=True)
        dyn = dyf * fg
        dout = r2 * (dyn - yn * jnp.mean(dyn * yn, axis=-1, keepdims=True))
        dout_ref[...] = dout
        doutb = dout.astype(BF16)
        doutb_ref[...] = doutb
        dy = _dot_nt(doutb, wout_ref[...])
        du_sb = (dy * g_sb).astype(BF16)
        du_hg = (dy * g_hg).astype(BF16)
        dgl_sb = dy * u_sb * g_sb * (1.0 - g_sb)
        dgl_hg = dy * u_hg * g_hg * (1.0 - g_hg)
        dmid_ref[:, 2 * D_MODEL : 3 * D_MODEL] = dgl_sb.astype(BF16)
        dmid_ref[:, 3 * D_MODEL :] = dgl_hg.astype(BF16)
        dbg_ref[:, 0:D_MODEL] += jnp.sum(dgl_sb, axis=0, keepdims=True)
        dbg_ref[:, D_MODEL:] += jnp.sum(dgl_hg, axis=0, keepdims=True)

        da_sb = _dot_nt(du_sb, wsb_ref[...])
        dsbo_ref[...] = (da_sb * silu_zsb).astype(BF16)
        dmid_ref[:, 0:D_MODEL] = (da_sb * sb_o * (sig_zsb * (1.0 + z_sb * (1.0 - sig_zsb)))).astype(BF16)

        da_hg = _dot_nt(du_hg, whg_ref[...])
        dhn = da_hg * silu_hz
        dmid_ref[:, D_MODEL : 2 * D_MODEL] = (da_hg * hn * (sig_hz * (1.0 + hz * (1.0 - sig_hz)))).astype(BF16)
        dhgn_ref[...] += jnp.sum(dhn * yn_hg, axis=0, keepdims=True)
        dyn_hg = dhn * gain
        prod = dyn_hg * yn_hg
        m_parts = []
        for h in range(HEADS):
            ph = prod[:, h * HEAD_DIM : (h + 1) * HEAD_DIM]
            m_parts.append(jnp.broadcast_to(jnp.mean(ph, axis=-1, keepdims=True), ph.shape))
        dhgo_ref[...] = (r_hg * (dyn_hg - yn_hg * jnp.concatenate(m_parts, axis=-1))).astype(BF16)

        asb_ref[...] = a_sb_f.T.astype(BF16)
        dusb_ref[...] = du_sb
        ahg_ref[...] = a_hg_f.T.astype(BF16)
        duhg_ref[...] = du_hg
        y_ref[...] = y_f.T.astype(BF16)

    def tile(width, off=0):
        return pl.BlockSpec((ts, width), lambda s: (s, off // width))

    def across():
        return pl.BlockSpec((D_MODEL, ts), lambda s: (0, s))

    def whole(shape):
        return pl.BlockSpec(shape, lambda s: (0,) * len(shape))

    def weight():
        return pl.BlockSpec((D_MODEL, D_MODEL), lambda s: (0, 0), pipeline_mode=pl.Buffered(1))

    f32_act = jax.ShapeDtypeStruct((s_len, D_MODEL), F32)
    bf_act = jax.ShapeDtypeStruct((s_len, D_MODEL), BF16)
    bf_act_t = jax.ShapeDtypeStruct((D_MODEL, s_len), BF16)
    return pl.pallas_call(
        body,
        name="mid",
        grid=(s_len // ts,),
        in_specs=[
            tile(D_MODEL, OFF_SB_Z), tile(D_MODEL, OFF_HG_Z), tile(2 * D_MODEL, OFF_GATE),
            tile(D_MODEL), tile(D_MODEL), tile(D_MODEL), tile(D_MODEL),
            whole((1, 2 * D_MODEL)), whole((1, D_MODEL)), whole((1, D_MODEL)),
            weight(), weight(), weight(),
        ],
        out_specs=[
            tile(D_MODEL), tile(D_MODEL), tile(D_MODEL), tile(4 * D_MODEL),
            across(), tile(D_MODEL), across(), tile(D_MODEL), across(), tile(D_MODEL),
            whole((1, 1)), whole((1, D_MODEL)), whole((1, 2 * D_MODEL)), whole((1, D_MODEL)),
        ],
        out_shape=[
            f32_act, bf_act, bf_act, jax.ShapeDtypeStruct((s_len, 4 * D_MODEL), BF16),
            bf_act_t, bf_act, bf_act_t, bf_act, bf_act_t, bf_act,
            jax.ShapeDtypeStruct((1, 1), F32), jax.ShapeDtypeStruct((1, D_MODEL), F32),
            jax.ShapeDtypeStruct((1, 2 * D_MODEL), F32), jax.ShapeDtypeStruct((1, D_MODEL), F32),
        ],
        compiler_params=_cparams(("arbitrary",)),
    )(proj, proj, proj, sb_o, hg_o, x, target, b_gate, hg_gain, final_g, w_sb, w_hg, w_out)


def _grad_square(a_t, b, name):
    s_len = b.shape[0]
    tk = min(1024, s_len)

    def body(a_ref, b_ref, o_ref):
        @pl.when(pl.program_id(0) == 0)
        def _():
            o_ref[...] = jnp.zeros_like(o_ref)

        o_ref[...] += _dot(a_ref[...], b_ref[...])

    return pl.pallas_call(
        body,
        name=name,
        grid=(s_len // tk,),
        in_specs=[pl.BlockSpec((D_MODEL, tk), lambda k: (0, k)), pl.BlockSpec((tk, D_MODEL), lambda k: (k, 0))],
        out_specs=pl.BlockSpec((D_MODEL, D_MODEL), lambda k: (0, 0)),
        out_shape=jax.ShapeDtypeStruct((D_MODEL, D_MODEL), F32),
        compiler_params=_cparams(("arbitrary",)),
    )(a_t, b)


SEG_WIDTHS = (1024, 1024, 1024, 4096, 3072)
SEG_TILE = 1024
SEG_BOUNDS = (0, 1, 2, 3, 7, 10)


def _w_in_tile(k):
    return jnp.where(k < 4, k, jnp.where(k < 7, k + 3, k - 3))


def _grad_w_in(h_t, segs):
    m, s_len = h_t.shape
    tk = min(1024, s_len)
    tn = SEG_TILE
    nk = s_len // tk
    bounds = SEG_BOUNDS

    def body(a_ref, *refs):
        seg_refs, o_ref = refs[:-1], refs[-1]
        j = pl.program_id(0)

        @pl.when(pl.program_id(1) == 0)
        def _():
            o_ref[...] = jnp.zeros_like(o_ref)

        for i, ref in enumerate(seg_refs):
            @pl.when((j >= bounds[i]) & (j < bounds[i + 1]))
            def _(ref=ref):
                o_ref[...] += _dot(a_ref[...], ref[...])

    def seg_spec(lo, hi):
        def index(j, k):
            return (jnp.where(j < lo, 0, jnp.where(j >= hi, nk - 1, k)), jnp.clip(j - lo, 0, hi - lo - 1))
        return pl.BlockSpec((tk, tn), index)

    return pl.pallas_call(
        body,
        name="grad_w_in",
        grid=(IN_WIDTH // tn, nk),
        in_specs=[pl.BlockSpec((m, tk), lambda j, k: (0, k))]
        + [seg_spec(bounds[i], bounds[i + 1]) for i in range(len(SEG_WIDTHS))],
        out_specs=pl.BlockSpec((m, tn), lambda j, k: (0, _w_in_tile(j))),
        out_shape=jax.ShapeDtypeStruct((m, IN_WIDTH), F32),
        compiler_params=_cparams(("arbitrary", "arbitrary")),
    )(h_t, *segs)


EXCHANGE_IN_PIECES = 8
EXCHANGE_PIECES = EXCHANGE_IN_PIECES + 3


def _exchange_copies(sin_ref, ssq_ref, got_in, got_sq, send_sems, recv_sems):
    _, _, c, chips = _position()
    rows = HALF_IN // EXCHANGE_IN_PIECES
    copies = []
    for k, (px, py) in enumerate(chips):
        chip = 2 * px + py
        for p in range(EXCHANGE_PIECES):
            if p < EXCHANGE_IN_PIECES:
                src, dst = sin_ref.at[chip, pl.ds(p * rows, rows), :], got_in.at[k, pl.ds(p * rows, rows), :]
            else:
                src, dst = ssq_ref.at[p - EXCHANGE_IN_PIECES, chip], got_sq.at[k, p - EXCHANGE_IN_PIECES]
            copies.append(_remote(src, dst, send_sems.at[k, p], recv_sems.at[k, p], (px, py, c)))
    return copies


def _dx(segs, w_all, x, norm_g, dout, s_in, s_sq):
    s_len = x.shape[0]
    ts = min(1024, s_len)
    tk = SEG_TILE
    nk = IN_WIDTH // tk
    ns = s_len // ts
    bounds = SEG_BOUNDS
    n_seg = len(SEG_WIDTHS)

    def body(*refs):
        seg_refs = refs[:n_seg]
        w_ref, x_ref, g_ref, dout_ref, sin_ref, ssq_ref, gx_ref, dg_ref, got_in, got_sq, acc, send_sems, recv_sems = refs[n_seg:]
        s, k = pl.program_id(0), pl.program_id(1)

        @pl.when((s == 0) & (k == 0))
        def _():
            dg_ref[...] = jnp.zeros_like(dg_ref)
            for cp in _exchange_copies(sin_ref, ssq_ref, got_in, got_sq, send_sems, recv_sems):
                cp.start()

        @pl.when(k == 0)
        def _():
            acc[...] = jnp.zeros_like(acc)

        for i, ref in enumerate(seg_refs):
            @pl.when((k >= bounds[i]) & (k < bounds[i + 1]))
            def _(ref=ref):
                acc[...] += _dot_nt(ref[...], w_ref[...])

        @pl.when(k == nk - 1)
        def _():
            dh = acc[...]
            xv = x_ref[...]
            r = lax.rsqrt(jnp.mean(xv * xv, axis=-1, keepdims=True) + RMS_EPS)
            xn = xv * r
            dg_ref[...] += jnp.sum(dh * xn, axis=0, keepdims=True)
            dxn = dh * g_ref[...]
            gx_ref[...] = r * (dxn - xn * jnp.mean(dxn * xn, axis=-1, keepdims=True)) + dout_ref[...]

        @pl.when((s == ns - 1) & (k == nk - 1))
        def _():
            for cp in _exchange_copies(sin_ref, ssq_ref, got_in, got_sq, send_sems, recv_sems):
                cp.wait()

    def seg_spec(lo, hi):
        return pl.BlockSpec((ts, tk), lambda s, k: (s, jnp.clip(k - lo, 0, hi - lo - 1)))

    row_tile = pl.BlockSpec((ts, D_MODEL), lambda s, k: (s, 0))
    vec = pl.BlockSpec((1, D_MODEL), lambda s, k: (0, 0))
    return pl.pallas_call(
        body,
        name="dx",
        grid=(ns, nk),
        in_specs=[seg_spec(bounds[i], bounds[i + 1]) for i in range(n_seg)] + [
            pl.BlockSpec((D_MODEL, tk), lambda s, k: (0, _w_in_tile(k))),
            row_tile, vec, row_tile, ANY, ANY,
        ],
        out_specs=[pl.BlockSpec((None, ts, D_MODEL), lambda s, k: (0, s, 0)), vec, ANY, ANY],
        out_shape=[jax.ShapeDtypeStruct((1, s_len, D_MODEL), F32), jax.ShapeDtypeStruct((1, D_MODEL), F32),
                   jax.ShapeDtypeStruct((3, HALF_IN, W_IN_SHARD), WIRE),
                   jax.ShapeDtypeStruct((3, 3, HALF_SQ, D_MODEL), WIRE)],
        scratch_shapes=[pltpu.VMEM((ts, D_MODEL), F32),
                        pltpu.SemaphoreType.DMA((3, EXCHANGE_PIECES)), pltpu.SemaphoreType.DMA((3, EXCHANGE_PIECES))],
        compiler_params=_cparams(("arbitrary", "arbitrary"), vmem=VMEM_LIMIT_DX),
    )(*segs, w_all, x, norm_g, dout, s_in, s_sq)


def _local_grads(x, target, proj, h_t, qkv, b_gate, lbl, hg_gain, final_g, w_sb, w_hg, w_out):
    sb_o, sb_o_fine = _sb_fwd(qkv)
    hg_o, states = _hg_fwd(proj, lbl)
    (dout, d_sbo, d_hgo, d_mid, a_sb, du_sb, a_hg, du_hg, y, doutb,
     loss, d_fg, d_bg, d_hgn) = _mid(proj, sb_o, hg_o, x, target, b_gate, hg_gain, final_g, w_sb, w_hg, w_out)
    g_w_sb = _grad_square(a_sb, du_sb, "grad_w_sb")
    g_w_hg = _grad_square(a_hg, du_hg, "grad_w_hg")
    g_w_out = _grad_square(y, doutb, "grad_w_out")
    d_q, d_k, d_v = _sb_bwd(qkv, sb_o_fine, d_sbo)
    d_hg, d_lb = _hg_bwd(proj, lbl, states, d_hgo)
    segs = (d_q, d_k, d_v, d_mid, d_hg)
    g_w_in = _grad_w_in(h_t, segs)
    return g_w_in, g_w_sb, g_w_hg, g_w_out, segs, dout, loss, d_bg, d_lb, d_hgn, d_fg


ANY = pl.BlockSpec(memory_space=pl.ANY)
WIRE = BF16
HALF_IN = D_MODEL // 2
HALF_SQ = ROW_SHARD // 2


def _position():
    x, y, c = lax.axis_index("x"), lax.axis_index("y"), lax.axis_index("c")
    chips = [(1 - x, y), (x, 1 - y), (1 - x, 1 - y)]
    return x, y, c, chips


def _remote(src, dst, send_sem, recv_sem, to):
    return pltpu.make_async_remote_copy(src_ref=src, dst_ref=dst, send_sem=send_sem, recv_sem=recv_sem,
                                        device_id=to, device_id_type=MESH)


PROJ_TILE = 1280
F32_FROM_TILE = 2
BF16_TO_TILE = 2
W_LOAD_PIECES = 8


def _gather_inproj(idx, h, w_in_b, w_sq_b):
    s_len = h.shape[0]
    ts = min(1024, s_len)
    ns = s_len // ts
    per = W_IN_SHARD // PROJ_TILE
    n_in = 4
    n_piece = n_in + 3
    rows = HALF_IN // n_in

    def chip_at(r, me):
        return me ^ jnp.where(r == 1, 2, jnp.where(r == 2, 1, jnp.where(r == 3, 3, 0)))

    def body(idx_ref, h_ref, win_ref, wsqb_ref, proj_ref, qkv_ref, wall_ref, wsq_ref, wbuf, send_sems, recv_sems, w_sems):
        r, t, s = pl.program_id(0), pl.program_id(1), pl.program_id(2)
        x, y, c, chips = _position()
        me = 2 * x + y
        sibling = (x, y, 1 - c)
        first = (t == 0) & (s == 0)

        def src_piece(p):
            if p < n_in:
                return win_ref.at[pl.ds(c * HALF_IN + p * rows, rows), :]
            return wsqb_ref.at[p - n_in, pl.ds(c * HALF_SQ, HALF_SQ), :]

        def piece(p, chip, core):
            if p < n_in:
                cols = pl.ds(pl.multiple_of(chip * W_IN_SHARD, W_IN_SHARD), W_IN_SHARD)
                return wall_ref.at[pl.ds(core * HALF_IN + p * rows, rows), cols]
            return wsq_ref.at[p - n_in, chip, pl.ds(core * HALF_SQ, HALF_SQ), :]

        def send(k, p):
            px, py = chips[k]
            return _remote(src_piece(p), piece(p, me, c), send_sems.at[k, p], recv_sems.at[k, p], (px, py, c))

        def forward(k, p, core):
            px, py = chips[k]
            got = piece(p, 2 * px + py, core)
            return _remote(got, got, send_sems.at[3 + k, p], recv_sems.at[3 + k, p], sibling)

        @pl.when((r == 0) & first)
        def _():
            for k in range(2):
                for p in range(n_piece):
                    send(k, p).start()

        for k in range(3):
            @pl.when((r == k + 1) & first)
            def _(k=k):
                px, py = chips[k]
                for p in range(n_piece):
                    got = piece(p, 2 * px + py, c)
                    _remote(got, got, send_sems.at[k, p], recv_sems.at[k, p], (px, py, c)).wait_recv()
                    forward(k, p, c).start()
                if k == 0:
                    for p in range(n_piece):
                        send(2, p).start()
                for p in range(n_piece):
                    forward(k, p, 1 - c).wait_recv()

        def tile_loads(slot, own):
            col = slot * PROJ_TILE
            if not own:
                col = pl.multiple_of(chip_at(r, me) * W_IN_SHARD + col, PROJ_TILE)
            src = win_ref if own else wall_ref
            part = D_MODEL // W_LOAD_PIECES
            return [pltpu.make_async_copy(src.at[pl.ds(q * part, part), pl.ds(col, PROJ_TILE)],
                                          wbuf.at[slot, pl.ds(q * part, part), :], w_sems.at[slot, q])
                    for q in range(W_LOAD_PIECES)]

        for own in (True, False):
            @pl.when(first & ((r == 0) if own else (r > 0)))
            def _(own=own):
                for slot in range(per):
                    for cp in tile_loads(slot, own):
                        cp.start()
                for cp in tile_loads(0, own):
                    cp.wait()

            @pl.when((t > 0) & (s == 0) & ((r == 0) if own else (r > 0)))
            def _(own=own):
                for cp in tile_loads(1, own):
                    cp.wait()

        tile_now = per * chip_at(r, me) + t
        want_f32, want_bf16 = tile_now >= F32_FROM_TILE, tile_now <= BF16_TO_TILE

        @pl.when(want_f32 & jnp.logical_not(want_bf16))
        def _():
            proj_ref[...] = _dot(h_ref[...], wbuf[t])

        @pl.when(want_bf16 & jnp.logical_not(want_f32))
        def _():
            qkv_ref[...] = _dot(h_ref[...], wbuf[t]).astype(BF16)

        @pl.when(want_f32 & want_bf16)
        def _():
            p = _dot(h_ref[...], wbuf[t])
            proj_ref[...] = p
            qkv_ref[...] = p.astype(BF16)

        @pl.when((r == 3) & (t == per - 1) & (s == ns - 1))
        def _():
            for k in range(3):
                for p in range(n_piece):
                    send(k, p).wait_send()
                    forward(k, p, c).wait_send()

    def out_index(wanted):
        order = [0, 2, 1, 3]
        table = []
        for chip in range(N_CHIPS):
            tiles = [per * (chip ^ order[q // per]) + q % per for q in range(N_CHIPS * per)]
            row = []
            for q, tile in enumerate(tiles):
                if wanted(tile):
                    row.append((tile, None))
                    continue
                before = [u for u in tiles[:q] if wanted(u)]
                after = [u for u in tiles[q:] if wanted(u)]
                row.append((before[-1], ns - 1) if before else (after[0], 0))
            table.append(row)

        def index(r, t, s, idx):
            q = r * per + t
            col, fixed_s = jnp.int32(0), jnp.int32(-1)
            for chip in range(N_CHIPS):
                for pos, (tile, hold) in enumerate(table[chip]):
                    here = (idx[0] == chip) & (q == pos)
                    col = jnp.where(here, tile, col)
                    fixed_s = jnp.where(here, -1 if hold is None else hold, fixed_s)
            return jnp.where(fixed_s < 0, s, fixed_s), col

        return index

    grid_spec = pltpu.PrefetchScalarGridSpec(
        num_scalar_prefetch=1,
        grid=(N_CHIPS, per, ns),
        in_specs=[pl.BlockSpec((ts, D_MODEL), lambda r, t, s, idx: (s, 0)), ANY, ANY],
        out_specs=[pl.BlockSpec((ts, PROJ_TILE), out_index(lambda tile: tile >= F32_FROM_TILE)),
                   pl.BlockSpec((ts, PROJ_TILE), out_index(lambda tile: tile <= BF16_TO_TILE)),
                   ANY, ANY],
        scratch_shapes=[pltpu.VMEM((per, D_MODEL, PROJ_TILE), BF16),
                        pltpu.SemaphoreType.DMA((6, n_piece)), pltpu.SemaphoreType.DMA((6, n_piece)),
                        pltpu.SemaphoreType.DMA((per, W_LOAD_PIECES))],
    )
    return pl.pallas_call(
        body,
        name="gather_inproj",
        grid_spec=grid_spec,
        out_shape=[jax.ShapeDtypeStruct((s_len, IN_WIDTH), F32),
                   jax.ShapeDtypeStruct((s_len, IN_WIDTH), BF16),
                   jax.ShapeDtypeStruct((D_MODEL, IN_WIDTH), BF16),
                   jax.ShapeDtypeStruct((3, N_CHIPS, ROW_SHARD, D_MODEL), BF16)],
        compiler_params=_cparams(("arbitrary", "arbitrary", "arbitrary")),
    )(idx, h, w_in_b, w_sq_b)


def _place_own(idx, w_in_b, w_sq_b, w_all, wsq):
    n = 4
    r_in, r_sq = D_MODEL // n, ROW_SHARD // n

    def body(idx_ref, win_ref, wsq_ref, w_all_in, wsq_in, w_all_out, wsq_out):
        w_all_out[...] = win_ref[...]
        wsq_out[:, 0] = wsq_ref[...]

    grid_spec = pltpu.PrefetchScalarGridSpec(
        num_scalar_prefetch=1,
        grid=(n,),
        in_specs=[pl.BlockSpec((r_in, W_IN_SHARD), lambda r, idx: (r, 0)),
                  pl.BlockSpec((3, r_sq, D_MODEL), lambda r, idx: (0, r, 0)), ANY, ANY],
        out_specs=[pl.BlockSpec((r_in, W_IN_SHARD), lambda r, idx: (r, idx[0])),
                   pl.BlockSpec((3, 1, r_sq, D_MODEL), lambda r, idx: (0, idx[0], r, 0))],
    )
    return pl.pallas_call(
        body,
        name="place_own",
        grid_spec=grid_spec,
        out_shape=[jax.ShapeDtypeStruct(w_all.shape, BF16), jax.ShapeDtypeStruct(wsq.shape, BF16)],
        input_output_aliases={3: 0, 4: 1},
        compiler_params=_cparams(("arbitrary",)),
    )(idx, w_in_b, w_sq_b, w_all, wsq)


def _swap_halves(g_in, g_sq):
    n_in = 16
    n_piece = n_in + 3 * N_CHIPS
    rows = HALF_IN // n_in

    def body(gin_ref, gsq_ref, got_in, got_sq, send_sems, recv_sems):
        x, y, c, _ = _position()
        sibling = (x, y, 1 - c)

        def src_piece(p):
            if p < n_in:
                return gin_ref.at[pl.ds((1 - c) * HALF_IN + p * rows, rows), :]
            a, chip = divmod(p - n_in, N_CHIPS)
            return gsq_ref.at[a, chip, pl.ds((1 - c) * HALF_SQ, HALF_SQ), :]

        def dst_piece(p):
            if p < n_in:
                return got_in.at[pl.ds(p * rows, rows), :]
            a, chip = divmod(p - n_in, N_CHIPS)
            return got_sq.at[a, chip]

        out = [_remote(src_piece(p), dst_piece(p), send_sems.at[p], recv_sems.at[p], sibling) for p in range(n_piece)]
        for cp in out:
            cp.start()
        for cp in out:
            cp.wait()

    return pl.pallas_call(
        body,
        name="swap_halves",
        in_specs=[ANY, ANY],
        out_specs=[ANY, ANY],
        out_shape=[jax.ShapeDtypeStruct((HALF_IN, IN_WIDTH), F32),
                   jax.ShapeDtypeStruct((3, N_CHIPS, HALF_SQ, D_MODEL), F32)],
        scratch_shapes=[pltpu.SemaphoreType.DMA((n_piece,))] * 2,
    )(g_in, g_sq)


def _join_halves(r_in, r_sq):
    n_in = 16
    n_piece = n_in + 3
    rows = HALF_IN // n_in

    def body(in_alias, sq_alias, full_in, full_sq, send_sems, recv_sems):
        del in_alias, sq_alias
        x, y, c, _ = _position()
        sibling = (x, y, 1 - c)

        def piece(p, core):
            if p < n_in:
                return full_in.at[pl.ds(core * HALF_IN + p * rows, rows), :]
            return full_sq.at[p - n_in, pl.ds(core * HALF_SQ, HALF_SQ), :]

        out = [_remote(piece(p, c), piece(p, c), send_sems.at[p], recv_sems.at[p], sibling) for p in range(n_piece)]
        for cp in out:
            cp.start()
        for p in range(n_piece):
            _remote(piece(p, 1 - c), piece(p, 1 - c), send_sems.at[p], recv_sems.at[p], sibling).wait_recv()
        for cp in out:
            cp.wait_send()

    return pl.pallas_call(
        body,
        name="join_halves",
        in_specs=[ANY, ANY],
        out_specs=[ANY, ANY],
        out_shape=[jax.ShapeDtypeStruct((D_MODEL, W_IN_SHARD), F32),
                   jax.ShapeDtypeStruct((3, ROW_SHARD, D_MODEL), F32)],
        input_output_aliases={0: 0, 1: 1},
        scratch_shapes=[pltpu.SemaphoreType.DMA((n_piece,)), pltpu.SemaphoreType.DMA((n_piece,))],
    )(r_in, r_sq)


SMALL_ROWS = 56
N_DEV = 8


def _sum_small(part):
    def body(part_ref, out_ref, slots, send_sems, recv_sems):
        x, y, c, _ = _position()
        me = 4 * x + 2 * y + c
        slots[me] = part_ref[...]
        out = []
        for r in range(1, N_DEV):
            rx, ry, rc = (r >> 2) & 1, (r >> 1) & 1, r & 1
            to = (1 - x if rx else x, 1 - y if ry else y, 1 - c if rc else c)
            out.append(_remote(part_ref, slots.at[me], send_sems.at[r - 1], recv_sems.at[r - 1], to))
        for cp in out:
            cp.start()
        for r in range(1, N_DEV):
            _remote(part_ref, slots.at[me ^ r], send_sems.at[r - 1], recv_sems.at[r - 1], (x, y, c)).wait_recv()
        for cp in out:
            cp.wait_send()
        total = slots[0]
        for d in range(1, N_DEV):
            total = total + slots[d]
        out_ref[...] = total

    vmem = pl.BlockSpec(memory_space=pltpu.VMEM)
    return pl.pallas_call(
        body,
        name="sum_small",
        in_specs=[vmem],
        out_specs=vmem,
        out_shape=jax.ShapeDtypeStruct((SMALL_ROWS, HEAD_DIM), F32),
        scratch_shapes=[pltpu.VMEM((N_DEV, SMALL_ROWS, HEAD_DIM), F32),
                        pltpu.SemaphoreType.DMA((N_DEV - 1,)), pltpu.SemaphoreType.DMA((N_DEV - 1,))],
    )(part)


def _prefetch_call(body, name, idx, grid, in_specs, out_specs, out_shape, args):
    grid_spec = pltpu.PrefetchScalarGridSpec(num_scalar_prefetch=1, grid=grid, in_specs=in_specs, out_specs=out_specs)
    return pl.pallas_call(body, name=name, grid_spec=grid_spec, out_shape=out_shape,
                          compiler_params=_cparams(("arbitrary",) * len(grid)))(idx, *args)


def _sum_a_in(idx, g_in, got_in):
    tr = 128
    nr = HALF_IN // tr

    def body(idx_ref, a_ref, b_ref, o_ref):
        o_ref[0] = (a_ref[...] + b_ref[...]).astype(WIRE)

    return _prefetch_call(
        body, "sum_a_in", idx, (N_CHIPS, nr),
        [pl.BlockSpec((tr, W_IN_SHARD), lambda j, r, idx: (idx[1] * nr + r, j)),
         pl.BlockSpec((tr, W_IN_SHARD), lambda j, r, idx: (r, j))],
        pl.BlockSpec((1, tr, W_IN_SHARD), lambda j, r, idx: (j, r, 0)),
        jax.ShapeDtypeStruct((N_CHIPS, HALF_IN, W_IN_SHARD), WIRE), (g_in, got_in))


def _sum_a_sq(idx, g_sq, got_sq):
    blk = (1, 1, HALF_SQ, D_MODEL)

    def body(idx_ref, a_ref, b_ref, o_ref):
        o_ref[...] = (a_ref[...] + b_ref[...]).astype(WIRE)

    return _prefetch_call(
        body, "sum_a_sq", idx, (3, N_CHIPS),
        [pl.BlockSpec(blk, lambda a, j, idx: (a, j, idx[1], 0)), pl.BlockSpec(blk, lambda a, j, idx: (a, j, 0, 0))],
        pl.BlockSpec(blk, lambda a, j, idx: (a, j, 0, 0)),
        jax.ShapeDtypeStruct((3, N_CHIPS, HALF_SQ, D_MODEL), WIRE), (g_sq, got_sq))


def _sum_b_in(idx, s_in, got_in):
    tr = 128
    nr = HALF_IN // tr

    def body(idx_ref, a_ref, b_ref, o_ref):
        o_ref[...] = ((a_ref[0].astype(F32) + b_ref[0].astype(F32)) + b_ref[1].astype(F32)) + b_ref[2].astype(F32)

    return _prefetch_call(
        body, "sum_b_in", idx, (nr,),
        [pl.BlockSpec((1, tr, W_IN_SHARD), lambda r, idx: (idx[0], r, 0)),
         pl.BlockSpec((3, tr, W_IN_SHARD), lambda r, idx: (0, r, 0))],
        pl.BlockSpec((tr, W_IN_SHARD), lambda r, idx: (idx[1] * nr + r, 0)),
        jax.ShapeDtypeStruct((D_MODEL, W_IN_SHARD), F32), (s_in, got_in))


def _sum_b_sq(idx, s_sq, got_sq):
    def body(idx_ref, a_ref, b_ref, o_ref):
        o_ref[0] = ((a_ref[0, 0].astype(F32) + b_ref[0, 0].astype(F32)) + b_ref[1, 0].astype(F32)) + b_ref[2, 0].astype(F32)

    return _prefetch_call(
        body, "sum_b_sq", idx, (3,),
        [pl.BlockSpec((1, 1, HALF_SQ, D_MODEL), lambda a, idx: (a, idx[0], 0, 0)),
         pl.BlockSpec((3, 1, HALF_SQ, D_MODEL), lambda a, idx: (0, a, 0, 0))],
        pl.BlockSpec((1, HALF_SQ, D_MODEL), lambda a, idx: (a, idx[1], 0)),
        jax.ShapeDtypeStruct((3, ROW_SHARD, D_MODEL), F32), (s_sq, got_sq))


def _adamw_math(w, g, m, v):
    m = ADAM_B1 * m + (1.0 - ADAM_B1) * g
    v = ADAM_B2 * v + (1.0 - ADAM_B2) * (g * g)
    m_hat = m / (1.0 - ADAM_B1 ** ADAM_STEP)
    v_hat = v / (1.0 - ADAM_B2 ** ADAM_STEP)
    delta = -ADAM_LR * (m_hat / (jnp.sqrt(v_hat) + ADAM_EPS) + ADAM_WD * w)
    return delta, m, v


def _adamw(w, g, m, v, name):
    rows, cols = w.shape
    tr = min(128, rows)

    def body(w_ref, g_ref, m_ref, v_ref, d_ref, nm_ref, nv_ref):
        d_ref[...], nm_ref[...], nv_ref[...] = _adamw_math(w_ref[...], g_ref[...], m_ref[...], v_ref[...])

    spec = pl.BlockSpec((tr, cols), lambda r: (r, 0))
    return pl.pallas_call(
        body,
        name=name,
        grid=(rows // tr,),
        in_specs=[spec] * 4,
        out_specs=[spec] * 3,
        out_shape=[jax.ShapeDtypeStruct((rows, cols), F32)] * 3,
        compiler_params=_cparams(("arbitrary",)),
    )(w, g, m, v)


def _adamw_small(sums, w, m, v):
    def body(s_ref, w_ref, m_ref, v_ref, loss_ref, g_ref, d_ref, nm_ref, nv_ref):
        s = s_ref[...]
        w = w_ref[...]
        loss_ref[...] = s[0:1, 0:1]
        l0, l1 = w[24:32], w[32:40]
        mx = jnp.maximum(l0, l1)
        e0, e1 = jnp.exp(l0 - mx), jnp.exp(l1 - mx)
        p0, p1 = e0 / (e0 + e1), e1 / (e0 + e1)
        d_lb = s[32:40]
        g = jnp.concatenate([s[8:16], s[16:32], d_lb * p0 * (1.0 - p0), -d_lb * p0 * p1, s[40:48], s[48:56]], axis=0)
        g_ref[...] = g
        d_ref[...], nm_ref[...], nv_ref[...] = _adamw_math(w, g, m_ref[...], v_ref[...])

    packed = jax.ShapeDtypeStruct((SMALL_ROWS, HEAD_DIM), F32)
    return pl.pallas_call(
        body,
        name="adamw_small",
        out_shape=[jax.ShapeDtypeStruct((1, 1), F32), packed, packed, packed, packed],
    )(sums, w, m, v)


def _pack_small(ng, bg, lbl, hgn, fg):
    return jnp.concatenate([a.reshape(-1, HEAD_DIM) for a in (ng, bg, lbl, hgn, fg)], axis=0)


def _unpack_small(p):
    return (p[0:8].reshape(1, D_MODEL), p[8:24].reshape(1, 2 * D_MODEL), p[24:40].reshape(2, HEADS, HEAD_DIM),
            p[40:48].reshape(1, HEADS, HEAD_DIM), p[48:56].reshape(D_MODEL))


def kernel(x, norm_g, w_in, b_gate, lb_logits, hg_norm_g, w_sb_proj, w_hg_proj, w_out, final_norm_g, loss_target, m_norm_g, m_w_in, m_b_gate, m_lb_logits, m_hg_norm_g, m_w_sb_proj, m_w_hg_proj, m_w_out, m_final_norm_g, v_norm_g, v_w_in, v_b_gate, v_lb_logits, v_hg_norm_g, v_w_sb_proj, v_w_hg_proj, v_w_out, v_final_norm_g):
    s_len = x.shape[1]
    w_sq = jnp.stack([w_sb_proj[0], w_hg_proj[0], w_out[0]])
    idx = jnp.stack([2 * lax.axis_index("x") + lax.axis_index("y"), lax.axis_index("c")]).astype(jnp.int32)
    w_in_b, w_sq_b = w_in[0].astype(BF16), w_sq.astype(BF16)
    h, h_t = _prenorm(x[0], norm_g)
    proj, qkv, w_all, wsq = _gather_inproj(idx, h, w_in_b, w_sq_b)
    w_all, wsq = _place_own(idx, w_in_b, w_sq_b, w_all, wsq)
    wsq = wsq.reshape(3, D_MODEL, D_MODEL)

    (g_in, g_sb, g_hg, g_out, segs, dout, loss, d_bg, d_lb, d_hgn, d_fg) = _local_grads(
        x[0], loss_target[0], proj, h_t, qkv, b_gate, lb_logits.reshape(2, D_MODEL), hg_norm_g.reshape(1, D_MODEL),
        final_norm_g.reshape(1, D_MODEL), wsq[0], wsq[1], wsq[2])

    g_sq = jnp.stack([g_sb, g_hg, g_out]).reshape(3, N_CHIPS, ROW_SHARD, D_MODEL)
    got_in, got_sq = _swap_halves(g_in, g_sq)
    s_in, s_sq = _sum_a_in(idx, g_in, got_in), _sum_a_sq(idx, g_sq, got_sq)
    grad_x, d_ng, got_in, got_sq = _dx(segs, w_all, x[0], norm_g, dout, s_in, s_sq)
    grad_in, grad_sq = _join_halves(_sum_b_in(idx, s_in, got_in), _sum_b_sq(idx, s_sq, got_sq))

    d_in, nm_in, nv_in = _adamw(w_in[0], grad_in, m_w_in[0], v_w_in[0], "adamw_in")
    flat = lambda a, b, c: jnp.concatenate([a[0], b[0], c[0]], axis=0)
    d_sq, nm_sq, nv_sq = _adamw(flat(w_sb_proj, w_hg_proj, w_out), grad_sq.reshape(3 * ROW_SHARD, D_MODEL),
                                flat(m_w_sb_proj, m_w_hg_proj, m_w_out), flat(v_w_sb_proj, v_w_hg_proj, v_w_out),
                                "adamw_sq")

    pad = jnp.zeros((8, HEAD_DIM), F32).at[0, 0].set(loss[0, 0])
    part = jnp.concatenate([pad] + [a.reshape(-1, HEAD_DIM) for a in (d_ng, d_bg, d_lb, d_hgn, d_fg)], axis=0)
    sums = _sum_small(part)
    loss_out, g_sm, d_sm, nm_sm, nv_sm = _adamw_small(
        sums, _pack_small(norm_g, b_gate, lb_logits, hg_norm_g, final_norm_g),
        _pack_small(m_norm_g, m_b_gate, m_lb_logits, m_hg_norm_g, m_final_norm_g),
        _pack_small(v_norm_g, v_b_gate, v_lb_logits, v_hg_norm_g, v_final_norm_g))

    def big(t_in, t_sq):
        sq = t_sq.reshape(3, 1, ROW_SHARD, D_MODEL)
        return t_in[None], sq[0], sq[1], sq[2]

    def order(small, in_, sb, hg, out):
        ng, bg, lbl, hgn, fg = small
        return [ng, in_, bg, lbl, hgn, sb, hg, out, fg]

    outs = [loss_out[0, 0], grad_x]
    for small, (t_in, t_sq) in ((g_sm, (grad_in, grad_sq)), (d_sm, (d_in, d_sq)), (nm_sm, (nm_in, nm_sq)), (nv_sm, (nv_in, nv_sq))):
        outs += order(_unpack_small(small), *big(t_in, t_sq))
    return tuple(outs)
```
